```python
import math
import jax, jax.numpy as jnp
from jax import lax
import numpy as np

D_MODEL = 2048
BATCH = 8
SEQ = 8192
DEPTH = 1

CHUNK = 64
Q_BLOCK = 128
FOX_HEADS = 8
FOX_HEAD_DIM = 128
FOX_WIDTH = FOX_HEADS * FOX_HEAD_DIM
GDN_HEADS = 8
GDN_HEAD_DIM = 128
GDN_WIDTH = GDN_HEADS * GDN_HEAD_DIM
MIX_WIDTH = FOX_WIDTH + GDN_WIDTH
CONV_WIDTH = 4
EPS = 1e-6
IN_SIZES = (FOX_WIDTH, FOX_WIDTH, FOX_WIDTH, FOX_WIDTH, FOX_HEADS,
            GDN_WIDTH, GDN_WIDTH, GDN_WIDTH, GDN_WIDTH, GDN_HEADS, GDN_HEADS)
IN_WIDTH = 4 * FOX_WIDTH + FOX_HEADS + 4 * GDN_WIDTH + 2 * GDN_HEADS

kernel_name = "hybrid_fox_gdn_adaln_block"


def rmsnorm(x, g):
    xf = x.astype(jnp.float32)
    y = xf * lax.rsqrt(jnp.mean(xf * xf, axis=-1, keepdims=True) + EPS)
    return (y * g.astype(jnp.float32)).astype(x.dtype)


def l2norm(x):
    xf = x.astype(jnp.float32)
    return xf * lax.rsqrt(jnp.sum(xf * xf, axis=-1, keepdims=True) + EPS)


def to_heads(t, n_heads):
    b, s, w = t.shape
    return t.reshape(b, s, n_heads, w // n_heads).transpose(0, 2, 1, 3)


def from_heads(t):
    b, n, s, d = t.shape
    return t.transpose(0, 2, 1, 3).reshape(b, s, n * d)


def split_cols(p):
    idx = np.cumsum(np.array(IN_SIZES))[:-1].tolist()
    return jnp.split(p, idx, axis=-1)


def forgetting_attention(q, k, v, f_logit, b_f, qn_g, kn_g):
    b, s, _ = q.shape
    qh = rmsnorm(to_heads(q, FOX_HEADS), qn_g).astype(jnp.float32)
    kh = rmsnorm(to_heads(k, FOX_HEADS), kn_g).astype(jnp.float32)
    vh = to_heads(v, FOX_HEADS).astype(jnp.float32)
    log_f = jax.nn.log_sigmoid(f_logit.astype(jnp.float32) + b_f.astype(jnp.float32))
    F = jnp.cumsum(log_f, axis=1).transpose(0, 2, 1)
    nq = s // Q_BLOCK
    qb = qh.reshape(b, FOX_HEADS, nq, Q_BLOCK, FOX_HEAD_DIM).transpose(2, 0, 1, 3, 4)
    Fb = F.reshape(b, FOX_HEADS, nq, Q_BLOCK).transpose(2, 0, 1, 3)
    tb = jnp.arange(s, dtype=jnp.int32).reshape(nq, Q_BLOCK)
    s_pos = jnp.arange(s, dtype=jnp.int32)
    scale = FOX_HEAD_DIM ** -0.5

    def block(args):
        q_i, F_i, t_i = args
        logits = (jnp.einsum('bhqd,bhkd->bhqk', q_i, kh) * scale
                  + (F_i[..., :, None] - F[..., None, :]))
        mask = s_pos[None, :] <= t_i[:, None]
        logits = jnp.where(mask, logits, -jnp.inf)
        p = jax.nn.softmax(logits, axis=-1)
        return jnp.einsum('bhqk,bhkd->bhqd', p, vh)

    o = lax.map(block, (qb, Fb, tb))
    return o.transpose(1, 0, 3, 2, 4).reshape(b, s, FOX_WIDTH)


def causal_depthwise_conv(x, w):
    c = x.shape[-1]
    return lax.conv_general_dilated(
        x, w[:, None, :], window_strides=(1,), padding=[(CONV_WIDTH - 1, 0)],
        dimension_numbers=('NWC', 'WIO', 'NWC'), feature_group_count=c)


def chunk_gated_delta_rule(q, k, v, g, beta):
    b, h, s, dk = q.shape
    dv = v.shape[-1]
    n = s // CHUNK
    q = q.reshape(b, h, n, CHUNK, dk)
    k = k.reshape(b, h, n, CHUNK, dk)
    v = v.reshape(b, h, n, CHUNK, dv)
    g = g.reshape(b, h, n, CHUNK)
    beta = beta.reshape(b, h, n, CHUNK)
    gc = jnp.cumsum(g, axis=-1)
    idx = jnp.arange(CHUNK)
    lower = idx[:, None] >= idx[None, :]
    strict = idx[:, None] > idx[None, :]
    decay = jnp.exp(jnp.where(lower, gc[..., :, None] - gc[..., None, :], -jnp.inf))
    kb = k * beta[..., None]
    vb = v * beta[..., None]
    M = jnp.where(strict, jnp.einsum('bhncd,bhnsd->bhncs', kb, k) * decay, 0.0)
    A = M + jnp.eye(CHUNK, dtype=jnp.float32)
    rhs = jnp.concatenate([vb, kb * jnp.exp(gc)[..., None]], axis=-1)
    sol = lax.linalg.triangular_solve(A, rhs, left_side=True, lower=True, unit_diagonal=True)
    u, w = sol[..., :dv], sol[..., dv:]
    attn = jnp.where(lower, jnp.einsum('bhncd,bhnsd->bhncs', q, k) * decay, 0.0)

    def step(state, xs):
        q_i, k_i, u_i, w_i, gc_i, attn_i = xs
        v_new = u_i - jnp.einsum('bhcd,bhde->bhce', w_i, state)
        o_i = (jnp.einsum('bhcd,bhde->bhce', q_i * jnp.exp(gc_i)[..., None], state)
               + jnp.einsum('bhcs,bhse->bhce', attn_i, v_new))
        g_last = gc_i[..., -1]
        k_dec = k_i * jnp.exp(g_last[..., None] - gc_i)[..., None]
        new_state = state * jnp.exp(g_last)[..., None, None] + jnp.einsum('bhcd,bhce->bhde', k_dec, v_new)
        return new_state, o_i

    mv = lambda t: jnp.moveaxis(t, 2, 0)
    state0 = jnp.zeros((b, h, dk, dv), jnp.float32)
    _, o = lax.scan(step, state0, (mv(q), mv(k), mv(u), mv(w), mv(gc), mv(attn)))
    return o.transpose(1, 2, 0, 3, 4).reshape(b, h, s, dv)


def gated_deltanet(q, k, v, a, bt, conv_w, A_log, dt_bias, norm_g):
    qkv = jnp.concatenate([q, k, v], axis=-1).astype(jnp.float32)
    qkv = jax.nn.silu(causal_depthwise_conv(qkv, conv_w.astype(jnp.float32)))
    q, k, v = jnp.split(qkv, 3, axis=-1)
    qh = l2norm(to_heads(q, GDN_HEADS)) * (GDN_HEAD_DIM ** -0.5)
    kh = l2norm(to_heads(k, GDN_HEADS))
    vh = to_heads(v, GDN_HEADS)
    beta = jax.nn.sigmoid(bt.astype(jnp.float32)).transpose(0, 2, 1)
    g = (-jnp.exp(A_log.astype(jnp.float32))
         * jax.nn.softplus(a.astype(jnp.float32) + dt_bias.astype(jnp.float32))).transpose(0, 2, 1)
    o = chunk_gated_delta_rule(qh, kh, vh, g, beta)
    o = rmsnorm(o, norm_g)
    return from_heads(o)


def _fwd_setup_inputs(seed: int = 0) -> dict:
    key = jax.random.key(seed)
    ks = jax.random.split(key, 16)
    f32 = jnp.float32
    x = jax.random.normal(ks[0], (BATCH, SEQ, D_MODEL), f32)
    c = jax.random.normal(ks[1], (BATCH, D_MODEL), f32)
    norm_g = 1.0 + 0.02 * jax.random.normal(ks[2], (DEPTH, D_MODEL), f32)
    w_ada = 0.5 * D_MODEL ** -0.5 * jax.random.normal(ks[3], (DEPTH, D_MODEL, 3 * D_MODEL), f32)
    b_ada = 0.01 * jax.random.normal(ks[4], (DEPTH, 3 * D_MODEL), f32)
    w_in = D_MODEL ** -0.5 * jax.random.normal(ks[5], (DEPTH, D_MODEL, IN_WIDTH), f32)
    b_fgate = jax.random.uniform(ks[6], (DEPTH, FOX_HEADS), f32, 1.0, 4.0)
    fox_qn_g = 1.0 + 0.02 * jax.random.normal(ks[7], (DEPTH, FOX_HEAD_DIM), f32)
    fox_kn_g = 1.0 + 0.02 * jax.random.normal(ks[8], (DEPTH, FOX_HEAD_DIM), f32)
    gdn_conv_w = CONV_WIDTH ** -0.5 * jax.random.normal(ks[9], (DEPTH, CONV_WIDTH, 3 * GDN_WIDTH), f32)
    gdn_A_log = jnp.log(jax.random.uniform(ks[10], (DEPTH, GDN_HEADS), f32, 1.0, 16.0))
    dt = jnp.exp(jax.random.uniform(ks[11], (DEPTH, GDN_HEADS), f32, math.log(1e-3), math.log(1e-1)))
    gdn_dt_bias = dt + jnp.log(-jnp.expm1(-dt))
    gdn_norm_g = 1.0 + 0.02 * jax.random.normal(ks[12], (DEPTH, GDN_HEAD_DIM), f32)
    w_out = MIX_WIDTH ** -0.5 * jax.random.normal(ks[13], (DEPTH, MIX_WIDTH, D_MODEL), f32)
    final_g = 1.0 + 0.02 * jax.random.normal(ks[14], (D_MODEL,), f32)
    return {"x": x, "c": c, "norm_g": norm_g, "w_ada": w_ada, "b_ada": b_ada,
            "w_in": w_in, "b_fgate": b_fgate, "fox_qn_g": fox_qn_g, "fox_kn_g": fox_kn_g,
            "gdn_conv_w": gdn_conv_w, "gdn_A_log": gdn_A_log, "gdn_dt_bias": gdn_dt_bias,
            "gdn_norm_g": gdn_norm_g, "w_out": w_out, "final_g": final_g}


def _fwd_reference(x, c, norm_g, w_ada, b_ada, w_in, b_fgate, fox_qn_g, fox_kn_g,
              gdn_conv_w, gdn_A_log, gdn_dt_bias, gdn_norm_g, w_out, final_g):
    c_act = jax.nn.silu(c)
    for l in range(DEPTH):
        mod = c_act @ w_ada[l] + b_ada[l]
        shift, scale, gate = jnp.split(mod, 3, axis=-1)
        h = rmsnorm(x, norm_g[l]) * (1.0 + scale[:, None, :]) + shift[:, None, :]
        p = h @ w_in[l]
        fq, fk, fv, fz, ff, gq, gk, gv, gz, ga, gb = split_cols(p)
        fox_o = forgetting_attention(fq, fk, fv, ff, b_fgate[l], fox_qn_g[l], fox_kn_g[l])
        fox_o = fox_o * jax.nn.silu(fz.astype(jnp.float32))
        gdn_o = gated_deltanet(gq, gk, gv, ga, gb, gdn_conv_w[l], gdn_A_log[l],
                               gdn_dt_bias[l], gdn_norm_g[l])
        gdn_o = gdn_o * jax.nn.silu(gz.astype(jnp.float32))
        mixed = jnp.concatenate([fox_o, gdn_o], axis=-1).astype(x.dtype)
        x = x + gate[:, None, :] * (mixed @ w_out[l])
    return rmsnorm(x, final_g)


import jax as _jax
import jax.numpy as _jnp

TWIN_FORMAT = 'train_step'
FWD_PARAMS = ['x', 'c', 'norm_g', 'w_ada', 'b_ada', 'w_in', 'b_fgate', 'fox_qn_g', 'fox_kn_g', 'gdn_conv_w', 'gdn_A_log', 'gdn_dt_bias', 'gdn_norm_g', 'w_out', 'final_g']
TWIN_WEIGHTS = ['norm_g', 'w_ada', 'b_ada', 'w_in', 'b_fgate', 'fox_qn_g', 'fox_kn_g', 'gdn_conv_w', 'gdn_A_log', 'gdn_dt_bias', 'gdn_norm_g', 'w_out', 'final_g']
TWIN_DIFF_INPUT = 'x'
TWIN_INPUTS = ['x', 'c', 'norm_g', 'w_ada', 'b_ada', 'w_in', 'b_fgate', 'fox_qn_g', 'fox_kn_g', 'gdn_conv_w', 'gdn_A_log', 'gdn_dt_bias', 'gdn_norm_g', 'w_out', 'final_g', 'loss_target', 'm_norm_g', 'm_w_ada', 'm_b_ada', 'm_w_in', 'm_b_fgate', 'm_fox_qn_g', 'm_fox_kn_g', 'm_gdn_conv_w', 'm_gdn_A_log', 'm_gdn_dt_bias', 'm_gdn_norm_g', 'm_w_out', 'm_final_g', 'v_norm_g', 'v_w_ada', 'v_b_ada', 'v_w_in', 'v_b_fgate', 'v_fox_qn_g', 'v_fox_kn_g', 'v_gdn_conv_w', 'v_gdn_A_log', 'v_gdn_dt_bias', 'v_gdn_norm_g', 'v_w_out', 'v_final_g']
TWIN_OUTPUTS = ['loss', 'grad_x', 'grad_norm_g', 'grad_w_ada', 'grad_b_ada', 'grad_w_in', 'grad_b_fgate', 'grad_fox_qn_g', 'grad_fox_kn_g', 'grad_gdn_conv_w', 'grad_gdn_A_log', 'grad_gdn_dt_bias', 'grad_gdn_norm_g', 'grad_w_out', 'grad_final_g', 'delta_norm_g', 'delta_w_ada', 'delta_b_ada', 'delta_w_in', 'delta_b_fgate', 'delta_fox_qn_g', 'delta_fox_kn_g', 'delta_gdn_conv_w', 'delta_gdn_A_log', 'delta_gdn_dt_bias', 'delta_gdn_norm_g', 'delta_w_out', 'delta_final_g', 'new_m_norm_g', 'new_m_w_ada', 'new_m_b_ada', 'new_m_w_in', 'new_m_b_fgate', 'new_m_fox_qn_g', 'new_m_fox_kn_g', 'new_m_gdn_conv_w', 'new_m_gdn_A_log', 'new_m_gdn_dt_bias', 'new_m_gdn_norm_g', 'new_m_w_out', 'new_m_final_g', 'new_v_norm_g', 'new_v_w_ada', 'new_v_b_ada', 'new_v_w_in', 'new_v_b_fgate', 'new_v_fox_qn_g', 'new_v_fox_kn_g', 'new_v_gdn_conv_w', 'new_v_gdn_A_log', 'new_v_gdn_dt_bias', 'new_v_gdn_norm_g', 'new_v_w_out', 'new_v_final_g']
TWIN_LEAF_KINDS = {'loss': 'loss', 'grad_x': 'grad_x', 'grad_norm_g': 'grad_w', 'grad_w_ada': 'grad_w', 'grad_b_ada': 'grad_w', 'grad_w_in': 'grad_w', 'grad_b_fgate': 'grad_w', 'grad_fox_qn_g': 'grad_w', 'grad_fox_kn_g': 'grad_w', 'grad_gdn_conv_w': 'grad_w', 'grad_gdn_A_log': 'grad_w', 'grad_gdn_dt_bias': 'grad_w', 'grad_gdn_norm_g': 'grad_w', 'grad_w_out': 'grad_w', 'grad_final_g': 'grad_w', 'delta_norm_g': 'delta_w', 'delta_w_ada': 'delta_w', 'delta_b_ada': 'delta_w', 'delta_w_in': 'delta_w', 'delta_b_fgate': 'delta_w', 'delta_fox_qn_g': 'delta_w', 'delta_fox_kn_g': 'delta_w', 'delta_gdn_conv_w': 'delta_w', 'delta_gdn_A_log': 'delta_w', 'delta_gdn_dt_bias': 'delta_w', 'delta_gdn_norm_g': 'delta_w', 'delta_w_out': 'delta_w', 'delta_final_g': 'delta_w', 'new_m_norm_g': 'new_m', 'new_m_w_ada': 'new_m', 'new_m_b_ada': 'new_m', 'new_m_w_in': 'new_m', 'new_m_b_fgate': 'new_m', 'new_m_fox_qn_g': 'new_m', 'new_m_fox_kn_g': 'new_m', 'new_m_gdn_conv_w': 'new_m', 'new_m_gdn_A_log': 'new_m', 'new_m_gdn_dt_bias': 'new_m', 'new_m_gdn_norm_g': 'new_m', 'new_m_w_out': 'new_m', 'new_m_final_g': 'new_m', 'new_v_norm_g': 'new_v', 'new_v_w_ada': 'new_v', 'new_v_b_ada': 'new_v', 'new_v_w_in': 'new_v', 'new_v_b_fgate': 'new_v', 'new_v_fox_qn_g': 'new_v', 'new_v_fox_kn_g': 'new_v', 'new_v_gdn_conv_w': 'new_v', 'new_v_gdn_A_log': 'new_v', 'new_v_gdn_dt_bias': 'new_v', 'new_v_gdn_norm_g': 'new_v', 'new_v_w_out': 'new_v', 'new_v_final_g': 'new_v'}


def _forward(args):
    return _fwd_reference(*[args[k] for k in FWD_PARAMS])


def _output_shape():
    def fwd():
        inp = _fwd_setup_inputs(0)
        return _fwd_reference(*[inp[k] for k in FWD_PARAMS])
    out = _jax.eval_shape(fwd)
    return out.shape, out.dtype

N_MICROBATCH = 1
ADAM_LR = 0.001
ADAM_B1 = 0.9
ADAM_B2 = 0.999
ADAM_EPS = 1e-08
ADAM_WD = 0.01
ADAM_STEP = 10
PER_EXAMPLE_BATCH_AXIS = {'x': 0, 'c': 0, 'loss_target': 0}
SHARED_INPUTS = []
_WEIGHT_DTYPES = {'norm_g': _jnp.float32, 'w_ada': _jnp.float32, 'b_ada': _jnp.float32, 'w_in': _jnp.float32, 'b_fgate': _jnp.float32, 'fox_qn_g': _jnp.float32, 'fox_kn_g': _jnp.float32, 'gdn_conv_w': _jnp.float32, 'gdn_A_log': _jnp.float32, 'gdn_dt_bias': _jnp.float32, 'gdn_norm_g': _jnp.float32, 'w_out': _jnp.float32, 'final_g': _jnp.float32}
MOMENT_SCALE = {'norm_g': 2.942522e-02, 'w_ada': 2.759204e-02, 'b_ada': 4.690230e-02, 'w_in': 1.573817e-02, 'b_fgate': 7.036532e-02, 'fox_qn_g': 1.898502e-02, 'fox_kn_g': 1.873995e-02, 'gdn_conv_w': 1.834932e-02, 'gdn_A_log': 1.324948e-01, 'gdn_dt_bias': 1.304453e-01, 'gdn_norm_g': 8.818058e-02, 'w_out': 1.892702e-02, 'final_g': 3.194716e+01}


def _to_microbatches(a, axis):
    t = _jnp.moveaxis(a, axis, 0)
    t = t.reshape((N_MICROBATCH, t.shape[0] // N_MICROBATCH) + t.shape[1:])
    return _jnp.moveaxis(t, 1, axis + 1)


def setup_inputs(seed: int = 0) -> dict:
    inp = _fwd_setup_inputs(seed)
    key = _jax.random.fold_in(_jax.random.key(seed), 7919)
    shape, _ = _output_shape()
    out = dict(inp)
    out["loss_target"] = _jax.random.normal(_jax.random.fold_in(key, 0), shape, _jnp.float32)
    for i, name in enumerate(TWIN_WEIGHTS):
        w = inp[name].astype(_jnp.float32)
        if MOMENT_SCALE is None:
            s = _jnp.sqrt(_jnp.mean(_jnp.square(w)) + 1e-30)
        else:
            s = MOMENT_SCALE[name]
        km, kv = _jax.random.split(_jax.random.fold_in(key, i + 1))
        out[name] = w
        out["m_" + name] = s * _jax.random.normal(km, w.shape, _jnp.float32)
        out["v_" + name] = (s * s) * _jax.random.uniform(kv, w.shape, _jnp.float32, 0.5, 1.5)
    if N_MICROBATCH > 1:
        for name, axis in PER_EXAMPLE_BATCH_AXIS.items():
            out[name] = _to_microbatches(out[name], axis)
    return {'x': out['x'], 'c': out['c'], 'norm_g': out['norm_g'], 'w_ada': out['w_ada'], 'b_ada': out['b_ada'], 'w_in': out['w_in'], 'b_fgate': out['b_fgate'], 'fox_qn_g': out['fox_qn_g'], 'fox_kn_g': out['fox_kn_g'], 'gdn_conv_w': out['gdn_conv_w'], 'gdn_A_log': out['gdn_A_log'], 'gdn_dt_bias': out['gdn_dt_bias'], 'gdn_norm_g': out['gdn_norm_g'], 'w_out': out['w_out'], 'final_g': out['final_g'], 'loss_target': out['loss_target'], 'm_norm_g': out['m_norm_g'], 'm_w_ada': out['m_w_ada'], 'm_b_ada': out['m_b_ada'], 'm_w_in': out['m_w_in'], 'm_b_fgate': out['m_b_fgate'], 'm_fox_qn_g': out['m_fox_qn_g'], 'm_fox_kn_g': out['m_fox_kn_g'], 'm_gdn_conv_w': out['m_gdn_conv_w'], 'm_gdn_A_log': out['m_gdn_A_log'], 'm_gdn_dt_bias': out['m_gdn_dt_bias'], 'm_gdn_norm_g': out['m_gdn_norm_g'], 'm_w_out': out['m_w_out'], 'm_final_g': out['m_final_g'], 'v_norm_g': out['v_norm_g'], 'v_w_ada': out['v_w_ada'], 'v_b_ada': out['v_b_ada'], 'v_w_in': out['v_w_in'], 'v_b_fgate': out['v_b_fgate'], 'v_fox_qn_g': out['v_fox_qn_g'], 'v_fox_kn_g': out['v_fox_kn_g'], 'v_gdn_conv_w': out['v_gdn_conv_w'], 'v_gdn_A_log': out['v_gdn_A_log'], 'v_gdn_dt_bias': out['v_gdn_dt_bias'], 'v_gdn_norm_g': out['v_gdn_norm_g'], 'v_w_out': out['v_w_out'], 'v_final_g': out['v_final_g']}


def _loss(weights, diff, rest, loss_target):
    with _jax.named_scope("forward"):
        args = {**rest, TWIN_DIFF_INPUT: diff, **{k: w.astype(_WEIGHT_DTYPES[k]) for k, w in weights.items()}}
        y = _forward(args)
    with _jax.named_scope("loss_head"):
        err = _jnp.square(y.astype(_jnp.float32) - loss_target)
        return 0.5 * _jnp.sum(_jnp.mean(err, axis=-1)) if err.ndim else 0.5 * err


def _adamw(w, g, m, v):
    m = ADAM_B1 * m + (1.0 - ADAM_B1) * g
    v = ADAM_B2 * v + (1.0 - ADAM_B2) * _jnp.square(g)
    m_hat = m / (1.0 - ADAM_B1 ** ADAM_STEP)
    v_hat = v / (1.0 - ADAM_B2 ** ADAM_STEP)
    delta = -ADAM_LR * (m_hat / (_jnp.sqrt(v_hat) + ADAM_EPS) + ADAM_WD * w)
    return delta, m, v


def reference(x, c, norm_g, w_ada, b_ada, w_in, b_fgate, fox_qn_g, fox_kn_g, gdn_conv_w, gdn_A_log, gdn_dt_bias, gdn_norm_g, w_out, final_g, loss_target, m_norm_g, m_w_ada, m_b_ada, m_w_in, m_b_fgate, m_fox_qn_g, m_fox_kn_g, m_gdn_conv_w, m_gdn_A_log, m_gdn_dt_bias, m_gdn_norm_g, m_w_out, m_final_g, v_norm_g, v_w_ada, v_b_ada, v_w_in, v_b_fgate, v_fox_qn_g, v_fox_kn_g, v_gdn_conv_w, v_gdn_A_log, v_gdn_dt_bias, v_gdn_norm_g, v_w_out, v_final_g):
    given = dict(x=x, c=c, norm_g=norm_g, w_ada=w_ada, b_ada=b_ada, w_in=w_in, b_fgate=b_fgate, fox_qn_g=fox_qn_g, fox_kn_g=fox_kn_g, gdn_conv_w=gdn_conv_w, gdn_A_log=gdn_A_log, gdn_dt_bias=gdn_dt_bias, gdn_norm_g=gdn_norm_g, w_out=w_out, final_g=final_g, loss_target=loss_target, m_norm_g=m_norm_g, m_w_ada=m_w_ada, m_b_ada=m_b_ada, m_w_in=m_w_in, m_b_fgate=m_b_fgate, m_fox_qn_g=m_fox_qn_g, m_fox_kn_g=m_fox_kn_g, m_gdn_conv_w=m_gdn_conv_w, m_gdn_A_log=m_gdn_A_log, m_gdn_dt_bias=m_gdn_dt_bias, m_gdn_norm_g=m_gdn_norm_g, m_w_out=m_w_out, m_final_g=m_final_g, v_norm_g=v_norm_g, v_w_ada=v_w_ada, v_b_ada=v_b_ada, v_w_in=v_w_in, v_b_fgate=v_b_fgate, v_fox_qn_g=v_fox_qn_g, v_fox_kn_g=v_fox_kn_g, v_gdn_conv_w=v_gdn_conv_w, v_gdn_A_log=v_gdn_A_log, v_gdn_dt_bias=v_gdn_dt_bias, v_gdn_norm_g=v_gdn_norm_g, v_w_out=v_w_out, v_final_g=v_final_g)
    weights = {n: given[n] for n in TWIN_WEIGHTS}
    shared = {n: given[n] for n in SHARED_INPUTS}
    per_example = {n: given[n] for n in ['x', 'c']}
    grad_fn = _jax.value_and_grad(_loss, argnums=(0, 1))

    def one_microbatch(ex, loss_target):
        ex = dict(ex)
        diff = ex.pop(TWIN_DIFF_INPUT)
        return grad_fn(weights, diff, {**shared, **ex}, loss_target)

    if N_MICROBATCH == 1:
        loss, (grad_w, grad_x) = one_microbatch(per_example, given["loss_target"])
    else:
        def body(carry, xs):
            loss_sum, grad_sum = carry
            l_k, (gw_k, gx_k) = one_microbatch(xs[0], xs[1])
            with _jax.named_scope("update"):
                return (loss_sum + l_k, _jax.tree.map(_jnp.add, grad_sum, gw_k)), gx_k

        init = (_jnp.zeros((), _jnp.float32), _jax.tree.map(_jnp.zeros_like, weights))
        (loss, grad_w), grad_x = _jax.lax.scan(body, init, (per_example, given["loss_target"]))
    with _jax.named_scope("update"):
        delta_w, new_m, new_v = {}, {}, {}
        for n in TWIN_WEIGHTS:
            delta_w[n], new_m[n], new_v[n] = _adamw(weights[n], grad_w[n], given["m_" + n], given["v_" + n])
    return (loss, grad_x, *[grad_w[n] for n in TWIN_WEIGHTS], *[delta_w[n] for n in TWIN_WEIGHTS],
            *[new_m[n] for n in TWIN_WEIGHTS], *[new_v[n] for n in TWIN_WEIGHTS])
```

```python
import functools
import math

import jax
import jax.numpy as jnp
from jax import lax
from jax.experimental import pallas as pl
from jax.experimental.pallas import tpu as pltpu

f32 = jnp.float32
bf16 = jnp.bfloat16
HI = lax.Precision.HIGHEST

N_DEV = 8
AXES = ("x", "y", "c")
D_MODEL = 2048
HEADS = 8
HEAD_DIM = 128
WIDTH = HEADS * HEAD_DIM
CHUNK = 64
CONV_K = 4
EPS = 1e-6
QK_SCALE = HEAD_DIM ** -0.5
N_MAIN = 8 * WIDTH
N_SMALL = 128
IN_WIDTH = 8 * WIDTH + 3 * HEADS
LANES = 128
VMEM_LIMIT = 56 * 1024 * 1024

ADAM_LR, ADAM_B1, ADAM_B2, ADAM_EPS, ADAM_WD, ADAM_STEP = 0.001, 0.9, 0.999, 1e-08, 0.01, 10


def _params(*sem):
    return pltpu.CompilerParams(dimension_semantics=sem, vmem_limit_bytes=VMEM_LIMIT)


def _iota(shape, dim):
    return lax.broadcasted_iota(jnp.int32, shape, dim)


def _sigmoid(z):
    return 1.0 / (1.0 + jnp.exp(-z))


def _softplus_parts(z):
    t = jnp.log(1.0 + jnp.exp(-jnp.abs(z)))
    return jnp.minimum(z, 0.0) - t, jnp.maximum(z, 0.0) + t


def _dg(a, b, ca, cb, prec=None):
    return lax.dot_general(a, b, (((ca,), (cb,)), ((), ())), preferred_element_type=f32, precision=prec)


def _make_mm(cast, prec):
    def nn_(a, b):
        return _dg(cast(a), cast(b), 1, 0, prec)

    def nt_(a, b):
        return _dg(cast(a), cast(b), 1, 1, prec)

    def tn_(a, b):
        return _dg(cast(a), cast(b), 0, 0, prec)

    @jax.custom_vjp
    def nn(a, b):
        return nn_(a, b)

    @jax.custom_vjp
    def nt(a, b):
        return nt_(a, b)

    @jax.custom_vjp
    def tn(a, b):
        return tn_(a, b)

    nn.defvjp(lambda a, b: (nn_(a, b), (a, b)), lambda r, g: (nt_(g, r[1]), tn_(r[0], g)))
    nt.defvjp(lambda a, b: (nt_(a, b), (a, b)), lambda r, g: (nn_(g, r[1]), tn_(g, r[0])))
    tn.defvjp(lambda a, b: (tn_(a, b), (a, b)), lambda r, g: (nt_(r[1], g), nn_(r[0], g)))
    return (nn_, nt_, tn_), (nn, nt, tn)


_to_bf16 = lambda t: t.astype(bf16)
_keep = lambda t: t
_BF_PLAIN, _BF_VJP = _make_mm(_to_bf16, None)
_HI_PLAIN, _HI_VJP = _make_mm(_keep, HI)


def _inv_unit_lower_plain(m, hi_nn):
    n = -m
    eye = (_iota(m.shape, 0) == _iota(m.shape, 1)).astype(f32)
    t = eye + n
    p = n
    for _ in range(int(math.log2(CHUNK)) - 1):
        p = hi_nn(p, p)
        t = t + hi_nn(t, p)
    return t


@jax.custom_vjp
def _inv_unit_lower(m):
    return _inv_unit_lower_plain(m, _HI_PLAIN[0])


def _inv_fwd(m):
    t = _inv_unit_lower_plain(m, _HI_PLAIN[0])
    return t, t


def _inv_bwd(t, g):
    return (-_HI_PLAIN[1](_HI_PLAIN[2](t, g), t),)


_inv_unit_lower.defvjp(_inv_fwd, _inv_bwd)


def _gdn_chunk(differentiable, s0, q, k, v, g_b, beta_b):
    (bnn, bnt, btn) = _BF_VJP if differentiable else _BF_PLAIN
    (hnn, hnt, _) = _HI_VJP if differentiable else _HI_PLAIN
    inv = _inv_unit_lower if differentiable else functools.partial(_inv_unit_lower_plain, hi_nn=_HI_PLAIN[0])
    c = CHUNK
    r_i, c_i = _iota((c, c), 0), _iota((c, c), 1)
    lower, strict = r_i >= c_i, r_i > c_i
    gc_b = hnn(lower.astype(f32), g_b)
    gc_i = hnn(gc_b, (_iota((HEAD_DIM, c), 0) == 0).astype(f32))
    gc_j = hnt((_iota((c, HEAD_DIM), 1) == 0).astype(f32), gc_b)
    decay = jnp.where(lower, jnp.exp(jnp.where(lower, gc_i - gc_j, 0.0)), 0.0)
    kb = k * beta_b
    vb = v * beta_b
    m = jnp.where(strict, bnt(kb, k) * decay, 0.0)
    t = inv(m)
    eg = jnp.exp(gc_b)
    u = hnn(t, vb)
    w = hnn(t, kb * eg)
    attn = jnp.where(lower, bnt(q, k) * decay, 0.0)
    v_new = u - bnn(w, s0)
    o = bnn(q * eg, s0) + bnn(attn, v_new)
    g_last64 = hnn(jnp.ones((c, c), f32), g_b)
    g_last128 = hnn(jnp.ones((HEAD_DIM, c), f32), g_b)
    k_dec = k * jnp.exp(g_last64 - gc_b)
    s1 = s0 * jnp.exp(g_last128) + btn(k_dec, v_new)
    return o, s1


def _my_index():
    return 4 * lax.axis_index("x") + 2 * lax.axis_index("y") + lax.axis_index("c")


def _peer(d):
    x, y, c = lax.axis_index("x"), lax.axis_index("y"), lax.axis_index("c")
    px, py, pc = (x + (d >> 2)) % 2, (y + ((d >> 1) & 1)) % 2, (c + (d & 1)) % 2
    return (px, py, pc), 4 * px + 2 * py + pc


def _exchange(name, arrays, scatter):
    n = len(arrays)

    def body(*refs):
        srcs, dsts = refs[:n], refs[n:2 * n]
        send_sems, recv_sems, local_sems = refs[2 * n:]
        me = _my_index()

        def remote(k, d):
            peer, pidx = _peer(d)
            src = srcs[k].at[pidx] if scatter else srcs[k]
            return pltpu.make_async_remote_copy(
                src_ref=src, dst_ref=dsts[k].at[me], send_sem=send_sems.at[k * 7 + d - 1],
                recv_sem=recv_sems.at[k * 7 + d - 1], device_id=peer, device_id_type=pl.DeviceIdType.MESH)

        def arrival(k, d):
            peer, pidx = _peer(d)
            src = srcs[k].at[pidx] if scatter else srcs[k]
            return pltpu.make_async_remote_copy(
                src_ref=src, dst_ref=dsts[k].at[pidx], send_sem=send_sems.at[k * 7 + d - 1],
                recv_sem=recv_sems.at[k * 7 + d - 1], device_id=peer, device_id_type=pl.DeviceIdType.MESH)

        local = [pltpu.make_async_copy(srcs[k].at[me] if scatter else srcs[k], dsts[k].at[me], local_sems.at[k])
                 for k in range(n)]
        sends = [remote(k, d) for k in range(n) for d in range(1, N_DEV)]
        for cp in local + sends:
            cp.start()
        for k in range(n):
            for d in range(1, N_DEV):
                arrival(k, d).wait_recv()
        for cp in sends:
            cp.wait_send()
        for cp in local:
            cp.wait()

    if scatter:
        out_shape = [jax.ShapeDtypeStruct(a.shape, a.dtype) for a in arrays]
    else:
        out_shape = [jax.ShapeDtypeStruct((N_DEV,) + a.shape, a.dtype) for a in arrays]
    any_spec = pl.BlockSpec(memory_space=pl.ANY)
    return pl.pallas_call(
        body, name=name, out_shape=out_shape, in_specs=[any_spec] * n, out_specs=[any_spec] * n,
        scratch_shapes=[pltpu.SemaphoreType.DMA((7 * n,)), pltpu.SemaphoreType.DMA((7 * n,)),
                        pltpu.SemaphoreType.DMA((n,))],
        compiler_params=pltpu.CompilerParams(has_side_effects=True),
    )(*arrays)


def _mod_shard(c_all, w_ada, b_shard):
    def body(c_ref, w_ref, b_ref, o_ref):
        cv = c_ref[...]
        ca = cv * _sigmoid(cv)
        o_ref[...] = jnp.dot(ca.astype(bf16), w_ref[...].astype(bf16), preferred_element_type=f32) + b_ref[...]

    return pl.pallas_call(body, name="mod_shard", out_shape=jax.ShapeDtypeStruct((N_DEV, w_ada.shape[1]), f32),
                          compiler_params=_params())(c_all, w_ada, b_shard)


def _in_proj(x, norm_g, scale1p, shift, w_main, w_small):
    s_len = x.shape[0]
    tm, tn = 512, 1024

    def body(x_ref, g_ref, sc_ref, sh_ref, w_ref, ws_ref, p_ref, ps_ref, h_ref, h_sc):
        @pl.when(pl.program_id(1) == 0)
        def _():
            xb = x_ref[...]
            r = lax.rsqrt(jnp.mean(xb * xb, axis=-1, keepdims=True) + EPS)
            hb = ((xb * r * g_ref[...]) * sc_ref[...] + sh_ref[...]).astype(bf16)
            h_sc[...] = hb
            h_ref[...] = hb
            ps_ref[...] = jnp.dot(hb, ws_ref[...], preferred_element_type=f32)

        p_ref[...] = jnp.dot(h_sc[...], w_ref[...], preferred_element_type=f32)

    vec = pl.BlockSpec((1, D_MODEL), lambda i, j: (0, 0))
    return pl.pallas_call(
        body, name="in_proj", grid=(s_len // tm, N_MAIN // tn),
        in_specs=[pl.BlockSpec((tm, D_MODEL), lambda i, j: (i, 0)), vec, vec, vec,
                  pl.BlockSpec((D_MODEL, tn), lambda i, j: (0, j)),
                  pl.BlockSpec((D_MODEL, N_SMALL), lambda i, j: (0, 0))],
        out_specs=[pl.BlockSpec((tm, tn), lambda i, j: (i, j)),
                   pl.BlockSpec((tm, N_SMALL), lambda i, j: (i, 0)),
                   pl.BlockSpec((tm, D_MODEL), lambda i, j: (i, 0))],
        out_shape=[jax.ShapeDtypeStruct((s_len, N_MAIN), f32), jax.ShapeDtypeStruct((s_len, N_SMALL), f32),
                   jax.ShapeDtypeStruct((s_len, D_MODEL), bf16)],
        scratch_shapes=[pltpu.VMEM((tm, D_MODEL), bf16)],
        compiler_params=_params("parallel", "arbitrary"),
    )(x, norm_g, scale1p, shift, w_main, w_small)


PREP_TM = 256
HALO = 8


def _conv_section(xe_ref, cw_ref, cols, tm):
    acc = cw_ref[pl.ds(CONV_K - 1, 1), cols] * xe_ref[pl.ds(HALO, tm), :]
    for tap in range(CONV_K - 1):
        acc = acc + cw_ref[pl.ds(tap, 1), cols] * xe_ref[pl.ds(HALO - (CONV_K - 1) + tap, tm), :]
    return acc


def _small_fwd(ps, bvec, alog):
    z = ps + bvec
    logsig, softp = _softplus_parts(z)
    gval = -jnp.exp(alog) * softp
    beta = _sigmoid(ps)
    return z, logsig, gval, beta


def _lane_group_selector(first_lane):
    return (_iota((LANES, WIDTH), 0) == first_lane + _iota((LANES, WIDTH), 1) // HEAD_DIM).astype(f32)


def _prep(p_main, p_small, qn_g, kn_g, conv_w, bvec, alog):
    s_len = p_main.shape[0]
    tm = PREP_TM
    nb = s_len // tm

    def body(fq_ref, fk_ref, fv_ref, gq_ref, gk_ref, gv_ref, hq_ref, hk_ref, hv_ref, ps_ref, qg_ref, kg_ref,
             cw_ref, bv_ref, al_ref,
             qs_ref, kn_ref, vb_ref, gqo_ref, gko_ref, gvo_ref, small_ref, gb_ref, bb_ref, xe_sc, carry_sc):
        i = pl.program_id(0)

        @pl.when(i == 0)
        def _():
            carry_sc[...] = jnp.zeros_like(carry_sc)

        qg, kg = qg_ref[...], kg_ref[...]
        for h in range(HEADS):
            sl = slice(h * HEAD_DIM, (h + 1) * HEAD_DIM)
            q = fq_ref[:, sl]
            rq = lax.rsqrt(jnp.mean(q * q, axis=-1, keepdims=True) + EPS)
            qs_ref[:, sl] = (q * rq * qg * QK_SCALE).astype(bf16)
            k = fk_ref[:, sl]
            rk = lax.rsqrt(jnp.mean(k * k, axis=-1, keepdims=True) + EPS)
            kn_ref[:, sl] = (k * rk * kg).astype(bf16)
        vb_ref[...] = fv_ref[...].astype(bf16)

        first = i == 0
        for sec, (x_ref, halo_ref, o_ref) in enumerate(((gq_ref, hq_ref, gqo_ref), (gk_ref, hk_ref, gko_ref),
                                                        (gv_ref, hv_ref, gvo_ref))):
            xe_sc[0:HALO, :] = jnp.where(first, 0.0, halo_ref[...])
            xe_sc[HALO:, :] = x_ref[...]
            cv = _conv_section(xe_sc, cw_ref, slice(sec * WIDTH, (sec + 1) * WIDTH), tm)
            y = cv * _sigmoid(cv)
            if sec == 2:
                o_ref[...] = y
            else:
                mul = QK_SCALE if sec == 0 else 1.0
                for h in range(HEADS):
                    sl = slice(h * HEAD_DIM, (h + 1) * HEAD_DIM)
                    yh = y[:, sl]
                    o_ref[:, sl] = yh * (lax.rsqrt(jnp.sum(yh * yh, axis=-1, keepdims=True) + EPS) * mul)

        lane = _iota((tm, N_SMALL), 1)
        _, logsig, gval, beta = _small_fwd(ps_ref[...], bv_ref[...], al_ref[...])
        lf = jnp.where(lane < HEADS, logsig, 0.0)
        tri = (_iota((tm, tm), 0) >= _iota((tm, tm), 1)).astype(f32)
        fcum = jnp.dot(tri, lf, preferred_element_type=f32, precision=HI) + carry_sc[...]
        carry_sc[...] += jnp.sum(lf, axis=0, keepdims=True)
        small = jnp.where(lane < HEADS, fcum, jnp.where(lane < 2 * HEADS, gval, jnp.where(lane < 3 * HEADS, beta, 0.0)))
        small_ref[...] = small
        gb_ref[...] = jnp.dot(small, _lane_group_selector(HEADS), preferred_element_type=f32, precision=HI)
        bb_ref[...] = jnp.dot(small, _lane_group_selector(2 * HEADS), preferred_element_type=f32, precision=HI)

    def col(cb):
        return pl.BlockSpec((tm, WIDTH), lambda i: (i, cb))

    def halo(cb):
        return pl.BlockSpec((HALO, WIDTH), lambda i: (jnp.maximum(i * (tm // HALO) - 1, 0), cb))

    vec = pl.BlockSpec((1, LANES), lambda i: (0, 0))
    wide_f32 = jax.ShapeDtypeStruct((s_len, WIDTH), f32)
    wide_bf = jax.ShapeDtypeStruct((s_len, WIDTH), bf16)
    out_col = pl.BlockSpec((tm, WIDTH), lambda i: (i, 0))
    return pl.pallas_call(
        body, name="prep", grid=(nb,),
        in_specs=[col(0), col(1), col(2), col(4), col(5), col(6), halo(4), halo(5), halo(6),
                  pl.BlockSpec((tm, N_SMALL), lambda i: (i, 0)), vec, vec,
                  pl.BlockSpec((CONV_K, 3 * WIDTH), lambda i: (0, 0)), vec, vec],
        out_specs=[out_col] * 6 + [pl.BlockSpec((tm, N_SMALL), lambda i: (i, 0)), out_col, out_col],
        out_shape=[wide_bf, wide_bf, wide_bf, wide_f32, wide_f32, wide_f32,
                   jax.ShapeDtypeStruct((s_len, N_SMALL), f32), wide_f32, wide_f32],
        scratch_shapes=[pltpu.VMEM((tm + HALO, WIDTH), f32), pltpu.VMEM((1, N_SMALL), f32)],
        compiler_params=_params("arbitrary"),
    )(p_main, p_main, p_main, p_main, p_main, p_main, p_main, p_main, p_main, p_small, qn_g, kn_g, conv_w, bvec, alog)


FOX_T = 512
NEG_BIG = -1e30


def _fox_fwd(qs, kn, vb, f_col, f_row):
    s_len = qs.shape[0]
    t = FOX_T
    nq = s_len // t

    def body(q_ref, k_ref, v_ref, fc_ref, fr_ref, o_ref, lse_ref):
        qi = pl.program_id(1)
        q = q_ref[...]
        fq = fc_ref[0]
        causal = _iota((t, t), 0) >= _iota((t, t), 1)

        def step(j, carry, masked):
            m, l, acc = carry
            rows = pl.ds(pl.multiple_of(j * t, t), t)
            s = _dg(q, k_ref[rows, :], 1, 1) + (fq - fr_ref[0, j])
            if masked:
                s = jnp.where(causal, s, NEG_BIG)
            m_new = jnp.maximum(m, jnp.max(s, axis=-1, keepdims=True))
            p = jnp.exp(s - m_new)
            alpha = jnp.exp(m - m_new)
            l = alpha * l + jnp.sum(p, axis=-1, keepdims=True)
            acc = alpha * acc + jnp.dot(p.astype(bf16), v_ref[rows, :], preferred_element_type=f32)
            return m_new, l, acc

        init = (jnp.full((t, 1), NEG_BIG, f32), jnp.zeros((t, 1), f32), jnp.zeros((t, HEAD_DIM), f32))
        carry = lax.fori_loop(0, qi, lambda j, c: step(j, c, False), init)
        m, l, acc = step(qi, carry, True)
        o_ref[...] = acc / l
        lse_ref[0] = m + jnp.log(l)

    return pl.pallas_call(
        body, name="fox_fwd", grid=(HEADS, nq),
        in_specs=[pl.BlockSpec((t, HEAD_DIM), lambda h, i: (i, h)),
                  pl.BlockSpec((s_len, HEAD_DIM), lambda h, i: (0, h)),
                  pl.BlockSpec((s_len, HEAD_DIM), lambda h, i: (0, h)),
                  pl.BlockSpec((1, t, 1), lambda h, i: (h, i, 0)),
                  pl.BlockSpec((1, nq, 1, t), lambda h, i: (h, 0, 0, 0))],
        out_specs=[pl.BlockSpec((t, HEAD_DIM), lambda h, i: (i, h)),
                   pl.BlockSpec((1, t, 1), lambda h, i: (h, i, 0))],
        out_shape=[jax.ShapeDtypeStruct((s_len, WIDTH), f32), jax.ShapeDtypeStruct((HEADS, s_len, 1), f32)],
        compiler_params=_params("parallel", "arbitrary"),
    )(qs, kn, vb, f_col, f_row)


def _fox_bwd(qs, kn, vb, do, a_col, delta_col, f_row):
    s_len = qs.shape[0]
    t = FOX_T
    nq = s_len // t

    def body(q_ref, do_ref, a_ref, dl_ref, k_ref, v_ref, fr_ref, dq_ref, dk_ref, dv_ref, df_ref, dfq_ref):
        qi = pl.program_id(1)

        @pl.when(qi == 0)
        def _():
            dk_ref[...] = jnp.zeros_like(dk_ref)
            dv_ref[...] = jnp.zeros_like(dv_ref)
            df_ref[...] = jnp.zeros_like(df_ref)

        q, do_b = q_ref[...], do_ref[...]
        a, dl = a_ref[0], dl_ref[0]
        causal = _iota((t, t), 0) >= _iota((t, t), 1)

        def step(j, carry, masked):
            dq, row_sum = carry
            rows = pl.ds(pl.multiple_of(j * t, t), t)
            kj, vj = k_ref[rows, :], v_ref[rows, :]
            p = jnp.exp(_dg(q, kj, 1, 1) + (a - fr_ref[0, j]))
            if masked:
                p = jnp.where(causal, p, 0.0)
            ds = p * (_dg(do_b, vj, 1, 1) - dl)
            ds_b = ds.astype(bf16)
            dk_ref[rows, :] += _dg(ds_b, q, 0, 0)
            dv_ref[rows, :] += _dg(p.astype(bf16), do_b, 0, 0)
            df_ref[0, j] += -jnp.sum(ds, axis=0, keepdims=True)
            return dq + jnp.dot(ds_b, kj, preferred_element_type=f32), row_sum + jnp.sum(ds, axis=-1, keepdims=True)

        carry = lax.fori_loop(0, qi, lambda j, c: step(j, c, False),
                              (jnp.zeros((t, HEAD_DIM), f32), jnp.zeros((t, 1), f32)))
        dq, row_sum = step(qi, carry, True)
        dq_ref[...] = dq
        dfq_ref[0] = row_sum

    blk = pl.BlockSpec((t, HEAD_DIM), lambda h, i: (i, h))
    full = pl.BlockSpec((s_len, HEAD_DIM), lambda h, i: (0, h))
    colv = pl.BlockSpec((1, t, 1), lambda h, i: (h, i, 0))
    rowv = pl.BlockSpec((1, nq, 1, t), lambda h, i: (h, 0, 0, 0))
    wide = jax.ShapeDtypeStruct((s_len, WIDTH), f32)
    return pl.pallas_call(
        body, name="fox_bwd", grid=(HEADS, nq),
        in_specs=[blk, blk, colv, colv, full, full, rowv],
        out_specs=[blk, full, full, rowv, colv],
        out_shape=[wide, wide, wide, jax.ShapeDtypeStruct((HEADS, nq, 1, t), f32),
                   jax.ShapeDtypeStruct((HEADS, s_len, 1), f32)],
        compiler_params=_params("parallel", "arbitrary"),
    )(qs, do, a_col, delta_col, kn, vb, f_row)


GDN_CHUNKS_PER_BLOCK = 16


def _gdn_fwd(gq, gk, gv, g_b, beta_b):
    s_len = gq.shape[0]
    cpb = GDN_CHUNKS_PER_BLOCK
    rows_blk = cpb * CHUNK
    n_chunks = s_len // CHUNK
    nb = n_chunks // cpb

    def body(q_ref, k_ref, v_ref, g_ref, b_ref, o_ref, st_ref, s_sc):
        @pl.when(pl.program_id(1) == 0)
        def _():
            s_sc[...] = jnp.zeros_like(s_sc)

        def chunk(ci, _):
            rows = pl.ds(pl.multiple_of(ci * CHUNK, CHUNK), CHUNK)
            s0 = s_sc[...]
            st_ref[0, ci] = s0
            o, s1 = _gdn_chunk(False, s0, q_ref[rows, :], k_ref[rows, :], v_ref[rows, :], g_ref[rows, :],
                               b_ref[rows, :])
            o_ref[rows, :] = o
            s_sc[...] = s1
            return 0

        lax.fori_loop(0, cpb, chunk, 0)

    blk = pl.BlockSpec((rows_blk, HEAD_DIM), lambda h, i: (i, h))
    return pl.pallas_call(
        body, name="gdn_fwd", grid=(HEADS, nb),
        in_specs=[blk] * 5,
        out_specs=[blk, pl.BlockSpec((1, cpb, HEAD_DIM, HEAD_DIM), lambda h, i: (h, i, 0, 0))],
        out_shape=[jax.ShapeDtypeStruct((s_len, WIDTH), f32),
                   jax.ShapeDtypeStruct((HEADS, n_chunks, HEAD_DIM, HEAD_DIM), f32)],
        scratch_shapes=[pltpu.VMEM((HEAD_DIM, HEAD_DIM), f32)],
        compiler_params=_params("parallel", "arbitrary"),
    )(gq, gk, gv, g_b, beta_b)


def _gdn_bwd(gq, gk, gv, g_b, beta_b, states, d_o):
    s_len = gq.shape[0]
    cpb = GDN_CHUNKS_PER_BLOCK
    rows_blk = cpb * CHUNK
    nb = s_len // rows_blk

    def body(q_ref, k_ref, v_ref, g_ref, b_ref, st_ref, do_ref, dq_ref, dk_ref, dv_ref, dg_ref, db_ref, ds_sc):
        @pl.when(pl.program_id(1) == 0)
        def _():
            ds_sc[...] = jnp.zeros_like(ds_sc)

        def chunk(step, _):
            ci = cpb - 1 - step
            rows = pl.ds(pl.multiple_of(ci * CHUNK, CHUNK), CHUNK)
            _, vjp = jax.vjp(functools.partial(_gdn_chunk, True), st_ref[0, ci], q_ref[rows, :], k_ref[rows, :],
                             v_ref[rows, :], g_ref[rows, :], b_ref[rows, :])
            ds0, dq, dk, dv, dg, db = vjp((do_ref[rows, :], ds_sc[...]))
            ds_sc[...] = ds0
            dq_ref[rows, :] = dq
            dk_ref[rows, :] = dk
            dv_ref[rows, :] = dv
            dg_ref[rows, :] = dg
            db_ref[rows, :] = db
            return 0

        lax.fori_loop(0, cpb, chunk, 0)

    blk = pl.BlockSpec((rows_blk, HEAD_DIM), lambda h, i: (nb - 1 - i, h))
    wide = jax.ShapeDtypeStruct((s_len, WIDTH), f32)
    return pl.pallas_call(
        body, name="gdn_bwd", grid=(HEADS, nb),
        in_specs=[blk] * 5 + [pl.BlockSpec((1, cpb, HEAD_DIM, HEAD_DIM), lambda h, i: (h, nb - 1 - i, 0, 0)), blk],
        out_specs=[blk] * 5,
        out_shape=[wide] * 5,
        scratch_shapes=[pltpu.VMEM((HEAD_DIM, HEAD_DIM), f32)],
        compiler_params=_params("parallel", "arbitrary"),
    )(gq, gk, gv, g_b, beta_b, states, d_o)


MIX_TM = 256


def _mix_fwd(fox_o, gdn_o, p_main, gnorm_g):
    s_len = fox_o.shape[0]
    tm = MIX_TM

    def body(fo_ref, go_ref, fz_ref, gz_ref, g_ref, mixed_ref):
        fz = fz_ref[...]
        mixed_ref[:, 0:WIDTH] = (fo_ref[...] * (fz * _sigmoid(fz))).astype(bf16)
        gz = gz_ref[...]
        gate = gz * _sigmoid(gz)
        gg = g_ref[...]
        for h in range(HEADS):
            sl = slice(h * HEAD_DIM, (h + 1) * HEAD_DIM)
            o = go_ref[:, sl]
            r = lax.rsqrt(jnp.mean(o * o, axis=-1, keepdims=True) + EPS)
            mixed_ref[:, WIDTH + h * HEAD_DIM:WIDTH + (h + 1) * HEAD_DIM] = (o * r * gg * gate[:, sl]).astype(bf16)

    row = pl.BlockSpec((tm, WIDTH), lambda i: (i, 0))
    return pl.pallas_call(
        body, name="mix_fwd", grid=(s_len // tm,),
        in_specs=[row, row, pl.BlockSpec((tm, WIDTH), lambda i: (i, 3)), pl.BlockSpec((tm, WIDTH), lambda i: (i, 7)),
                  pl.BlockSpec((1, LANES), lambda i: (0, 0))],
        out_specs=pl.BlockSpec((tm, 2 * WIDTH), lambda i: (i, 0)),
        out_shape=jax.ShapeDtypeStruct((s_len, 2 * WIDTH), bf16),
        compiler_params=_params("parallel"),
    )(fox_o, gdn_o, p_main, p_main, gnorm_g)


def _silu_grad(z):
    sg = _sigmoid(z)
    return sg * (1.0 + z * (1.0 - sg))


def _mix_bwd(dmixed, fox_o, gdn_o, p_main, gnorm_g):
    s_len = fox_o.shape[0]
    tm = MIX_TM

    def body(dm_ref, fo_ref, go_ref, fz_ref, gz_ref, g_ref, dof_ref, delta_ref, dfz_ref, dgz_ref, dgo_ref, dg_ref):
        @pl.when(pl.program_id(0) == 0)
        def _():
            dg_ref[...] = jnp.zeros_like(dg_ref)

        lane = _iota((tm, LANES), 1)
        fz = fz_ref[...]
        dmf = dm_ref[:, 0:WIDTH]
        fo = fo_ref[...]
        dof = dmf * (fz * _sigmoid(fz))
        dof_ref[...] = dof.astype(bf16)
        dfz_ref[...] = (dmf * fo * _silu_grad(fz)).astype(bf16)
        prod = dof * fo
        delta = jnp.zeros((tm, LANES), f32)
        for h in range(HEADS):
            dh = jnp.sum(prod[:, h * HEAD_DIM:(h + 1) * HEAD_DIM], axis=-1, keepdims=True)
            delta = jnp.where(lane == h, dh, delta)
        delta_ref[...] = delta

        gz = gz_ref[...]
        dmg = dm_ref[:, WIDTH:2 * WIDTH]
        gate = gz * _sigmoid(gz)
        sgrad = _silu_grad(gz)
        gg = g_ref[...]
        dg_acc = jnp.zeros((1, HEAD_DIM), f32)
        for h in range(HEADS):
            sl = slice(h * HEAD_DIM, (h + 1) * HEAD_DIM)
            o = go_ref[:, sl]
            r = lax.rsqrt(jnp.mean(o * o, axis=-1, keepdims=True) + EPS)
            on = o * r
            dmh = dmg[:, sl]
            dgz_ref[:, sl] = (dmh * (on * gg) * sgrad[:, sl]).astype(bf16)
            dy = dmh * gate[:, sl]
            dg_acc = dg_acc + jnp.sum(dy * on, axis=0, keepdims=True)
            tt = dy * gg
            dgo_ref[:, sl] = r * (tt - on * jnp.mean(tt * on, axis=-1, keepdims=True))
        dg_ref[...] += dg_acc

    row = pl.BlockSpec((tm, WIDTH), lambda i: (i, 0))
    wide_bf = jax.ShapeDtypeStruct((s_len, WIDTH), bf16)
    return pl.pallas_call(
        body, name="mix_bwd", grid=(s_len // tm,),
        in_specs=[pl.BlockSpec((tm, 2 * WIDTH), lambda i: (i, 0)), row, row,
                  pl.BlockSpec((tm, WIDTH), lambda i: (i, 3)), pl.BlockSpec((tm, WIDTH), lambda i: (i, 7)),
                  pl.BlockSpec((1, LANES), lambda i: (0, 0))],
        out_specs=[row, pl.BlockSpec((tm, LANES), lambda i: (i, 0)), row, row, row,
                   pl.BlockSpec((1, LANES), lambda i: (0, 0))],
        out_shape=[wide_bf, jax.ShapeDtypeStruct((s_len, LANES), f32), wide_bf, wide_bf,
                   jax.ShapeDtypeStruct((s_len, WIDTH), f32), jax.ShapeDtypeStruct((1, LANES), f32)],
        compiler_params=_params("arbitrary"),
    )(dmixed, fox_o, gdn_o, p_main, p_main, gnorm_g)


def _out_head(mixed, w_out, x, target, gate, final_g):
    s_len = x.shape[0]
    tm = 256

    def body(mx_ref, w_ref, x_ref, t_ref, gate_ref, fg_ref, loss_ref, dy_ref, dz_ref, dm_ref, dfg_ref, dgate_ref):
        @pl.when(pl.program_id(0) == 0)
        def _():
            loss_ref[...] = jnp.zeros_like(loss_ref)
            dfg_ref[...] = jnp.zeros_like(dfg_ref)
            dgate_ref[...] = jnp.zeros_like(dgate_ref)

        w = w_ref[...]
        z = jnp.dot(mx_ref[...], w, preferred_element_type=f32)
        gate_v, fg = gate_ref[...], fg_ref[...]
        y1 = x_ref[...] + gate_v * z
        r = lax.rsqrt(jnp.mean(y1 * y1, axis=-1, keepdims=True) + EPS)
        yn = y1 * r
        err = yn * fg - t_ref[...]
        loss_ref[...] += 0.5 * jnp.sum(jnp.mean(err * err, axis=-1, keepdims=True))
        dout = err * (1.0 / D_MODEL)
        dfg_ref[...] += jnp.sum(dout * yn, axis=0, keepdims=True)
        tt = dout * fg
        dy1 = r * (tt - yn * jnp.mean(tt * yn, axis=-1, keepdims=True))
        dy_ref[...] = dy1
        dgate_ref[...] += jnp.sum(dy1 * z, axis=0, keepdims=True)
        dz = (dy1 * gate_v).astype(bf16)
        dz_ref[...] = dz
        dm_ref[...] = _dg(dz, w, 1, 1)

    row = pl.BlockSpec((tm, D_MODEL), lambda i: (i, 0))
    vec = pl.BlockSpec((1, D_MODEL), lambda i: (0, 0))
    big = jax.ShapeDtypeStruct((s_len, D_MODEL), f32)
    return pl.pallas_call(
        body, name="out_head", grid=(s_len // tm,),
        in_specs=[row, pl.BlockSpec((D_MODEL, D_MODEL), lambda i: (0, 0)), row, row, vec, vec],
        out_specs=[pl.BlockSpec((1, LANES), lambda i: (0, 0)), row, row, row, vec, vec],
        out_shape=[jax.ShapeDtypeStruct((1, LANES), f32), big, jax.ShapeDtypeStruct((s_len, D_MODEL), bf16), big,
                   jax.ShapeDtypeStruct((1, D_MODEL), f32), jax.ShapeDtypeStruct((1, D_MODEL), f32)],
        compiler_params=_params("arbitrary"),
    )(mixed, w_out, x, target, gate, final_g)


def _matmul_tn(name, a, b):
    k_len, m_len = a.shape
    n_len = b.shape[1]
    tk, tm, tn = 512, 1024, min(1024, n_len)

    def body(a_ref, b_ref, o_ref):
        @pl.when(pl.program_id(2) == 0)
        def _():
            o_ref[...] = jnp.zeros_like(o_ref)

        o_ref[...] += _dg(a_ref[...], b_ref[...], 0, 0)

    return pl.pallas_call(
        body, name=name, grid=(m_len // tm, n_len // tn, k_len // tk),
        in_specs=[pl.BlockSpec((tk, tm), lambda i, j, k: (k, i)), pl.BlockSpec((tk, tn), lambda i, j, k: (k, j))],
        out_specs=pl.BlockSpec((tm, tn), lambda i, j, k: (i, j)),
        out_shape=jax.ShapeDtypeStruct((m_len, n_len), f32),
        compiler_params=_params("parallel", "parallel", "arbitrary"),
    )(a, b)


def _post1(p_main, p_small, qn_g, kn_g, conv_w, bvec, alog, dqs, dkn, dgq, dgk, dgv, dg_b, dbeta_b, df):
    s_len = p_main.shape[0]
    tm = PREP_TM
    nb = s_len // tm

    def body(fq_ref, fk_ref, gq_ref, gk_ref, gv_ref, hq_ref, hk_ref, hv_ref, ps_ref, qg_ref, kg_ref, cw_ref, bv_ref,
             al_ref, dqs_ref, dkn_ref, dgq_ref, dgk_ref, dgv_ref, dgb_ref, dbb_ref, df_ref,
             dfq_ref, dfk_ref, dconv_ref, dps_ref, dqg_ref, dkg_ref, sums_ref, xe_sc, carry_sc):
        step = pl.program_id(0)
        blk = nb - 1 - step

        @pl.when(step == 0)
        def _():
            carry_sc[...] = jnp.zeros_like(carry_sc)
            dqg_ref[...] = jnp.zeros_like(dqg_ref)
            dkg_ref[...] = jnp.zeros_like(dkg_ref)
            sums_ref[...] = jnp.zeros_like(sums_ref)

        for x_ref, g_ref, dy_ref, o_ref, acc_ref, mul in ((fq_ref, qg_ref, dqs_ref, dfq_ref, dqg_ref, QK_SCALE),
                                                          (fk_ref, kg_ref, dkn_ref, dfk_ref, dkg_ref, 1.0)):
            gain = g_ref[...]
            acc = jnp.zeros((1, HEAD_DIM), f32)
            for h in range(HEADS):
                sl = slice(h * HEAD_DIM, (h + 1) * HEAD_DIM)
                xv = x_ref[:, sl]
                r = lax.rsqrt(jnp.mean(xv * xv, axis=-1, keepdims=True) + EPS)
                xn = xv * r
                dy = dy_ref[:, sl] * mul
                acc = acc + jnp.sum(dy * xn, axis=0, keepdims=True)
                tt = dy * gain
                o_ref[:, sl] = (r * (tt - xn * jnp.mean(tt * xn, axis=-1, keepdims=True))).astype(bf16)
            acc_ref[...] += acc

        first = blk == 0
        for sec, (x_ref, halo_ref, dy_ref) in enumerate(((gq_ref, hq_ref, dgq_ref), (gk_ref, hk_ref, dgk_ref),
                                                         (gv_ref, hv_ref, dgv_ref))):
            xe_sc[0:HALO, :] = jnp.where(first, 0.0, halo_ref[...])
            xe_sc[HALO:, :] = x_ref[...]
            cv = _conv_section(xe_sc, cw_ref, slice(sec * WIDTH, (sec + 1) * WIDTH), tm)
            sgrad = _silu_grad(cv)
            if sec == 2:
                dconv_ref[:, sec * WIDTH:(sec + 1) * WIDTH] = dy_ref[...] * sgrad
            else:
                y = cv * _sigmoid(cv)
                mul = QK_SCALE if sec == 0 else 1.0
                for h in range(HEADS):
                    sl = slice(h * HEAD_DIM, (h + 1) * HEAD_DIM)
                    yh = y[:, sl]
                    r = lax.rsqrt(jnp.sum(yh * yh, axis=-1, keepdims=True) + EPS)
                    dqh = dy_ref[:, sl]
                    dyh = (mul * r) * (dqh - yh * (r * r) * jnp.sum(dqh * yh, axis=-1, keepdims=True))
                    dconv_ref[:, sec * WIDTH + h * HEAD_DIM:sec * WIDTH + (h + 1) * HEAD_DIM] = dyh * sgrad[:, sl]

        lane = _iota((tm, N_SMALL), 1)
        z, _, gval, beta = _small_fwd(ps_ref[...], bv_ref[...], al_ref[...])
        sig_z = _sigmoid(z)
        sel_t = (_iota((WIDTH, LANES), 1) == HEADS + _iota((WIDTH, LANES), 0) // HEAD_DIM).astype(f32)
        dg = jnp.dot(dgb_ref[...], sel_t, preferred_element_type=f32, precision=HI)
        sel_t2 = (_iota((WIDTH, LANES), 1) == 2 * HEADS + _iota((WIDTH, LANES), 0) // HEAD_DIM).astype(f32)
        dbeta = jnp.dot(dbb_ref[...], sel_t2, preferred_element_type=f32, precision=HI)
        dfb = jnp.where(lane < HEADS, df_ref[...], 0.0)
        tri_u = (_iota((tm, tm), 1) >= _iota((tm, tm), 0)).astype(f32)
        dlogf = jnp.dot(tri_u, dfb, preferred_element_type=f32, precision=HI) + carry_sc[...]
        carry_sc[...] += jnp.sum(dfb, axis=0, keepdims=True)
        dff = dlogf * (1.0 - sig_z)
        dga = dg * (-jnp.exp(al_ref[...])) * sig_z
        dgb_small = dbeta * beta * (1.0 - beta)
        dps = jnp.where(lane < HEADS, dff, jnp.where(lane < 2 * HEADS, dga, jnp.where(lane < 3 * HEADS, dgb_small, 0.0)))
        dps_ref[...] = dps.astype(bf16)
        row = _iota((8, N_SMALL), 0)
        s0 = jnp.sum(dps, axis=0, keepdims=True)
        s1 = jnp.sum(jnp.where((lane >= HEADS) & (lane < 2 * HEADS), dg * gval, 0.0), axis=0, keepdims=True)
        sums_ref[...] += jnp.where(row == 0, s0, jnp.where(row == 1, s1, 0.0))

    def col(cb):
        return pl.BlockSpec((tm, WIDTH), lambda i: (nb - 1 - i, cb))

    def halo(cb):
        return pl.BlockSpec((HALO, WIDTH), lambda i: (jnp.maximum((nb - 1 - i) * (tm // HALO) - 1, 0), cb))

    vec = pl.BlockSpec((1, LANES), lambda i: (0, 0))
    row0 = pl.BlockSpec((tm, WIDTH), lambda i: (nb - 1 - i, 0))
    small = pl.BlockSpec((tm, N_SMALL), lambda i: (nb - 1 - i, 0))
    wide_bf = jax.ShapeDtypeStruct((s_len, WIDTH), bf16)
    return pl.pallas_call(
        body, name="post1", grid=(nb,),
        in_specs=[col(0), col(1), col(4), col(5), col(6), halo(4), halo(5), halo(6), small, vec, vec,
                  pl.BlockSpec((CONV_K, 3 * WIDTH), lambda i: (0, 0)), vec, vec,
                  row0, row0, row0, row0, row0, row0, row0, small],
        out_specs=[row0, row0, pl.BlockSpec((tm, 3 * WIDTH), lambda i: (nb - 1 - i, 0)), small, vec, vec,
                   pl.BlockSpec((8, N_SMALL), lambda i: (0, 0))],
        out_shape=[wide_bf, wide_bf, jax.ShapeDtypeStruct((s_len, 3 * WIDTH), f32),
                   jax.ShapeDtypeStruct((s_len, N_SMALL), bf16), jax.ShapeDtypeStruct((1, LANES), f32),
                   jax.ShapeDtypeStruct((1, LANES), f32), jax.ShapeDtypeStruct((8, N_SMALL), f32)],
        scratch_shapes=[pltpu.VMEM((tm + HALO, WIDTH), f32), pltpu.VMEM((1, N_SMALL), f32)],
        compiler_params=_params("arbitrary"),
    )(p_main, p_main, p_main, p_main, p_main, p_main, p_main, p_main, p_small, qn_g, kn_g, conv_w, bvec, alog,
      dqs, dkn, dgq, dgk, dgv, dg_b, dbeta_b, df)


def _post2(p_main, dconv, conv_w):
    s_len = p_main.shape[0]
    tm = PREP_TM
    nb = s_len // tm

    def body(gq_ref, gk_ref, gv_ref, hq_ref, hk_ref, hv_ref, dc_ref, dnext_ref, cw_ref, dx_ref, dw_ref, xe_sc, de_sc):
        i = pl.program_id(0)

        @pl.when(i == 0)
        def _():
            dw_ref[...] = jnp.zeros_like(dw_ref)

        first, last = i == 0, i == nb - 1
        row = _iota((8, WIDTH), 0)
        for sec, (x_ref, halo_ref) in enumerate(((gq_ref, hq_ref), (gk_ref, hk_ref), (gv_ref, hv_ref))):
            cols = slice(sec * WIDTH, (sec + 1) * WIDTH)
            dc = dc_ref[:, cols]
            de_sc[0:tm, :] = dc
            de_sc[tm:, :] = jnp.where(last, 0.0, dnext_ref[:, cols])
            dx = cw_ref[pl.ds(CONV_K - 1, 1), cols] * dc
            for tap in range(CONV_K - 1):
                dx = dx + cw_ref[pl.ds(tap, 1), cols] * de_sc[pl.ds(CONV_K - 1 - tap, tm), :]
            dx_ref[:, cols] = dx.astype(bf16)
            xe_sc[0:HALO, :] = jnp.where(first, 0.0, halo_ref[...])
            xe_sc[HALO:, :] = x_ref[...]
            dw = jnp.zeros((8, WIDTH), f32)
            for tap in range(CONV_K):
                contrib = jnp.sum(dc * xe_sc[pl.ds(HALO - (CONV_K - 1) + tap, tm), :], axis=0, keepdims=True)
                dw = jnp.where(row == tap, contrib, dw)
            dw_ref[:, cols] += dw

    def col(cb):
        return pl.BlockSpec((tm, WIDTH), lambda i: (i, cb))

    def halo(cb):
        return pl.BlockSpec((HALO, WIDTH), lambda i: (jnp.maximum(i * (tm // HALO) - 1, 0), cb))

    return pl.pallas_call(
        body, name="post2", grid=(nb,),
        in_specs=[col(4), col(5), col(6), halo(4), halo(5), halo(6),
                  pl.BlockSpec((tm, 3 * WIDTH), lambda i: (i, 0)),
                  pl.BlockSpec((HALO, 3 * WIDTH), lambda i: (jnp.minimum((i + 1) * (tm // HALO), s_len // HALO - 1), 0)),
                  pl.BlockSpec((CONV_K, 3 * WIDTH), lambda i: (0, 0))],
        out_specs=[pl.BlockSpec((tm, 3 * WIDTH), lambda i: (i, 0)), pl.BlockSpec((8, 3 * WIDTH), lambda i: (0, 0))],
        out_shape=[jax.ShapeDtypeStruct((s_len, 3 * WIDTH), bf16), jax.ShapeDtypeStruct((8, 3 * WIDTH), f32)],
        scratch_shapes=[pltpu.VMEM((tm + HALO, WIDTH), f32), pltpu.VMEM((tm + HALO, WIDTH), f32)],
        compiler_params=_params("arbitrary"),
    )(p_main, p_main, p_main, p_main, p_main, p_main, dconv, dconv, conv_w)


def _in_proj_bwd(dp_main, dp_small, w_main, w_small, x, dy1, norm_g, scale1p):
    s_len = x.shape[0]
    tm, tk = 512, 1024
    nk = N_MAIN // tk

    def body(dp_ref, dps_ref, w_ref, ws_ref, x_ref, dy_ref, g_ref, sc_ref, dx_ref, dsh_ref, dsc_ref, dg_ref, acc_sc):
        i, k = pl.program_id(0), pl.program_id(1)

        @pl.when((i == 0) & (k == 0))
        def _():
            dsh_ref[...] = jnp.zeros_like(dsh_ref)
            dsc_ref[...] = jnp.zeros_like(dsc_ref)
            dg_ref[...] = jnp.zeros_like(dg_ref)

        @pl.when(k == 0)
        def _():
            acc_sc[...] = _dg(dps_ref[...], ws_ref[...], 1, 1)

        acc_sc[...] += _dg(dp_ref[...], w_ref[...], 1, 1)

        @pl.when(k == nk - 1)
        def _():
            dh = acc_sc[...]
            xb = x_ref[...]
            r = lax.rsqrt(jnp.mean(xb * xb, axis=-1, keepdims=True) + EPS)
            xr = xb * r
            gain = g_ref[...]
            dsh_ref[...] += jnp.sum(dh, axis=0, keepdims=True)
            dsc_ref[...] += jnp.sum(dh * (xr * gain), axis=0, keepdims=True)
            dxn = dh * sc_ref[...]
            dg_ref[...] += jnp.sum(dxn * xr, axis=0, keepdims=True)
            tt = dxn * gain
            dx_ref[...] = r * (tt - xr * jnp.mean(tt * xr, axis=-1, keepdims=True)) + dy_ref[...]

    row = pl.BlockSpec((tm, D_MODEL), lambda i, k: (i, 0))
    vec = pl.BlockSpec((1, D_MODEL), lambda i, k: (0, 0))
    vshape = jax.ShapeDtypeStruct((1, D_MODEL), f32)
    return pl.pallas_call(
        body, name="in_proj_bwd", grid=(s_len // tm, nk),
        in_specs=[pl.BlockSpec((tm, tk), lambda i, k: (i, k)), pl.BlockSpec((tm, N_SMALL), lambda i, k: (i, 0)),
                  pl.BlockSpec((D_MODEL, tk), lambda i, k: (0, k)), pl.BlockSpec((D_MODEL, N_SMALL), lambda i, k: (0, 0)),
                  row, row, vec, vec],
        out_specs=[row, vec, vec, vec],
        out_shape=[jax.ShapeDtypeStruct((s_len, D_MODEL), f32), vshape, vshape, vshape],
        scratch_shapes=[pltpu.VMEM((tm, D_MODEL), f32)],
        compiler_params=_params("arbitrary", "arbitrary"),
    )(dp_main, dp_small, w_main, w_small, x, dy1, norm_g, scale1p)


def _adamw(name, w, g_stack, m, v, tr):
    n_stack, rows, cols = g_stack.shape

    def body(w_ref, g_ref, m_ref, v_ref, go_ref, d_ref, mo_ref, vo_ref):
        g = g_ref[0]
        for k in range(1, n_stack):
            g = g + g_ref[k]
        go_ref[...] = g
        m_new = ADAM_B1 * m_ref[...] + (1.0 - ADAM_B1) * g
        v_new = ADAM_B2 * v_ref[...] + (1.0 - ADAM_B2) * (g * g)
        mo_ref[...] = m_new
        vo_ref[...] = v_new
        m_hat = m_new / (1.0 - ADAM_B1 ** ADAM_STEP)
        v_hat = v_new / (1.0 - ADAM_B2 ** ADAM_STEP)
        d_ref[...] = -ADAM_LR * (m_hat / (jnp.sqrt(v_hat) + ADAM_EPS) + ADAM_WD * w_ref[...])

    blk = pl.BlockSpec((tr, cols), lambda i: (i, 0))
    shape = jax.ShapeDtypeStruct((rows, cols), f32)
    return pl.pallas_call(
        body, name=name, grid=(rows // tr,),
        in_specs=[blk, pl.BlockSpec((n_stack, tr, cols), lambda i: (0, i, 0)), blk, blk],
        out_specs=[blk] * 4, out_shape=[shape] * 4,
        compiler_params=_params("parallel"),
    )(w, g_stack, m, v)


def _w_ada_grad(c_all_t, dmod_pad):
    def body(c_ref, d_ref, o_ref):
        cv = c_ref[...]
        o_ref[...] = jnp.dot(cv * _sigmoid(cv), d_ref[...], preferred_element_type=f32, precision=HI)

    return pl.pallas_call(body, name="w_ada_grad",
                          out_shape=jax.ShapeDtypeStruct((c_all_t.shape[0], dmod_pad.shape[1]), f32),
                          compiler_params=_params())(c_all_t, dmod_pad)


SMALL_NAMES = ("norm_g", "b_ada", "b_fgate", "fox_qn_g", "fox_kn_g", "gdn_A_log", "gdn_dt_bias", "gdn_norm_g", "final_g")
SMALL_SIZES = (D_MODEL, 3 * D_MODEL, HEADS, HEAD_DIM, HEAD_DIM, HEADS, HEADS, HEAD_DIM, D_MODEL)
SMALL_PACK = 10752


def _pack(vectors, total):
    flat = jnp.concatenate([t.reshape(-1) for t in vectors])
    return jnp.pad(flat, (0, total - flat.shape[0])).reshape(1, total)


def _lanes(*pieces):
    row = jnp.zeros((LANES,), f32)
    for off, vec in pieces:
        row = lax.dynamic_update_slice(row, vec.reshape(-1).astype(f32), (off,))
    return row.reshape(1, LANES)


def kernel(x, c, norm_g, w_ada, b_ada, w_in, b_fgate, fox_qn_g, fox_kn_g, gdn_conv_w, gdn_A_log, gdn_dt_bias, gdn_norm_g, w_out, final_g, loss_target, m_norm_g, m_w_ada, m_b_ada, m_w_in, m_b_fgate, m_fox_qn_g, m_fox_kn_g, m_gdn_conv_w, m_gdn_A_log, m_gdn_dt_bias, m_gdn_norm_g, m_w_out, m_final_g, v_norm_g, v_w_ada, v_b_ada, v_w_in, v_b_fgate, v_fox_qn_g, v_fox_kn_g, v_gdn_conv_w, v_gdn_A_log, v_gdn_dt_bias, v_gdn_norm_g, v_w_out, v_final_g):
    me = _my_index()
    s_len = x.shape[1]
    nq = s_len // FOX_T
    x2 = x.reshape(s_len, D_MODEL)
    tgt = loss_target.reshape(s_len, D_MODEL)
    ada_cols = w_ada.shape[2]
    in_cols = w_in.shape[2]
    conv_cols = gdn_conv_w.shape[2]

    (c_all,) = _exchange("gather_c", [c], scatter=False)
    c_all = c_all.reshape(N_DEV, D_MODEL)
    b_shard = lax.dynamic_slice(b_ada, (0, me * ada_cols), (1, ada_cols))
    mod_mine = _mod_shard(c_all, w_ada[0], b_shard)
    mod_all, w_in_all, w_out_all, conv_all = _exchange(
        "gather_weights", [mod_mine, w_in[0].astype(bf16), w_out[0].astype(bf16), gdn_conv_w[0]], scatter=False)
    mod = lax.dynamic_slice(mod_all, (0, me, 0), (N_DEV, 1, ada_cols)).reshape(1, 3 * D_MODEL)
    shift, scale, gate = mod[:, :D_MODEL], mod[:, D_MODEL:2 * D_MODEL], mod[:, 2 * D_MODEL:]
    scale1p = 1.0 + scale
    w_in_full = jnp.transpose(w_in_all, (1, 0, 2)).reshape(D_MODEL, N_DEV * in_cols)
    g0 = 4 * WIDTH + HEADS
    w_main = jnp.concatenate([w_in_full[:, :4 * WIDTH], w_in_full[:, g0:g0 + 4 * WIDTH]], axis=1)
    w_small = jnp.concatenate([w_in_full[:, 4 * WIDTH:g0], w_in_full[:, g0 + 4 * WIDTH:],
                               jnp.zeros((D_MODEL, N_SMALL - 3 * HEADS), bf16)], axis=1)
    w_out_full = w_out_all.reshape(2 * WIDTH, D_MODEL)
    conv_full = jnp.transpose(conv_all, (1, 0, 2)).reshape(CONV_K, 3 * WIDTH)

    qn_g, kn_g, gn_g = fox_qn_g.reshape(1, LANES), fox_kn_g.reshape(1, LANES), gdn_norm_g.reshape(1, LANES)
    bvec = _lanes((0, b_fgate), (HEADS, gdn_dt_bias))
    alog = _lanes((HEADS, gdn_A_log))
    fg = final_g.reshape(1, D_MODEL)

    p_main, p_small, h_bf = _in_proj(x2, norm_g, scale1p, shift, w_main, w_small)
    qs, kn, vb, gq, gk, gv, small, g_b, beta_b = _prep(p_main, p_small, qn_g, kn_g, conv_full, bvec, alog)
    f_heads = jnp.transpose(small[:, :HEADS])
    f_col = f_heads.reshape(HEADS, s_len, 1)
    f_row = f_heads.reshape(HEADS, nq, 1, FOX_T)
    fox_o, lse = _fox_fwd(qs, kn, vb, f_col, f_row)
    gdn_o, states = _gdn_fwd(gq, gk, gv, g_b, beta_b)
    mixed = _mix_fwd(fox_o, gdn_o, p_main, gn_g)

    loss_row, dy1, dz, dmixed, d_final_g, d_gate = _out_head(mixed, w_out_full, x2, tgt, gate, fg)
    loss = lax.psum(loss_row[0, 0], AXES)
    dw_out = _matmul_tn("dw_out", mixed, dz)
    do_fox, delta, dfz, dgz, dgdn_o, d_gn_g = _mix_bwd(dmixed, fox_o, gdn_o, p_main, gn_g)
    delta_col = jnp.transpose(delta[:, :HEADS]).reshape(HEADS, s_len, 1)
    dqs, dkn, dvf, df_key, df_query = _fox_bwd(qs, kn, vb, do_fox, f_col - lse, delta_col, f_row)
    dgq, dgk, dgv, dg_b, dbeta_b = _gdn_bwd(gq, gk, gv, g_b, beta_b, states, dgdn_o)
    df_heads = df_key.reshape(HEADS, s_len) + df_query.reshape(HEADS, s_len)
    df_small = jnp.pad(jnp.transpose(df_heads), ((0, 0), (0, N_SMALL - HEADS)))
    dfq, dfk, dconv, dp_small, d_qn_g, d_kn_g, sums = _post1(
        p_main, p_small, qn_g, kn_g, conv_full, bvec, alog, dqs, dkn, dgq, dgk, dgv, dg_b, dbeta_b, df_small)
    dgqkv, d_conv = _post2(p_main, dconv, conv_full)
    dp_main = jnp.concatenate([dfq, dfk, dvf.astype(bf16), dfz, dgqkv, dgz], axis=1)
    grad_x, d_shift, d_scale, d_norm_g = _in_proj_bwd(dp_main, dp_small, w_main, w_small, x2, dy1, norm_g, scale1p)
    dw_main = _matmul_tn("dw_main", h_bf, dp_main)
    dw_small = _matmul_tn("dw_small", h_bf, dp_small)
    dw_in_full = jnp.concatenate([dw_main[:, :4 * WIDTH], dw_small[:, :HEADS], dw_main[:, 4 * WIDTH:],
                                  dw_small[:, HEADS:3 * HEADS]], axis=1)
    dw_in_parts = jnp.transpose(dw_in_full.reshape(D_MODEL, N_DEV, in_cols), (1, 0, 2))
    dw_out_parts = dw_out.reshape(N_DEV, w_out.shape[1], D_MODEL)

    dmod = jnp.concatenate([d_shift, d_scale, d_gate], axis=1)
    small_grads = _pack([d_norm_g, dmod, sums[0, :HEADS], d_qn_g, d_kn_g, sums[1, HEADS:2 * HEADS],
                         sums[0, HEADS:2 * HEADS], d_gn_g, d_final_g], SMALL_PACK)
    conv_grad = d_conv[:CONV_K]
    dw_in_recv, dw_out_recv = _exchange("scatter_grads", [dw_in_parts, dw_out_parts], scatter=True)
    small_all, conv_all_g = _exchange("gather_small_grads", [small_grads, conv_grad], scatter=False)

    outs = {}
    outs["w_in"] = _adamw("adamw_w_in", w_in[0], dw_in_recv, m_w_in[0], v_w_in[0], 128)
    outs["w_out"] = _adamw("adamw_w_out", w_out[0], dw_out_recv, m_w_out[0], v_w_out[0], 128)
    conv_mine = lax.dynamic_slice(jnp.transpose(conv_all_g.reshape(N_DEV, CONV_K, N_DEV, conv_cols), (0, 2, 1, 3)),
                                  (0, me, 0, 0), (N_DEV, 1, CONV_K, conv_cols)).reshape(N_DEV, CONV_K, conv_cols)
    outs["gdn_conv_w"] = _adamw("adamw_conv", gdn_conv_w[0], conv_mine, m_gdn_conv_w[0], v_gdn_conv_w[0], CONV_K)
    small_all = small_all.reshape(N_DEV, 1, SMALL_PACK)
    dmod_all = small_all[:, 0, D_MODEL:D_MODEL + 3 * D_MODEL]
    dmod_mine = lax.dynamic_slice(dmod_all, (0, me * ada_cols), (N_DEV, ada_cols))
    c_all_t = jnp.pad(jnp.transpose(c_all), ((0, 0), (0, LANES - N_DEV)))
    g_w_ada = _w_ada_grad(c_all_t, jnp.pad(dmod_mine, ((0, LANES - N_DEV), (0, 0))))
    outs["w_ada"] = _adamw("adamw_w_ada", w_ada[0], g_w_ada[None], m_w_ada[0], v_w_ada[0], 256)
    given = dict(norm_g=(norm_g, m_norm_g, v_norm_g), b_ada=(b_ada, m_b_ada, v_b_ada), b_fgate=(b_fgate, m_b_fgate, v_b_fgate),
                 fox_qn_g=(fox_qn_g, m_fox_qn_g, v_fox_qn_g), fox_kn_g=(fox_kn_g, m_fox_kn_g, v_fox_kn_g),
                 gdn_A_log=(gdn_A_log, m_gdn_A_log, v_gdn_A_log), gdn_dt_bias=(gdn_dt_bias, m_gdn_dt_bias, v_gdn_dt_bias),
                 gdn_norm_g=(gdn_norm_g, m_gdn_norm_g, v_gdn_norm_g), final_g=(final_g, m_final_g, v_final_g))
    w_pack = _pack([given[n][0] for n in SMALL_NAMES], SMALL_PACK)
    m_pack = _pack([given[n][1] for n in SMALL_NAMES], SMALL_PACK)
    v_pack = _pack([given[n][2] for n in SMALL_NAMES], SMALL_PACK)
    packed = _adamw("adamw_small", w_pack, small_all, m_pack, v_pack, 1)
    off = 0
    for n, size in zip(SMALL_NAMES, SMALL_SIZES):
        outs[n] = tuple(t[0, off:off + size].reshape(given[n][0].shape) for t in packed)
        off += size
    for n in ("w_in", "w_out", "gdn_conv_w", "w_ada"):
        outs[n] = tuple(t[None] for t in outs[n])

    order = ("norm_g", "w_ada", "b_ada", "w_in", "b_fgate", "fox_qn_g", "fox_kn_g", "gdn_conv_w", "gdn_A_log",
             "gdn_dt_bias", "gdn_norm_g", "w_out", "final_g")
    result = [loss, grad_x.reshape(x.shape)]
    for part in range(4):
        result += [outs[n][part] for n in order]
    return tuple(result)
```

```python
import functools
import math

import jax
import jax.numpy as jnp
from jax import lax
from jax.experimental import pallas as pl
from jax.experimental.pallas import tpu as pltpu

f32 = jnp.float32
bf16 = jnp.bfloat16
HI = lax.Precision.HIGHEST

N_DEV = 8
AXES = ("x", "y", "c")
D_MODEL = 2048
HEADS = 8
HEAD_DIM = 128
WIDTH = HEADS * HEAD_DIM
CHUNK = 64
CONV_K = 4
EPS = 1e-6
QK_SCALE = HEAD_DIM ** -0.5
N_MAIN = 8 * WIDTH
N_SMALL = 128
IN_WIDTH = 8 * WIDTH + 3 * HEADS
LANES = 128
VMEM_LIMIT = 56 * 1024 * 1024

ADAM_LR, ADAM_B1, ADAM_B2, ADAM_EPS, ADAM_WD, ADAM_STEP = 0.001, 0.9, 0.999, 1e-08, 0.01, 10


def _params(*sem):
    return pltpu.CompilerParams(dimension_semantics=sem, vmem_limit_bytes=VMEM_LIMIT)


def _iota(shape, dim):
    return lax.broadcasted_iota(jnp.int32, shape, dim)


def _sigmoid(z):
    return 1.0 / (1.0 + jnp.exp(-z))


def _softplus_parts(z):
    t = jnp.log(1.0 + jnp.exp(-jnp.abs(z)))
    return jnp.minimum(z, 0.0) - t, jnp.maximum(z, 0.0) + t


def _dg(a, b, ca, cb, prec=None):
    return lax.dot_general(a, b, (((ca,), (cb,)), ((), ())), preferred_element_type=f32, precision=prec)


def _make_mm(cast, prec):
    def nn_(a, b):
        return _dg(cast(a), cast(b), 1, 0, prec)

    def nt_(a, b):
        return _dg(cast(a), cast(b), 1, 1, prec)

    def tn_(a, b):
        return _dg(cast(a), cast(b), 0, 0, prec)

    @jax.custom_vjp
    def nn(a, b):
        return nn_(a, b)

    @jax.custom_vjp
    def nt(a, b):
        return nt_(a, b)

    @jax.custom_vjp
    def tn(a, b):
        return tn_(a, b)

    nn.defvjp(lambda a, b: (nn_(a, b), (a, b)), lambda r, g: (nt_(g, r[1]), tn_(r[0], g)))
    nt.defvjp(lambda a, b: (nt_(a, b), (a, b)), lambda r, g: (nn_(g, r[1]), tn_(g, r[0])))
    tn.defvjp(lambda a, b: (tn_(a, b), (a, b)), lambda r, g: (nt_(r[1], g), nn_(r[0], g)))
    return (nn_, nt_, tn_), (nn, nt, tn)


_to_bf16 = lambda t: t.astype(bf16)
_keep = lambda t: t
_BF_PLAIN, _BF_VJP = _make_mm(_to_bf16, None)
_HI_PLAIN, _HI_VJP = _make_mm(_keep, HI)


def _inv_unit_lower_plain(m, hi_nn):
    n = -m
    eye = (_iota(m.shape, 0) == _iota(m.shape, 1)).astype(f32)
    t = eye + n
    p = n
    for _ in range(int(math.log2(CHUNK)) - 1):
        p = hi_nn(p, p)
        t = t + hi_nn(t, p)
    return t


@jax.custom_vjp
def _inv_given(m, t):
    return t


_inv_given.defvjp(lambda m, t: (t, t),
                  lambda t, g: (-_HI_PLAIN[1](_HI_PLAIN[2](t, g), t), jnp.zeros_like(t)))

SUBLANES = 8


def _gdn_intra(q, k, v, g_b, beta_b, t_known=None):
    diff = t_known is not None
    (_, bnt, _) = _BF_VJP if diff else _BF_PLAIN
    (hnn, hnt, _) = _HI_VJP if diff else _HI_PLAIN
    c = CHUNK
    r_i, c_i = _iota((c, c), 0), _iota((c, c), 1)
    lower, strict = r_i >= c_i, r_i > c_i
    gc_b = hnn(lower.astype(f32), g_b)
    gc_i = hnn(gc_b, (_iota((HEAD_DIM, c), 0) == 0).astype(f32))
    gc_j = hnt((_iota((c, HEAD_DIM), 1) == 0).astype(f32), gc_b)
    decay = jnp.where(lower, jnp.exp(jnp.where(lower, gc_i - gc_j, 0.0)), 0.0)
    kb = k * beta_b
    vb = v * beta_b
    m = jnp.where(strict, bnt(kb, k) * decay, 0.0)
    t = _inv_given(m, t_known) if diff else _inv_unit_lower_plain(m, hnn)
    eg = jnp.exp(gc_b)
    u = hnn(t, vb)
    w = hnn(t, kb * eg)
    attn = jnp.where(lower, bnt(q, k) * decay, 0.0)
    g_last = hnn(jnp.ones((c, c), f32), g_b)
    k_dec = k * jnp.exp(g_last - gc_b)
    eg_last = jnp.exp(hnn(jnp.ones((SUBLANES, c), f32), g_b))
    return u, w, q * eg, k_dec, attn, eg_last, t


def _scale_rows(s, eg_last):
    return (s.reshape(HEAD_DIM // SUBLANES, SUBLANES, HEAD_DIM) * eg_last[None]).reshape(HEAD_DIM, HEAD_DIM)


def _my_index():
    return 4 * lax.axis_index("x") + 2 * lax.axis_index("y") + lax.axis_index("c")


def _peer(d):
    x, y, c = lax.axis_index("x"), lax.axis_index("y"), lax.axis_index("c")
    px, py, pc = (x + (d >> 2)) % 2, (y + ((d >> 1) & 1)) % 2, (c + (d & 1)) % 2
    return (px, py, pc), 4 * px + 2 * py + pc


def _exchange(name, arrays, scatter):
    n = len(arrays)

    def body(*refs):
        srcs, dsts = refs[:n], refs[n:2 * n]
        send_sems, recv_sems, local_sems = refs[2 * n:]
        me = _my_index()

        def remote(k, d):
            peer, pidx = _peer(d)
            src = srcs[k].at[pidx] if scatter else srcs[k]
            return pltpu.make_async_remote_copy(
                src_ref=src, dst_ref=dsts[k].at[me], send_sem=send_sems.at[k * 7 + d - 1],
                recv_sem=recv_sems.at[k * 7 + d - 1], device_id=peer, device_id_type=pl.DeviceIdType.MESH)

        def arrival(k, d):
            peer, pidx = _peer(d)
            src = srcs[k].at[pidx] if scatter else srcs[k]
            return pltpu.make_async_remote_copy(
                src_ref=src, dst_ref=dsts[k].at[pidx], send_sem=send_sems.at[k * 7 + d - 1],
                recv_sem=recv_sems.at[k * 7 + d - 1], device_id=peer, device_id_type=pl.DeviceIdType.MESH)

        local = [pltpu.make_async_copy(srcs[k].at[me] if scatter else srcs[k], dsts[k].at[me], local_sems.at[k])
                 for k in range(n)]
        sends = [remote(k, d) for k in range(n) for d in range(1, N_DEV)]
        for cp in local + sends:
            cp.start()
        for k in range(n):
            for d in range(1, N_DEV):
                arrival(k, d).wait_recv()
        for cp in sends:
            cp.wait_send()
        for cp in local:
            cp.wait()

    if scatter:
        out_shape = [jax.ShapeDtypeStruct(a.shape, a.dtype) for a in arrays]
    else:
        out_shape = [jax.ShapeDtypeStruct((N_DEV,) + a.shape, a.dtype) for a in arrays]
    any_spec = pl.BlockSpec(memory_space=pl.ANY)
    return pl.pallas_call(
        body, name=name, out_shape=out_shape, in_specs=[any_spec] * n, out_specs=[any_spec] * n,
        scratch_shapes=[pltpu.SemaphoreType.DMA((7 * n,)), pltpu.SemaphoreType.DMA((7 * n,)),
                        pltpu.SemaphoreType.DMA((n,))],
        compiler_params=pltpu.CompilerParams(has_side_effects=True),
    )(*arrays)


def _mod_shard(c_all, w_ada, b_shard):
    def body(c_ref, w_ref, b_ref, o_ref):
        cv = c_ref[...]
        ca = cv * _sigmoid(cv)
        o_ref[...] = jnp.dot(ca.astype(bf16), w_ref[...].astype(bf16), preferred_element_type=f32) + b_ref[...]

    return pl.pallas_call(body, name="mod_shard", out_shape=jax.ShapeDtypeStruct((N_DEV, w_ada.shape[1]), f32),
                          compiler_params=_params())(c_all, w_ada, b_shard)


def _in_proj(x, norm_g, scale1p, shift, w_main, w_small):
    s_len = x.shape[0]
    tm, tn = 512, 1024

    def body(x_ref, g_ref, sc_ref, sh_ref, w_ref, ws_ref, p_ref, ps_ref, h_ref, h_sc):
        @pl.when(pl.program_id(1) == 0)
        def _():
            xb = x_ref[...]
            r = lax.rsqrt(jnp.mean(xb * xb, axis=-1, keepdims=True) + EPS)
            hb = ((xb * r * g_ref[...]) * sc_ref[...] + sh_ref[...]).astype(bf16)
            h_sc[...] = hb
            h_ref[...] = hb
            ps_ref[...] = jnp.dot(hb, ws_ref[...], preferred_element_type=f32)

        p_ref[...] = jnp.dot(h_sc[...], w_ref[...], preferred_element_type=f32)

    vec = pl.BlockSpec((1, D_MODEL), lambda i, j: (0, 0))
    return pl.pallas_call(
        body, name="in_proj", grid=(s_len // tm, N_MAIN // tn),
        in_specs=[pl.BlockSpec((tm, D_MODEL), lambda i, j: (i, 0)), vec, vec, vec,
                  pl.BlockSpec((D_MODEL, tn), lambda i, j: (0, j)),
                  pl.BlockSpec((D_MODEL, N_SMALL), lambda i, j: (0, 0))],
        out_specs=[pl.BlockSpec((tm, tn), lambda i, j: (i, j)),
                   pl.BlockSpec((tm, N_SMALL), lambda i, j: (i, 0)),
                   pl.BlockSpec((tm, D_MODEL), lambda i, j: (i, 0))],
        out_shape=[jax.ShapeDtypeStruct((s_len, N_MAIN), f32), jax.ShapeDtypeStruct((s_len, N_SMALL), f32),
                   jax.ShapeDtypeStruct((s_len, D_MODEL), bf16)],
        scratch_shapes=[pltpu.VMEM((tm, D_MODEL), bf16)],
        compiler_params=_params("parallel", "arbitrary"),
    )(x, norm_g, scale1p, shift, w_main, w_small)


PREP_TM = 256
HALO = 8


def _conv_section(xe_ref, cw_ref, cols, tm):
    acc = cw_ref[pl.ds(CONV_K - 1, 1), cols] * xe_ref[pl.ds(HALO, tm), :]
    for tap in range(CONV_K - 1):
        acc = acc + cw_ref[pl.ds(tap, 1), cols] * xe_ref[pl.ds(HALO - (CONV_K - 1) + tap, tm), :]
    return acc


def _small_fwd(ps, bvec, alog):
    z = ps + bvec
    logsig, softp = _softplus_parts(z)
    gval = -jnp.exp(alog) * softp
    beta = _sigmoid(ps)
    return z, logsig, gval, beta


def _lane_group_selector(first_lane):
    return (_iota((LANES, WIDTH), 0) == first_lane + _iota((LANES, WIDTH), 1) // HEAD_DIM).astype(f32)


def _prep(p_main, p_small, qn_g, kn_g, conv_w, bvec, alog):
    s_len = p_main.shape[0]
    tm = PREP_TM
    nb = s_len // tm

    def body(fq_ref, fk_ref, fv_ref, gq_ref, gk_ref, gv_ref, hq_ref, hk_ref, hv_ref, ps_ref, qg_ref, kg_ref,
             cw_ref, bv_ref, al_ref,
             qs_ref, kn_ref, vb_ref, gqo_ref, gko_ref, gvo_ref, small_ref, gb_ref, bb_ref, xe_sc, carry_sc):
        i = pl.program_id(0)

        @pl.when(i == 0)
        def _():
            carry_sc[...] = jnp.zeros_like(carry_sc)

        qg, kg = qg_ref[...], kg_ref[...]
        for h in range(HEADS):
            sl = slice(h * HEAD_DIM, (h + 1) * HEAD_DIM)
            q = fq_ref[:, sl]
            rq = lax.rsqrt(jnp.mean(q * q, axis=-1, keepdims=True) + EPS)
            qs_ref[:, sl] = (q * rq * qg * QK_SCALE).astype(bf16)
            k = fk_ref[:, sl]
            rk = lax.rsqrt(jnp.mean(k * k, axis=-1, keepdims=True) + EPS)
            kn_ref[:, sl] = (k * rk * kg).astype(bf16)
        vb_ref[...] = fv_ref[...].astype(bf16)

        first = i == 0
        for sec, (x_ref, halo_ref, o_ref) in enumerate(((gq_ref, hq_ref, gqo_ref), (gk_ref, hk_ref, gko_ref),
                                                        (gv_ref, hv_ref, gvo_ref))):
            xe_sc[0:HALO, :] = jnp.where(first, 0.0, halo_ref[...])
            xe_sc[HALO:, :] = x_ref[...]
            cv = _conv_section(xe_sc, cw_ref, slice(sec * WIDTH, (sec + 1) * WIDTH), tm)
            y = cv * _sigmoid(cv)
            if sec == 2:
                o_ref[...] = y
            else:
                mul = QK_SCALE if sec == 0 else 1.0
                for h in range(HEADS):
                    sl = slice(h * HEAD_DIM, (h + 1) * HEAD_DIM)
                    yh = y[:, sl]
                    o_ref[:, sl] = yh * (lax.rsqrt(jnp.sum(yh * yh, axis=-1, keepdims=True) + EPS) * mul)

        lane = _iota((tm, N_SMALL), 1)
        _, logsig, gval, beta = _small_fwd(ps_ref[...], bv_ref[...], al_ref[...])
        lf = jnp.where(lane < HEADS, logsig, 0.0)
        tri = (_iota((tm, tm), 0) >= _iota((tm, tm), 1)).astype(f32)
        fcum = jnp.dot(tri, lf, preferred_element_type=f32, precision=HI) + carry_sc[...]
        carry_sc[...] += jnp.sum(lf, axis=0, keepdims=True)
        small = jnp.where(lane < HEADS, fcum, jnp.where(lane < 2 * HEADS, gval, jnp.where(lane < 3 * HEADS, beta, 0.0)))
        small_ref[...] = small
        gb_ref[...] = jnp.dot(small, _lane_group_selector(HEADS), preferred_element_type=f32, precision=HI)
        bb_ref[...] = jnp.dot(small, _lane_group_selector(2 * HEADS), preferred_element_type=f32, precision=HI)

    def col(cb):
        return pl.BlockSpec((tm, WIDTH), lambda i: (i, cb))

    def halo(cb):
        return pl.BlockSpec((HALO, WIDTH), lambda i: (jnp.maximum(i * (tm // HALO) - 1, 0), cb))

    vec = pl.BlockSpec((1, LANES), lambda i: (0, 0))
    wide_f32 = jax.ShapeDtypeStruct((s_len, WIDTH), f32)
    wide_bf = jax.ShapeDtypeStruct((s_len, WIDTH), bf16)
    out_col = pl.BlockSpec((tm, WIDTH), lambda i: (i, 0))
    return pl.pallas_call(
        body, name="prep", grid=(nb,),
        in_specs=[col(0), col(1), col(2), col(4), col(5), col(6), halo(4), halo(5), halo(6),
                  pl.BlockSpec((tm, N_SMALL), lambda i: (i, 0)), vec, vec,
                  pl.BlockSpec((CONV_K, 3 * WIDTH), lambda i: (0, 0)), vec, vec],
        out_specs=[out_col] * 6 + [pl.BlockSpec((tm, N_SMALL), lambda i: (i, 0)), out_col, out_col],
        out_shape=[wide_bf, wide_bf, wide_bf, wide_f32, wide_f32, wide_f32,
                   jax.ShapeDtypeStruct((s_len, N_SMALL), f32), wide_f32, wide_f32],
        scratch_shapes=[pltpu.VMEM((tm + HALO, WIDTH), f32), pltpu.VMEM((1, N_SMALL), f32)],
        compiler_params=_params("arbitrary"),
    )(p_main, p_main, p_main, p_main, p_main, p_main, p_main, p_main, p_main, p_small, qn_g, kn_g, conv_w, bvec, alog)


FOX_T = 512
NEG_BIG = -1e30


def _fox_fwd(qs, kn, vb, f_col, f_row):
    s_len = qs.shape[0]
    t = FOX_T
    nq = s_len // t

    def body(q_ref, k_ref, v_ref, fc_ref, fr_ref, o_ref, lse_ref):
        qi = pl.program_id(1)
        q = q_ref[...]
        fq = fc_ref[0]
        causal = _iota((t, t), 0) >= _iota((t, t), 1)

        def step(j, carry, masked):
            m, l, acc = carry
            rows = pl.ds(pl.multiple_of(j * t, t), t)
            s = _dg(q, k_ref[rows, :], 1, 1) + (fq - fr_ref[0, j])
            if masked:
                s = jnp.where(causal, s, NEG_BIG)
            m_new = jnp.maximum(m, jnp.max(s, axis=-1, keepdims=True))
            p = jnp.exp(s - m_new)
            alpha = jnp.exp(m - m_new)
            l = alpha * l + jnp.sum(p, axis=-1, keepdims=True)
            acc = alpha * acc + jnp.dot(p.astype(bf16), v_ref[rows, :], preferred_element_type=f32)
            return m_new, l, acc

        init = (jnp.full((t, 1), NEG_BIG, f32), jnp.zeros((t, 1), f32), jnp.zeros((t, HEAD_DIM), f32))
        carry = lax.fori_loop(0, qi, lambda j, c: step(j, c, False), init)
        m, l, acc = step(qi, carry, True)
        o_ref[...] = acc / l
        lse_ref[0] = m + jnp.log(l)

    return pl.pallas_call(
        body, name="fox_fwd", grid=(HEADS, nq),
        in_specs=[pl.BlockSpec((t, HEAD_DIM), lambda h, i: (i, h)),
                  pl.BlockSpec((s_len, HEAD_DIM), lambda h, i: (0, h)),
                  pl.BlockSpec((s_len, HEAD_DIM), lambda h, i: (0, h)),
                  pl.BlockSpec((1, t, 1), lambda h, i: (h, i, 0)),
                  pl.BlockSpec((1, nq, 1, t), lambda h, i: (h, 0, 0, 0))],
        out_specs=[pl.BlockSpec((t, HEAD_DIM), lambda h, i: (i, h)),
                   pl.BlockSpec((1, t, 1), lambda h, i: (h, i, 0))],
        out_shape=[jax.ShapeDtypeStruct((s_len, WIDTH), f32), jax.ShapeDtypeStruct((HEADS, s_len, 1), f32)],
        compiler_params=_params("parallel", "arbitrary"),
    )(qs, kn, vb, f_col, f_row)


def _fox_bwd(qs, kn, vb, do, a_col, delta_col, f_row):
    s_len = qs.shape[0]
    t = FOX_T
    nq = s_len // t

    def body(q_ref, do_ref, a_ref, dl_ref, k_ref, v_ref, fr_ref, dq_ref, dk_ref, dv_ref, df_ref, dfq_ref):
        qi = pl.program_id(1)

        @pl.when(qi == 0)
        def _():
            dk_ref[...] = jnp.zeros_like(dk_ref)
            dv_ref[...] = jnp.zeros_like(dv_ref)
            df_ref[...] = jnp.zeros_like(df_ref)

        q, do_b = q_ref[...], do_ref[...]
        a, dl = a_ref[0], dl_ref[0]
        causal = _iota((t, t), 0) >= _iota((t, t), 1)

        def step(j, carry, masked):
            dq, row_sum = carry
            rows = pl.ds(pl.multiple_of(j * t, t), t)
            kj, vj = k_ref[rows, :], v_ref[rows, :]
            p = jnp.exp(_dg(q, kj, 1, 1) + (a - fr_ref[0, j]))
            if masked:
                p = jnp.where(causal, p, 0.0)
            ds = p * (_dg(do_b, vj, 1, 1) - dl)
            ds_b = ds.astype(bf16)
            dk_ref[rows, :] += _dg(ds_b, q, 0, 0)
            dv_ref[rows, :] += _dg(p.astype(bf16), do_b, 0, 0)
            df_ref[0, j] += -jnp.sum(ds, axis=0, keepdims=True)
            return dq + jnp.dot(ds_b, kj, preferred_element_type=f32), row_sum + jnp.sum(ds, axis=-1, keepdims=True)

        carry = lax.fori_loop(0, qi, lambda j, c: step(j, c, False),
                              (jnp.zeros((t, HEAD_DIM), f32), jnp.zeros((t, 1), f32)))
        dq, row_sum = step(qi, carry, True)
        dq_ref[...] = dq
        dfq_ref[0] = row_sum

    blk = pl.BlockSpec((t, HEAD_DIM), lambda h, i: (i, h))
    full = pl.BlockSpec((s_len, HEAD_DIM), lambda h, i: (0, h))
    colv = pl.BlockSpec((1, t, 1), lambda h, i: (h, i, 0))
    rowv = pl.BlockSpec((1, nq, 1, t), lambda h, i: (h, 0, 0, 0))
    wide = jax.ShapeDtypeStruct((s_len, WIDTH), f32)
    return pl.pallas_call(
        body, name="fox_bwd", grid=(HEADS, nq),
        in_specs=[blk, blk, colv, colv, full, full, rowv],
        out_specs=[blk, full, full, rowv, colv],
        out_shape=[wide, wide, wide, jax.ShapeDtypeStruct((HEADS, nq, 1, t), f32),
                   jax.ShapeDtypeStruct((HEADS, s_len, 1), f32)],
        compiler_params=_params("parallel", "arbitrary"),
    )(qs, do, a_col, delta_col, kn, vb, f_row)


INTRA_CHUNKS = 8
INTRA_INTERLEAVE = 4
SCAN_FWD_CHUNKS = 8
SCAN_BWD_CHUNKS = 4


def _gdn_intra_fwd(gq, gk, gv, g_b, beta_b):
    s_len = gq.shape[0]
    cpb = INTRA_CHUNKS
    rows_blk = cpb * CHUNK
    n_chunks = s_len // CHUNK

    def body(q_ref, k_ref, v_ref, g_ref, b_ref, u_ref, w_ref, qg_ref, kd_ref, attn_ref, t_ref, eg_ref):
        def group(it, _):
            for un in range(INTRA_INTERLEAVE):
                ci = it * INTRA_INTERLEAVE + un
                rows = pl.ds(pl.multiple_of(ci * CHUNK, CHUNK), CHUNK)
                u, w, qg, kd, attn, eg_last, t = _gdn_intra(q_ref[rows, :], k_ref[rows, :], v_ref[rows, :],
                                                            g_ref[rows, :], b_ref[rows, :])
                u_ref[rows, :] = u
                w_ref[rows, :] = w.astype(bf16)
                qg_ref[rows, :] = qg.astype(bf16)
                kd_ref[rows, :] = kd.astype(bf16)
                attn_ref[0, ci] = attn.astype(bf16)
                t_ref[0, ci] = t
                eg_ref[0, ci] = eg_last
            return 0

        lax.fori_loop(0, cpb // INTRA_INTERLEAVE, group, 0)

    blk = pl.BlockSpec((rows_blk, HEAD_DIM), lambda h, i: (i, h))
    sq = pl.BlockSpec((1, cpb, CHUNK, CHUNK), lambda h, i: (h, i, 0, 0))
    wide_bf = jax.ShapeDtypeStruct((s_len, WIDTH), bf16)
    return pl.pallas_call(
        body, name="gdn_intra_fwd", grid=(HEADS, s_len // rows_blk),
        in_specs=[blk] * 5,
        out_specs=[blk] * 4 + [sq, sq, pl.BlockSpec((1, cpb, SUBLANES, HEAD_DIM), lambda h, i: (h, i, 0, 0))],
        out_shape=[jax.ShapeDtypeStruct((s_len, WIDTH), f32), wide_bf, wide_bf, wide_bf,
                   jax.ShapeDtypeStruct((HEADS, n_chunks, CHUNK, CHUNK), bf16),
                   jax.ShapeDtypeStruct((HEADS, n_chunks, CHUNK, CHUNK), f32),
                   jax.ShapeDtypeStruct((HEADS, n_chunks, SUBLANES, HEAD_DIM), f32)],
        compiler_params=_params("parallel", "parallel"),
    )(gq, gk, gv, g_b, beta_b)


def _gdn_scan_fwd(u, w, qg, kd, attn, eg):
    s_len = u.shape[0]
    cpb = SCAN_FWD_CHUNKS
    rows_blk = cpb * CHUNK
    n_chunks = s_len // CHUNK

    def body(u_ref, w_ref, qg_ref, kd_ref, attn_ref, eg_ref, o_ref, st_ref, s_sc):
        @pl.when(pl.program_id(0) == 0)
        def _():
            s_sc[...] = jnp.zeros_like(s_sc)

        def chunk(ci, _):
            rows = pl.ds(pl.multiple_of(ci * CHUNK, CHUNK), CHUNK)
            for h in range(HEADS):
                cols = slice(h * HEAD_DIM, (h + 1) * HEAD_DIM)
                s0 = s_sc[h]
                st_ref[h, ci] = s0
                s0_b = s0.astype(bf16)
                v_new = u_ref[rows, cols] - jnp.dot(w_ref[rows, cols], s0_b, preferred_element_type=f32)
                vn_b = v_new.astype(bf16)
                o_ref[rows, cols] = (jnp.dot(qg_ref[rows, cols], s0_b, preferred_element_type=f32)
                                     + jnp.dot(attn_ref[h, ci], vn_b, preferred_element_type=f32))
                s_sc[h] = _scale_rows(s0, eg_ref[h, ci]) + _dg(kd_ref[rows, cols], vn_b, 0, 0)
            return 0

        lax.fori_loop(0, cpb, chunk, 0)

    row = pl.BlockSpec((rows_blk, WIDTH), lambda i: (i, 0))
    return pl.pallas_call(
        body, name="gdn_scan_fwd", grid=(s_len // rows_blk,),
        in_specs=[row] * 4 + [pl.BlockSpec((HEADS, cpb, CHUNK, CHUNK), lambda i: (0, i, 0, 0)),
                              pl.BlockSpec((HEADS, cpb, SUBLANES, HEAD_DIM), lambda i: (0, i, 0, 0))],
        out_specs=[row, pl.BlockSpec((HEADS, cpb, HEAD_DIM, HEAD_DIM), lambda i: (0, i, 0, 0))],
        out_shape=[jax.ShapeDtypeStruct((s_len, WIDTH), f32),
                   jax.ShapeDtypeStruct((HEADS, n_chunks, HEAD_DIM, HEAD_DIM), f32)],
        scratch_shapes=[pltpu.VMEM((HEADS, HEAD_DIM, HEAD_DIM), f32)],
        compiler_params=_params("arbitrary"),
    )(u, w, qg, kd, attn, eg)


def _gdn_scan_bwd(u, w, qg, kd, attn, eg, states, d_o):
    s_len = u.shape[0]
    cpb = SCAN_BWD_CHUNKS
    rows_blk = cpb * CHUNK
    n_chunks = s_len // CHUNK
    nb = s_len // rows_blk

    def body(u_ref, w_ref, qg_ref, kd_ref, attn_ref, eg_ref, st_ref, do_ref,
             du_ref, dw_ref, dqg_ref, dkd_ref, dattn_ref, deg_ref, ds_sc):
        @pl.when(pl.program_id(0) == 0)
        def _():
            ds_sc[...] = jnp.zeros_like(ds_sc)

        def chunk(step, _):
            ci = cpb - 1 - step
            rows = pl.ds(pl.multiple_of(ci * CHUNK, CHUNK), CHUNK)
            for h in range(HEADS):
                cols = slice(h * HEAD_DIM, (h + 1) * HEAD_DIM)
                s0 = st_ref[h, ci]
                s0_b = s0.astype(bf16)
                ds1 = ds_sc[h]
                ds1_b = ds1.astype(bf16)
                w_b, qg_b, kd_b, attn_b = w_ref[rows, cols], qg_ref[rows, cols], kd_ref[rows, cols], attn_ref[h, ci]
                do_b = do_ref[rows, cols].astype(bf16)
                vn_b = (u_ref[rows, cols] - jnp.dot(w_b, s0_b, preferred_element_type=f32)).astype(bf16)
                dvn = _dg(attn_b, do_b, 0, 0) + jnp.dot(kd_b, ds1_b, preferred_element_type=f32)
                dvn_b = dvn.astype(bf16)
                dattn_ref[h, ci] = _dg(do_b, vn_b, 1, 1)
                dqg_ref[rows, cols] = _dg(do_b, s0_b, 1, 1)
                dkd_ref[rows, cols] = _dg(vn_b, ds1_b, 1, 1)
                du_ref[rows, cols] = dvn
                dw_ref[rows, cols] = -_dg(dvn_b, s0_b, 1, 1)
                eg_last = eg_ref[h, ci]
                ds_sc[h] = _dg(qg_b, do_b, 0, 0) - _dg(w_b, dvn_b, 0, 0) + _scale_rows(ds1, eg_last)
                deg_ref[h, ci] = jnp.sum((ds1 * s0).reshape(HEAD_DIM // SUBLANES, SUBLANES, HEAD_DIM), axis=0)
            return 0

        lax.fori_loop(0, cpb, chunk, 0)

    row = pl.BlockSpec((rows_blk, WIDTH), lambda i: (nb - 1 - i, 0))
    sq = pl.BlockSpec((HEADS, cpb, CHUNK, CHUNK), lambda i: (0, nb - 1 - i, 0, 0))
    egs = pl.BlockSpec((HEADS, cpb, SUBLANES, HEAD_DIM), lambda i: (0, nb - 1 - i, 0, 0))
    wide = jax.ShapeDtypeStruct((s_len, WIDTH), f32)
    return pl.pallas_call(
        body, name="gdn_scan_bwd", grid=(nb,),
        in_specs=[row] * 4 + [sq, egs, pl.BlockSpec((HEADS, cpb, HEAD_DIM, HEAD_DIM), lambda i: (0, nb - 1 - i, 0, 0)), row],
        out_specs=[row] * 4 + [sq, egs],
        out_shape=[wide] * 4 + [jax.ShapeDtypeStruct((HEADS, n_chunks, CHUNK, CHUNK), f32),
                                jax.ShapeDtypeStruct((HEADS, n_chunks, SUBLANES, HEAD_DIM), f32)],
        scratch_shapes=[pltpu.VMEM((HEADS, HEAD_DIM, HEAD_DIM), f32)],
        compiler_params=_params("arbitrary"),
    )(u, w, qg, kd, attn, eg, states, d_o)


def _gdn_intra_bwd(gq, gk, gv, g_b, beta_b, t_inv, du, dw, dqg, dkd, dattn, deg):
    s_len = gq.shape[0]
    cpb = INTRA_CHUNKS
    rows_blk = cpb * CHUNK

    def body(q_ref, k_ref, v_ref, g_ref, b_ref, t_ref, du_ref, dw_ref, dqg_ref, dkd_ref, dattn_ref, deg_ref,
             dq_ref, dk_ref, dv_ref, dg_ref, db_ref):
        def group(it, _):
            for un in range(INTRA_INTERLEAVE):
                ci = it * INTRA_INTERLEAVE + un
                rows = pl.ds(pl.multiple_of(ci * CHUNK, CHUNK), CHUNK)
                t_known = t_ref[0, ci]
                _, vjp = jax.vjp(lambda q, k, v, g, b: _gdn_intra(q, k, v, g, b, t_known)[:6],
                                 q_ref[rows, :], k_ref[rows, :], v_ref[rows, :], g_ref[rows, :], b_ref[rows, :])
                dq, dk, dv, dg, db = vjp((du_ref[rows, :], dw_ref[rows, :], dqg_ref[rows, :], dkd_ref[rows, :],
                                          dattn_ref[0, ci], deg_ref[0, ci]))
                dq_ref[rows, :] = dq
                dk_ref[rows, :] = dk
                dv_ref[rows, :] = dv
                dg_ref[rows, :] = dg
                db_ref[rows, :] = db
            return 0

        lax.fori_loop(0, cpb // INTRA_INTERLEAVE, group, 0)

    blk = pl.BlockSpec((rows_blk, HEAD_DIM), lambda h, i: (i, h))
    sq = pl.BlockSpec((1, cpb, CHUNK, CHUNK), lambda h, i: (h, i, 0, 0))
    egs = pl.BlockSpec((1, cpb, SUBLANES, HEAD_DIM), lambda h, i: (h, i, 0, 0))
    wide = jax.ShapeDtypeStruct((s_len, WIDTH), f32)
    return pl.pallas_call(
        body, name="gdn_intra_bwd", grid=(HEADS, s_len // rows_blk),
        in_specs=[blk] * 5 + [sq] + [blk] * 4 + [sq, egs],
        out_specs=[blk] * 5,
        out_shape=[wide] * 5,
        compiler_params=_params("parallel", "parallel"),
    )(gq, gk, gv, g_b, beta_b, t_inv, du, dw, dqg, dkd, dattn, deg)


MIX_TM = 256


def _mix_fwd(fox_o, gdn_o, p_main, gnorm_g):
    s_len = fox_o.shape[0]
    tm = MIX_TM

    def body(fo_ref, go_ref, fz_ref, gz_ref, g_ref, mixed_ref):
        fz = fz_ref[...]
        mixed_ref[:, 0:WIDTH] = (fo_ref[...] * (fz * _sigmoid(fz))).astype(bf16)
        gz = gz_ref[...]
        gate = gz * _sigmoid(gz)
        gg = g_ref[...]
        for h in range(HEADS):
            sl = slice(h * HEAD_DIM, (h + 1) * HEAD_DIM)
            o = go_ref[:, sl]
            r = lax.rsqrt(jnp.mean(o * o, axis=-1, keepdims=True) + EPS)
            mixed_ref[:, WIDTH + h * HEAD_DIM:WIDTH + (h + 1) * HEAD_DIM] = (o * r * gg * gate[:, sl]).astype(bf16)

    row = pl.BlockSpec((tm, WIDTH), lambda i: (i, 0))
    return pl.pallas_call(
        body, name="mix_fwd", grid=(s_len // tm,),
        in_specs=[row, row, pl.BlockSpec((tm, WIDTH), lambda i: (i, 3)), pl.BlockSpec((tm, WIDTH), lambda i: (i, 7)),
                  pl.BlockSpec((1, LANES), lambda i: (0, 0))],
        out_specs=pl.BlockSpec((tm, 2 * WIDTH), lambda i: (i, 0)),
        out_shape=jax.ShapeDtypeStruct((s_len, 2 * WIDTH), bf16),
        compiler_params=_params("parallel"),
    )(fox_o, gdn_o, p_main, p_main, gnorm_g)


def _silu_grad(z):
    sg = _sigmoid(z)
    return sg * (1.0 + z * (1.0 - sg))


def _mix_bwd(dmixed, fox_o, gdn_o, p_main, gnorm_g):
    s_len = fox_o.shape[0]
    tm = MIX_TM

    def body(dm_ref, fo_ref, go_ref, fz_ref, gz_ref, g_ref, dof_ref, delta_ref, dfz_ref, dgz_ref, dgo_ref, dg_ref):
        @pl.when(pl.program_id(0) == 0)
        def _():
            dg_ref[...] = jnp.zeros_like(dg_ref)

        lane = _iota((tm, LANES), 1)
        fz = fz_ref[...]
        dmf = dm_ref[:, 0:WIDTH]
        fo = fo_ref[...]
        dof = dmf * (fz * _sigmoid(fz))
        dof_ref[...] = dof.astype(bf16)
        dfz_ref[...] = (dmf * fo * _silu_grad(fz)).astype(bf16)
        prod = dof * fo
        delta = jnp.zeros((tm, LANES), f32)
        for h in range(HEADS):
            dh = jnp.sum(prod[:, h * HEAD_DIM:(h + 1) * HEAD_DIM], axis=-1, keepdims=True)
            delta = jnp.where(lane == h, dh, delta)
        delta_ref[...] = delta

        gz = gz_ref[...]
        dmg = dm_ref[:, WIDTH:2 * WIDTH]
        gate = gz * _sigmoid(gz)
        sgrad = _silu_grad(gz)
        gg = g_ref[...]
        dg_acc = jnp.zeros((1, HEAD_DIM), f32)
        for h in range(HEADS):
            sl = slice(h * HEAD_DIM, (h + 1) * HEAD_DIM)
            o = go_ref[:, sl]
            r = lax.rsqrt(jnp.mean(o * o, axis=-1, keepdims=True) + EPS)
            on = o * r
            dmh = dmg[:, sl]
            dgz_ref[:, sl] = (dmh * (on * gg) * sgrad[:, sl]).astype(bf16)
            dy = dmh * gate[:, sl]
            dg_acc = dg_acc + jnp.sum(dy * on, axis=0, keepdims=True)
            tt = dy * gg
            dgo_ref[:, sl] = r * (tt - on * jnp.mean(tt * on, axis=-1, keepdims=True))
        dg_ref[...] += dg_acc

    row = pl.BlockSpec((tm, WIDTH), lambda i: (i, 0))
    wide_bf = jax.ShapeDtypeStruct((s_len, WIDTH), bf16)
    return pl.pallas_call(
        body, name="mix_bwd", grid=(s_len // tm,),
        in_specs=[pl.BlockSpec((tm, 2 * WIDTH), lambda i: (i, 0)), row, row,
                  pl.BlockSpec((tm, WIDTH), lambda i: (i, 3)), pl.BlockSpec((tm, WIDTH), lambda i: (i, 7)),
                  pl.BlockSpec((1, LANES), lambda i: (0, 0))],
        out_specs=[row, pl.BlockSpec((tm, LANES), lambda i: (i, 0)), row, row, row,
                   pl.BlockSpec((1, LANES), lambda i: (0, 0))],
        out_shape=[wide_bf, jax.ShapeDtypeStruct((s_len, LANES), f32), wide_bf, wide_bf,
                   jax.ShapeDtypeStruct((s_len, WIDTH), f32), jax.ShapeDtypeStruct((1, LANES), f32)],
        compiler_params=_params("arbitrary"),
    )(dmixed, fox_o, gdn_o, p_main, p_main, gnorm_g)


def _out_head(mixed, w_out, x, target, gate, final_g):
    s_len = x.shape[0]
    tm = 256

    def body(mx_ref, w_ref, x_ref, t_ref, gate_ref, fg_ref, loss_ref, dy_ref, dz_ref, dm_ref, dfg_ref, dgate_ref):
        @pl.when(pl.program_id(0) == 0)
        def _():
            loss_ref[...] = jnp.zeros_like(loss_ref)
            dfg_ref[...] = jnp.zeros_like(dfg_ref)
            dgate_ref[...] = jnp.zeros_like(dgate_ref)

        w = w_ref[...]
        z = jnp.dot(mx_ref[...], w, preferred_element_type=f32)
        gate_v, fg = gate_ref[...], fg_ref[...]
        y1 = x_ref[...] + gate_v * z
        r = lax.rsqrt(jnp.mean(y1 * y1, axis=-1, keepdims=True) + EPS)
        yn = y1 * r
        err = yn * fg - t_ref[...]
        loss_ref[...] += 0.5 * jnp.sum(jnp.mean(err * err, axis=-1, keepdims=True))
        dout = err * (1.0 / D_MODEL)
        dfg_ref[...] += jnp.sum(dout * yn, axis=0, keepdims=True)
        tt = dout * fg
        dy1 = r * (tt - yn * jnp.mean(tt * yn, axis=-1, keepdims=True))
        dy_ref[...] = dy1
        dgate_ref[...] += jnp.sum(dy1 * z, axis=0, keepdims=True)
        dz = (dy1 * gate_v).astype(bf16)
        dz_ref[...] = dz
        dm_ref[...] = _dg(dz, w, 1, 1)

    row = pl.BlockSpec((tm, D_MODEL), lambda i: (i, 0))
    vec = pl.BlockSpec((1, D_MODEL), lambda i: (0, 0))
    big = jax.ShapeDtypeStruct((s_len, D_MODEL), f32)
    return pl.pallas_call(
        body, name="out_head", grid=(s_len // tm,),
        in_specs=[row, pl.BlockSpec((D_MODEL, D_MODEL), lambda i: (0, 0)), row, row, vec, vec],
        out_specs=[pl.BlockSpec((1, LANES), lambda i: (0, 0)), row, row, row, vec, vec],
        out_shape=[jax.ShapeDtypeStruct((1, LANES), f32), big, jax.ShapeDtypeStruct((s_len, D_MODEL), bf16), big,
                   jax.ShapeDtypeStruct((1, D_MODEL), f32), jax.ShapeDtypeStruct((1, D_MODEL), f32)],
        compiler_params=_params("arbitrary"),
    )(mixed, w_out, x, target, gate, final_g)


def _matmul_tn(name, a, b):
    k_len, m_len = a.shape
    n_len = b.shape[1]
    tk, tm, tn = 512, 1024, min(1024, n_len)

    def body(a_ref, b_ref, o_ref):
        @pl.when(pl.program_id(2) == 0)
        def _():
            o_ref[...] = jnp.zeros_like(o_ref)

        o_ref[...] += _dg(a_ref[...], b_ref[...], 0, 0)

    return pl.pallas_call(
        body, name=name, grid=(m_len // tm, n_len // tn, k_len // tk),
        in_specs=[pl.BlockSpec((tk, tm), lambda i, j, k: (k, i)), pl.BlockSpec((tk, tn), lambda i, j, k: (k, j))],
        out_specs=pl.BlockSpec((tm, tn), lambda i, j, k: (i, j)),
        out_shape=jax.ShapeDtypeStruct((m_len, n_len), f32),
        compiler_params=_params("parallel", "parallel", "arbitrary"),
    )(a, b)


def _post1(p_main, p_small, qn_g, kn_g, conv_w, bvec, alog, dqs, dkn, dgq, dgk, dgv, dg_b, dbeta_b, df):
    s_len = p_main.shape[0]
    tm = PREP_TM
    nb = s_len // tm

    def body(fq_ref, fk_ref, gq_ref, gk_ref, gv_ref, hq_ref, hk_ref, hv_ref, ps_ref, qg_ref, kg_ref, cw_ref, bv_ref,
             al_ref, dqs_ref, dkn_ref, dgq_ref, dgk_ref, dgv_ref, dgb_ref, dbb_ref, df_ref,
             dfq_ref, dfk_ref, dconv_ref, dps_ref, dqg_ref, dkg_ref, sums_ref, xe_sc, carry_sc):
        step = pl.program_id(0)
        blk = nb - 1 - step

        @pl.when(step == 0)
        def _():
            carry_sc[...] = jnp.zeros_like(carry_sc)
            dqg_ref[...] = jnp.zeros_like(dqg_ref)
            dkg_ref[...] = jnp.zeros_like(dkg_ref)
            sums_ref[...] = jnp.zeros_like(sums_ref)

        for x_ref, g_ref, dy_ref, o_ref, acc_ref, mul in ((fq_ref, qg_ref, dqs_ref, dfq_ref, dqg_ref, QK_SCALE),
                                                          (fk_ref, kg_ref, dkn_ref, dfk_ref, dkg_ref, 1.0)):
            gain = g_ref[...]
            acc = jnp.zeros((1, HEAD_DIM), f32)
            for h in range(HEADS):
                sl = slice(h * HEAD_DIM, (h + 1) * HEAD_DIM)
                xv = x_ref[:, sl]
                r = lax.rsqrt(jnp.mean(xv * xv, axis=-1, keepdims=True) + EPS)
                xn = xv * r
                dy = dy_ref[:, sl] * mul
                acc = acc + jnp.sum(dy * xn, axis=0, keepdims=True)
                tt = dy * gain
                o_ref[:, sl] = (r * (tt - xn * jnp.mean(tt * xn, axis=-1, keepdims=True))).astype(bf16)
            acc_ref[...] += acc

        first = blk == 0
        for sec, (x_ref, halo_ref, dy_ref) in enumerate(((gq_ref, hq_ref, dgq_ref), (gk_ref, hk_ref, dgk_ref),
                                                         (gv_ref, hv_ref, dgv_ref))):
            xe_sc[0:HALO, :] = jnp.where(first, 0.0, halo_ref[...])
            xe_sc[HALO:, :] = x_ref[...]
            cv = _conv_section(xe_sc, cw_ref, slice(sec * WIDTH, (sec + 1) * WIDTH), tm)
            sgrad = _silu_grad(cv)
            if sec == 2:
                dconv_ref[:, sec * WIDTH:(sec + 1) * WIDTH] = dy_ref[...] * sgrad
            else:
                y = cv * _sigmoid(cv)
                mul = QK_SCALE if sec == 0 else 1.0
                for h in range(HEADS):
                    sl = slice(h * HEAD_DIM, (h + 1) * HEAD_DIM)
                    yh = y[:, sl]
                    r = lax.rsqrt(jnp.sum(yh * yh, axis=-1, keepdims=True) + EPS)
                    dqh = dy_ref[:, sl]
                    dyh = (mul * r) * (dqh - yh * (r * r) * jnp.sum(dqh * yh, axis=-1, keepdims=True))
                    dconv_ref[:, sec * WIDTH + h * HEAD_DIM:sec * WIDTH + (h + 1) * HEAD_DIM] = dyh * sgrad[:, sl]

        lane = _iota((tm, N_SMALL), 1)
        z, _, gval, beta = _small_fwd(ps_ref[...], bv_ref[...], al_ref[...])
        sig_z = _sigmoid(z)
        sel_t = (_iota((WIDTH, LANES), 1) == HEADS + _iota((WIDTH, LANES), 0) // HEAD_DIM).astype(f32)
        dg = jnp.dot(dgb_ref[...], sel_t, preferred_element_type=f32, precision=HI)
        sel_t2 = (_iota((WIDTH, LANES), 1) == 2 * HEADS + _iota((WIDTH, LANES), 0) // HEAD_DIM).astype(f32)
        dbeta = jnp.dot(dbb_ref[...], sel_t2, preferred_element_type=f32, precision=HI)
        dfb = jnp.where(lane < HEADS, df_ref[...], 0.0)
        tri_u = (_iota((tm, tm), 1) >= _iota((tm, tm), 0)).astype(f32)
        dlogf = jnp.dot(tri_u, dfb, preferred_element_type=f32, precision=HI) + carry_sc[...]
        carry_sc[...] += jnp.sum(dfb, axis=0, keepdims=True)
        dff = dlogf * (1.0 - sig_z)
        dga = dg * (-jnp.exp(al_ref[...])) * sig_z
        dgb_small = dbeta * beta * (1.0 - beta)
        dps = jnp.where(lane < HEADS, dff, jnp.where(lane < 2 * HEADS, dga, jnp.where(lane < 3 * HEADS, dgb_small, 0.0)))
        dps_ref[...] = dps.astype(bf16)
        row = _iota((8, N_SMALL), 0)
        s0 = jnp.sum(dps, axis=0, keepdims=True)
        s1 = jnp.sum(jnp.where((lane >= HEADS) & (lane < 2 * HEADS), dg * gval, 0.0), axis=0, keepdims=True)
        sums_ref[...] += jnp.where(row == 0, s0, jnp.where(row == 1, s1, 0.0))

    def col(cb):
        return pl.BlockSpec((tm, WIDTH), lambda i: (nb - 1 - i, cb))

    def halo(cb):
        return pl.BlockSpec((HALO, WIDTH), lambda i: (jnp.maximum((nb - 1 - i) * (tm // HALO) - 1, 0), cb))

    vec = pl.BlockSpec((1, LANES), lambda i: (0, 0))
    row0 = pl.BlockSpec((tm, WIDTH), lambda i: (nb - 1 - i, 0))
    small = pl.BlockSpec((tm, N_SMALL), lambda i: (nb - 1 - i, 0))
    wide_bf = jax.ShapeDtypeStruct((s_len, WIDTH), bf16)
    return pl.pallas_call(
        body, name="post1", grid=(nb,),
        in_specs=[col(0), col(1), col(4), col(5), col(6), halo(4), halo(5), halo(6), small, vec, vec,
                  pl.BlockSpec((CONV_K, 3 * WIDTH), lambda i: (0, 0)), vec, vec,
                  row0, row0, row0, row0, row0, row0, row0, small],
        out_specs=[row0, row0, pl.BlockSpec((tm, 3 * WIDTH), lambda i: (nb - 1 - i, 0)), small, vec, vec,
                   pl.BlockSpec((8, N_SMALL), lambda i: (0, 0))],
        out_shape=[wide_bf, wide_bf, jax.ShapeDtypeStruct((s_len, 3 * WIDTH), f32),
                   jax.ShapeDtypeStruct((s_len, N_SMALL), bf16), jax.ShapeDtypeStruct((1, LANES), f32),
                   jax.ShapeDtypeStruct((1, LANES), f32), jax.ShapeDtypeStruct((8, N_SMALL), f32)],
        scratch_shapes=[pltpu.VMEM((tm + HALO, WIDTH), f32), pltpu.VMEM((1, N_SMALL), f32)],
        compiler_params=_params("arbitrary"),
    )(p_main, p_main, p_main, p_main, p_main, p_main, p_main, p_main, p_small, qn_g, kn_g, conv_w, bvec, alog,
      dqs, dkn, dgq, dgk, dgv, dg_b, dbeta_b, df)


def _post2(p_main, dconv, conv_w):
    s_len = p_main.shape[0]
    tm = PREP_TM
    nb = s_len // tm

    def body(gq_ref, gk_ref, gv_ref, hq_ref, hk_ref, hv_ref, dc_ref, dnext_ref, cw_ref, dx_ref, dw_ref, xe_sc, de_sc):
        i = pl.program_id(0)

        @pl.when(i == 0)
        def _():
            dw_ref[...] = jnp.zeros_like(dw_ref)

        first, last = i == 0, i == nb - 1
        row = _iota((8, WIDTH), 0)
        for sec, (x_ref, halo_ref) in enumerate(((gq_ref, hq_ref), (gk_ref, hk_ref), (gv_ref, hv_ref))):
            cols = slice(sec * WIDTH, (sec + 1) * WIDTH)
            dc = dc_ref[:, cols]
            de_sc[0:tm, :] = dc
            de_sc[tm:, :] = jnp.where(last, 0.0, dnext_ref[:, cols])
            dx = cw_ref[pl.ds(CONV_K - 1, 1), cols] * dc
            for tap in range(CONV_K - 1):
                dx = dx + cw_ref[pl.ds(tap, 1), cols] * de_sc[pl.ds(CONV_K - 1 - tap, tm), :]
            dx_ref[:, cols] = dx.astype(bf16)
            xe_sc[0:HALO, :] = jnp.where(first, 0.0, halo_ref[...])
            xe_sc[HALO:, :] = x_ref[...]
            dw = jnp.zeros((8, WIDTH), f32)
            for tap in range(CONV_K):
                contrib = jnp.sum(dc * xe_sc[pl.ds(HALO - (CONV_K - 1) + tap, tm), :], axis=0, keepdims=True)
                dw = jnp.where(row == tap, contrib, dw)
            dw_ref[:, cols] += dw

    def col(cb):
        return pl.BlockSpec((tm, WIDTH), lambda i: (i, cb))

    def halo(cb):
        return pl.BlockSpec((HALO, WIDTH), lambda i: (jnp.maximum(i * (tm // HALO) - 1, 0), cb))

    return pl.pallas_call(
        body, name="post2", grid=(nb,),
        in_specs=[col(4), col(5), col(6), halo(4), halo(5), halo(6),
                  pl.BlockSpec((tm, 3 * WIDTH), lambda i: (i, 0)),
                  pl.BlockSpec((HALO, 3 * WIDTH), lambda i: (jnp.minimum((i + 1) * (tm // HALO), s_len // HALO - 1), 0)),
                  pl.BlockSpec((CONV_K, 3 * WIDTH), lambda i: (0, 0))],
        out_specs=[pl.BlockSpec((tm, 3 * WIDTH), lambda i: (i, 0)), pl.BlockSpec((8, 3 * WIDTH), lambda i: (0, 0))],
        out_shape=[jax.ShapeDtypeStruct((s_len, 3 * WIDTH), bf16), jax.ShapeDtypeStruct((8, 3 * WIDTH), f32)],
        scratch_shapes=[pltpu.VMEM((tm + HALO, WIDTH), f32), pltpu.VMEM((tm + HALO, WIDTH), f32)],
        compiler_params=_params("arbitrary"),
    )(p_main, p_main, p_main, p_main, p_main, p_main, dconv, dconv, conv_w)


def _in_proj_bwd(dp_main, dp_small, w_main, w_small, x, dy1, norm_g, scale1p):
    s_len = x.shape[0]
    tm, tk = 512, 1024
    nk = N_MAIN // tk

    def body(dp_ref, dps_ref, w_ref, ws_ref, x_ref, dy_ref, g_ref, sc_ref, dx_ref, dsh_ref, dsc_ref, dg_ref, acc_sc):
        i, k = pl.program_id(0), pl.program_id(1)

        @pl.when((i == 0) & (k == 0))
        def _():
            dsh_ref[...] = jnp.zeros_like(dsh_ref)
            dsc_ref[...] = jnp.zeros_like(dsc_ref)
            dg_ref[...] = jnp.zeros_like(dg_ref)

        @pl.when(k == 0)
        def _():
            acc_sc[...] = _dg(dps_ref[...], ws_ref[...], 1, 1)

        acc_sc[...] += _dg(dp_ref[...], w_ref[...], 1, 1)

        @pl.when(k == nk - 1)
        def _():
            dh = acc_sc[...]
            xb = x_ref[...]
            r = lax.rsqrt(jnp.mean(xb * xb, axis=-1, keepdims=True) + EPS)
            xr = xb * r
            gain = g_ref[...]
            dsh_ref[...] += jnp.sum(dh, axis=0, keepdims=True)
            dsc_ref[...] += jnp.sum(dh * (xr * gain), axis=0, keepdims=True)
            dxn = dh * sc_ref[...]
            dg_ref[...] += jnp.sum(dxn * xr, axis=0, keepdims=True)
            tt = dxn * gain
            dx_ref[...] = r * (tt - xr * jnp.mean(tt * xr, axis=-1, keepdims=True)) + dy_ref[...]

    row = pl.BlockSpec((tm, D_MODEL), lambda i, k: (i, 0))
    vec = pl.BlockSpec((1, D_MODEL), lambda i, k: (0, 0))
    vshape = jax.ShapeDtypeStruct((1, D_MODEL), f32)
    return pl.pallas_call(
        body, name="in_proj_bwd", grid=(s_len // tm, nk),
        in_specs=[pl.BlockSpec((tm, tk), lambda i, k: (i, k)), pl.BlockSpec((tm, N_SMALL), lambda i, k: (i, 0)),
                  pl.BlockSpec((D_MODEL, tk), lambda i, k: (0, k)), pl.BlockSpec((D_MODEL, N_SMALL), lambda i, k: (0, 0)),
                  row, row, vec, vec],
        out_specs=[row, vec, vec, vec],
        out_shape=[jax.ShapeDtypeStruct((s_len, D_MODEL), f32), vshape, vshape, vshape],
        scratch_shapes=[pltpu.VMEM((tm, D_MODEL), f32)],
        compiler_params=_params("arbitrary", "arbitrary"),
    )(dp_main, dp_small, w_main, w_small, x, dy1, norm_g, scale1p)


def _adamw(name, w, g_stack, m, v, tr):
    n_stack, rows, cols = g_stack.shape

    def body(w_ref, g_ref, m_ref, v_ref, go_ref, d_ref, mo_ref, vo_ref):
        g = g_ref[0]
        for k in range(1, n_stack):
            g = g + g_ref[k]
        go_ref[...] = g
        m_new = ADAM_B1 * m_ref[...] + (1.0 - ADAM_B1) * g
        v_new = ADAM_B2 * v_ref[...] + (1.0 - ADAM_B2) * (g * g)
        mo_ref[...] = m_new
        vo_ref[...] = v_new
        m_hat = m_new / (1.0 - ADAM_B1 ** ADAM_STEP)
        v_hat = v_new / (1.0 - ADAM_B2 ** ADAM_STEP)
        d_ref[...] = -ADAM_LR * (m_hat / (jnp.sqrt(v_hat) + ADAM_EPS) + ADAM_WD * w_ref[...])

    blk = pl.BlockSpec((tr, cols), lambda i: (i, 0))
    shape = jax.ShapeDtypeStruct((rows, cols), f32)
    return pl.pallas_call(
        body, name=name, grid=(rows // tr,),
        in_specs=[blk, pl.BlockSpec((n_stack, tr, cols), lambda i: (0, i, 0)), blk, blk],
        out_specs=[blk] * 4, out_shape=[shape] * 4,
        compiler_params=_params("parallel"),
    )(w, g_stack, m, v)


def _w_ada_grad(c_all_t, dmod_pad):
    def body(c_ref, d_ref, o_ref):
        cv = c_ref[...]
        o_ref[...] = jnp.dot(cv * _sigmoid(cv), d_ref[...], preferred_element_type=f32, precision=HI)

    return pl.pallas_call(body, name="w_ada_grad",
                          out_shape=jax.ShapeDtypeStruct((c_all_t.shape[0], dmod_pad.shape[1]), f32),
                          compiler_params=_params())(c_all_t, dmod_pad)


SMALL_NAMES = ("norm_g", "b_ada", "b_fgate", "fox_qn_g", "fox_kn_g", "gdn_A_log", "gdn_dt_bias", "gdn_norm_g", "final_g")
SMALL_SIZES = (D_MODEL, 3 * D_MODEL, HEADS, HEAD_DIM, HEAD_DIM, HEADS, HEADS, HEAD_DIM, D_MODEL)
SMALL_PACK = 10752


def _pack(vectors, total):
    flat = jnp.concatenate([t.reshape(-1) for t in vectors])
    return jnp.pad(flat, (0, total - flat.shape[0])).reshape(1, total)


def _lanes(*pieces):
    row = jnp.zeros((LANES,), f32)
    for off, vec in pieces:
        row = lax.dynamic_update_slice(row, vec.reshape(-1).astype(f32), (off,))
    return row.reshape(1, LANES)


def kernel(x, c, norm_g, w_ada, b_ada, w_in, b_fgate, fox_qn_g, fox_kn_g, gdn_conv_w, gdn_A_log, gdn_dt_bias, gdn_norm_g, w_out, final_g, loss_target, m_norm_g, m_w_ada, m_b_ada, m_w_in, m_b_fgate, m_fox_qn_g, m_fox_kn_g, m_gdn_conv_w, m_gdn_A_log, m_gdn_dt_bias, m_gdn_norm_g, m_w_out, m_final_g, v_norm_g, v_w_ada, v_b_ada, v_w_in, v_b_fgate, v_fox_qn_g, v_fox_kn_g, v_gdn_conv_w, v_gdn_A_log, v_gdn_dt_bias, v_gdn_norm_g, v_w_out, v_final_g):
    me = _my_index()
    s_len = x.shape[1]
    nq = s_len // FOX_T
    x2 = x.reshape(s_len, D_MODEL)
    tgt = loss_target.reshape(s_len, D_MODEL)
    ada_cols = w_ada.shape[2]
    in_cols = w_in.shape[2]
    conv_cols = gdn_conv_w.shape[2]

    (c_all,) = _exchange("gather_c", [c], scatter=False)
    c_all = c_all.reshape(N_DEV, D_MODEL)
    b_shard = lax.dynamic_slice(b_ada, (0, me * ada_cols), (1, ada_cols))
    mod_mine = _mod_shard(c_all, w_ada[0], b_shard)
    mod_all, w_in_all, w_out_all, conv_all = _exchange(
        "gather_weights", [mod_mine, w_in[0].astype(bf16), w_out[0].astype(bf16), gdn_conv_w[0]], scatter=False)
    mod = lax.dynamic_slice(mod_all, (0, me, 0), (N_DEV, 1, ada_cols)).reshape(1, 3 * D_MODEL)
    shift, scale, gate = mod[:, :D_MODEL], mod[:, D_MODEL:2 * D_MODEL], mod[:, 2 * D_MODEL:]
    scale1p = 1.0 + scale
    w_in_full = jnp.transpose(w_in_all, (1, 0, 2)).reshape(D_MODEL, N_DEV * in_cols)
    g0 = 4 * WIDTH + HEADS
    w_main = jnp.concatenate([w_in_full[:, :4 * WIDTH], w_in_full[:, g0:g0 + 4 * WIDTH]], axis=1)
    w_small = jnp.concatenate([w_in_full[:, 4 * WIDTH:g0], w_in_full[:, g0 + 4 * WIDTH:],
                               jnp.zeros((D_MODEL, N_SMALL - 3 * HEADS), bf16)], axis=1)
    w_out_full = w_out_all.reshape(2 * WIDTH, D_MODEL)
    conv_full = jnp.transpose(conv_all, (1, 0, 2)).reshape(CONV_K, 3 * WIDTH)

    qn_g, kn_g, gn_g = fox_qn_g.reshape(1, LANES), fox_kn_g.reshape(1, LANES), gdn_norm_g.reshape(1, LANES)
    bvec = _lanes((0, b_fgate), (HEADS, gdn_dt_bias))
    alog = _lanes((HEADS, gdn_A_log))
    fg = final_g.reshape(1, D_MODEL)

    p_main, p_small, h_bf = _in_proj(x2, norm_g, scale1p, shift, w_main, w_small)
    qs, kn, vb, gq, gk, gv, small, g_b, beta_b = _prep(p_main, p_small, qn_g, kn_g, conv_full, bvec, alog)
    f_heads = jnp.transpose(small[:, :HEADS])
    f_col = f_heads.reshape(HEADS, s_len, 1)
    f_row = f_heads.reshape(HEADS, nq, 1, FOX_T)
    fox_o, lse = _fox_fwd(qs, kn, vb, f_col, f_row)
    gu, gw, gqg, gkd, gattn, t_inv, eg_last = _gdn_intra_fwd(gq, gk, gv, g_b, beta_b)
    gdn_o, states = _gdn_scan_fwd(gu, gw, gqg, gkd, gattn, eg_last)
    mixed = _mix_fwd(fox_o, gdn_o, p_main, gn_g)

    loss_row, dy1, dz, dmixed, d_final_g, d_gate = _out_head(mixed, w_out_full, x2, tgt, gate, fg)
    loss = lax.psum(loss_row[0, 0], AXES)
    dw_out = _matmul_tn("dw_out", mixed, dz)
    do_fox, delta, dfz, dgz, dgdn_o, d_gn_g = _mix_bwd(dmixed, fox_o, gdn_o, p_main, gn_g)
    delta_col = jnp.transpose(delta[:, :HEADS]).reshape(HEADS, s_len, 1)
    dqs, dkn, dvf, df_key, df_query = _fox_bwd(qs, kn, vb, do_fox, f_col - lse, delta_col, f_row)
    du, dw, dqg, dkd, dattn, deg = _gdn_scan_bwd(gu, gw, gqg, gkd, gattn, eg_last, states, dgdn_o)
    dgq, dgk, dgv, dg_b, dbeta_b = _gdn_intra_bwd(gq, gk, gv, g_b, beta_b, t_inv, du, dw, dqg, dkd, dattn, deg)
    df_heads = df_key.reshape(HEADS, s_len) + df_query.reshape(HEADS, s_len)
    df_small = jnp.pad(jnp.transpose(df_heads), ((0, 0), (0, N_SMALL - HEADS)))
    dfq, dfk, dconv, dp_small, d_qn_g, d_kn_g, sums = _post1(
        p_main, p_small, qn_g, kn_g, conv_full, bvec, alog, dqs, dkn, dgq, dgk, dgv, dg_b, dbeta_b, df_small)
    dgqkv, d_conv = _post2(p_main, dconv, conv_full)
    dp_main = jnp.concatenate([dfq, dfk, dvf.astype(bf16), dfz, dgqkv, dgz], axis=1)
    grad_x, d_shift, d_scale, d_norm_g = _in_proj_bwd(dp_main, dp_small, w_main, w_small, x2, dy1, norm_g, scale1p)
    dw_main = _matmul_tn("dw_main", h_bf, dp_main)
    dw_small = _matmul_tn("dw_small", h_bf, dp_small)
    dw_in_full = jnp.concatenate([dw_main[:, :4 * WIDTH], dw_small[:, :HEADS], dw_main[:, 4 * WIDTH:],
                                  dw_small[:, HEADS:3 * HEADS]], axis=1)
    dw_in_parts = jnp.transpose(dw_in_full.reshape(D_MODEL, N_DEV, in_cols), (1, 0, 2))
    dw_out_parts = dw_out.reshape(N_DEV, w_out.shape[1], D_MODEL)

    dmod = jnp.concatenate([d_shift, d_scale, d_gate], axis=1)
    small_grads = _pack([d_norm_g, dmod, sums[0, :HEADS], d_qn_g, d_kn_g, sums[1, HEADS:2 * HEADS],
                         sums[0, HEADS:2 * HEADS], d_gn_g, d_final_g], SMALL_PACK)
    conv_grad = d_conv[:CONV_K]
    dw_in_recv, dw_out_recv = _exchange("scatter_grads", [dw_in_parts, dw_out_parts], scatter=True)
    small_all, conv_all_g = _exchange("gather_small_grads", [small_grads, conv_grad], scatter=False)

    outs = {}
    outs["w_in"] = _adamw("adamw_w_in", w_in[0], dw_in_recv, m_w_in[0], v_w_in[0], 128)
    outs["w_out"] = _adamw("adamw_w_out", w_out[0], dw_out_recv, m_w_out[0], v_w_out[0], 128)
    conv_mine = lax.dynamic_slice(jnp.transpose(conv_all_g.reshape(N_DEV, CONV_K, N_DEV, conv_cols), (0, 2, 1, 3)),
                                  (0, me, 0, 0), (N_DEV, 1, CONV_K, conv_cols)).reshape(N_DEV, CONV_K, conv_cols)
    outs["gdn_conv_w"] = _adamw("adamw_conv", gdn_conv_w[0], conv_mine, m_gdn_conv_w[0], v_gdn_conv_w[0], CONV_K)
    small_all = small_all.reshape(N_DEV, 1, SMALL_PACK)
    dmod_all = small_all[:, 0, D_MODEL:D_MODEL + 3 * D_MODEL]
    dmod_mine = lax.dynamic_slice(dmod_all, (0, me * ada_cols), (N_DEV, ada_cols))
    c_all_t = jnp.pad(jnp.transpose(c_all), ((0, 0), (0, LANES - N_DEV)))
    g_w_ada = _w_ada_grad(c_all_t, jnp.pad(dmod_mine, ((0, LANES - N_DEV), (0, 0))))
    outs["w_ada"] = _adamw("adamw_w_ada", w_ada[0], g_w_ada[None], m_w_ada[0], v_w_ada[0], 256)
    given = dict(norm_g=(norm_g, m_norm_g, v_norm_g), b_ada=(b_ada, m_b_ada, v_b_ada), b_fgate=(b_fgate, m_b_fgate, v_b_fgate),
                 fox_qn_g=(fox_qn_g, m_fox_qn_g, v_fox_qn_g), fox_kn_g=(fox_kn_g, m_fox_kn_g, v_fox_kn_g),
                 gdn_A_log=(gdn_A_log, m_gdn_A_log, v_gdn_A_log), gdn_dt_bias=(gdn_dt_bias, m_gdn_dt_bias, v_gdn_dt_bias),
                 gdn_norm_g=(gdn_norm_g, m_gdn_norm_g, v_gdn_norm_g), final_g=(final_g, m_final_g, v_final_g))
    w_pack = _pack([given[n][0] for n in SMALL_NAMES], SMALL_PACK)
    m_pack = _pack([given[n][1] for n in SMALL_NAMES], SMALL_PACK)
    v_pack = _pack([given[n][2] for n in SMALL_NAMES], SMALL_PACK)
    packed = _adamw("adamw_small", w_pack, small_all, m_pack, v_pack, 1)
    off = 0
    for n, size in zip(SMALL_NAMES, SMALL_SIZES):
        outs[n] = tuple(t[0, off:off + size].reshape(given[n][0].shape) for t in packed)
        off += size
    for n in ("w_in", "w_out", "gdn_conv_w", "w_ada"):
        outs[n] = tuple(t[None] for t in outs[n])

    order = ("norm_g", "w_ada", "b_ada", "w_in", "b_fgate", "fox_qn_g", "fox_kn_g", "gdn_conv_w", "gdn_A_log",
             "gdn_dt_bias", "gdn_norm_g", "w_out", "final_g")
    result = [loss, grad_x.reshape(x.shape)]
    for part in range(4):
        result += [outs[n][part] for n in order]
    return tuple(result)
```

```python
import math

import jax
import jax.numpy as jnp
from jax import lax
from jax.experimental import pallas as pl
from jax.experimental.pallas import tpu as pltpu

f32 = jnp.float32
bf16 = jnp.bfloat16
HI = lax.Precision.HIGHEST

N_DEV = 8
AXES = ("x", "y", "c")
D_MODEL = 2048
HEADS = 8
HEAD_DIM = 128
WIDTH = HEADS * HEAD_DIM
CHUNK = 64
CONV_K = 4
EPS = 1e-6
QK_SCALE = HEAD_DIM ** -0.5
N_MAIN = 8 * WIDTH
N_SMALL = 128
IN_WIDTH = 8 * WIDTH + 3 * HEADS
LANES = 128
VMEM_LIMIT = 56 * 1024 * 1024

ADAM_LR, ADAM_B1, ADAM_B2, ADAM_EPS, ADAM_WD, ADAM_STEP = 0.001, 0.9, 0.999, 1e-08, 0.01, 10


def _params(*sem):
    return pltpu.CompilerParams(dimension_semantics=sem, vmem_limit_bytes=VMEM_LIMIT)


def _iota(shape, dim):
    return lax.broadcasted_iota(jnp.int32, shape, dim)


def _sigmoid(z):
    return 1.0 / (1.0 + jnp.exp(-z))


def _softplus_parts(z):
    t = jnp.log(1.0 + jnp.exp(-jnp.abs(z)))
    return jnp.minimum(z, 0.0) - t, jnp.maximum(z, 0.0) + t


def _dg(a, b, ca, cb, prec=None):
    return lax.dot_general(a, b, (((ca,), (cb,)), ((), ())), preferred_element_type=f32, precision=prec)


def _dot_bf16(a, b, ca, cb):
    return _dg(a.astype(bf16), b.astype(bf16), ca, cb)


def _split_bf16(a):
    hi = a.astype(bf16)
    return hi, (a - hi.astype(f32)).astype(bf16)


def _dot_3pass(a, b, ca, cb):
    a_hi, a_lo = _split_bf16(a)
    b_hi, b_lo = _split_bf16(b)
    return _dg(a_hi, b_hi, ca, cb) + (_dg(a_hi, b_lo, ca, cb) + _dg(a_lo, b_hi, ca, cb))


def _make_mm(dot):
    def nn_(a, b):
        return dot(a, b, 1, 0)

    def nt_(a, b):
        return dot(a, b, 1, 1)

    def tn_(a, b):
        return dot(a, b, 0, 0)

    @jax.custom_vjp
    def nn(a, b):
        return nn_(a, b)

    @jax.custom_vjp
    def nt(a, b):
        return nt_(a, b)

    @jax.custom_vjp
    def tn(a, b):
        return tn_(a, b)

    nn.defvjp(lambda a, b: (nn_(a, b), (a, b)), lambda r, g: (nt_(g, r[1]), tn_(r[0], g)))
    nt.defvjp(lambda a, b: (nt_(a, b), (a, b)), lambda r, g: (nn_(g, r[1]), tn_(g, r[0])))
    tn.defvjp(lambda a, b: (tn_(a, b), (a, b)), lambda r, g: (nt_(r[1], g), nn_(r[0], g)))
    return (nn_, nt_, tn_), (nn, nt, tn)


_BF_PLAIN, _BF_VJP = _make_mm(_dot_bf16)
_X3_PLAIN, _X3_VJP = _make_mm(_dot_3pass)


def _inv_unit_lower_many(ms):
    c = CHUNK
    nn = _X3_PLAIN[0]
    eye = (_iota((c, c), 0) == _iota((c, c), 1)).astype(f32)
    top = _iota((2 * c, c), 0) < c
    xs = [jnp.concatenate([eye - m, nn(m, m)], axis=0) for m in ms]
    for _ in range(int(math.log2(CHUNK)) - 2):
        xs = [jnp.where(top, x, 0.0) + nn(x, x[c:]) for x in xs]
    return [x[:c] + nn(x[:c], x[c:]) for x in xs]


@jax.custom_vjp
def _inv_given(m, t):
    return t


_inv_given.defvjp(lambda m, t: (t, t),
                  lambda t, g: (-_X3_PLAIN[1](_X3_PLAIN[2](t, g), t), jnp.zeros_like(t)))

SUBLANES = 8


def _gdn_intra_pre(q, k, v, gc_b, g_last_b, beta_b, bnt):
    c = CHUNK
    r_i, c_i = _iota((c, c), 0), _iota((c, c), 1)
    lower, strict = r_i >= c_i, r_i > c_i
    gc_i = gc_b[:, :c]
    gc_j = gc_i.T
    decay = jnp.where(lower, jnp.exp(jnp.where(lower, gc_i - gc_j, 0.0)), 0.0)
    kb = k * beta_b
    both = bnt(jnp.concatenate([kb, q], axis=0), k)
    m = jnp.where(strict, both[:c] * decay, 0.0)
    attn = jnp.where(lower, both[c:] * decay, 0.0)
    eg = jnp.exp(gc_b)
    rhs = jnp.concatenate([v * beta_b, kb * eg], axis=1)
    k_dec = k * jnp.exp(g_last_b - gc_b)
    eg_last = jnp.exp(g_last_b[:SUBLANES])
    return m, rhs, q * eg, k_dec, attn, eg_last


def _gdn_intra(q, k, v, gc_b, g_last_b, beta_b, t_known):
    m, rhs, qg, k_dec, attn, eg_last = _gdn_intra_pre(q, k, v, gc_b, g_last_b, beta_b, _BF_VJP[1])
    return _X3_VJP[0](_inv_given(m, t_known), rhs), qg, k_dec, attn, eg_last


def _scale_rows(s, eg_last):
    return (s.reshape(HEAD_DIM // SUBLANES, SUBLANES, HEAD_DIM) * eg_last[None]).reshape(HEAD_DIM, HEAD_DIM)


def _my_index():
    return 4 * lax.axis_index("x") + 2 * lax.axis_index("y") + lax.axis_index("c")


def _peer(d):
    x, y, c = lax.axis_index("x"), lax.axis_index("y"), lax.axis_index("c")
    px, py, pc = (x + (d >> 2)) % 2, (y + ((d >> 1) & 1)) % 2, (c + (d & 1)) % 2
    return (px, py, pc), 4 * px + 2 * py + pc


def _exchange(name, arrays, scatter):
    n = len(arrays)

    def body(*refs):
        srcs, dsts = refs[:n], refs[n:2 * n]
        send_sems, recv_sems, local_sems = refs[2 * n:]
        me = _my_index()

        def remote(k, d):
            peer, pidx = _peer(d)
            src = srcs[k].at[pidx] if scatter else srcs[k]
            return pltpu.make_async_remote_copy(
                src_ref=src, dst_ref=dsts[k].at[me], send_sem=send_sems.at[k * 7 + d - 1],
                recv_sem=recv_sems.at[k * 7 + d - 1], device_id=peer, device_id_type=pl.DeviceIdType.MESH)

        def arrival(k, d):
            peer, pidx = _peer(d)
            src = srcs[k].at[pidx] if scatter else srcs[k]
            return pltpu.make_async_remote_copy(
                src_ref=src, dst_ref=dsts[k].at[pidx], send_sem=send_sems.at[k * 7 + d - 1],
                recv_sem=recv_sems.at[k * 7 + d - 1], device_id=peer, device_id_type=pl.DeviceIdType.MESH)

        local = [pltpu.make_async_copy(srcs[k].at[me] if scatter else srcs[k], dsts[k].at[me], local_sems.at[k])
                 for k in range(n)]
        sends = [remote(k, d) for k in range(n) for d in range(1, N_DEV)]
        for cp in local + sends:
            cp.start()
        for k in range(n):
            for d in range(1, N_DEV):
                arrival(k, d).wait_recv()
        for cp in sends:
            cp.wait_send()
        for cp in local:
            cp.wait()

    if scatter:
        out_shape = [jax.ShapeDtypeStruct(a.shape, a.dtype) for a in arrays]
    else:
        out_shape = [jax.ShapeDtypeStruct((N_DEV,) + a.shape, a.dtype) for a in arrays]
    any_spec = pl.BlockSpec(memory_space=pl.ANY)
    return pl.pallas_call(
        body, name=name, out_shape=out_shape, in_specs=[any_spec] * n, out_specs=[any_spec] * n,
        scratch_shapes=[pltpu.SemaphoreType.DMA((7 * n,)), pltpu.SemaphoreType.DMA((7 * n,)),
                        pltpu.SemaphoreType.DMA((n,))],
        compiler_params=pltpu.CompilerParams(has_side_effects=True),
    )(*arrays)


def _mod_shard(c_all, w_ada, b_shard):
    def body(c_ref, w_ref, b_ref, o_ref):
        cv = c_ref[...]
        ca = cv * _sigmoid(cv)
        o_ref[...] = jnp.dot(ca.astype(bf16), w_ref[...].astype(bf16), preferred_element_type=f32) + b_ref[...]

    return pl.pallas_call(body, name="mod_shard", out_shape=jax.ShapeDtypeStruct((N_DEV, w_ada.shape[1]), f32),
                          compiler_params=_params())(c_all, w_ada, b_shard)


def _in_proj(x, norm_g, scale1p, shift, w_main, w_small):
    s_len = x.shape[0]
    tm, tn = 512, 1024

    def body(x_ref, g_ref, sc_ref, sh_ref, w_ref, ws_ref, p_ref, ps_ref, h_ref, h_sc):
        @pl.when(pl.program_id(1) == 0)
        def _():
            xb = x_ref[...]
            r = lax.rsqrt(jnp.mean(xb * xb, axis=-1, keepdims=True) + EPS)
            hb = ((xb * r * g_ref[...]) * sc_ref[...] + sh_ref[...]).astype(bf16)
            h_sc[...] = hb
            h_ref[...] = hb
            ps_ref[...] = jnp.dot(hb, ws_ref[...], preferred_element_type=f32)

        p_ref[...] = jnp.dot(h_sc[...], w_ref[...], preferred_element_type=f32)

    vec = pl.BlockSpec((1, D_MODEL), lambda i, j: (0, 0))
    return pl.pallas_call(
        body, name="in_proj", grid=(s_len // tm, N_MAIN // tn),
        in_specs=[pl.BlockSpec((tm, D_MODEL), lambda i, j: (i, 0)), vec, vec, vec,
                  pl.BlockSpec((D_MODEL, tn), lambda i, j: (0, j)),
                  pl.BlockSpec((D_MODEL, N_SMALL), lambda i, j: (0, 0))],
        out_specs=[pl.BlockSpec((tm, tn), lambda i, j: (i, j)),
                   pl.BlockSpec((tm, N_SMALL), lambda i, j: (i, 0)),
                   pl.BlockSpec((tm, D_MODEL), lambda i, j: (i, 0))],
        out_shape=[jax.ShapeDtypeStruct((s_len, N_MAIN), f32), jax.ShapeDtypeStruct((s_len, N_SMALL), f32),
                   jax.ShapeDtypeStruct((s_len, D_MODEL), bf16)],
        scratch_shapes=[pltpu.VMEM((tm, D_MODEL), bf16)],
        compiler_params=_params("parallel", "arbitrary"),
    )(x, norm_g, scale1p, shift, w_main, w_small)


PREP_TM = 256
HALO = 8


def _conv_section(xe_ref, cw_ref, cols, tm):
    acc = cw_ref[pl.ds(CONV_K - 1, 1), cols] * xe_ref[pl.ds(HALO, tm), :]
    for tap in range(CONV_K - 1):
        acc = acc + cw_ref[pl.ds(tap, 1), cols] * xe_ref[pl.ds(HALO - (CONV_K - 1) + tap, tm), :]
    return acc


def _small_fwd(ps, bvec, alog):
    z = ps + bvec
    logsig, softp = _softplus_parts(z)
    gval = -jnp.exp(alog) * softp
    beta = _sigmoid(ps)
    return z, logsig, gval, beta


def _lane_group_selector(first_lane):
    return (_iota((LANES, WIDTH), 0) == first_lane + _iota((LANES, WIDTH), 1) // HEAD_DIM).astype(f32)


def _chunk_masks(tm):
    r, c = _iota((tm, tm), 0), _iota((tm, tm), 1)
    same = (r // CHUNK) == (c // CHUNK)
    return (same & (r >= c)).astype(f32), same.astype(f32)


def _prep(p_main, p_small, qn_g, kn_g, conv_w, bvec, alog):
    s_len = p_main.shape[0]
    tm = PREP_TM
    nb = s_len // tm

    def body(fq_ref, fk_ref, fv_ref, gq_ref, gk_ref, gv_ref, hq_ref, hk_ref, hv_ref, ps_ref, qg_ref, kg_ref,
             cw_ref, bv_ref, al_ref,
             qs_ref, kn_ref, vb_ref, gqo_ref, gko_ref, gvo_ref, small_ref, gcb_ref, glb_ref, bb_ref, xe_sc, carry_sc):
        i = pl.program_id(0)

        @pl.when(i == 0)
        def _():
            carry_sc[...] = jnp.zeros_like(carry_sc)

        qg, kg = qg_ref[...], kg_ref[...]
        for h in range(HEADS):
            sl = slice(h * HEAD_DIM, (h + 1) * HEAD_DIM)
            q = fq_ref[:, sl]
            rq = lax.rsqrt(jnp.mean(q * q, axis=-1, keepdims=True) + EPS)
            qs_ref[:, sl] = (q * rq * qg * QK_SCALE).astype(bf16)
            k = fk_ref[:, sl]
            rk = lax.rsqrt(jnp.mean(k * k, axis=-1, keepdims=True) + EPS)
            kn_ref[:, sl] = (k * rk * kg).astype(bf16)
        vb_ref[...] = fv_ref[...].astype(bf16)

        first = i == 0
        for sec, (x_ref, halo_ref, o_ref) in enumerate(((gq_ref, hq_ref, gqo_ref), (gk_ref, hk_ref, gko_ref),
                                                        (gv_ref, hv_ref, gvo_ref))):
            xe_sc[0:HALO, :] = jnp.where(first, 0.0, halo_ref[...])
            xe_sc[HALO:, :] = x_ref[...]
            cv = _conv_section(xe_sc, cw_ref, slice(sec * WIDTH, (sec + 1) * WIDTH), tm)
            y = cv * _sigmoid(cv)
            if sec == 2:
                o_ref[...] = y
            else:
                mul = QK_SCALE if sec == 0 else 1.0
                for h in range(HEADS):
                    sl = slice(h * HEAD_DIM, (h + 1) * HEAD_DIM)
                    yh = y[:, sl]
                    o_ref[:, sl] = yh * (lax.rsqrt(jnp.sum(yh * yh, axis=-1, keepdims=True) + EPS) * mul)

        lane = _iota((tm, N_SMALL), 1)
        _, logsig, gval, beta = _small_fwd(ps_ref[...], bv_ref[...], al_ref[...])
        lf = jnp.where(lane < HEADS, logsig, 0.0)
        tri = (_iota((tm, tm), 0) >= _iota((tm, tm), 1)).astype(f32)
        fcum = jnp.dot(tri, lf, preferred_element_type=f32, precision=HI) + carry_sc[...]
        carry_sc[...] += jnp.sum(lf, axis=0, keepdims=True)
        small = jnp.where(lane < HEADS, fcum, jnp.where(lane < 2 * HEADS, gval, jnp.where(lane < 3 * HEADS, beta, 0.0)))
        small_ref[...] = small
        tri_c, ones_c = _chunk_masks(tm)
        g_lanes = jnp.where((lane >= HEADS) & (lane < 2 * HEADS), gval, 0.0)
        sel_g = _lane_group_selector(HEADS)
        gc = jnp.dot(tri_c, g_lanes, preferred_element_type=f32, precision=HI)
        gcb_ref[...] = jnp.dot(gc, sel_g, preferred_element_type=f32, precision=HI)
        g_last = jnp.dot(ones_c, g_lanes, preferred_element_type=f32, precision=HI)
        glb_ref[...] = jnp.dot(g_last, sel_g, preferred_element_type=f32, precision=HI)
        bb_ref[...] = jnp.dot(small, _lane_group_selector(2 * HEADS), preferred_element_type=f32, precision=HI)

    def col(cb):
        return pl.BlockSpec((tm, WIDTH), lambda i: (i, cb))

    def halo(cb):
        return pl.BlockSpec((HALO, WIDTH), lambda i: (jnp.maximum(i * (tm // HALO) - 1, 0), cb))

    vec = pl.BlockSpec((1, LANES), lambda i: (0, 0))
    wide_f32 = jax.ShapeDtypeStruct((s_len, WIDTH), f32)
    wide_bf = jax.ShapeDtypeStruct((s_len, WIDTH), bf16)
    out_col = pl.BlockSpec((tm, WIDTH), lambda i: (i, 0))
    return pl.pallas_call(
        body, name="prep", grid=(nb,),
        in_specs=[col(0), col(1), col(2), col(4), col(5), col(6), halo(4), halo(5), halo(6),
                  pl.BlockSpec((tm, N_SMALL), lambda i: (i, 0)), vec, vec,
                  pl.BlockSpec((CONV_K, 3 * WIDTH), lambda i: (0, 0)), vec, vec],
        out_specs=[out_col] * 6 + [pl.BlockSpec((tm, N_SMALL), lambda i: (i, 0)), out_col, out_col, out_col],
        out_shape=[wide_bf, wide_bf, wide_bf, wide_f32, wide_f32, wide_f32,
                   jax.ShapeDtypeStruct((s_len, N_SMALL), f32), wide_f32, wide_f32, wide_f32],
        scratch_shapes=[pltpu.VMEM((tm + HALO, WIDTH), f32), pltpu.VMEM((1, N_SMALL), f32)],
        compiler_params=_params("arbitrary"),
    )(p_main, p_main, p_main, p_main, p_main, p_main, p_main, p_main, p_main, p_small, qn_g, kn_g, conv_w, bvec, alog)


FOX_T = 512
NEG_BIG = -1e30


def _fox_fwd(qs, kn, vb, f_col, f_row):
    s_len = qs.shape[0]
    t = FOX_T
    nq = s_len // t

    def body(q_ref, k_ref, v_ref, fc_ref, fr_ref, o_ref, lse_ref):
        qi = pl.program_id(1)
        q = q_ref[...]
        fq = fc_ref[0]
        causal = _iota((t, t), 0) >= _iota((t, t), 1)

        def step(j, carry, masked):
            m, l, acc = carry
            rows = pl.ds(pl.multiple_of(j * t, t), t)
            s = _dg(q, k_ref[rows, :], 1, 1) + (fq - fr_ref[0, j])
            if masked:
                s = jnp.where(causal, s, NEG_BIG)
            m_new = jnp.maximum(m, jnp.max(s, axis=-1, keepdims=True))
            p = jnp.exp(s - m_new)
            alpha = jnp.exp(m - m_new)
            l = alpha * l + jnp.sum(p, axis=-1, keepdims=True)
            acc = alpha * acc + jnp.dot(p.astype(bf16), v_ref[rows, :], preferred_element_type=f32)
            return m_new, l, acc

        init = (jnp.full((t, 1), NEG_BIG, f32), jnp.zeros((t, 1), f32), jnp.zeros((t, HEAD_DIM), f32))
        carry = lax.fori_loop(0, qi, lambda j, c: step(j, c, False), init)
        m, l, acc = step(qi, carry, True)
        o_ref[...] = acc / l
        lse_ref[0] = m + jnp.log(l)

    return pl.pallas_call(
        body, name="fox_fwd", grid=(HEADS, nq),
        in_specs=[pl.BlockSpec((t, HEAD_DIM), lambda h, i: (i, h)),
                  pl.BlockSpec((s_len, HEAD_DIM), lambda h, i: (0, h)),
                  pl.BlockSpec((s_len, HEAD_DIM), lambda h, i: (0, h)),
                  pl.BlockSpec((1, t, 1), lambda h, i: (h, i, 0)),
                  pl.BlockSpec((1, nq, 1, t), lambda h, i: (h, 0, 0, 0))],
        out_specs=[pl.BlockSpec((t, HEAD_DIM), lambda h, i: (i, h)),
                   pl.BlockSpec((1, t, 1), lambda h, i: (h, i, 0))],
        out_shape=[jax.ShapeDtypeStruct((s_len, WIDTH), f32), jax.ShapeDtypeStruct((HEADS, s_len, 1), f32)],
        compiler_params=_params("parallel", "arbitrary"),
    )(qs, kn, vb, f_col, f_row)


def _fox_bwd(qs, kn, vb, do, a_col, delta_col, f_row):
    s_len = qs.shape[0]
    t = FOX_T
    nq = s_len // t

    def body(q_ref, do_ref, a_ref, dl_ref, k_ref, v_ref, fr_ref, dq_ref, dk_ref, dv_ref, df_ref, dfq_ref):
        qi = pl.program_id(1)

        @pl.when(qi == 0)
        def _():
            dk_ref[...] = jnp.zeros_like(dk_ref)
            dv_ref[...] = jnp.zeros_like(dv_ref)
            df_ref[...] = jnp.zeros_like(df_ref)

        q, do_b = q_ref[...], do_ref[...]
        a, dl = a_ref[0], dl_ref[0]
        causal = _iota((t, t), 0) >= _iota((t, t), 1)

        def step(j, carry, masked):
            dq, row_sum = carry
            rows = pl.ds(pl.multiple_of(j * t, t), t)
            kj, vj = k_ref[rows, :], v_ref[rows, :]
            p = jnp.exp(_dg(q, kj, 1, 1) + (a - fr_ref[0, j]))
            if masked:
                p = jnp.where(causal, p, 0.0)
            ds = p * (_dg(do_b, vj, 1, 1) - dl)
            ds_b = ds.astype(bf16)
            dk_ref[rows, :] += _dg(ds_b, q, 0, 0)
            dv_ref[rows, :] += _dg(p.astype(bf16), do_b, 0, 0)
            df_ref[0, j] += -jnp.sum(ds, axis=0, keepdims=True)
            return dq + jnp.dot(ds_b, kj, preferred_element_type=f32), row_sum + jnp.sum(ds, axis=-1, keepdims=True)

        carry = lax.fori_loop(0, qi, lambda j, c: step(j, c, False),
                              (jnp.zeros((t, HEAD_DIM), f32), jnp.zeros((t, 1), f32)))
        dq, row_sum = step(qi, carry, True)
        dq_ref[...] = dq
        dfq_ref[0] = row_sum

    blk = pl.BlockSpec((t, HEAD_DIM), lambda h, i: (i, h))
    full = pl.BlockSpec((s_len, HEAD_DIM), lambda h, i: (0, h))
    colv = pl.BlockSpec((1, t, 1), lambda h, i: (h, i, 0))
    rowv = pl.BlockSpec((1, nq, 1, t), lambda h, i: (h, 0, 0, 0))
    wide = jax.ShapeDtypeStruct((s_len, WIDTH), f32)
    return pl.pallas_call(
        body, name="fox_bwd", grid=(HEADS, nq),
        in_specs=[blk, blk, colv, colv, full, full, rowv],
        out_specs=[blk, full, full, rowv, colv],
        out_shape=[wide, wide, wide, jax.ShapeDtypeStruct((HEADS, nq, 1, t), f32),
                   jax.ShapeDtypeStruct((HEADS, s_len, 1), f32)],
        compiler_params=_params("parallel", "arbitrary"),
    )(qs, do, a_col, delta_col, kn, vb, f_row)


INTRA_CHUNKS = 8
INTRA_INTERLEAVE = 4
SCAN_FWD_CHUNKS = 8
SCAN_BWD_CHUNKS = 4


def _gdn_intra_fwd(gq, gk, gv, gc_b, g_last_b, beta_b):
    s_len = gq.shape[0]
    cpb = INTRA_CHUNKS
    rows_blk = cpb * CHUNK
    n_chunks = s_len // CHUNK

    def body(q_ref, k_ref, v_ref, gc_ref, gl_ref, b_ref, u_ref, w_ref, qg_ref, kd_ref, attn_ref, t_ref, eg_ref):
        ms, rhss = [], []
        for ci in range(cpb):
            rows = pl.ds(ci * CHUNK, CHUNK)
            m, rhs, qg, kd, attn, eg_last = _gdn_intra_pre(q_ref[rows, :], k_ref[rows, :], v_ref[rows, :],
                                                           gc_ref[rows, :], gl_ref[rows, :], b_ref[rows, :],
                                                           _BF_PLAIN[1])
            qg_ref[rows, :] = qg.astype(bf16)
            kd_ref[rows, :] = kd.astype(bf16)
            attn_ref[0, ci] = attn.astype(bf16)
            eg_ref[0, ci] = eg_last
            ms.append(m)
            rhss.append(rhs)
        for ci, (t, rhs) in enumerate(zip(_inv_unit_lower_many(ms), rhss)):
            rows = pl.ds(ci * CHUNK, CHUNK)
            t_ref[0, ci] = t
            uw = _X3_PLAIN[0](t, rhs)
            u_ref[rows, :] = uw[:, :HEAD_DIM]
            w_ref[rows, :] = uw[:, HEAD_DIM:].astype(bf16)

    blk = pl.BlockSpec((rows_blk, HEAD_DIM), lambda h, i: (i, h))
    sq = pl.BlockSpec((1, cpb, CHUNK, CHUNK), lambda h, i: (h, i, 0, 0))
    wide_bf = jax.ShapeDtypeStruct((s_len, WIDTH), bf16)
    return pl.pallas_call(
        body, name="gdn_intra_fwd", grid=(HEADS, s_len // rows_blk),
        in_specs=[blk] * 6,
        out_specs=[blk] * 4 + [sq, sq, pl.BlockSpec((1, cpb, SUBLANES, HEAD_DIM), lambda h, i: (h, i, 0, 0))],
        out_shape=[jax.ShapeDtypeStruct((s_len, WIDTH), f32), wide_bf, wide_bf, wide_bf,
                   jax.ShapeDtypeStruct((HEADS, n_chunks, CHUNK, CHUNK), bf16),
                   jax.ShapeDtypeStruct((HEADS, n_chunks, CHUNK, CHUNK), f32),
                   jax.ShapeDtypeStruct((HEADS, n_chunks, SUBLANES, HEAD_DIM), f32)],
        compiler_params=_params("parallel", "parallel"),
    )(gq, gk, gv, gc_b, g_last_b, beta_b)


def _gdn_scan_fwd(u, w, qg, kd, attn, eg):
    s_len = u.shape[0]
    cpb = SCAN_FWD_CHUNKS
    rows_blk = cpb * CHUNK
    n_chunks = s_len // CHUNK

    def body(u_ref, w_ref, qg_ref, kd_ref, attn_ref, eg_ref, o_ref, st_ref, s_sc):
        @pl.when(pl.program_id(0) == 0)
        def _():
            s_sc[...] = jnp.zeros_like(s_sc)

        def chunk(ci, _):
            rows = pl.ds(pl.multiple_of(ci * CHUNK, CHUNK), CHUNK)
            for h in range(HEADS):
                cols = slice(h * HEAD_DIM, (h + 1) * HEAD_DIM)
                s0 = s_sc[h]
                st_ref[h, ci] = s0
                s0_b = s0.astype(bf16)
                v_new = u_ref[rows, cols] - jnp.dot(w_ref[rows, cols], s0_b, preferred_element_type=f32)
                vn_b = v_new.astype(bf16)
                o_ref[rows, cols] = (jnp.dot(qg_ref[rows, cols], s0_b, preferred_element_type=f32)
                                     + jnp.dot(attn_ref[h, ci], vn_b, preferred_element_type=f32))
                s_sc[h] = _scale_rows(s0, eg_ref[h, ci]) + _dg(kd_ref[rows, cols], vn_b, 0, 0)
            return 0

        lax.fori_loop(0, cpb, chunk, 0)

    row = pl.BlockSpec((rows_blk, WIDTH), lambda i: (i, 0))
    return pl.pallas_call(
        body, name="gdn_scan_fwd", grid=(s_len // rows_blk,),
        in_specs=[row] * 4 + [pl.BlockSpec((HEADS, cpb, CHUNK, CHUNK), lambda i: (0, i, 0, 0)),
                              pl.BlockSpec((HEADS, cpb, SUBLANES, HEAD_DIM), lambda i: (0, i, 0, 0))],
        out_specs=[row, pl.BlockSpec((HEADS, cpb, HEAD_DIM, HEAD_DIM), lambda i: (0, i, 0, 0))],
        out_shape=[jax.ShapeDtypeStruct((s_len, WIDTH), f32),
                   jax.ShapeDtypeStruct((HEADS, n_chunks, HEAD_DIM, HEAD_DIM), f32)],
        scratch_shapes=[pltpu.VMEM((HEADS, HEAD_DIM, HEAD_DIM), f32)],
        compiler_params=_params("arbitrary"),
    )(u, w, qg, kd, attn, eg)


def _gdn_scan_bwd(u, w, qg, kd, attn, eg, states, d_o):
    s_len = u.shape[0]
    cpb = SCAN_BWD_CHUNKS
    rows_blk = cpb * CHUNK
    n_chunks = s_len // CHUNK
    nb = s_len // rows_blk

    def body(u_ref, w_ref, qg_ref, kd_ref, attn_ref, eg_ref, st_ref, do_ref,
             du_ref, dw_ref, dqg_ref, dkd_ref, dattn_ref, deg_ref, ds_sc):
        @pl.when(pl.program_id(0) == 0)
        def _():
            ds_sc[...] = jnp.zeros_like(ds_sc)

        def chunk(step, _):
            ci = cpb - 1 - step
            rows = pl.ds(pl.multiple_of(ci * CHUNK, CHUNK), CHUNK)
            for h in range(HEADS):
                cols = slice(h * HEAD_DIM, (h + 1) * HEAD_DIM)
                s0 = st_ref[h, ci]
                s0_b = s0.astype(bf16)
                ds1 = ds_sc[h]
                ds1_b = ds1.astype(bf16)
                w_b, qg_b, kd_b, attn_b = w_ref[rows, cols], qg_ref[rows, cols], kd_ref[rows, cols], attn_ref[h, ci]
                do_b = do_ref[rows, cols].astype(bf16)
                vn_b = (u_ref[rows, cols] - jnp.dot(w_b, s0_b, preferred_element_type=f32)).astype(bf16)
                dvn = _dg(attn_b, do_b, 0, 0) + jnp.dot(kd_b, ds1_b, preferred_element_type=f32)
                dvn_b = dvn.astype(bf16)
                dattn_ref[h, ci] = _dg(do_b, vn_b, 1, 1)
                dqg_ref[rows, cols] = _dg(do_b, s0_b, 1, 1)
                dkd_ref[rows, cols] = _dg(vn_b, ds1_b, 1, 1)
                du_ref[rows, cols] = dvn
                dw_ref[rows, cols] = -_dg(dvn_b, s0_b, 1, 1)
                eg_last = eg_ref[h, ci]
                ds_sc[h] = _dg(qg_b, do_b, 0, 0) - _dg(w_b, dvn_b, 0, 0) + _scale_rows(ds1, eg_last)
                deg_ref[h, ci] = jnp.sum((ds1 * s0).reshape(HEAD_DIM // SUBLANES, SUBLANES, HEAD_DIM), axis=0)
            return 0

        lax.fori_loop(0, cpb, chunk, 0)

    row = pl.BlockSpec((rows_blk, WIDTH), lambda i: (nb - 1 - i, 0))
    sq = pl.BlockSpec((HEADS, cpb, CHUNK, CHUNK), lambda i: (0, nb - 1 - i, 0, 0))
    egs = pl.BlockSpec((HEADS, cpb, SUBLANES, HEAD_DIM), lambda i: (0, nb - 1 - i, 0, 0))
    wide = jax.ShapeDtypeStruct((s_len, WIDTH), f32)
    return pl.pallas_call(
        body, name="gdn_scan_bwd", grid=(nb,),
        in_specs=[row] * 4 + [sq, egs, pl.BlockSpec((HEADS, cpb, HEAD_DIM, HEAD_DIM), lambda i: (0, nb - 1 - i, 0, 0)), row],
        out_specs=[row] * 4 + [sq, egs],
        out_shape=[wide] * 4 + [jax.ShapeDtypeStruct((HEADS, n_chunks, CHUNK, CHUNK), f32),
                                jax.ShapeDtypeStruct((HEADS, n_chunks, SUBLANES, HEAD_DIM), f32)],
        scratch_shapes=[pltpu.VMEM((HEADS, HEAD_DIM, HEAD_DIM), f32)],
        compiler_params=_params("arbitrary"),
    )(u, w, qg, kd, attn, eg, states, d_o)


def _gdn_intra_bwd(gq, gk, gv, gc_b, g_last_b, beta_b, t_inv, du, dw, dqg, dkd, dattn, deg):
    s_len = gq.shape[0]
    cpb = INTRA_CHUNKS
    rows_blk = cpb * CHUNK

    def body(q_ref, k_ref, v_ref, gc_ref, gl_ref, b_ref, t_ref, du_ref, dw_ref, dqg_ref, dkd_ref, dattn_ref, deg_ref,
             dq_ref, dk_ref, dv_ref, dgc_ref, dgl_ref, db_ref):
        def group(it, _):
            for un in range(INTRA_INTERLEAVE):
                ci = it * INTRA_INTERLEAVE + un
                rows = pl.ds(pl.multiple_of(ci * CHUNK, CHUNK), CHUNK)
                t_known = t_ref[0, ci]
                _, vjp = jax.vjp(lambda q, k, v, gc, gl, b: _gdn_intra(q, k, v, gc, gl, b, t_known),
                                 q_ref[rows, :], k_ref[rows, :], v_ref[rows, :], gc_ref[rows, :], gl_ref[rows, :],
                                 b_ref[rows, :])
                duw = jnp.concatenate([du_ref[rows, :], dw_ref[rows, :]], axis=1)
                dq, dk, dv, dgc, dgl, db = vjp((duw, dqg_ref[rows, :], dkd_ref[rows, :], dattn_ref[0, ci],
                                                deg_ref[0, ci]))
                dq_ref[rows, :] = dq
                dk_ref[rows, :] = dk
                dv_ref[rows, :] = dv
                dgc_ref[rows, :] = dgc
                dgl_ref[rows, :] = dgl
                db_ref[rows, :] = db
            return 0

        lax.fori_loop(0, cpb // INTRA_INTERLEAVE, group, 0)

    blk = pl.BlockSpec((rows_blk, HEAD_DIM), lambda h, i: (i, h))
    sq = pl.BlockSpec((1, cpb, CHUNK, CHUNK), lambda h, i: (h, i, 0, 0))
    egs = pl.BlockSpec((1, cpb, SUBLANES, HEAD_DIM), lambda h, i: (h, i, 0, 0))
    wide = jax.ShapeDtypeStruct((s_len, WIDTH), f32)
    return pl.pallas_call(
        body, name="gdn_intra_bwd", grid=(HEADS, s_len // rows_blk),
        in_specs=[blk] * 6 + [sq] + [blk] * 4 + [sq, egs],
        out_specs=[blk] * 6,
        out_shape=[wide] * 6,
        compiler_params=_params("parallel", "parallel"),
    )(gq, gk, gv, gc_b, g_last_b, beta_b, t_inv, du, dw, dqg, dkd, dattn, deg)


MIX_TM = 256


def _mix_fwd(fox_o, gdn_o, p_main, gnorm_g):
    s_len = fox_o.shape[0]
    tm = MIX_TM

    def body(fo_ref, go_ref, fz_ref, gz_ref, g_ref, mixed_ref):
        fz = fz_ref[...]
        mixed_ref[:, 0:WIDTH] = (fo_ref[...] * (fz * _sigmoid(fz))).astype(bf16)
        gz = gz_ref[...]
        gate = gz * _sigmoid(gz)
        gg = g_ref[...]
        for h in range(HEADS):
            sl = slice(h * HEAD_DIM, (h + 1) * HEAD_DIM)
            o = go_ref[:, sl]
            r = lax.rsqrt(jnp.mean(o * o, axis=-1, keepdims=True) + EPS)
            mixed_ref[:, WIDTH + h * HEAD_DIM:WIDTH + (h + 1) * HEAD_DIM] = (o * r * gg * gate[:, sl]).astype(bf16)

    row = pl.BlockSpec((tm, WIDTH), lambda i: (i, 0))
    return pl.pallas_call(
        body, name="mix_fwd", grid=(s_len // tm,),
        in_specs=[row, row, pl.BlockSpec((tm, WIDTH), lambda i: (i, 3)), pl.BlockSpec((tm, WIDTH), lambda i: (i, 7)),
                  pl.BlockSpec((1, LANES), lambda i: (0, 0))],
        out_specs=pl.BlockSpec((tm, 2 * WIDTH), lambda i: (i, 0)),
        out_shape=jax.ShapeDtypeStruct((s_len, 2 * WIDTH), bf16),
        compiler_params=_params("parallel"),
    )(fox_o, gdn_o, p_main, p_main, gnorm_g)


def _silu_grad(z):
    sg = _sigmoid(z)
    return sg * (1.0 + z * (1.0 - sg))


def _mix_bwd(dmixed, fox_o, gdn_o, p_main, gnorm_g):
    s_len = fox_o.shape[0]
    tm = MIX_TM

    def body(dm_ref, fo_ref, go_ref, fz_ref, gz_ref, g_ref, dof_ref, delta_ref, dfz_ref, dgz_ref, dgo_ref, dg_ref):
        @pl.when(pl.program_id(0) == 0)
        def _():
            dg_ref[...] = jnp.zeros_like(dg_ref)

        lane = _iota((tm, LANES), 1)
        fz = fz_ref[...]
        dmf = dm_ref[:, 0:WIDTH]
        fo = fo_ref[...]
        dof = dmf * (fz * _sigmoid(fz))
        dof_ref[...] = dof.astype(bf16)
        dfz_ref[...] = (dmf * fo * _silu_grad(fz)).astype(bf16)
        prod = dof * fo
        delta = jnp.zeros((tm, LANES), f32)
        for h in range(HEADS):
            dh = jnp.sum(prod[:, h * HEAD_DIM:(h + 1) * HEAD_DIM], axis=-1, keepdims=True)
            delta = jnp.where(lane == h, dh, delta)
        delta_ref[...] = delta

        gz = gz_ref[...]
        dmg = dm_ref[:, WIDTH:2 * WIDTH]
        gate = gz * _sigmoid(gz)
        sgrad = _silu_grad(gz)
        gg = g_ref[...]
        dg_acc = jnp.zeros((1, HEAD_DIM), f32)
        for h in range(HEADS):
            sl = slice(h * HEAD_DIM, (h + 1) * HEAD_DIM)
            o = go_ref[:, sl]
            r = lax.rsqrt(jnp.mean(o * o, axis=-1, keepdims=True) + EPS)
            on = o * r
            dmh = dmg[:, sl]
            dgz_ref[:, sl] = (dmh * (on * gg) * sgrad[:, sl]).astype(bf16)
            dy = dmh * gate[:, sl]
            dg_acc = dg_acc + jnp.sum(dy * on, axis=0, keepdims=True)
            tt = dy * gg
            dgo_ref[:, sl] = r * (tt - on * jnp.mean(tt * on, axis=-1, keepdims=True))
        dg_ref[...] += dg_acc

    row = pl.BlockSpec((tm, WIDTH), lambda i: (i, 0))
    wide_bf = jax.ShapeDtypeStruct((s_len, WIDTH), bf16)
    return pl.pallas_call(
        body, name="mix_bwd", grid=(s_len // tm,),
        in_specs=[pl.BlockSpec((tm, 2 * WIDTH), lambda i: (i, 0)), row, row,
                  pl.BlockSpec((tm, WIDTH), lambda i: (i, 3)), pl.BlockSpec((tm, WIDTH), lambda i: (i, 7)),
                  pl.BlockSpec((1, LANES), lambda i: (0, 0))],
        out_specs=[row, pl.BlockSpec((tm, LANES), lambda i: (i, 0)), row, row, row,
                   pl.BlockSpec((1, LANES), lambda i: (0, 0))],
        out_shape=[wide_bf, jax.ShapeDtypeStruct((s_len, LANES), f32), wide_bf, wide_bf,
                   jax.ShapeDtypeStruct((s_len, WIDTH), f32), jax.ShapeDtypeStruct((1, LANES), f32)],
        compiler_params=_params("arbitrary"),
    )(dmixed, fox_o, gdn_o, p_main, p_main, gnorm_g)


def _out_head(mixed, w_out, x, target, gate, final_g):
    s_len = x.shape[0]
    tm = 256

    def body(mx_ref, w_ref, x_ref, t_ref, gate_ref, fg_ref, loss_ref, dy_ref, dz_ref, dm_ref, dfg_ref, dgate_ref):
        @pl.when(pl.program_id(0) == 0)
        def _():
            loss_ref[...] = jnp.zeros_like(loss_ref)
            dfg_ref[...] = jnp.zeros_like(dfg_ref)
            dgate_ref[...] = jnp.zeros_like(dgate_ref)

        w = w_ref[...]
        z = jnp.dot(mx_ref[...], w, preferred_element_type=f32)
        gate_v, fg = gate_ref[...], fg_ref[...]
        y1 = x_ref[...] + gate_v * z
        r = lax.rsqrt(jnp.mean(y1 * y1, axis=-1, keepdims=True) + EPS)
        yn = y1 * r
        err = yn * fg - t_ref[...]
        loss_ref[...] += 0.5 * jnp.sum(jnp.mean(err * err, axis=-1, keepdims=True))
        dout = err * (1.0 / D_MODEL)
        dfg_ref[...] += jnp.sum(dout * yn, axis=0, keepdims=True)
        tt = dout * fg
        dy1 = r * (tt - yn * jnp.mean(tt * yn, axis=-1, keepdims=True))
        dy_ref[...] = dy1
        dgate_ref[...] += jnp.sum(dy1 * z, axis=0, keepdims=True)
        dz = (dy1 * gate_v).astype(bf16)
        dz_ref[...] = dz
        dm_ref[...] = _dg(dz, w, 1, 1)

    row = pl.BlockSpec((tm, D_MODEL), lambda i: (i, 0))
    vec = pl.BlockSpec((1, D_MODEL), lambda i: (0, 0))
    big = jax.ShapeDtypeStruct((s_len, D_MODEL), f32)
    return pl.pallas_call(
        body, name="out_head", grid=(s_len // tm,),
        in_specs=[row, pl.BlockSpec((D_MODEL, D_MODEL), lambda i: (0, 0)), row, row, vec, vec],
        out_specs=[pl.BlockSpec((1, LANES), lambda i: (0, 0)), row, row, row, vec, vec],
        out_shape=[jax.ShapeDtypeStruct((1, LANES), f32), big, jax.ShapeDtypeStruct((s_len, D_MODEL), bf16), big,
                   jax.ShapeDtypeStruct((1, D_MODEL), f32), jax.ShapeDtypeStruct((1, D_MODEL), f32)],
        compiler_params=_params("arbitrary"),
    )(mixed, w_out, x, target, gate, final_g)


def _matmul_tn(name, a, b):
    k_len, m_len = a.shape
    n_len = b.shape[1]
    tk, tm, tn = 512, 1024, min(1024, n_len)

    def body(a_ref, b_ref, o_ref):
        @pl.when(pl.program_id(2) == 0)
        def _():
            o_ref[...] = jnp.zeros_like(o_ref)

        o_ref[...] += _dg(a_ref[...], b_ref[...], 0, 0)

    return pl.pallas_call(
        body, name=name, grid=(m_len // tm, n_len // tn, k_len // tk),
        in_specs=[pl.BlockSpec((tk, tm), lambda i, j, k: (k, i)), pl.BlockSpec((tk, tn), lambda i, j, k: (k, j))],
        out_specs=pl.BlockSpec((tm, tn), lambda i, j, k: (i, j)),
        out_shape=jax.ShapeDtypeStruct((m_len, n_len), f32),
        compiler_params=_params("parallel", "parallel", "arbitrary"),
    )(a, b)


def _post1(p_main, p_small, qn_g, kn_g, conv_w, bvec, alog, dqs, dkn, dgq, dgk, dgv, dgc_b, dgl_b, dbeta_b, df):
    s_len = p_main.shape[0]
    tm = PREP_TM
    nb = s_len // tm

    def body(fq_ref, fk_ref, gq_ref, gk_ref, gv_ref, hq_ref, hk_ref, hv_ref, ps_ref, qg_ref, kg_ref, cw_ref, bv_ref,
             al_ref, dqs_ref, dkn_ref, dgq_ref, dgk_ref, dgv_ref, dgcb_ref, dglb_ref, dbb_ref, df_ref,
             dfq_ref, dfk_ref, dconv_ref, dps_ref, dqg_ref, dkg_ref, sums_ref, xe_sc, carry_sc):
        step = pl.program_id(0)
        blk = nb - 1 - step

        @pl.when(step == 0)
        def _():
            carry_sc[...] = jnp.zeros_like(carry_sc)
            dqg_ref[...] = jnp.zeros_like(dqg_ref)
            dkg_ref[...] = jnp.zeros_like(dkg_ref)
            sums_ref[...] = jnp.zeros_like(sums_ref)

        for x_ref, g_ref, dy_ref, o_ref, acc_ref, mul in ((fq_ref, qg_ref, dqs_ref, dfq_ref, dqg_ref, QK_SCALE),
                                                          (fk_ref, kg_ref, dkn_ref, dfk_ref, dkg_ref, 1.0)):
            gain = g_ref[...]
            acc = jnp.zeros((1, HEAD_DIM), f32)
            for h in range(HEADS):
                sl = slice(h * HEAD_DIM, (h + 1) * HEAD_DIM)
                xv = x_ref[:, sl]
                r = lax.rsqrt(jnp.mean(xv * xv, axis=-1, keepdims=True) + EPS)
                xn = xv * r
                dy = dy_ref[:, sl] * mul
                acc = acc + jnp.sum(dy * xn, axis=0, keepdims=True)
                tt = dy * gain
                o_ref[:, sl] = (r * (tt - xn * jnp.mean(tt * xn, axis=-1, keepdims=True))).astype(bf16)
            acc_ref[...] += acc

        first = blk == 0
        for sec, (x_ref, halo_ref, dy_ref) in enumerate(((gq_ref, hq_ref, dgq_ref), (gk_ref, hk_ref, dgk_ref),
                                                         (gv_ref, hv_ref, dgv_ref))):
            xe_sc[0:HALO, :] = jnp.where(first, 0.0, halo_ref[...])
            xe_sc[HALO:, :] = x_ref[...]
            cv = _conv_section(xe_sc, cw_ref, slice(sec * WIDTH, (sec + 1) * WIDTH), tm)
            sgrad = _silu_grad(cv)
            if sec == 2:
                dconv_ref[:, sec * WIDTH:(sec + 1) * WIDTH] = dy_ref[...] * sgrad
            else:
                y = cv * _sigmoid(cv)
                mul = QK_SCALE if sec == 0 else 1.0
                for h in range(HEADS):
                    sl = slice(h * HEAD_DIM, (h + 1) * HEAD_DIM)
                    yh = y[:, sl]
                    r = lax.rsqrt(jnp.sum(yh * yh, axis=-1, keepdims=True) + EPS)
                    dqh = dy_ref[:, sl]
                    dyh = (mul * r) * (dqh - yh * (r * r) * jnp.sum(dqh * yh, axis=-1, keepdims=True))
                    dconv_ref[:, sec * WIDTH + h * HEAD_DIM:sec * WIDTH + (h + 1) * HEAD_DIM] = dyh * sgrad[:, sl]

        lane = _iota((tm, N_SMALL), 1)
        z, _, gval, beta = _small_fwd(ps_ref[...], bv_ref[...], al_ref[...])
        sig_z = _sigmoid(z)
        sel_t = (_iota((WIDTH, LANES), 1) == HEADS + _iota((WIDTH, LANES), 0) // HEAD_DIM).astype(f32)
        dgc = jnp.dot(dgcb_ref[...], sel_t, preferred_element_type=f32, precision=HI)
        dgl = jnp.dot(dglb_ref[...], sel_t, preferred_element_type=f32, precision=HI)
        tri_c, ones_c = _chunk_masks(tm)
        dg = (_dg(tri_c, dgc, 0, 0, HI) + jnp.dot(ones_c, dgl, preferred_element_type=f32, precision=HI))
        sel_t2 = (_iota((WIDTH, LANES), 1) == 2 * HEADS + _iota((WIDTH, LANES), 0) // HEAD_DIM).astype(f32)
        dbeta = jnp.dot(dbb_ref[...], sel_t2, preferred_element_type=f32, precision=HI)
        dfb = jnp.where(lane < HEADS, df_ref[...], 0.0)
        tri_u = (_iota((tm, tm), 1) >= _iota((tm, tm), 0)).astype(f32)
        dlogf = jnp.dot(tri_u, dfb, preferred_element_type=f32, precision=HI) + carry_sc[...]
        carry_sc[...] += jnp.sum(dfb, axis=0, keepdims=True)
        dff = dlogf * (1.0 - sig_z)
        dga = dg * (-jnp.exp(al_ref[...])) * sig_z
        dgb_small = dbeta * beta * (1.0 - beta)
        dps = jnp.where(lane < HEADS, dff, jnp.where(lane < 2 * HEADS, dga, jnp.where(lane < 3 * HEADS, dgb_small, 0.0)))
        dps_ref[...] = dps.astype(bf16)
        row = _iota((8, N_SMALL), 0)
        s0 = jnp.sum(dps, axis=0, keepdims=True)
        s1 = jnp.sum(jnp.where((lane >= HEADS) & (lane < 2 * HEADS), dg * gval, 0.0), axis=0, keepdims=True)
        sums_ref[...] += jnp.where(row == 0, s0, jnp.where(row == 1, s1, 0.0))

    def col(cb):
        return pl.BlockSpec((tm, WIDTH), lambda i: (nb - 1 - i, cb))

    def halo(cb):
        return pl.BlockSpec((HALO, WIDTH), lambda i: (jnp.maximum((nb - 1 - i) * (tm // HALO) - 1, 0), cb))

    vec = pl.BlockSpec((1, LANES), lambda i: (0, 0))
    row0 = pl.BlockSpec((tm, WIDTH), lambda i: (nb - 1 - i, 0))
    small = pl.BlockSpec((tm, N_SMALL), lambda i: (nb - 1 - i, 0))
    wide_bf = jax.ShapeDtypeStruct((s_len, WIDTH), bf16)
    return pl.pallas_call(
        body, name="post1", grid=(nb,),
        in_specs=[col(0), col(1), col(4), col(5), col(6), halo(4), halo(5), halo(6), small, vec, vec,
                  pl.BlockSpec((CONV_K, 3 * WIDTH), lambda i: (0, 0)), vec, vec,
                  row0, row0, row0, row0, row0, row0, row0, row0, small],
        out_specs=[row0, row0, pl.BlockSpec((tm, 3 * WIDTH), lambda i: (nb - 1 - i, 0)), small, vec, vec,
                   pl.BlockSpec((8, N_SMALL), lambda i: (0, 0))],
        out_shape=[wide_bf, wide_bf, jax.ShapeDtypeStruct((s_len, 3 * WIDTH), f32),
                   jax.ShapeDtypeStruct((s_len, N_SMALL), bf16), jax.ShapeDtypeStruct((1, LANES), f32),
                   jax.ShapeDtypeStruct((1, LANES), f32), jax.ShapeDtypeStruct((8, N_SMALL), f32)],
        scratch_shapes=[pltpu.VMEM((tm + HALO, WIDTH), f32), pltpu.VMEM((1, N_SMALL), f32)],
        compiler_params=_params("arbitrary"),
    )(p_main, p_main, p_main, p_main, p_main, p_main, p_main, p_main, p_small, qn_g, kn_g, conv_w, bvec, alog,
      dqs, dkn, dgq, dgk, dgv, dgc_b, dgl_b, dbeta_b, df)


def _post2(p_main, dconv, conv_w):
    s_len = p_main.shape[0]
    tm = PREP_TM
    nb = s_len // tm

    def body(gq_ref, gk_ref, gv_ref, hq_ref, hk_ref, hv_ref, dc_ref, dnext_ref, cw_ref, dx_ref, dw_ref, xe_sc, de_sc):
        i = pl.program_id(0)

        @pl.when(i == 0)
        def _():
            dw_ref[...] = jnp.zeros_like(dw_ref)

        first, last = i == 0, i == nb - 1
        row = _iota((8, WIDTH), 0)
        for sec, (x_ref, halo_ref) in enumerate(((gq_ref, hq_ref), (gk_ref, hk_ref), (gv_ref, hv_ref))):
            cols = slice(sec * WIDTH, (sec + 1) * WIDTH)
            dc = dc_ref[:, cols]
            de_sc[0:tm, :] = dc
            de_sc[tm:, :] = jnp.where(last, 0.0, dnext_ref[:, cols])
            dx = cw_ref[pl.ds(CONV_K - 1, 1), cols] * dc
            for tap in range(CONV_K - 1):
                dx = dx + cw_ref[pl.ds(tap, 1), cols] * de_sc[pl.ds(CONV_K - 1 - tap, tm), :]
            dx_ref[:, cols] = dx.astype(bf16)
            xe_sc[0:HALO, :] = jnp.where(first, 0.0, halo_ref[...])
            xe_sc[HALO:, :] = x_ref[...]
            dw = jnp.zeros((8, WIDTH), f32)
            for tap in range(CONV_K):
                contrib = jnp.sum(dc * xe_sc[pl.ds(HALO - (CONV_K - 1) + tap, tm), :], axis=0, keepdims=True)
                dw = jnp.where(row == tap, contrib, dw)
            dw_ref[:, cols] += dw

    def col(cb):
        return pl.BlockSpec((tm, WIDTH), lambda i: (i, cb))

    def halo(cb):
        return pl.BlockSpec((HALO, WIDTH), lambda i: (jnp.maximum(i * (tm // HALO) - 1, 0), cb))

    return pl.pallas_call(
        body, name="post2", grid=(nb,),
        in_specs=[col(4), col(5), col(6), halo(4), halo(5), halo(6),
                  pl.BlockSpec((tm, 3 * WIDTH), lambda i: (i, 0)),
                  pl.BlockSpec((HALO, 3 * WIDTH), lambda i: (jnp.minimum((i + 1) * (tm // HALO), s_len // HALO - 1), 0)),
                  pl.BlockSpec((CONV_K, 3 * WIDTH), lambda i: (0, 0))],
        out_specs=[pl.BlockSpec((tm, 3 * WIDTH), lambda i: (i, 0)), pl.BlockSpec((8, 3 * WIDTH), lambda i: (0, 0))],
        out_shape=[jax.ShapeDtypeStruct((s_len, 3 * WIDTH), bf16), jax.ShapeDtypeStruct((8, 3 * WIDTH), f32)],
        scratch_shapes=[pltpu.VMEM((tm + HALO, WIDTH), f32), pltpu.VMEM((tm + HALO, WIDTH), f32)],
        compiler_params=_params("arbitrary"),
    )(p_main, p_main, p_main, p_main, p_main, p_main, dconv, dconv, conv_w)


def _in_proj_bwd(dp_main, dp_small, w_main, w_small, x, dy1, norm_g, scale1p):
    s_len = x.shape[0]
    tm, tk = 512, 1024
    nk = N_MAIN // tk

    def body(dp_ref, dps_ref, w_ref, ws_ref, x_ref, dy_ref, g_ref, sc_ref, dx_ref, dsh_ref, dsc_ref, dg_ref, acc_sc):
        i, k = pl.program_id(0), pl.program_id(1)

        @pl.when((i == 0) & (k == 0))
        def _():
            dsh_ref[...] = jnp.zeros_like(dsh_ref)
            dsc_ref[...] = jnp.zeros_like(dsc_ref)
            dg_ref[...] = jnp.zeros_like(dg_ref)

        @pl.when(k == 0)
        def _():
            acc_sc[...] = _dg(dps_ref[...], ws_ref[...], 1, 1)

        acc_sc[...] += _dg(dp_ref[...], w_ref[...], 1, 1)

        @pl.when(k == nk - 1)
        def _():
            dh = acc_sc[...]
            xb = x_ref[...]
            r = lax.rsqrt(jnp.mean(xb * xb, axis=-1, keepdims=True) + EPS)
            xr = xb * r
            gain = g_ref[...]
            dsh_ref[...] += jnp.sum(dh, axis=0, keepdims=True)
            dsc_ref[...] += jnp.sum(dh * (xr * gain), axis=0, keepdims=True)
            dxn = dh * sc_ref[...]
            dg_ref[...] += jnp.sum(dxn * xr, axis=0, keepdims=True)
            tt = dxn * gain
            dx_ref[...] = r * (tt - xr * jnp.mean(tt * xr, axis=-1, keepdims=True)) + dy_ref[...]

    row = pl.BlockSpec((tm, D_MODEL), lambda i, k: (i, 0))
    vec = pl.BlockSpec((1, D_MODEL), lambda i, k: (0, 0))
    vshape = jax.ShapeDtypeStruct((1, D_MODEL), f32)
    return pl.pallas_call(
        body, name="in_proj_bwd", grid=(s_len // tm, nk),
        in_specs=[pl.BlockSpec((tm, tk), lambda i, k: (i, k)), pl.BlockSpec((tm, N_SMALL), lambda i, k: (i, 0)),
                  pl.BlockSpec((D_MODEL, tk), lambda i, k: (0, k)), pl.BlockSpec((D_MODEL, N_SMALL), lambda i, k: (0, 0)),
                  row, row, vec, vec],
        out_specs=[row, vec, vec, vec],
        out_shape=[jax.ShapeDtypeStruct((s_len, D_MODEL), f32), vshape, vshape, vshape],
        scratch_shapes=[pltpu.VMEM((tm, D_MODEL), f32)],
        compiler_params=_params("arbitrary", "arbitrary"),
    )(dp_main, dp_small, w_main, w_small, x, dy1, norm_g, scale1p)


def _adamw(name, w, g_stack, m, v, tr):
    n_stack, rows, cols = g_stack.shape

    def body(w_ref, g_ref, m_ref, v_ref, go_ref, d_ref, mo_ref, vo_ref):
        g = g_ref[0].astype(f32)
        for k in range(1, n_stack):
            g = g + g_ref[k].astype(f32)
        go_ref[...] = g
        m_new = ADAM_B1 * m_ref[...] + (1.0 - ADAM_B1) * g
        v_new = ADAM_B2 * v_ref[...] + (1.0 - ADAM_B2) * (g * g)
        mo_ref[...] = m_new
        vo_ref[...] = v_new
        m_hat = m_new / (1.0 - ADAM_B1 ** ADAM_STEP)
        v_hat = v_new / (1.0 - ADAM_B2 ** ADAM_STEP)
        d_ref[...] = -ADAM_LR * (m_hat / (jnp.sqrt(v_hat) + ADAM_EPS) + ADAM_WD * w_ref[...])

    blk = pl.BlockSpec((tr, cols), lambda i: (i, 0))
    shape = jax.ShapeDtypeStruct((rows, cols), f32)
    return pl.pallas_call(
        body, name=name, grid=(rows // tr,),
        in_specs=[blk, pl.BlockSpec((n_stack, tr, cols), lambda i: (0, i, 0)), blk, blk],
        out_specs=[blk] * 4, out_shape=[shape] * 4,
        compiler_params=_params("parallel"),
    )(w, g_stack, m, v)


def _w_ada_grad(c_all_t, dmod_pad):
    def body(c_ref, d_ref, o_ref):
        cv = c_ref[...]
        o_ref[...] = jnp.dot(cv * _sigmoid(cv), d_ref[...], preferred_element_type=f32, precision=HI)

    return pl.pallas_call(body, name="w_ada_grad",
                          out_shape=jax.ShapeDtypeStruct((c_all_t.shape[0], dmod_pad.shape[1]), f32),
                          compiler_params=_params())(c_all_t, dmod_pad)


SMALL_NAMES = ("norm_g", "b_ada", "b_fgate", "fox_qn_g", "fox_kn_g", "gdn_A_log", "gdn_dt_bias", "gdn_norm_g", "final_g")
SMALL_SIZES = (D_MODEL, 3 * D_MODEL, HEADS, HEAD_DIM, HEAD_DIM, HEADS, HEADS, HEAD_DIM, D_MODEL)
SMALL_PACK = 10752


def _pack(vectors, total):
    flat = jnp.concatenate([t.reshape(-1) for t in vectors])
    return jnp.pad(flat, (0, total - flat.shape[0])).reshape(1, total)


def _lanes(*pieces):
    row = jnp.zeros((LANES,), f32)
    for off, vec in pieces:
        row = lax.dynamic_update_slice(row, vec.reshape(-1).astype(f32), (off,))
    return row.reshape(1, LANES)


def kernel(x, c, norm_g, w_ada, b_ada, w_in, b_fgate, fox_qn_g, fox_kn_g, gdn_conv_w, gdn_A_log, gdn_dt_bias, gdn_norm_g, w_out, final_g, loss_target, m_norm_g, m_w_ada, m_b_ada, m_w_in, m_b_fgate, m_fox_qn_g, m_fox_kn_g, m_gdn_conv_w, m_gdn_A_log, m_gdn_dt_bias, m_gdn_norm_g, m_w_out, m_final_g, v_norm_g, v_w_ada, v_b_ada, v_w_in, v_b_fgate, v_fox_qn_g, v_fox_kn_g, v_gdn_conv_w, v_gdn_A_log, v_gdn_dt_bias, v_gdn_norm_g, v_w_out, v_final_g):
    me = _my_index()
    s_len = x.shape[1]
    nq = s_len // FOX_T
    x2 = x.reshape(s_len, D_MODEL)
    tgt = loss_target.reshape(s_len, D_MODEL)
    ada_cols = w_ada.shape[2]
    in_cols = w_in.shape[2]
    conv_cols = gdn_conv_w.shape[2]

    (c_all,) = _exchange("gather_c", [c], scatter=False)
    c_all = c_all.reshape(N_DEV, D_MODEL)
    b_shard = lax.dynamic_slice(b_ada, (0, me * ada_cols), (1, ada_cols))
    mod_mine = _mod_shard(c_all, w_ada[0], b_shard)
    mod_all, w_in_all, w_out_all, conv_all = _exchange(
        "gather_weights", [mod_mine, w_in[0].astype(bf16), w_out[0].astype(bf16), gdn_conv_w[0]], scatter=False)
    mod = lax.dynamic_slice(mod_all, (0, me, 0), (N_DEV, 1, ada_cols)).reshape(1, 3 * D_MODEL)
    shift, scale, gate = mod[:, :D_MODEL], mod[:, D_MODEL:2 * D_MODEL], mod[:, 2 * D_MODEL:]
    scale1p = 1.0 + scale
    w_in_full = jnp.transpose(w_in_all, (1, 0, 2)).reshape(D_MODEL, N_DEV * in_cols)
    g0 = 4 * WIDTH + HEADS
    w_main = jnp.concatenate([w_in_full[:, :4 * WIDTH], w_in_full[:, g0:g0 + 4 * WIDTH]], axis=1)
    w_small = jnp.concatenate([w_in_full[:, 4 * WIDTH:g0], w_in_full[:, g0 + 4 * WIDTH:],
                               jnp.zeros((D_MODEL, N_SMALL - 3 * HEADS), bf16)], axis=1)
    w_out_full = w_out_all.reshape(2 * WIDTH, D_MODEL)
    conv_full = jnp.transpose(conv_all, (1, 0, 2)).reshape(CONV_K, 3 * WIDTH)

    qn_g, kn_g, gn_g = fox_qn_g.reshape(1, LANES), fox_kn_g.reshape(1, LANES), gdn_norm_g.reshape(1, LANES)
    bvec = _lanes((0, b_fgate), (HEADS, gdn_dt_bias))
    alog = _lanes((HEADS, gdn_A_log))
    fg = final_g.reshape(1, D_MODEL)

    p_main, p_small, h_bf = _in_proj(x2, norm_g, scale1p, shift, w_main, w_small)
    qs, kn, vb, gq, gk, gv, small, gc_b, gl_b, beta_b = _prep(p_main, p_small, qn_g, kn_g, conv_full, bvec, alog)
    f_heads = jnp.transpose(small[:, :HEADS])
    f_col = f_heads.reshape(HEADS, s_len, 1)
    f_row = f_heads.reshape(HEADS, nq, 1, FOX_T)
    fox_o, lse = _fox_fwd(qs, kn, vb, f_col, f_row)
    gu, gw, gqg, gkd, gattn, t_inv, eg_last = _gdn_intra_fwd(gq, gk, gv, gc_b, gl_b, beta_b)
    gdn_o, states = _gdn_scan_fwd(gu, gw, gqg, gkd, gattn, eg_last)
    mixed = _mix_fwd(fox_o, gdn_o, p_main, gn_g)

    loss_row, dy1, dz, dmixed, d_final_g, d_gate = _out_head(mixed, w_out_full, x2, tgt, gate, fg)
    loss = lax.psum(loss_row[0, 0], AXES)
    dw_out = _matmul_tn("dw_out", mixed, dz)
    do_fox, delta, dfz, dgz, dgdn_o, d_gn_g = _mix_bwd(dmixed, fox_o, gdn_o, p_main, gn_g)
    delta_col = jnp.transpose(delta[:, :HEADS]).reshape(HEADS, s_len, 1)
    dqs, dkn, dvf, df_key, df_query = _fox_bwd(qs, kn, vb, do_fox, f_col - lse, delta_col, f_row)
    du, dw, dqg, dkd, dattn, deg = _gdn_scan_bwd(gu, gw, gqg, gkd, gattn, eg_last, states, dgdn_o)
    dgq, dgk, dgv, dgc_b, dgl_b, dbeta_b = _gdn_intra_bwd(gq, gk, gv, gc_b, gl_b, beta_b, t_inv, du, dw, dqg, dkd,
                                                          dattn, deg)
    df_heads = df_key.reshape(HEADS, s_len) + df_query.reshape(HEADS, s_len)
    df_small = jnp.pad(jnp.transpose(df_heads), ((0, 0), (0, N_SMALL - HEADS)))
    dfq, dfk, dconv, dp_small, d_qn_g, d_kn_g, sums = _post1(
        p_main, p_small, qn_g, kn_g, conv_full, bvec, alog, dqs, dkn, dgq, dgk, dgv, dgc_b, dgl_b, dbeta_b, df_small)
    dgqkv, d_conv = _post2(p_main, dconv, conv_full)
    dp_main = jnp.concatenate([dfq, dfk, dvf.astype(bf16), dfz, dgqkv, dgz], axis=1)
    grad_x, d_shift, d_scale, d_norm_g = _in_proj_bwd(dp_main, dp_small, w_main, w_small, x2, dy1, norm_g, scale1p)
    dw_main = _matmul_tn("dw_main", h_bf, dp_main)
    dw_small = _matmul_tn("dw_small", h_bf, dp_small)
    dw_in_full = jnp.concatenate([dw_main[:, :4 * WIDTH], dw_small[:, :HEADS], dw_main[:, 4 * WIDTH:],
                                  dw_small[:, HEADS:3 * HEADS]], axis=1)
    dw_in_parts = jnp.transpose(dw_in_full.reshape(D_MODEL, N_DEV, in_cols), (1, 0, 2)).astype(bf16)
    dw_out_parts = dw_out.reshape(N_DEV, w_out.shape[1], D_MODEL).astype(bf16)

    dmod = jnp.concatenate([d_shift, d_scale, d_gate], axis=1)
    small_grads = _pack([d_norm_g, dmod, sums[0, :HEADS], d_qn_g, d_kn_g, sums[1, HEADS:2 * HEADS],
                         sums[0, HEADS:2 * HEADS], d_gn_g, d_final_g], SMALL_PACK)
    conv_grad = d_conv[:CONV_K]
    dw_in_recv, dw_out_recv = _exchange("scatter_grads", [dw_in_parts, dw_out_parts], scatter=True)
    small_all, conv_all_g = _exchange("gather_small_grads", [small_grads, conv_grad], scatter=False)

    outs = {}
    outs["w_in"] = _adamw("adamw_w_in", w_in[0], dw_in_recv, m_w_in[0], v_w_in[0], 128)
    outs["w_out"] = _adamw("adamw_w_out", w_out[0], dw_out_recv, m_w_out[0], v_w_out[0], 128)
    conv_mine = lax.dynamic_slice(jnp.transpose(conv_all_g.reshape(N_DEV, CONV_K, N_DEV, conv_cols), (0, 2, 1, 3)),
                                  (0, me, 0, 0), (N_DEV, 1, CONV_K, conv_cols)).reshape(N_DEV, CONV_K, conv_cols)
    outs["gdn_conv_w"] = _adamw("adamw_conv", gdn_conv_w[0], conv_mine, m_gdn_conv_w[0], v_gdn_conv_w[0], CONV_K)
    small_all = small_all.reshape(N_DEV, 1, SMALL_PACK)
    dmod_all = small_all[:, 0, D_MODEL:D_MODEL + 3 * D_MODEL]
    dmod_mine = lax.dynamic_slice(dmod_all, (0, me * ada_cols), (N_DEV, ada_cols))
    c_all_t = jnp.pad(jnp.transpose(c_all), ((0, 0), (0, LANES - N_DEV)))
    g_w_ada = _w_ada_grad(c_all_t, jnp.pad(dmod_mine, ((0, LANES - N_DEV), (0, 0))))
    outs["w_ada"] = _adamw("adamw_w_ada", w_ada[0], g_w_ada[None], m_w_ada[0], v_w_ada[0], 256)
    given = dict(norm_g=(norm_g, m_norm_g, v_norm_g), b_ada=(b_ada, m_b_ada, v_b_ada), b_fgate=(b_fgate, m_b_fgate, v_b_fgate),
                 fox_qn_g=(fox_qn_g, m_fox_qn_g, v_fox_qn_g), fox_kn_g=(fox_kn_g, m_fox_kn_g, v_fox_kn_g),
                 gdn_A_log=(gdn_A_log, m_gdn_A_log, v_gdn_A_log), gdn_dt_bias=(gdn_dt_bias, m_gdn_dt_bias, v_gdn_dt_bias),
                 gdn_norm_g=(gdn_norm_g, m_gdn_norm_g, v_gdn_norm_g), final_g=(final_g, m_final_g, v_final_g))
    w_pack = _pack([given[n][0] for n in SMALL_NAMES], SMALL_PACK)
    m_pack = _pack([given[n][1] for n in SMALL_NAMES], SMALL_PACK)
    v_pack = _pack([given[n][2] for n in SMALL_NAMES], SMALL_PACK)
    packed = _adamw("adamw_small", w_pack, small_all, m_pack, v_pack, 1)
    off = 0
    for n, size in zip(SMALL_NAMES, SMALL_SIZES):
        outs[n] = tuple(t[0, off:off + size].reshape(given[n][0].shape) for t in packed)
        off += size
    for n in ("w_in", "w_out", "gdn_conv_w", "w_ada"):
        outs[n] = tuple(t[None] for t in outs[n])

    order = ("norm_g", "w_ada", "b_ada", "w_in", "b_fgate", "fox_qn_g", "fox_kn_g", "gdn_conv_w", "gdn_A_log",
             "gdn_dt_bias", "gdn_norm_g", "w_out", "final_g")
    result = [loss, grad_x.reshape(x.shape)]
    for part in range(4):
        result += [outs[n][part] for n in order]
    return tuple(result)
```

```python
import math

import jax
import jax.numpy as jnp
from jax import lax
from jax.experimental import pallas as pl
from jax.experimental.pallas import tpu as pltpu

f32 = jnp.float32
bf16 = jnp.bfloat16
HI = lax.Precision.HIGHEST

N_DEV = 8
AXES = ("x", "y", "c")
D_MODEL = 2048
HEADS = 8
HEAD_DIM = 128
WIDTH = HEADS * HEAD_DIM
CHUNK = 64
CONV_K = 4
EPS = 1e-6
QK_SCALE = HEAD_DIM ** -0.5
N_MAIN = 8 * WIDTH
N_SMALL = 128
IN_WIDTH = 8 * WIDTH + 3 * HEADS
LANES = 128
VMEM_LIMIT = 56 * 1024 * 1024

ADAM_LR, ADAM_B1, ADAM_B2, ADAM_EPS, ADAM_WD, ADAM_STEP = 0.001, 0.9, 0.999, 1e-08, 0.01, 10


def _params(*sem):
    return pltpu.CompilerParams(dimension_semantics=sem, vmem_limit_bytes=VMEM_LIMIT)


def _iota(shape, dim):
    return lax.broadcasted_iota(jnp.int32, shape, dim)


def _sigmoid(z):
    return 1.0 / (1.0 + jnp.exp(-z))


def _softplus_parts(z):
    t = jnp.log(1.0 + jnp.exp(-jnp.abs(z)))
    return jnp.minimum(z, 0.0) - t, jnp.maximum(z, 0.0) + t


def _dg(a, b, ca, cb, prec=None):
    return lax.dot_general(a, b, (((ca,), (cb,)), ((), ())), preferred_element_type=f32, precision=prec)


def _dot_bf16(a, b, ca, cb):
    return _dg(a.astype(bf16), b.astype(bf16), ca, cb)


def _split_bf16(a):
    hi = a.astype(bf16)
    return hi, (a - hi.astype(f32)).astype(bf16)


def _dot_3pass(a, b, ca, cb):
    a_hi, a_lo = _split_bf16(a)
    b_hi, b_lo = _split_bf16(b)
    return _dg(a_hi, b_hi, ca, cb) + (_dg(a_hi, b_lo, ca, cb) + _dg(a_lo, b_hi, ca, cb))


def _make_mm(dot):
    def nn_(a, b):
        return dot(a, b, 1, 0)

    def nt_(a, b):
        return dot(a, b, 1, 1)

    def tn_(a, b):
        return dot(a, b, 0, 0)

    @jax.custom_vjp
    def nn(a, b):
        return nn_(a, b)

    @jax.custom_vjp
    def nt(a, b):
        return nt_(a, b)

    @jax.custom_vjp
    def tn(a, b):
        return tn_(a, b)

    nn.defvjp(lambda a, b: (nn_(a, b), (a, b)), lambda r, g: (nt_(g, r[1]), tn_(r[0], g)))
    nt.defvjp(lambda a, b: (nt_(a, b), (a, b)), lambda r, g: (nn_(g, r[1]), tn_(g, r[0])))
    tn.defvjp(lambda a, b: (tn_(a, b), (a, b)), lambda r, g: (nt_(r[1], g), nn_(r[0], g)))
    return (nn_, nt_, tn_), (nn, nt, tn)


_BF_PLAIN, _BF_VJP = _make_mm(_dot_bf16)
_X3_PLAIN, _X3_VJP = _make_mm(_dot_3pass)


def _inv_unit_lower_many(ms):
    c = CHUNK
    nn = _X3_PLAIN[0]
    eye = (_iota((c, c), 0) == _iota((c, c), 1)).astype(f32)
    top = _iota((2 * c, c), 0) < c
    xs = [jnp.concatenate([eye - m, nn(m, m)], axis=0) for m in ms]
    for _ in range(int(math.log2(CHUNK)) - 2):
        xs = [jnp.where(top, x, 0.0) + nn(x, x[c:]) for x in xs]
    return [x[:c] + nn(x[:c], x[c:]) for x in xs]


@jax.custom_vjp
def _inv_given(m, t):
    return t


_inv_given.defvjp(lambda m, t: (t, t),
                  lambda t, g: (-_X3_PLAIN[1](_X3_PLAIN[2](t, g), t), jnp.zeros_like(t)))

SUBLANES = 8


def _gdn_intra_pre(q, k, v, gc_b, g_last_b, beta_b, bnt):
    c = CHUNK
    r_i, c_i = _iota((c, c), 0), _iota((c, c), 1)
    lower, strict = r_i >= c_i, r_i > c_i
    gc_i = gc_b[:, :c]
    gc_j = gc_i.T
    decay = jnp.where(lower, jnp.exp(jnp.where(lower, gc_i - gc_j, 0.0)), 0.0)
    kb = k * beta_b
    both = bnt(jnp.concatenate([kb, q], axis=0), k)
    m = jnp.where(strict, both[:c] * decay, 0.0)
    attn = jnp.where(lower, both[c:] * decay, 0.0)
    eg = jnp.exp(gc_b)
    rhs = jnp.concatenate([v * beta_b, kb * eg], axis=1)
    k_dec = k * jnp.exp(g_last_b - gc_b)
    eg_last = jnp.exp(g_last_b[:SUBLANES])
    return m, rhs, q * eg, k_dec, attn, eg_last


def _gdn_intra(q, k, v, gc_b, g_last_b, beta_b, t_known):
    m, rhs, qg, k_dec, attn, eg_last = _gdn_intra_pre(q, k, v, gc_b, g_last_b, beta_b, _BF_VJP[1])
    return _X3_VJP[0](_inv_given(m, t_known), rhs), qg, k_dec, attn, eg_last


def _scale_rows(s, eg_last):
    return (s.reshape(HEAD_DIM // SUBLANES, SUBLANES, HEAD_DIM) * eg_last[None]).reshape(HEAD_DIM, HEAD_DIM)


def _my_index():
    return 4 * lax.axis_index("x") + 2 * lax.axis_index("y") + lax.axis_index("c")


def _peer(d):
    x, y, c = lax.axis_index("x"), lax.axis_index("y"), lax.axis_index("c")
    px, py, pc = (x + (d >> 2)) % 2, (y + ((d >> 1) & 1)) % 2, (c + (d & 1)) % 2
    return (px, py, pc), 4 * px + 2 * py + pc


def _exchange(name, arrays, scatter):
    n = len(arrays)

    def body(*refs):
        srcs, dsts = refs[:n], refs[n:2 * n]
        send_sems, recv_sems, local_sems = refs[2 * n:]
        me = _my_index()

        def remote(k, d):
            peer, pidx = _peer(d)
            src = srcs[k].at[pidx] if scatter else srcs[k]
            return pltpu.make_async_remote_copy(
                src_ref=src, dst_ref=dsts[k].at[me], send_sem=send_sems.at[k * 7 + d - 1],
                recv_sem=recv_sems.at[k * 7 + d - 1], device_id=peer, device_id_type=pl.DeviceIdType.MESH)

        def arrival(k, d):
            peer, pidx = _peer(d)
            src = srcs[k].at[pidx] if scatter else srcs[k]
            return pltpu.make_async_remote_copy(
                src_ref=src, dst_ref=dsts[k].at[pidx], send_sem=send_sems.at[k * 7 + d - 1],
                recv_sem=recv_sems.at[k * 7 + d - 1], device_id=peer, device_id_type=pl.DeviceIdType.MESH)

        local = [pltpu.make_async_copy(srcs[k].at[me] if scatter else srcs[k], dsts[k].at[me], local_sems.at[k])
                 for k in range(n)]
        sends = [remote(k, d) for k in range(n) for d in range(1, N_DEV)]
        for cp in local + sends:
            cp.start()
        for k in range(n):
            for d in range(1, N_DEV):
                arrival(k, d).wait_recv()
        for cp in sends:
            cp.wait_send()
        for cp in local:
            cp.wait()

    if scatter:
        out_shape = [jax.ShapeDtypeStruct(a.shape, a.dtype) for a in arrays]
    else:
        out_shape = [jax.ShapeDtypeStruct((N_DEV,) + a.shape, a.dtype) for a in arrays]
    any_spec = pl.BlockSpec(memory_space=pl.ANY)
    return pl.pallas_call(
        body, name=name, out_shape=out_shape, in_specs=[any_spec] * n, out_specs=[any_spec] * n,
        scratch_shapes=[pltpu.SemaphoreType.DMA((7 * n,)), pltpu.SemaphoreType.DMA((7 * n,)),
                        pltpu.SemaphoreType.DMA((n,))],
        compiler_params=pltpu.CompilerParams(has_side_effects=True),
    )(*arrays)


def _gather_two_level(name, arrays):
    n = len(arrays)

    def body(*refs):
        srcs, dsts = refs[:n], refs[n:2 * n]
        send_sems, recv_sems, local_sems = refs[2 * n:]
        x, y, c = lax.axis_index("x"), lax.axis_index("y"), lax.axis_index("c")
        sibling = (x, y, 1 - c)
        chips = [((x + 1) % 2, y), (x, (y + 1) % 2), ((x + 1) % 2, (y + 1) % 2)]

        def index(px, py, pc):
            return 4 * px + 2 * py + pc

        def copy(k, slot, block, to, src=None):
            return pltpu.make_async_remote_copy(
                src_ref=dsts[k].at[index(*block)] if src is None else src, dst_ref=dsts[k].at[index(*block)],
                send_sem=send_sems.at[k * 7 + slot], recv_sem=recv_sems.at[k * 7 + slot],
                device_id=to, device_id_type=pl.DeviceIdType.MESH)

        me = (x, y, c)
        local = [pltpu.make_async_copy(srcs[k], dsts[k].at[index(*me)], local_sems.at[k]) for k in range(n)]
        first = [copy(k, 0, me, sibling, src=srcs[k]) for k in range(n)]
        first += [copy(k, 1 + j, me, (*chip, c), src=srcs[k]) for j, chip in enumerate(chips) for k in range(n)]
        for cp in local + first:
            cp.start()
        passed = []
        for j, chip in enumerate(chips):
            for k in range(n):
                copy(k, 1 + j, (*chip, c), me).wait_recv()
                fwd = copy(k, 4 + j, (*chip, c), sibling)
                fwd.start()
                passed.append(fwd)
        for k in range(n):
            copy(k, 0, sibling, me).wait_recv()
            for j, chip in enumerate(chips):
                copy(k, 4 + j, (*chip, 1 - c), me).wait_recv()
        for cp in first + passed:
            cp.wait_send()
        for cp in local:
            cp.wait()

    any_spec = pl.BlockSpec(memory_space=pl.ANY)
    return pl.pallas_call(
        body, name=name, out_shape=[jax.ShapeDtypeStruct((N_DEV,) + a.shape, a.dtype) for a in arrays],
        in_specs=[any_spec] * n, out_specs=[any_spec] * n,
        scratch_shapes=[pltpu.SemaphoreType.DMA((7 * n,)), pltpu.SemaphoreType.DMA((7 * n,)),
                        pltpu.SemaphoreType.DMA((n,))],
        compiler_params=pltpu.CompilerParams(has_side_effects=True),
    )(*arrays)


def _mod_shard(c_all, w_ada, b_shard):
    def body(c_ref, w_ref, b_ref, o_ref):
        cv = c_ref[...]
        ca = cv * _sigmoid(cv)
        o_ref[...] = jnp.dot(ca.astype(bf16), w_ref[...].astype(bf16), preferred_element_type=f32) + b_ref[...]

    return pl.pallas_call(body, name="mod_shard", out_shape=jax.ShapeDtypeStruct((N_DEV, w_ada.shape[1]), f32),
                          compiler_params=_params())(c_all, w_ada, b_shard)


def _in_proj(x, norm_g, scale1p, shift, w_main, w_small):
    s_len = x.shape[0]
    tm, tn = 512, 1024

    def body(x_ref, g_ref, sc_ref, sh_ref, w_ref, ws_ref, p_ref, ps_ref, h_ref, h_sc):
        @pl.when(pl.program_id(1) == 0)
        def _():
            xb = x_ref[...]
            r = lax.rsqrt(jnp.mean(xb * xb, axis=-1, keepdims=True) + EPS)
            hb = ((xb * r * g_ref[...]) * sc_ref[...] + sh_ref[...]).astype(bf16)
            h_sc[...] = hb
            h_ref[...] = hb
            ps_ref[...] = jnp.dot(hb, ws_ref[...], preferred_element_type=f32)

        p_ref[...] = jnp.dot(h_sc[...], w_ref[...], preferred_element_type=f32)

    vec = pl.BlockSpec((1, D_MODEL), lambda i, j: (0, 0))
    return pl.pallas_call(
        body, name="in_proj", grid=(s_len // tm, N_MAIN // tn),
        in_specs=[pl.BlockSpec((tm, D_MODEL), lambda i, j: (i, 0)), vec, vec, vec,
                  pl.BlockSpec((D_MODEL, tn), lambda i, j: (0, j)),
                  pl.BlockSpec((D_MODEL, N_SMALL), lambda i, j: (0, 0))],
        out_specs=[pl.BlockSpec((tm, tn), lambda i, j: (i, j)),
                   pl.BlockSpec((tm, N_SMALL), lambda i, j: (i, 0)),
                   pl.BlockSpec((tm, D_MODEL), lambda i, j: (i, 0))],
        out_shape=[jax.ShapeDtypeStruct((s_len, N_MAIN), f32), jax.ShapeDtypeStruct((s_len, N_SMALL), f32),
                   jax.ShapeDtypeStruct((s_len, D_MODEL), bf16)],
        scratch_shapes=[pltpu.VMEM((tm, D_MODEL), bf16)],
        compiler_params=_params("parallel", "arbitrary"),
    )(x, norm_g, scale1p, shift, w_main, w_small)


PREP_TM = 256
HALO = 8


def _conv_section(xe_ref, cw_ref, cols, tm):
    acc = cw_ref[pl.ds(CONV_K - 1, 1), cols] * xe_ref[pl.ds(HALO, tm), :]
    for tap in range(CONV_K - 1):
        acc = acc + cw_ref[pl.ds(tap, 1), cols] * xe_ref[pl.ds(HALO - (CONV_K - 1) + tap, tm), :]
    return acc


def _small_fwd(ps, bvec, alog):
    z = ps + bvec
    logsig, softp = _softplus_parts(z)
    gval = -jnp.exp(alog) * softp
    beta = _sigmoid(ps)
    return z, logsig, gval, beta


def _lane_group_selector(first_lane):
    return (_iota((LANES, WIDTH), 0) == first_lane + _iota((LANES, WIDTH), 1) // HEAD_DIM).astype(f32)


def _chunk_masks(tm):
    r, c = _iota((tm, tm), 0), _iota((tm, tm), 1)
    same = (r // CHUNK) == (c // CHUNK)
    return (same & (r >= c)).astype(f32), same.astype(f32)


def _prep(p_main, p_small, qn_g, kn_g, conv_w, bvec, alog):
    s_len = p_main.shape[0]
    tm = PREP_TM
    nb = s_len // tm

    def body(fq_ref, fk_ref, fv_ref, gq_ref, gk_ref, gv_ref, hq_ref, hk_ref, hv_ref, ps_ref, qg_ref, kg_ref,
             cw_ref, bv_ref, al_ref,
             qs_ref, kn_ref, vb_ref, gqo_ref, gko_ref, gvo_ref, small_ref, gcb_ref, glb_ref, bb_ref, xe_sc, carry_sc):
        i = pl.program_id(0)

        @pl.when(i == 0)
        def _():
            carry_sc[...] = jnp.zeros_like(carry_sc)

        qg, kg = qg_ref[...], kg_ref[...]
        for h in range(HEADS):
            sl = slice(h * HEAD_DIM, (h + 1) * HEAD_DIM)
            q = fq_ref[:, sl]
            rq = lax.rsqrt(jnp.mean(q * q, axis=-1, keepdims=True) + EPS)
            qs_ref[:, sl] = (q * rq * qg * QK_SCALE).astype(bf16)
            k = fk_ref[:, sl]
            rk = lax.rsqrt(jnp.mean(k * k, axis=-1, keepdims=True) + EPS)
            kn_ref[:, sl] = (k * rk * kg).astype(bf16)
        vb_ref[...] = fv_ref[...].astype(bf16)

        first = i == 0
        for sec, (x_ref, halo_ref, o_ref) in enumerate(((gq_ref, hq_ref, gqo_ref), (gk_ref, hk_ref, gko_ref),
                                                        (gv_ref, hv_ref, gvo_ref))):
            xe_sc[0:HALO, :] = jnp.where(first, 0.0, halo_ref[...])
            xe_sc[HALO:, :] = x_ref[...]
            cv = _conv_section(xe_sc, cw_ref, slice(sec * WIDTH, (sec + 1) * WIDTH), tm)
            y = cv * _sigmoid(cv)
            if sec == 2:
                o_ref[...] = y
            else:
                mul = QK_SCALE if sec == 0 else 1.0
                for h in range(HEADS):
                    sl = slice(h * HEAD_DIM, (h + 1) * HEAD_DIM)
                    yh = y[:, sl]
                    o_ref[:, sl] = yh * (lax.rsqrt(jnp.sum(yh * yh, axis=-1, keepdims=True) + EPS) * mul)

        lane = _iota((tm, N_SMALL), 1)
        _, logsig, gval, beta = _small_fwd(ps_ref[...], bv_ref[...], al_ref[...])
        lf = jnp.where(lane < HEADS, logsig, 0.0)
        tri = (_iota((tm, tm), 0) >= _iota((tm, tm), 1)).astype(f32)
        fcum = jnp.dot(tri, lf, preferred_element_type=f32, precision=HI) + carry_sc[...]
        carry_sc[...] += jnp.sum(lf, axis=0, keepdims=True)
        small = jnp.where(lane < HEADS, fcum, jnp.where(lane < 2 * HEADS, gval, jnp.where(lane < 3 * HEADS, beta, 0.0)))
        small_ref[...] = small
        tri_c, ones_c = _chunk_masks(tm)
        g_lanes = jnp.where((lane >= HEADS) & (lane < 2 * HEADS), gval, 0.0)
        sel_g = _lane_group_selector(HEADS)
        gc = jnp.dot(tri_c, g_lanes, preferred_element_type=f32, precision=HI)
        gcb_ref[...] = jnp.dot(gc, sel_g, preferred_element_type=f32, precision=HI)
        g_last = jnp.dot(ones_c, g_lanes, preferred_element_type=f32, precision=HI)
        glb_ref[...] = jnp.dot(g_last, sel_g, preferred_element_type=f32, precision=HI)
        bb_ref[...] = jnp.dot(small, _lane_group_selector(2 * HEADS), preferred_element_type=f32, precision=HI)

    def col(cb):
        return pl.BlockSpec((tm, WIDTH), lambda i: (i, cb))

    def halo(cb):
        return pl.BlockSpec((HALO, WIDTH), lambda i: (jnp.maximum(i * (tm // HALO) - 1, 0), cb))

    vec = pl.BlockSpec((1, LANES), lambda i: (0, 0))
    wide_f32 = jax.ShapeDtypeStruct((s_len, WIDTH), f32)
    wide_bf = jax.ShapeDtypeStruct((s_len, WIDTH), bf16)
    out_col = pl.BlockSpec((tm, WIDTH), lambda i: (i, 0))
    return pl.pallas_call(
        body, name="prep", grid=(nb,),
        in_specs=[col(0), col(1), col(2), col(4), col(5), col(6), halo(4), halo(5), halo(6),
                  pl.BlockSpec((tm, N_SMALL), lambda i: (i, 0)), vec, vec,
                  pl.BlockSpec((CONV_K, 3 * WIDTH), lambda i: (0, 0)), vec, vec],
        out_specs=[out_col] * 6 + [pl.BlockSpec((tm, N_SMALL), lambda i: (i, 0)), out_col, out_col, out_col],
        out_shape=[wide_bf, wide_bf, wide_bf, wide_f32, wide_f32, wide_f32,
                   jax.ShapeDtypeStruct((s_len, N_SMALL), f32), wide_f32, wide_f32, wide_f32],
        scratch_shapes=[pltpu.VMEM((tm + HALO, WIDTH), f32), pltpu.VMEM((1, N_SMALL), f32)],
        compiler_params=_params("arbitrary"),
    )(p_main, p_main, p_main, p_main, p_main, p_main, p_main, p_main, p_main, p_small, qn_g, kn_g, conv_w, bvec, alog)


FOX_T = 1024
NEG_BIG = -1e30


def _head_lane(block, head):
    return jnp.sum(jnp.where(_iota(block.shape, 1) == head, block, 0.0), axis=1, keepdims=True)


def _fox_fwd(qs, kn, vb, small, f_row):
    s_len = qs.shape[0]
    t = FOX_T
    nq = s_len // t

    def body(q_ref, k_ref, v_ref, sm_ref, fr_ref, o_ref, lse_ref):
        qi = pl.program_id(1)
        q = q_ref[...]
        fq = _head_lane(sm_ref[...], pl.program_id(0))
        causal = _iota((t, t), 0) >= _iota((t, t), 1)

        def step(j, carry, masked):
            m, l, acc = carry
            rows = pl.ds(pl.multiple_of(j * t, t), t)
            s = _dg(q, k_ref[rows, :], 1, 1) + (fq - fr_ref[0, j])
            if masked:
                s = jnp.where(causal, s, NEG_BIG)
            m_new = jnp.maximum(m, jnp.max(s, axis=-1, keepdims=True))
            p = jnp.exp(s - m_new)
            alpha = jnp.exp(m - m_new)
            l = alpha * l + jnp.sum(p, axis=-1, keepdims=True)
            acc = alpha * acc + jnp.dot(p.astype(bf16), v_ref[rows, :], preferred_element_type=f32)
            return m_new, l, acc

        init = (jnp.full((t, 1), NEG_BIG, f32), jnp.zeros((t, 1), f32), jnp.zeros((t, HEAD_DIM), f32))
        carry = lax.fori_loop(0, qi, lambda j, c: step(j, c, False), init)
        m, l, acc = step(qi, carry, True)
        o_ref[...] = acc / l
        lse_ref[0] = m + jnp.log(l)

    return pl.pallas_call(
        body, name="fox_fwd", grid=(HEADS, nq),
        in_specs=[pl.BlockSpec((t, HEAD_DIM), lambda h, i: (i, h)),
                  pl.BlockSpec((s_len, HEAD_DIM), lambda h, i: (0, h)),
                  pl.BlockSpec((s_len, HEAD_DIM), lambda h, i: (0, h)),
                  pl.BlockSpec((t, N_SMALL), lambda h, i: (i, 0)),
                  pl.BlockSpec((1, nq, 1, t), lambda h, i: (h, 0, 0, 0))],
        out_specs=[pl.BlockSpec((t, HEAD_DIM), lambda h, i: (i, h)),
                   pl.BlockSpec((1, t, 1), lambda h, i: (h, i, 0))],
        out_shape=[jax.ShapeDtypeStruct((s_len, WIDTH), f32), jax.ShapeDtypeStruct((HEADS, s_len, 1), f32)],
        compiler_params=_params("parallel", "arbitrary"),
    )(qs, kn, vb, small, f_row)


def _fox_bwd(qs, kn, vb, do, small, lse, delta, f_row):
    s_len = qs.shape[0]
    t = FOX_T
    nq = s_len // t

    def body(q_ref, do_ref, sm_ref, lse_ref, dl_ref, k_ref, v_ref, fr_ref, dq_ref, dk_ref, dv_ref, df_ref, dfq_ref):
        head, qi = pl.program_id(0), pl.program_id(1)

        @pl.when(qi == 0)
        def _():
            dk_ref[...] = jnp.zeros_like(dk_ref)
            dv_ref[...] = jnp.zeros_like(dv_ref)
            df_ref[...] = jnp.zeros_like(df_ref)

        q, do_b = q_ref[...], do_ref[...]
        a = _head_lane(sm_ref[...], head) - lse_ref[0]
        dl = _head_lane(dl_ref[...], head)
        causal = _iota((t, t), 0) >= _iota((t, t), 1)

        def step(j, carry, masked):
            dq, row_sum = carry
            rows = pl.ds(pl.multiple_of(j * t, t), t)
            kj, vj = k_ref[rows, :], v_ref[rows, :]
            p = jnp.exp(_dg(q, kj, 1, 1) + (a - fr_ref[0, j]))
            if masked:
                p = jnp.where(causal, p, 0.0)
            ds = p * (_dg(do_b, vj, 1, 1) - dl)
            ds_b = ds.astype(bf16)
            dk_ref[rows, :] += _dg(ds_b, q, 0, 0)
            dv_ref[rows, :] += _dg(p.astype(bf16), do_b, 0, 0)
            df_ref[0, j] += -jnp.sum(ds, axis=0, keepdims=True)
            return dq + jnp.dot(ds_b, kj, preferred_element_type=f32), row_sum + jnp.sum(ds, axis=-1, keepdims=True)

        carry = lax.fori_loop(0, qi, lambda j, c: step(j, c, False),
                              (jnp.zeros((t, HEAD_DIM), f32), jnp.zeros((t, 1), f32)))
        dq, row_sum = step(qi, carry, True)
        dq_ref[...] = dq
        dfq_ref[0] = row_sum

    blk = pl.BlockSpec((t, HEAD_DIM), lambda h, i: (i, h))
    full = pl.BlockSpec((s_len, HEAD_DIM), lambda h, i: (0, h))
    colv = pl.BlockSpec((1, t, 1), lambda h, i: (h, i, 0))
    rowv = pl.BlockSpec((1, nq, 1, t), lambda h, i: (h, 0, 0, 0))
    lanes = pl.BlockSpec((t, N_SMALL), lambda h, i: (i, 0))
    wide = jax.ShapeDtypeStruct((s_len, WIDTH), f32)
    return pl.pallas_call(
        body, name="fox_bwd", grid=(HEADS, nq),
        in_specs=[blk, blk, lanes, colv, lanes, full, full, rowv],
        out_specs=[blk, full, full, rowv, colv],
        out_shape=[wide, wide, wide, jax.ShapeDtypeStruct((HEADS, nq, 1, t), f32),
                   jax.ShapeDtypeStruct((HEADS, s_len, 1), f32)],
        compiler_params=_params("parallel", "arbitrary"),
    )(qs, do, small, lse, delta, kn, vb, f_row)


INTRA_CHUNKS = 8
INTRA_INTERLEAVE = 4
SCAN_FWD_CHUNKS = 8
SCAN_BWD_CHUNKS = 4


def _gdn_intra_fwd(gq, gk, gv, gc_b, g_last_b, beta_b):
    s_len = gq.shape[0]
    cpb = INTRA_CHUNKS
    rows_blk = cpb * CHUNK
    n_chunks = s_len // CHUNK

    def body(q_ref, k_ref, v_ref, gc_ref, gl_ref, b_ref, u_ref, w_ref, qg_ref, kd_ref, attn_ref, t_ref, eg_ref):
        ms, rhss = [], []
        for ci in range(cpb):
            rows = pl.ds(ci * CHUNK, CHUNK)
            m, rhs, qg, kd, attn, eg_last = _gdn_intra_pre(q_ref[rows, :], k_ref[rows, :], v_ref[rows, :],
                                                           gc_ref[rows, :], gl_ref[rows, :], b_ref[rows, :],
                                                           _BF_PLAIN[1])
            qg_ref[rows, :] = qg.astype(bf16)
            kd_ref[rows, :] = kd.astype(bf16)
            attn_ref[0, ci] = attn.astype(bf16)
            eg_ref[0, ci] = eg_last
            ms.append(m)
            rhss.append(rhs)
        for ci, (t, rhs) in enumerate(zip(_inv_unit_lower_many(ms), rhss)):
            rows = pl.ds(ci * CHUNK, CHUNK)
            t_ref[0, ci] = t
            uw = _X3_PLAIN[0](t, rhs)
            u_ref[rows, :] = uw[:, :HEAD_DIM]
            w_ref[rows, :] = uw[:, HEAD_DIM:].astype(bf16)

    blk = pl.BlockSpec((rows_blk, HEAD_DIM), lambda h, i: (i, h))
    sq = pl.BlockSpec((1, cpb, CHUNK, CHUNK), lambda h, i: (h, i, 0, 0))
    wide_bf = jax.ShapeDtypeStruct((s_len, WIDTH), bf16)
    return pl.pallas_call(
        body, name="gdn_intra_fwd", grid=(HEADS, s_len // rows_blk),
        in_specs=[blk] * 6,
        out_specs=[blk] * 4 + [sq, sq, pl.BlockSpec((1, cpb, SUBLANES, HEAD_DIM), lambda h, i: (h, i, 0, 0))],
        out_shape=[jax.ShapeDtypeStruct((s_len, WIDTH), f32), wide_bf, wide_bf, wide_bf,
                   jax.ShapeDtypeStruct((HEADS, n_chunks, CHUNK, CHUNK), bf16),
                   jax.ShapeDtypeStruct((HEADS, n_chunks, CHUNK, CHUNK), f32),
                   jax.ShapeDtypeStruct((HEADS, n_chunks, SUBLANES, HEAD_DIM), f32)],
        compiler_params=_params("parallel", "parallel"),
    )(gq, gk, gv, gc_b, g_last_b, beta_b)


def _gdn_scan_fwd(u, w, qg, kd, attn, eg):
    s_len = u.shape[0]
    cpb = SCAN_FWD_CHUNKS
    rows_blk = cpb * CHUNK
    n_chunks = s_len // CHUNK

    def body(u_ref, w_ref, qg_ref, kd_ref, attn_ref, eg_ref, o_ref, st_ref, s_sc):
        @pl.when(pl.program_id(0) == 0)
        def _():
            s_sc[...] = jnp.zeros_like(s_sc)

        def chunk(ci, _):
            rows = pl.ds(pl.multiple_of(ci * CHUNK, CHUNK), CHUNK)
            for h in range(HEADS):
                cols = slice(h * HEAD_DIM, (h + 1) * HEAD_DIM)
                s0 = s_sc[h]
                st_ref[h, ci] = s0
                s0_b = s0.astype(bf16)
                v_new = u_ref[rows, cols] - jnp.dot(w_ref[rows, cols], s0_b, preferred_element_type=f32)
                vn_b = v_new.astype(bf16)
                o_ref[rows, cols] = (jnp.dot(qg_ref[rows, cols], s0_b, preferred_element_type=f32)
                                     + jnp.dot(attn_ref[h, ci], vn_b, preferred_element_type=f32))
                s_sc[h] = _scale_rows(s0, eg_ref[h, ci]) + _dg(kd_ref[rows, cols], vn_b, 0, 0)
            return 0

        lax.fori_loop(0, cpb, chunk, 0)

    row = pl.BlockSpec((rows_blk, WIDTH), lambda i: (i, 0))
    return pl.pallas_call(
        body, name="gdn_scan_fwd", grid=(s_len // rows_blk,),
        in_specs=[row] * 4 + [pl.BlockSpec((HEADS, cpb, CHUNK, CHUNK), lambda i: (0, i, 0, 0)),
                              pl.BlockSpec((HEADS, cpb, SUBLANES, HEAD_DIM), lambda i: (0, i, 0, 0))],
        out_specs=[row, pl.BlockSpec((HEADS, cpb, HEAD_DIM, HEAD_DIM), lambda i: (0, i, 0, 0))],
        out_shape=[jax.ShapeDtypeStruct((s_len, WIDTH), f32),
                   jax.ShapeDtypeStruct((HEADS, n_chunks, HEAD_DIM, HEAD_DIM), f32)],
        scratch_shapes=[pltpu.VMEM((HEADS, HEAD_DIM, HEAD_DIM), f32)],
        compiler_params=_params("arbitrary"),
    )(u, w, qg, kd, attn, eg)


def _gdn_scan_bwd(u, w, qg, kd, attn, eg, states, d_o):
    s_len = u.shape[0]
    cpb = SCAN_BWD_CHUNKS
    rows_blk = cpb * CHUNK
    n_chunks = s_len // CHUNK
    nb = s_len // rows_blk

    def body(u_ref, w_ref, qg_ref, kd_ref, attn_ref, eg_ref, st_ref, do_ref,
             du_ref, dw_ref, dqg_ref, dkd_ref, dattn_ref, deg_ref, ds_sc):
        @pl.when(pl.program_id(0) == 0)
        def _():
            ds_sc[...] = jnp.zeros_like(ds_sc)

        def chunk(step, _):
            ci = cpb - 1 - step
            rows = pl.ds(pl.multiple_of(ci * CHUNK, CHUNK), CHUNK)
            for h in range(HEADS):
                cols = slice(h * HEAD_DIM, (h + 1) * HEAD_DIM)
                s0 = st_ref[h, ci]
                s0_b = s0.astype(bf16)
                ds1 = ds_sc[h]
                ds1_b = ds1.astype(bf16)
                w_b, qg_b, kd_b, attn_b = w_ref[rows, cols], qg_ref[rows, cols], kd_ref[rows, cols], attn_ref[h, ci]
                do_b = do_ref[rows, cols].astype(bf16)
                vn_b = (u_ref[rows, cols] - jnp.dot(w_b, s0_b, preferred_element_type=f32)).astype(bf16)
                dvn = _dg(attn_b, do_b, 0, 0) + jnp.dot(kd_b, ds1_b, preferred_element_type=f32)
                dvn_b = dvn.astype(bf16)
                dattn_ref[h, ci] = _dg(do_b, vn_b, 1, 1)
                dqg_ref[rows, cols] = _dg(do_b, s0_b, 1, 1)
                dkd_ref[rows, cols] = _dg(vn_b, ds1_b, 1, 1)
                du_ref[rows, cols] = dvn
                dw_ref[rows, cols] = -_dg(dvn_b, s0_b, 1, 1)
                eg_last = eg_ref[h, ci]
                ds_sc[h] = _dg(qg_b, do_b, 0, 0) - _dg(w_b, dvn_b, 0, 0) + _scale_rows(ds1, eg_last)
                deg_ref[h, ci] = jnp.sum((ds1 * s0).reshape(HEAD_DIM // SUBLANES, SUBLANES, HEAD_DIM), axis=0)
            return 0

        lax.fori_loop(0, cpb, chunk, 0)

    row = pl.BlockSpec((rows_blk, WIDTH), lambda i: (nb - 1 - i, 0))
    sq = pl.BlockSpec((HEADS, cpb, CHUNK, CHUNK), lambda i: (0, nb - 1 - i, 0, 0))
    egs = pl.BlockSpec((HEADS, cpb, SUBLANES, HEAD_DIM), lambda i: (0, nb - 1 - i, 0, 0))
    wide = jax.ShapeDtypeStruct((s_len, WIDTH), f32)
    return pl.pallas_call(
        body, name="gdn_scan_bwd", grid=(nb,),
        in_specs=[row] * 4 + [sq, egs, pl.BlockSpec((HEADS, cpb, HEAD_DIM, HEAD_DIM), lambda i: (0, nb - 1 - i, 0, 0)), row],
        out_specs=[row] * 4 + [sq, egs],
        out_shape=[wide] * 4 + [jax.ShapeDtypeStruct((HEADS, n_chunks, CHUNK, CHUNK), f32),
                                jax.ShapeDtypeStruct((HEADS, n_chunks, SUBLANES, HEAD_DIM), f32)],
        scratch_shapes=[pltpu.VMEM((HEADS, HEAD_DIM, HEAD_DIM), f32)],
        compiler_params=_params("arbitrary"),
    )(u, w, qg, kd, attn, eg, states, d_o)


def _gdn_intra_bwd(gq, gk, gv, gc_b, g_last_b, beta_b, t_inv, du, dw, dqg, dkd, dattn, deg):
    s_len = gq.shape[0]
    cpb = INTRA_CHUNKS
    rows_blk = cpb * CHUNK

    def body(q_ref, k_ref, v_ref, gc_ref, gl_ref, b_ref, t_ref, du_ref, dw_ref, dqg_ref, dkd_ref, dattn_ref, deg_ref,
             dq_ref, dk_ref, dv_ref, dgc_ref, dgl_ref, db_ref):
        def group(it, _):
            for un in range(INTRA_INTERLEAVE):
                ci = it * INTRA_INTERLEAVE + un
                rows = pl.ds(pl.multiple_of(ci * CHUNK, CHUNK), CHUNK)
                t_known = t_ref[0, ci]
                _, vjp = jax.vjp(lambda q, k, v, gc, gl, b: _gdn_intra(q, k, v, gc, gl, b, t_known),
                                 q_ref[rows, :], k_ref[rows, :], v_ref[rows, :], gc_ref[rows, :], gl_ref[rows, :],
                                 b_ref[rows, :])
                duw = jnp.concatenate([du_ref[rows, :], dw_ref[rows, :]], axis=1)
                dq, dk, dv, dgc, dgl, db = vjp((duw, dqg_ref[rows, :], dkd_ref[rows, :], dattn_ref[0, ci],
                                                deg_ref[0, ci]))
                dq_ref[rows, :] = dq
                dk_ref[rows, :] = dk
                dv_ref[rows, :] = dv
                dgc_ref[rows, :] = dgc
                dgl_ref[rows, :] = dgl
                db_ref[rows, :] = db
            return 0

        lax.fori_loop(0, cpb // INTRA_INTERLEAVE, group, 0)

    blk = pl.BlockSpec((rows_blk, HEAD_DIM), lambda h, i: (i, h))
    sq = pl.BlockSpec((1, cpb, CHUNK, CHUNK), lambda h, i: (h, i, 0, 0))
    egs = pl.BlockSpec((1, cpb, SUBLANES, HEAD_DIM), lambda h, i: (h, i, 0, 0))
    wide = jax.ShapeDtypeStruct((s_len, WIDTH), f32)
    return pl.pallas_call(
        body, name="gdn_intra_bwd", grid=(HEADS, s_len // rows_blk),
        in_specs=[blk] * 6 + [sq] + [blk] * 4 + [sq, egs],
        out_specs=[blk] * 6,
        out_shape=[wide] * 6,
        compiler_params=_params("parallel", "parallel"),
    )(gq, gk, gv, gc_b, g_last_b, beta_b, t_inv, du, dw, dqg, dkd, dattn, deg)


MIX_TM = 256


def _mix_fwd(fox_o, gdn_o, p_main, gnorm_g):
    s_len = fox_o.shape[0]
    tm = MIX_TM

    def body(fo_ref, go_ref, fz_ref, gz_ref, g_ref, mixed_ref):
        fz = fz_ref[...]
        mixed_ref[:, 0:WIDTH] = (fo_ref[...] * (fz * _sigmoid(fz))).astype(bf16)
        gz = gz_ref[...]
        gate = gz * _sigmoid(gz)
        gg = g_ref[...]
        for h in range(HEADS):
            sl = slice(h * HEAD_DIM, (h + 1) * HEAD_DIM)
            o = go_ref[:, sl]
            r = lax.rsqrt(jnp.mean(o * o, axis=-1, keepdims=True) + EPS)
            mixed_ref[:, WIDTH + h * HEAD_DIM:WIDTH + (h + 1) * HEAD_DIM] = (o * r * gg * gate[:, sl]).astype(bf16)

    row = pl.BlockSpec((tm, WIDTH), lambda i: (i, 0))
    return pl.pallas_call(
        body, name="mix_fwd", grid=(s_len // tm,),
        in_specs=[row, row, pl.BlockSpec((tm, WIDTH), lambda i: (i, 3)), pl.BlockSpec((tm, WIDTH), lambda i: (i, 7)),
                  pl.BlockSpec((1, LANES), lambda i: (0, 0))],
        out_specs=pl.BlockSpec((tm, 2 * WIDTH), lambda i: (i, 0)),
        out_shape=jax.ShapeDtypeStruct((s_len, 2 * WIDTH), bf16),
        compiler_params=_params("parallel"),
    )(fox_o, gdn_o, p_main, p_main, gnorm_g)


def _silu_grad(z):
    sg = _sigmoid(z)
    return sg * (1.0 + z * (1.0 - sg))


def _mix_bwd(dmixed, fox_o, gdn_o, p_main, gnorm_g):
    s_len = fox_o.shape[0]
    tm = MIX_TM

    def body(dm_ref, fo_ref, go_ref, fz_ref, gz_ref, g_ref, dof_ref, delta_ref, dfz_ref, dgz_ref, dgo_ref, dg_ref):
        @pl.when(pl.program_id(0) == 0)
        def _():
            dg_ref[...] = jnp.zeros_like(dg_ref)

        lane = _iota((tm, LANES), 1)
        fz = fz_ref[...]
        dmf = dm_ref[:, 0:WIDTH]
        fo = fo_ref[...]
        dof = dmf * (fz * _sigmoid(fz))
        dof_ref[...] = dof.astype(bf16)
        dfz_ref[...] = (dmf * fo * _silu_grad(fz)).astype(bf16)
        prod = dof * fo
        delta = jnp.zeros((tm, LANES), f32)
        for h in range(HEADS):
            dh = jnp.sum(prod[:, h * HEAD_DIM:(h + 1) * HEAD_DIM], axis=-1, keepdims=True)
            delta = jnp.where(lane == h, dh, delta)
        delta_ref[...] = delta

        gz = gz_ref[...]
        dmg = dm_ref[:, WIDTH:2 * WIDTH]
        gate = gz * _sigmoid(gz)
        sgrad = _silu_grad(gz)
        gg = g_ref[...]
        dg_acc = jnp.zeros((1, HEAD_DIM), f32)
        for h in range(HEADS):
            sl = slice(h * HEAD_DIM, (h + 1) * HEAD_DIM)
            o = go_ref[:, sl]
            r = lax.rsqrt(jnp.mean(o * o, axis=-1, keepdims=True) + EPS)
            on = o * r
            dmh = dmg[:, sl]
            dgz_ref[:, sl] = (dmh * (on * gg) * sgrad[:, sl]).astype(bf16)
            dy = dmh * gate[:, sl]
            dg_acc = dg_acc + jnp.sum(dy * on, axis=0, keepdims=True)
            tt = dy * gg
            dgo_ref[:, sl] = r * (tt - on * jnp.mean(tt * on, axis=-1, keepdims=True))
        dg_ref[...] += dg_acc

    row = pl.BlockSpec((tm, WIDTH), lambda i: (i, 0))
    wide_bf = jax.ShapeDtypeStruct((s_len, WIDTH), bf16)
    return pl.pallas_call(
        body, name="mix_bwd", grid=(s_len // tm,),
        in_specs=[pl.BlockSpec((tm, 2 * WIDTH), lambda i: (i, 0)), row, row,
                  pl.BlockSpec((tm, WIDTH), lambda i: (i, 3)), pl.BlockSpec((tm, WIDTH), lambda i: (i, 7)),
                  pl.BlockSpec((1, LANES), lambda i: (0, 0))],
        out_specs=[row, pl.BlockSpec((tm, LANES), lambda i: (i, 0)), row, row, row,
                   pl.BlockSpec((1, LANES), lambda i: (0, 0))],
        out_shape=[wide_bf, jax.ShapeDtypeStruct((s_len, LANES), f32), wide_bf, wide_bf,
                   jax.ShapeDtypeStruct((s_len, WIDTH), f32), jax.ShapeDtypeStruct((1, LANES), f32)],
        compiler_params=_params("arbitrary"),
    )(dmixed, fox_o, gdn_o, p_main, p_main, gnorm_g)


def _out_head(mixed, w_out, x, target, gate, final_g):
    s_len = x.shape[0]
    tm = 256

    def body(mx_ref, w_ref, x_ref, t_ref, gate_ref, fg_ref, loss_ref, dy_ref, dz_ref, dm_ref, dfg_ref, dgate_ref):
        @pl.when(pl.program_id(0) == 0)
        def _():
            loss_ref[...] = jnp.zeros_like(loss_ref)
            dfg_ref[...] = jnp.zeros_like(dfg_ref)
            dgate_ref[...] = jnp.zeros_like(dgate_ref)

        w = w_ref[...]
        z = jnp.dot(mx_ref[...], w, preferred_element_type=f32)
        gate_v, fg = gate_ref[...], fg_ref[...]
        y1 = x_ref[...] + gate_v * z
        r = lax.rsqrt(jnp.mean(y1 * y1, axis=-1, keepdims=True) + EPS)
        yn = y1 * r
        err = yn * fg - t_ref[...]
        loss_ref[...] += 0.5 * jnp.sum(jnp.mean(err * err, axis=-1, keepdims=True))
        dout = err * (1.0 / D_MODEL)
        dfg_ref[...] += jnp.sum(dout * yn, axis=0, keepdims=True)
        tt = dout * fg
        dy1 = r * (tt - yn * jnp.mean(tt * yn, axis=-1, keepdims=True))
        dy_ref[...] = dy1
        dgate_ref[...] += jnp.sum(dy1 * z, axis=0, keepdims=True)
        dz = (dy1 * gate_v).astype(bf16)
        dz_ref[...] = dz
        dm_ref[...] = _dg(dz, w, 1, 1)

    row = pl.BlockSpec((tm, D_MODEL), lambda i: (i, 0))
    vec = pl.BlockSpec((1, D_MODEL), lambda i: (0, 0))
    big = jax.ShapeDtypeStruct((s_len, D_MODEL), f32)
    return pl.pallas_call(
        body, name="out_head", grid=(s_len // tm,),
        in_specs=[row, pl.BlockSpec((D_MODEL, D_MODEL), lambda i: (0, 0)), row, row, vec, vec],
        out_specs=[pl.BlockSpec((1, LANES), lambda i: (0, 0)), row, row, row, vec, vec],
        out_shape=[jax.ShapeDtypeStruct((1, LANES), f32), big, jax.ShapeDtypeStruct((s_len, D_MODEL), bf16), big,
                   jax.ShapeDtypeStruct((1, D_MODEL), f32), jax.ShapeDtypeStruct((1, D_MODEL), f32)],
        compiler_params=_params("arbitrary"),
    )(mixed, w_out, x, target, gate, final_g)


def _matmul_tn(name, a, b, out_dtype):
    k_len, m_len = a.shape
    n_len = b.shape[1]
    tk, tm, tn = 512, 1024, min(1024, n_len)
    nk = k_len // tk

    def body(a_ref, b_ref, o_ref, acc_sc):
        k = pl.program_id(2)

        @pl.when(k == 0)
        def _():
            acc_sc[...] = jnp.zeros_like(acc_sc)

        acc_sc[...] += _dg(a_ref[...], b_ref[...], 0, 0)

        @pl.when(k == nk - 1)
        def _():
            o_ref[...] = acc_sc[...].astype(out_dtype)

    return pl.pallas_call(
        body, name=name, grid=(m_len // tm, n_len // tn, nk),
        in_specs=[pl.BlockSpec((tk, tm), lambda i, j, k: (k, i)), pl.BlockSpec((tk, tn), lambda i, j, k: (k, j))],
        out_specs=pl.BlockSpec((tm, tn), lambda i, j, k: (i, j)),
        out_shape=jax.ShapeDtypeStruct((m_len, n_len), out_dtype),
        scratch_shapes=[pltpu.VMEM((tm, tn), f32)],
        compiler_params=_params("parallel", "parallel", "arbitrary"),
    )(a, b)


def _post1(p_main, p_small, qn_g, kn_g, conv_w, bvec, alog, dqs, dkn, dgq, dgk, dgv, dgc_b, dgl_b, dbeta_b, df,
           df_query):
    s_len = p_main.shape[0]
    tm = PREP_TM
    nb = s_len // tm

    def body(fq_ref, fk_ref, gq_ref, gk_ref, gv_ref, hq_ref, hk_ref, hv_ref, ps_ref, qg_ref, kg_ref, cw_ref, bv_ref,
             al_ref, dqs_ref, dkn_ref, dgq_ref, dgk_ref, dgv_ref, dgcb_ref, dglb_ref, dbb_ref, df_ref, dfq_in_ref,
             dfq_ref, dfk_ref, dconv_ref, dps_ref, dqg_ref, dkg_ref, sums_ref, xe_sc, carry_sc):
        step = pl.program_id(0)
        blk = nb - 1 - step

        @pl.when(step == 0)
        def _():
            carry_sc[...] = jnp.zeros_like(carry_sc)
            dqg_ref[...] = jnp.zeros_like(dqg_ref)
            dkg_ref[...] = jnp.zeros_like(dkg_ref)
            sums_ref[...] = jnp.zeros_like(sums_ref)

        for x_ref, g_ref, dy_ref, o_ref, acc_ref, mul in ((fq_ref, qg_ref, dqs_ref, dfq_ref, dqg_ref, QK_SCALE),
                                                          (fk_ref, kg_ref, dkn_ref, dfk_ref, dkg_ref, 1.0)):
            gain = g_ref[...]
            acc = jnp.zeros((1, HEAD_DIM), f32)
            for h in range(HEADS):
                sl = slice(h * HEAD_DIM, (h + 1) * HEAD_DIM)
                xv = x_ref[:, sl]
                r = lax.rsqrt(jnp.mean(xv * xv, axis=-1, keepdims=True) + EPS)
                xn = xv * r
                dy = dy_ref[:, sl] * mul
                acc = acc + jnp.sum(dy * xn, axis=0, keepdims=True)
                tt = dy * gain
                o_ref[:, sl] = (r * (tt - xn * jnp.mean(tt * xn, axis=-1, keepdims=True))).astype(bf16)
            acc_ref[...] += acc

        first = blk == 0
        for sec, (x_ref, halo_ref, dy_ref) in enumerate(((gq_ref, hq_ref, dgq_ref), (gk_ref, hk_ref, dgk_ref),
                                                         (gv_ref, hv_ref, dgv_ref))):
            xe_sc[0:HALO, :] = jnp.where(first, 0.0, halo_ref[...])
            xe_sc[HALO:, :] = x_ref[...]
            cv = _conv_section(xe_sc, cw_ref, slice(sec * WIDTH, (sec + 1) * WIDTH), tm)
            sgrad = _silu_grad(cv)
            if sec == 2:
                dconv_ref[:, sec * WIDTH:(sec + 1) * WIDTH] = dy_ref[...] * sgrad
            else:
                y = cv * _sigmoid(cv)
                mul = QK_SCALE if sec == 0 else 1.0
                for h in range(HEADS):
                    sl = slice(h * HEAD_DIM, (h + 1) * HEAD_DIM)
                    yh = y[:, sl]
                    r = lax.rsqrt(jnp.sum(yh * yh, axis=-1, keepdims=True) + EPS)
                    dqh = dy_ref[:, sl]
                    dyh = (mul * r) * (dqh - yh * (r * r) * jnp.sum(dqh * yh, axis=-1, keepdims=True))
                    dconv_ref[:, sec * WIDTH + h * HEAD_DIM:sec * WIDTH + (h + 1) * HEAD_DIM] = dyh * sgrad[:, sl]

        lane = _iota((tm, N_SMALL), 1)
        z, _, gval, beta = _small_fwd(ps_ref[...], bv_ref[...], al_ref[...])
        sig_z = _sigmoid(z)
        sel_t = (_iota((WIDTH, LANES), 1) == HEADS + _iota((WIDTH, LANES), 0) // HEAD_DIM).astype(f32)
        dgc = jnp.dot(dgcb_ref[...], sel_t, preferred_element_type=f32, precision=HI)
        dgl = jnp.dot(dglb_ref[...], sel_t, preferred_element_type=f32, precision=HI)
        tri_c, ones_c = _chunk_masks(tm)
        dg = (_dg(tri_c, dgc, 0, 0, HI) + jnp.dot(ones_c, dgl, preferred_element_type=f32, precision=HI))
        sel_t2 = (_iota((WIDTH, LANES), 1) == 2 * HEADS + _iota((WIDTH, LANES), 0) // HEAD_DIM).astype(f32)
        dbeta = jnp.dot(dbb_ref[...], sel_t2, preferred_element_type=f32, precision=HI)
        dfb = jnp.where(lane < HEADS, df_ref[...], 0.0)
        for h in range(HEADS):
            dfb = dfb + jnp.where(lane == h, dfq_in_ref[h], 0.0)
        tri_u = (_iota((tm, tm), 1) >= _iota((tm, tm), 0)).astype(f32)
        dlogf = jnp.dot(tri_u, dfb, preferred_element_type=f32, precision=HI) + carry_sc[...]
        carry_sc[...] += jnp.sum(dfb, axis=0, keepdims=True)
        dff = dlogf * (1.0 - sig_z)
        dga = dg * (-jnp.exp(al_ref[...])) * sig_z
        dgb_small = dbeta * beta * (1.0 - beta)
        dps = jnp.where(lane < HEADS, dff, jnp.where(lane < 2 * HEADS, dga, jnp.where(lane < 3 * HEADS, dgb_small, 0.0)))
        dps_ref[...] = dps.astype(bf16)
        row = _iota((8, N_SMALL), 0)
        s0 = jnp.sum(dps, axis=0, keepdims=True)
        s1 = jnp.sum(jnp.where((lane >= HEADS) & (lane < 2 * HEADS), dg * gval, 0.0), axis=0, keepdims=True)
        sums_ref[...] += jnp.where(row == 0, s0, jnp.where(row == 1, s1, 0.0))

    def col(cb):
        return pl.BlockSpec((tm, WIDTH), lambda i: (nb - 1 - i, cb))

    def halo(cb):
        return pl.BlockSpec((HALO, WIDTH), lambda i: (jnp.maximum((nb - 1 - i) * (tm // HALO) - 1, 0), cb))

    vec = pl.BlockSpec((1, LANES), lambda i: (0, 0))
    row0 = pl.BlockSpec((tm, WIDTH), lambda i: (nb - 1 - i, 0))
    small = pl.BlockSpec((tm, N_SMALL), lambda i: (nb - 1 - i, 0))
    wide_bf = jax.ShapeDtypeStruct((s_len, WIDTH), bf16)
    return pl.pallas_call(
        body, name="post1", grid=(nb,),
        in_specs=[col(0), col(1), col(4), col(5), col(6), halo(4), halo(5), halo(6), small, vec, vec,
                  pl.BlockSpec((CONV_K, 3 * WIDTH), lambda i: (0, 0)), vec, vec,
                  row0, row0, row0, row0, row0, row0, row0, row0, small,
                  pl.BlockSpec((HEADS, tm, 1), lambda i: (0, nb - 1 - i, 0))],
        out_specs=[row0, row0, pl.BlockSpec((tm, 3 * WIDTH), lambda i: (nb - 1 - i, 0)), small, vec, vec,
                   pl.BlockSpec((8, N_SMALL), lambda i: (0, 0))],
        out_shape=[wide_bf, wide_bf, jax.ShapeDtypeStruct((s_len, 3 * WIDTH), f32),
                   jax.ShapeDtypeStruct((s_len, N_SMALL), bf16), jax.ShapeDtypeStruct((1, LANES), f32),
                   jax.ShapeDtypeStruct((1, LANES), f32), jax.ShapeDtypeStruct((8, N_SMALL), f32)],
        scratch_shapes=[pltpu.VMEM((tm + HALO, WIDTH), f32), pltpu.VMEM((1, N_SMALL), f32)],
        compiler_params=_params("arbitrary"),
    )(p_main, p_main, p_main, p_main, p_main, p_main, p_main, p_main, p_small, qn_g, kn_g, conv_w, bvec, alog,
      dqs, dkn, dgq, dgk, dgv, dgc_b, dgl_b, dbeta_b, df, df_query)


def _post2(p_main, dconv, conv_w):
    s_len = p_main.shape[0]
    tm = PREP_TM
    nb = s_len // tm

    def body(gq_ref, gk_ref, gv_ref, hq_ref, hk_ref, hv_ref, dc_ref, dnext_ref, cw_ref, dx_ref, dw_ref, xe_sc, de_sc):
        i = pl.program_id(0)

        @pl.when(i == 0)
        def _():
            dw_ref[...] = jnp.zeros_like(dw_ref)

        first, last = i == 0, i == nb - 1
        row = _iota((8, WIDTH), 0)
        for sec, (x_ref, halo_ref) in enumerate(((gq_ref, hq_ref), (gk_ref, hk_ref), (gv_ref, hv_ref))):
            cols = slice(sec * WIDTH, (sec + 1) * WIDTH)
            dc = dc_ref[:, cols]
            de_sc[0:tm, :] = dc
            de_sc[tm:, :] = jnp.where(last, 0.0, dnext_ref[:, cols])
            dx = cw_ref[pl.ds(CONV_K - 1, 1), cols] * dc
            for tap in range(CONV_K - 1):
                dx = dx + cw_ref[pl.ds(tap, 1), cols] * de_sc[pl.ds(CONV_K - 1 - tap, tm), :]
            dx_ref[:, cols] = dx.astype(bf16)
            xe_sc[0:HALO, :] = jnp.where(first, 0.0, halo_ref[...])
            xe_sc[HALO:, :] = x_ref[...]
            dw = jnp.zeros((8, WIDTH), f32)
            for tap in range(CONV_K):
                contrib = jnp.sum(dc * xe_sc[pl.ds(HALO - (CONV_K - 1) + tap, tm), :], axis=0, keepdims=True)
                dw = jnp.where(row == tap, contrib, dw)
            dw_ref[:, cols] += dw

    def col(cb):
        return pl.BlockSpec((tm, WIDTH), lambda i: (i, cb))

    def halo(cb):
        return pl.BlockSpec((HALO, WIDTH), lambda i: (jnp.maximum(i * (tm // HALO) - 1, 0), cb))

    return pl.pallas_call(
        body, name="post2", grid=(nb,),
        in_specs=[col(4), col(5), col(6), halo(4), halo(5), halo(6),
                  pl.BlockSpec((tm, 3 * WIDTH), lambda i: (i, 0)),
                  pl.BlockSpec((HALO, 3 * WIDTH), lambda i: (jnp.minimum((i + 1) * (tm // HALO), s_len // HALO - 1), 0)),
                  pl.BlockSpec((CONV_K, 3 * WIDTH), lambda i: (0, 0))],
        out_specs=[pl.BlockSpec((tm, 3 * WIDTH), lambda i: (i, 0)), pl.BlockSpec((8, 3 * WIDTH), lambda i: (0, 0))],
        out_shape=[jax.ShapeDtypeStruct((s_len, 3 * WIDTH), bf16), jax.ShapeDtypeStruct((8, 3 * WIDTH), f32)],
        scratch_shapes=[pltpu.VMEM((tm + HALO, WIDTH), f32), pltpu.VMEM((tm + HALO, WIDTH), f32)],
        compiler_params=_params("arbitrary"),
    )(p_main, p_main, p_main, p_main, p_main, p_main, dconv, dconv, conv_w)


def _in_proj_bwd(dp_main, dp_small, w_main, w_small, x, dy1, norm_g, scale1p):
    s_len = x.shape[0]
    tm, tk = 512, 1024
    nk = N_MAIN // tk

    def body(dp_ref, dps_ref, w_ref, ws_ref, x_ref, dy_ref, g_ref, sc_ref, dx_ref, dsh_ref, dsc_ref, dg_ref, acc_sc):
        i, k = pl.program_id(0), pl.program_id(1)

        @pl.when((i == 0) & (k == 0))
        def _():
            dsh_ref[...] = jnp.zeros_like(dsh_ref)
            dsc_ref[...] = jnp.zeros_like(dsc_ref)
            dg_ref[...] = jnp.zeros_like(dg_ref)

        @pl.when(k == 0)
        def _():
            acc_sc[...] = _dg(dps_ref[...], ws_ref[...], 1, 1)

        acc_sc[...] += _dg(dp_ref[...], w_ref[...], 1, 1)

        @pl.when(k == nk - 1)
        def _():
            dh = acc_sc[...]
            xb = x_ref[...]
            r = lax.rsqrt(jnp.mean(xb * xb, axis=-1, keepdims=True) + EPS)
            xr = xb * r
            gain = g_ref[...]
            dsh_ref[...] += jnp.sum(dh, axis=0, keepdims=True)
            dsc_ref[...] += jnp.sum(dh * (xr * gain), axis=0, keepdims=True)
            dxn = dh * sc_ref[...]
            dg_ref[...] += jnp.sum(dxn * xr, axis=0, keepdims=True)
            tt = dxn * gain
            dx_ref[...] = r * (tt - xr * jnp.mean(tt * xr, axis=-1, keepdims=True)) + dy_ref[...]

    row = pl.BlockSpec((tm, D_MODEL), lambda i, k: (i, 0))
    vec = pl.BlockSpec((1, D_MODEL), lambda i, k: (0, 0))
    vshape = jax.ShapeDtypeStruct((1, D_MODEL), f32)
    return pl.pallas_call(
        body, name="in_proj_bwd", grid=(s_len // tm, nk),
        in_specs=[pl.BlockSpec((tm, tk), lambda i, k: (i, k)), pl.BlockSpec((tm, N_SMALL), lambda i, k: (i, 0)),
                  pl.BlockSpec((D_MODEL, tk), lambda i, k: (0, k)), pl.BlockSpec((D_MODEL, N_SMALL), lambda i, k: (0, 0)),
                  row, row, vec, vec],
        out_specs=[row, vec, vec, vec],
        out_shape=[jax.ShapeDtypeStruct((s_len, D_MODEL), f32), vshape, vshape, vshape],
        scratch_shapes=[pltpu.VMEM((tm, D_MODEL), f32)],
        compiler_params=_params("arbitrary", "arbitrary"),
    )(dp_main, dp_small, w_main, w_small, x, dy1, norm_g, scale1p)


def _adamw(name, w, g_stack, m, v, tr):
    n_stack, rows, cols = g_stack.shape

    def body(w_ref, g_ref, m_ref, v_ref, go_ref, d_ref, mo_ref, vo_ref):
        g = g_ref[0].astype(f32)
        for k in range(1, n_stack):
            g = g + g_ref[k].astype(f32)
        go_ref[0] = g
        m_new = ADAM_B1 * m_ref[0] + (1.0 - ADAM_B1) * g
        v_new = ADAM_B2 * v_ref[0] + (1.0 - ADAM_B2) * (g * g)
        mo_ref[0] = m_new
        vo_ref[0] = v_new
        m_hat = m_new / (1.0 - ADAM_B1 ** ADAM_STEP)
        v_hat = v_new / (1.0 - ADAM_B2 ** ADAM_STEP)
        d_ref[0] = -ADAM_LR * (m_hat / (jnp.sqrt(v_hat) + ADAM_EPS) + ADAM_WD * w_ref[0])

    blk = pl.BlockSpec((1, tr, cols), lambda i: (0, i, 0))
    shape = jax.ShapeDtypeStruct((1, rows, cols), f32)
    return pl.pallas_call(
        body, name=name, grid=(rows // tr,),
        in_specs=[blk, pl.BlockSpec((n_stack, tr, cols), lambda i: (0, i, 0)), blk, blk],
        out_specs=[blk] * 4, out_shape=[shape] * 4,
        compiler_params=_params("parallel"),
    )(w, g_stack, m, v)


def _w_ada_grad(c_all_t, dmod_pad):
    def body(c_ref, d_ref, o_ref):
        cv = c_ref[...]
        o_ref[...] = jnp.dot(cv * _sigmoid(cv), d_ref[...], preferred_element_type=f32, precision=HI)

    return pl.pallas_call(body, name="w_ada_grad",
                          out_shape=jax.ShapeDtypeStruct((c_all_t.shape[0], dmod_pad.shape[1]), f32),
                          compiler_params=_params())(c_all_t, dmod_pad)


SMALL_NAMES = ("norm_g", "b_ada", "b_fgate", "fox_qn_g", "fox_kn_g", "gdn_A_log", "gdn_dt_bias", "gdn_norm_g", "final_g")
SMALL_SIZES = (D_MODEL, 3 * D_MODEL, HEADS, HEAD_DIM, HEAD_DIM, HEADS, HEADS, HEAD_DIM, D_MODEL)
SMALL_PACK = 10752


def _pack(vectors, total):
    flat = jnp.concatenate([t.reshape(-1) for t in vectors])
    return jnp.pad(flat, (0, total - flat.shape[0])).reshape(1, total)


def _lanes(*pieces):
    row = jnp.zeros((LANES,), f32)
    for off, vec in pieces:
        row = lax.dynamic_update_slice(row, vec.reshape(-1).astype(f32), (off,))
    return row.reshape(1, LANES)


def kernel(x, c, norm_g, w_ada, b_ada, w_in, b_fgate, fox_qn_g, fox_kn_g, gdn_conv_w, gdn_A_log, gdn_dt_bias, gdn_norm_g, w_out, final_g, loss_target, m_norm_g, m_w_ada, m_b_ada, m_w_in, m_b_fgate, m_fox_qn_g, m_fox_kn_g, m_gdn_conv_w, m_gdn_A_log, m_gdn_dt_bias, m_gdn_norm_g, m_w_out, m_final_g, v_norm_g, v_w_ada, v_b_ada, v_w_in, v_b_fgate, v_fox_qn_g, v_fox_kn_g, v_gdn_conv_w, v_gdn_A_log, v_gdn_dt_bias, v_gdn_norm_g, v_w_out, v_final_g):
    me = _my_index()
    s_len = x.shape[1]
    nq = s_len // FOX_T
    x2 = x.reshape(s_len, D_MODEL)
    tgt = loss_target.reshape(s_len, D_MODEL)
    ada_cols = w_ada.shape[2]
    in_cols = w_in.shape[2]
    conv_cols = gdn_conv_w.shape[2]

    (c_all,) = _exchange("gather_c", [c], scatter=False)
    c_all = c_all.reshape(N_DEV, D_MODEL)
    b_shard = lax.dynamic_slice(b_ada, (0, me * ada_cols), (1, ada_cols))
    mod_mine = _mod_shard(c_all, w_ada[0], b_shard)
    mod_all, w_in_all, w_out_all, conv_all = _gather_two_level(
        "gather_weights", [mod_mine, w_in[0].astype(bf16), w_out[0].astype(bf16), gdn_conv_w[0]])
    mod = lax.dynamic_slice(mod_all, (0, me, 0), (N_DEV, 1, ada_cols)).reshape(1, 3 * D_MODEL)
    shift, scale, gate = mod[:, :D_MODEL], mod[:, D_MODEL:2 * D_MODEL], mod[:, 2 * D_MODEL:]
    scale1p = 1.0 + scale
    w_in_full = jnp.transpose(w_in_all, (1, 0, 2)).reshape(D_MODEL, N_DEV * in_cols)
    g0 = 4 * WIDTH + HEADS
    w_main = jnp.concatenate([w_in_full[:, :4 * WIDTH], w_in_full[:, g0:g0 + 4 * WIDTH]], axis=1)
    w_small = jnp.concatenate([w_in_full[:, 4 * WIDTH:g0], w_in_full[:, g0 + 4 * WIDTH:],
                               jnp.zeros((D_MODEL, N_SMALL - 3 * HEADS), bf16)], axis=1)
    w_out_full = w_out_all.reshape(2 * WIDTH, D_MODEL)
    conv_full = jnp.transpose(conv_all, (1, 0, 2)).reshape(CONV_K, 3 * WIDTH)

    qn_g, kn_g, gn_g = fox_qn_g.reshape(1, LANES), fox_kn_g.reshape(1, LANES), gdn_norm_g.reshape(1, LANES)
    bvec = _lanes((0, b_fgate), (HEADS, gdn_dt_bias))
    alog = _lanes((HEADS, gdn_A_log))
    fg = final_g.reshape(1, D_MODEL)

    p_main, p_small, h_bf = _in_proj(x2, norm_g, scale1p, shift, w_main, w_small)
    qs, kn, vb, gq, gk, gv, small, gc_b, gl_b, beta_b = _prep(p_main, p_small, qn_g, kn_g, conv_full, bvec, alog)
    f_row = jnp.transpose(small[:, :HEADS]).reshape(HEADS, nq, 1, FOX_T)
    fox_o, lse = _fox_fwd(qs, kn, vb, small, f_row)
    gu, gw, gqg, gkd, gattn, t_inv, eg_last = _gdn_intra_fwd(gq, gk, gv, gc_b, gl_b, beta_b)
    gdn_o, states = _gdn_scan_fwd(gu, gw, gqg, gkd, gattn, eg_last)
    mixed = _mix_fwd(fox_o, gdn_o, p_main, gn_g)

    loss_row, dy1, dz, dmixed, d_final_g, d_gate = _out_head(mixed, w_out_full, x2, tgt, gate, fg)
    loss = lax.psum(loss_row[0, 0], AXES)
    dw_out = _matmul_tn("dw_out", mixed, dz, bf16)
    do_fox, delta, dfz, dgz, dgdn_o, d_gn_g = _mix_bwd(dmixed, fox_o, gdn_o, p_main, gn_g)
    dqs, dkn, dvf, df_key, df_query = _fox_bwd(qs, kn, vb, do_fox, small, lse, delta, f_row)
    du, dw, dqg, dkd, dattn, deg = _gdn_scan_bwd(gu, gw, gqg, gkd, gattn, eg_last, states, dgdn_o)
    dgq, dgk, dgv, dgc_b, dgl_b, dbeta_b = _gdn_intra_bwd(gq, gk, gv, gc_b, gl_b, beta_b, t_inv, du, dw, dqg, dkd,
                                                          dattn, deg)
    df_small = jnp.pad(jnp.transpose(df_key.reshape(HEADS, s_len)), ((0, 0), (0, N_SMALL - HEADS)))
    dfq, dfk, dconv, dp_small, d_qn_g, d_kn_g, sums = _post1(
        p_main, p_small, qn_g, kn_g, conv_full, bvec, alog, dqs, dkn, dgq, dgk, dgv, dgc_b, dgl_b, dbeta_b, df_small,
        df_query)
    dgqkv, d_conv = _post2(p_main, dconv, conv_full)
    dp_main = jnp.concatenate([dfq, dfk, dvf.astype(bf16), dfz, dgqkv, dgz], axis=1)
    grad_x, d_shift, d_scale, d_norm_g = _in_proj_bwd(dp_main, dp_small, w_main, w_small, x2, dy1, norm_g, scale1p)
    dw_main = _matmul_tn("dw_main", h_bf, dp_main, bf16)
    dw_small = _matmul_tn("dw_small", h_bf, dp_small, bf16)
    dw_in_full = jnp.concatenate([dw_main[:, :4 * WIDTH], dw_small[:, :HEADS], dw_main[:, 4 * WIDTH:],
                                  dw_small[:, HEADS:3 * HEADS]], axis=1)
    dw_in_parts = jnp.transpose(dw_in_full.reshape(D_MODEL, N_DEV, in_cols), (1, 0, 2))
    dw_out_parts = dw_out.reshape(N_DEV, w_out.shape[1], D_MODEL)

    dmod = jnp.concatenate([d_shift, d_scale, d_gate], axis=1)
    small_grads = _pack([d_norm_g, dmod, sums[0, :HEADS], d_qn_g, d_kn_g, sums[1, HEADS:2 * HEADS],
                         sums[0, HEADS:2 * HEADS], d_gn_g, d_final_g], SMALL_PACK)
    conv_grad = d_conv[:CONV_K]
    dw_in_recv, dw_out_recv = _exchange("scatter_grads", [dw_in_parts, dw_out_parts], scatter=True)
    small_all, conv_all_g = _exchange("gather_small_grads", [small_grads, conv_grad], scatter=False)

    outs = {}
    outs["w_in"] = _adamw("adamw_w_in", w_in, dw_in_recv, m_w_in, v_w_in, 128)
    outs["w_out"] = _adamw("adamw_w_out", w_out, dw_out_recv, m_w_out, v_w_out, 128)
    conv_mine = lax.dynamic_slice(jnp.transpose(conv_all_g.reshape(N_DEV, CONV_K, N_DEV, conv_cols), (0, 2, 1, 3)),
                                  (0, me, 0, 0), (N_DEV, 1, CONV_K, conv_cols)).reshape(N_DEV, CONV_K, conv_cols)
    outs["gdn_conv_w"] = _adamw("adamw_conv", gdn_conv_w, conv_mine, m_gdn_conv_w, v_gdn_conv_w, CONV_K)
    small_all = small_all.reshape(N_DEV, 1, SMALL_PACK)
    dmod_all = small_all[:, 0, D_MODEL:D_MODEL + 3 * D_MODEL]
    dmod_mine = lax.dynamic_slice(dmod_all, (0, me * ada_cols), (N_DEV, ada_cols))
    c_all_t = jnp.pad(jnp.transpose(c_all), ((0, 0), (0, LANES - N_DEV)))
    g_w_ada = _w_ada_grad(c_all_t, jnp.pad(dmod_mine, ((0, LANES - N_DEV), (0, 0))))
    outs["w_ada"] = _adamw("adamw_w_ada", w_ada, g_w_ada[None], m_w_ada, v_w_ada, 256)
    given = dict(norm_g=(norm_g, m_norm_g, v_norm_g), b_ada=(b_ada, m_b_ada, v_b_ada), b_fgate=(b_fgate, m_b_fgate, v_b_fgate),
                 fox_qn_g=(fox_qn_g, m_fox_qn_g, v_fox_qn_g), fox_kn_g=(fox_kn_g, m_fox_kn_g, v_fox_kn_g),
                 gdn_A_log=(gdn_A_log, m_gdn_A_log, v_gdn_A_log), gdn_dt_bias=(gdn_dt_bias, m_gdn_dt_bias, v_gdn_dt_bias),
                 gdn_norm_g=(gdn_norm_g, m_gdn_norm_g, v_gdn_norm_g), final_g=(final_g, m_final_g, v_final_g))
    w_pack = _pack([given[n][0] for n in SMALL_NAMES], SMALL_PACK)
    m_pack = _pack([given[n][1] for n in SMALL_NAMES], SMALL_PACK)
    v_pack = _pack([given[n][2] for n in SMALL_NAMES], SMALL_PACK)
    packed = _adamw("adamw_small", w_pack[None], small_all, m_pack[None], v_pack[None], 1)
    off = 0
    for n, size in zip(SMALL_NAMES, SMALL_SIZES):
        outs[n] = tuple(t[0, 0, off:off + size].reshape(given[n][0].shape) for t in packed)
        off += size

    order = ("norm_g", "w_ada", "b_ada", "w_in", "b_fgate", "fox_qn_g", "fox_kn_g", "gdn_conv_w", "gdn_A_log",
             "gdn_dt_bias", "gdn_norm_g", "w_out", "final_g")
    result = [loss, grad_x.reshape(x.shape)]
    for part in range(4):
        result += [outs[n][part] for n in order]
    return tuple(result)
```

```python
import math

import jax
import jax.numpy as jnp
from jax import lax
from jax.experimental import pallas as pl
from jax.experimental.pallas import tpu as pltpu

f32 = jnp.float32
bf16 = jnp.bfloat16
HI = lax.Precision.HIGHEST

N_DEV = 8
AXES = ("x", "y", "c")
D_MODEL = 2048
HEADS = 8
HEAD_DIM = 128
WIDTH = HEADS * HEAD_DIM
CHUNK = 64
CONV_K = 4
EPS = 1e-6
QK_SCALE = HEAD_DIM ** -0.5
N_MAIN = 8 * WIDTH
N_SMALL = 128
IN_WIDTH = 8 * WIDTH + 3 * HEADS
LANES = 128
VMEM_LIMIT = 56 * 1024 * 1024

ADAM_LR, ADAM_B1, ADAM_B2, ADAM_EPS, ADAM_WD, ADAM_STEP = 0.001, 0.9, 0.999, 1e-08, 0.01, 10


def _params(*sem):
    return pltpu.CompilerParams(dimension_semantics=sem, vmem_limit_bytes=VMEM_LIMIT)


def _iota(shape, dim):
    return lax.broadcasted_iota(jnp.int32, shape, dim)


def _sigmoid(z):
    return 1.0 / (1.0 + jnp.exp(-z))


def _softplus_parts(z):
    t = jnp.log(1.0 + jnp.exp(-jnp.abs(z)))
    return jnp.minimum(z, 0.0) - t, jnp.maximum(z, 0.0) + t


def _dg(a, b, ca, cb, prec=None):
    return lax.dot_general(a, b, (((ca,), (cb,)), ((), ())), preferred_element_type=f32, precision=prec)


def _dot_bf16(a, b, ca, cb):
    return _dg(a.astype(bf16), b.astype(bf16), ca, cb)


def _split_bf16(a):
    hi = a.astype(bf16)
    return hi, (a - hi.astype(f32)).astype(bf16)


def _dot_3pass(a, b, ca, cb):
    a_hi, a_lo = _split_bf16(a)
    b_hi, b_lo = _split_bf16(b)
    return _dg(a_hi, b_hi, ca, cb) + (_dg(a_hi, b_lo, ca, cb) + _dg(a_lo, b_hi, ca, cb))


def _make_mm(dot):
    def nn_(a, b):
        return dot(a, b, 1, 0)

    def nt_(a, b):
        return dot(a, b, 1, 1)

    def tn_(a, b):
        return dot(a, b, 0, 0)

    @jax.custom_vjp
    def nn(a, b):
        return nn_(a, b)

    @jax.custom_vjp
    def nt(a, b):
        return nt_(a, b)

    @jax.custom_vjp
    def tn(a, b):
        return tn_(a, b)

    nn.defvjp(lambda a, b: (nn_(a, b), (a, b)), lambda r, g: (nt_(g, r[1]), tn_(r[0], g)))
    nt.defvjp(lambda a, b: (nt_(a, b), (a, b)), lambda r, g: (nn_(g, r[1]), tn_(g, r[0])))
    tn.defvjp(lambda a, b: (tn_(a, b), (a, b)), lambda r, g: (nt_(r[1], g), nn_(r[0], g)))
    return (nn_, nt_, tn_), (nn, nt, tn)


_BF_PLAIN, _BF_VJP = _make_mm(_dot_bf16)
_X3_PLAIN, _X3_VJP = _make_mm(_dot_3pass)


def _inv_unit_lower_many(ms):
    c = CHUNK
    nn = _X3_PLAIN[0]
    eye = (_iota((c, c), 0) == _iota((c, c), 1)).astype(f32)
    top = _iota((2 * c, c), 0) < c
    xs = [jnp.concatenate([eye - m, nn(m, m)], axis=0) for m in ms]
    for _ in range(int(math.log2(CHUNK)) - 2):
        xs = [jnp.where(top, x, 0.0) + nn(x, x[c:]) for x in xs]
    return [x[:c] + nn(x[:c], x[c:]) for x in xs]


@jax.custom_vjp
def _inv_given(m, t):
    return t


_inv_given.defvjp(lambda m, t: (t, t),
                  lambda t, g: (-_X3_PLAIN[1](_X3_PLAIN[2](t, g), t), jnp.zeros_like(t)))

SUBLANES = 8


def _gdn_intra_pre(q, k, v, gc_b, g_last_b, beta_b, bnt):
    c = CHUNK
    r_i, c_i = _iota((c, c), 0), _iota((c, c), 1)
    lower, strict = r_i >= c_i, r_i > c_i
    gc_i = gc_b[:, :c]
    gc_j = gc_i.T
    decay = jnp.where(lower, jnp.exp(jnp.where(lower, gc_i - gc_j, 0.0)), 0.0)
    kb = k * beta_b
    both = bnt(jnp.concatenate([kb, q], axis=0), k)
    m = jnp.where(strict, both[:c] * decay, 0.0)
    attn = jnp.where(lower, both[c:] * decay, 0.0)
    eg = jnp.exp(gc_b)
    rhs = jnp.concatenate([v * beta_b, kb * eg], axis=1)
    k_dec = k * jnp.exp(g_last_b - gc_b)
    eg_last = jnp.exp(g_last_b[:SUBLANES])
    return m, rhs, q * eg, k_dec, attn, eg_last


def _gdn_intra(q, k, v, gc_b, g_last_b, beta_b, t_known):
    m, rhs, qg, k_dec, attn, eg_last = _gdn_intra_pre(q, k, v, gc_b, g_last_b, beta_b, _BF_VJP[1])
    return _X3_VJP[0](_inv_given(m, t_known), rhs), qg, k_dec, attn, eg_last


def _scale_rows(s, eg_last):
    return (s.reshape(HEAD_DIM // SUBLANES, SUBLANES, HEAD_DIM) * eg_last[None]).reshape(HEAD_DIM, HEAD_DIM)


def _my_index():
    return 4 * lax.axis_index("x") + 2 * lax.axis_index("y") + lax.axis_index("c")


def _peer(d):
    x, y, c = lax.axis_index("x"), lax.axis_index("y"), lax.axis_index("c")
    px, py, pc = (x + (d >> 2)) % 2, (y + ((d >> 1) & 1)) % 2, (c + (d & 1)) % 2
    return (px, py, pc), 4 * px + 2 * py + pc


def _exchange(name, arrays, scatter):
    n = len(arrays)

    def body(*refs):
        srcs, dsts = refs[:n], refs[n:2 * n]
        send_sems, recv_sems, local_sems = refs[2 * n:]
        me = _my_index()

        def remote(k, d):
            peer, pidx = _peer(d)
            src = srcs[k].at[pidx] if scatter else srcs[k]
            return pltpu.make_async_remote_copy(
                src_ref=src, dst_ref=dsts[k].at[me], send_sem=send_sems.at[k * 7 + d - 1],
                recv_sem=recv_sems.at[k * 7 + d - 1], device_id=peer, device_id_type=pl.DeviceIdType.MESH)

        def arrival(k, d):
            peer, pidx = _peer(d)
            src = srcs[k].at[pidx] if scatter else srcs[k]
            return pltpu.make_async_remote_copy(
                src_ref=src, dst_ref=dsts[k].at[pidx], send_sem=send_sems.at[k * 7 + d - 1],
                recv_sem=recv_sems.at[k * 7 + d - 1], device_id=peer, device_id_type=pl.DeviceIdType.MESH)

        local = [pltpu.make_async_copy(srcs[k].at[me] if scatter else srcs[k], dsts[k].at[me], local_sems.at[k])
                 for k in range(n)]
        sends = [remote(k, d) for k in range(n) for d in range(1, N_DEV)]
        for cp in local + sends:
            cp.start()
        for k in range(n):
            for d in range(1, N_DEV):
                arrival(k, d).wait_recv()
        for cp in sends:
            cp.wait_send()
        for cp in local:
            cp.wait()

    if scatter:
        out_shape = [jax.ShapeDtypeStruct(a.shape, a.dtype) for a in arrays]
    else:
        out_shape = [jax.ShapeDtypeStruct((N_DEV,) + a.shape, a.dtype) for a in arrays]
    any_spec = pl.BlockSpec(memory_space=pl.ANY)
    return pl.pallas_call(
        body, name=name, out_shape=out_shape, in_specs=[any_spec] * n, out_specs=[any_spec] * n,
        scratch_shapes=[pltpu.SemaphoreType.DMA((7 * n,)), pltpu.SemaphoreType.DMA((7 * n,)),
                        pltpu.SemaphoreType.DMA((n,))],
        compiler_params=pltpu.CompilerParams(has_side_effects=True),
    )(*arrays)


def _gather_two_level(name, arrays):
    n = len(arrays)

    def body(*refs):
        srcs, dsts = refs[:n], refs[n:2 * n]
        send_sems, recv_sems, local_sems = refs[2 * n:]
        x, y, c = lax.axis_index("x"), lax.axis_index("y"), lax.axis_index("c")
        sibling = (x, y, 1 - c)
        chips = [((x + 1) % 2, y), (x, (y + 1) % 2), ((x + 1) % 2, (y + 1) % 2)]

        def index(px, py, pc):
            return 4 * px + 2 * py + pc

        def copy(k, slot, block, to, src=None):
            return pltpu.make_async_remote_copy(
                src_ref=dsts[k].at[index(*block)] if src is None else src, dst_ref=dsts[k].at[index(*block)],
                send_sem=send_sems.at[k * 7 + slot], recv_sem=recv_sems.at[k * 7 + slot],
                device_id=to, device_id_type=pl.DeviceIdType.MESH)

        me = (x, y, c)
        local = [pltpu.make_async_copy(srcs[k], dsts[k].at[index(*me)], local_sems.at[k]) for k in range(n)]
        first = [copy(k, 0, me, sibling, src=srcs[k]) for k in range(n)]
        first += [copy(k, 1 + j, me, (*chip, c), src=srcs[k]) for j, chip in enumerate(chips) for k in range(n)]
        for cp in local + first:
            cp.start()
        passed = []
        for j, chip in enumerate(chips):
            for k in range(n):
                copy(k, 1 + j, (*chip, c), me).wait_recv()
                fwd = copy(k, 4 + j, (*chip, c), sibling)
                fwd.start()
                passed.append(fwd)
        for k in range(n):
            copy(k, 0, sibling, me).wait_recv()
            for j, chip in enumerate(chips):
                copy(k, 4 + j, (*chip, 1 - c), me).wait_recv()
        for cp in first + passed:
            cp.wait_send()
        for cp in local:
            cp.wait()

    any_spec = pl.BlockSpec(memory_space=pl.ANY)
    return pl.pallas_call(
        body, name=name, out_shape=[jax.ShapeDtypeStruct((N_DEV,) + a.shape, a.dtype) for a in arrays],
        in_specs=[any_spec] * n, out_specs=[any_spec] * n,
        scratch_shapes=[pltpu.SemaphoreType.DMA((7 * n,)), pltpu.SemaphoreType.DMA((7 * n,)),
                        pltpu.SemaphoreType.DMA((n,))],
        compiler_params=pltpu.CompilerParams(has_side_effects=True),
    )(*arrays)


def _mod_shard(c_all, w_ada, b_shard):
    def body(c_ref, w_ref, b_ref, o_ref):
        cv = c_ref[...]
        ca = cv * _sigmoid(cv)
        o_ref[...] = jnp.dot(ca.astype(bf16), w_ref[...].astype(bf16), preferred_element_type=f32) + b_ref[...]

    return pl.pallas_call(body, name="mod_shard", out_shape=jax.ShapeDtypeStruct((N_DEV, w_ada.shape[1]), f32),
                          compiler_params=_params())(c_all, w_ada, b_shard)


def _in_proj(x, norm_g, scale1p, shift, wt_main, wt_small):
    s_len = x.shape[0]
    tm, tn = 512, 1024

    def body(x_ref, g_ref, sc_ref, sh_ref, w_ref, ws_ref, p_ref, ps_ref, h_ref, h_sc):
        @pl.when(pl.program_id(1) == 0)
        def _():
            xb = x_ref[...]
            r = lax.rsqrt(jnp.mean(xb * xb, axis=-1, keepdims=True) + EPS)
            hb = ((xb * r * g_ref[...]) * sc_ref[...] + sh_ref[...]).astype(bf16)
            h_sc[...] = hb
            h_ref[...] = hb
            ps_ref[...] = _dg(hb, ws_ref[...], 1, 1)

        p_ref[...] = _dg(h_sc[...], w_ref[...], 1, 1)

    vec = pl.BlockSpec((1, D_MODEL), lambda i, j: (0, 0))
    return pl.pallas_call(
        body, name="in_proj", grid=(s_len // tm, N_MAIN // tn),
        in_specs=[pl.BlockSpec((tm, D_MODEL), lambda i, j: (i, 0)), vec, vec, vec,
                  pl.BlockSpec((tn, D_MODEL), lambda i, j: (j, 0)),
                  pl.BlockSpec((N_SMALL, D_MODEL), lambda i, j: (0, 0))],
        out_specs=[pl.BlockSpec((tm, tn), lambda i, j: (i, j)),
                   pl.BlockSpec((tm, N_SMALL), lambda i, j: (i, 0)),
                   pl.BlockSpec((tm, D_MODEL), lambda i, j: (i, 0))],
        out_shape=[jax.ShapeDtypeStruct((s_len, N_MAIN), f32), jax.ShapeDtypeStruct((s_len, N_SMALL), f32),
                   jax.ShapeDtypeStruct((s_len, D_MODEL), bf16)],
        scratch_shapes=[pltpu.VMEM((tm, D_MODEL), bf16)],
        compiler_params=_params("parallel", "arbitrary"),
    )(x, norm_g, scale1p, shift, wt_main, wt_small)


PREP_TM = 256
HALO = 8


def _conv_section(xe_ref, cw_ref, cols, tm):
    acc = cw_ref[pl.ds(CONV_K - 1, 1), cols] * xe_ref[pl.ds(HALO, tm), :]
    for tap in range(CONV_K - 1):
        acc = acc + cw_ref[pl.ds(tap, 1), cols] * xe_ref[pl.ds(HALO - (CONV_K - 1) + tap, tm), :]
    return acc


def _small_fwd(ps, bvec, alog):
    z = ps + bvec
    logsig, softp = _softplus_parts(z)
    gval = -jnp.exp(alog) * softp
    beta = _sigmoid(ps)
    return z, logsig, gval, beta


def _lane_group_selector(first_lane):
    return (_iota((LANES, WIDTH), 0) == first_lane + _iota((LANES, WIDTH), 1) // HEAD_DIM).astype(f32)


def _chunk_masks(tm):
    r, c = _iota((tm, tm), 0), _iota((tm, tm), 1)
    same = (r // CHUNK) == (c // CHUNK)
    return (same & (r >= c)).astype(f32), same.astype(f32)


def _prep(p_main, p_small, qn_g, kn_g, conv_w, bvec, alog):
    s_len = p_main.shape[0]
    tm = PREP_TM
    nb = s_len // tm

    def body(fq_ref, fk_ref, fv_ref, gq_ref, gk_ref, gv_ref, hq_ref, hk_ref, hv_ref, ps_ref, qg_ref, kg_ref,
             cw_ref, bv_ref, al_ref,
             qs_ref, kn_ref, vb_ref, gqo_ref, gko_ref, gvo_ref, small_ref, gcb_ref, glb_ref, bb_ref, xe_sc, carry_sc):
        i = pl.program_id(0)

        @pl.when(i == 0)
        def _():
            carry_sc[...] = jnp.zeros_like(carry_sc)

        qg, kg = qg_ref[...], kg_ref[...]
        for h in range(HEADS):
            sl = slice(h * HEAD_DIM, (h + 1) * HEAD_DIM)
            q = fq_ref[:, sl]
            rq = lax.rsqrt(jnp.mean(q * q, axis=-1, keepdims=True) + EPS)
            qs_ref[:, sl] = (q * rq * qg * QK_SCALE).astype(bf16)
            k = fk_ref[:, sl]
            rk = lax.rsqrt(jnp.mean(k * k, axis=-1, keepdims=True) + EPS)
            kn_ref[:, sl] = (k * rk * kg).astype(bf16)
        vb_ref[...] = fv_ref[...].astype(bf16)

        first = i == 0
        for sec, (x_ref, halo_ref, o_ref) in enumerate(((gq_ref, hq_ref, gqo_ref), (gk_ref, hk_ref, gko_ref),
                                                        (gv_ref, hv_ref, gvo_ref))):
            xe_sc[0:HALO, :] = jnp.where(first, 0.0, halo_ref[...])
            xe_sc[HALO:, :] = x_ref[...]
            cv = _conv_section(xe_sc, cw_ref, slice(sec * WIDTH, (sec + 1) * WIDTH), tm)
            y = cv * _sigmoid(cv)
            if sec == 2:
                o_ref[...] = y
            else:
                mul = QK_SCALE if sec == 0 else 1.0
                for h in range(HEADS):
                    sl = slice(h * HEAD_DIM, (h + 1) * HEAD_DIM)
                    yh = y[:, sl]
                    o_ref[:, sl] = yh * (lax.rsqrt(jnp.sum(yh * yh, axis=-1, keepdims=True) + EPS) * mul)

        lane = _iota((tm, N_SMALL), 1)
        _, logsig, gval, beta = _small_fwd(ps_ref[...], bv_ref[...], al_ref[...])
        lf = jnp.where(lane < HEADS, logsig, 0.0)
        tri = (_iota((tm, tm), 0) >= _iota((tm, tm), 1)).astype(f32)
        fcum = jnp.dot(tri, lf, preferred_element_type=f32, precision=HI) + carry_sc[...]
        carry_sc[...] += jnp.sum(lf, axis=0, keepdims=True)
        small = jnp.where(lane < HEADS, fcum, jnp.where(lane < 2 * HEADS, gval, jnp.where(lane < 3 * HEADS, beta, 0.0)))
        small_ref[...] = small
        tri_c, ones_c = _chunk_masks(tm)
        g_lanes = jnp.where((lane >= HEADS) & (lane < 2 * HEADS), gval, 0.0)
        sel_g = _lane_group_selector(HEADS)
        gc = jnp.dot(tri_c, g_lanes, preferred_element_type=f32, precision=HI)
        gcb_ref[...] = jnp.dot(gc, sel_g, preferred_element_type=f32, precision=HI)
        g_last = jnp.dot(ones_c, g_lanes, preferred_element_type=f32, precision=HI)
        glb_ref[...] = jnp.dot(g_last, sel_g, preferred_element_type=f32, precision=HI)
        bb_ref[...] = jnp.dot(small, _lane_group_selector(2 * HEADS), preferred_element_type=f32, precision=HI)

    def col(cb):
        return pl.BlockSpec((tm, WIDTH), lambda i: (i, cb))

    def halo(cb):
        return pl.BlockSpec((HALO, WIDTH), lambda i: (jnp.maximum(i * (tm // HALO) - 1, 0), cb))

    vec = pl.BlockSpec((1, LANES), lambda i: (0, 0))
    wide_f32 = jax.ShapeDtypeStruct((s_len, WIDTH), f32)
    wide_bf = jax.ShapeDtypeStruct((s_len, WIDTH), bf16)
    out_col = pl.BlockSpec((tm, WIDTH), lambda i: (i, 0))
    return pl.pallas_call(
        body, name="prep", grid=(nb,),
        in_specs=[col(0), col(1), col(2), col(4), col(5), col(6), halo(4), halo(5), halo(6),
                  pl.BlockSpec((tm, N_SMALL), lambda i: (i, 0)), vec, vec,
                  pl.BlockSpec((CONV_K, 3 * WIDTH), lambda i: (0, 0)), vec, vec],
        out_specs=[out_col] * 6 + [pl.BlockSpec((tm, N_SMALL), lambda i: (i, 0)), out_col, out_col, out_col],
        out_shape=[wide_bf, wide_bf, wide_bf, wide_f32, wide_f32, wide_f32,
                   jax.ShapeDtypeStruct((s_len, N_SMALL), f32), wide_f32, wide_f32, wide_f32],
        scratch_shapes=[pltpu.VMEM((tm + HALO, WIDTH), f32), pltpu.VMEM((1, N_SMALL), f32)],
        compiler_params=_params("arbitrary"),
    )(p_main, p_main, p_main, p_main, p_main, p_main, p_main, p_main, p_main, p_small, qn_g, kn_g, conv_w, bvec, alog)


FOX_T = 1024
NEG_BIG = -1e30


def _head_lane(block, head):
    return jnp.sum(jnp.where(_iota(block.shape, 1) == head, block, 0.0), axis=1, keepdims=True)


def _fox_fwd(qs, kn, vb, small, f_row):
    s_len = qs.shape[0]
    t = FOX_T
    nq = s_len // t

    def body(q_ref, k_ref, v_ref, sm_ref, fr_ref, o_ref, lse_ref):
        qi = pl.program_id(1)
        q = q_ref[...]
        fq = _head_lane(sm_ref[...], pl.program_id(0))
        causal = _iota((t, t), 0) >= _iota((t, t), 1)

        def step(j, carry, masked):
            m, l, acc = carry
            rows = pl.ds(pl.multiple_of(j * t, t), t)
            s = _dg(q, k_ref[rows, :], 1, 1) + (fq - fr_ref[0, j])
            if masked:
                s = jnp.where(causal, s, NEG_BIG)
            m_new = jnp.maximum(m, jnp.max(s, axis=-1, keepdims=True))
            p = jnp.exp(s - m_new)
            alpha = jnp.exp(m - m_new)
            l = alpha * l + jnp.sum(p, axis=-1, keepdims=True)
            acc = alpha * acc + jnp.dot(p.astype(bf16), v_ref[rows, :], preferred_element_type=f32)
            return m_new, l, acc

        init = (jnp.full((t, 1), NEG_BIG, f32), jnp.zeros((t, 1), f32), jnp.zeros((t, HEAD_DIM), f32))
        carry = lax.fori_loop(0, qi, lambda j, c: step(j, c, False), init)
        m, l, acc = step(qi, carry, True)
        o_ref[...] = acc / l
        lse_ref[0] = m + jnp.log(l)

    return pl.pallas_call(
        body, name="fox_fwd", grid=(HEADS, nq),
        in_specs=[pl.BlockSpec((t, HEAD_DIM), lambda h, i: (i, h)),
                  pl.BlockSpec((s_len, HEAD_DIM), lambda h, i: (0, h)),
                  pl.BlockSpec((s_len, HEAD_DIM), lambda h, i: (0, h)),
                  pl.BlockSpec((t, N_SMALL), lambda h, i: (i, 0)),
                  pl.BlockSpec((1, nq, 1, t), lambda h, i: (h, 0, 0, 0))],
        out_specs=[pl.BlockSpec((t, HEAD_DIM), lambda h, i: (i, h)),
                   pl.BlockSpec((1, t, 1), lambda h, i: (h, i, 0))],
        out_shape=[jax.ShapeDtypeStruct((s_len, WIDTH), f32), jax.ShapeDtypeStruct((HEADS, s_len, 1), f32)],
        compiler_params=_params("parallel", "arbitrary"),
    )(qs, kn, vb, small, f_row)


def _fox_bwd(qs, kn, vb, do, small, lse, delta, f_row):
    s_len = qs.shape[0]
    t = FOX_T
    nq = s_len // t

    def body(q_ref, do_ref, sm_ref, lse_ref, dl_ref, k_ref, v_ref, fr_ref, dq_ref, dk_ref, dv_ref, df_ref, dfq_ref):
        head, qi = pl.program_id(0), pl.program_id(1)

        @pl.when(qi == 0)
        def _():
            dk_ref[...] = jnp.zeros_like(dk_ref)
            dv_ref[...] = jnp.zeros_like(dv_ref)
            df_ref[...] = jnp.zeros_like(df_ref)

        q, do_b = q_ref[...], do_ref[...]
        a = _head_lane(sm_ref[...], head) - lse_ref[0]
        dl = _head_lane(dl_ref[...], head)
        causal = _iota((t, t), 0) >= _iota((t, t), 1)

        def step(j, carry, masked):
            dq, row_sum = carry
            rows = pl.ds(pl.multiple_of(j * t, t), t)
            kj, vj = k_ref[rows, :], v_ref[rows, :]
            p = jnp.exp(_dg(q, kj, 1, 1) + (a - fr_ref[0, j]))
            if masked:
                p = jnp.where(causal, p, 0.0)
            ds = p * (_dg(do_b, vj, 1, 1) - dl)
            ds_b = ds.astype(bf16)
            dk_ref[rows, :] += _dg(ds_b, q, 0, 0)
            dv_ref[rows, :] += _dg(p.astype(bf16), do_b, 0, 0)
            df_ref[0, j] += -jnp.sum(ds, axis=0, keepdims=True)
            return dq + jnp.dot(ds_b, kj, preferred_element_type=f32), row_sum + jnp.sum(ds, axis=-1, keepdims=True)

        carry = lax.fori_loop(0, qi, lambda j, c: step(j, c, False),
                              (jnp.zeros((t, HEAD_DIM), f32), jnp.zeros((t, 1), f32)))
        dq, row_sum = step(qi, carry, True)
        dq_ref[...] = dq
        dfq_ref[0] = row_sum

    blk = pl.BlockSpec((t, HEAD_DIM), lambda h, i: (i, h))
    full = pl.BlockSpec((s_len, HEAD_DIM), lambda h, i: (0, h))
    colv = pl.BlockSpec((1, t, 1), lambda h, i: (h, i, 0))
    rowv = pl.BlockSpec((1, nq, 1, t), lambda h, i: (h, 0, 0, 0))
    lanes = pl.BlockSpec((t, N_SMALL), lambda h, i: (i, 0))
    wide = jax.ShapeDtypeStruct((s_len, WIDTH), f32)
    return pl.pallas_call(
        body, name="fox_bwd", grid=(HEADS, nq),
        in_specs=[blk, blk, lanes, colv, lanes, full, full, rowv],
        out_specs=[blk, full, full, rowv, colv],
        out_shape=[wide, wide, wide, jax.ShapeDtypeStruct((HEADS, nq, 1, t), f32),
                   jax.ShapeDtypeStruct((HEADS, s_len, 1), f32)],
        compiler_params=_params("parallel", "arbitrary"),
    )(qs, do, small, lse, delta, kn, vb, f_row)


INTRA_CHUNKS = 8
INTRA_INTERLEAVE = 4
SCAN_FWD_CHUNKS = 8
SCAN_BWD_CHUNKS = 4


def _gdn_intra_fwd(gq, gk, gv, gc_b, g_last_b, beta_b):
    s_len = gq.shape[0]
    cpb = INTRA_CHUNKS
    rows_blk = cpb * CHUNK
    n_chunks = s_len // CHUNK

    def body(q_ref, k_ref, v_ref, gc_ref, gl_ref, b_ref, u_ref, w_ref, qg_ref, kd_ref, attn_ref, t_ref, eg_ref):
        ms, rhss = [], []
        for ci in range(cpb):
            rows = pl.ds(ci * CHUNK, CHUNK)
            m, rhs, qg, kd, attn, eg_last = _gdn_intra_pre(q_ref[rows, :], k_ref[rows, :], v_ref[rows, :],
                                                           gc_ref[rows, :], gl_ref[rows, :], b_ref[rows, :],
                                                           _BF_PLAIN[1])
            qg_ref[rows, :] = qg.astype(bf16)
            kd_ref[rows, :] = kd.astype(bf16)
            attn_ref[0, ci] = attn.astype(bf16)
            eg_ref[0, ci] = eg_last
            ms.append(m)
            rhss.append(rhs)
        for ci, (t, rhs) in enumerate(zip(_inv_unit_lower_many(ms), rhss)):
            rows = pl.ds(ci * CHUNK, CHUNK)
            t_ref[0, ci] = t
            uw = _X3_PLAIN[0](t, rhs)
            u_ref[rows, :] = uw[:, :HEAD_DIM]
            w_ref[rows, :] = uw[:, HEAD_DIM:].astype(bf16)

    blk = pl.BlockSpec((rows_blk, HEAD_DIM), lambda h, i: (i, h))
    sq = pl.BlockSpec((1, cpb, CHUNK, CHUNK), lambda h, i: (h, i, 0, 0))
    wide_bf = jax.ShapeDtypeStruct((s_len, WIDTH), bf16)
    return pl.pallas_call(
        body, name="gdn_intra_fwd", grid=(HEADS, s_len // rows_blk),
        in_specs=[blk] * 6,
        out_specs=[blk] * 4 + [sq, sq, pl.BlockSpec((1, cpb, SUBLANES, HEAD_DIM), lambda h, i: (h, i, 0, 0))],
        out_shape=[jax.ShapeDtypeStruct((s_len, WIDTH), f32), wide_bf, wide_bf, wide_bf,
                   jax.ShapeDtypeStruct((HEADS, n_chunks, CHUNK, CHUNK), bf16),
                   jax.ShapeDtypeStruct((HEADS, n_chunks, CHUNK, CHUNK), f32),
                   jax.ShapeDtypeStruct((HEADS, n_chunks, SUBLANES, HEAD_DIM), f32)],
        compiler_params=_params("parallel", "parallel"),
    )(gq, gk, gv, gc_b, g_last_b, beta_b)


def _gdn_scan_fwd(u, w, qg, kd, attn, eg):
    s_len = u.shape[0]
    cpb = SCAN_FWD_CHUNKS
    rows_blk = cpb * CHUNK
    n_chunks = s_len // CHUNK

    def body(u_ref, w_ref, qg_ref, kd_ref, attn_ref, eg_ref, o_ref, st_ref, s_sc):
        @pl.when(pl.program_id(0) == 0)
        def _():
            s_sc[...] = jnp.zeros_like(s_sc)

        def chunk(ci, _):
            rows = pl.ds(pl.multiple_of(ci * CHUNK, CHUNK), CHUNK)
            for h in range(HEADS):
                cols = slice(h * HEAD_DIM, (h + 1) * HEAD_DIM)
                s0 = s_sc[h]
                st_ref[h, ci] = s0
                s0_b = s0.astype(bf16)
                v_new = u_ref[rows, cols] - jnp.dot(w_ref[rows, cols], s0_b, preferred_element_type=f32)
                vn_b = v_new.astype(bf16)
                o_ref[rows, cols] = (jnp.dot(qg_ref[rows, cols], s0_b, preferred_element_type=f32)
                                     + jnp.dot(attn_ref[h, ci], vn_b, preferred_element_type=f32))
                s_sc[h] = _scale_rows(s0, eg_ref[h, ci]) + _dg(kd_ref[rows, cols], vn_b, 0, 0)
            return 0

        lax.fori_loop(0, cpb, chunk, 0)

    row = pl.BlockSpec((rows_blk, WIDTH), lambda i: (i, 0))
    return pl.pallas_call(
        body, name="gdn_scan_fwd", grid=(s_len // rows_blk,),
        in_specs=[row] * 4 + [pl.BlockSpec((HEADS, cpb, CHUNK, CHUNK), lambda i: (0, i, 0, 0)),
                              pl.BlockSpec((HEADS, cpb, SUBLANES, HEAD_DIM), lambda i: (0, i, 0, 0))],
        out_specs=[row, pl.BlockSpec((HEADS, cpb, HEAD_DIM, HEAD_DIM), lambda i: (0, i, 0, 0))],
        out_shape=[jax.ShapeDtypeStruct((s_len, WIDTH), f32),
                   jax.ShapeDtypeStruct((HEADS, n_chunks, HEAD_DIM, HEAD_DIM), f32)],
        scratch_shapes=[pltpu.VMEM((HEADS, HEAD_DIM, HEAD_DIM), f32)],
        compiler_params=_params("arbitrary"),
    )(u, w, qg, kd, attn, eg)


def _gdn_scan_bwd(u, w, qg, kd, attn, eg, states, d_o):
    s_len = u.shape[0]
    cpb = SCAN_BWD_CHUNKS
    rows_blk = cpb * CHUNK
    n_chunks = s_len // CHUNK
    nb = s_len // rows_blk

    def body(u_ref, w_ref, qg_ref, kd_ref, attn_ref, eg_ref, st_ref, do_ref,
             du_ref, dw_ref, dqg_ref, dkd_ref, dattn_ref, deg_ref, ds_sc):
        @pl.when(pl.program_id(0) == 0)
        def _():
            ds_sc[...] = jnp.zeros_like(ds_sc)

        def chunk(step, _):
            ci = cpb - 1 - step
            rows = pl.ds(pl.multiple_of(ci * CHUNK, CHUNK), CHUNK)
            for h in range(HEADS):
                cols = slice(h * HEAD_DIM, (h + 1) * HEAD_DIM)
                s0 = st_ref[h, ci]
                s0_b = s0.astype(bf16)
                ds1 = ds_sc[h]
                ds1_b = ds1.astype(bf16)
                w_b, qg_b, kd_b, attn_b = w_ref[rows, cols], qg_ref[rows, cols], kd_ref[rows, cols], attn_ref[h, ci]
                do_b = do_ref[rows, cols].astype(bf16)
                vn_b = (u_ref[rows, cols] - jnp.dot(w_b, s0_b, preferred_element_type=f32)).astype(bf16)
                dvn = _dg(attn_b, do_b, 0, 0) + jnp.dot(kd_b, ds1_b, preferred_element_type=f32)
                dvn_b = dvn.astype(bf16)
                dattn_ref[h, ci] = _dg(do_b, vn_b, 1, 1)
                dqg_ref[rows, cols] = _dg(do_b, s0_b, 1, 1)
                dkd_ref[rows, cols] = _dg(vn_b, ds1_b, 1, 1)
                du_ref[rows, cols] = dvn
                dw_ref[rows, cols] = -_dg(dvn_b, s0_b, 1, 1)
                eg_last = eg_ref[h, ci]
                ds_sc[h] = _dg(qg_b, do_b, 0, 0) - _dg(w_b, dvn_b, 0, 0) + _scale_rows(ds1, eg_last)
                deg_ref[h, ci] = jnp.sum((ds1 * s0).reshape(HEAD_DIM // SUBLANES, SUBLANES, HEAD_DIM), axis=0)
            return 0

        lax.fori_loop(0, cpb, chunk, 0)

    row = pl.BlockSpec((rows_blk, WIDTH), lambda i: (nb - 1 - i, 0))
    sq = pl.BlockSpec((HEADS, cpb, CHUNK, CHUNK), lambda i: (0, nb - 1 - i, 0, 0))
    egs = pl.BlockSpec((HEADS, cpb, SUBLANES, HEAD_DIM), lambda i: (0, nb - 1 - i, 0, 0))
    wide = jax.ShapeDtypeStruct((s_len, WIDTH), f32)
    return pl.pallas_call(
        body, name="gdn_scan_bwd", grid=(nb,),
        in_specs=[row] * 4 + [sq, egs, pl.BlockSpec((HEADS, cpb, HEAD_DIM, HEAD_DIM), lambda i: (0, nb - 1 - i, 0, 0)), row],
        out_specs=[row] * 4 + [sq, egs],
        out_shape=[wide] * 4 + [jax.ShapeDtypeStruct((HEADS, n_chunks, CHUNK, CHUNK), f32),
                                jax.ShapeDtypeStruct((HEADS, n_chunks, SUBLANES, HEAD_DIM), f32)],
        scratch_shapes=[pltpu.VMEM((HEADS, HEAD_DIM, HEAD_DIM), f32)],
        compiler_params=_params("arbitrary"),
    )(u, w, qg, kd, attn, eg, states, d_o)


def _gdn_intra_bwd(gq, gk, gv, gc_b, g_last_b, beta_b, t_inv, du, dw, dqg, dkd, dattn, deg):
    s_len = gq.shape[0]
    cpb = INTRA_CHUNKS
    rows_blk = cpb * CHUNK

    def body(q_ref, k_ref, v_ref, gc_ref, gl_ref, b_ref, t_ref, du_ref, dw_ref, dqg_ref, dkd_ref, dattn_ref, deg_ref,
             dq_ref, dk_ref, dv_ref, dgc_ref, dgl_ref, db_ref):
        def group(it, _):
            for un in range(INTRA_INTERLEAVE):
                ci = it * INTRA_INTERLEAVE + un
                rows = pl.ds(pl.multiple_of(ci * CHUNK, CHUNK), CHUNK)
                t_known = t_ref[0, ci]
                _, vjp = jax.vjp(lambda q, k, v, gc, gl, b: _gdn_intra(q, k, v, gc, gl, b, t_known),
                                 q_ref[rows, :], k_ref[rows, :], v_ref[rows, :], gc_ref[rows, :], gl_ref[rows, :],
                                 b_ref[rows, :])
                duw = jnp.concatenate([du_ref[rows, :], dw_ref[rows, :]], axis=1)
                dq, dk, dv, dgc, dgl, db = vjp((duw, dqg_ref[rows, :], dkd_ref[rows, :], dattn_ref[0, ci],
                                                deg_ref[0, ci]))
                dq_ref[rows, :] = dq
                dk_ref[rows, :] = dk
                dv_ref[rows, :] = dv
                dgc_ref[rows, :] = dgc
                dgl_ref[rows, :] = dgl
                db_ref[rows, :] = db
            return 0

        lax.fori_loop(0, cpb // INTRA_INTERLEAVE, group, 0)

    blk = pl.BlockSpec((rows_blk, HEAD_DIM), lambda h, i: (i, h))
    sq = pl.BlockSpec((1, cpb, CHUNK, CHUNK), lambda h, i: (h, i, 0, 0))
    egs = pl.BlockSpec((1, cpb, SUBLANES, HEAD_DIM), lambda h, i: (h, i, 0, 0))
    wide = jax.ShapeDtypeStruct((s_len, WIDTH), f32)
    return pl.pallas_call(
        body, name="gdn_intra_bwd", grid=(HEADS, s_len // rows_blk),
        in_specs=[blk] * 6 + [sq] + [blk] * 4 + [sq, egs],
        out_specs=[blk] * 6,
        out_shape=[wide] * 6,
        compiler_params=_params("parallel", "parallel"),
    )(gq, gk, gv, gc_b, g_last_b, beta_b, t_inv, du, dw, dqg, dkd, dattn, deg)


MIX_TM = 256


def _mix_fwd(fox_o, gdn_o, p_main, gnorm_g):
    s_len = fox_o.shape[0]
    tm = MIX_TM

    def body(fo_ref, go_ref, fz_ref, gz_ref, g_ref, mixed_ref):
        fz = fz_ref[...]
        mixed_ref[:, 0:WIDTH] = (fo_ref[...] * (fz * _sigmoid(fz))).astype(bf16)
        gz = gz_ref[...]
        gate = gz * _sigmoid(gz)
        gg = g_ref[...]
        for h in range(HEADS):
            sl = slice(h * HEAD_DIM, (h + 1) * HEAD_DIM)
            o = go_ref[:, sl]
            r = lax.rsqrt(jnp.mean(o * o, axis=-1, keepdims=True) + EPS)
            mixed_ref[:, WIDTH + h * HEAD_DIM:WIDTH + (h + 1) * HEAD_DIM] = (o * r * gg * gate[:, sl]).astype(bf16)

    row = pl.BlockSpec((tm, WIDTH), lambda i: (i, 0))
    return pl.pallas_call(
        body, name="mix_fwd", grid=(s_len // tm,),
        in_specs=[row, row, pl.BlockSpec((tm, WIDTH), lambda i: (i, 3)), pl.BlockSpec((tm, WIDTH), lambda i: (i, 7)),
                  pl.BlockSpec((1, LANES), lambda i: (0, 0))],
        out_specs=pl.BlockSpec((tm, 2 * WIDTH), lambda i: (i, 0)),
        out_shape=jax.ShapeDtypeStruct((s_len, 2 * WIDTH), bf16),
        compiler_params=_params("parallel"),
    )(fox_o, gdn_o, p_main, p_main, gnorm_g)


def _silu_grad(z):
    sg = _sigmoid(z)
    return sg * (1.0 + z * (1.0 - sg))


def _mix_bwd(dmixed, fox_o, gdn_o, p_main, gnorm_g):
    s_len = fox_o.shape[0]
    tm = MIX_TM

    def body(dm_ref, fo_ref, go_ref, fz_ref, gz_ref, g_ref, dof_ref, delta_ref, dfz_ref, dgz_ref, dgo_ref, dg_ref):
        @pl.when(pl.program_id(0) == 0)
        def _():
            dg_ref[...] = jnp.zeros_like(dg_ref)

        lane = _iota((tm, LANES), 1)
        fz = fz_ref[...]
        dmf = dm_ref[:, 0:WIDTH]
        fo = fo_ref[...]
        dof = dmf * (fz * _sigmoid(fz))
        dof_ref[...] = dof.astype(bf16)
        dfz_ref[...] = (dmf * fo * _silu_grad(fz)).astype(bf16)
        prod = dof * fo
        delta = jnp.zeros((tm, LANES), f32)
        for h in range(HEADS):
            dh = jnp.sum(prod[:, h * HEAD_DIM:(h + 1) * HEAD_DIM], axis=-1, keepdims=True)
            delta = jnp.where(lane == h, dh, delta)
        delta_ref[...] = delta

        gz = gz_ref[...]
        dmg = dm_ref[:, WIDTH:2 * WIDTH]
        gate = gz * _sigmoid(gz)
        sgrad = _silu_grad(gz)
        gg = g_ref[...]
        dg_acc = jnp.zeros((1, HEAD_DIM), f32)
        for h in range(HEADS):
            sl = slice(h * HEAD_DIM, (h + 1) * HEAD_DIM)
            o = go_ref[:, sl]
            r = lax.rsqrt(jnp.mean(o * o, axis=-1, keepdims=True) + EPS)
            on = o * r
            dmh = dmg[:, sl]
            dgz_ref[:, sl] = (dmh * (on * gg) * sgrad[:, sl]).astype(bf16)
            dy = dmh * gate[:, sl]
            dg_acc = dg_acc + jnp.sum(dy * on, axis=0, keepdims=True)
            tt = dy * gg
            dgo_ref[:, sl] = r * (tt - on * jnp.mean(tt * on, axis=-1, keepdims=True))
        dg_ref[...] += dg_acc

    row = pl.BlockSpec((tm, WIDTH), lambda i: (i, 0))
    wide_bf = jax.ShapeDtypeStruct((s_len, WIDTH), bf16)
    return pl.pallas_call(
        body, name="mix_bwd", grid=(s_len // tm,),
        in_specs=[pl.BlockSpec((tm, 2 * WIDTH), lambda i: (i, 0)), row, row,
                  pl.BlockSpec((tm, WIDTH), lambda i: (i, 3)), pl.BlockSpec((tm, WIDTH), lambda i: (i, 7)),
                  pl.BlockSpec((1, LANES), lambda i: (0, 0))],
        out_specs=[row, pl.BlockSpec((tm, LANES), lambda i: (i, 0)), row, row, row,
                   pl.BlockSpec((1, LANES), lambda i: (0, 0))],
        out_shape=[wide_bf, jax.ShapeDtypeStruct((s_len, LANES), f32), wide_bf, wide_bf,
                   jax.ShapeDtypeStruct((s_len, WIDTH), f32), jax.ShapeDtypeStruct((1, LANES), f32)],
        compiler_params=_params("arbitrary"),
    )(dmixed, fox_o, gdn_o, p_main, p_main, gnorm_g)


def _out_head(mixed, w_out, x, target, gate, final_g):
    s_len = x.shape[0]
    tm = 256

    def body(mx_ref, w_ref, x_ref, t_ref, gate_ref, fg_ref, loss_ref, dy_ref, dz_ref, dm_ref, dfg_ref, dgate_ref):
        @pl.when(pl.program_id(0) == 0)
        def _():
            loss_ref[...] = jnp.zeros_like(loss_ref)
            dfg_ref[...] = jnp.zeros_like(dfg_ref)
            dgate_ref[...] = jnp.zeros_like(dgate_ref)

        w = w_ref[...]
        z = jnp.dot(mx_ref[...], w, preferred_element_type=f32)
        gate_v, fg = gate_ref[...], fg_ref[...]
        y1 = x_ref[...] + gate_v * z
        r = lax.rsqrt(jnp.mean(y1 * y1, axis=-1, keepdims=True) + EPS)
        yn = y1 * r
        err = yn * fg - t_ref[...]
        loss_ref[...] += 0.5 * jnp.sum(jnp.mean(err * err, axis=-1, keepdims=True))
        dout = err * (1.0 / D_MODEL)
        dfg_ref[...] += jnp.sum(dout * yn, axis=0, keepdims=True)
        tt = dout * fg
        dy1 = r * (tt - yn * jnp.mean(tt * yn, axis=-1, keepdims=True))
        dy_ref[...] = dy1
        dgate_ref[...] += jnp.sum(dy1 * z, axis=0, keepdims=True)
        dz = (dy1 * gate_v).astype(bf16)
        dz_ref[...] = dz
        dm_ref[...] = _dg(dz, w, 1, 1)

    row = pl.BlockSpec((tm, D_MODEL), lambda i: (i, 0))
    vec = pl.BlockSpec((1, D_MODEL), lambda i: (0, 0))
    big = jax.ShapeDtypeStruct((s_len, D_MODEL), f32)
    return pl.pallas_call(
        body, name="out_head", grid=(s_len // tm,),
        in_specs=[row, pl.BlockSpec((D_MODEL, D_MODEL), lambda i: (0, 0)), row, row, vec, vec],
        out_specs=[pl.BlockSpec((1, LANES), lambda i: (0, 0)), row, row, row, vec, vec],
        out_shape=[jax.ShapeDtypeStruct((1, LANES), f32), big, jax.ShapeDtypeStruct((s_len, D_MODEL), bf16), big,
                   jax.ShapeDtypeStruct((1, D_MODEL), f32), jax.ShapeDtypeStruct((1, D_MODEL), f32)],
        compiler_params=_params("arbitrary"),
    )(mixed, w_out, x, target, gate, final_g)


def _matmul_tn(name, a, b, out_dtype):
    k_len, m_len = a.shape
    n_len = b.shape[1]
    tk, tm, tn = 512, min(1024, m_len), min(1024, n_len)
    nk = k_len // tk

    def body(a_ref, b_ref, o_ref, acc_sc):
        k = pl.program_id(2)

        @pl.when(k == 0)
        def _():
            acc_sc[...] = jnp.zeros_like(acc_sc)

        acc_sc[...] += _dg(a_ref[...], b_ref[...], 0, 0)

        @pl.when(k == nk - 1)
        def _():
            o_ref[...] = acc_sc[...].astype(out_dtype)

    return pl.pallas_call(
        body, name=name, grid=(m_len // tm, n_len // tn, nk),
        in_specs=[pl.BlockSpec((tk, tm), lambda i, j, k: (k, i)), pl.BlockSpec((tk, tn), lambda i, j, k: (k, j))],
        out_specs=pl.BlockSpec((tm, tn), lambda i, j, k: (i, j)),
        out_shape=jax.ShapeDtypeStruct((m_len, n_len), out_dtype),
        scratch_shapes=[pltpu.VMEM((tm, tn), f32)],
        compiler_params=_params("parallel", "parallel", "arbitrary"),
    )(a, b)


def _post1(p_main, p_small, qn_g, kn_g, conv_w, bvec, alog, dqs, dkn, dgq, dgk, dgv, dgc_b, dgl_b, dbeta_b, df,
           df_query):
    s_len = p_main.shape[0]
    tm = PREP_TM
    nb = s_len // tm

    def body(fq_ref, fk_ref, gq_ref, gk_ref, gv_ref, hq_ref, hk_ref, hv_ref, ps_ref, qg_ref, kg_ref, cw_ref, bv_ref,
             al_ref, dqs_ref, dkn_ref, dgq_ref, dgk_ref, dgv_ref, dgcb_ref, dglb_ref, dbb_ref, df_ref, dfq_in_ref,
             dfq_ref, dfk_ref, dconv_ref, dps_ref, dqg_ref, dkg_ref, sums_ref, xe_sc, carry_sc):
        step = pl.program_id(0)
        blk = nb - 1 - step

        @pl.when(step == 0)
        def _():
            carry_sc[...] = jnp.zeros_like(carry_sc)
            dqg_ref[...] = jnp.zeros_like(dqg_ref)
            dkg_ref[...] = jnp.zeros_like(dkg_ref)
            sums_ref[...] = jnp.zeros_like(sums_ref)

        for x_ref, g_ref, dy_ref, o_ref, acc_ref, mul in ((fq_ref, qg_ref, dqs_ref, dfq_ref, dqg_ref, QK_SCALE),
                                                          (fk_ref, kg_ref, dkn_ref, dfk_ref, dkg_ref, 1.0)):
            gain = g_ref[...]
            acc = jnp.zeros((1, HEAD_DIM), f32)
            for h in range(HEADS):
                sl = slice(h * HEAD_DIM, (h + 1) * HEAD_DIM)
                xv = x_ref[:, sl]
                r = lax.rsqrt(jnp.mean(xv * xv, axis=-1, keepdims=True) + EPS)
                xn = xv * r
                dy = dy_ref[:, sl] * mul
                acc = acc + jnp.sum(dy * xn, axis=0, keepdims=True)
                tt = dy * gain
                o_ref[:, sl] = (r * (tt - xn * jnp.mean(tt * xn, axis=-1, keepdims=True))).astype(bf16)
            acc_ref[...] += acc

        first = blk == 0
        for sec, (x_ref, halo_ref, dy_ref) in enumerate(((gq_ref, hq_ref, dgq_ref), (gk_ref, hk_ref, dgk_ref),
                                                         (gv_ref, hv_ref, dgv_ref))):
            xe_sc[0:HALO, :] = jnp.where(first, 0.0, halo_ref[...])
            xe_sc[HALO:, :] = x_ref[...]
            cv = _conv_section(xe_sc, cw_ref, slice(sec * WIDTH, (sec + 1) * WIDTH), tm)
            sgrad = _silu_grad(cv)
            if sec == 2:
                dconv_ref[:, sec * WIDTH:(sec + 1) * WIDTH] = dy_ref[...] * sgrad
            else:
                y = cv * _sigmoid(cv)
                mul = QK_SCALE if sec == 0 else 1.0
                for h in range(HEADS):
                    sl = slice(h * HEAD_DIM, (h + 1) * HEAD_DIM)
                    yh = y[:, sl]
                    r = lax.rsqrt(jnp.sum(yh * yh, axis=-1, keepdims=True) + EPS)
                    dqh = dy_ref[:, sl]
                    dyh = (mul * r) * (dqh - yh * (r * r) * jnp.sum(dqh * yh, axis=-1, keepdims=True))
                    dconv_ref[:, sec * WIDTH + h * HEAD_DIM:sec * WIDTH + (h + 1) * HEAD_DIM] = dyh * sgrad[:, sl]

        lane = _iota((tm, N_SMALL), 1)
        z, _, gval, beta = _small_fwd(ps_ref[...], bv_ref[...], al_ref[...])
        sig_z = _sigmoid(z)
        sel_t = (_iota((WIDTH, LANES), 1) == HEADS + _iota((WIDTH, LANES), 0) // HEAD_DIM).astype(f32)
        dgc = jnp.dot(dgcb_ref[...], sel_t, preferred_element_type=f32, precision=HI)
        dgl = jnp.dot(dglb_ref[...], sel_t, preferred_element_type=f32, precision=HI)
        tri_c, ones_c = _chunk_masks(tm)
        dg = (_dg(tri_c, dgc, 0, 0, HI) + jnp.dot(ones_c, dgl, preferred_element_type=f32, precision=HI))
        sel_t2 = (_iota((WIDTH, LANES), 1) == 2 * HEADS + _iota((WIDTH, LANES), 0) // HEAD_DIM).astype(f32)
        dbeta = jnp.dot(dbb_ref[...], sel_t2, preferred_element_type=f32, precision=HI)
        dfb = jnp.where(lane < HEADS, df_ref[...], 0.0)
        for h in range(HEADS):
            dfb = dfb + jnp.where(lane == h, dfq_in_ref[h], 0.0)
        tri_u = (_iota((tm, tm), 1) >= _iota((tm, tm), 0)).astype(f32)
        dlogf = jnp.dot(tri_u, dfb, preferred_element_type=f32, precision=HI) + carry_sc[...]
        carry_sc[...] += jnp.sum(dfb, axis=0, keepdims=True)
        dff = dlogf * (1.0 - sig_z)
        dga = dg * (-jnp.exp(al_ref[...])) * sig_z
        dgb_small = dbeta * beta * (1.0 - beta)
        dps = jnp.where(lane < HEADS, dff, jnp.where(lane < 2 * HEADS, dga, jnp.where(lane < 3 * HEADS, dgb_small, 0.0)))
        dps_ref[...] = dps.astype(bf16)
        row = _iota((8, N_SMALL), 0)
        s0 = jnp.sum(dps, axis=0, keepdims=True)
        s1 = jnp.sum(jnp.where((lane >= HEADS) & (lane < 2 * HEADS), dg * gval, 0.0), axis=0, keepdims=True)
        sums_ref[...] += jnp.where(row == 0, s0, jnp.where(row == 1, s1, 0.0))

    def col(cb):
        return pl.BlockSpec((tm, WIDTH), lambda i: (nb - 1 - i, cb))

    def halo(cb):
        return pl.BlockSpec((HALO, WIDTH), lambda i: (jnp.maximum((nb - 1 - i) * (tm // HALO) - 1, 0), cb))

    vec = pl.BlockSpec((1, LANES), lambda i: (0, 0))
    row0 = pl.BlockSpec((tm, WIDTH), lambda i: (nb - 1 - i, 0))
    small = pl.BlockSpec((tm, N_SMALL), lambda i: (nb - 1 - i, 0))
    wide_bf = jax.ShapeDtypeStruct((s_len, WIDTH), bf16)
    return pl.pallas_call(
        body, name="post1", grid=(nb,),
        in_specs=[col(0), col(1), col(4), col(5), col(6), halo(4), halo(5), halo(6), small, vec, vec,
                  pl.BlockSpec((CONV_K, 3 * WIDTH), lambda i: (0, 0)), vec, vec,
                  row0, row0, row0, row0, row0, row0, row0, row0, small,
                  pl.BlockSpec((HEADS, tm, 1), lambda i: (0, nb - 1 - i, 0))],
        out_specs=[row0, row0, pl.BlockSpec((tm, 3 * WIDTH), lambda i: (nb - 1 - i, 0)), small, vec, vec,
                   pl.BlockSpec((8, N_SMALL), lambda i: (0, 0))],
        out_shape=[wide_bf, wide_bf, jax.ShapeDtypeStruct((s_len, 3 * WIDTH), f32),
                   jax.ShapeDtypeStruct((s_len, N_SMALL), bf16), jax.ShapeDtypeStruct((1, LANES), f32),
                   jax.ShapeDtypeStruct((1, LANES), f32), jax.ShapeDtypeStruct((8, N_SMALL), f32)],
        scratch_shapes=[pltpu.VMEM((tm + HALO, WIDTH), f32), pltpu.VMEM((1, N_SMALL), f32)],
        compiler_params=_params("arbitrary"),
    )(p_main, p_main, p_main, p_main, p_main, p_main, p_main, p_main, p_small, qn_g, kn_g, conv_w, bvec, alog,
      dqs, dkn, dgq, dgk, dgv, dgc_b, dgl_b, dbeta_b, df, df_query)


def _post2(p_main, dconv, conv_w):
    s_len = p_main.shape[0]
    tm = PREP_TM
    nb = s_len // tm

    def body(gq_ref, gk_ref, gv_ref, hq_ref, hk_ref, hv_ref, dc_ref, dnext_ref, cw_ref, dx_ref, dw_ref, xe_sc, de_sc):
        i = pl.program_id(0)

        @pl.when(i == 0)
        def _():
            dw_ref[...] = jnp.zeros_like(dw_ref)

        first, last = i == 0, i == nb - 1
        row = _iota((8, WIDTH), 0)
        for sec, (x_ref, halo_ref) in enumerate(((gq_ref, hq_ref), (gk_ref, hk_ref), (gv_ref, hv_ref))):
            cols = slice(sec * WIDTH, (sec + 1) * WIDTH)
            dc = dc_ref[:, cols]
            de_sc[0:tm, :] = dc
            de_sc[tm:, :] = jnp.where(last, 0.0, dnext_ref[:, cols])
            dx = cw_ref[pl.ds(CONV_K - 1, 1), cols] * dc
            for tap in range(CONV_K - 1):
                dx = dx + cw_ref[pl.ds(tap, 1), cols] * de_sc[pl.ds(CONV_K - 1 - tap, tm), :]
            dx_ref[:, cols] = dx.astype(bf16)
            xe_sc[0:HALO, :] = jnp.where(first, 0.0, halo_ref[...])
            xe_sc[HALO:, :] = x_ref[...]
            dw = jnp.zeros((8, WIDTH), f32)
            for tap in range(CONV_K):
                contrib = jnp.sum(dc * xe_sc[pl.ds(HALO - (CONV_K - 1) + tap, tm), :], axis=0, keepdims=True)
                dw = jnp.where(row == tap, contrib, dw)
            dw_ref[:, cols] += dw

    def col(cb):
        return pl.BlockSpec((tm, WIDTH), lambda i: (i, cb))

    def halo(cb):
        return pl.BlockSpec((HALO, WIDTH), lambda i: (jnp.maximum(i * (tm // HALO) - 1, 0), cb))

    return pl.pallas_call(
        body, name="post2", grid=(nb,),
        in_specs=[col(4), col(5), col(6), halo(4), halo(5), halo(6),
                  pl.BlockSpec((tm, 3 * WIDTH), lambda i: (i, 0)),
                  pl.BlockSpec((HALO, 3 * WIDTH), lambda i: (jnp.minimum((i + 1) * (tm // HALO), s_len // HALO - 1), 0)),
                  pl.BlockSpec((CONV_K, 3 * WIDTH), lambda i: (0, 0))],
        out_specs=[pl.BlockSpec((tm, 3 * WIDTH), lambda i: (i, 0)), pl.BlockSpec((8, 3 * WIDTH), lambda i: (0, 0))],
        out_shape=[jax.ShapeDtypeStruct((s_len, 3 * WIDTH), bf16), jax.ShapeDtypeStruct((8, 3 * WIDTH), f32)],
        scratch_shapes=[pltpu.VMEM((tm + HALO, WIDTH), f32), pltpu.VMEM((tm + HALO, WIDTH), f32)],
        compiler_params=_params("arbitrary"),
    )(p_main, p_main, p_main, p_main, p_main, p_main, dconv, dconv, conv_w)


def _in_proj_bwd(dp_main, dp_small, wt_main, wt_small, x, dy1, norm_g, scale1p):
    s_len = x.shape[0]
    tm, tk = 512, 1024
    nk = N_MAIN // tk

    def body(dp_ref, dps_ref, w_ref, ws_ref, x_ref, dy_ref, g_ref, sc_ref, dx_ref, dsh_ref, dsc_ref, dg_ref, acc_sc):
        i, k = pl.program_id(0), pl.program_id(1)

        @pl.when((i == 0) & (k == 0))
        def _():
            dsh_ref[...] = jnp.zeros_like(dsh_ref)
            dsc_ref[...] = jnp.zeros_like(dsc_ref)
            dg_ref[...] = jnp.zeros_like(dg_ref)

        @pl.when(k == 0)
        def _():
            acc_sc[...] = jnp.dot(dps_ref[...], ws_ref[...], preferred_element_type=f32)

        acc_sc[...] += jnp.dot(dp_ref[...], w_ref[...], preferred_element_type=f32)

        @pl.when(k == nk - 1)
        def _():
            dh = acc_sc[...]
            xb = x_ref[...]
            r = lax.rsqrt(jnp.mean(xb * xb, axis=-1, keepdims=True) + EPS)
            xr = xb * r
            gain = g_ref[...]
            dsh_ref[...] += jnp.sum(dh, axis=0, keepdims=True)
            dsc_ref[...] += jnp.sum(dh * (xr * gain), axis=0, keepdims=True)
            dxn = dh * sc_ref[...]
            dg_ref[...] += jnp.sum(dxn * xr, axis=0, keepdims=True)
            tt = dxn * gain
            dx_ref[...] = r * (tt - xr * jnp.mean(tt * xr, axis=-1, keepdims=True)) + dy_ref[...]

    row = pl.BlockSpec((tm, D_MODEL), lambda i, k: (i, 0))
    vec = pl.BlockSpec((1, D_MODEL), lambda i, k: (0, 0))
    vshape = jax.ShapeDtypeStruct((1, D_MODEL), f32)
    return pl.pallas_call(
        body, name="in_proj_bwd", grid=(s_len // tm, nk),
        in_specs=[pl.BlockSpec((tm, tk), lambda i, k: (i, k)), pl.BlockSpec((tm, N_SMALL), lambda i, k: (i, 0)),
                  pl.BlockSpec((tk, D_MODEL), lambda i, k: (k, 0)), pl.BlockSpec((N_SMALL, D_MODEL), lambda i, k: (0, 0)),
                  row, row, vec, vec],
        out_specs=[row, vec, vec, vec],
        out_shape=[jax.ShapeDtypeStruct((s_len, D_MODEL), f32), vshape, vshape, vshape],
        scratch_shapes=[pltpu.VMEM((tm, D_MODEL), f32)],
        compiler_params=_params("arbitrary", "arbitrary"),
    )(dp_main, dp_small, wt_main, wt_small, x, dy1, norm_g, scale1p)


def _adamw(name, w, g_stack, m, v, tr, tc=None):
    n_stack, rows, cols = g_stack.shape
    tc = cols if tc is None else tc

    def body(w_ref, g_ref, m_ref, v_ref, go_ref, d_ref, mo_ref, vo_ref):
        g = g_ref[0].astype(f32)
        for k in range(1, n_stack):
            g = g + g_ref[k].astype(f32)
        go_ref[0] = g
        m_new = ADAM_B1 * m_ref[0] + (1.0 - ADAM_B1) * g
        v_new = ADAM_B2 * v_ref[0] + (1.0 - ADAM_B2) * (g * g)
        mo_ref[0] = m_new
        vo_ref[0] = v_new
        m_hat = m_new / (1.0 - ADAM_B1 ** ADAM_STEP)
        v_hat = v_new / (1.0 - ADAM_B2 ** ADAM_STEP)
        d_ref[0] = -ADAM_LR * (m_hat / (jnp.sqrt(v_hat) + ADAM_EPS) + ADAM_WD * w_ref[0])

    blk = pl.BlockSpec((1, tr, tc), lambda i, j: (0, i, j))
    shape = jax.ShapeDtypeStruct((1, rows, cols), f32)
    return pl.pallas_call(
        body, name=name, grid=(rows // tr, cols // tc),
        in_specs=[blk, pl.BlockSpec((n_stack, tr, tc), lambda i, j: (0, i, j)), blk, blk],
        out_specs=[blk] * 4, out_shape=[shape] * 4,
        compiler_params=_params("parallel", "parallel"),
    )(w, g_stack, m, v)


def _w_ada_grad(c_all_t, dmod_pad):
    def body(c_ref, d_ref, o_ref):
        cv = c_ref[...]
        o_ref[...] = jnp.dot(cv * _sigmoid(cv), d_ref[...], preferred_element_type=f32, precision=HI)

    return pl.pallas_call(body, name="w_ada_grad",
                          out_shape=jax.ShapeDtypeStruct((c_all_t.shape[0], dmod_pad.shape[1]), f32),
                          compiler_params=_params())(c_all_t, dmod_pad)


SMALL_NAMES = ("norm_g", "b_ada", "b_fgate", "fox_qn_g", "fox_kn_g", "gdn_A_log", "gdn_dt_bias", "gdn_norm_g", "final_g")
SMALL_SIZES = (D_MODEL, 3 * D_MODEL, HEADS, HEAD_DIM, HEAD_DIM, HEADS, HEADS, HEAD_DIM, D_MODEL)
SMALL_PACK = 10752


def _pack(vectors, total):
    flat = jnp.concatenate([t.reshape(-1) for t in vectors])
    return jnp.pad(flat, (0, total - flat.shape[0])).reshape(1, total)


def _lanes(*pieces):
    row = jnp.zeros((LANES,), f32)
    for off, vec in pieces:
        row = lax.dynamic_update_slice(row, vec.reshape(-1).astype(f32), (off,))
    return row.reshape(1, LANES)


def kernel(x, c, norm_g, w_ada, b_ada, w_in, b_fgate, fox_qn_g, fox_kn_g, gdn_conv_w, gdn_A_log, gdn_dt_bias, gdn_norm_g, w_out, final_g, loss_target, m_norm_g, m_w_ada, m_b_ada, m_w_in, m_b_fgate, m_fox_qn_g, m_fox_kn_g, m_gdn_conv_w, m_gdn_A_log, m_gdn_dt_bias, m_gdn_norm_g, m_w_out, m_final_g, v_norm_g, v_w_ada, v_b_ada, v_w_in, v_b_fgate, v_fox_qn_g, v_fox_kn_g, v_gdn_conv_w, v_gdn_A_log, v_gdn_dt_bias, v_gdn_norm_g, v_w_out, v_final_g):
    me = _my_index()
    s_len = x.shape[1]
    nq = s_len // FOX_T
    x2 = x.reshape(s_len, D_MODEL)
    tgt = loss_target.reshape(s_len, D_MODEL)
    ada_cols = w_ada.shape[2]
    in_cols = w_in.shape[2]
    conv_cols = gdn_conv_w.shape[2]

    (c_all,) = _exchange("gather_c", [c], scatter=False)
    c_all = c_all.reshape(N_DEV, D_MODEL)
    b_shard = lax.dynamic_slice(b_ada, (0, me * ada_cols), (1, ada_cols))
    mod_mine = _mod_shard(c_all, w_ada[0], b_shard)
    wt_shard = jnp.transpose(w_in[0])
    mod_all, wt_all, w_out_all, conv_all = _gather_two_level(
        "gather_weights", [mod_mine, wt_shard.astype(bf16), w_out[0].astype(bf16), gdn_conv_w[0]])
    mod = lax.dynamic_slice(mod_all, (0, me, 0), (N_DEV, 1, ada_cols)).reshape(1, 3 * D_MODEL)
    shift, scale, gate = mod[:, :D_MODEL], mod[:, D_MODEL:2 * D_MODEL], mod[:, 2 * D_MODEL:]
    scale1p = 1.0 + scale
    wt_full = wt_all.reshape(N_DEV * in_cols, D_MODEL)
    g0 = 4 * WIDTH + HEADS
    w_main = jnp.concatenate([wt_full[:4 * WIDTH], wt_full[g0:g0 + 4 * WIDTH]], axis=0)
    w_small = jnp.concatenate([wt_full[4 * WIDTH:g0], wt_full[g0 + 4 * WIDTH:],
                               jnp.zeros((N_SMALL - 3 * HEADS, D_MODEL), bf16)], axis=0)
    w_out_full = w_out_all.reshape(2 * WIDTH, D_MODEL)
    conv_full = jnp.transpose(conv_all, (1, 0, 2)).reshape(CONV_K, 3 * WIDTH)

    qn_g, kn_g, gn_g = fox_qn_g.reshape(1, LANES), fox_kn_g.reshape(1, LANES), gdn_norm_g.reshape(1, LANES)
    bvec = _lanes((0, b_fgate), (HEADS, gdn_dt_bias))
    alog = _lanes((HEADS, gdn_A_log))
    fg = final_g.reshape(1, D_MODEL)

    p_main, p_small, h_bf = _in_proj(x2, norm_g, scale1p, shift, w_main, w_small)
    qs, kn, vb, gq, gk, gv, small, gc_b, gl_b, beta_b = _prep(p_main, p_small, qn_g, kn_g, conv_full, bvec, alog)
    f_row = jnp.transpose(small[:, :HEADS]).reshape(HEADS, nq, 1, FOX_T)
    fox_o, lse = _fox_fwd(qs, kn, vb, small, f_row)
    gu, gw, gqg, gkd, gattn, t_inv, eg_last = _gdn_intra_fwd(gq, gk, gv, gc_b, gl_b, beta_b)
    gdn_o, states = _gdn_scan_fwd(gu, gw, gqg, gkd, gattn, eg_last)
    mixed = _mix_fwd(fox_o, gdn_o, p_main, gn_g)

    loss_row, dy1, dz, dmixed, d_final_g, d_gate = _out_head(mixed, w_out_full, x2, tgt, gate, fg)
    loss = lax.psum(loss_row[0, 0], AXES)
    dw_out = _matmul_tn("dw_out", mixed, dz, bf16)
    do_fox, delta, dfz, dgz, dgdn_o, d_gn_g = _mix_bwd(dmixed, fox_o, gdn_o, p_main, gn_g)
    dqs, dkn, dvf, df_key, df_query = _fox_bwd(qs, kn, vb, do_fox, small, lse, delta, f_row)
    du, dw, dqg, dkd, dattn, deg = _gdn_scan_bwd(gu, gw, gqg, gkd, gattn, eg_last, states, dgdn_o)
    dgq, dgk, dgv, dgc_b, dgl_b, dbeta_b = _gdn_intra_bwd(gq, gk, gv, gc_b, gl_b, beta_b, t_inv, du, dw, dqg, dkd,
                                                          dattn, deg)
    df_small = jnp.pad(jnp.transpose(df_key.reshape(HEADS, s_len)), ((0, 0), (0, N_SMALL - HEADS)))
    dfq, dfk, dconv, dp_small, d_qn_g, d_kn_g, sums = _post1(
        p_main, p_small, qn_g, kn_g, conv_full, bvec, alog, dqs, dkn, dgq, dgk, dgv, dgc_b, dgl_b, dbeta_b, df_small,
        df_query)
    dgqkv, d_conv = _post2(p_main, dconv, conv_full)
    dp_main = jnp.concatenate([dfq, dfk, dvf.astype(bf16), dfz, dgqkv, dgz], axis=1)
    grad_x, d_shift, d_scale, d_norm_g = _in_proj_bwd(dp_main, dp_small, w_main, w_small, x2, dy1, norm_g, scale1p)
    dw_main = _matmul_tn("dw_main", dp_main, h_bf, bf16)
    dw_small = _matmul_tn("dw_small", dp_small, h_bf, bf16)
    dw_in_full = jnp.concatenate([dw_main[:4 * WIDTH], dw_small[:HEADS], dw_main[4 * WIDTH:],
                                  dw_small[HEADS:3 * HEADS]], axis=0)
    dw_in_parts = dw_in_full.reshape(N_DEV, in_cols, D_MODEL)
    dw_out_parts = dw_out.reshape(N_DEV, w_out.shape[1], D_MODEL)

    dmod = jnp.concatenate([d_shift, d_scale, d_gate], axis=1)
    small_grads = _pack([d_norm_g, dmod, sums[0, :HEADS], d_qn_g, d_kn_g, sums[1, HEADS:2 * HEADS],
                         sums[0, HEADS:2 * HEADS], d_gn_g, d_final_g], SMALL_PACK)
    conv_grad = d_conv[:CONV_K]
    dw_in_recv, dw_out_recv = _exchange("scatter_grads", [dw_in_parts, dw_out_parts], scatter=True)
    small_all, conv_all_g = _exchange("gather_small_grads", [small_grads, conv_grad], scatter=False)

    outs = {}
    to_t = lambda t: jnp.transpose(t, (0, 2, 1))
    outs["w_in"] = tuple(to_t(t) for t in _adamw("adamw_w_in", to_t(w_in), dw_in_recv, to_t(m_w_in), to_t(v_w_in),
                                                  in_cols, 256))
    outs["w_out"] = _adamw("adamw_w_out", w_out, dw_out_recv, m_w_out, v_w_out, 128)
    conv_mine = lax.dynamic_slice(jnp.transpose(conv_all_g.reshape(N_DEV, CONV_K, N_DEV, conv_cols), (0, 2, 1, 3)),
                                  (0, me, 0, 0), (N_DEV, 1, CONV_K, conv_cols)).reshape(N_DEV, CONV_K, conv_cols)
    outs["gdn_conv_w"] = _adamw("adamw_conv", gdn_conv_w, conv_mine, m_gdn_conv_w, v_gdn_conv_w, CONV_K)
    small_all = small_all.reshape(N_DEV, 1, SMALL_PACK)
    dmod_all = small_all[:, 0, D_MODEL:D_MODEL + 3 * D_MODEL]
    dmod_mine = lax.dynamic_slice(dmod_all, (0, me * ada_cols), (N_DEV, ada_cols))
    c_all_t = jnp.pad(jnp.transpose(c_all), ((0, 0), (0, LANES - N_DEV)))
    g_w_ada = _w_ada_grad(c_all_t, jnp.pad(dmod_mine, ((0, LANES - N_DEV), (0, 0))))
    outs["w_ada"] = _adamw("adamw_w_ada", w_ada, g_w_ada[None], m_w_ada, v_w_ada, 256)
    given = dict(norm_g=(norm_g, m_norm_g, v_norm_g), b_ada=(b_ada, m_b_ada, v_b_ada), b_fgate=(b_fgate, m_b_fgate, v_b_fgate),
                 fox_qn_g=(fox_qn_g, m_fox_qn_g, v_fox_qn_g), fox_kn_g=(fox_kn_g, m_fox_kn_g, v_fox_kn_g),
                 gdn_A_log=(gdn_A_log, m_gdn_A_log, v_gdn_A_log), gdn_dt_bias=(gdn_dt_bias, m_gdn_dt_bias, v_gdn_dt_bias),
                 gdn_norm_g=(gdn_norm_g, m_gdn_norm_g, v_gdn_norm_g), final_g=(final_g, m_final_g, v_final_g))
    w_pack = _pack([given[n][0] for n in SMALL_NAMES], SMALL_PACK)
    m_pack = _pack([given[n][1] for n in SMALL_NAMES], SMALL_PACK)
    v_pack = _pack([given[n][2] for n in SMALL_NAMES], SMALL_PACK)
    packed = _adamw("adamw_small", w_pack[None], small_all, m_pack[None], v_pack[None], 1)
    off = 0
    for n, size in zip(SMALL_NAMES, SMALL_SIZES):
        outs[n] = tuple(t[0, 0, off:off + size].reshape(given[n][0].shape) for t in packed)
        off += size

    order = ("norm_g", "w_ada", "b_ada", "w_in", "b_fgate", "fox_qn_g", "fox_kn_g", "gdn_conv_w", "gdn_A_log",
             "gdn_dt_bias", "gdn_norm_g", "w_out", "final_g")
    result = [loss, grad_x.reshape(x.shape)]
    for part in range(4):
        result += [outs[n][part] for n in order]
    return tuple(result)
```

```python
import math

import jax
import jax.numpy as jnp
from jax import lax
from jax.experimental import pallas as pl
from jax.experimental.pallas import tpu as pltpu

f32 = jnp.float32
bf16 = jnp.bfloat16
HI = lax.Precision.HIGHEST

N_DEV = 8
AXES = ("x", "y", "c")
D_MODEL = 2048
HEADS = 8
HEAD_DIM = 128
WIDTH = HEADS * HEAD_DIM
CHUNK = 64
CONV_K = 4
EPS = 1e-6
QK_SCALE = HEAD_DIM ** -0.5
N_MAIN = 8 * WIDTH
N_SMALL = 128
IN_WIDTH = 8 * WIDTH + 3 * HEADS
LANES = 128
VMEM_LIMIT = 56 * 1024 * 1024

ADAM_LR, ADAM_B1, ADAM_B2, ADAM_EPS, ADAM_WD, ADAM_STEP = 0.001, 0.9, 0.999, 1e-08, 0.01, 10


def _params(*sem):
    return pltpu.CompilerParams(dimension_semantics=sem, vmem_limit_bytes=VMEM_LIMIT)


def _iota(shape, dim):
    return lax.broadcasted_iota(jnp.int32, shape, dim)


def _sigmoid(z):
    return 1.0 / (1.0 + jnp.exp(-z))


def _softplus_parts(z):
    t = jnp.log(1.0 + jnp.exp(-jnp.abs(z)))
    return jnp.minimum(z, 0.0) - t, jnp.maximum(z, 0.0) + t


def _dg(a, b, ca, cb, prec=None):
    if a.ndim == 3:
        dims = (((ca + 1,), (cb + 1,)), ((0,), (0,)))
    else:
        dims = (((ca,), (cb,)), ((), ()))
    return lax.dot_general(a, b, dims, preferred_element_type=f32, precision=prec)


def _dot_bf16(a, b, ca, cb):
    return _dg(a.astype(bf16), b.astype(bf16), ca, cb)


def _split_bf16(a):
    hi = a.astype(bf16)
    return hi, (a - hi.astype(f32)).astype(bf16)


def _dot_3pass(a, b, ca, cb):
    a_hi, a_lo = _split_bf16(a)
    b_hi, b_lo = _split_bf16(b)
    return _dg(a_hi, b_hi, ca, cb) + (_dg(a_hi, b_lo, ca, cb) + _dg(a_lo, b_hi, ca, cb))


def _make_mm(dot):
    def nn_(a, b):
        return dot(a, b, 1, 0)

    def nt_(a, b):
        return dot(a, b, 1, 1)

    def tn_(a, b):
        return dot(a, b, 0, 0)

    @jax.custom_vjp
    def nn(a, b):
        return nn_(a, b)

    @jax.custom_vjp
    def nt(a, b):
        return nt_(a, b)

    @jax.custom_vjp
    def tn(a, b):
        return tn_(a, b)

    nn.defvjp(lambda a, b: (nn_(a, b), (a, b)), lambda r, g: (nt_(g, r[1]), tn_(r[0], g)))
    nt.defvjp(lambda a, b: (nt_(a, b), (a, b)), lambda r, g: (nn_(g, r[1]), tn_(g, r[0])))
    tn.defvjp(lambda a, b: (tn_(a, b), (a, b)), lambda r, g: (nt_(r[1], g), nn_(r[0], g)))
    return (nn_, nt_, tn_), (nn, nt, tn)


_BF_PLAIN, _BF_VJP = _make_mm(_dot_bf16)
_X3_PLAIN, _X3_VJP = _make_mm(_dot_3pass)


def _inv_unit_lower_many(ms):
    c = CHUNK
    nn = _X3_PLAIN[0]
    eye = (_iota((c, c), 0) == _iota((c, c), 1)).astype(f32)
    top = _iota((2 * c, c), 0) < c
    xs = [jnp.concatenate([eye - m, nn(m, m)], axis=0) for m in ms]
    for _ in range(int(math.log2(CHUNK)) - 2):
        xs = [jnp.where(top, x, 0.0) + nn(x, x[c:]) for x in xs]
    return [x[:c] + nn(x[:c], x[c:]) for x in xs]


@jax.custom_vjp
def _inv_given(m, t):
    return t


_inv_given.defvjp(lambda m, t: (t, t),
                  lambda t, g: (-_X3_PLAIN[1](_X3_PLAIN[2](t, g), t), jnp.zeros_like(t)))

SUBLANES = 8


def _gdn_intra_pre(q, k, v, gc_b, g_last_b, beta_b, bnt):
    c = CHUNK
    r_i, c_i = _iota((c, c), 0), _iota((c, c), 1)
    lower, strict = r_i >= c_i, r_i > c_i
    gc_i = gc_b[..., :c]
    gc_j = jnp.swapaxes(gc_i, -1, -2)
    decay = jnp.where(lower, jnp.exp(jnp.where(lower, gc_i - gc_j, 0.0)), 0.0)
    kb = k * beta_b
    both = bnt(jnp.concatenate([kb, q], axis=-2), k)
    m = jnp.where(strict, both[..., :c, :] * decay, 0.0)
    attn = jnp.where(lower, both[..., c:, :] * decay, 0.0)
    eg = jnp.exp(gc_b)
    rhs = jnp.concatenate([v * beta_b, kb * eg], axis=-1)
    k_dec = k * jnp.exp(g_last_b - gc_b)
    eg_last = jnp.exp(g_last_b[..., :SUBLANES, :])
    return m, rhs, q * eg, k_dec, attn, eg_last


def _gdn_intra(q, k, v, gc_b, g_last_b, beta_b, t_known):
    m, rhs, qg, k_dec, attn, eg_last = _gdn_intra_pre(q, k, v, gc_b, g_last_b, beta_b, _BF_VJP[1])
    return _X3_VJP[0](_inv_given(m, t_known), rhs), qg, k_dec, attn, eg_last


def _scale_rows(s, eg_last):
    return (s.reshape(HEAD_DIM // SUBLANES, SUBLANES, HEAD_DIM) * eg_last[None]).reshape(HEAD_DIM, HEAD_DIM)


def _my_index():
    return 4 * lax.axis_index("x") + 2 * lax.axis_index("y") + lax.axis_index("c")


def _peer(d):
    x, y, c = lax.axis_index("x"), lax.axis_index("y"), lax.axis_index("c")
    px, py, pc = (x + (d >> 2)) % 2, (y + ((d >> 1) & 1)) % 2, (c + (d & 1)) % 2
    return (px, py, pc), 4 * px + 2 * py + pc


def _exchange(name, arrays, scatter):
    n = len(arrays)

    def body(*refs):
        srcs, dsts = refs[:n], refs[n:2 * n]
        send_sems, recv_sems, local_sems = refs[2 * n:]
        me = _my_index()

        def remote(k, d):
            peer, pidx = _peer(d)
            src = srcs[k].at[pidx] if scatter else srcs[k]
            return pltpu.make_async_remote_copy(
                src_ref=src, dst_ref=dsts[k].at[me], send_sem=send_sems.at[k * 7 + d - 1],
                recv_sem=recv_sems.at[k * 7 + d - 1], device_id=peer, device_id_type=pl.DeviceIdType.MESH)

        def arrival(k, d):
            peer, pidx = _peer(d)
            src = srcs[k].at[pidx] if scatter else srcs[k]
            return pltpu.make_async_remote_copy(
                src_ref=src, dst_ref=dsts[k].at[pidx], send_sem=send_sems.at[k * 7 + d - 1],
                recv_sem=recv_sems.at[k * 7 + d - 1], device_id=peer, device_id_type=pl.DeviceIdType.MESH)

        local = [pltpu.make_async_copy(srcs[k].at[me] if scatter else srcs[k], dsts[k].at[me], local_sems.at[k])
                 for k in range(n)]
        sends = [remote(k, d) for k in range(n) for d in range(1, N_DEV)]
        for cp in local + sends:
            cp.start()
        for k in range(n):
            for d in range(1, N_DEV):
                arrival(k, d).wait_recv()
        for cp in sends:
            cp.wait_send()
        for cp in local:
            cp.wait()

    if scatter:
        out_shape = [jax.ShapeDtypeStruct(a.shape, a.dtype) for a in arrays]
    else:
        out_shape = [jax.ShapeDtypeStruct((N_DEV,) + a.shape, a.dtype) for a in arrays]
    any_spec = pl.BlockSpec(memory_space=pl.ANY)
    return pl.pallas_call(
        body, name=name, out_shape=out_shape, in_specs=[any_spec] * n, out_specs=[any_spec] * n,
        scratch_shapes=[pltpu.SemaphoreType.DMA((7 * n,)), pltpu.SemaphoreType.DMA((7 * n,)),
                        pltpu.SemaphoreType.DMA((n,))],
        compiler_params=pltpu.CompilerParams(has_side_effects=True),
    )(*arrays)


def _gather_two_level(name, arrays):
    n = len(arrays)

    def body(*refs):
        srcs, dsts = refs[:n], refs[n:2 * n]
        send_sems, recv_sems, local_sems = refs[2 * n:]
        x, y, c = lax.axis_index("x"), lax.axis_index("y"), lax.axis_index("c")
        sibling = (x, y, 1 - c)
        chips = [((x + 1) % 2, y), (x, (y + 1) % 2), ((x + 1) % 2, (y + 1) % 2)]

        def index(px, py, pc):
            return 4 * px + 2 * py + pc

        def copy(k, slot, block, to, src=None):
            return pltpu.make_async_remote_copy(
                src_ref=dsts[k].at[index(*block)] if src is None else src, dst_ref=dsts[k].at[index(*block)],
                send_sem=send_sems.at[k * 7 + slot], recv_sem=recv_sems.at[k * 7 + slot],
                device_id=to, device_id_type=pl.DeviceIdType.MESH)

        me = (x, y, c)
        local = [pltpu.make_async_copy(srcs[k], dsts[k].at[index(*me)], local_sems.at[k]) for k in range(n)]
        first = [copy(k, 0, me, sibling, src=srcs[k]) for k in range(n)]
        first += [copy(k, 1 + j, me, (*chip, c), src=srcs[k]) for j, chip in enumerate(chips) for k in range(n)]
        for cp in local + first:
            cp.start()
        passed = []
        for j, chip in enumerate(chips):
            for k in range(n):
                copy(k, 1 + j, (*chip, c), me).wait_recv()
                fwd = copy(k, 4 + j, (*chip, c), sibling)
                fwd.start()
                passed.append(fwd)
        for k in range(n):
            copy(k, 0, sibling, me).wait_recv()
            for j, chip in enumerate(chips):
                copy(k, 4 + j, (*chip, 1 - c), me).wait_recv()
        for cp in first + passed:
            cp.wait_send()
        for cp in local:
            cp.wait()

    any_spec = pl.BlockSpec(memory_space=pl.ANY)
    return pl.pallas_call(
        body, name=name, out_shape=[jax.ShapeDtypeStruct((N_DEV,) + a.shape, a.dtype) for a in arrays],
        in_specs=[any_spec] * n, out_specs=[any_spec] * n,
        scratch_shapes=[pltpu.SemaphoreType.DMA((7 * n,)), pltpu.SemaphoreType.DMA((7 * n,)),
                        pltpu.SemaphoreType.DMA((n,))],
        compiler_params=pltpu.CompilerParams(has_side_effects=True),
    )(*arrays)


def _mod_shard(c_all, w_ada, b_shard):
    def body(c_ref, w_ref, b_ref, o_ref):
        cv = c_ref[...]
        ca = cv * _sigmoid(cv)
        o_ref[...] = jnp.dot(ca.astype(bf16), w_ref[...].astype(bf16), preferred_element_type=f32) + b_ref[...]

    return pl.pallas_call(body, name="mod_shard", out_shape=jax.ShapeDtypeStruct((N_DEV, w_ada.shape[1]), f32),
                          compiler_params=_params())(c_all, w_ada, b_shard)


def _in_proj(x, norm_g, scale1p, shift, wt_main, wt_small):
    s_len = x.shape[0]
    tm, tn = 512, 1024

    def body(x_ref, g_ref, sc_ref, sh_ref, w_ref, ws_ref, p_ref, ps_ref, h_ref, h_sc):
        @pl.when(pl.program_id(1) == 0)
        def _():
            xb = x_ref[...]
            r = lax.rsqrt(jnp.mean(xb * xb, axis=-1, keepdims=True) + EPS)
            hb = ((xb * r * g_ref[...]) * sc_ref[...] + sh_ref[...]).astype(bf16)
            h_sc[...] = hb
            h_ref[...] = hb
            ps_ref[...] = _dg(hb, ws_ref[...], 1, 1)

        p_ref[...] = _dg(h_sc[...], w_ref[...], 1, 1)

    vec = pl.BlockSpec((1, D_MODEL), lambda i, j: (0, 0))
    return pl.pallas_call(
        body, name="in_proj", grid=(s_len // tm, N_MAIN // tn),
        in_specs=[pl.BlockSpec((tm, D_MODEL), lambda i, j: (i, 0)), vec, vec, vec,
                  pl.BlockSpec((tn, D_MODEL), lambda i, j: (j, 0)),
                  pl.BlockSpec((N_SMALL, D_MODEL), lambda i, j: (0, 0))],
        out_specs=[pl.BlockSpec((tm, tn), lambda i, j: (i, j)),
                   pl.BlockSpec((tm, N_SMALL), lambda i, j: (i, 0)),
                   pl.BlockSpec((tm, D_MODEL), lambda i, j: (i, 0))],
        out_shape=[jax.ShapeDtypeStruct((s_len, N_MAIN), f32), jax.ShapeDtypeStruct((s_len, N_SMALL), f32),
                   jax.ShapeDtypeStruct((s_len, D_MODEL), bf16)],
        scratch_shapes=[pltpu.VMEM((tm, D_MODEL), bf16)],
        compiler_params=_params("parallel", "arbitrary"),
    )(x, norm_g, scale1p, shift, wt_main, wt_small)


PREP_TM = 256
HALO = 8


def _conv_section(xe_ref, cw_ref, cols, tm):
    acc = cw_ref[pl.ds(CONV_K - 1, 1), cols] * xe_ref[pl.ds(HALO, tm), :]
    for tap in range(CONV_K - 1):
        acc = acc + cw_ref[pl.ds(tap, 1), cols] * xe_ref[pl.ds(HALO - (CONV_K - 1) + tap, tm), :]
    return acc


def _small_fwd(ps, bvec, alog):
    z = ps + bvec
    logsig, softp = _softplus_parts(z)
    gval = -jnp.exp(alog) * softp
    beta = _sigmoid(ps)
    return z, logsig, gval, beta


def _lane_group_selector(first_lane):
    return (_iota((LANES, WIDTH), 0) == first_lane + _iota((LANES, WIDTH), 1) // HEAD_DIM).astype(f32)


def _chunk_masks(tm):
    r, c = _iota((tm, tm), 0), _iota((tm, tm), 1)
    same = (r // CHUNK) == (c // CHUNK)
    return (same & (r >= c)).astype(f32), same.astype(f32)


def _prep(p_main, p_small, qn_g, kn_g, conv_w, bvec, alog):
    s_len = p_main.shape[0]
    tm = PREP_TM
    nb = s_len // tm

    def body(fq_ref, fk_ref, fv_ref, gq_ref, gk_ref, gv_ref, hq_ref, hk_ref, hv_ref, ps_ref, qg_ref, kg_ref,
             cw_ref, bv_ref, al_ref,
             qs_ref, kn_ref, vb_ref, gqo_ref, gko_ref, gvo_ref, small_ref, gcb_ref, glb_ref, bb_ref, xe_sc, carry_sc):
        i = pl.program_id(0)

        @pl.when(i == 0)
        def _():
            carry_sc[...] = jnp.zeros_like(carry_sc)

        qg, kg = qg_ref[...], kg_ref[...]
        for h in range(HEADS):
            sl = slice(h * HEAD_DIM, (h + 1) * HEAD_DIM)
            q = fq_ref[:, sl]
            rq = lax.rsqrt(jnp.mean(q * q, axis=-1, keepdims=True) + EPS)
            qs_ref[:, sl] = (q * rq * qg * QK_SCALE).astype(bf16)
            k = fk_ref[:, sl]
            rk = lax.rsqrt(jnp.mean(k * k, axis=-1, keepdims=True) + EPS)
            kn_ref[:, sl] = (k * rk * kg).astype(bf16)
        vb_ref[...] = fv_ref[...].astype(bf16)

        first = i == 0
        for sec, (x_ref, halo_ref, o_ref) in enumerate(((gq_ref, hq_ref, gqo_ref), (gk_ref, hk_ref, gko_ref),
                                                        (gv_ref, hv_ref, gvo_ref))):
            xe_sc[0:HALO, :] = jnp.where(first, 0.0, halo_ref[...])
            xe_sc[HALO:, :] = x_ref[...]
            cv = _conv_section(xe_sc, cw_ref, slice(sec * WIDTH, (sec + 1) * WIDTH), tm)
            y = cv * _sigmoid(cv)
            if sec == 2:
                o_ref[...] = y
            else:
                mul = QK_SCALE if sec == 0 else 1.0
                for h in range(HEADS):
                    sl = slice(h * HEAD_DIM, (h + 1) * HEAD_DIM)
                    yh = y[:, sl]
                    o_ref[:, sl] = yh * (lax.rsqrt(jnp.sum(yh * yh, axis=-1, keepdims=True) + EPS) * mul)

        lane = _iota((tm, N_SMALL), 1)
        _, logsig, gval, beta = _small_fwd(ps_ref[...], bv_ref[...], al_ref[...])
        lf = jnp.where(lane < HEADS, logsig, 0.0)
        tri = (_iota((tm, tm), 0) >= _iota((tm, tm), 1)).astype(f32)
        fcum = jnp.dot(tri, lf, preferred_element_type=f32, precision=HI) + carry_sc[...]
        carry_sc[...] += jnp.sum(lf, axis=0, keepdims=True)
        small = jnp.where(lane < HEADS, fcum, jnp.where(lane < 2 * HEADS, gval, jnp.where(lane < 3 * HEADS, beta, 0.0)))
        small_ref[...] = small
        tri_c, ones_c = _chunk_masks(tm)
        g_lanes = jnp.where((lane >= HEADS) & (lane < 2 * HEADS), gval, 0.0)
        sel_g = _lane_group_selector(HEADS)
        gc = jnp.dot(tri_c, g_lanes, preferred_element_type=f32, precision=HI)
        gcb_ref[...] = jnp.dot(gc, sel_g, preferred_element_type=f32, precision=HI)
        g_last = jnp.dot(ones_c, g_lanes, preferred_element_type=f32, precision=HI)
        glb_ref[...] = jnp.dot(g_last, sel_g, preferred_element_type=f32, precision=HI)
        bb_ref[...] = jnp.dot(small, _lane_group_selector(2 * HEADS), preferred_element_type=f32, precision=HI)

    def col(cb):
        return pl.BlockSpec((tm, WIDTH), lambda i: (i, cb))

    def halo(cb):
        return pl.BlockSpec((HALO, WIDTH), lambda i: (jnp.maximum(i * (tm // HALO) - 1, 0), cb))

    vec = pl.BlockSpec((1, LANES), lambda i: (0, 0))
    wide_f32 = jax.ShapeDtypeStruct((s_len, WIDTH), f32)
    wide_bf = jax.ShapeDtypeStruct((s_len, WIDTH), bf16)
    out_col = pl.BlockSpec((tm, WIDTH), lambda i: (i, 0))
    return pl.pallas_call(
        body, name="prep", grid=(nb,),
        in_specs=[col(0), col(1), col(2), col(4), col(5), col(6), halo(4), halo(5), halo(6),
                  pl.BlockSpec((tm, N_SMALL), lambda i: (i, 0)), vec, vec,
                  pl.BlockSpec((CONV_K, 3 * WIDTH), lambda i: (0, 0)), vec, vec],
        out_specs=[out_col] * 6 + [pl.BlockSpec((tm, N_SMALL), lambda i: (i, 0)), out_col, out_col, out_col],
        out_shape=[wide_bf, wide_bf, wide_bf, wide_f32, wide_f32, wide_f32,
                   jax.ShapeDtypeStruct((s_len, N_SMALL), f32), wide_f32, wide_f32, wide_f32],
        scratch_shapes=[pltpu.VMEM((tm + HALO, WIDTH), f32), pltpu.VMEM((1, N_SMALL), f32)],
        compiler_params=_params("arbitrary"),
    )(p_main, p_main, p_main, p_main, p_main, p_main, p_main, p_main, p_main, p_small, qn_g, kn_g, conv_w, bvec, alog)


FOX_T = 1024
NEG_BIG = -1e30


def _head_lane(block, head):
    return jnp.sum(jnp.where(_iota(block.shape, 1) == head, block, 0.0), axis=1, keepdims=True)


def _fox_fwd(qs, kn, vb, small, f_row):
    s_len = qs.shape[0]
    t = FOX_T
    nq = s_len // t

    def body(q_ref, k_ref, v_ref, sm_ref, fr_ref, o_ref, lse_ref):
        qi = pl.program_id(1)
        q = q_ref[...]
        fq = _head_lane(sm_ref[...], pl.program_id(0))
        causal = _iota((t, t), 0) >= _iota((t, t), 1)

        def step(j, carry, masked):
            m, l, acc = carry
            rows = pl.ds(pl.multiple_of(j * t, t), t)
            s = _dg(q, k_ref[rows, :], 1, 1) + (fq - fr_ref[0, j])
            if masked:
                s = jnp.where(causal, s, NEG_BIG)
            m_new = jnp.maximum(m, jnp.max(s, axis=-1, keepdims=True))
            p = jnp.exp(s - m_new)
            alpha = jnp.exp(m - m_new)
            l = alpha * l + jnp.sum(p, axis=-1, keepdims=True)
            acc = alpha * acc + jnp.dot(p.astype(bf16), v_ref[rows, :], preferred_element_type=f32)
            return m_new, l, acc

        init = (jnp.full((t, 1), NEG_BIG, f32), jnp.zeros((t, 1), f32), jnp.zeros((t, HEAD_DIM), f32))
        carry = lax.fori_loop(0, qi, lambda j, c: step(j, c, False), init)
        m, l, acc = step(qi, carry, True)
        o_ref[...] = acc / l
        lse_ref[0] = m + jnp.log(l)

    return pl.pallas_call(
        body, name="fox_fwd", grid=(HEADS, nq),
        in_specs=[pl.BlockSpec((t, HEAD_DIM), lambda h, i: (i, h)),
                  pl.BlockSpec((s_len, HEAD_DIM), lambda h, i: (0, h)),
                  pl.BlockSpec((s_len, HEAD_DIM), lambda h, i: (0, h)),
                  pl.BlockSpec((t, N_SMALL), lambda h, i: (i, 0)),
                  pl.BlockSpec((1, nq, 1, t), lambda h, i: (h, 0, 0, 0))],
        out_specs=[pl.BlockSpec((t, HEAD_DIM), lambda h, i: (i, h)),
                   pl.BlockSpec((1, t, 1), lambda h, i: (h, i, 0))],
        out_shape=[jax.ShapeDtypeStruct((s_len, WIDTH), f32), jax.ShapeDtypeStruct((HEADS, s_len, 1), f32)],
        compiler_params=_params("parallel", "arbitrary"),
    )(qs, kn, vb, small, f_row)


def _fox_bwd(qs, kn, vb, do, small, lse, delta, f_row):
    s_len = qs.shape[0]
    t = FOX_T
    nq = s_len // t

    def body(q_ref, do_ref, sm_ref, lse_ref, dl_ref, k_ref, v_ref, fr_ref, dq_ref, dk_ref, dv_ref, df_ref, dfq_ref):
        head, qi = pl.program_id(0), pl.program_id(1)

        @pl.when(qi == 0)
        def _():
            dk_ref[...] = jnp.zeros_like(dk_ref)
            dv_ref[...] = jnp.zeros_like(dv_ref)
            df_ref[...] = jnp.zeros_like(df_ref)

        q, do_b = q_ref[...], do_ref[...]
        a = _head_lane(sm_ref[...], head) - lse_ref[0]
        dl = _head_lane(dl_ref[...], head)
        causal = _iota((t, t), 0) >= _iota((t, t), 1)

        def step(j, carry, masked):
            dq, row_sum = carry
            rows = pl.ds(pl.multiple_of(j * t, t), t)
            kj, vj = k_ref[rows, :], v_ref[rows, :]
            p = jnp.exp(_dg(q, kj, 1, 1) + (a - fr_ref[0, j]))
            if masked:
                p = jnp.where(causal, p, 0.0)
            ds = p * (_dg(do_b, vj, 1, 1) - dl)
            ds_b = ds.astype(bf16)
            dk_ref[rows, :] += _dg(ds_b, q, 0, 0)
            dv_ref[rows, :] += _dg(p.astype(bf16), do_b, 0, 0)
            df_ref[0, j] += -jnp.sum(ds, axis=0, keepdims=True)
            return dq + jnp.dot(ds_b, kj, preferred_element_type=f32), row_sum + jnp.sum(ds, axis=-1, keepdims=True)

        carry = lax.fori_loop(0, qi, lambda j, c: step(j, c, False),
                              (jnp.zeros((t, HEAD_DIM), f32), jnp.zeros((t, 1), f32)))
        dq, row_sum = step(qi, carry, True)
        dq_ref[...] = dq
        dfq_ref[0] = row_sum

    blk = pl.BlockSpec((t, HEAD_DIM), lambda h, i: (i, h))
    full = pl.BlockSpec((s_len, HEAD_DIM), lambda h, i: (0, h))
    colv = pl.BlockSpec((1, t, 1), lambda h, i: (h, i, 0))
    rowv = pl.BlockSpec((1, nq, 1, t), lambda h, i: (h, 0, 0, 0))
    lanes = pl.BlockSpec((t, N_SMALL), lambda h, i: (i, 0))
    wide = jax.ShapeDtypeStruct((s_len, WIDTH), f32)
    return pl.pallas_call(
        body, name="fox_bwd", grid=(HEADS, nq),
        in_specs=[blk, blk, lanes, colv, lanes, full, full, rowv],
        out_specs=[blk, full, full, rowv, colv],
        out_shape=[wide, wide, wide, jax.ShapeDtypeStruct((HEADS, nq, 1, t), f32),
                   jax.ShapeDtypeStruct((HEADS, s_len, 1), f32)],
        compiler_params=_params("parallel", "arbitrary"),
    )(qs, do, small, lse, delta, kn, vb, f_row)


INTRA_CHUNKS = 8
INTRA_INTERLEAVE = 4
SCAN_FWD_CHUNKS = 8
SCAN_BWD_CHUNKS = 4


def _gdn_intra_fwd(gq, gk, gv, gc_b, g_last_b, beta_b):
    s_len = gq.shape[0]
    cpb = INTRA_CHUNKS
    rows_blk = cpb * CHUNK
    n_chunks = s_len // CHUNK

    def body(q_ref, k_ref, v_ref, gc_ref, gl_ref, b_ref, u_ref, w_ref, qg_ref, kd_ref, attn_ref, t_ref, eg_ref):
        ms, rhss = [], []
        for ci in range(cpb):
            rows = pl.ds(ci * CHUNK, CHUNK)
            m, rhs, qg, kd, attn, eg_last = _gdn_intra_pre(q_ref[rows, :], k_ref[rows, :], v_ref[rows, :],
                                                           gc_ref[rows, :], gl_ref[rows, :], b_ref[rows, :],
                                                           _BF_PLAIN[1])
            qg_ref[rows, :] = qg.astype(bf16)
            kd_ref[rows, :] = kd.astype(bf16)
            attn_ref[0, ci] = attn.astype(bf16)
            eg_ref[0, ci] = eg_last
            ms.append(m)
            rhss.append(rhs)
        for ci, (t, rhs) in enumerate(zip(_inv_unit_lower_many(ms), rhss)):
            rows = pl.ds(ci * CHUNK, CHUNK)
            t_ref[0, ci] = t
            uw = _X3_PLAIN[0](t, rhs)
            u_ref[rows, :] = uw[:, :HEAD_DIM]
            w_ref[rows, :] = uw[:, HEAD_DIM:].astype(bf16)

    blk = pl.BlockSpec((rows_blk, HEAD_DIM), lambda h, i: (i, h))
    sq = pl.BlockSpec((1, cpb, CHUNK, CHUNK), lambda h, i: (h, i, 0, 0))
    wide_bf = jax.ShapeDtypeStruct((s_len, WIDTH), bf16)
    return pl.pallas_call(
        body, name="gdn_intra_fwd", grid=(HEADS, s_len // rows_blk),
        in_specs=[blk] * 6,
        out_specs=[blk] * 4 + [sq, sq, pl.BlockSpec((1, cpb, SUBLANES, HEAD_DIM), lambda h, i: (h, i, 0, 0))],
        out_shape=[jax.ShapeDtypeStruct((s_len, WIDTH), f32), wide_bf, wide_bf, wide_bf,
                   jax.ShapeDtypeStruct((HEADS, n_chunks, CHUNK, CHUNK), bf16),
                   jax.ShapeDtypeStruct((HEADS, n_chunks, CHUNK, CHUNK), f32),
                   jax.ShapeDtypeStruct((HEADS, n_chunks, SUBLANES, HEAD_DIM), f32)],
        compiler_params=_params("parallel", "parallel"),
    )(gq, gk, gv, gc_b, g_last_b, beta_b)


def _gdn_scan_fwd(u, w, qg, kd, attn, eg):
    s_len = u.shape[0]
    cpb = SCAN_FWD_CHUNKS
    rows_blk = cpb * CHUNK
    n_chunks = s_len // CHUNK

    def body(u_ref, w_ref, qg_ref, kd_ref, attn_ref, eg_ref, o_ref, st_ref, s_sc):
        @pl.when(pl.program_id(0) == 0)
        def _():
            s_sc[...] = jnp.zeros_like(s_sc)

        def chunk(ci, _):
            rows = pl.ds(pl.multiple_of(ci * CHUNK, CHUNK), CHUNK)
            cols = [slice(h * HEAD_DIM, (h + 1) * HEAD_DIM) for h in range(HEADS)]
            s0 = [s_sc[h] for h in range(HEADS)]
            s0_b = [s.astype(bf16) for s in s0]
            for h in range(HEADS):
                st_ref[h, ci] = s0[h]
            ws = [jnp.dot(w_ref[rows, cols[h]], s0_b[h], preferred_element_type=f32) for h in range(HEADS)]
            qs = [jnp.dot(qg_ref[rows, cols[h]], s0_b[h], preferred_element_type=f32) for h in range(HEADS)]
            vn_b = [(u_ref[rows, cols[h]] - ws[h]).astype(bf16) for h in range(HEADS)]
            av = [jnp.dot(attn_ref[h, ci], vn_b[h], preferred_element_type=f32) for h in range(HEADS)]
            kv = [_dg(kd_ref[rows, cols[h]], vn_b[h], 0, 0) for h in range(HEADS)]
            for h in range(HEADS):
                o_ref[rows, cols[h]] = qs[h] + av[h]
                s_sc[h] = _scale_rows(s0[h], eg_ref[h, ci]) + kv[h]
            return 0

        lax.fori_loop(0, cpb, chunk, 0)

    row = pl.BlockSpec((rows_blk, WIDTH), lambda i: (i, 0))
    return pl.pallas_call(
        body, name="gdn_scan_fwd", grid=(s_len // rows_blk,),
        in_specs=[row] * 4 + [pl.BlockSpec((HEADS, cpb, CHUNK, CHUNK), lambda i: (0, i, 0, 0)),
                              pl.BlockSpec((HEADS, cpb, SUBLANES, HEAD_DIM), lambda i: (0, i, 0, 0))],
        out_specs=[row, pl.BlockSpec((HEADS, cpb, HEAD_DIM, HEAD_DIM), lambda i: (0, i, 0, 0))],
        out_shape=[jax.ShapeDtypeStruct((s_len, WIDTH), f32),
                   jax.ShapeDtypeStruct((HEADS, n_chunks, HEAD_DIM, HEAD_DIM), f32)],
        scratch_shapes=[pltpu.VMEM((HEADS, HEAD_DIM, HEAD_DIM), f32)],
        compiler_params=_params("arbitrary"),
    )(u, w, qg, kd, attn, eg)


def _gdn_scan_bwd(u, w, qg, kd, attn, eg, states, d_o):
    s_len = u.shape[0]
    cpb = SCAN_BWD_CHUNKS
    rows_blk = cpb * CHUNK
    n_chunks = s_len // CHUNK
    nb = s_len // rows_blk

    def body(u_ref, w_ref, qg_ref, kd_ref, attn_ref, eg_ref, st_ref, do_ref,
             du_ref, dw_ref, dqg_ref, dkd_ref, dattn_ref, deg_ref, ds_sc):
        @pl.when(pl.program_id(0) == 0)
        def _():
            ds_sc[...] = jnp.zeros_like(ds_sc)

        def chunk(step, _):
            ci = cpb - 1 - step
            rows = pl.ds(pl.multiple_of(ci * CHUNK, CHUNK), CHUNK)
            hs = range(HEADS)
            cols = [slice(h * HEAD_DIM, (h + 1) * HEAD_DIM) for h in hs]
            s0 = [st_ref[h, ci] for h in hs]
            s0_b = [s.astype(bf16) for s in s0]
            ds1 = [ds_sc[h] for h in hs]
            ds1_b = [d.astype(bf16) for d in ds1]
            do_b = [do_ref[rows, cols[h]].astype(bf16) for h in hs]
            ws = [jnp.dot(w_ref[rows, cols[h]], s0_b[h], preferred_element_type=f32) for h in hs]
            ad = [_dg(attn_ref[h, ci], do_b[h], 0, 0) for h in hs]
            kd_ds = [jnp.dot(kd_ref[rows, cols[h]], ds1_b[h], preferred_element_type=f32) for h in hs]
            dqg = [_dg(do_b[h], s0_b[h], 1, 1) for h in hs]
            qd = [_dg(qg_ref[rows, cols[h]], do_b[h], 0, 0) for h in hs]
            vn_b = [(u_ref[rows, cols[h]] - ws[h]).astype(bf16) for h in hs]
            dvn = [ad[h] + kd_ds[h] for h in hs]
            dvn_b = [d.astype(bf16) for d in dvn]
            dattn = [_dg(do_b[h], vn_b[h], 1, 1) for h in hs]
            dkd = [_dg(vn_b[h], ds1_b[h], 1, 1) for h in hs]
            dw = [_dg(dvn_b[h], s0_b[h], 1, 1) for h in hs]
            wd = [_dg(w_ref[rows, cols[h]], dvn_b[h], 0, 0) for h in hs]
            for h in hs:
                dattn_ref[h, ci] = dattn[h]
                dqg_ref[rows, cols[h]] = dqg[h]
                dkd_ref[rows, cols[h]] = dkd[h]
                du_ref[rows, cols[h]] = dvn[h]
                dw_ref[rows, cols[h]] = -dw[h]
                ds_sc[h] = qd[h] - wd[h] + _scale_rows(ds1[h], eg_ref[h, ci])
                deg_ref[h, ci] = jnp.sum((ds1[h] * s0[h]).reshape(HEAD_DIM // SUBLANES, SUBLANES, HEAD_DIM), axis=0)
            return 0

        lax.fori_loop(0, cpb, chunk, 0)

    row = pl.BlockSpec((rows_blk, WIDTH), lambda i: (nb - 1 - i, 0))
    sq = pl.BlockSpec((HEADS, cpb, CHUNK, CHUNK), lambda i: (0, nb - 1 - i, 0, 0))
    egs = pl.BlockSpec((HEADS, cpb, SUBLANES, HEAD_DIM), lambda i: (0, nb - 1 - i, 0, 0))
    wide = jax.ShapeDtypeStruct((s_len, WIDTH), f32)
    return pl.pallas_call(
        body, name="gdn_scan_bwd", grid=(nb,),
        in_specs=[row] * 4 + [sq, egs, pl.BlockSpec((HEADS, cpb, HEAD_DIM, HEAD_DIM), lambda i: (0, nb - 1 - i, 0, 0)), row],
        out_specs=[row] * 4 + [sq, egs],
        out_shape=[wide] * 4 + [jax.ShapeDtypeStruct((HEADS, n_chunks, CHUNK, CHUNK), f32),
                                jax.ShapeDtypeStruct((HEADS, n_chunks, SUBLANES, HEAD_DIM), f32)],
        scratch_shapes=[pltpu.VMEM((HEADS, HEAD_DIM, HEAD_DIM), f32)],
        compiler_params=_params("arbitrary"),
    )(u, w, qg, kd, attn, eg, states, d_o)


def _gdn_intra_bwd(gq, gk, gv, gc_b, g_last_b, beta_b, t_inv, du, dw, dqg, dkd, dattn, deg):
    s_len = gq.shape[0]
    cpb = INTRA_CHUNKS
    rows_blk = cpb * CHUNK

    def body(q_ref, k_ref, v_ref, gc_ref, gl_ref, b_ref, t_ref, du_ref, dw_ref, dqg_ref, dkd_ref, dattn_ref, deg_ref,
             dq_ref, dk_ref, dv_ref, dgc_ref, dgl_ref, db_ref):
        def batch(ref):
            return ref[...].reshape(cpb, CHUNK, HEAD_DIM)

        t_known = t_ref[0]
        _, vjp = jax.vjp(lambda q, k, v, gc, gl, b: _gdn_intra(q, k, v, gc, gl, b, t_known),
                         batch(q_ref), batch(k_ref), batch(v_ref), batch(gc_ref), batch(gl_ref), batch(b_ref))
        duw = jnp.concatenate([batch(du_ref), batch(dw_ref)], axis=-1)
        grads = vjp((duw, batch(dqg_ref), batch(dkd_ref), dattn_ref[0], deg_ref[0]))
        for ref, grad in zip((dq_ref, dk_ref, dv_ref, dgc_ref, dgl_ref, db_ref), grads):
            ref[...] = grad.reshape(rows_blk, HEAD_DIM)

    blk = pl.BlockSpec((rows_blk, HEAD_DIM), lambda h, i: (i, h))
    sq = pl.BlockSpec((1, cpb, CHUNK, CHUNK), lambda h, i: (h, i, 0, 0))
    egs = pl.BlockSpec((1, cpb, SUBLANES, HEAD_DIM), lambda h, i: (h, i, 0, 0))
    wide = jax.ShapeDtypeStruct((s_len, WIDTH), f32)
    return pl.pallas_call(
        body, name="gdn_intra_bwd", grid=(HEADS, s_len // rows_blk),
        in_specs=[blk] * 6 + [sq] + [blk] * 4 + [sq, egs],
        out_specs=[blk] * 6,
        out_shape=[wide] * 6,
        compiler_params=_params("parallel", "parallel"),
    )(gq, gk, gv, gc_b, g_last_b, beta_b, t_inv, du, dw, dqg, dkd, dattn, deg)


MIX_TM = 256


def _mix_fwd(fox_o, gdn_o, p_main, gnorm_g):
    s_len = fox_o.shape[0]
    tm = MIX_TM

    def body(fo_ref, go_ref, fz_ref, gz_ref, g_ref, mixed_ref):
        fz = fz_ref[...]
        mixed_ref[:, 0:WIDTH] = (fo_ref[...] * (fz * _sigmoid(fz))).astype(bf16)
        gz = gz_ref[...]
        gate = gz * _sigmoid(gz)
        gg = g_ref[...]
        for h in range(HEADS):
            sl = slice(h * HEAD_DIM, (h + 1) * HEAD_DIM)
            o = go_ref[:, sl]
            r = lax.rsqrt(jnp.mean(o * o, axis=-1, keepdims=True) + EPS)
            mixed_ref[:, WIDTH + h * HEAD_DIM:WIDTH + (h + 1) * HEAD_DIM] = (o * r * gg * gate[:, sl]).astype(bf16)

    row = pl.BlockSpec((tm, WIDTH), lambda i: (i, 0))
    return pl.pallas_call(
        body, name="mix_fwd", grid=(s_len // tm,),
        in_specs=[row, row, pl.BlockSpec((tm, WIDTH), lambda i: (i, 3)), pl.BlockSpec((tm, WIDTH), lambda i: (i, 7)),
                  pl.BlockSpec((1, LANES), lambda i: (0, 0))],
        out_specs=pl.BlockSpec((tm, 2 * WIDTH), lambda i: (i, 0)),
        out_shape=jax.ShapeDtypeStruct((s_len, 2 * WIDTH), bf16),
        compiler_params=_params("parallel"),
    )(fox_o, gdn_o, p_main, p_main, gnorm_g)


def _silu_grad(z):
    sg = _sigmoid(z)
    return sg * (1.0 + z * (1.0 - sg))


def _mix_bwd(dmixed, fox_o, gdn_o, p_main, gnorm_g):
    s_len = fox_o.shape[0]
    tm = MIX_TM

    def body(dm_ref, fo_ref, go_ref, fz_ref, gz_ref, g_ref, dof_ref, delta_ref, dfz_ref, dgz_ref, dgo_ref, dg_ref):
        @pl.when(pl.program_id(0) == 0)
        def _():
            dg_ref[...] = jnp.zeros_like(dg_ref)

        lane = _iota((tm, LANES), 1)
        fz = fz_ref[...]
        dmf = dm_ref[:, 0:WIDTH]
        fo = fo_ref[...]
        dof = dmf * (fz * _sigmoid(fz))
        dof_ref[...] = dof.astype(bf16)
        dfz_ref[...] = (dmf * fo * _silu_grad(fz)).astype(bf16)
        prod = dof * fo
        delta = jnp.zeros((tm, LANES), f32)
        for h in range(HEADS):
            dh = jnp.sum(prod[:, h * HEAD_DIM:(h + 1) * HEAD_DIM], axis=-1, keepdims=True)
            delta = jnp.where(lane == h, dh, delta)
        delta_ref[...] = delta

        gz = gz_ref[...]
        dmg = dm_ref[:, WIDTH:2 * WIDTH]
        gate = gz * _sigmoid(gz)
        sgrad = _silu_grad(gz)
        gg = g_ref[...]
        dg_acc = jnp.zeros((1, HEAD_DIM), f32)
        for h in range(HEADS):
            sl = slice(h * HEAD_DIM, (h + 1) * HEAD_DIM)
            o = go_ref[:, sl]
            r = lax.rsqrt(jnp.mean(o * o, axis=-1, keepdims=True) + EPS)
            on = o * r
            dmh = dmg[:, sl]
            dgz_ref[:, sl] = (dmh * (on * gg) * sgrad[:, sl]).astype(bf16)
            dy = dmh * gate[:, sl]
            dg_acc = dg_acc + jnp.sum(dy * on, axis=0, keepdims=True)
            tt = dy * gg
            dgo_ref[:, sl] = r * (tt - on * jnp.mean(tt * on, axis=-1, keepdims=True))
        dg_ref[...] += dg_acc

    row = pl.BlockSpec((tm, WIDTH), lambda i: (i, 0))
    wide_bf = jax.ShapeDtypeStruct((s_len, WIDTH), bf16)
    return pl.pallas_call(
        body, name="mix_bwd", grid=(s_len // tm,),
        in_specs=[pl.BlockSpec((tm, 2 * WIDTH), lambda i: (i, 0)), row, row,
                  pl.BlockSpec((tm, WIDTH), lambda i: (i, 3)), pl.BlockSpec((tm, WIDTH), lambda i: (i, 7)),
                  pl.BlockSpec((1, LANES), lambda i: (0, 0))],
        out_specs=[row, pl.BlockSpec((tm, LANES), lambda i: (i, 0)), row, row, row,
                   pl.BlockSpec((1, LANES), lambda i: (0, 0))],
        out_shape=[wide_bf, jax.ShapeDtypeStruct((s_len, LANES), f32), wide_bf, wide_bf,
                   jax.ShapeDtypeStruct((s_len, WIDTH), f32), jax.ShapeDtypeStruct((1, LANES), f32)],
        compiler_params=_params("arbitrary"),
    )(dmixed, fox_o, gdn_o, p_main, p_main, gnorm_g)


def _out_head(mixed, w_out, x, target, gate, final_g):
    s_len = x.shape[0]
    tm = 256

    def body(mx_ref, w_ref, x_ref, t_ref, gate_ref, fg_ref, loss_ref, dy_ref, dz_ref, dm_ref, dfg_ref, dgate_ref):
        @pl.when(pl.program_id(0) == 0)
        def _():
            loss_ref[...] = jnp.zeros_like(loss_ref)
            dfg_ref[...] = jnp.zeros_like(dfg_ref)
            dgate_ref[...] = jnp.zeros_like(dgate_ref)

        w = w_ref[...]
        z = jnp.dot(mx_ref[...], w, preferred_element_type=f32)
        gate_v, fg = gate_ref[...], fg_ref[...]
        y1 = x_ref[...] + gate_v * z
        r = lax.rsqrt(jnp.mean(y1 * y1, axis=-1, keepdims=True) + EPS)
        yn = y1 * r
        err = yn * fg - t_ref[...]
        loss_ref[...] += 0.5 * jnp.sum(jnp.mean(err * err, axis=-1, keepdims=True))
        dout = err * (1.0 / D_MODEL)
        dfg_ref[...] += jnp.sum(dout * yn, axis=0, keepdims=True)
        tt = dout * fg
        dy1 = r * (tt - yn * jnp.mean(tt * yn, axis=-1, keepdims=True))
        dy_ref[...] = dy1
        dgate_ref[...] += jnp.sum(dy1 * z, axis=0, keepdims=True)
        dz = (dy1 * gate_v).astype(bf16)
        dz_ref[...] = dz
        dm_ref[...] = _dg(dz, w, 1, 1)

    row = pl.BlockSpec((tm, D_MODEL), lambda i: (i, 0))
    vec = pl.BlockSpec((1, D_MODEL), lambda i: (0, 0))
    big = jax.ShapeDtypeStruct((s_len, D_MODEL), f32)
    return pl.pallas_call(
        body, name="out_head", grid=(s_len // tm,),
        in_specs=[row, pl.BlockSpec((D_MODEL, D_MODEL), lambda i: (0, 0)), row, row, vec, vec],
        out_specs=[pl.BlockSpec((1, LANES), lambda i: (0, 0)), row, row, row, vec, vec],
        out_shape=[jax.ShapeDtypeStruct((1, LANES), f32), big, jax.ShapeDtypeStruct((s_len, D_MODEL), bf16), big,
                   jax.ShapeDtypeStruct((1, D_MODEL), f32), jax.ShapeDtypeStruct((1, D_MODEL), f32)],
        compiler_params=_params("arbitrary"),
    )(mixed, w_out, x, target, gate, final_g)


def _matmul_tn(name, a, b, out_dtype):
    k_len, m_len = a.shape
    n_len = b.shape[1]
    tk, tm, tn = 512, min(1024, m_len), min(1024, n_len)
    nk = k_len // tk

    def body(a_ref, b_ref, o_ref, acc_sc):
        k = pl.program_id(2)

        @pl.when(k == 0)
        def _():
            acc_sc[...] = jnp.zeros_like(acc_sc)

        acc_sc[...] += _dg(a_ref[...], b_ref[...], 0, 0)

        @pl.when(k == nk - 1)
        def _():
            o_ref[...] = acc_sc[...].astype(out_dtype)

    return pl.pallas_call(
        body, name=name, grid=(m_len // tm, n_len // tn, nk),
        in_specs=[pl.BlockSpec((tk, tm), lambda i, j, k: (k, i)), pl.BlockSpec((tk, tn), lambda i, j, k: (k, j))],
        out_specs=pl.BlockSpec((tm, tn), lambda i, j, k: (i, j)),
        out_shape=jax.ShapeDtypeStruct((m_len, n_len), out_dtype),
        scratch_shapes=[pltpu.VMEM((tm, tn), f32)],
        compiler_params=_params("parallel", "parallel", "arbitrary"),
    )(a, b)


def _post1(p_main, p_small, qn_g, kn_g, conv_w, bvec, alog, dqs, dkn, dgq, dgk, dgv, dgc_b, dgl_b, dbeta_b, df,
           df_query):
    s_len = p_main.shape[0]
    tm = PREP_TM
    nb = s_len // tm

    def body(fq_ref, fk_ref, gq_ref, gk_ref, gv_ref, hq_ref, hk_ref, hv_ref, ps_ref, qg_ref, kg_ref, cw_ref, bv_ref,
             al_ref, dqs_ref, dkn_ref, dgq_ref, dgk_ref, dgv_ref, dgcb_ref, dglb_ref, dbb_ref, df_ref, dfq_in_ref,
             dfq_ref, dfk_ref, dconv_ref, dps_ref, dqg_ref, dkg_ref, sums_ref, xe_sc, carry_sc):
        step = pl.program_id(0)
        blk = nb - 1 - step

        @pl.when(step == 0)
        def _():
            carry_sc[...] = jnp.zeros_like(carry_sc)
            dqg_ref[...] = jnp.zeros_like(dqg_ref)
            dkg_ref[...] = jnp.zeros_like(dkg_ref)
            sums_ref[...] = jnp.zeros_like(sums_ref)

        for x_ref, g_ref, dy_ref, o_ref, acc_ref, mul in ((fq_ref, qg_ref, dqs_ref, dfq_ref, dqg_ref, QK_SCALE),
                                                          (fk_ref, kg_ref, dkn_ref, dfk_ref, dkg_ref, 1.0)):
            gain = g_ref[...]
            acc = jnp.zeros((1, HEAD_DIM), f32)
            for h in range(HEADS):
                sl = slice(h * HEAD_DIM, (h + 1) * HEAD_DIM)
                xv = x_ref[:, sl]
                r = lax.rsqrt(jnp.mean(xv * xv, axis=-1, keepdims=True) + EPS)
                xn = xv * r
                dy = dy_ref[:, sl] * mul
                acc = acc + jnp.sum(dy * xn, axis=0, keepdims=True)
                tt = dy * gain
                o_ref[:, sl] = (r * (tt - xn * jnp.mean(tt * xn, axis=-1, keepdims=True))).astype(bf16)
            acc_ref[...] += acc

        first = blk == 0
        for sec, (x_ref, halo_ref, dy_ref) in enumerate(((gq_ref, hq_ref, dgq_ref), (gk_ref, hk_ref, dgk_ref),
                                                         (gv_ref, hv_ref, dgv_ref))):
            xe_sc[0:HALO, :] = jnp.where(first, 0.0, halo_ref[...])
            xe_sc[HALO:, :] = x_ref[...]
            cv = _conv_section(xe_sc, cw_ref, slice(sec * WIDTH, (sec + 1) * WIDTH), tm)
            sgrad = _silu_grad(cv)
            if sec == 2:
                dconv_ref[:, sec * WIDTH:(sec + 1) * WIDTH] = dy_ref[...] * sgrad
            else:
                y = cv * _sigmoid(cv)
                mul = QK_SCALE if sec == 0 else 1.0
                for h in range(HEADS):
                    sl = slice(h * HEAD_DIM, (h + 1) * HEAD_DIM)
                    yh = y[:, sl]
                    r = lax.rsqrt(jnp.sum(yh * yh, axis=-1, keepdims=True) + EPS)
                    dqh = dy_ref[:, sl]
                    dyh = (mul * r) * (dqh - yh * (r * r) * jnp.sum(dqh * yh, axis=-1, keepdims=True))
                    dconv_ref[:, sec * WIDTH + h * HEAD_DIM:sec * WIDTH + (h + 1) * HEAD_DIM] = dyh * sgrad[:, sl]

        lane = _iota((tm, N_SMALL), 1)
        z, _, gval, beta = _small_fwd(ps_ref[...], bv_ref[...], al_ref[...])
        sig_z = _sigmoid(z)
        sel_t = (_iota((WIDTH, LANES), 1) == HEADS + _iota((WIDTH, LANES), 0) // HEAD_DIM).astype(f32)
        dgc = jnp.dot(dgcb_ref[...], sel_t, preferred_element_type=f32, precision=HI)
        dgl = jnp.dot(dglb_ref[...], sel_t, preferred_element_type=f32, precision=HI)
        tri_c, ones_c = _chunk_masks(tm)
        dg = (_dg(tri_c, dgc, 0, 0, HI) + jnp.dot(ones_c, dgl, preferred_element_type=f32, precision=HI))
        sel_t2 = (_iota((WIDTH, LANES), 1) == 2 * HEADS + _iota((WIDTH, LANES), 0) // HEAD_DIM).astype(f32)
        dbeta = jnp.dot(dbb_ref[...], sel_t2, preferred_element_type=f32, precision=HI)
        dfb = jnp.where(lane < HEADS, df_ref[...], 0.0)
        for h in range(HEADS):
            dfb = dfb + jnp.where(lane == h, dfq_in_ref[h], 0.0)
        tri_u = (_iota((tm, tm), 1) >= _iota((tm, tm), 0)).astype(f32)
        dlogf = jnp.dot(tri_u, dfb, preferred_element_type=f32, precision=HI) + carry_sc[...]
        carry_sc[...] += jnp.sum(dfb, axis=0, keepdims=True)
        dff = dlogf * (1.0 - sig_z)
        dga = dg * (-jnp.exp(al_ref[...])) * sig_z
        dgb_small = dbeta * beta * (1.0 - beta)
        dps = jnp.where(lane < HEADS, dff, jnp.where(lane < 2 * HEADS, dga, jnp.where(lane < 3 * HEADS, dgb_small, 0.0)))
        dps_ref[...] = dps.astype(bf16)
        row = _iota((8, N_SMALL), 0)
        s0 = jnp.sum(dps, axis=0, keepdims=True)
        s1 = jnp.sum(jnp.where((lane >= HEADS) & (lane < 2 * HEADS), dg * gval, 0.0), axis=0, keepdims=True)
        sums_ref[...] += jnp.where(row == 0, s0, jnp.where(row == 1, s1, 0.0))

    def col(cb):
        return pl.BlockSpec((tm, WIDTH), lambda i: (nb - 1 - i, cb))

    def halo(cb):
        return pl.BlockSpec((HALO, WIDTH), lambda i: (jnp.maximum((nb - 1 - i) * (tm // HALO) - 1, 0), cb))

    vec = pl.BlockSpec((1, LANES), lambda i: (0, 0))
    row0 = pl.BlockSpec((tm, WIDTH), lambda i: (nb - 1 - i, 0))
    small = pl.BlockSpec((tm, N_SMALL), lambda i: (nb - 1 - i, 0))
    wide_bf = jax.ShapeDtypeStruct((s_len, WIDTH), bf16)
    return pl.pallas_call(
        body, name="post1", grid=(nb,),
        in_specs=[col(0), col(1), col(4), col(5), col(6), halo(4), halo(5), halo(6), small, vec, vec,
                  pl.BlockSpec((CONV_K, 3 * WIDTH), lambda i: (0, 0)), vec, vec,
                  row0, row0, row0, row0, row0, row0, row0, row0, small,
                  pl.BlockSpec((HEADS, tm, 1), lambda i: (0, nb - 1 - i, 0))],
        out_specs=[row0, row0, pl.BlockSpec((tm, 3 * WIDTH), lambda i: (nb - 1 - i, 0)), small, vec, vec,
                   pl.BlockSpec((8, N_SMALL), lambda i: (0, 0))],
        out_shape=[wide_bf, wide_bf, jax.ShapeDtypeStruct((s_len, 3 * WIDTH), f32),
                   jax.ShapeDtypeStruct((s_len, N_SMALL), bf16), jax.ShapeDtypeStruct((1, LANES), f32),
                   jax.ShapeDtypeStruct((1, LANES), f32), jax.ShapeDtypeStruct((8, N_SMALL), f32)],
        scratch_shapes=[pltpu.VMEM((tm + HALO, WIDTH), f32), pltpu.VMEM((1, N_SMALL), f32)],
        compiler_params=_params("arbitrary"),
    )(p_main, p_main, p_main, p_main, p_main, p_main, p_main, p_main, p_small, qn_g, kn_g, conv_w, bvec, alog,
      dqs, dkn, dgq, dgk, dgv, dgc_b, dgl_b, dbeta_b, df, df_query)


def _post2(p_main, dconv, conv_w):
    s_len = p_main.shape[0]
    tm = PREP_TM
    nb = s_len // tm

    def body(gq_ref, gk_ref, gv_ref, hq_ref, hk_ref, hv_ref, dc_ref, dnext_ref, cw_ref, dx_ref, dw_ref, xe_sc, de_sc):
        i = pl.program_id(0)

        @pl.when(i == 0)
        def _():
            dw_ref[...] = jnp.zeros_like(dw_ref)

        first, last = i == 0, i == nb - 1
        row = _iota((8, WIDTH), 0)
        for sec, (x_ref, halo_ref) in enumerate(((gq_ref, hq_ref), (gk_ref, hk_ref), (gv_ref, hv_ref))):
            cols = slice(sec * WIDTH, (sec + 1) * WIDTH)
            dc = dc_ref[:, cols]
            de_sc[0:tm, :] = dc
            de_sc[tm:, :] = jnp.where(last, 0.0, dnext_ref[:, cols])
            dx = cw_ref[pl.ds(CONV_K - 1, 1), cols] * dc
            for tap in range(CONV_K - 1):
                dx = dx + cw_ref[pl.ds(tap, 1), cols] * de_sc[pl.ds(CONV_K - 1 - tap, tm), :]
            dx_ref[:, cols] = dx.astype(bf16)
            xe_sc[0:HALO, :] = jnp.where(first, 0.0, halo_ref[...])
            xe_sc[HALO:, :] = x_ref[...]
            dw = jnp.zeros((8, WIDTH), f32)
            for tap in range(CONV_K):
                contrib = jnp.sum(dc * xe_sc[pl.ds(HALO - (CONV_K - 1) + tap, tm), :], axis=0, keepdims=True)
                dw = jnp.where(row == tap, contrib, dw)
            dw_ref[:, cols] += dw

    def col(cb):
        return pl.BlockSpec((tm, WIDTH), lambda i: (i, cb))

    def halo(cb):
        return pl.BlockSpec((HALO, WIDTH), lambda i: (jnp.maximum(i * (tm // HALO) - 1, 0), cb))

    return pl.pallas_call(
        body, name="post2", grid=(nb,),
        in_specs=[col(4), col(5), col(6), halo(4), halo(5), halo(6),
                  pl.BlockSpec((tm, 3 * WIDTH), lambda i: (i, 0)),
                  pl.BlockSpec((HALO, 3 * WIDTH), lambda i: (jnp.minimum((i + 1) * (tm // HALO), s_len // HALO - 1), 0)),
                  pl.BlockSpec((CONV_K, 3 * WIDTH), lambda i: (0, 0))],
        out_specs=[pl.BlockSpec((tm, 3 * WIDTH), lambda i: (i, 0)), pl.BlockSpec((8, 3 * WIDTH), lambda i: (0, 0))],
        out_shape=[jax.ShapeDtypeStruct((s_len, 3 * WIDTH), bf16), jax.ShapeDtypeStruct((8, 3 * WIDTH), f32)],
        scratch_shapes=[pltpu.VMEM((tm + HALO, WIDTH), f32), pltpu.VMEM((tm + HALO, WIDTH), f32)],
        compiler_params=_params("arbitrary"),
    )(p_main, p_main, p_main, p_main, p_main, p_main, dconv, dconv, conv_w)


def _in_proj_bwd(dp_main, dp_small, wt_main, wt_small, x, dy1, norm_g, scale1p):
    s_len = x.shape[0]
    tm, tk = 512, 1024
    nk = N_MAIN // tk

    def body(dp_ref, dps_ref, w_ref, ws_ref, x_ref, dy_ref, g_ref, sc_ref, dx_ref, dsh_ref, dsc_ref, dg_ref, acc_sc):
        i, k = pl.program_id(0), pl.program_id(1)

        @pl.when((i == 0) & (k == 0))
        def _():
            dsh_ref[...] = jnp.zeros_like(dsh_ref)
            dsc_ref[...] = jnp.zeros_like(dsc_ref)
            dg_ref[...] = jnp.zeros_like(dg_ref)

        @pl.when(k == 0)
        def _():
            acc_sc[...] = jnp.dot(dps_ref[...], ws_ref[...], preferred_element_type=f32)

        acc_sc[...] += jnp.dot(dp_ref[...], w_ref[...], preferred_element_type=f32)

        @pl.when(k == nk - 1)
        def _():
            dh = acc_sc[...]
            xb = x_ref[...]
            r = lax.rsqrt(jnp.mean(xb * xb, axis=-1, keepdims=True) + EPS)
            xr = xb * r
            gain = g_ref[...]
            dsh_ref[...] += jnp.sum(dh, axis=0, keepdims=True)
            dsc_ref[...] += jnp.sum(dh * (xr * gain), axis=0, keepdims=True)
            dxn = dh * sc_ref[...]
            dg_ref[...] += jnp.sum(dxn * xr, axis=0, keepdims=True)
            tt = dxn * gain
            dx_ref[...] = r * (tt - xr * jnp.mean(tt * xr, axis=-1, keepdims=True)) + dy_ref[...]

    row = pl.BlockSpec((tm, D_MODEL), lambda i, k: (i, 0))
    vec = pl.BlockSpec((1, D_MODEL), lambda i, k: (0, 0))
    vshape = jax.ShapeDtypeStruct((1, D_MODEL), f32)
    return pl.pallas_call(
        body, name="in_proj_bwd", grid=(s_len // tm, nk),
        in_specs=[pl.BlockSpec((tm, tk), lambda i, k: (i, k)), pl.BlockSpec((tm, N_SMALL), lambda i, k: (i, 0)),
                  pl.BlockSpec((tk, D_MODEL), lambda i, k: (k, 0)), pl.BlockSpec((N_SMALL, D_MODEL), lambda i, k: (0, 0)),
                  row, row, vec, vec],
        out_specs=[row, vec, vec, vec],
        out_shape=[jax.ShapeDtypeStruct((s_len, D_MODEL), f32), vshape, vshape, vshape],
        scratch_shapes=[pltpu.VMEM((tm, D_MODEL), f32)],
        compiler_params=_params("arbitrary", "arbitrary"),
    )(dp_main, dp_small, wt_main, wt_small, x, dy1, norm_g, scale1p)


def _adamw(name, w, g_stack, m, v, tr, tc=None):
    n_stack, rows, cols = g_stack.shape
    tc = cols if tc is None else tc

    def body(w_ref, g_ref, m_ref, v_ref, go_ref, d_ref, mo_ref, vo_ref):
        g = g_ref[0].astype(f32)
        for k in range(1, n_stack):
            g = g + g_ref[k].astype(f32)
        go_ref[0] = g
        m_new = ADAM_B1 * m_ref[0] + (1.0 - ADAM_B1) * g
        v_new = ADAM_B2 * v_ref[0] + (1.0 - ADAM_B2) * (g * g)
        mo_ref[0] = m_new
        vo_ref[0] = v_new
        m_hat = m_new / (1.0 - ADAM_B1 ** ADAM_STEP)
        v_hat = v_new / (1.0 - ADAM_B2 ** ADAM_STEP)
        d_ref[0] = -ADAM_LR * (m_hat / (jnp.sqrt(v_hat) + ADAM_EPS) + ADAM_WD * w_ref[0])

    blk = pl.BlockSpec((1, tr, tc), lambda i, j: (0, i, j))
    shape = jax.ShapeDtypeStruct((1, rows, cols), f32)
    return pl.pallas_call(
        body, name=name, grid=(rows // tr, cols // tc),
        in_specs=[blk, pl.BlockSpec((n_stack, tr, tc), lambda i, j: (0, i, j)), blk, blk],
        out_specs=[blk] * 4, out_shape=[shape] * 4,
        compiler_params=_params("parallel", "parallel"),
    )(w, g_stack, m, v)


def _w_ada_grad(c_all_t, dmod_pad):
    def body(c_ref, d_ref, o_ref):
        cv = c_ref[...]
        o_ref[...] = jnp.dot(cv * _sigmoid(cv), d_ref[...], preferred_element_type=f32, precision=HI)

    return pl.pallas_call(body, name="w_ada_grad",
                          out_shape=jax.ShapeDtypeStruct((c_all_t.shape[0], dmod_pad.shape[1]), f32),
                          compiler_params=_params())(c_all_t, dmod_pad)


SMALL_NAMES = ("norm_g", "b_ada", "b_fgate", "fox_qn_g", "fox_kn_g", "gdn_A_log", "gdn_dt_bias", "gdn_norm_g", "final_g")
SMALL_SIZES = (D_MODEL, 3 * D_MODEL, HEADS, HEAD_DIM, HEAD_DIM, HEADS, HEADS, HEAD_DIM, D_MODEL)
SMALL_PACK = 10752


def _pack(vectors, total):
    flat = jnp.concatenate([t.reshape(-1) for t in vectors])
    return jnp.pad(flat, (0, total - flat.shape[0])).reshape(1, total)


def _lanes(*pieces):
    row = jnp.zeros((LANES,), f32)
    for off, vec in pieces:
        row = lax.dynamic_update_slice(row, vec.reshape(-1).astype(f32), (off,))
    return row.reshape(1, LANES)


def kernel(x, c, norm_g, w_ada, b_ada, w_in, b_fgate, fox_qn_g, fox_kn_g, gdn_conv_w, gdn_A_log, gdn_dt_bias, gdn_norm_g, w_out, final_g, loss_target, m_norm_g, m_w_ada, m_b_ada, m_w_in, m_b_fgate, m_fox_qn_g, m_fox_kn_g, m_gdn_conv_w, m_gdn_A_log, m_gdn_dt_bias, m_gdn_norm_g, m_w_out, m_final_g, v_norm_g, v_w_ada, v_b_ada, v_w_in, v_b_fgate, v_fox_qn_g, v_fox_kn_g, v_gdn_conv_w, v_gdn_A_log, v_gdn_dt_bias, v_gdn_norm_g, v_w_out, v_final_g):
    me = _my_index()
    s_len = x.shape[1]
    nq = s_len // FOX_T
    x2 = x.reshape(s_len, D_MODEL)
    tgt = loss_target.reshape(s_len, D_MODEL)
    ada_cols = w_ada.shape[2]
    in_cols = w_in.shape[2]
    conv_cols = gdn_conv_w.shape[2]

    (c_all,) = _exchange("gather_c", [c], scatter=False)
    c_all = c_all.reshape(N_DEV, D_MODEL)
    b_shard = lax.dynamic_slice(b_ada, (0, me * ada_cols), (1, ada_cols))
    mod_mine = _mod_shard(c_all, w_ada[0], b_shard)
    wt_shard = jnp.transpose(w_in[0])
    mod_all, wt_all, w_out_all, conv_all = _gather_two_level(
        "gather_weights", [mod_mine, wt_shard.astype(bf16), w_out[0].astype(bf16), gdn_conv_w[0]])
    mod = lax.dynamic_slice(mod_all, (0, me, 0), (N_DEV, 1, ada_cols)).reshape(1, 3 * D_MODEL)
    shift, scale, gate = mod[:, :D_MODEL], mod[:, D_MODEL:2 * D_MODEL], mod[:, 2 * D_MODEL:]
    scale1p = 1.0 + scale
    wt_full = wt_all.reshape(N_DEV * in_cols, D_MODEL)
    g0 = 4 * WIDTH + HEADS
    w_main = jnp.concatenate([wt_full[:4 * WIDTH], wt_full[g0:g0 + 4 * WIDTH]], axis=0)
    w_small = jnp.concatenate([wt_full[4 * WIDTH:g0], wt_full[g0 + 4 * WIDTH:],
                               jnp.zeros((N_SMALL - 3 * HEADS, D_MODEL), bf16)], axis=0)
    w_out_full = w_out_all.reshape(2 * WIDTH, D_MODEL)
    conv_full = jnp.transpose(conv_all, (1, 0, 2)).reshape(CONV_K, 3 * WIDTH)

    qn_g, kn_g, gn_g = fox_qn_g.reshape(1, LANES), fox_kn_g.reshape(1, LANES), gdn_norm_g.reshape(1, LANES)
    bvec = _lanes((0, b_fgate), (HEADS, gdn_dt_bias))
    alog = _lanes((HEADS, gdn_A_log))
    fg = final_g.reshape(1, D_MODEL)

    p_main, p_small, h_bf = _in_proj(x2, norm_g, scale1p, shift, w_main, w_small)
    qs, kn, vb, gq, gk, gv, small, gc_b, gl_b, beta_b = _prep(p_main, p_small, qn_g, kn_g, conv_full, bvec, alog)
    f_row = jnp.transpose(small[:, :HEADS]).reshape(HEADS, nq, 1, FOX_T)
    fox_o, lse = _fox_fwd(qs, kn, vb, small, f_row)
    gu, gw, gqg, gkd, gattn, t_inv, eg_last = _gdn_intra_fwd(gq, gk, gv, gc_b, gl_b, beta_b)
    gdn_o, states = _gdn_scan_fwd(gu, gw, gqg, gkd, gattn, eg_last)
    mixed = _mix_fwd(fox_o, gdn_o, p_main, gn_g)

    loss_row, dy1, dz, dmixed, d_final_g, d_gate = _out_head(mixed, w_out_full, x2, tgt, gate, fg)
    loss = lax.psum(loss_row[0, 0], AXES)
    dw_out = _matmul_tn("dw_out", mixed, dz, bf16)
    do_fox, delta, dfz, dgz, dgdn_o, d_gn_g = _mix_bwd(dmixed, fox_o, gdn_o, p_main, gn_g)
    dqs, dkn, dvf, df_key, df_query = _fox_bwd(qs, kn, vb, do_fox, small, lse, delta, f_row)
    du, dw, dqg, dkd, dattn, deg = _gdn_scan_bwd(gu, gw, gqg, gkd, gattn, eg_last, states, dgdn_o)
    dgq, dgk, dgv, dgc_b, dgl_b, dbeta_b = _gdn_intra_bwd(gq, gk, gv, gc_b, gl_b, beta_b, t_inv, du, dw, dqg, dkd,
                                                          dattn, deg)
    df_small = jnp.pad(jnp.transpose(df_key.reshape(HEADS, s_len)), ((0, 0), (0, N_SMALL - HEADS)))
    dfq, dfk, dconv, dp_small, d_qn_g, d_kn_g, sums = _post1(
        p_main, p_small, qn_g, kn_g, conv_full, bvec, alog, dqs, dkn, dgq, dgk, dgv, dgc_b, dgl_b, dbeta_b, df_small,
        df_query)
    dgqkv, d_conv = _post2(p_main, dconv, conv_full)
    dp_main = jnp.concatenate([dfq, dfk, dvf.astype(bf16), dfz, dgqkv, dgz], axis=1)
    grad_x, d_shift, d_scale, d_norm_g = _in_proj_bwd(dp_main, dp_small, w_main, w_small, x2, dy1, norm_g, scale1p)
    dw_main = _matmul_tn("dw_main", dp_main, h_bf, bf16)
    dw_small = _matmul_tn("dw_small", dp_small, h_bf, bf16)
    dw_in_full = jnp.concatenate([dw_main[:4 * WIDTH], dw_small[:HEADS], dw_main[4 * WIDTH:],
                                  dw_small[HEADS:3 * HEADS]], axis=0)
    dw_in_parts = dw_in_full.reshape(N_DEV, in_cols, D_MODEL)
    dw_out_parts = dw_out.reshape(N_DEV, w_out.shape[1], D_MODEL)

    dmod = jnp.concatenate([d_shift, d_scale, d_gate], axis=1)
    small_grads = _pack([d_norm_g, dmod, sums[0, :HEADS], d_qn_g, d_kn_g, sums[1, HEADS:2 * HEADS],
                         sums[0, HEADS:2 * HEADS], d_gn_g, d_final_g], SMALL_PACK)
    conv_grad = d_conv[:CONV_K]
    dw_in_recv, dw_out_recv = _exchange("scatter_grads", [dw_in_parts, dw_out_parts], scatter=True)
    small_all, conv_all_g = _exchange("gather_small_grads", [small_grads, conv_grad], scatter=False)

    outs = {}
    to_t = lambda t: jnp.transpose(t, (0, 2, 1))
    outs["w_in"] = tuple(to_t(t) for t in _adamw("adamw_w_in", to_t(w_in), dw_in_recv, to_t(m_w_in), to_t(v_w_in),
                                                  in_cols, 256))
    outs["w_out"] = _adamw("adamw_w_out", w_out, dw_out_recv, m_w_out, v_w_out, 128)
    conv_mine = lax.dynamic_slice(jnp.transpose(conv_all_g.reshape(N_DEV, CONV_K, N_DEV, conv_cols), (0, 2, 1, 3)),
                                  (0, me, 0, 0), (N_DEV, 1, CONV_K, conv_cols)).reshape(N_DEV, CONV_K, conv_cols)
    outs["gdn_conv_w"] = _adamw("adamw_conv", gdn_conv_w, conv_mine, m_gdn_conv_w, v_gdn_conv_w, CONV_K)
    small_all = small_all.reshape(N_DEV, 1, SMALL_PACK)
    dmod_all = small_all[:, 0, D_MODEL:D_MODEL + 3 * D_MODEL]
    dmod_mine = lax.dynamic_slice(dmod_all, (0, me * ada_cols), (N_DEV, ada_cols))
    c_all_t = jnp.pad(jnp.transpose(c_all), ((0, 0), (0, LANES - N_DEV)))
    g_w_ada = _w_ada_grad(c_all_t, jnp.pad(dmod_mine, ((0, LANES - N_DEV), (0, 0))))
    outs["w_ada"] = _adamw("adamw_w_ada", w_ada, g_w_ada[None], m_w_ada, v_w_ada, 256)
    given = dict(norm_g=(norm_g, m_norm_g, v_norm_g), b_ada=(b_ada, m_b_ada, v_b_ada), b_fgate=(b_fgate, m_b_fgate, v_b_fgate),
                 fox_qn_g=(fox_qn_g, m_fox_qn_g, v_fox_qn_g), fox_kn_g=(fox_kn_g, m_fox_kn_g, v_fox_kn_g),
                 gdn_A_log=(gdn_A_log, m_gdn_A_log, v_gdn_A_log), gdn_dt_bias=(gdn_dt_bias, m_gdn_dt_bias, v_gdn_dt_bias),
                 gdn_norm_g=(gdn_norm_g, m_gdn_norm_g, v_gdn_norm_g), final_g=(final_g, m_final_g, v_final_g))
    w_pack = _pack([given[n][0] for n in SMALL_NAMES], SMALL_PACK)
    m_pack = _pack([given[n][1] for n in SMALL_NAMES], SMALL_PACK)
    v_pack = _pack([given[n][2] for n in SMALL_NAMES], SMALL_PACK)
    packed = _adamw("adamw_small", w_pack[None], small_all, m_pack[None], v_pack[None], 1)
    off = 0
    for n, size in zip(SMALL_NAMES, SMALL_SIZES):
        outs[n] = tuple(t[0, 0, off:off + size].reshape(given[n][0].shape) for t in packed)
        off += size

    order = ("norm_g", "w_ada", "b_ada", "w_in", "b_fgate", "fox_qn_g", "fox_kn_g", "gdn_conv_w", "gdn_A_log",
             "gdn_dt_bias", "gdn_norm_g", "w_out", "final_g")
    result = [loss, grad_x.reshape(x.shape)]
    for part in range(4):
        result += [outs[n][part] for n in order]
    return tuple(result)
```

```python
import math

import jax
import jax.numpy as jnp
from jax import lax
from jax.experimental import pallas as pl
from jax.experimental.pallas import tpu as pltpu

f32 = jnp.float32
bf16 = jnp.bfloat16
HI = lax.Precision.HIGHEST

N_DEV = 8
AXES = ("x", "y", "c")
D_MODEL = 2048
HEADS = 8
HEAD_DIM = 128
WIDTH = HEADS * HEAD_DIM
CHUNK = 64
CONV_K = 4
EPS = 1e-6
QK_SCALE = HEAD_DIM ** -0.5
N_MAIN = 8 * WIDTH
N_SMALL = 128
LANE_F, LANE_G, LANE_BETA, LANE_GC, LANE_GLAST = 0, 8, 16, 24, 32
IN_WIDTH = 8 * WIDTH + 3 * HEADS
LANES = 128
VMEM_LIMIT = 56 * 1024 * 1024

ADAM_LR, ADAM_B1, ADAM_B2, ADAM_EPS, ADAM_WD, ADAM_STEP = 0.001, 0.9, 0.999, 1e-08, 0.01, 10


def _params(*sem):
    return pltpu.CompilerParams(dimension_semantics=sem, vmem_limit_bytes=VMEM_LIMIT)


def _iota(shape, dim):
    return lax.broadcasted_iota(jnp.int32, shape, dim)


def _sigmoid(z):
    return 1.0 / (1.0 + jnp.exp(-z))


def _softplus_parts(z):
    t = jnp.log(1.0 + jnp.exp(-jnp.abs(z)))
    return jnp.minimum(z, 0.0) - t, jnp.maximum(z, 0.0) + t


def _dg(a, b, ca, cb, prec=None):
    if a.ndim == 3:
        dims = (((ca + 1,), (cb + 1,)), ((0,), (0,)))
    else:
        dims = (((ca,), (cb,)), ((), ()))
    return lax.dot_general(a, b, dims, preferred_element_type=f32, precision=prec)


def _dot_bf16(a, b, ca, cb):
    return _dg(a.astype(bf16), b.astype(bf16), ca, cb)


def _split_bf16(a):
    hi = a.astype(bf16)
    return hi, (a - hi.astype(f32)).astype(bf16)


def _dot_3pass(a, b, ca, cb):
    a_hi, a_lo = _split_bf16(a)
    b_hi, b_lo = _split_bf16(b)
    return _dg(a_hi, b_hi, ca, cb) + (_dg(a_hi, b_lo, ca, cb) + _dg(a_lo, b_hi, ca, cb))


def _make_mm(dot):
    def nn_(a, b):
        return dot(a, b, 1, 0)

    def nt_(a, b):
        return dot(a, b, 1, 1)

    def tn_(a, b):
        return dot(a, b, 0, 0)

    @jax.custom_vjp
    def nn(a, b):
        return nn_(a, b)

    @jax.custom_vjp
    def nt(a, b):
        return nt_(a, b)

    @jax.custom_vjp
    def tn(a, b):
        return tn_(a, b)

    nn.defvjp(lambda a, b: (nn_(a, b), (a, b)), lambda r, g: (nt_(g, r[1]), tn_(r[0], g)))
    nt.defvjp(lambda a, b: (nt_(a, b), (a, b)), lambda r, g: (nn_(g, r[1]), tn_(g, r[0])))
    tn.defvjp(lambda a, b: (tn_(a, b), (a, b)), lambda r, g: (nt_(r[1], g), nn_(r[0], g)))
    return (nn_, nt_, tn_), (nn, nt, tn)


_BF_PLAIN, _BF_VJP = _make_mm(_dot_bf16)
_X3_PLAIN, _X3_VJP = _make_mm(_dot_3pass)


def _inv_unit_lower_many(ms):
    c = CHUNK
    nn = _X3_PLAIN[0]
    eye = (_iota((c, c), 0) == _iota((c, c), 1)).astype(f32)
    top = _iota((2 * c, c), 0) < c
    xs = [jnp.concatenate([eye - m, nn(m, m)], axis=0) for m in ms]
    for _ in range(int(math.log2(CHUNK)) - 2):
        xs = [jnp.where(top, x, 0.0) + nn(x, x[c:]) for x in xs]
    return [x[:c] + nn(x[:c], x[c:]) for x in xs]


@jax.custom_vjp
def _inv_given(m, t):
    return t


_inv_given.defvjp(lambda m, t: (t, t),
                  lambda t, g: (-_X3_PLAIN[1](_X3_PLAIN[2](t, g), t), jnp.zeros_like(t)))

SUBLANES = 8


def _gdn_intra_pre(q, k, v, gc_b, g_last_b, beta_b, bnt):
    c = CHUNK
    r_i, c_i = _iota((c, c), 0), _iota((c, c), 1)
    lower, strict = r_i >= c_i, r_i > c_i
    gc_i = gc_b[..., :c]
    gc_j = jnp.swapaxes(gc_i, -1, -2)
    decay = jnp.where(lower, jnp.exp(jnp.where(lower, gc_i - gc_j, 0.0)), 0.0)
    kb = k * beta_b
    both = bnt(jnp.concatenate([kb, q], axis=-2), k)
    m = jnp.where(strict, both[..., :c, :] * decay, 0.0)
    attn = jnp.where(lower, both[..., c:, :] * decay, 0.0)
    eg = jnp.exp(gc_b)
    rhs = jnp.concatenate([v * beta_b, kb * eg], axis=-1)
    k_dec = k * jnp.exp(g_last_b - gc_b)
    eg_last = jnp.exp(g_last_b[..., :SUBLANES, :])
    return m, rhs, q * eg, k_dec, attn, eg_last


def _gdn_intra(q, k, v, gc_b, g_last_b, beta_b, t_known):
    m, rhs, qg, k_dec, attn, eg_last = _gdn_intra_pre(q, k, v, gc_b, g_last_b, beta_b, _BF_VJP[1])
    return _X3_VJP[0](_inv_given(m, t_known), rhs), qg, k_dec, attn, eg_last


def _scale_rows(s, eg_last):
    return (s.reshape(HEAD_DIM // SUBLANES, SUBLANES, HEAD_DIM) * eg_last[None]).reshape(HEAD_DIM, HEAD_DIM)


def _my_index():
    return 4 * lax.axis_index("x") + 2 * lax.axis_index("y") + lax.axis_index("c")


def _peer(d):
    x, y, c = lax.axis_index("x"), lax.axis_index("y"), lax.axis_index("c")
    px, py, pc = (x + (d >> 2)) % 2, (y + ((d >> 1) & 1)) % 2, (c + (d & 1)) % 2
    return (px, py, pc), 4 * px + 2 * py + pc


def _exchange(name, arrays, scatter):
    n = len(arrays)

    def body(*refs):
        srcs, dsts = refs[:n], refs[n:2 * n]
        send_sems, recv_sems, local_sems = refs[2 * n:]
        me = _my_index()

        def remote(k, d):
            peer, pidx = _peer(d)
            src = srcs[k].at[pidx] if scatter else srcs[k]
            return pltpu.make_async_remote_copy(
                src_ref=src, dst_ref=dsts[k].at[me], send_sem=send_sems.at[k * 7 + d - 1],
                recv_sem=recv_sems.at[k * 7 + d - 1], device_id=peer, device_id_type=pl.DeviceIdType.MESH)

        def arrival(k, d):
            peer, pidx = _peer(d)
            src = srcs[k].at[pidx] if scatter else srcs[k]
            return pltpu.make_async_remote_copy(
                src_ref=src, dst_ref=dsts[k].at[pidx], send_sem=send_sems.at[k * 7 + d - 1],
                recv_sem=recv_sems.at[k * 7 + d - 1], device_id=peer, device_id_type=pl.DeviceIdType.MESH)

        local = [pltpu.make_async_copy(srcs[k].at[me] if scatter else srcs[k], dsts[k].at[me], local_sems.at[k])
                 for k in range(n)]
        sends = [remote(k, d) for k in range(n) for d in range(1, N_DEV)]
        for cp in local + sends:
            cp.start()
        for k in range(n):
            for d in range(1, N_DEV):
                arrival(k, d).wait_recv()
        for cp in sends:
            cp.wait_send()
        for cp in local:
            cp.wait()

    if scatter:
        out_shape = [jax.ShapeDtypeStruct(a.shape, a.dtype) for a in arrays]
    else:
        out_shape = [jax.ShapeDtypeStruct((N_DEV,) + a.shape, a.dtype) for a in arrays]
    any_spec = pl.BlockSpec(memory_space=pl.ANY)
    return pl.pallas_call(
        body, name=name, out_shape=out_shape, in_specs=[any_spec] * n, out_specs=[any_spec] * n,
        scratch_shapes=[pltpu.SemaphoreType.DMA((7 * n,)), pltpu.SemaphoreType.DMA((7 * n,)),
                        pltpu.SemaphoreType.DMA((n,))],
        compiler_params=pltpu.CompilerParams(has_side_effects=True),
    )(*arrays)


N_CHIPS = 4


def _pair_exchange(name, arrays):
    n = len(arrays)

    def body(*refs):
        srcs, dsts = refs[:n], refs[n:2 * n]
        send_sems, recv_sems, local_sems = refs[2 * n:]
        x, y, c = lax.axis_index("x"), lax.axis_index("y"), lax.axis_index("c")
        sibling = (x, y, 1 - c)

        def remote(k, j):
            return pltpu.make_async_remote_copy(
                src_ref=srcs[k].at[2 * j + (1 - c)], dst_ref=dsts[k].at[c, j], send_sem=send_sems.at[k * N_CHIPS + j],
                recv_sem=recv_sems.at[k * N_CHIPS + j], device_id=sibling, device_id_type=pl.DeviceIdType.MESH)

        def arrival(k, j):
            return pltpu.make_async_remote_copy(
                src_ref=srcs[k].at[2 * j + c], dst_ref=dsts[k].at[1 - c, j], send_sem=send_sems.at[k * N_CHIPS + j],
                recv_sem=recv_sems.at[k * N_CHIPS + j], device_id=sibling, device_id_type=pl.DeviceIdType.MESH)

        local = [pltpu.make_async_copy(srcs[k].at[2 * j + c], dsts[k].at[c, j], local_sems.at[k * N_CHIPS + j])
                 for k in range(n) for j in range(N_CHIPS)]
        sends = [remote(k, j) for k in range(n) for j in range(N_CHIPS)]
        for cp in local + sends:
            cp.start()
        for k in range(n):
            for j in range(N_CHIPS):
                arrival(k, j).wait_recv()
        for cp in sends:
            cp.wait_send()
        for cp in local:
            cp.wait()

    any_spec = pl.BlockSpec(memory_space=pl.ANY)
    return pl.pallas_call(
        body, name=name, out_shape=[jax.ShapeDtypeStruct((2, N_CHIPS) + a.shape[1:], a.dtype) for a in arrays],
        in_specs=[any_spec] * n, out_specs=[any_spec] * n,
        scratch_shapes=[pltpu.SemaphoreType.DMA((N_CHIPS * n,)), pltpu.SemaphoreType.DMA((N_CHIPS * n,)),
                        pltpu.SemaphoreType.DMA((N_CHIPS * n,))],
        compiler_params=pltpu.CompilerParams(has_side_effects=True),
    )(*arrays)


def _chip_exchange(name, arrays):
    n = len(arrays)

    def body(*refs):
        srcs, dsts = refs[:n], refs[n:2 * n]
        send_sems, recv_sems, local_sems = refs[2 * n:]
        x, y, c = lax.axis_index("x"), lax.axis_index("y"), lax.axis_index("c")
        my_chip = 2 * x + y

        def peer(d):
            px, py = (x + (d >> 1)) % 2, (y + (d & 1)) % 2
            return (px, py, c), 2 * px + py

        def remote(k, d, started):
            to, chip = peer(d)
            return pltpu.make_async_remote_copy(
                src_ref=srcs[k].at[chip], dst_ref=dsts[k].at[my_chip if started else chip],
                send_sem=send_sems.at[k * 3 + d - 1], recv_sem=recv_sems.at[k * 3 + d - 1],
                device_id=to, device_id_type=pl.DeviceIdType.MESH)

        local = [pltpu.make_async_copy(srcs[k].at[my_chip], dsts[k].at[my_chip], local_sems.at[k]) for k in range(n)]
        sends = [remote(k, d, True) for k in range(n) for d in range(1, N_CHIPS)]
        for cp in local + sends:
            cp.start()
        for k in range(n):
            for d in range(1, N_CHIPS):
                remote(k, d, False).wait_recv()
        for cp in sends:
            cp.wait_send()
        for cp in local:
            cp.wait()

    any_spec = pl.BlockSpec(memory_space=pl.ANY)
    return pl.pallas_call(
        body, name=name, out_shape=[jax.ShapeDtypeStruct(a.shape, a.dtype) for a in arrays],
        in_specs=[any_spec] * n, out_specs=[any_spec] * n,
        scratch_shapes=[pltpu.SemaphoreType.DMA((3 * n,)), pltpu.SemaphoreType.DMA((3 * n,)),
                        pltpu.SemaphoreType.DMA((n,))],
        compiler_params=pltpu.CompilerParams(has_side_effects=True),
    )(*arrays)


def _pair_sum(name, pair):
    _, n_blocks, rows, cols = pair.shape
    tr = rows if rows % 256 else 256

    def body(p_ref, o_ref):
        o_ref[...] = (p_ref[0].astype(f32) + p_ref[1].astype(f32)).astype(bf16)

    return pl.pallas_call(
        body, name=name, grid=(n_blocks, rows // tr),
        in_specs=[pl.BlockSpec((2, 1, tr, cols), lambda j, i: (0, j, i, 0))],
        out_specs=pl.BlockSpec((1, tr, cols), lambda j, i: (j, i, 0)),
        out_shape=jax.ShapeDtypeStruct((n_blocks, rows, cols), bf16),
        compiler_params=_params("parallel", "parallel"),
    )(pair)


def _gather_two_level(name, arrays):
    n = len(arrays)

    def body(*refs):
        srcs, dsts = refs[:n], refs[n:2 * n]
        send_sems, recv_sems, local_sems = refs[2 * n:]
        x, y, c = lax.axis_index("x"), lax.axis_index("y"), lax.axis_index("c")
        sibling = (x, y, 1 - c)
        chips = [((x + 1) % 2, y), (x, (y + 1) % 2), ((x + 1) % 2, (y + 1) % 2)]

        def index(px, py, pc):
            return 4 * px + 2 * py + pc

        def copy(k, slot, block, to, src=None):
            return pltpu.make_async_remote_copy(
                src_ref=dsts[k].at[index(*block)] if src is None else src, dst_ref=dsts[k].at[index(*block)],
                send_sem=send_sems.at[k * 7 + slot], recv_sem=recv_sems.at[k * 7 + slot],
                device_id=to, device_id_type=pl.DeviceIdType.MESH)

        me = (x, y, c)
        local = [pltpu.make_async_copy(srcs[k], dsts[k].at[index(*me)], local_sems.at[k]) for k in range(n)]
        first = [copy(k, 0, me, sibling, src=srcs[k]) for k in range(n)]
        first += [copy(k, 1 + j, me, (*chip, c), src=srcs[k]) for j, chip in enumerate(chips) for k in range(n)]
        for cp in local + first:
            cp.start()
        passed = []
        for j, chip in enumerate(chips):
            for k in range(n):
                copy(k, 1 + j, (*chip, c), me).wait_recv()
                fwd = copy(k, 4 + j, (*chip, c), sibling)
                fwd.start()
                passed.append(fwd)
        for k in range(n):
            copy(k, 0, sibling, me).wait_recv()
            for j, chip in enumerate(chips):
                copy(k, 4 + j, (*chip, 1 - c), me).wait_recv()
        for cp in first + passed:
            cp.wait_send()
        for cp in local:
            cp.wait()

    any_spec = pl.BlockSpec(memory_space=pl.ANY)
    return pl.pallas_call(
        body, name=name, out_shape=[jax.ShapeDtypeStruct((N_DEV,) + a.shape, a.dtype) for a in arrays],
        in_specs=[any_spec] * n, out_specs=[any_spec] * n,
        scratch_shapes=[pltpu.SemaphoreType.DMA((7 * n,)), pltpu.SemaphoreType.DMA((7 * n,)),
                        pltpu.SemaphoreType.DMA((n,))],
        compiler_params=pltpu.CompilerParams(has_side_effects=True),
    )(*arrays)


def _mod_shard(c_all, w_ada, b_shard):
    def body(c_ref, w_ref, b_ref, o_ref):
        cv = c_ref[...]
        ca = cv * _sigmoid(cv)
        o_ref[...] = jnp.dot(ca.astype(bf16), w_ref[...].astype(bf16), preferred_element_type=f32) + b_ref[...]

    return pl.pallas_call(body, name="mod_shard", out_shape=jax.ShapeDtypeStruct((N_DEV, w_ada.shape[1]), f32),
                          compiler_params=_params())(c_all, w_ada, b_shard)


def _in_proj(x, norm_g, scale1p, shift, wt_main, wt_small):
    s_len = x.shape[0]
    tm, tn = 512, 1024

    def body(x_ref, g_ref, sc_ref, sh_ref, w_ref, ws_ref, p_ref, ps_ref, h_ref, h_sc):
        @pl.when(pl.program_id(1) == 0)
        def _():
            xb = x_ref[...]
            r = lax.rsqrt(jnp.mean(xb * xb, axis=-1, keepdims=True) + EPS)
            hb = ((xb * r * g_ref[...]) * sc_ref[...] + sh_ref[...]).astype(bf16)
            h_sc[...] = hb
            h_ref[...] = hb
            ps_ref[...] = _dg(hb, ws_ref[...], 1, 1)

        p_ref[...] = _dg(h_sc[...], w_ref[...], 1, 1)

    vec = pl.BlockSpec((1, D_MODEL), lambda i, j: (0, 0))
    return pl.pallas_call(
        body, name="in_proj", grid=(s_len // tm, N_MAIN // tn),
        in_specs=[pl.BlockSpec((tm, D_MODEL), lambda i, j: (i, 0)), vec, vec, vec,
                  pl.BlockSpec((tn, D_MODEL), lambda i, j: (j, 0)),
                  pl.BlockSpec((N_SMALL, D_MODEL), lambda i, j: (0, 0))],
        out_specs=[pl.BlockSpec((tm, tn), lambda i, j: (i, j)),
                   pl.BlockSpec((tm, N_SMALL), lambda i, j: (i, 0)),
                   pl.BlockSpec((tm, D_MODEL), lambda i, j: (i, 0))],
        out_shape=[jax.ShapeDtypeStruct((s_len, N_MAIN), f32), jax.ShapeDtypeStruct((s_len, N_SMALL), f32),
                   jax.ShapeDtypeStruct((s_len, D_MODEL), bf16)],
        scratch_shapes=[pltpu.VMEM((tm, D_MODEL), bf16)],
        compiler_params=_params("parallel", "arbitrary"),
    )(x, norm_g, scale1p, shift, wt_main, wt_small)


PREP_TM = 256
HALO = 8


def _conv_section(xe_ref, cw_ref, cols, tm):
    acc = cw_ref[pl.ds(CONV_K - 1, 1), cols] * xe_ref[pl.ds(HALO, tm), :]
    for tap in range(CONV_K - 1):
        acc = acc + cw_ref[pl.ds(tap, 1), cols] * xe_ref[pl.ds(HALO - (CONV_K - 1) + tap, tm), :]
    return acc


def _small_fwd(ps, bvec, alog):
    z = ps + bvec
    logsig, softp = _softplus_parts(z)
    gval = -jnp.exp(alog) * softp
    beta = _sigmoid(ps)
    return z, logsig, gval, beta


def _head_lane(block, lane):
    return jnp.sum(jnp.where(_iota(block.shape, 1) == lane, block, 0.0), axis=1, keepdims=True)


def _head_slab(block, lane):
    return jnp.broadcast_to(_head_lane(block, lane), block.shape)


def _chunk_masks(tm):
    r, c = _iota((tm, tm), 0), _iota((tm, tm), 1)
    same = (r // CHUNK) == (c // CHUNK)
    return (same & (r >= c)).astype(f32), same.astype(f32)


def _prep(p_main, p_small, qn_g, kn_g, conv_w, bvec, alog):
    s_len = p_main.shape[0]
    tm = PREP_TM
    nb = s_len // tm

    def body(fq_ref, fk_ref, fv_ref, gq_ref, gk_ref, gv_ref, hq_ref, hk_ref, hv_ref, ps_ref, qg_ref, kg_ref,
             cw_ref, bv_ref, al_ref,
             qs_ref, kn_ref, vb_ref, gqo_ref, gko_ref, gvo_ref, small_ref, xe_sc, carry_sc):
        i = pl.program_id(0)

        @pl.when(i == 0)
        def _():
            carry_sc[...] = jnp.zeros_like(carry_sc)

        qg, kg = qg_ref[...], kg_ref[...]
        for h in range(HEADS):
            sl = slice(h * HEAD_DIM, (h + 1) * HEAD_DIM)
            q = fq_ref[:, sl]
            rq = lax.rsqrt(jnp.mean(q * q, axis=-1, keepdims=True) + EPS)
            qs_ref[:, sl] = (q * rq * qg * QK_SCALE).astype(bf16)
            k = fk_ref[:, sl]
            rk = lax.rsqrt(jnp.mean(k * k, axis=-1, keepdims=True) + EPS)
            kn_ref[:, sl] = (k * rk * kg).astype(bf16)
        vb_ref[...] = fv_ref[...].astype(bf16)

        first = i == 0
        for sec, (x_ref, halo_ref, o_ref) in enumerate(((gq_ref, hq_ref, gqo_ref), (gk_ref, hk_ref, gko_ref),
                                                        (gv_ref, hv_ref, gvo_ref))):
            xe_sc[0:HALO, :] = jnp.where(first, 0.0, halo_ref[...])
            xe_sc[HALO:, :] = x_ref[...]
            cv = _conv_section(xe_sc, cw_ref, slice(sec * WIDTH, (sec + 1) * WIDTH), tm)
            y = cv * _sigmoid(cv)
            if sec == 2:
                o_ref[...] = y
            else:
                mul = QK_SCALE if sec == 0 else 1.0
                for h in range(HEADS):
                    sl = slice(h * HEAD_DIM, (h + 1) * HEAD_DIM)
                    yh = y[:, sl]
                    o_ref[:, sl] = yh * (lax.rsqrt(jnp.sum(yh * yh, axis=-1, keepdims=True) + EPS) * mul)

        lane = _iota((tm, N_SMALL), 1)
        _, logsig, gval, beta = _small_fwd(ps_ref[...], bv_ref[...], al_ref[...])
        lf = jnp.where(lane < HEADS, logsig, 0.0)
        tri = (_iota((tm, tm), 0) >= _iota((tm, tm), 1)).astype(f32)
        fcum = jnp.dot(tri, lf, preferred_element_type=f32, precision=HI) + carry_sc[...]
        carry_sc[...] += jnp.sum(lf, axis=0, keepdims=True)
        tri_c, ones_c = _chunk_masks(tm)
        g_lanes = jnp.where((lane >= LANE_G) & (lane < LANE_G + HEADS), gval, 0.0)
        gc = jnp.dot(tri_c, g_lanes, preferred_element_type=f32, precision=HI)
        g_last = jnp.dot(ones_c, g_lanes, preferred_element_type=f32, precision=HI)
        small = jnp.where(lane < LANE_G, fcum, jnp.where(lane < LANE_BETA, gval, jnp.where(lane < LANE_GC, beta, 0.0)))
        small_ref[...] = small + pltpu.roll(gc, LANE_GC - LANE_G, 1) + pltpu.roll(g_last, LANE_GLAST - LANE_G, 1)

    def col(cb):
        return pl.BlockSpec((tm, WIDTH), lambda i: (i, cb))

    def halo(cb):
        return pl.BlockSpec((HALO, WIDTH), lambda i: (jnp.maximum(i * (tm // HALO) - 1, 0), cb))

    vec = pl.BlockSpec((1, LANES), lambda i: (0, 0))
    wide_f32 = jax.ShapeDtypeStruct((s_len, WIDTH), f32)
    wide_bf = jax.ShapeDtypeStruct((s_len, WIDTH), bf16)
    out_col = pl.BlockSpec((tm, WIDTH), lambda i: (i, 0))
    return pl.pallas_call(
        body, name="prep", grid=(nb,),
        in_specs=[col(0), col(1), col(2), col(4), col(5), col(6), halo(4), halo(5), halo(6),
                  pl.BlockSpec((tm, N_SMALL), lambda i: (i, 0)), vec, vec,
                  pl.BlockSpec((CONV_K, 3 * WIDTH), lambda i: (0, 0)), vec, vec],
        out_specs=[out_col] * 6 + [pl.BlockSpec((tm, N_SMALL), lambda i: (i, 0))],
        out_shape=[wide_bf, wide_bf, wide_bf, wide_f32, wide_f32, wide_f32,
                   jax.ShapeDtypeStruct((s_len, N_SMALL), f32)],
        scratch_shapes=[pltpu.VMEM((tm + HALO, WIDTH), f32), pltpu.VMEM((1, N_SMALL), f32)],
        compiler_params=_params("arbitrary"),
    )(p_main, p_main, p_main, p_main, p_main, p_main, p_main, p_main, p_main, p_small, qn_g, kn_g, conv_w, bvec, alog)


FOX_T = 1024
NEG_BIG = -1e30


def _fox_fwd(qs, kn, vb, small, f_row):
    s_len = qs.shape[0]
    t = FOX_T
    nq = s_len // t

    def body(q_ref, k_ref, v_ref, sm_ref, fr_ref, o_ref, lse_ref):
        qi = pl.program_id(1)
        q = q_ref[...]
        fq = _head_lane(sm_ref[...], pl.program_id(0))
        causal = _iota((t, t), 0) >= _iota((t, t), 1)

        def step(j, carry, masked):
            m, l, acc = carry
            rows = pl.ds(pl.multiple_of(j * t, t), t)
            s = _dg(q, k_ref[rows, :], 1, 1) + (fq - fr_ref[0, j])
            if masked:
                s = jnp.where(causal, s, NEG_BIG)
            m_new = jnp.maximum(m, jnp.max(s, axis=-1, keepdims=True))
            p = jnp.exp(s - m_new)
            alpha = jnp.exp(m - m_new)
            l = alpha * l + jnp.sum(p, axis=-1, keepdims=True)
            acc = alpha * acc + jnp.dot(p.astype(bf16), v_ref[rows, :], preferred_element_type=f32)
            return m_new, l, acc

        init = (jnp.full((t, 1), NEG_BIG, f32), jnp.zeros((t, 1), f32), jnp.zeros((t, HEAD_DIM), f32))
        carry = lax.fori_loop(0, qi, lambda j, c: step(j, c, False), init)
        m, l, acc = step(qi, carry, True)
        o_ref[...] = acc / l
        lse_ref[0] = m + jnp.log(l)

    return pl.pallas_call(
        body, name="fox_fwd", grid=(HEADS, nq),
        in_specs=[pl.BlockSpec((t, HEAD_DIM), lambda h, i: (i, h)),
                  pl.BlockSpec((s_len, HEAD_DIM), lambda h, i: (0, h)),
                  pl.BlockSpec((s_len, HEAD_DIM), lambda h, i: (0, h)),
                  pl.BlockSpec((t, N_SMALL), lambda h, i: (i, 0)),
                  pl.BlockSpec((1, nq, 1, t), lambda h, i: (h, 0, 0, 0))],
        out_specs=[pl.BlockSpec((t, HEAD_DIM), lambda h, i: (i, h)),
                   pl.BlockSpec((1, t, 1), lambda h, i: (h, i, 0))],
        out_shape=[jax.ShapeDtypeStruct((s_len, WIDTH), f32), jax.ShapeDtypeStruct((HEADS, s_len, 1), f32)],
        compiler_params=_params("parallel", "arbitrary"),
    )(qs, kn, vb, small, f_row)


def _fox_bwd(qs, kn, vb, do, small, lse, delta, f_row):
    s_len = qs.shape[0]
    t = FOX_T
    nq = s_len // t

    def body(q_ref, do_ref, sm_ref, lse_ref, dl_ref, k_ref, v_ref, fr_ref, dq_ref, dk_ref, dv_ref, df_ref, dfq_ref):
        head, qi = pl.program_id(0), pl.program_id(1)

        @pl.when(qi == 0)
        def _():
            dk_ref[...] = jnp.zeros_like(dk_ref)
            dv_ref[...] = jnp.zeros_like(dv_ref)
            df_ref[...] = jnp.zeros_like(df_ref)

        q, do_b = q_ref[...], do_ref[...]
        a = _head_lane(sm_ref[...], head) - lse_ref[0]
        dl = _head_lane(dl_ref[...], head)
        causal = _iota((t, t), 0) >= _iota((t, t), 1)

        def step(j, carry, masked):
            dq, row_sum = carry
            rows = pl.ds(pl.multiple_of(j * t, t), t)
            kj, vj = k_ref[rows, :], v_ref[rows, :]
            p = jnp.exp(_dg(q, kj, 1, 1) + (a - fr_ref[0, j]))
            if masked:
                p = jnp.where(causal, p, 0.0)
            ds = p * (_dg(do_b, vj, 1, 1) - dl)
            ds_b = ds.astype(bf16)
            dk_ref[rows, :] += _dg(ds_b, q, 0, 0)
            dv_ref[rows, :] += _dg(p.astype(bf16), do_b, 0, 0)
            df_ref[0, j] += -jnp.sum(ds, axis=0, keepdims=True)
            return dq + jnp.dot(ds_b, kj, preferred_element_type=f32), row_sum + jnp.sum(ds, axis=-1, keepdims=True)

        carry = lax.fori_loop(0, qi, lambda j, c: step(j, c, False),
                              (jnp.zeros((t, HEAD_DIM), f32), jnp.zeros((t, 1), f32)))
        dq, row_sum = step(qi, carry, True)
        dq_ref[...] = dq
        dfq_ref[0] = row_sum

    blk = pl.BlockSpec((t, HEAD_DIM), lambda h, i: (i, h))
    full = pl.BlockSpec((s_len, HEAD_DIM), lambda h, i: (0, h))
    colv = pl.BlockSpec((1, t, 1), lambda h, i: (h, i, 0))
    rowv = pl.BlockSpec((1, nq, 1, t), lambda h, i: (h, 0, 0, 0))
    lanes = pl.BlockSpec((t, N_SMALL), lambda h, i: (i, 0))
    wide = jax.ShapeDtypeStruct((s_len, WIDTH), f32)
    return pl.pallas_call(
        body, name="fox_bwd", grid=(HEADS, nq),
        in_specs=[blk, blk, lanes, colv, lanes, full, full, rowv],
        out_specs=[blk, full, full, rowv, colv],
        out_shape=[wide, wide, wide, jax.ShapeDtypeStruct((HEADS, nq, 1, t), f32),
                   jax.ShapeDtypeStruct((HEADS, s_len, 1), f32)],
        compiler_params=_params("parallel", "arbitrary"),
    )(qs, do, small, lse, delta, kn, vb, f_row)


INTRA_CHUNKS = 8
SCAN_FWD_CHUNKS = 8
SCAN_BWD_CHUNKS = 4


def _gdn_intra_fwd(gq, gk, gv, small):
    s_len = gq.shape[0]
    cpb = INTRA_CHUNKS
    rows_blk = cpb * CHUNK
    n_chunks = s_len // CHUNK

    def body(q_ref, k_ref, v_ref, sm_ref, u_ref, w_ref, qg_ref, kd_ref, attn_ref, t_ref, eg_ref):
        head = pl.program_id(0)
        sm = sm_ref[...]
        gc_b, gl_b, beta_b = (_head_slab(sm, LANE_GC + head), _head_slab(sm, LANE_GLAST + head),
                              _head_slab(sm, LANE_BETA + head))
        ms, rhss = [], []
        for ci in range(cpb):
            rows = pl.ds(ci * CHUNK, CHUNK)
            sl = slice(ci * CHUNK, (ci + 1) * CHUNK)
            m, rhs, qg, kd, attn, eg_last = _gdn_intra_pre(q_ref[rows, :], k_ref[rows, :], v_ref[rows, :],
                                                           gc_b[sl], gl_b[sl], beta_b[sl], _BF_PLAIN[1])
            qg_ref[rows, :] = qg.astype(bf16)
            kd_ref[rows, :] = kd.astype(bf16)
            attn_ref[0, ci] = attn.astype(bf16)
            eg_ref[0, ci] = eg_last
            ms.append(m)
            rhss.append(rhs)
        for ci, (t, rhs) in enumerate(zip(_inv_unit_lower_many(ms), rhss)):
            rows = pl.ds(ci * CHUNK, CHUNK)
            t_ref[0, ci] = t
            uw = _X3_PLAIN[0](t, rhs)
            u_ref[rows, :] = uw[:, :HEAD_DIM]
            w_ref[rows, :] = uw[:, HEAD_DIM:].astype(bf16)

    blk = pl.BlockSpec((rows_blk, HEAD_DIM), lambda h, i: (i, h))
    sq = pl.BlockSpec((1, cpb, CHUNK, CHUNK), lambda h, i: (h, i, 0, 0))
    wide_bf = jax.ShapeDtypeStruct((s_len, WIDTH), bf16)
    return pl.pallas_call(
        body, name="gdn_intra_fwd", grid=(HEADS, s_len // rows_blk),
        in_specs=[blk] * 3 + [pl.BlockSpec((rows_blk, N_SMALL), lambda h, i: (i, 0))],
        out_specs=[blk] * 4 + [sq, sq, pl.BlockSpec((1, cpb, SUBLANES, HEAD_DIM), lambda h, i: (h, i, 0, 0))],
        out_shape=[jax.ShapeDtypeStruct((s_len, WIDTH), f32), wide_bf, wide_bf, wide_bf,
                   jax.ShapeDtypeStruct((HEADS, n_chunks, CHUNK, CHUNK), bf16),
                   jax.ShapeDtypeStruct((HEADS, n_chunks, CHUNK, CHUNK), f32),
                   jax.ShapeDtypeStruct((HEADS, n_chunks, SUBLANES, HEAD_DIM), f32)],
        compiler_params=_params("parallel", "parallel"),
    )(gq, gk, gv, small)


def _gdn_scan_fwd(u, w, qg, kd, attn, eg):
    s_len = u.shape[0]
    cpb = SCAN_FWD_CHUNKS
    rows_blk = cpb * CHUNK
    n_chunks = s_len // CHUNK

    def body(u_ref, w_ref, qg_ref, kd_ref, attn_ref, eg_ref, o_ref, st_ref, s_sc):
        @pl.when(pl.program_id(0) == 0)
        def _():
            s_sc[...] = jnp.zeros_like(s_sc)

        def chunk(ci, _):
            rows = pl.ds(pl.multiple_of(ci * CHUNK, CHUNK), CHUNK)
            cols = [slice(h * HEAD_DIM, (h + 1) * HEAD_DIM) for h in range(HEADS)]
            s0 = [s_sc[h] for h in range(HEADS)]
            s0_b = [s.astype(bf16) for s in s0]
            for h in range(HEADS):
                st_ref[h, ci] = s0[h]
            ws = [jnp.dot(w_ref[rows, cols[h]], s0_b[h], preferred_element_type=f32) for h in range(HEADS)]
            qs = [jnp.dot(qg_ref[rows, cols[h]], s0_b[h], preferred_element_type=f32) for h in range(HEADS)]
            vn_b = [(u_ref[rows, cols[h]] - ws[h]).astype(bf16) for h in range(HEADS)]
            av = [jnp.dot(attn_ref[h, ci], vn_b[h], preferred_element_type=f32) for h in range(HEADS)]
            kv = [_dg(kd_ref[rows, cols[h]], vn_b[h], 0, 0) for h in range(HEADS)]
            for h in range(HEADS):
                o_ref[rows, cols[h]] = qs[h] + av[h]
                s_sc[h] = _scale_rows(s0[h], eg_ref[h, ci]) + kv[h]
            return 0

        lax.fori_loop(0, cpb, chunk, 0)

    row = pl.BlockSpec((rows_blk, WIDTH), lambda i: (i, 0))
    return pl.pallas_call(
        body, name="gdn_scan_fwd", grid=(s_len // rows_blk,),
        in_specs=[row] * 4 + [pl.BlockSpec((HEADS, cpb, CHUNK, CHUNK), lambda i: (0, i, 0, 0)),
                              pl.BlockSpec((HEADS, cpb, SUBLANES, HEAD_DIM), lambda i: (0, i, 0, 0))],
        out_specs=[row, pl.BlockSpec((HEADS, cpb, HEAD_DIM, HEAD_DIM), lambda i: (0, i, 0, 0))],
        out_shape=[jax.ShapeDtypeStruct((s_len, WIDTH), f32),
                   jax.ShapeDtypeStruct((HEADS, n_chunks, HEAD_DIM, HEAD_DIM), f32)],
        scratch_shapes=[pltpu.VMEM((HEADS, HEAD_DIM, HEAD_DIM), f32)],
        compiler_params=_params("arbitrary"),
    )(u, w, qg, kd, attn, eg)


def _gdn_scan_bwd(u, w, qg, kd, attn, eg, states, d_o):
    s_len = u.shape[0]
    cpb = SCAN_BWD_CHUNKS
    rows_blk = cpb * CHUNK
    n_chunks = s_len // CHUNK
    nb = s_len // rows_blk

    def body(u_ref, w_ref, qg_ref, kd_ref, attn_ref, eg_ref, st_ref, do_ref,
             du_ref, dw_ref, dqg_ref, dkd_ref, dattn_ref, deg_ref, ds_sc):
        @pl.when(pl.program_id(0) == 0)
        def _():
            ds_sc[...] = jnp.zeros_like(ds_sc)

        def chunk(step, _):
            ci = cpb - 1 - step
            rows = pl.ds(pl.multiple_of(ci * CHUNK, CHUNK), CHUNK)
            hs = range(HEADS)
            cols = [slice(h * HEAD_DIM, (h + 1) * HEAD_DIM) for h in hs]
            s0 = [st_ref[h, ci] for h in hs]
            s0_b = [s.astype(bf16) for s in s0]
            ds1 = [ds_sc[h] for h in hs]
            ds1_b = [d.astype(bf16) for d in ds1]
            do_b = [do_ref[rows, cols[h]].astype(bf16) for h in hs]
            ws = [jnp.dot(w_ref[rows, cols[h]], s0_b[h], preferred_element_type=f32) for h in hs]
            ad = [_dg(attn_ref[h, ci], do_b[h], 0, 0) for h in hs]
            kd_ds = [jnp.dot(kd_ref[rows, cols[h]], ds1_b[h], preferred_element_type=f32) for h in hs]
            dqg = [_dg(do_b[h], s0_b[h], 1, 1) for h in hs]
            qd = [_dg(qg_ref[rows, cols[h]], do_b[h], 0, 0) for h in hs]
            vn_b = [(u_ref[rows, cols[h]] - ws[h]).astype(bf16) for h in hs]
            dvn = [ad[h] + kd_ds[h] for h in hs]
            dvn_b = [d.astype(bf16) for d in dvn]
            dattn = [_dg(do_b[h], vn_b[h], 1, 1) for h in hs]
            dkd = [_dg(vn_b[h], ds1_b[h], 1, 1) for h in hs]
            dw = [_dg(dvn_b[h], s0_b[h], 1, 1) for h in hs]
            wd = [_dg(w_ref[rows, cols[h]], dvn_b[h], 0, 0) for h in hs]
            for h in hs:
                dattn_ref[h, ci] = dattn[h]
                dqg_ref[rows, cols[h]] = dqg[h]
                dkd_ref[rows, cols[h]] = dkd[h]
                du_ref[rows, cols[h]] = dvn[h]
                dw_ref[rows, cols[h]] = -dw[h]
                ds_sc[h] = qd[h] - wd[h] + _scale_rows(ds1[h], eg_ref[h, ci])
                deg_ref[h, ci] = jnp.sum((ds1[h] * s0[h]).reshape(HEAD_DIM // SUBLANES, SUBLANES, HEAD_DIM), axis=0)
            return 0

        lax.fori_loop(0, cpb, chunk, 0)

    row = pl.BlockSpec((rows_blk, WIDTH), lambda i: (nb - 1 - i, 0))
    sq = pl.BlockSpec((HEADS, cpb, CHUNK, CHUNK), lambda i: (0, nb - 1 - i, 0, 0))
    egs = pl.BlockSpec((HEADS, cpb, SUBLANES, HEAD_DIM), lambda i: (0, nb - 1 - i, 0, 0))
    wide = jax.ShapeDtypeStruct((s_len, WIDTH), f32)
    return pl.pallas_call(
        body, name="gdn_scan_bwd", grid=(nb,),
        in_specs=[row] * 4 + [sq, egs, pl.BlockSpec((HEADS, cpb, HEAD_DIM, HEAD_DIM), lambda i: (0, nb - 1 - i, 0, 0)), row],
        out_specs=[row] * 4 + [sq, egs],
        out_shape=[wide] * 4 + [jax.ShapeDtypeStruct((HEADS, n_chunks, CHUNK, CHUNK), f32),
                                jax.ShapeDtypeStruct((HEADS, n_chunks, SUBLANES, HEAD_DIM), f32)],
        scratch_shapes=[pltpu.VMEM((HEADS, HEAD_DIM, HEAD_DIM), f32)],
        compiler_params=_params("arbitrary"),
    )(u, w, qg, kd, attn, eg, states, d_o)


def _gdn_intra_bwd(gq, gk, gv, small, t_inv, du, dw, dqg, dkd, dattn, deg):
    s_len = gq.shape[0]
    cpb = INTRA_CHUNKS
    rows_blk = cpb * CHUNK

    def body(q_ref, k_ref, v_ref, sm_ref, t_ref, du_ref, dw_ref, dqg_ref, dkd_ref, dattn_ref, deg_ref,
             dq_ref, dk_ref, dv_ref, dsm_ref):
        head = pl.program_id(1)

        def batch(value):
            return value.reshape(cpb, CHUNK, HEAD_DIM)

        sm = sm_ref[...]
        slabs = [batch(_head_slab(sm, first + head)) for first in (LANE_GC, LANE_GLAST, LANE_BETA)]
        t_known = t_ref[0]
        _, vjp = jax.vjp(lambda q, k, v, gc, gl, b: _gdn_intra(q, k, v, gc, gl, b, t_known),
                         batch(q_ref[...]), batch(k_ref[...]), batch(v_ref[...]), *slabs)
        duw = jnp.concatenate([batch(du_ref[...]), batch(dw_ref[...])], axis=-1)
        dq, dk, dv, dgc, dgl, db = vjp((duw, batch(dqg_ref[...]), batch(dkd_ref[...]), dattn_ref[0], deg_ref[0]))
        for ref, grad in zip((dq_ref, dk_ref, dv_ref), (dq, dk, dv)):
            ref[...] = grad.reshape(rows_blk, HEAD_DIM)

        @pl.when(head == 0)
        def _():
            dsm_ref[...] = jnp.zeros_like(dsm_ref)

        lane = _iota((rows_blk, N_SMALL), 1)
        acc = dsm_ref[...]
        for first, grad in ((LANE_GC, dgc), (LANE_GLAST, dgl), (LANE_BETA, db)):
            col = jnp.sum(grad.reshape(rows_blk, HEAD_DIM), axis=1, keepdims=True)
            acc = acc + jnp.where(lane == first + head, col, 0.0)
        dsm_ref[...] = acc

    blk = pl.BlockSpec((rows_blk, HEAD_DIM), lambda i, h: (i, h))
    sq = pl.BlockSpec((1, cpb, CHUNK, CHUNK), lambda i, h: (h, i, 0, 0))
    egs = pl.BlockSpec((1, cpb, SUBLANES, HEAD_DIM), lambda i, h: (h, i, 0, 0))
    lanes = pl.BlockSpec((rows_blk, N_SMALL), lambda i, h: (i, 0))
    wide = jax.ShapeDtypeStruct((s_len, WIDTH), f32)
    return pl.pallas_call(
        body, name="gdn_intra_bwd", grid=(s_len // rows_blk, HEADS),
        in_specs=[blk] * 3 + [lanes, sq] + [blk] * 4 + [sq, egs],
        out_specs=[blk] * 3 + [lanes],
        out_shape=[wide] * 3 + [jax.ShapeDtypeStruct((s_len, N_SMALL), f32)],
        compiler_params=_params("parallel", "arbitrary"),
    )(gq, gk, gv, small, t_inv, du, dw, dqg, dkd, dattn, deg)


MIX_TM = 256


def _mix_fwd(fox_o, gdn_o, p_main, gnorm_g):
    s_len = fox_o.shape[0]
    tm = MIX_TM

    def body(fo_ref, go_ref, fz_ref, gz_ref, g_ref, mixed_ref):
        fz = fz_ref[...]
        mixed_ref[:, 0:WIDTH] = (fo_ref[...] * (fz * _sigmoid(fz))).astype(bf16)
        gz = gz_ref[...]
        gate = gz * _sigmoid(gz)
        gg = g_ref[...]
        for h in range(HEADS):
            sl = slice(h * HEAD_DIM, (h + 1) * HEAD_DIM)
            o = go_ref[:, sl]
            r = lax.rsqrt(jnp.mean(o * o, axis=-1, keepdims=True) + EPS)
            mixed_ref[:, WIDTH + h * HEAD_DIM:WIDTH + (h + 1) * HEAD_DIM] = (o * r * gg * gate[:, sl]).astype(bf16)

    row = pl.BlockSpec((tm, WIDTH), lambda i: (i, 0))
    return pl.pallas_call(
        body, name="mix_fwd", grid=(s_len // tm,),
        in_specs=[row, row, pl.BlockSpec((tm, WIDTH), lambda i: (i, 3)), pl.BlockSpec((tm, WIDTH), lambda i: (i, 7)),
                  pl.BlockSpec((1, LANES), lambda i: (0, 0))],
        out_specs=pl.BlockSpec((tm, 2 * WIDTH), lambda i: (i, 0)),
        out_shape=jax.ShapeDtypeStruct((s_len, 2 * WIDTH), bf16),
        compiler_params=_params("parallel"),
    )(fox_o, gdn_o, p_main, p_main, gnorm_g)


def _silu_grad(z):
    sg = _sigmoid(z)
    return sg * (1.0 + z * (1.0 - sg))


def _mix_bwd(dmixed, fox_o, gdn_o, p_main, gnorm_g):
    s_len = fox_o.shape[0]
    tm = MIX_TM

    def body(dm_ref, fo_ref, go_ref, fz_ref, gz_ref, g_ref, dof_ref, delta_ref, dfz_ref, dgz_ref, dgo_ref, dg_ref):
        @pl.when(pl.program_id(0) == 0)
        def _():
            dg_ref[...] = jnp.zeros_like(dg_ref)

        lane = _iota((tm, LANES), 1)
        fz = fz_ref[...]
        dmf = dm_ref[:, 0:WIDTH]
        fo = fo_ref[...]
        dof = dmf * (fz * _sigmoid(fz))
        dof_ref[...] = dof.astype(bf16)
        dfz_ref[...] = (dmf * fo * _silu_grad(fz)).astype(bf16)
        prod = dof * fo
        delta = jnp.zeros((tm, LANES), f32)
        for h in range(HEADS):
            dh = jnp.sum(prod[:, h * HEAD_DIM:(h + 1) * HEAD_DIM], axis=-1, keepdims=True)
            delta = jnp.where(lane == h, dh, delta)
        delta_ref[...] = delta

        gz = gz_ref[...]
        dmg = dm_ref[:, WIDTH:2 * WIDTH]
        gate = gz * _sigmoid(gz)
        sgrad = _silu_grad(gz)
        gg = g_ref[...]
        dg_acc = jnp.zeros((1, HEAD_DIM), f32)
        for h in range(HEADS):
            sl = slice(h * HEAD_DIM, (h + 1) * HEAD_DIM)
            o = go_ref[:, sl]
            r = lax.rsqrt(jnp.mean(o * o, axis=-1, keepdims=True) + EPS)
            on = o * r
            dmh = dmg[:, sl]
            dgz_ref[:, sl] = (dmh * (on * gg) * sgrad[:, sl]).astype(bf16)
            dy = dmh * gate[:, sl]
            dg_acc = dg_acc + jnp.sum(dy * on, axis=0, keepdims=True)
            tt = dy * gg
            dgo_ref[:, sl] = r * (tt - on * jnp.mean(tt * on, axis=-1, keepdims=True))
        dg_ref[...] += dg_acc

    row = pl.BlockSpec((tm, WIDTH), lambda i: (i, 0))
    wide_bf = jax.ShapeDtypeStruct((s_len, WIDTH), bf16)
    return pl.pallas_call(
        body, name="mix_bwd", grid=(s_len // tm,),
        in_specs=[pl.BlockSpec((tm, 2 * WIDTH), lambda i: (i, 0)), row, row,
                  pl.BlockSpec((tm, WIDTH), lambda i: (i, 3)), pl.BlockSpec((tm, WIDTH), lambda i: (i, 7)),
                  pl.BlockSpec((1, LANES), lambda i: (0, 0))],
        out_specs=[row, pl.BlockSpec((tm, LANES), lambda i: (i, 0)), row, row, row,
                   pl.BlockSpec((1, LANES), lambda i: (0, 0))],
        out_shape=[wide_bf, jax.ShapeDtypeStruct((s_len, LANES), f32), wide_bf, wide_bf,
                   jax.ShapeDtypeStruct((s_len, WIDTH), f32), jax.ShapeDtypeStruct((1, LANES), f32)],
        compiler_params=_params("arbitrary"),
    )(dmixed, fox_o, gdn_o, p_main, p_main, gnorm_g)


def _out_head(mixed, w_out, x, target, gate, final_g):
    s_len = x.shape[0]
    tm = 256

    def body(mx_ref, w_ref, x_ref, t_ref, gate_ref, fg_ref, loss_ref, dy_ref, dz_ref, dm_ref, dfg_ref, dgate_ref):
        @pl.when(pl.program_id(0) == 0)
        def _():
            loss_ref[...] = jnp.zeros_like(loss_ref)
            dfg_ref[...] = jnp.zeros_like(dfg_ref)
            dgate_ref[...] = jnp.zeros_like(dgate_ref)

        w = w_ref[...]
        z = jnp.dot(mx_ref[...], w, preferred_element_type=f32)
        gate_v, fg = gate_ref[...], fg_ref[...]
        y1 = x_ref[...] + gate_v * z
        r = lax.rsqrt(jnp.mean(y1 * y1, axis=-1, keepdims=True) + EPS)
        yn = y1 * r
        err = yn * fg - t_ref[...]
        loss_ref[...] += 0.5 * jnp.sum(jnp.mean(err * err, axis=-1, keepdims=True))
        dout = err * (1.0 / D_MODEL)
        dfg_ref[...] += jnp.sum(dout * yn, axis=0, keepdims=True)
        tt = dout * fg
        dy1 = r * (tt - yn * jnp.mean(tt * yn, axis=-1, keepdims=True))
        dy_ref[...] = dy1
        dgate_ref[...] += jnp.sum(dy1 * z, axis=0, keepdims=True)
        dz = (dy1 * gate_v).astype(bf16)
        dz_ref[...] = dz
        dm_ref[...] = _dg(dz, w, 1, 1)

    row = pl.BlockSpec((tm, D_MODEL), lambda i: (i, 0))
    vec = pl.BlockSpec((1, D_MODEL), lambda i: (0, 0))
    big = jax.ShapeDtypeStruct((s_len, D_MODEL), f32)
    return pl.pallas_call(
        body, name="out_head", grid=(s_len // tm,),
        in_specs=[row, pl.BlockSpec((D_MODEL, D_MODEL), lambda i: (0, 0)), row, row, vec, vec],
        out_specs=[pl.BlockSpec((1, LANES), lambda i: (0, 0)), row, row, row, vec, vec],
        out_shape=[jax.ShapeDtypeStruct((1, LANES), f32), big, jax.ShapeDtypeStruct((s_len, D_MODEL), bf16), big,
                   jax.ShapeDtypeStruct((1, D_MODEL), f32), jax.ShapeDtypeStruct((1, D_MODEL), f32)],
        compiler_params=_params("arbitrary"),
    )(mixed, w_out, x, target, gate, final_g)


def _matmul_tn(name, a, b, out_dtype):
    k_len, m_len = a.shape
    n_len = b.shape[1]
    tk, tm, tn = min(2048, k_len), min(1024, m_len), min(1024, n_len)
    nk = k_len // tk

    def body(a_ref, b_ref, o_ref, acc_sc):
        k = pl.program_id(2)

        @pl.when(k == 0)
        def _():
            acc_sc[...] = jnp.zeros_like(acc_sc)

        acc_sc[...] += _dg(a_ref[...], b_ref[...], 0, 0)

        @pl.when(k == nk - 1)
        def _():
            o_ref[...] = acc_sc[...].astype(out_dtype)

    return pl.pallas_call(
        body, name=name, grid=(m_len // tm, n_len // tn, nk),
        in_specs=[pl.BlockSpec((tk, tm), lambda i, j, k: (k, i)), pl.BlockSpec((tk, tn), lambda i, j, k: (k, j))],
        out_specs=pl.BlockSpec((tm, tn), lambda i, j, k: (i, j)),
        out_shape=jax.ShapeDtypeStruct((m_len, n_len), out_dtype),
        scratch_shapes=[pltpu.VMEM((tm, tn), f32)],
        compiler_params=_params("parallel", "parallel", "arbitrary"),
    )(a, b)


def _post1(p_main, p_small, qn_g, kn_g, conv_w, bvec, alog, dqs, dkn, dgq, dgk, dgv, d_small, df, df_query):
    s_len = p_main.shape[0]
    tm = PREP_TM
    nb = s_len // tm

    def body(fq_ref, fk_ref, gq_ref, gk_ref, gv_ref, hq_ref, hk_ref, hv_ref, ps_ref, qg_ref, kg_ref, cw_ref, bv_ref,
             al_ref, dqs_ref, dkn_ref, dgq_ref, dgk_ref, dgv_ref, dsm_ref, df_ref, dfq_in_ref,
             dfq_ref, dfk_ref, dconv_ref, dps_ref, dqg_ref, dkg_ref, sums_ref, xe_sc, carry_sc):
        step = pl.program_id(0)
        blk = nb - 1 - step

        @pl.when(step == 0)
        def _():
            carry_sc[...] = jnp.zeros_like(carry_sc)
            dqg_ref[...] = jnp.zeros_like(dqg_ref)
            dkg_ref[...] = jnp.zeros_like(dkg_ref)
            sums_ref[...] = jnp.zeros_like(sums_ref)

        for x_ref, g_ref, dy_ref, o_ref, acc_ref, mul in ((fq_ref, qg_ref, dqs_ref, dfq_ref, dqg_ref, QK_SCALE),
                                                          (fk_ref, kg_ref, dkn_ref, dfk_ref, dkg_ref, 1.0)):
            gain = g_ref[...]
            acc = jnp.zeros((1, HEAD_DIM), f32)
            for h in range(HEADS):
                sl = slice(h * HEAD_DIM, (h + 1) * HEAD_DIM)
                xv = x_ref[:, sl]
                r = lax.rsqrt(jnp.mean(xv * xv, axis=-1, keepdims=True) + EPS)
                xn = xv * r
                dy = dy_ref[:, sl] * mul
                acc = acc + jnp.sum(dy * xn, axis=0, keepdims=True)
                tt = dy * gain
                o_ref[:, sl] = (r * (tt - xn * jnp.mean(tt * xn, axis=-1, keepdims=True))).astype(bf16)
            acc_ref[...] += acc

        first = blk == 0
        for sec, (x_ref, halo_ref, dy_ref) in enumerate(((gq_ref, hq_ref, dgq_ref), (gk_ref, hk_ref, dgk_ref),
                                                         (gv_ref, hv_ref, dgv_ref))):
            xe_sc[0:HALO, :] = jnp.where(first, 0.0, halo_ref[...])
            xe_sc[HALO:, :] = x_ref[...]
            cv = _conv_section(xe_sc, cw_ref, slice(sec * WIDTH, (sec + 1) * WIDTH), tm)
            sgrad = _silu_grad(cv)
            if sec == 2:
                dconv_ref[:, sec * WIDTH:(sec + 1) * WIDTH] = dy_ref[...] * sgrad
            else:
                y = cv * _sigmoid(cv)
                mul = QK_SCALE if sec == 0 else 1.0
                for h in range(HEADS):
                    sl = slice(h * HEAD_DIM, (h + 1) * HEAD_DIM)
                    yh = y[:, sl]
                    r = lax.rsqrt(jnp.sum(yh * yh, axis=-1, keepdims=True) + EPS)
                    dqh = dy_ref[:, sl]
                    dyh = (mul * r) * (dqh - yh * (r * r) * jnp.sum(dqh * yh, axis=-1, keepdims=True))
                    dconv_ref[:, sec * WIDTH + h * HEAD_DIM:sec * WIDTH + (h + 1) * HEAD_DIM] = dyh * sgrad[:, sl]

        lane = _iota((tm, N_SMALL), 1)
        z, _, gval, beta = _small_fwd(ps_ref[...], bv_ref[...], al_ref[...])
        sig_z = _sigmoid(z)
        dsm = dsm_ref[...]
        in_g = (lane >= LANE_G) & (lane < LANE_G + HEADS)
        dgc = jnp.where(in_g, pltpu.roll(dsm, N_SMALL - (LANE_GC - LANE_G), 1), 0.0)
        dgl = jnp.where(in_g, pltpu.roll(dsm, N_SMALL - (LANE_GLAST - LANE_G), 1), 0.0)
        tri_c, ones_c = _chunk_masks(tm)
        dg = (_dg(tri_c, dgc, 0, 0, HI) + jnp.dot(ones_c, dgl, preferred_element_type=f32, precision=HI))
        dbeta = dsm
        dfb = jnp.where(lane < HEADS, df_ref[...], 0.0)
        for h in range(HEADS):
            dfb = dfb + jnp.where(lane == h, dfq_in_ref[h], 0.0)
        tri_u = (_iota((tm, tm), 1) >= _iota((tm, tm), 0)).astype(f32)
        dlogf = jnp.dot(tri_u, dfb, preferred_element_type=f32, precision=HI) + carry_sc[...]
        carry_sc[...] += jnp.sum(dfb, axis=0, keepdims=True)
        dff = dlogf * (1.0 - sig_z)
        dga = dg * (-jnp.exp(al_ref[...])) * sig_z
        dgb_small = dbeta * beta * (1.0 - beta)
        dps = jnp.where(lane < HEADS, dff, jnp.where(lane < 2 * HEADS, dga, jnp.where(lane < 3 * HEADS, dgb_small, 0.0)))
        dps_ref[...] = dps.astype(bf16)
        row = _iota((8, N_SMALL), 0)
        s0 = jnp.sum(dps, axis=0, keepdims=True)
        s1 = jnp.sum(jnp.where((lane >= HEADS) & (lane < 2 * HEADS), dg * gval, 0.0), axis=0, keepdims=True)
        sums_ref[...] += jnp.where(row == 0, s0, jnp.where(row == 1, s1, 0.0))

    def col(cb):
        return pl.BlockSpec((tm, WIDTH), lambda i: (nb - 1 - i, cb))

    def halo(cb):
        return pl.BlockSpec((HALO, WIDTH), lambda i: (jnp.maximum((nb - 1 - i) * (tm // HALO) - 1, 0), cb))

    vec = pl.BlockSpec((1, LANES), lambda i: (0, 0))
    row0 = pl.BlockSpec((tm, WIDTH), lambda i: (nb - 1 - i, 0))
    small = pl.BlockSpec((tm, N_SMALL), lambda i: (nb - 1 - i, 0))
    wide_bf = jax.ShapeDtypeStruct((s_len, WIDTH), bf16)
    return pl.pallas_call(
        body, name="post1", grid=(nb,),
        in_specs=[col(0), col(1), col(4), col(5), col(6), halo(4), halo(5), halo(6), small, vec, vec,
                  pl.BlockSpec((CONV_K, 3 * WIDTH), lambda i: (0, 0)), vec, vec,
                  row0, row0, row0, row0, row0, small, small,
                  pl.BlockSpec((HEADS, tm, 1), lambda i: (0, nb - 1 - i, 0))],
        out_specs=[row0, row0, pl.BlockSpec((tm, 3 * WIDTH), lambda i: (nb - 1 - i, 0)), small, vec, vec,
                   pl.BlockSpec((8, N_SMALL), lambda i: (0, 0))],
        out_shape=[wide_bf, wide_bf, jax.ShapeDtypeStruct((s_len, 3 * WIDTH), f32),
                   jax.ShapeDtypeStruct((s_len, N_SMALL), bf16), jax.ShapeDtypeStruct((1, LANES), f32),
                   jax.ShapeDtypeStruct((1, LANES), f32), jax.ShapeDtypeStruct((8, N_SMALL), f32)],
        scratch_shapes=[pltpu.VMEM((tm + HALO, WIDTH), f32), pltpu.VMEM((1, N_SMALL), f32)],
        compiler_params=_params("arbitrary"),
    )(p_main, p_main, p_main, p_main, p_main, p_main, p_main, p_main, p_small, qn_g, kn_g, conv_w, bvec, alog,
      dqs, dkn, dgq, dgk, dgv, d_small, df, df_query)


def _post2(p_main, dconv, conv_w):
    s_len = p_main.shape[0]
    tm = PREP_TM
    nb = s_len // tm

    def body(gq_ref, gk_ref, gv_ref, hq_ref, hk_ref, hv_ref, dc_ref, dnext_ref, cw_ref, dx_ref, dw_ref, xe_sc, de_sc):
        i = pl.program_id(0)

        @pl.when(i == 0)
        def _():
            dw_ref[...] = jnp.zeros_like(dw_ref)

        first, last = i == 0, i == nb - 1
        row = _iota((8, WIDTH), 0)
        for sec, (x_ref, halo_ref) in enumerate(((gq_ref, hq_ref), (gk_ref, hk_ref), (gv_ref, hv_ref))):
            cols = slice(sec * WIDTH, (sec + 1) * WIDTH)
            dc = dc_ref[:, cols]
            de_sc[0:tm, :] = dc
            de_sc[tm:, :] = jnp.where(last, 0.0, dnext_ref[:, cols])
            dx = cw_ref[pl.ds(CONV_K - 1, 1), cols] * dc
            for tap in range(CONV_K - 1):
                dx = dx + cw_ref[pl.ds(tap, 1), cols] * de_sc[pl.ds(CONV_K - 1 - tap, tm), :]
            dx_ref[:, cols] = dx.astype(bf16)
            xe_sc[0:HALO, :] = jnp.where(first, 0.0, halo_ref[...])
            xe_sc[HALO:, :] = x_ref[...]
            dw = jnp.zeros((8, WIDTH), f32)
            for tap in range(CONV_K):
                contrib = jnp.sum(dc * xe_sc[pl.ds(HALO - (CONV_K - 1) + tap, tm), :], axis=0, keepdims=True)
                dw = jnp.where(row == tap, contrib, dw)
            dw_ref[:, cols] += dw

    def col(cb):
        return pl.BlockSpec((tm, WIDTH), lambda i: (i, cb))

    def halo(cb):
        return pl.BlockSpec((HALO, WIDTH), lambda i: (jnp.maximum(i * (tm // HALO) - 1, 0), cb))

    return pl.pallas_call(
        body, name="post2", grid=(nb,),
        in_specs=[col(4), col(5), col(6), halo(4), halo(5), halo(6),
                  pl.BlockSpec((tm, 3 * WIDTH), lambda i: (i, 0)),
                  pl.BlockSpec((HALO, 3 * WIDTH), lambda i: (jnp.minimum((i + 1) * (tm // HALO), s_len // HALO - 1), 0)),
                  pl.BlockSpec((CONV_K, 3 * WIDTH), lambda i: (0, 0))],
        out_specs=[pl.BlockSpec((tm, 3 * WIDTH), lambda i: (i, 0)), pl.BlockSpec((8, 3 * WIDTH), lambda i: (0, 0))],
        out_shape=[jax.ShapeDtypeStruct((s_len, 3 * WIDTH), bf16), jax.ShapeDtypeStruct((8, 3 * WIDTH), f32)],
        scratch_shapes=[pltpu.VMEM((tm + HALO, WIDTH), f32), pltpu.VMEM((tm + HALO, WIDTH), f32)],
        compiler_params=_params("arbitrary"),
    )(p_main, p_main, p_main, p_main, p_main, p_main, dconv, dconv, conv_w)


def _in_proj_bwd(dp_main, dp_small, wt_main, wt_small, x, dy1, norm_g, scale1p):
    s_len = x.shape[0]
    tm, tk = 512, 1024
    nk = N_MAIN // tk

    def body(dp_ref, dps_ref, w_ref, ws_ref, x_ref, dy_ref, g_ref, sc_ref, dx_ref, dsh_ref, dsc_ref, dg_ref, acc_sc):
        i, k = pl.program_id(0), pl.program_id(1)

        @pl.when((i == 0) & (k == 0))
        def _():
            dsh_ref[...] = jnp.zeros_like(dsh_ref)
            dsc_ref[...] = jnp.zeros_like(dsc_ref)
            dg_ref[...] = jnp.zeros_like(dg_ref)

        @pl.when(k == 0)
        def _():
            acc_sc[...] = jnp.dot(dps_ref[...], ws_ref[...], preferred_element_type=f32)

        acc_sc[...] += jnp.dot(dp_ref[...], w_ref[...], preferred_element_type=f32)

        @pl.when(k == nk - 1)
        def _():
            dh = acc_sc[...]
            xb = x_ref[...]
            r = lax.rsqrt(jnp.mean(xb * xb, axis=-1, keepdims=True) + EPS)
            xr = xb * r
            gain = g_ref[...]
            dsh_ref[...] += jnp.sum(dh, axis=0, keepdims=True)
            dsc_ref[...] += jnp.sum(dh * (xr * gain), axis=0, keepdims=True)
            dxn = dh * sc_ref[...]
            dg_ref[...] += jnp.sum(dxn * xr, axis=0, keepdims=True)
            tt = dxn * gain
            dx_ref[...] = r * (tt - xr * jnp.mean(tt * xr, axis=-1, keepdims=True)) + dy_ref[...]

    row = pl.BlockSpec((tm, D_MODEL), lambda i, k: (i, 0))
    vec = pl.BlockSpec((1, D_MODEL), lambda i, k: (0, 0))
    vshape = jax.ShapeDtypeStruct((1, D_MODEL), f32)
    return pl.pallas_call(
        body, name="in_proj_bwd", grid=(s_len // tm, nk),
        in_specs=[pl.BlockSpec((tm, tk), lambda i, k: (i, k)), pl.BlockSpec((tm, N_SMALL), lambda i, k: (i, 0)),
                  pl.BlockSpec((tk, D_MODEL), lambda i, k: (k, 0)), pl.BlockSpec((N_SMALL, D_MODEL), lambda i, k: (0, 0)),
                  row, row, vec, vec],
        out_specs=[row, vec, vec, vec],
        out_shape=[jax.ShapeDtypeStruct((s_len, D_MODEL), f32), vshape, vshape, vshape],
        scratch_shapes=[pltpu.VMEM((tm, D_MODEL), f32)],
        compiler_params=_params("arbitrary", "arbitrary"),
    )(dp_main, dp_small, wt_main, wt_small, x, dy1, norm_g, scale1p)


def _adamw(name, w, g_stack, m, v, tr, tc=None):
    n_stack, rows, cols = g_stack.shape
    tc = cols if tc is None else tc

    def body(w_ref, g_ref, m_ref, v_ref, go_ref, d_ref, mo_ref, vo_ref):
        g = g_ref[0].astype(f32)
        for k in range(1, n_stack):
            g = g + g_ref[k].astype(f32)
        go_ref[0] = g
        m_new = ADAM_B1 * m_ref[0] + (1.0 - ADAM_B1) * g
        v_new = ADAM_B2 * v_ref[0] + (1.0 - ADAM_B2) * (g * g)
        mo_ref[0] = m_new
        vo_ref[0] = v_new
        m_hat = m_new / (1.0 - ADAM_B1 ** ADAM_STEP)
        v_hat = v_new / (1.0 - ADAM_B2 ** ADAM_STEP)
        d_ref[0] = -ADAM_LR * (m_hat / (jnp.sqrt(v_hat) + ADAM_EPS) + ADAM_WD * w_ref[0])

    blk = pl.BlockSpec((1, tr, tc), lambda i, j: (0, i, j))
    shape = jax.ShapeDtypeStruct((1, rows, cols), f32)
    return pl.pallas_call(
        body, name=name, grid=(rows // tr, cols // tc),
        in_specs=[blk, pl.BlockSpec((n_stack, tr, tc), lambda i, j: (0, i, j)), blk, blk],
        out_specs=[blk] * 4, out_shape=[shape] * 4,
        compiler_params=_params("parallel", "parallel"),
    )(w, g_stack, m, v)


def _w_ada_grad(c_all_t, dmod_pad):
    def body(c_ref, d_ref, o_ref):
        cv = c_ref[...]
        o_ref[...] = jnp.dot(cv * _sigmoid(cv), d_ref[...], preferred_element_type=f32, precision=HI)

    return pl.pallas_call(body, name="w_ada_grad",
                          out_shape=jax.ShapeDtypeStruct((c_all_t.shape[0], dmod_pad.shape[1]), f32),
                          compiler_params=_params())(c_all_t, dmod_pad)


SMALL_NAMES = ("norm_g", "b_ada", "b_fgate", "fox_qn_g", "fox_kn_g", "gdn_A_log", "gdn_dt_bias", "gdn_norm_g", "final_g")
SMALL_SIZES = (D_MODEL, 3 * D_MODEL, HEADS, HEAD_DIM, HEAD_DIM, HEADS, HEADS, HEAD_DIM, D_MODEL)
SMALL_PACK = 10752


def _pack(vectors, total):
    flat = jnp.concatenate([t.reshape(-1) for t in vectors])
    return jnp.pad(flat, (0, total - flat.shape[0])).reshape(1, total)


def _lanes(*pieces):
    row = jnp.zeros((LANES,), f32)
    for off, vec in pieces:
        row = lax.dynamic_update_slice(row, vec.reshape(-1).astype(f32), (off,))
    return row.reshape(1, LANES)


def kernel(x, c, norm_g, w_ada, b_ada, w_in, b_fgate, fox_qn_g, fox_kn_g, gdn_conv_w, gdn_A_log, gdn_dt_bias, gdn_norm_g, w_out, final_g, loss_target, m_norm_g, m_w_ada, m_b_ada, m_w_in, m_b_fgate, m_fox_qn_g, m_fox_kn_g, m_gdn_conv_w, m_gdn_A_log, m_gdn_dt_bias, m_gdn_norm_g, m_w_out, m_final_g, v_norm_g, v_w_ada, v_b_ada, v_w_in, v_b_fgate, v_fox_qn_g, v_fox_kn_g, v_gdn_conv_w, v_gdn_A_log, v_gdn_dt_bias, v_gdn_norm_g, v_w_out, v_final_g):
    me = _my_index()
    s_len = x.shape[1]
    nq = s_len // FOX_T
    x2 = x.reshape(s_len, D_MODEL)
    tgt = loss_target.reshape(s_len, D_MODEL)
    ada_cols = w_ada.shape[2]
    in_cols = w_in.shape[2]
    conv_cols = gdn_conv_w.shape[2]

    (c_all,) = _exchange("gather_c", [c], scatter=False)
    c_all = c_all.reshape(N_DEV, D_MODEL)
    b_shard = lax.dynamic_slice(b_ada, (0, me * ada_cols), (1, ada_cols))
    mod_mine = _mod_shard(c_all, w_ada[0], b_shard)
    wt_shard = jnp.transpose(w_in[0])
    mod_all, wt_all, w_out_all, conv_all = _gather_two_level(
        "gather_weights", [mod_mine, wt_shard.astype(bf16), w_out[0].astype(bf16), gdn_conv_w[0]])
    mod = lax.dynamic_slice(mod_all, (0, me, 0), (N_DEV, 1, ada_cols)).reshape(1, 3 * D_MODEL)
    shift, scale, gate = mod[:, :D_MODEL], mod[:, D_MODEL:2 * D_MODEL], mod[:, 2 * D_MODEL:]
    scale1p = 1.0 + scale
    wt_full = wt_all.reshape(N_DEV * in_cols, D_MODEL)
    g0 = 4 * WIDTH + HEADS
    w_main = jnp.concatenate([wt_full[:4 * WIDTH], wt_full[g0:g0 + 4 * WIDTH]], axis=0)
    w_small = jnp.concatenate([wt_full[4 * WIDTH:g0], wt_full[g0 + 4 * WIDTH:],
                               jnp.zeros((N_SMALL - 3 * HEADS, D_MODEL), bf16)], axis=0)
    w_out_full = w_out_all.reshape(2 * WIDTH, D_MODEL)
    conv_full = jnp.transpose(conv_all, (1, 0, 2)).reshape(CONV_K, 3 * WIDTH)

    qn_g, kn_g, gn_g = fox_qn_g.reshape(1, LANES), fox_kn_g.reshape(1, LANES), gdn_norm_g.reshape(1, LANES)
    bvec = _lanes((0, b_fgate), (HEADS, gdn_dt_bias))
    alog = _lanes((HEADS, gdn_A_log))
    fg = final_g.reshape(1, D_MODEL)

    p_main, p_small, h_bf = _in_proj(x2, norm_g, scale1p, shift, w_main, w_small)
    qs, kn, vb, gq, gk, gv, small = _prep(p_main, p_small, qn_g, kn_g, conv_full, bvec, alog)
    f_row = jnp.transpose(small[:, :HEADS]).reshape(HEADS, nq, 1, FOX_T)
    fox_o, lse = _fox_fwd(qs, kn, vb, small, f_row)
    gu, gw, gqg, gkd, gattn, t_inv, eg_last = _gdn_intra_fwd(gq, gk, gv, small)
    gdn_o, states = _gdn_scan_fwd(gu, gw, gqg, gkd, gattn, eg_last)
    mixed = _mix_fwd(fox_o, gdn_o, p_main, gn_g)

    loss_row, dy1, dz, dmixed, d_final_g, d_gate = _out_head(mixed, w_out_full, x2, tgt, gate, fg)
    loss = lax.psum(loss_row[0, 0], AXES)
    dw_out = _matmul_tn("dw_out", mixed, dz, bf16)
    do_fox, delta, dfz, dgz, dgdn_o, d_gn_g = _mix_bwd(dmixed, fox_o, gdn_o, p_main, gn_g)
    dqs, dkn, dvf, df_key, df_query = _fox_bwd(qs, kn, vb, do_fox, small, lse, delta, f_row)
    du, dw, dqg, dkd, dattn, deg = _gdn_scan_bwd(gu, gw, gqg, gkd, gattn, eg_last, states, dgdn_o)
    dgq, dgk, dgv, d_small = _gdn_intra_bwd(gq, gk, gv, small, t_inv, du, dw, dqg, dkd, dattn, deg)
    df_small = jnp.pad(jnp.transpose(df_key.reshape(HEADS, s_len)), ((0, 0), (0, N_SMALL - HEADS)))
    dfq, dfk, dconv, dp_small, d_qn_g, d_kn_g, sums = _post1(
        p_main, p_small, qn_g, kn_g, conv_full, bvec, alog, dqs, dkn, dgq, dgk, dgv, d_small, df_small, df_query)
    dgqkv, d_conv = _post2(p_main, dconv, conv_full)
    dp_main = jnp.concatenate([dfq, dfk, dvf.astype(bf16), dfz, dgqkv, dgz], axis=1)
    grad_x, d_shift, d_scale, d_norm_g = _in_proj_bwd(dp_main, dp_small, w_main, w_small, x2, dy1, norm_g, scale1p)
    dw_main = _matmul_tn("dw_main", dp_main, h_bf, bf16)
    dw_small = _matmul_tn("dw_small", dp_small, h_bf, bf16)
    dw_in_full = jnp.concatenate([dw_main[:4 * WIDTH], dw_small[:HEADS], dw_main[4 * WIDTH:],
                                  dw_small[HEADS:3 * HEADS]], axis=0)
    dw_in_parts = dw_in_full.reshape(N_DEV, in_cols, D_MODEL)
    dw_out_parts = dw_out.reshape(N_DEV, w_out.shape[1], D_MODEL)

    dmod = jnp.concatenate([d_shift, d_scale, d_gate], axis=1)
    small_grads = _pack([d_norm_g, dmod, sums[0, :HEADS], d_qn_g, d_kn_g, sums[1, HEADS:2 * HEADS],
                         sums[0, HEADS:2 * HEADS], d_gn_g, d_final_g], SMALL_PACK)
    conv_grad = d_conv[:CONV_K]
    pair_in, pair_out = _pair_exchange("pair_grads", [dw_in_parts, dw_out_parts])
    dw_in_recv, dw_out_recv = _chip_exchange(
        "chip_grads", [_pair_sum("pair_sum_w_in", pair_in), _pair_sum("pair_sum_w_out", pair_out)])
    small_all, conv_all_g = _exchange("gather_small_grads", [small_grads, conv_grad], scatter=False)

    outs = {}
    to_t = lambda t: jnp.transpose(t, (0, 2, 1))
    outs["w_in"] = tuple(to_t(t) for t in _adamw("adamw_w_in", to_t(w_in), dw_in_recv, to_t(m_w_in), to_t(v_w_in),
                                                  in_cols, 256))
    outs["w_out"] = _adamw("adamw_w_out", w_out, dw_out_recv, m_w_out, v_w_out, 128)
    conv_mine = lax.dynamic_slice(jnp.transpose(conv_all_g.reshape(N_DEV, CONV_K, N_DEV, conv_cols), (0, 2, 1, 3)),
                                  (0, me, 0, 0), (N_DEV, 1, CONV_K, conv_cols)).reshape(N_DEV, CONV_K, conv_cols)
    outs["gdn_conv_w"] = _adamw("adamw_conv", gdn_conv_w, conv_mine, m_gdn_conv_w, v_gdn_conv_w, CONV_K)
    small_all = small_all.reshape(N_DEV, 1, SMALL_PACK)
    dmod_all = small_all[:, 0, D_MODEL:D_MODEL + 3 * D_MODEL]
    dmod_mine = lax.dynamic_slice(dmod_all, (0, me * ada_cols), (N_DEV, ada_cols))
    c_all_t = jnp.pad(jnp.transpose(c_all), ((0, 0), (0, LANES - N_DEV)))
    g_w_ada = _w_ada_grad(c_all_t, jnp.pad(dmod_mine, ((0, LANES - N_DEV), (0, 0))))
    outs["w_ada"] = _adamw("adamw_w_ada", w_ada, g_w_ada[None], m_w_ada, v_w_ada, 256)
    given = dict(norm_g=(norm_g, m_norm_g, v_norm_g), b_ada=(b_ada, m_b_ada, v_b_ada), b_fgate=(b_fgate, m_b_fgate, v_b_fgate),
                 fox_qn_g=(fox_qn_g, m_fox_qn_g, v_fox_qn_g), fox_kn_g=(fox_kn_g, m_fox_kn_g, v_fox_kn_g),
                 gdn_A_log=(gdn_A_log, m_gdn_A_log, v_gdn_A_log), gdn_dt_bias=(gdn_dt_bias, m_gdn_dt_bias, v_gdn_dt_bias),
                 gdn_norm_g=(gdn_norm_g, m_gdn_norm_g, v_gdn_norm_g), final_g=(final_g, m_final_g, v_final_g))
    w_pack = _pack([given[n][0] for n in SMALL_NAMES], SMALL_PACK)
    m_pack = _pack([given[n][1] for n in SMALL_NAMES], SMALL_PACK)
    v_pack = _pack([given[n][2] for n in SMALL_NAMES], SMALL_PACK)
    packed = _adamw("adamw_small", w_pack[None], small_all, m_pack[None], v_pack[None], 1)
    off = 0
    for n, size in zip(SMALL_NAMES, SMALL_SIZES):
        outs[n] = tuple(t[0, 0, off:off + size].reshape(given[n][0].shape) for t in packed)
        off += size

    order = ("norm_g", "w_ada", "b_ada", "w_in", "b_fgate", "fox_qn_g", "fox_kn_g", "gdn_conv_w", "gdn_A_log",
             "gdn_dt_bias", "gdn_norm_g", "w_out", "final_g")
    result = [loss, grad_x.reshape(x.shape)]
    for part in range(4):
        result += [outs[n][part] for n in order]
    return tuple(result)
```

```python
import math

import jax
import jax.numpy as jnp
from jax import lax
from jax.experimental import pallas as pl
from jax.experimental.pallas import tpu as pltpu

f32 = jnp.float32
bf16 = jnp.bfloat16
HI = lax.Precision.HIGHEST

N_DEV = 8
AXES = ("x", "y", "c")
D_MODEL = 2048
HEADS = 8
HEAD_DIM = 128
WIDTH = HEADS * HEAD_DIM
CHUNK = 64
CONV_K = 4
EPS = 1e-6
QK_SCALE = HEAD_DIM ** -0.5
N_MAIN = 8 * WIDTH
N_SMALL = 128
LANE_F, LANE_G, LANE_BETA, LANE_GC, LANE_GLAST = 0, 8, 16, 24, 32
IN_WIDTH = 8 * WIDTH + 3 * HEADS
LANES = 128
VMEM_LIMIT = 56 * 1024 * 1024

ADAM_LR, ADAM_B1, ADAM_B2, ADAM_EPS, ADAM_WD, ADAM_STEP = 0.001, 0.9, 0.999, 1e-08, 0.01, 10


def _params(*sem):
    return pltpu.CompilerParams(dimension_semantics=sem, vmem_limit_bytes=VMEM_LIMIT)


def _iota(shape, dim):
    return lax.broadcasted_iota(jnp.int32, shape, dim)


def _sigmoid(z):
    return 1.0 / (1.0 + jnp.exp(-z))


def _softplus_parts(z):
    t = jnp.log(1.0 + jnp.exp(-jnp.abs(z)))
    return jnp.minimum(z, 0.0) - t, jnp.maximum(z, 0.0) + t


def _dg(a, b, ca, cb, prec=None):
    if a.ndim == 3:
        dims = (((ca + 1,), (cb + 1,)), ((0,), (0,)))
    else:
        dims = (((ca,), (cb,)), ((), ()))
    return lax.dot_general(a, b, dims, preferred_element_type=f32, precision=prec)


def _dot_bf16(a, b, ca, cb):
    return _dg(a.astype(bf16), b.astype(bf16), ca, cb)


def _split_bf16(a):
    hi = a.astype(bf16)
    return hi, (a - hi.astype(f32)).astype(bf16)


def _dot_3pass(a, b, ca, cb):
    a_hi, a_lo = _split_bf16(a)
    b_hi, b_lo = _split_bf16(b)
    return _dg(a_hi, b_hi, ca, cb) + (_dg(a_hi, b_lo, ca, cb) + _dg(a_lo, b_hi, ca, cb))


def _make_mm(dot):
    def nn_(a, b):
        return dot(a, b, 1, 0)

    def nt_(a, b):
        return dot(a, b, 1, 1)

    def tn_(a, b):
        return dot(a, b, 0, 0)

    @jax.custom_vjp
    def nn(a, b):
        return nn_(a, b)

    @jax.custom_vjp
    def nt(a, b):
        return nt_(a, b)

    @jax.custom_vjp
    def tn(a, b):
        return tn_(a, b)

    nn.defvjp(lambda a, b: (nn_(a, b), (a, b)), lambda r, g: (nt_(g, r[1]), tn_(r[0], g)))
    nt.defvjp(lambda a, b: (nt_(a, b), (a, b)), lambda r, g: (nn_(g, r[1]), tn_(g, r[0])))
    tn.defvjp(lambda a, b: (tn_(a, b), (a, b)), lambda r, g: (nt_(r[1], g), nn_(r[0], g)))
    return (nn_, nt_, tn_), (nn, nt, tn)


_BF_PLAIN, _BF_VJP = _make_mm(_dot_bf16)
_X3_PLAIN, _X3_VJP = _make_mm(_dot_3pass)


def _inv_unit_lower_many(ms):
    c = CHUNK
    nn = _X3_PLAIN[0]
    eye = (_iota((c, c), 0) == _iota((c, c), 1)).astype(f32)
    top = _iota((2 * c, c), 0) < c
    xs = [jnp.concatenate([eye - m, nn(m, m)], axis=0) for m in ms]
    for _ in range(int(math.log2(CHUNK)) - 2):
        xs = [jnp.where(top, x, 0.0) + nn(x, x[c:]) for x in xs]
    return [x[:c] + nn(x[:c], x[c:]) for x in xs]


@jax.custom_vjp
def _inv_given(m, t):
    return t


_inv_given.defvjp(lambda m, t: (t, t),
                  lambda t, g: (-_X3_PLAIN[1](_X3_PLAIN[2](t, g), t), jnp.zeros_like(t)))

SUBLANES = 8


def _gdn_intra_pre(q, k, v, gc_b, g_last_b, beta_b, bnt):
    c = CHUNK
    r_i, c_i = _iota((c, c), 0), _iota((c, c), 1)
    lower, strict = r_i >= c_i, r_i > c_i
    gc_i = gc_b[..., :c]
    gc_j = jnp.swapaxes(gc_i, -1, -2)
    decay = jnp.where(lower, jnp.exp(jnp.where(lower, gc_i - gc_j, 0.0)), 0.0)
    kb = k * beta_b
    both = bnt(jnp.concatenate([kb, q], axis=-2), k)
    m = jnp.where(strict, both[..., :c, :] * decay, 0.0)
    attn = jnp.where(lower, both[..., c:, :] * decay, 0.0)
    eg = jnp.exp(gc_b)
    rhs = jnp.concatenate([v * beta_b, kb * eg], axis=-1)
    k_dec = k * jnp.exp(g_last_b - gc_b)
    eg_last = jnp.exp(g_last_b[..., :SUBLANES, :])
    return m, rhs, q * eg, k_dec, attn, eg_last


def _gdn_intra(q, k, v, gc_b, g_last_b, beta_b, t_known):
    m, rhs, qg, k_dec, attn, eg_last = _gdn_intra_pre(q, k, v, gc_b, g_last_b, beta_b, _BF_VJP[1])
    return _X3_VJP[0](_inv_given(m, t_known), rhs), qg, k_dec, attn, eg_last


def _scale_rows(s, eg_last):
    return (s.reshape(HEAD_DIM // SUBLANES, SUBLANES, HEAD_DIM) * eg_last[None]).reshape(HEAD_DIM, HEAD_DIM)


def _my_index():
    return 4 * lax.axis_index("x") + 2 * lax.axis_index("y") + lax.axis_index("c")


def _peer(d):
    x, y, c = lax.axis_index("x"), lax.axis_index("y"), lax.axis_index("c")
    px, py, pc = (x + (d >> 2)) % 2, (y + ((d >> 1) & 1)) % 2, (c + (d & 1)) % 2
    return (px, py, pc), 4 * px + 2 * py + pc


def _exchange(name, arrays, scatter):
    n = len(arrays)

    def body(*refs):
        srcs, dsts = refs[:n], refs[n:2 * n]
        send_sems, recv_sems, local_sems = refs[2 * n:]
        me = _my_index()

        def remote(k, d):
            peer, pidx = _peer(d)
            src = srcs[k].at[pidx] if scatter else srcs[k]
            return pltpu.make_async_remote_copy(
                src_ref=src, dst_ref=dsts[k].at[me], send_sem=send_sems.at[k * 7 + d - 1],
                recv_sem=recv_sems.at[k * 7 + d - 1], device_id=peer, device_id_type=pl.DeviceIdType.MESH)

        def arrival(k, d):
            peer, pidx = _peer(d)
            src = srcs[k].at[pidx] if scatter else srcs[k]
            return pltpu.make_async_remote_copy(
                src_ref=src, dst_ref=dsts[k].at[pidx], send_sem=send_sems.at[k * 7 + d - 1],
                recv_sem=recv_sems.at[k * 7 + d - 1], device_id=peer, device_id_type=pl.DeviceIdType.MESH)

        local = [pltpu.make_async_copy(srcs[k].at[me] if scatter else srcs[k], dsts[k].at[me], local_sems.at[k])
                 for k in range(n)]
        sends = [remote(k, d) for k in range(n) for d in range(1, N_DEV)]
        for cp in local + sends:
            cp.start()
        for k in range(n):
            for d in range(1, N_DEV):
                arrival(k, d).wait_recv()
        for cp in sends:
            cp.wait_send()
        for cp in local:
            cp.wait()

    if scatter:
        out_shape = [jax.ShapeDtypeStruct(a.shape, a.dtype) for a in arrays]
    else:
        out_shape = [jax.ShapeDtypeStruct((N_DEV,) + a.shape, a.dtype) for a in arrays]
    any_spec = pl.BlockSpec(memory_space=pl.ANY)
    return pl.pallas_call(
        body, name=name, out_shape=out_shape, in_specs=[any_spec] * n, out_specs=[any_spec] * n,
        scratch_shapes=[pltpu.SemaphoreType.DMA((7 * n,)), pltpu.SemaphoreType.DMA((7 * n,)),
                        pltpu.SemaphoreType.DMA((n,))],
        compiler_params=pltpu.CompilerParams(has_side_effects=True),
    )(*arrays)


N_CHIPS = 4


def _pair_exchange(name, arrays):
    n = len(arrays)

    def body(*refs):
        srcs, dsts = refs[:n], refs[n:2 * n]
        send_sems, recv_sems = refs[2 * n:]
        x, y, c = lax.axis_index("x"), lax.axis_index("y"), lax.axis_index("c")
        sibling = (x, y, 1 - c)

        def copy(k, j):
            return pltpu.make_async_remote_copy(
                src_ref=srcs[k].at[2 * j + (1 - c)], dst_ref=dsts[k].at[j], send_sem=send_sems.at[k * N_CHIPS + j],
                recv_sem=recv_sems.at[k * N_CHIPS + j], device_id=sibling, device_id_type=pl.DeviceIdType.MESH)

        copies = [copy(k, j) for k in range(n) for j in range(N_CHIPS)]
        for cp in copies:
            cp.start()
        for cp in copies:
            cp.wait_recv()
        for cp in copies:
            cp.wait_send()

    any_spec = pl.BlockSpec(memory_space=pl.ANY)
    return pl.pallas_call(
        body, name=name, out_shape=[jax.ShapeDtypeStruct((N_CHIPS,) + a.shape[1:], a.dtype) for a in arrays],
        in_specs=[any_spec] * n, out_specs=[any_spec] * n,
        scratch_shapes=[pltpu.SemaphoreType.DMA((N_CHIPS * n,)), pltpu.SemaphoreType.DMA((N_CHIPS * n,))],
        compiler_params=pltpu.CompilerParams(has_side_effects=True),
    )(*arrays)


def _chip_exchange(name, arrays):
    n = len(arrays)

    def body(*refs):
        srcs, dsts = refs[:n], refs[n:2 * n]
        send_sems, recv_sems, local_sems = refs[2 * n:]
        x, y, c = lax.axis_index("x"), lax.axis_index("y"), lax.axis_index("c")
        my_chip = 2 * x + y

        def peer(d):
            px, py = (x + (d >> 1)) % 2, (y + (d & 1)) % 2
            return (px, py, c), 2 * px + py

        def remote(k, d, started):
            to, chip = peer(d)
            return pltpu.make_async_remote_copy(
                src_ref=srcs[k].at[chip], dst_ref=dsts[k].at[my_chip if started else chip],
                send_sem=send_sems.at[k * 3 + d - 1], recv_sem=recv_sems.at[k * 3 + d - 1],
                device_id=to, device_id_type=pl.DeviceIdType.MESH)

        local = [pltpu.make_async_copy(srcs[k].at[my_chip], dsts[k].at[my_chip], local_sems.at[k]) for k in range(n)]
        sends = [remote(k, d, True) for k in range(n) for d in range(1, N_CHIPS)]
        for cp in local + sends:
            cp.start()
        for k in range(n):
            for d in range(1, N_CHIPS):
                remote(k, d, False).wait_recv()
        for cp in sends:
            cp.wait_send()
        for cp in local:
            cp.wait()

    any_spec = pl.BlockSpec(memory_space=pl.ANY)
    return pl.pallas_call(
        body, name=name, out_shape=[jax.ShapeDtypeStruct(a.shape, a.dtype) for a in arrays],
        in_specs=[any_spec] * n, out_specs=[any_spec] * n,
        scratch_shapes=[pltpu.SemaphoreType.DMA((3 * n,)), pltpu.SemaphoreType.DMA((3 * n,)),
                        pltpu.SemaphoreType.DMA((n,))],
        compiler_params=pltpu.CompilerParams(has_side_effects=True),
    )(*arrays)


def _pair_sum(name, parts, received, core):
    n_blocks, rows, cols = received.shape
    tr = rows if rows % 256 else 256

    def body(core_ref, mine_ref, recv_ref, o_ref):
        o_ref[...] = (mine_ref[...].astype(f32) + recv_ref[...].astype(f32)).astype(bf16)

    return pl.pallas_call(
        body, name=name,
        grid_spec=pltpu.PrefetchScalarGridSpec(
            num_scalar_prefetch=1, grid=(n_blocks, rows // tr),
            in_specs=[pl.BlockSpec((1, tr, cols), lambda j, i, core_ref: (2 * j + core_ref[0], i, 0)),
                      pl.BlockSpec((1, tr, cols), lambda j, i, core_ref: (j, i, 0))],
            out_specs=pl.BlockSpec((1, tr, cols), lambda j, i, core_ref: (j, i, 0))),
        out_shape=jax.ShapeDtypeStruct((n_blocks, rows, cols), bf16),
        compiler_params=_params("parallel", "parallel"),
    )(core, parts, received)


def _gather_two_level(name, arrays):
    n = len(arrays)

    def body(*refs):
        srcs, dsts = refs[:n], refs[n:2 * n]
        send_sems, recv_sems, local_sems = refs[2 * n:]
        x, y, c = lax.axis_index("x"), lax.axis_index("y"), lax.axis_index("c")
        sibling = (x, y, 1 - c)
        chips = [((x + 1) % 2, y), (x, (y + 1) % 2), ((x + 1) % 2, (y + 1) % 2)]

        def index(px, py, pc):
            return 4 * px + 2 * py + pc

        def copy(k, slot, block, to, src=None):
            return pltpu.make_async_remote_copy(
                src_ref=dsts[k].at[index(*block)] if src is None else src, dst_ref=dsts[k].at[index(*block)],
                send_sem=send_sems.at[k * 7 + slot], recv_sem=recv_sems.at[k * 7 + slot],
                device_id=to, device_id_type=pl.DeviceIdType.MESH)

        me = (x, y, c)
        local = [pltpu.make_async_copy(srcs[k], dsts[k].at[index(*me)], local_sems.at[k]) for k in range(n)]
        first = [copy(k, 0, me, sibling, src=srcs[k]) for k in range(n)]
        first += [copy(k, 1 + j, me, (*chip, c), src=srcs[k]) for j, chip in enumerate(chips) for k in range(n)]
        for cp in local + first:
            cp.start()
        passed = []
        for j, chip in enumerate(chips):
            for k in range(n):
                copy(k, 1 + j, (*chip, c), me).wait_recv()
                fwd = copy(k, 4 + j, (*chip, c), sibling)
                fwd.start()
                passed.append(fwd)
        for k in range(n):
            copy(k, 0, sibling, me).wait_recv()
            for j, chip in enumerate(chips):
                copy(k, 4 + j, (*chip, 1 - c), me).wait_recv()
        for cp in first + passed:
            cp.wait_send()
        for cp in local:
            cp.wait()

    any_spec = pl.BlockSpec(memory_space=pl.ANY)
    return pl.pallas_call(
        body, name=name, out_shape=[jax.ShapeDtypeStruct((N_DEV,) + a.shape, a.dtype) for a in arrays],
        in_specs=[any_spec] * n, out_specs=[any_spec] * n,
        scratch_shapes=[pltpu.SemaphoreType.DMA((7 * n,)), pltpu.SemaphoreType.DMA((7 * n,)),
                        pltpu.SemaphoreType.DMA((n,))],
        compiler_params=pltpu.CompilerParams(has_side_effects=True),
    )(*arrays)


def _mod_shard(c_all, w_ada, b_shard):
    def body(c_ref, w_ref, b_ref, o_ref):
        cv = c_ref[...]
        ca = cv * _sigmoid(cv)
        o_ref[...] = jnp.dot(ca.astype(bf16), w_ref[...].astype(bf16), preferred_element_type=f32) + b_ref[...]

    return pl.pallas_call(body, name="mod_shard", out_shape=jax.ShapeDtypeStruct((N_DEV, w_ada.shape[1]), f32),
                          compiler_params=_params())(c_all, w_ada, b_shard)


def _in_proj(x, norm_g, scale1p, shift, wt_main, wt_small):
    s_len = x.shape[0]
    tm, tn = 512, 1024

    def body(x_ref, g_ref, sc_ref, sh_ref, w_ref, ws_ref, p_ref, ps_ref, h_ref, h_sc):
        @pl.when(pl.program_id(1) == 0)
        def _():
            xb = x_ref[...]
            r = lax.rsqrt(jnp.mean(xb * xb, axis=-1, keepdims=True) + EPS)
            hb = ((xb * r * g_ref[...]) * sc_ref[...] + sh_ref[...]).astype(bf16)
            h_sc[...] = hb
            h_ref[...] = hb
            ps_ref[...] = _dg(hb, ws_ref[...], 1, 1)

        p_ref[...] = _dg(h_sc[...], w_ref[...], 1, 1)

    vec = pl.BlockSpec((1, D_MODEL), lambda i, j: (0, 0))
    return pl.pallas_call(
        body, name="in_proj", grid=(s_len // tm, N_MAIN // tn),
        in_specs=[pl.BlockSpec((tm, D_MODEL), lambda i, j: (i, 0)), vec, vec, vec,
                  pl.BlockSpec((tn, D_MODEL), lambda i, j: (j, 0)),
                  pl.BlockSpec((N_SMALL, D_MODEL), lambda i, j: (0, 0))],
        out_specs=[pl.BlockSpec((tm, tn), lambda i, j: (i, j)),
                   pl.BlockSpec((tm, N_SMALL), lambda i, j: (i, 0)),
                   pl.BlockSpec((tm, D_MODEL), lambda i, j: (i, 0))],
        out_shape=[jax.ShapeDtypeStruct((s_len, N_MAIN), f32), jax.ShapeDtypeStruct((s_len, N_SMALL), f32),
                   jax.ShapeDtypeStruct((s_len, D_MODEL), bf16)],
        scratch_shapes=[pltpu.VMEM((tm, D_MODEL), bf16)],
        compiler_params=_params("parallel", "arbitrary"),
    )(x, norm_g, scale1p, shift, wt_main, wt_small)


PREP_TM = 256
HALO = 8


def _conv_section(xe_ref, cw_ref, cols, tm):
    acc = cw_ref[pl.ds(CONV_K - 1, 1), cols] * xe_ref[pl.ds(HALO, tm), :]
    for tap in range(CONV_K - 1):
        acc = acc + cw_ref[pl.ds(tap, 1), cols] * xe_ref[pl.ds(HALO - (CONV_K - 1) + tap, tm), :]
    return acc


def _small_fwd(ps, bvec, alog):
    z = ps + bvec
    logsig, softp = _softplus_parts(z)
    gval = -jnp.exp(alog) * softp
    beta = _sigmoid(ps)
    return z, logsig, gval, beta


def _head_lane(block, lane):
    return jnp.sum(jnp.where(_iota(block.shape, 1) == lane, block, 0.0), axis=1, keepdims=True)


def _head_slab(block, lane):
    return jnp.broadcast_to(_head_lane(block, lane), block.shape)


def _chunk_masks(tm):
    r, c = _iota((tm, tm), 0), _iota((tm, tm), 1)
    same = (r // CHUNK) == (c // CHUNK)
    return (same & (r >= c)).astype(f32), same.astype(f32)


def _prep(p_main, p_small, qn_g, kn_g, conv_w, bvec, alog):
    s_len = p_main.shape[0]
    tm = PREP_TM
    nb = s_len // tm

    def body(fq_ref, fk_ref, fv_ref, gq_ref, gk_ref, gv_ref, hq_ref, hk_ref, hv_ref, ps_ref, qg_ref, kg_ref,
             cw_ref, bv_ref, al_ref,
             qs_ref, kn_ref, vb_ref, gqo_ref, gko_ref, gvo_ref, small_ref, xe_sc, carry_sc):
        i = pl.program_id(0)

        @pl.when(i == 0)
        def _():
            carry_sc[...] = jnp.zeros_like(carry_sc)

        qg, kg = qg_ref[...], kg_ref[...]
        for h in range(HEADS):
            sl = slice(h * HEAD_DIM, (h + 1) * HEAD_DIM)
            q = fq_ref[:, sl]
            rq = lax.rsqrt(jnp.mean(q * q, axis=-1, keepdims=True) + EPS)
            qs_ref[:, sl] = (q * rq * qg * QK_SCALE).astype(bf16)
            k = fk_ref[:, sl]
            rk = lax.rsqrt(jnp.mean(k * k, axis=-1, keepdims=True) + EPS)
            kn_ref[:, sl] = (k * rk * kg).astype(bf16)
        vb_ref[...] = fv_ref[...].astype(bf16)

        first = i == 0
        for sec, (x_ref, halo_ref, o_ref) in enumerate(((gq_ref, hq_ref, gqo_ref), (gk_ref, hk_ref, gko_ref),
                                                        (gv_ref, hv_ref, gvo_ref))):
            xe_sc[0:HALO, :] = jnp.where(first, 0.0, halo_ref[...])
            xe_sc[HALO:, :] = x_ref[...]
            cv = _conv_section(xe_sc, cw_ref, slice(sec * WIDTH, (sec + 1) * WIDTH), tm)
            y = cv * _sigmoid(cv)
            if sec == 2:
                o_ref[...] = y
            else:
                mul = QK_SCALE if sec == 0 else 1.0
                for h in range(HEADS):
                    sl = slice(h * HEAD_DIM, (h + 1) * HEAD_DIM)
                    yh = y[:, sl]
                    o_ref[:, sl] = yh * (lax.rsqrt(jnp.sum(yh * yh, axis=-1, keepdims=True) + EPS) * mul)

        lane = _iota((tm, N_SMALL), 1)
        _, logsig, gval, beta = _small_fwd(ps_ref[...], bv_ref[...], al_ref[...])
        lf = jnp.where(lane < HEADS, logsig, 0.0)
        tri = (_iota((tm, tm), 0) >= _iota((tm, tm), 1)).astype(f32)
        fcum = jnp.dot(tri, lf, preferred_element_type=f32, precision=HI) + carry_sc[...]
        carry_sc[...] += jnp.sum(lf, axis=0, keepdims=True)
        tri_c, ones_c = _chunk_masks(tm)
        g_lanes = jnp.where((lane >= LANE_G) & (lane < LANE_G + HEADS), gval, 0.0)
        gc = jnp.dot(tri_c, g_lanes, preferred_element_type=f32, precision=HI)
        g_last = jnp.dot(ones_c, g_lanes, preferred_element_type=f32, precision=HI)
        small = jnp.where(lane < LANE_G, fcum, jnp.where(lane < LANE_BETA, gval, jnp.where(lane < LANE_GC, beta, 0.0)))
        small_ref[...] = small + pltpu.roll(gc, LANE_GC - LANE_G, 1) + pltpu.roll(g_last, LANE_GLAST - LANE_G, 1)

    def col(cb):
        return pl.BlockSpec((tm, WIDTH), lambda i: (i, cb))

    def halo(cb):
        return pl.BlockSpec((HALO, WIDTH), lambda i: (jnp.maximum(i * (tm // HALO) - 1, 0), cb))

    vec = pl.BlockSpec((1, LANES), lambda i: (0, 0))
    wide_f32 = jax.ShapeDtypeStruct((s_len, WIDTH), f32)
    wide_bf = jax.ShapeDtypeStruct((s_len, WIDTH), bf16)
    out_col = pl.BlockSpec((tm, WIDTH), lambda i: (i, 0))
    return pl.pallas_call(
        body, name="prep", grid=(nb,),
        in_specs=[col(0), col(1), col(2), col(4), col(5), col(6), halo(4), halo(5), halo(6),
                  pl.BlockSpec((tm, N_SMALL), lambda i: (i, 0)), vec, vec,
                  pl.BlockSpec((CONV_K, 3 * WIDTH), lambda i: (0, 0)), vec, vec],
        out_specs=[out_col] * 6 + [pl.BlockSpec((tm, N_SMALL), lambda i: (i, 0))],
        out_shape=[wide_bf, wide_bf, wide_bf, wide_f32, wide_f32, wide_f32,
                   jax.ShapeDtypeStruct((s_len, N_SMALL), f32)],
        scratch_shapes=[pltpu.VMEM((tm + HALO, WIDTH), f32), pltpu.VMEM((1, N_SMALL), f32)],
        compiler_params=_params("arbitrary"),
    )(p_main, p_main, p_main, p_main, p_main, p_main, p_main, p_main, p_main, p_small, qn_g, kn_g, conv_w, bvec, alog)


FOX_T = 1024
NEG_BIG = -1e30


def _fox_fwd(qs, kn, vb, small, f_row):
    s_len = qs.shape[0]
    t = FOX_T
    nq = s_len // t

    def body(q_ref, k_ref, v_ref, sm_ref, fr_ref, o_ref, lse_ref):
        qi = pl.program_id(1)
        q = q_ref[...]
        fq = _head_lane(sm_ref[...], pl.program_id(0))
        causal = _iota((t, t), 0) >= _iota((t, t), 1)

        def step(j, carry, masked):
            m, l, acc = carry
            rows = pl.ds(pl.multiple_of(j * t, t), t)
            s = _dg(q, k_ref[rows, :], 1, 1) + (fq - fr_ref[0, j])
            if masked:
                s = jnp.where(causal, s, NEG_BIG)
            m_new = jnp.maximum(m, jnp.max(s, axis=-1, keepdims=True))
            p = jnp.exp(s - m_new)
            alpha = jnp.exp(m - m_new)
            l = alpha * l + jnp.sum(p, axis=-1, keepdims=True)
            acc = alpha * acc + jnp.dot(p.astype(bf16), v_ref[rows, :], preferred_element_type=f32)
            return m_new, l, acc

        init = (jnp.full((t, 1), NEG_BIG, f32), jnp.zeros((t, 1), f32), jnp.zeros((t, HEAD_DIM), f32))
        carry = lax.fori_loop(0, qi, lambda j, c: step(j, c, False), init)
        m, l, acc = step(qi, carry, True)
        o_ref[...] = acc / l
        lse_ref[0] = m + jnp.log(l)

    return pl.pallas_call(
        body, name="fox_fwd", grid=(HEADS, nq),
        in_specs=[pl.BlockSpec((t, HEAD_DIM), lambda h, i: (i, h)),
                  pl.BlockSpec((s_len, HEAD_DIM), lambda h, i: (0, h)),
                  pl.BlockSpec((s_len, HEAD_DIM), lambda h, i: (0, h)),
                  pl.BlockSpec((t, N_SMALL), lambda h, i: (i, 0)),
                  pl.BlockSpec((1, nq, 1, t), lambda h, i: (h, 0, 0, 0))],
        out_specs=[pl.BlockSpec((t, HEAD_DIM), lambda h, i: (i, h)),
                   pl.BlockSpec((1, t, 1), lambda h, i: (h, i, 0))],
        out_shape=[jax.ShapeDtypeStruct((s_len, WIDTH), f32), jax.ShapeDtypeStruct((HEADS, s_len, 1), f32)],
        compiler_params=_params("parallel", "arbitrary"),
    )(qs, kn, vb, small, f_row)


def _fox_bwd(qs, kn, vb, do, small, lse, delta, f_row):
    s_len = qs.shape[0]
    t = FOX_T
    nq = s_len // t

    def body(q_ref, do_ref, sm_ref, lse_ref, dl_ref, k_ref, v_ref, fr_ref, dq_ref, dk_ref, dv_ref, df_ref, dfq_ref):
        head, qi = pl.program_id(0), pl.program_id(1)

        @pl.when(qi == 0)
        def _():
            dk_ref[...] = jnp.zeros_like(dk_ref)
            dv_ref[...] = jnp.zeros_like(dv_ref)
            df_ref[...] = jnp.zeros_like(df_ref)

        q, do_b = q_ref[...], do_ref[...]
        a = _head_lane(sm_ref[...], head) - lse_ref[0]
        dl = _head_lane(dl_ref[...], head)
        causal = _iota((t, t), 0) >= _iota((t, t), 1)

        def step(j, carry, masked):
            dq, row_sum = carry
            rows = pl.ds(pl.multiple_of(j * t, t), t)
            kj, vj = k_ref[rows, :], v_ref[rows, :]
            p = jnp.exp(_dg(q, kj, 1, 1) + (a - fr_ref[0, j]))
            if masked:
                p = jnp.where(causal, p, 0.0)
            ds = p * (_dg(do_b, vj, 1, 1) - dl)
            ds_b = ds.astype(bf16)
            dk_ref[rows, :] += _dg(ds_b, q, 0, 0)
            dv_ref[rows, :] += _dg(p.astype(bf16), do_b, 0, 0)
            df_ref[0, j] += -jnp.sum(ds, axis=0, keepdims=True)
            return dq + jnp.dot(ds_b, kj, preferred_element_type=f32), row_sum + jnp.sum(ds, axis=-1, keepdims=True)

        carry = lax.fori_loop(0, qi, lambda j, c: step(j, c, False),
                              (jnp.zeros((t, HEAD_DIM), f32), jnp.zeros((t, 1), f32)))
        dq, row_sum = step(qi, carry, True)
        dq_ref[...] = dq
        dfq_ref[0] = row_sum

    blk = pl.BlockSpec((t, HEAD_DIM), lambda h, i: (i, h))
    full = pl.BlockSpec((s_len, HEAD_DIM), lambda h, i: (0, h))
    colv = pl.BlockSpec((1, t, 1), lambda h, i: (h, i, 0))
    rowv = pl.BlockSpec((1, nq, 1, t), lambda h, i: (h, 0, 0, 0))
    lanes = pl.BlockSpec((t, N_SMALL), lambda h, i: (i, 0))
    wide = jax.ShapeDtypeStruct((s_len, WIDTH), f32)
    return pl.pallas_call(
        body, name="fox_bwd", grid=(HEADS, nq),
        in_specs=[blk, blk, lanes, colv, lanes, full, full, rowv],
        out_specs=[blk, full, full, rowv, colv],
        out_shape=[wide, wide, wide, jax.ShapeDtypeStruct((HEADS, nq, 1, t), f32),
                   jax.ShapeDtypeStruct((HEADS, s_len, 1), f32)],
        compiler_params=_params("parallel", "arbitrary"),
    )(qs, do, small, lse, delta, kn, vb, f_row)


INTRA_CHUNKS = 8
SCAN_FWD_CHUNKS = 8
SCAN_BWD_CHUNKS = 4


def _gdn_intra_fwd(gq, gk, gv, small):
    s_len = gq.shape[0]
    cpb = INTRA_CHUNKS
    rows_blk = cpb * CHUNK
    n_chunks = s_len // CHUNK

    def body(q_ref, k_ref, v_ref, sm_ref, u_ref, w_ref, qg_ref, kd_ref, attn_ref, t_ref, eg_ref):
        head = pl.program_id(0)
        sm = sm_ref[...]
        gc_b, gl_b, beta_b = (_head_slab(sm, LANE_GC + head), _head_slab(sm, LANE_GLAST + head),
                              _head_slab(sm, LANE_BETA + head))
        ms, rhss = [], []
        for ci in range(cpb):
            rows = pl.ds(ci * CHUNK, CHUNK)
            sl = slice(ci * CHUNK, (ci + 1) * CHUNK)
            m, rhs, qg, kd, attn, eg_last = _gdn_intra_pre(q_ref[rows, :], k_ref[rows, :], v_ref[rows, :],
                                                           gc_b[sl], gl_b[sl], beta_b[sl], _BF_PLAIN[1])
            qg_ref[rows, :] = qg.astype(bf16)
            kd_ref[rows, :] = kd.astype(bf16)
            attn_ref[0, ci] = attn.astype(bf16)
            eg_ref[0, ci] = eg_last
            ms.append(m)
            rhss.append(rhs)
        for ci, (t, rhs) in enumerate(zip(_inv_unit_lower_many(ms), rhss)):
            rows = pl.ds(ci * CHUNK, CHUNK)
            t_ref[0, ci] = t
            uw = _X3_PLAIN[0](t, rhs)
            u_ref[rows, :] = uw[:, :HEAD_DIM]
            w_ref[rows, :] = uw[:, HEAD_DIM:].astype(bf16)

    blk = pl.BlockSpec((rows_blk, HEAD_DIM), lambda h, i: (i, h))
    sq = pl.BlockSpec((1, cpb, CHUNK, CHUNK), lambda h, i: (h, i, 0, 0))
    wide_bf = jax.ShapeDtypeStruct((s_len, WIDTH), bf16)
    return pl.pallas_call(
        body, name="gdn_intra_fwd", grid=(HEADS, s_len // rows_blk),
        in_specs=[blk] * 3 + [pl.BlockSpec((rows_blk, N_SMALL), lambda h, i: (i, 0))],
        out_specs=[blk] * 4 + [sq, sq, pl.BlockSpec((1, cpb, SUBLANES, HEAD_DIM), lambda h, i: (h, i, 0, 0))],
        out_shape=[jax.ShapeDtypeStruct((s_len, WIDTH), f32), wide_bf, wide_bf, wide_bf,
                   jax.ShapeDtypeStruct((HEADS, n_chunks, CHUNK, CHUNK), bf16),
                   jax.ShapeDtypeStruct((HEADS, n_chunks, CHUNK, CHUNK), f32),
                   jax.ShapeDtypeStruct((HEADS, n_chunks, SUBLANES, HEAD_DIM), f32)],
        compiler_params=_params("parallel", "parallel"),
    )(gq, gk, gv, small)


def _gdn_scan_fwd(u, w, qg, kd, attn, eg):
    s_len = u.shape[0]
    cpb = SCAN_FWD_CHUNKS
    rows_blk = cpb * CHUNK
    n_chunks = s_len // CHUNK

    def body(u_ref, w_ref, qg_ref, kd_ref, attn_ref, eg_ref, o_ref, st_ref, s_sc):
        @pl.when(pl.program_id(0) == 0)
        def _():
            s_sc[...] = jnp.zeros_like(s_sc)

        def chunk(ci, _):
            rows = pl.ds(pl.multiple_of(ci * CHUNK, CHUNK), CHUNK)
            cols = [slice(h * HEAD_DIM, (h + 1) * HEAD_DIM) for h in range(HEADS)]
            s0 = [s_sc[h] for h in range(HEADS)]
            s0_b = [s.astype(bf16) for s in s0]
            for h in range(HEADS):
                st_ref[h, ci] = s0[h]
            ws = [jnp.dot(w_ref[rows, cols[h]], s0_b[h], preferred_element_type=f32) for h in range(HEADS)]
            qs = [jnp.dot(qg_ref[rows, cols[h]], s0_b[h], preferred_element_type=f32) for h in range(HEADS)]
            vn_b = [(u_ref[rows, cols[h]] - ws[h]).astype(bf16) for h in range(HEADS)]
            av = [jnp.dot(attn_ref[h, ci], vn_b[h], preferred_element_type=f32) for h in range(HEADS)]
            kv = [_dg(kd_ref[rows, cols[h]], vn_b[h], 0, 0) for h in range(HEADS)]
            for h in range(HEADS):
                o_ref[rows, cols[h]] = qs[h] + av[h]
                s_sc[h] = _scale_rows(s0[h], eg_ref[h, ci]) + kv[h]
            return 0

        lax.fori_loop(0, cpb, chunk, 0)

    row = pl.BlockSpec((rows_blk, WIDTH), lambda i: (i, 0))
    return pl.pallas_call(
        body, name="gdn_scan_fwd", grid=(s_len // rows_blk,),
        in_specs=[row] * 4 + [pl.BlockSpec((HEADS, cpb, CHUNK, CHUNK), lambda i: (0, i, 0, 0)),
                              pl.BlockSpec((HEADS, cpb, SUBLANES, HEAD_DIM), lambda i: (0, i, 0, 0))],
        out_specs=[row, pl.BlockSpec((HEADS, cpb, HEAD_DIM, HEAD_DIM), lambda i: (0, i, 0, 0))],
        out_shape=[jax.ShapeDtypeStruct((s_len, WIDTH), f32),
                   jax.ShapeDtypeStruct((HEADS, n_chunks, HEAD_DIM, HEAD_DIM), f32)],
        scratch_shapes=[pltpu.VMEM((HEADS, HEAD_DIM, HEAD_DIM), f32)],
        compiler_params=_params("arbitrary"),
    )(u, w, qg, kd, attn, eg)


def _gdn_scan_bwd(u, w, qg, kd, attn, eg, states, d_o):
    s_len = u.shape[0]
    cpb = SCAN_BWD_CHUNKS
    rows_blk = cpb * CHUNK
    n_chunks = s_len // CHUNK
    nb = s_len // rows_blk

    def body(u_ref, w_ref, qg_ref, kd_ref, attn_ref, eg_ref, st_ref, do_ref,
             du_ref, dw_ref, dqg_ref, dkd_ref, dattn_ref, deg_ref, ds_sc):
        @pl.when(pl.program_id(0) == 0)
        def _():
            ds_sc[...] = jnp.zeros_like(ds_sc)

        def chunk(step, _):
            ci = cpb - 1 - step
            rows = pl.ds(pl.multiple_of(ci * CHUNK, CHUNK), CHUNK)
            hs = range(HEADS)
            cols = [slice(h * HEAD_DIM, (h + 1) * HEAD_DIM) for h in hs]
            s0 = [st_ref[h, ci] for h in hs]
            s0_b = [s.astype(bf16) for s in s0]
            ds1 = [ds_sc[h] for h in hs]
            ds1_b = [d.astype(bf16) for d in ds1]
            do_b = [do_ref[rows, cols[h]].astype(bf16) for h in hs]
            ws = [jnp.dot(w_ref[rows, cols[h]], s0_b[h], preferred_element_type=f32) for h in hs]
            ad = [_dg(attn_ref[h, ci], do_b[h], 0, 0) for h in hs]
            kd_ds = [jnp.dot(kd_ref[rows, cols[h]], ds1_b[h], preferred_element_type=f32) for h in hs]
            dqg = [_dg(do_b[h], s0_b[h], 1, 1) for h in hs]
            qd = [_dg(qg_ref[rows, cols[h]], do_b[h], 0, 0) for h in hs]
            vn_b = [(u_ref[rows, cols[h]] - ws[h]).astype(bf16) for h in hs]
            dvn = [ad[h] + kd_ds[h] for h in hs]
            dvn_b = [d.astype(bf16) for d in dvn]
            dattn = [_dg(do_b[h], vn_b[h], 1, 1) for h in hs]
            dkd = [_dg(vn_b[h], ds1_b[h], 1, 1) for h in hs]
            dw = [_dg(dvn_b[h], s0_b[h], 1, 1) for h in hs]
            wd = [_dg(w_ref[rows, cols[h]], dvn_b[h], 0, 0) for h in hs]
            for h in hs:
                dattn_ref[h, ci] = dattn[h]
                dqg_ref[rows, cols[h]] = dqg[h]
                dkd_ref[rows, cols[h]] = dkd[h]
                du_ref[rows, cols[h]] = dvn[h]
                dw_ref[rows, cols[h]] = -dw[h]
                ds_sc[h] = qd[h] - wd[h] + _scale_rows(ds1[h], eg_ref[h, ci])
                deg_ref[h, ci] = jnp.sum((ds1[h] * s0[h]).reshape(HEAD_DIM // SUBLANES, SUBLANES, HEAD_DIM), axis=0)
            return 0

        lax.fori_loop(0, cpb, chunk, 0)

    row = pl.BlockSpec((rows_blk, WIDTH), lambda i: (nb - 1 - i, 0))
    sq = pl.BlockSpec((HEADS, cpb, CHUNK, CHUNK), lambda i: (0, nb - 1 - i, 0, 0))
    egs = pl.BlockSpec((HEADS, cpb, SUBLANES, HEAD_DIM), lambda i: (0, nb - 1 - i, 0, 0))
    wide = jax.ShapeDtypeStruct((s_len, WIDTH), f32)
    return pl.pallas_call(
        body, name="gdn_scan_bwd", grid=(nb,),
        in_specs=[row] * 4 + [sq, egs, pl.BlockSpec((HEADS, cpb, HEAD_DIM, HEAD_DIM), lambda i: (0, nb - 1 - i, 0, 0)), row],
        out_specs=[row] * 4 + [sq, egs],
        out_shape=[wide] * 4 + [jax.ShapeDtypeStruct((HEADS, n_chunks, CHUNK, CHUNK), f32),
                                jax.ShapeDtypeStruct((HEADS, n_chunks, SUBLANES, HEAD_DIM), f32)],
        scratch_shapes=[pltpu.VMEM((HEADS, HEAD_DIM, HEAD_DIM), f32)],
        compiler_params=_params("arbitrary"),
    )(u, w, qg, kd, attn, eg, states, d_o)


def _gdn_intra_bwd(gq, gk, gv, small, t_inv, du, dw, dqg, dkd, dattn, deg):
    s_len = gq.shape[0]
    cpb = INTRA_CHUNKS
    rows_blk = cpb * CHUNK

    def body(q_ref, k_ref, v_ref, sm_ref, t_ref, du_ref, dw_ref, dqg_ref, dkd_ref, dattn_ref, deg_ref,
             dq_ref, dk_ref, dv_ref, dsm_ref):
        head = pl.program_id(1)

        def batch(value):
            return value.reshape(cpb, CHUNK, HEAD_DIM)

        sm = sm_ref[...]
        slabs = [batch(_head_slab(sm, first + head)) for first in (LANE_GC, LANE_GLAST, LANE_BETA)]
        t_known = t_ref[0]
        _, vjp = jax.vjp(lambda q, k, v, gc, gl, b: _gdn_intra(q, k, v, gc, gl, b, t_known),
                         batch(q_ref[...]), batch(k_ref[...]), batch(v_ref[...]), *slabs)
        duw = jnp.concatenate([batch(du_ref[...]), batch(dw_ref[...])], axis=-1)
        dq, dk, dv, dgc, dgl, db = vjp((duw, batch(dqg_ref[...]), batch(dkd_ref[...]), dattn_ref[0], deg_ref[0]))
        for ref, grad in zip((dq_ref, dk_ref, dv_ref), (dq, dk, dv)):
            ref[...] = grad.reshape(rows_blk, HEAD_DIM)

        @pl.when(head == 0)
        def _():
            dsm_ref[...] = jnp.zeros_like(dsm_ref)

        lane = _iota((rows_blk, N_SMALL), 1)
        acc = dsm_ref[...]
        for first, grad in ((LANE_GC, dgc), (LANE_GLAST, dgl), (LANE_BETA, db)):
            col = jnp.sum(grad.reshape(rows_blk, HEAD_DIM), axis=1, keepdims=True)
            acc = acc + jnp.where(lane == first + head, col, 0.0)
        dsm_ref[...] = acc

    blk = pl.BlockSpec((rows_blk, HEAD_DIM), lambda i, h: (i, h))
    sq = pl.BlockSpec((1, cpb, CHUNK, CHUNK), lambda i, h: (h, i, 0, 0))
    egs = pl.BlockSpec((1, cpb, SUBLANES, HEAD_DIM), lambda i, h: (h, i, 0, 0))
    lanes = pl.BlockSpec((rows_blk, N_SMALL), lambda i, h: (i, 0))
    wide = jax.ShapeDtypeStruct((s_len, WIDTH), f32)
    return pl.pallas_call(
        body, name="gdn_intra_bwd", grid=(s_len // rows_blk, HEADS),
        in_specs=[blk] * 3 + [lanes, sq] + [blk] * 4 + [sq, egs],
        out_specs=[blk] * 3 + [lanes],
        out_shape=[wide] * 3 + [jax.ShapeDtypeStruct((s_len, N_SMALL), f32)],
        compiler_params=_params("parallel", "arbitrary"),
    )(gq, gk, gv, small, t_inv, du, dw, dqg, dkd, dattn, deg)


MIX_TM = 256


def _mix_fwd(fox_o, gdn_o, p_main, gnorm_g):
    s_len = fox_o.shape[0]
    tm = MIX_TM

    def body(fo_ref, go_ref, fz_ref, gz_ref, g_ref, mixed_ref):
        fz = fz_ref[...]
        mixed_ref[:, 0:WIDTH] = (fo_ref[...] * (fz * _sigmoid(fz))).astype(bf16)
        gz = gz_ref[...]
        gate = gz * _sigmoid(gz)
        gg = g_ref[...]
        for h in range(HEADS):
            sl = slice(h * HEAD_DIM, (h + 1) * HEAD_DIM)
            o = go_ref[:, sl]
            r = lax.rsqrt(jnp.mean(o * o, axis=-1, keepdims=True) + EPS)
            mixed_ref[:, WIDTH + h * HEAD_DIM:WIDTH + (h + 1) * HEAD_DIM] = (o * r * gg * gate[:, sl]).astype(bf16)

    row = pl.BlockSpec((tm, WIDTH), lambda i: (i, 0))
    return pl.pallas_call(
        body, name="mix_fwd", grid=(s_len // tm,),
        in_specs=[row, row, pl.BlockSpec((tm, WIDTH), lambda i: (i, 3)), pl.BlockSpec((tm, WIDTH), lambda i: (i, 7)),
                  pl.BlockSpec((1, LANES), lambda i: (0, 0))],
        out_specs=pl.BlockSpec((tm, 2 * WIDTH), lambda i: (i, 0)),
        out_shape=jax.ShapeDtypeStruct((s_len, 2 * WIDTH), bf16),
        compiler_params=_params("parallel"),
    )(fox_o, gdn_o, p_main, p_main, gnorm_g)


def _silu_grad(z):
    sg = _sigmoid(z)
    return sg * (1.0 + z * (1.0 - sg))


def _mix_bwd(dmixed, fox_o, gdn_o, p_main, gnorm_g):
    s_len = fox_o.shape[0]
    tm = MIX_TM

    def body(dm_ref, fo_ref, go_ref, fz_ref, gz_ref, g_ref, dof_ref, delta_ref, dfz_ref, dgz_ref, dgo_ref, dg_ref):
        @pl.when(pl.program_id(0) == 0)
        def _():
            dg_ref[...] = jnp.zeros_like(dg_ref)

        lane = _iota((tm, LANES), 1)
        fz = fz_ref[...]
        dmf = dm_ref[:, 0:WIDTH]
        fo = fo_ref[...]
        dof = dmf * (fz * _sigmoid(fz))
        dof_ref[...] = dof.astype(bf16)
        dfz_ref[...] = (dmf * fo * _silu_grad(fz)).astype(bf16)
        prod = dof * fo
        delta = jnp.zeros((tm, LANES), f32)
        for h in range(HEADS):
            dh = jnp.sum(prod[:, h * HEAD_DIM:(h + 1) * HEAD_DIM], axis=-1, keepdims=True)
            delta = jnp.where(lane == h, dh, delta)
        delta_ref[...] = delta

        gz = gz_ref[...]
        dmg = dm_ref[:, WIDTH:2 * WIDTH]
        gate = gz * _sigmoid(gz)
        sgrad = _silu_grad(gz)
        gg = g_ref[...]
        dg_acc = jnp.zeros((1, HEAD_DIM), f32)
        for h in range(HEADS):
            sl = slice(h * HEAD_DIM, (h + 1) * HEAD_DIM)
            o = go_ref[:, sl]
            r = lax.rsqrt(jnp.mean(o * o, axis=-1, keepdims=True) + EPS)
            on = o * r
            dmh = dmg[:, sl]
            dgz_ref[:, sl] = (dmh * (on * gg) * sgrad[:, sl]).astype(bf16)
            dy = dmh * gate[:, sl]
            dg_acc = dg_acc + jnp.sum(dy * on, axis=0, keepdims=True)
            tt = dy * gg
            dgo_ref[:, sl] = r * (tt - on * jnp.mean(tt * on, axis=-1, keepdims=True))
        dg_ref[...] += dg_acc

    row = pl.BlockSpec((tm, WIDTH), lambda i: (i, 0))
    wide_bf = jax.ShapeDtypeStruct((s_len, WIDTH), bf16)
    return pl.pallas_call(
        body, name="mix_bwd", grid=(s_len // tm,),
        in_specs=[pl.BlockSpec((tm, 2 * WIDTH), lambda i: (i, 0)), row, row,
                  pl.BlockSpec((tm, WIDTH), lambda i: (i, 3)), pl.BlockSpec((tm, WIDTH), lambda i: (i, 7)),
                  pl.BlockSpec((1, LANES), lambda i: (0, 0))],
        out_specs=[row, pl.BlockSpec((tm, LANES), lambda i: (i, 0)), row, row, row,
                   pl.BlockSpec((1, LANES), lambda i: (0, 0))],
        out_shape=[wide_bf, jax.ShapeDtypeStruct((s_len, LANES), f32), wide_bf, wide_bf,
                   jax.ShapeDtypeStruct((s_len, WIDTH), f32), jax.ShapeDtypeStruct((1, LANES), f32)],
        compiler_params=_params("arbitrary"),
    )(dmixed, fox_o, gdn_o, p_main, p_main, gnorm_g)


def _out_head(mixed, w_out, x, target, gate, final_g):
    s_len = x.shape[0]
    tm = 256

    def body(mx_ref, w_ref, x_ref, t_ref, gate_ref, fg_ref, loss_ref, dy_ref, dz_ref, dm_ref, dfg_ref, dgate_ref):
        @pl.when(pl.program_id(0) == 0)
        def _():
            loss_ref[...] = jnp.zeros_like(loss_ref)
            dfg_ref[...] = jnp.zeros_like(dfg_ref)
            dgate_ref[...] = jnp.zeros_like(dgate_ref)

        w = w_ref[...]
        z = jnp.dot(mx_ref[...], w, preferred_element_type=f32)
        gate_v, fg = gate_ref[...], fg_ref[...]
        y1 = x_ref[...] + gate_v * z
        r = lax.rsqrt(jnp.mean(y1 * y1, axis=-1, keepdims=True) + EPS)
        yn = y1 * r
        err = yn * fg - t_ref[...]
        loss_ref[...] += 0.5 * jnp.sum(jnp.mean(err * err, axis=-1, keepdims=True))
        dout = err * (1.0 / D_MODEL)
        dfg_ref[...] += jnp.sum(dout * yn, axis=0, keepdims=True)
        tt = dout * fg
        dy1 = r * (tt - yn * jnp.mean(tt * yn, axis=-1, keepdims=True))
        dy_ref[...] = dy1
        dgate_ref[...] += jnp.sum(dy1 * z, axis=0, keepdims=True)
        dz = (dy1 * gate_v).astype(bf16)
        dz_ref[...] = dz
        dm_ref[...] = _dg(dz, w, 1, 1)

    row = pl.BlockSpec((tm, D_MODEL), lambda i: (i, 0))
    vec = pl.BlockSpec((1, D_MODEL), lambda i: (0, 0))
    big = jax.ShapeDtypeStruct((s_len, D_MODEL), f32)
    return pl.pallas_call(
        body, name="out_head", grid=(s_len // tm,),
        in_specs=[row, pl.BlockSpec((D_MODEL, D_MODEL), lambda i: (0, 0)), row, row, vec, vec],
        out_specs=[pl.BlockSpec((1, LANES), lambda i: (0, 0)), row, row, row, vec, vec],
        out_shape=[jax.ShapeDtypeStruct((1, LANES), f32), big, jax.ShapeDtypeStruct((s_len, D_MODEL), bf16), big,
                   jax.ShapeDtypeStruct((1, D_MODEL), f32), jax.ShapeDtypeStruct((1, D_MODEL), f32)],
        compiler_params=_params("arbitrary"),
    )(mixed, w_out, x, target, gate, final_g)


def _matmul_tn(name, a, b, out_dtype):
    k_len, m_len = a.shape
    n_len = b.shape[1]
    tk, tm, tn = min(2048, k_len), min(1024, m_len), min(1024, n_len)
    nk = k_len // tk

    def body(a_ref, b_ref, o_ref, acc_sc):
        k = pl.program_id(2)

        @pl.when(k == 0)
        def _():
            acc_sc[...] = jnp.zeros_like(acc_sc)

        acc_sc[...] += _dg(a_ref[...], b_ref[...], 0, 0)

        @pl.when(k == nk - 1)
        def _():
            o_ref[...] = acc_sc[...].astype(out_dtype)

    return pl.pallas_call(
        body, name=name, grid=(m_len // tm, n_len // tn, nk),
        in_specs=[pl.BlockSpec((tk, tm), lambda i, j, k: (k, i)), pl.BlockSpec((tk, tn), lambda i, j, k: (k, j))],
        out_specs=pl.BlockSpec((tm, tn), lambda i, j, k: (i, j)),
        out_shape=jax.ShapeDtypeStruct((m_len, n_len), out_dtype),
        scratch_shapes=[pltpu.VMEM((tm, tn), f32)],
        compiler_params=_params("parallel", "parallel", "arbitrary"),
    )(a, b)


def _post1(p_main, p_small, qn_g, kn_g, conv_w, bvec, alog, dqs, dkn, dgq, dgk, dgv, d_small, df, df_query):
    s_len = p_main.shape[0]
    tm = PREP_TM
    nb = s_len // tm

    def body(fq_ref, fk_ref, gq_ref, gk_ref, gv_ref, hq_ref, hk_ref, hv_ref, ps_ref, qg_ref, kg_ref, cw_ref, bv_ref,
             al_ref, dqs_ref, dkn_ref, dgq_ref, dgk_ref, dgv_ref, dsm_ref, df_ref, dfq_in_ref,
             dfq_ref, dfk_ref, dconv_ref, dps_ref, dqg_ref, dkg_ref, sums_ref, xe_sc, carry_sc):
        step = pl.program_id(0)
        blk = nb - 1 - step

        @pl.when(step == 0)
        def _():
            carry_sc[...] = jnp.zeros_like(carry_sc)
            dqg_ref[...] = jnp.zeros_like(dqg_ref)
            dkg_ref[...] = jnp.zeros_like(dkg_ref)
            sums_ref[...] = jnp.zeros_like(sums_ref)

        for x_ref, g_ref, dy_ref, o_ref, acc_ref, mul in ((fq_ref, qg_ref, dqs_ref, dfq_ref, dqg_ref, QK_SCALE),
                                                          (fk_ref, kg_ref, dkn_ref, dfk_ref, dkg_ref, 1.0)):
            gain = g_ref[...]
            acc = jnp.zeros((1, HEAD_DIM), f32)
            for h in range(HEADS):
                sl = slice(h * HEAD_DIM, (h + 1) * HEAD_DIM)
                xv = x_ref[:, sl]
                r = lax.rsqrt(jnp.mean(xv * xv, axis=-1, keepdims=True) + EPS)
                xn = xv * r
                dy = dy_ref[:, sl] * mul
                acc = acc + jnp.sum(dy * xn, axis=0, keepdims=True)
                tt = dy * gain
                o_ref[:, sl] = (r * (tt - xn * jnp.mean(tt * xn, axis=-1, keepdims=True))).astype(bf16)
            acc_ref[...] += acc

        first = blk == 0
        for sec, (x_ref, halo_ref, dy_ref) in enumerate(((gq_ref, hq_ref, dgq_ref), (gk_ref, hk_ref, dgk_ref),
                                                         (gv_ref, hv_ref, dgv_ref))):
            xe_sc[0:HALO, :] = jnp.where(first, 0.0, halo_ref[...])
            xe_sc[HALO:, :] = x_ref[...]
            cv = _conv_section(xe_sc, cw_ref, slice(sec * WIDTH, (sec + 1) * WIDTH), tm)
            sgrad = _silu_grad(cv)
            if sec == 2:
                dconv_ref[:, sec * WIDTH:(sec + 1) * WIDTH] = dy_ref[...] * sgrad
            else:
                y = cv * _sigmoid(cv)
                mul = QK_SCALE if sec == 0 else 1.0
                for h in range(HEADS):
                    sl = slice(h * HEAD_DIM, (h + 1) * HEAD_DIM)
                    yh = y[:, sl]
                    r = lax.rsqrt(jnp.sum(yh * yh, axis=-1, keepdims=True) + EPS)
                    dqh = dy_ref[:, sl]
                    dyh = (mul * r) * (dqh - yh * (r * r) * jnp.sum(dqh * yh, axis=-1, keepdims=True))
                    dconv_ref[:, sec * WIDTH + h * HEAD_DIM:sec * WIDTH + (h + 1) * HEAD_DIM] = dyh * sgrad[:, sl]

        lane = _iota((tm, N_SMALL), 1)
        z, _, gval, beta = _small_fwd(ps_ref[...], bv_ref[...], al_ref[...])
        sig_z = _sigmoid(z)
        dsm = dsm_ref[...]
        in_g = (lane >= LANE_G) & (lane < LANE_G + HEADS)
        dgc = jnp.where(in_g, pltpu.roll(dsm, N_SMALL - (LANE_GC - LANE_G), 1), 0.0)
        dgl = jnp.where(in_g, pltpu.roll(dsm, N_SMALL - (LANE_GLAST - LANE_G), 1), 0.0)
        tri_c, ones_c = _chunk_masks(tm)
        dg = (_dg(tri_c, dgc, 0, 0, HI) + jnp.dot(ones_c, dgl, preferred_element_type=f32, precision=HI))
        dbeta = dsm
        dfb = jnp.where(lane < HEADS, df_ref[...], 0.0)
        for h in range(HEADS):
            dfb = dfb + jnp.where(lane == h, dfq_in_ref[h], 0.0)
        tri_u = (_iota((tm, tm), 1) >= _iota((tm, tm), 0)).astype(f32)
        dlogf = jnp.dot(tri_u, dfb, preferred_element_type=f32, precision=HI) + carry_sc[...]
        carry_sc[...] += jnp.sum(dfb, axis=0, keepdims=True)
        dff = dlogf * (1.0 - sig_z)
        dga = dg * (-jnp.exp(al_ref[...])) * sig_z
        dgb_small = dbeta * beta * (1.0 - beta)
        dps = jnp.where(lane < HEADS, dff, jnp.where(lane < 2 * HEADS, dga, jnp.where(lane < 3 * HEADS, dgb_small, 0.0)))
        dps_ref[...] = dps.astype(bf16)
        row = _iota((8, N_SMALL), 0)
        s0 = jnp.sum(dps, axis=0, keepdims=True)
        s1 = jnp.sum(jnp.where((lane >= HEADS) & (lane < 2 * HEADS), dg * gval, 0.0), axis=0, keepdims=True)
        sums_ref[...] += jnp.where(row == 0, s0, jnp.where(row == 1, s1, 0.0))

    def col(cb):
        return pl.BlockSpec((tm, WIDTH), lambda i: (nb - 1 - i, cb))

    def halo(cb):
        return pl.BlockSpec((HALO, WIDTH), lambda i: (jnp.maximum((nb - 1 - i) * (tm // HALO) - 1, 0), cb))

    vec = pl.BlockSpec((1, LANES), lambda i: (0, 0))
    row0 = pl.BlockSpec((tm, WIDTH), lambda i: (nb - 1 - i, 0))
    small = pl.BlockSpec((tm, N_SMALL), lambda i: (nb - 1 - i, 0))
    wide_bf = jax.ShapeDtypeStruct((s_len, WIDTH), bf16)
    return pl.pallas_call(
        body, name="post1", grid=(nb,),
        in_specs=[col(0), col(1), col(4), col(5), col(6), halo(4), halo(5), halo(6), small, vec, vec,
                  pl.BlockSpec((CONV_K, 3 * WIDTH), lambda i: (0, 0)), vec, vec,
                  row0, row0, row0, row0, row0, small, small,
                  pl.BlockSpec((HEADS, tm, 1), lambda i: (0, nb - 1 - i, 0))],
        out_specs=[row0, row0, pl.BlockSpec((tm, 3 * WIDTH), lambda i: (nb - 1 - i, 0)), small, vec, vec,
                   pl.BlockSpec((8, N_SMALL), lambda i: (0, 0))],
        out_shape=[wide_bf, wide_bf, jax.ShapeDtypeStruct((s_len, 3 * WIDTH), f32),
                   jax.ShapeDtypeStruct((s_len, N_SMALL), bf16), jax.ShapeDtypeStruct((1, LANES), f32),
                   jax.ShapeDtypeStruct((1, LANES), f32), jax.ShapeDtypeStruct((8, N_SMALL), f32)],
        scratch_shapes=[pltpu.VMEM((tm + HALO, WIDTH), f32), pltpu.VMEM((1, N_SMALL), f32)],
        compiler_params=_params("arbitrary"),
    )(p_main, p_main, p_main, p_main, p_main, p_main, p_main, p_main, p_small, qn_g, kn_g, conv_w, bvec, alog,
      dqs, dkn, dgq, dgk, dgv, d_small, df, df_query)


def _post2(p_main, dconv, conv_w):
    s_len = p_main.shape[0]
    tm = PREP_TM
    nb = s_len // tm

    def body(gq_ref, gk_ref, gv_ref, hq_ref, hk_ref, hv_ref, dc_ref, dnext_ref, cw_ref, dx_ref, dw_ref, xe_sc, de_sc):
        i = pl.program_id(0)

        @pl.when(i == 0)
        def _():
            dw_ref[...] = jnp.zeros_like(dw_ref)

        first, last = i == 0, i == nb - 1
        row = _iota((8, WIDTH), 0)
        for sec, (x_ref, halo_ref) in enumerate(((gq_ref, hq_ref), (gk_ref, hk_ref), (gv_ref, hv_ref))):
            cols = slice(sec * WIDTH, (sec + 1) * WIDTH)
            dc = dc_ref[:, cols]
            de_sc[0:tm, :] = dc
            de_sc[tm:, :] = jnp.where(last, 0.0, dnext_ref[:, cols])
            dx = cw_ref[pl.ds(CONV_K - 1, 1), cols] * dc
            for tap in range(CONV_K - 1):
                dx = dx + cw_ref[pl.ds(tap, 1), cols] * de_sc[pl.ds(CONV_K - 1 - tap, tm), :]
            dx_ref[:, cols] = dx.astype(bf16)
            xe_sc[0:HALO, :] = jnp.where(first, 0.0, halo_ref[...])
            xe_sc[HALO:, :] = x_ref[...]
            dw = jnp.zeros((8, WIDTH), f32)
            for tap in range(CONV_K):
                contrib = jnp.sum(dc * xe_sc[pl.ds(HALO - (CONV_K - 1) + tap, tm), :], axis=0, keepdims=True)
                dw = jnp.where(row == tap, contrib, dw)
            dw_ref[:, cols] += dw

    def col(cb):
        return pl.BlockSpec((tm, WIDTH), lambda i: (i, cb))

    def halo(cb):
        return pl.BlockSpec((HALO, WIDTH), lambda i: (jnp.maximum(i * (tm // HALO) - 1, 0), cb))

    return pl.pallas_call(
        body, name="post2", grid=(nb,),
        in_specs=[col(4), col(5), col(6), halo(4), halo(5), halo(6),
                  pl.BlockSpec((tm, 3 * WIDTH), lambda i: (i, 0)),
                  pl.BlockSpec((HALO, 3 * WIDTH), lambda i: (jnp.minimum((i + 1) * (tm // HALO), s_len // HALO - 1), 0)),
                  pl.BlockSpec((CONV_K, 3 * WIDTH), lambda i: (0, 0))],
        out_specs=[pl.BlockSpec((tm, 3 * WIDTH), lambda i: (i, 0)), pl.BlockSpec((8, 3 * WIDTH), lambda i: (0, 0))],
        out_shape=[jax.ShapeDtypeStruct((s_len, 3 * WIDTH), bf16), jax.ShapeDtypeStruct((8, 3 * WIDTH), f32)],
        scratch_shapes=[pltpu.VMEM((tm + HALO, WIDTH), f32), pltpu.VMEM((tm + HALO, WIDTH), f32)],
        compiler_params=_params("arbitrary"),
    )(p_main, p_main, p_main, p_main, p_main, p_main, dconv, dconv, conv_w)


def _in_proj_bwd(dp_main, dp_small, wt_main, wt_small, x, dy1, norm_g, scale1p):
    s_len = x.shape[0]
    tm, tk = 512, 1024
    nk = N_MAIN // tk

    def body(dp_ref, dps_ref, w_ref, ws_ref, x_ref, dy_ref, g_ref, sc_ref, dx_ref, dsh_ref, dsc_ref, dg_ref, acc_sc):
        i, k = pl.program_id(0), pl.program_id(1)

        @pl.when((i == 0) & (k == 0))
        def _():
            dsh_ref[...] = jnp.zeros_like(dsh_ref)
            dsc_ref[...] = jnp.zeros_like(dsc_ref)
            dg_ref[...] = jnp.zeros_like(dg_ref)

        @pl.when(k == 0)
        def _():
            acc_sc[...] = jnp.dot(dps_ref[...], ws_ref[...], preferred_element_type=f32)

        acc_sc[...] += jnp.dot(dp_ref[...], w_ref[...], preferred_element_type=f32)

        @pl.when(k == nk - 1)
        def _():
            dh = acc_sc[...]
            xb = x_ref[...]
            r = lax.rsqrt(jnp.mean(xb * xb, axis=-1, keepdims=True) + EPS)
            xr = xb * r
            gain = g_ref[...]
            dsh_ref[...] += jnp.sum(dh, axis=0, keepdims=True)
            dsc_ref[...] += jnp.sum(dh * (xr * gain), axis=0, keepdims=True)
            dxn = dh * sc_ref[...]
            dg_ref[...] += jnp.sum(dxn * xr, axis=0, keepdims=True)
            tt = dxn * gain
            dx_ref[...] = r * (tt - xr * jnp.mean(tt * xr, axis=-1, keepdims=True)) + dy_ref[...]

    row = pl.BlockSpec((tm, D_MODEL), lambda i, k: (i, 0))
    vec = pl.BlockSpec((1, D_MODEL), lambda i, k: (0, 0))
    vshape = jax.ShapeDtypeStruct((1, D_MODEL), f32)
    return pl.pallas_call(
        body, name="in_proj_bwd", grid=(s_len // tm, nk),
        in_specs=[pl.BlockSpec((tm, tk), lambda i, k: (i, k)), pl.BlockSpec((tm, N_SMALL), lambda i, k: (i, 0)),
                  pl.BlockSpec((tk, D_MODEL), lambda i, k: (k, 0)), pl.BlockSpec((N_SMALL, D_MODEL), lambda i, k: (0, 0)),
                  row, row, vec, vec],
        out_specs=[row, vec, vec, vec],
        out_shape=[jax.ShapeDtypeStruct((s_len, D_MODEL), f32), vshape, vshape, vshape],
        scratch_shapes=[pltpu.VMEM((tm, D_MODEL), f32)],
        compiler_params=_params("arbitrary", "arbitrary"),
    )(dp_main, dp_small, wt_main, wt_small, x, dy1, norm_g, scale1p)


def _adamw(name, w, g_stack, m, v, tr, tc=None):
    n_stack, rows, cols = g_stack.shape
    tc = cols if tc is None else tc

    def body(w_ref, g_ref, m_ref, v_ref, go_ref, d_ref, mo_ref, vo_ref):
        g = g_ref[0].astype(f32)
        for k in range(1, n_stack):
            g = g + g_ref[k].astype(f32)
        go_ref[0] = g
        m_new = ADAM_B1 * m_ref[0] + (1.0 - ADAM_B1) * g
        v_new = ADAM_B2 * v_ref[0] + (1.0 - ADAM_B2) * (g * g)
        mo_ref[0] = m_new
        vo_ref[0] = v_new
        m_hat = m_new / (1.0 - ADAM_B1 ** ADAM_STEP)
        v_hat = v_new / (1.0 - ADAM_B2 ** ADAM_STEP)
        d_ref[0] = -ADAM_LR * (m_hat / (jnp.sqrt(v_hat) + ADAM_EPS) + ADAM_WD * w_ref[0])

    blk = pl.BlockSpec((1, tr, tc), lambda i, j: (0, i, j))
    shape = jax.ShapeDtypeStruct((1, rows, cols), f32)
    return pl.pallas_call(
        body, name=name, grid=(rows // tr, cols // tc),
        in_specs=[blk, pl.BlockSpec((n_stack, tr, tc), lambda i, j: (0, i, j)), blk, blk],
        out_specs=[blk] * 4, out_shape=[shape] * 4,
        compiler_params=_params("parallel", "parallel"),
    )(w, g_stack, m, v)


def _w_ada_grad(c_all_t, dmod_pad):
    def body(c_ref, d_ref, o_ref):
        cv = c_ref[...]
        o_ref[...] = jnp.dot(cv * _sigmoid(cv), d_ref[...], preferred_element_type=f32, precision=HI)

    return pl.pallas_call(body, name="w_ada_grad",
                          out_shape=jax.ShapeDtypeStruct((c_all_t.shape[0], dmod_pad.shape[1]), f32),
                          compiler_params=_params())(c_all_t, dmod_pad)


SMALL_NAMES = ("norm_g", "b_ada", "b_fgate", "fox_qn_g", "fox_kn_g", "gdn_A_log", "gdn_dt_bias", "gdn_norm_g", "final_g")
SMALL_SIZES = (D_MODEL, 3 * D_MODEL, HEADS, HEAD_DIM, HEAD_DIM, HEADS, HEADS, HEAD_DIM, D_MODEL)
SMALL_PACK = 10752


def _pack(vectors, total):
    flat = jnp.concatenate([t.reshape(-1) for t in vectors])
    return jnp.pad(flat, (0, total - flat.shape[0])).reshape(1, total)


def _lanes(*pieces):
    row = jnp.zeros((LANES,), f32)
    for off, vec in pieces:
        row = lax.dynamic_update_slice(row, vec.reshape(-1).astype(f32), (off,))
    return row.reshape(1, LANES)


def kernel(x, c, norm_g, w_ada, b_ada, w_in, b_fgate, fox_qn_g, fox_kn_g, gdn_conv_w, gdn_A_log, gdn_dt_bias, gdn_norm_g, w_out, final_g, loss_target, m_norm_g, m_w_ada, m_b_ada, m_w_in, m_b_fgate, m_fox_qn_g, m_fox_kn_g, m_gdn_conv_w, m_gdn_A_log, m_gdn_dt_bias, m_gdn_norm_g, m_w_out, m_final_g, v_norm_g, v_w_ada, v_b_ada, v_w_in, v_b_fgate, v_fox_qn_g, v_fox_kn_g, v_gdn_conv_w, v_gdn_A_log, v_gdn_dt_bias, v_gdn_norm_g, v_w_out, v_final_g):
    me = _my_index()
    s_len = x.shape[1]
    nq = s_len // FOX_T
    x2 = x.reshape(s_len, D_MODEL)
    tgt = loss_target.reshape(s_len, D_MODEL)
    ada_cols = w_ada.shape[2]
    in_cols = w_in.shape[2]
    conv_cols = gdn_conv_w.shape[2]

    (c_all,) = _exchange("gather_c", [c], scatter=False)
    c_all = c_all.reshape(N_DEV, D_MODEL)
    b_shard = lax.dynamic_slice(b_ada, (0, me * ada_cols), (1, ada_cols))
    mod_mine = _mod_shard(c_all, w_ada[0], b_shard)
    wt_shard = jnp.transpose(w_in[0])
    mod_all, wt_all, w_out_all, conv_all = _gather_two_level(
        "gather_weights", [mod_mine, wt_shard.astype(bf16), w_out[0].astype(bf16), gdn_conv_w[0]])
    mod = lax.dynamic_slice(mod_all, (0, me, 0), (N_DEV, 1, ada_cols)).reshape(1, 3 * D_MODEL)
    shift, scale, gate = mod[:, :D_MODEL], mod[:, D_MODEL:2 * D_MODEL], mod[:, 2 * D_MODEL:]
    scale1p = 1.0 + scale
    wt_full = wt_all.reshape(N_DEV * in_cols, D_MODEL)
    g0 = 4 * WIDTH + HEADS
    w_main = jnp.concatenate([wt_full[:4 * WIDTH], wt_full[g0:g0 + 4 * WIDTH]], axis=0)
    w_small = jnp.concatenate([wt_full[4 * WIDTH:g0], wt_full[g0 + 4 * WIDTH:],
                               jnp.zeros((N_SMALL - 3 * HEADS, D_MODEL), bf16)], axis=0)
    w_out_full = w_out_all.reshape(2 * WIDTH, D_MODEL)
    conv_full = jnp.transpose(conv_all, (1, 0, 2)).reshape(CONV_K, 3 * WIDTH)

    qn_g, kn_g, gn_g = fox_qn_g.reshape(1, LANES), fox_kn_g.reshape(1, LANES), gdn_norm_g.reshape(1, LANES)
    bvec = _lanes((0, b_fgate), (HEADS, gdn_dt_bias))
    alog = _lanes((HEADS, gdn_A_log))
    fg = final_g.reshape(1, D_MODEL)

    p_main, p_small, h_bf = _in_proj(x2, norm_g, scale1p, shift, w_main, w_small)
    qs, kn, vb, gq, gk, gv, small = _prep(p_main, p_small, qn_g, kn_g, conv_full, bvec, alog)
    f_row = jnp.transpose(small[:, :HEADS]).reshape(HEADS, nq, 1, FOX_T)
    fox_o, lse = _fox_fwd(qs, kn, vb, small, f_row)
    gu, gw, gqg, gkd, gattn, t_inv, eg_last = _gdn_intra_fwd(gq, gk, gv, small)
    gdn_o, states = _gdn_scan_fwd(gu, gw, gqg, gkd, gattn, eg_last)
    mixed = _mix_fwd(fox_o, gdn_o, p_main, gn_g)

    loss_row, dy1, dz, dmixed, d_final_g, d_gate = _out_head(mixed, w_out_full, x2, tgt, gate, fg)
    loss = lax.psum(loss_row[0, 0], AXES)
    dw_out = _matmul_tn("dw_out", mixed, dz, bf16)
    do_fox, delta, dfz, dgz, dgdn_o, d_gn_g = _mix_bwd(dmixed, fox_o, gdn_o, p_main, gn_g)
    dqs, dkn, dvf, df_key, df_query = _fox_bwd(qs, kn, vb, do_fox, small, lse, delta, f_row)
    du, dw, dqg, dkd, dattn, deg = _gdn_scan_bwd(gu, gw, gqg, gkd, gattn, eg_last, states, dgdn_o)
    dgq, dgk, dgv, d_small = _gdn_intra_bwd(gq, gk, gv, small, t_inv, du, dw, dqg, dkd, dattn, deg)
    df_small = jnp.pad(jnp.transpose(df_key.reshape(HEADS, s_len)), ((0, 0), (0, N_SMALL - HEADS)))
    dfq, dfk, dconv, dp_small, d_qn_g, d_kn_g, sums = _post1(
        p_main, p_small, qn_g, kn_g, conv_full, bvec, alog, dqs, dkn, dgq, dgk, dgv, d_small, df_small, df_query)
    dgqkv, d_conv = _post2(p_main, dconv, conv_full)
    dp_main = jnp.concatenate([dfq, dfk, dvf.astype(bf16), dfz, dgqkv, dgz], axis=1)
    grad_x, d_shift, d_scale, d_norm_g = _in_proj_bwd(dp_main, dp_small, w_main, w_small, x2, dy1, norm_g, scale1p)
    dw_main = _matmul_tn("dw_main", dp_main, h_bf, bf16)
    dw_small = _matmul_tn("dw_small", dp_small, h_bf, bf16)
    dw_in_full = jnp.concatenate([dw_main[:4 * WIDTH], dw_small[:HEADS], dw_main[4 * WIDTH:],
                                  dw_small[HEADS:3 * HEADS]], axis=0)
    dw_in_parts = dw_in_full.reshape(N_DEV, in_cols, D_MODEL)
    dw_out_parts = dw_out.reshape(N_DEV, w_out.shape[1], D_MODEL)

    dmod = jnp.concatenate([d_shift, d_scale, d_gate], axis=1)
    small_grads = _pack([d_norm_g, dmod, sums[0, :HEADS], d_qn_g, d_kn_g, sums[1, HEADS:2 * HEADS],
                         sums[0, HEADS:2 * HEADS], d_gn_g, d_final_g], SMALL_PACK)
    conv_grad = d_conv[:CONV_K]
    pair_in, pair_out = _pair_exchange("pair_grads", [dw_in_parts, dw_out_parts])
    core = lax.axis_index("c").astype(jnp.int32).reshape(1)
    dw_in_recv, dw_out_recv = _chip_exchange(
        "chip_grads", [_pair_sum("pair_sum_w_in", dw_in_parts, pair_in, core),
                       _pair_sum("pair_sum_w_out", dw_out_parts, pair_out, core)])
    small_all, conv_all_g = _exchange("gather_small_grads", [small_grads, conv_grad], scatter=False)

    outs = {}
    to_t = lambda t: jnp.transpose(t, (0, 2, 1))
    outs["w_in"] = tuple(to_t(t) for t in _adamw("adamw_w_in", to_t(w_in), dw_in_recv, to_t(m_w_in), to_t(v_w_in),
                                                  in_cols, 256))
    outs["w_out"] = _adamw("adamw_w_out", w_out, dw_out_recv, m_w_out, v_w_out, 128)
    conv_mine = lax.dynamic_slice(jnp.transpose(conv_all_g.reshape(N_DEV, CONV_K, N_DEV, conv_cols), (0, 2, 1, 3)),
                                  (0, me, 0, 0), (N_DEV, 1, CONV_K, conv_cols)).reshape(N_DEV, CONV_K, conv_cols)
    outs["gdn_conv_w"] = _adamw("adamw_conv", gdn_conv_w, conv_mine, m_gdn_conv_w, v_gdn_conv_w, CONV_K)
    small_all = small_all.reshape(N_DEV, 1, SMALL_PACK)
    dmod_all = small_all[:, 0, D_MODEL:D_MODEL + 3 * D_MODEL]
    dmod_mine = lax.dynamic_slice(dmod_all, (0, me * ada_cols), (N_DEV, ada_cols))
    c_all_t = jnp.pad(jnp.transpose(c_all), ((0, 0), (0, LANES - N_DEV)))
    g_w_ada = _w_ada_grad(c_all_t, jnp.pad(dmod_mine, ((0, LANES - N_DEV), (0, 0))))
    outs["w_ada"] = _adamw("adamw_w_ada", w_ada, g_w_ada[None], m_w_ada, v_w_ada, 256)
    given = dict(norm_g=(norm_g, m_norm_g, v_norm_g), b_ada=(b_ada, m_b_ada, v_b_ada), b_fgate=(b_fgate, m_b_fgate, v_b_fgate),
                 fox_qn_g=(fox_qn_g, m_fox_qn_g, v_fox_qn_g), fox_kn_g=(fox_kn_g, m_fox_kn_g, v_fox_kn_g),
                 gdn_A_log=(gdn_A_log, m_gdn_A_log, v_gdn_A_log), gdn_dt_bias=(gdn_dt_bias, m_gdn_dt_bias, v_gdn_dt_bias),
                 gdn_norm_g=(gdn_norm_g, m_gdn_norm_g, v_gdn_norm_g), final_g=(final_g, m_final_g, v_final_g))
    w_pack = _pack([given[n][0] for n in SMALL_NAMES], SMALL_PACK)
    m_pack = _pack([given[n][1] for n in SMALL_NAMES], SMALL_PACK)
    v_pack = _pack([given[n][2] for n in SMALL_NAMES], SMALL_PACK)
    packed = _adamw("adamw_small", w_pack[None], small_all, m_pack[None], v_pack[None], 1)
    off = 0
    for n, size in zip(SMALL_NAMES, SMALL_SIZES):
        outs[n] = tuple(t[0, 0, off:off + size].reshape(given[n][0].shape) for t in packed)
        off += size

    order = ("norm_g", "w_ada", "b_ada", "w_in", "b_fgate", "fox_qn_g", "fox_kn_g", "gdn_conv_w", "gdn_A_log",
             "gdn_dt_bias", "gdn_norm_g", "w_out", "final_g")
    result = [loss, grad_x.reshape(x.shape)]
    for part in range(4):
        result += [outs[n][part] for n in order]
    return tuple(result)
```

```python
import math

import jax
import jax.numpy as jnp
from jax import lax
from jax.experimental import pallas as pl
from jax.experimental.pallas import tpu as pltpu

f32 = jnp.float32
bf16 = jnp.bfloat16
HI = lax.Precision.HIGHEST

N_DEV = 8
AXES = ("x", "y", "c")
D_MODEL = 2048
HEADS = 8
HEAD_DIM = 128
WIDTH = HEADS * HEAD_DIM
CHUNK = 64
CONV_K = 4
EPS = 1e-6
QK_SCALE = HEAD_DIM ** -0.5
LOG2E = 1.0 / math.log(2.0)
LN2 = math.log(2.0)
N_MAIN = 8 * WIDTH
N_SMALL = 128
LANE_F, LANE_G, LANE_BETA, LANE_GC, LANE_GLAST = 0, 8, 16, 24, 32
IN_WIDTH = 8 * WIDTH + 3 * HEADS
LANES = 128
VMEM_LIMIT = 56 * 1024 * 1024

ADAM_LR, ADAM_B1, ADAM_B2, ADAM_EPS, ADAM_WD, ADAM_STEP = 0.001, 0.9, 0.999, 1e-08, 0.01, 10


def _params(*sem):
    return pltpu.CompilerParams(dimension_semantics=sem, vmem_limit_bytes=VMEM_LIMIT)


def _iota(shape, dim):
    return lax.broadcasted_iota(jnp.int32, shape, dim)


def _sigmoid(z):
    return 1.0 / (1.0 + jnp.exp(-z))


def _softplus_parts(z):
    t = jnp.log(1.0 + jnp.exp(-jnp.abs(z)))
    return jnp.minimum(z, 0.0) - t, jnp.maximum(z, 0.0) + t


def _dg(a, b, ca, cb, prec=None):
    if a.ndim == 3:
        dims = (((ca + 1,), (cb + 1,)), ((0,), (0,)))
    else:
        dims = (((ca,), (cb,)), ((), ()))
    return lax.dot_general(a, b, dims, preferred_element_type=f32, precision=prec)


def _dot_bf16(a, b, ca, cb):
    return _dg(a.astype(bf16), b.astype(bf16), ca, cb)


def _split_bf16(a):
    hi = a.astype(bf16)
    return hi, (a - hi.astype(f32)).astype(bf16)


def _dot_3pass(a, b, ca, cb):
    a_hi, a_lo = _split_bf16(a)
    b_hi, b_lo = _split_bf16(b)
    return _dg(a_hi, b_hi, ca, cb) + (_dg(a_hi, b_lo, ca, cb) + _dg(a_lo, b_hi, ca, cb))


def _make_mm(dot):
    def nn_(a, b):
        return dot(a, b, 1, 0)

    def nt_(a, b):
        return dot(a, b, 1, 1)

    def tn_(a, b):
        return dot(a, b, 0, 0)

    @jax.custom_vjp
    def nn(a, b):
        return nn_(a, b)

    @jax.custom_vjp
    def nt(a, b):
        return nt_(a, b)

    @jax.custom_vjp
    def tn(a, b):
        return tn_(a, b)

    nn.defvjp(lambda a, b: (nn_(a, b), (a, b)), lambda r, g: (nt_(g, r[1]), tn_(r[0], g)))
    nt.defvjp(lambda a, b: (nt_(a, b), (a, b)), lambda r, g: (nn_(g, r[1]), tn_(g, r[0])))
    tn.defvjp(lambda a, b: (tn_(a, b), (a, b)), lambda r, g: (nt_(r[1], g), nn_(r[0], g)))
    return (nn_, nt_, tn_), (nn, nt, tn)


_BF_PLAIN, _BF_VJP = _make_mm(_dot_bf16)
_X3_PLAIN, _X3_VJP = _make_mm(_dot_3pass)


def _inv_unit_lower_many(ms):
    c = CHUNK
    nn = _X3_PLAIN[0]
    eye = (_iota((c, c), 0) == _iota((c, c), 1)).astype(f32)
    top = _iota((2 * c, c), 0) < c
    xs = [jnp.concatenate([eye - m, nn(m, m)], axis=0) for m in ms]
    for _ in range(int(math.log2(CHUNK)) - 2):
        xs = [jnp.where(top, x, 0.0) + nn(x, x[c:]) for x in xs]
    return [x[:c] + nn(x[:c], x[c:]) for x in xs]


@jax.custom_vjp
def _inv_given(m, t):
    return t


_inv_given.defvjp(lambda m, t: (t, t),
                  lambda t, g: (-_X3_PLAIN[1](_X3_PLAIN[2](t, g), t), jnp.zeros_like(t)))

SUBLANES = 8


def _gdn_intra_pre(q, k, v, gc_b, g_last_b, beta_b, bnt):
    c = CHUNK
    r_i, c_i = _iota((c, c), 0), _iota((c, c), 1)
    lower, strict = r_i >= c_i, r_i > c_i
    gc_i = gc_b[..., :c]
    gc_j = jnp.swapaxes(gc_i, -1, -2)
    decay = jnp.where(lower, jnp.exp(jnp.where(lower, gc_i - gc_j, 0.0)), 0.0)
    kb = k * beta_b
    both = bnt(jnp.concatenate([kb, q], axis=-2), k)
    m = jnp.where(strict, both[..., :c, :] * decay, 0.0)
    attn = jnp.where(lower, both[..., c:, :] * decay, 0.0)
    eg = jnp.exp(gc_b)
    rhs = jnp.concatenate([v * beta_b, kb * eg], axis=-1)
    k_dec = k * jnp.exp(g_last_b - gc_b)
    eg_last = jnp.exp(g_last_b[..., :SUBLANES, :])
    return m, rhs, q * eg, k_dec, attn, eg_last


def _gdn_intra(q, k, v, gc_b, g_last_b, beta_b, t_known):
    m, rhs, qg, k_dec, attn, eg_last = _gdn_intra_pre(q, k, v, gc_b, g_last_b, beta_b, _BF_VJP[1])
    return _X3_VJP[0](_inv_given(m, t_known), rhs), qg, k_dec, attn, eg_last


def _scale_rows(s, eg_last):
    return (s.reshape(HEAD_DIM // SUBLANES, SUBLANES, HEAD_DIM) * eg_last[None]).reshape(HEAD_DIM, HEAD_DIM)


def _my_index():
    return 4 * lax.axis_index("x") + 2 * lax.axis_index("y") + lax.axis_index("c")


def _peer(d):
    x, y, c = lax.axis_index("x"), lax.axis_index("y"), lax.axis_index("c")
    px, py, pc = (x + (d >> 2)) % 2, (y + ((d >> 1) & 1)) % 2, (c + (d & 1)) % 2
    return (px, py, pc), 4 * px + 2 * py + pc


def _gather_direct(name, arrays):
    n = len(arrays)

    def body(*refs):
        srcs, dsts = refs[:n], refs[n:2 * n]
        send_sems, recv_sems, local_sems = refs[2 * n:]
        me = _my_index()

        def copy(k, d, started):
            peer, pidx = _peer(d)
            return pltpu.make_async_remote_copy(
                src_ref=srcs[k], dst_ref=dsts[k].at[me if started else pidx], send_sem=send_sems.at[k * 7 + d - 1],
                recv_sem=recv_sems.at[k * 7 + d - 1], device_id=peer, device_id_type=pl.DeviceIdType.MESH)

        local = [pltpu.make_async_copy(srcs[k], dsts[k].at[me], local_sems.at[k]) for k in range(n)]
        sends = [copy(k, d, True) for k in range(n) for d in range(1, N_DEV)]
        for cp in local + sends:
            cp.start()
        for k in range(n):
            for d in range(1, N_DEV):
                copy(k, d, False).wait_recv()
        for cp in sends:
            cp.wait_send()
        for cp in local:
            cp.wait()

    out_shape = [jax.ShapeDtypeStruct((N_DEV,) + a.shape, a.dtype) for a in arrays]
    any_spec = pl.BlockSpec(memory_space=pl.ANY)
    return pl.pallas_call(
        body, name=name, out_shape=out_shape, in_specs=[any_spec] * n, out_specs=[any_spec] * n,
        scratch_shapes=[pltpu.SemaphoreType.DMA((7 * n,)), pltpu.SemaphoreType.DMA((7 * n,)),
                        pltpu.SemaphoreType.DMA((n,))],
        compiler_params=pltpu.CompilerParams(has_side_effects=True),
    )(*arrays)


N_CHIPS = 4


def _pair_exchange(name, arrays):
    n = len(arrays)

    def body(*refs):
        srcs, dsts = refs[:n], refs[n:2 * n]
        send_sems, recv_sems = refs[2 * n:]
        x, y, c = lax.axis_index("x"), lax.axis_index("y"), lax.axis_index("c")
        sibling = (x, y, 1 - c)

        def copy(k, j):
            return pltpu.make_async_remote_copy(
                src_ref=srcs[k].at[2 * j + (1 - c)], dst_ref=dsts[k].at[j], send_sem=send_sems.at[k * N_CHIPS + j],
                recv_sem=recv_sems.at[k * N_CHIPS + j], device_id=sibling, device_id_type=pl.DeviceIdType.MESH)

        copies = [copy(k, j) for k in range(n) for j in range(N_CHIPS)]
        for cp in copies:
            cp.start()
        for cp in copies:
            cp.wait_recv()
        for cp in copies:
            cp.wait_send()

    any_spec = pl.BlockSpec(memory_space=pl.ANY)
    return pl.pallas_call(
        body, name=name, out_shape=[jax.ShapeDtypeStruct((N_CHIPS,) + a.shape[1:], a.dtype) for a in arrays],
        in_specs=[any_spec] * n, out_specs=[any_spec] * n,
        scratch_shapes=[pltpu.SemaphoreType.DMA((N_CHIPS * n,)), pltpu.SemaphoreType.DMA((N_CHIPS * n,))],
        compiler_params=pltpu.CompilerParams(has_side_effects=True),
    )(*arrays)


def _chip_exchange(name, arrays):
    n = len(arrays)

    def body(*refs):
        srcs, dsts = refs[:n], refs[n:2 * n]
        send_sems, recv_sems, local_sems = refs[2 * n:]
        x, y, c = lax.axis_index("x"), lax.axis_index("y"), lax.axis_index("c")
        my_chip = 2 * x + y

        def peer(d):
            px, py = (x + (d >> 1)) % 2, (y + (d & 1)) % 2
            return (px, py, c), 2 * px + py

        def remote(k, d, started):
            to, chip = peer(d)
            return pltpu.make_async_remote_copy(
                src_ref=srcs[k].at[chip], dst_ref=dsts[k].at[my_chip if started else chip],
                send_sem=send_sems.at[k * 3 + d - 1], recv_sem=recv_sems.at[k * 3 + d - 1],
                device_id=to, device_id_type=pl.DeviceIdType.MESH)

        local = [pltpu.make_async_copy(srcs[k].at[my_chip], dsts[k].at[my_chip], local_sems.at[k]) for k in range(n)]
        sends = [remote(k, d, True) for k in range(n) for d in range(1, N_CHIPS)]
        for cp in local + sends:
            cp.start()
        for k in range(n):
            for d in range(1, N_CHIPS):
                remote(k, d, False).wait_recv()
        for cp in sends:
            cp.wait_send()
        for cp in local:
            cp.wait()

    any_spec = pl.BlockSpec(memory_space=pl.ANY)
    return pl.pallas_call(
        body, name=name, out_shape=[jax.ShapeDtypeStruct(a.shape, a.dtype) for a in arrays],
        in_specs=[any_spec] * n, out_specs=[any_spec] * n,
        scratch_shapes=[pltpu.SemaphoreType.DMA((3 * n,)), pltpu.SemaphoreType.DMA((3 * n,)),
                        pltpu.SemaphoreType.DMA((n,))],
        compiler_params=pltpu.CompilerParams(has_side_effects=True),
    )(*arrays)


def _pair_sum(name, parts, received, core):
    n_blocks, rows, cols = received.shape
    tr = rows if rows % 256 else 256

    def body(core_ref, mine_ref, recv_ref, o_ref):
        o_ref[...] = (mine_ref[...].astype(f32) + recv_ref[...].astype(f32)).astype(bf16)

    return pl.pallas_call(
        body, name=name,
        grid_spec=pltpu.PrefetchScalarGridSpec(
            num_scalar_prefetch=1, grid=(n_blocks, rows // tr),
            in_specs=[pl.BlockSpec((1, tr, cols), lambda j, i, core_ref: (2 * j + core_ref[0], i, 0)),
                      pl.BlockSpec((1, tr, cols), lambda j, i, core_ref: (j, i, 0))],
            out_specs=pl.BlockSpec((1, tr, cols), lambda j, i, core_ref: (j, i, 0))),
        out_shape=jax.ShapeDtypeStruct((n_blocks, rows, cols), bf16),
        compiler_params=_params("parallel", "parallel"),
    )(core, parts, received)


def _gather_two_level(name, arrays):
    n = len(arrays)

    def body(*refs):
        srcs, dsts = refs[:n], refs[n:2 * n]
        send_sems, recv_sems, local_sems = refs[2 * n:]
        x, y, c = lax.axis_index("x"), lax.axis_index("y"), lax.axis_index("c")
        sibling = (x, y, 1 - c)
        chips = [((x + 1) % 2, y), (x, (y + 1) % 2), ((x + 1) % 2, (y + 1) % 2)]

        def index(px, py, pc):
            return 4 * px + 2 * py + pc

        def copy(k, slot, block, to, src=None):
            return pltpu.make_async_remote_copy(
                src_ref=dsts[k].at[index(*block)] if src is None else src, dst_ref=dsts[k].at[index(*block)],
                send_sem=send_sems.at[k * 7 + slot], recv_sem=recv_sems.at[k * 7 + slot],
                device_id=to, device_id_type=pl.DeviceIdType.MESH)

        me = (x, y, c)
        local = [pltpu.make_async_copy(srcs[k], dsts[k].at[index(*me)], local_sems.at[k]) for k in range(n)]
        first = [copy(k, 0, me, sibling, src=srcs[k]) for k in range(n)]
        first += [copy(k, 1 + j, me, (*chip, c), src=srcs[k]) for j, chip in enumerate(chips) for k in range(n)]
        for cp in local + first:
            cp.start()
        passed = []
        for j, chip in enumerate(chips):
            for k in range(n):
                copy(k, 1 + j, (*chip, c), me).wait_recv()
                fwd = copy(k, 4 + j, (*chip, c), sibling)
                fwd.start()
                passed.append(fwd)
        for k in range(n):
            copy(k, 0, sibling, me).wait_recv()
            for j, chip in enumerate(chips):
                copy(k, 4 + j, (*chip, 1 - c), me).wait_recv()
        for cp in first + passed:
            cp.wait_send()
        for cp in local:
            cp.wait()

    any_spec = pl.BlockSpec(memory_space=pl.ANY)
    return pl.pallas_call(
        body, name=name, out_shape=[jax.ShapeDtypeStruct((N_DEV,) + a.shape, a.dtype) for a in arrays],
        in_specs=[any_spec] * n, out_specs=[any_spec] * n,
        scratch_shapes=[pltpu.SemaphoreType.DMA((7 * n,)), pltpu.SemaphoreType.DMA((7 * n,)),
                        pltpu.SemaphoreType.DMA((n,))],
        compiler_params=pltpu.CompilerParams(has_side_effects=True),
    )(*arrays)


def _mod_shard(c_all, w_ada, b_shard):
    def body(c_ref, w_ref, b_ref, o_ref):
        cv = c_ref[...]
        ca = cv * _sigmoid(cv)
        o_ref[...] = jnp.dot(ca.astype(bf16), w_ref[...].astype(bf16), preferred_element_type=f32) + b_ref[...]

    return pl.pallas_call(body, name="mod_shard", out_shape=jax.ShapeDtypeStruct((N_DEV, w_ada.shape[1]), f32),
                          compiler_params=_params())(c_all, w_ada, b_shard)


def _in_proj(x, norm_g, scale1p, shift, wt_main, wt_small):
    s_len = x.shape[0]
    tm, tn = 512, 1024

    def body(x_ref, g_ref, sc_ref, sh_ref, w_ref, ws_ref, p_ref, ps_ref, h_ref, h_sc):
        @pl.when(pl.program_id(1) == 0)
        def _():
            xb = x_ref[...]
            r = lax.rsqrt(jnp.mean(xb * xb, axis=-1, keepdims=True) + EPS)
            hb = ((xb * r * g_ref[...]) * sc_ref[...] + sh_ref[...]).astype(bf16)
            h_sc[...] = hb
            h_ref[...] = hb
            ps_ref[...] = _dg(hb, ws_ref[...], 1, 1)

        p_ref[...] = _dg(h_sc[...], w_ref[...], 1, 1)

    vec = pl.BlockSpec((1, D_MODEL), lambda i, j: (0, 0))
    return pl.pallas_call(
        body, name="in_proj", grid=(s_len // tm, N_MAIN // tn),
        in_specs=[pl.BlockSpec((tm, D_MODEL), lambda i, j: (i, 0)), vec, vec, vec,
                  pl.BlockSpec((tn, D_MODEL), lambda i, j: (j, 0)),
                  pl.BlockSpec((N_SMALL, D_MODEL), lambda i, j: (0, 0))],
        out_specs=[pl.BlockSpec((tm, tn), lambda i, j: (i, j)),
                   pl.BlockSpec((tm, N_SMALL), lambda i, j: (i, 0)),
                   pl.BlockSpec((tm, D_MODEL), lambda i, j: (i, 0))],
        out_shape=[jax.ShapeDtypeStruct((s_len, N_MAIN), f32), jax.ShapeDtypeStruct((s_len, N_SMALL), f32),
                   jax.ShapeDtypeStruct((s_len, D_MODEL), bf16)],
        scratch_shapes=[pltpu.VMEM((tm, D_MODEL), bf16)],
        compiler_params=_params("parallel", "arbitrary"),
    )(x, norm_g, scale1p, shift, wt_main, wt_small)


PREP_TM = 256
HALO = 8


def _conv_section(xe_ref, cw_ref, cols, tm):
    acc = cw_ref[pl.ds(CONV_K - 1, 1), cols] * xe_ref[pl.ds(HALO, tm), :]
    for tap in range(CONV_K - 1):
        acc = acc + cw_ref[pl.ds(tap, 1), cols] * xe_ref[pl.ds(HALO - (CONV_K - 1) + tap, tm), :]
    return acc


def _small_fwd(ps, bvec, alog):
    z = ps + bvec
    logsig, softp = _softplus_parts(z)
    gval = -jnp.exp(alog) * softp
    beta = _sigmoid(ps)
    return z, logsig, gval, beta


def _head_lane(block, lane):
    return jnp.sum(jnp.where(_iota(block.shape, 1) == lane, block, 0.0), axis=1, keepdims=True)


def _head_slab(block, lane):
    return jnp.broadcast_to(_head_lane(block, lane), block.shape)


def _chunk_masks(tm):
    r, c = _iota((tm, tm), 0), _iota((tm, tm), 1)
    same = (r // CHUNK) == (c // CHUNK)
    return (same & (r >= c)).astype(f32), same.astype(f32)


def _prep(p_main, p_small, qn_g, kn_g, conv_w, bvec, alog):
    s_len = p_main.shape[0]
    tm = PREP_TM
    nb = s_len // tm

    def body(fq_ref, fk_ref, fv_ref, gq_ref, gk_ref, gv_ref, hq_ref, hk_ref, hv_ref, ps_ref, qg_ref, kg_ref,
             cw_ref, bv_ref, al_ref,
             qs_ref, kn_ref, vb_ref, gqo_ref, gko_ref, gvo_ref, small_ref, xe_sc, carry_sc):
        i = pl.program_id(0)

        @pl.when(i == 0)
        def _():
            carry_sc[...] = jnp.zeros_like(carry_sc)

        vb_ref[...] = fv_ref[...].astype(bf16)

        first = i == 0
        for sec, (x_ref, halo_ref, o_ref) in enumerate(((gq_ref, hq_ref, gqo_ref), (gk_ref, hk_ref, gko_ref),
                                                        (gv_ref, hv_ref, gvo_ref))):
            xe_sc[0:HALO, :] = jnp.where(first, 0.0, halo_ref[...])
            xe_sc[HALO:, :] = x_ref[...]
            cv = _conv_section(xe_sc, cw_ref, slice(sec * WIDTH, (sec + 1) * WIDTH), tm)
            y = cv * _sigmoid(cv)
            if sec == 2:
                o_ref[...] = y
            else:
                mul = QK_SCALE if sec == 0 else 1.0
                for h in range(HEADS):
                    sl = slice(h * HEAD_DIM, (h + 1) * HEAD_DIM)
                    yh = y[:, sl]
                    o_ref[:, sl] = yh * (lax.rsqrt(jnp.sum(yh * yh, axis=-1, keepdims=True) + EPS) * mul)

        lane = _iota((tm, N_SMALL), 1)
        _, logsig, gval, beta = _small_fwd(ps_ref[...], bv_ref[...], al_ref[...])
        lf = jnp.where(lane < HEADS, logsig, 0.0)
        tri = (_iota((tm, tm), 0) >= _iota((tm, tm), 1)).astype(f32)
        fcum = jnp.dot(tri, lf, preferred_element_type=f32, precision=HI) + carry_sc[...]
        carry_sc[...] += jnp.sum(lf, axis=0, keepdims=True)
        tri_c, ones_c = _chunk_masks(tm)
        g_lanes = jnp.where((lane >= LANE_G) & (lane < LANE_G + HEADS), gval, 0.0)
        gc = jnp.dot(tri_c, g_lanes, preferred_element_type=f32, precision=HI)
        g_last = jnp.dot(ones_c, g_lanes, preferred_element_type=f32, precision=HI)
        small = jnp.where(lane < LANE_G, fcum, jnp.where(lane < LANE_BETA, gval, jnp.where(lane < LANE_GC, beta, 0.0)))
        small_ref[...] = small + pltpu.roll(gc, LANE_GC - LANE_G, 1) + pltpu.roll(g_last, LANE_GLAST - LANE_G, 1)

        qg, kg = qg_ref[...], kg_ref[...]
        f2 = fcum * LOG2E
        for h in range(HEADS):
            sl = slice(h * HEAD_DIM, (h + 1) * HEAD_DIM)
            q = fq_ref[:, sl]
            rq = lax.rsqrt(jnp.mean(q * q, axis=-1, keepdims=True) + EPS)
            k = fk_ref[:, sl]
            rk = lax.rsqrt(jnp.mean(k * k, axis=-1, keepdims=True) + EPS)
            f_col = _head_lane(f2, LANE_F + h)
            hi = f_col.astype(bf16).astype(f32)
            mid = (f_col - hi).astype(bf16).astype(f32)
            lo = f_col - hi - mid
            q_bias = jnp.where(lane == 0, hi, jnp.where(lane == 1, mid, jnp.where(lane == 2, lo,
                                                                                  jnp.where(lane < 6, 1.0, 0.0))))
            k_bias = jnp.where(lane < 3, 1.0, jnp.where(lane == 3, -hi, jnp.where(lane == 4, -mid,
                                                                                 jnp.where(lane == 5, -lo, 0.0))))
            base = 2 * h * HEAD_DIM
            qs_ref[:, base:base + HEAD_DIM] = (q * rq * qg * (QK_SCALE * LOG2E)).astype(bf16)
            qs_ref[:, base + HEAD_DIM:base + 2 * HEAD_DIM] = q_bias.astype(bf16)
            kn_ref[:, base:base + HEAD_DIM] = (k * rk * kg).astype(bf16)
            kn_ref[:, base + HEAD_DIM:base + 2 * HEAD_DIM] = k_bias.astype(bf16)

    def col(cb):
        return pl.BlockSpec((tm, WIDTH), lambda i: (i, cb))

    def halo(cb):
        return pl.BlockSpec((HALO, WIDTH), lambda i: (jnp.maximum(i * (tm // HALO) - 1, 0), cb))

    vec = pl.BlockSpec((1, LANES), lambda i: (0, 0))
    wide_f32 = jax.ShapeDtypeStruct((s_len, WIDTH), f32)
    wide_bf = jax.ShapeDtypeStruct((s_len, WIDTH), bf16)
    out_col = pl.BlockSpec((tm, WIDTH), lambda i: (i, 0))
    return pl.pallas_call(
        body, name="prep", grid=(nb,),
        in_specs=[col(0), col(1), col(2), col(4), col(5), col(6), halo(4), halo(5), halo(6),
                  pl.BlockSpec((tm, N_SMALL), lambda i: (i, 0)), vec, vec,
                  pl.BlockSpec((CONV_K, 3 * WIDTH), lambda i: (0, 0)), vec, vec],
        out_specs=[pl.BlockSpec((tm, 2 * WIDTH), lambda i: (i, 0))] * 2 + [out_col] * 4
                  + [pl.BlockSpec((tm, N_SMALL), lambda i: (i, 0))],
        out_shape=[jax.ShapeDtypeStruct((s_len, 2 * WIDTH), bf16)] * 2 + [wide_bf, wide_f32, wide_f32, wide_f32,
                                                                          jax.ShapeDtypeStruct((s_len, N_SMALL), f32)],
        scratch_shapes=[pltpu.VMEM((tm + HALO, WIDTH), f32), pltpu.VMEM((1, N_SMALL), f32)],
        compiler_params=_params("arbitrary"),
    )(p_main, p_main, p_main, p_main, p_main, p_main, p_main, p_main, p_main, p_small, qn_g, kn_g, conv_w, bvec, alog)


FOX_T = 1024
NEG_BIG = -1e30


def _fox_fwd(qs, kn, vb):
    s_len = qs.shape[0]
    t = FOX_T
    nq = s_len // t

    def body(q_ref, k_ref, v_ref, o_ref, lse_ref):
        qi = pl.program_id(1)
        q = q_ref[...]
        causal = _iota((t, t), 0) >= _iota((t, t), 1)

        def step(j, carry, masked):
            m, l, acc = carry
            rows = pl.ds(pl.multiple_of(j * t, t), t)
            s = _dg(q, k_ref[rows, :], 1, 1)
            if masked:
                s = jnp.where(causal, s, NEG_BIG)
            m_new = jnp.maximum(m, jnp.max(s, axis=-1, keepdims=True))
            p = jnp.exp2(s - m_new)
            alpha = jnp.exp2(m - m_new)
            l = alpha * l + jnp.sum(p, axis=-1, keepdims=True)
            acc = alpha * acc + jnp.dot(p.astype(bf16), v_ref[rows, :], preferred_element_type=f32)
            return m_new, l, acc

        init = (jnp.full((t, 1), NEG_BIG, f32), jnp.zeros((t, 1), f32), jnp.zeros((t, HEAD_DIM), f32))
        carry = lax.fori_loop(0, qi, lambda j, c: step(j, c, False), init)
        m, l, acc = step(qi, carry, True)
        o_ref[...] = acc / l
        lse_ref[0] = m + jnp.log2(l)

    return pl.pallas_call(
        body, name="fox_fwd", grid=(HEADS, nq),
        in_specs=[pl.BlockSpec((t, 2 * HEAD_DIM), lambda h, i: (i, h)),
                  pl.BlockSpec((s_len, 2 * HEAD_DIM), lambda h, i: (0, h)),
                  pl.BlockSpec((s_len, HEAD_DIM), lambda h, i: (0, h))],
        out_specs=[pl.BlockSpec((t, HEAD_DIM), lambda h, i: (i, h)),
                   pl.BlockSpec((1, t, 1), lambda h, i: (h, i, 0))],
        out_shape=[jax.ShapeDtypeStruct((s_len, WIDTH), f32), jax.ShapeDtypeStruct((HEADS, s_len, 1), f32)],
        compiler_params=_params("parallel", "arbitrary"),
    )(qs, kn, vb)


def _fox_bwd(qs, kn, vb, do, lse, delta):
    s_len = qs.shape[0]
    t = FOX_T
    nq = s_len // t

    def body(q_ref, do_ref, lse_ref, dl_ref, k_ref, v_ref, dq_ref, dk_ref, dv_ref, df_ref, dfq_ref):
        head, qi = pl.program_id(0), pl.program_id(1)

        @pl.when(qi == 0)
        def _():
            dk_ref[...] = jnp.zeros_like(dk_ref)
            dv_ref[...] = jnp.zeros_like(dv_ref)
            df_ref[...] = jnp.zeros_like(df_ref)

        q, do_b = q_ref[...], do_ref[...]
        q_main = q_ref[:, 0:HEAD_DIM]
        lse_col = lse_ref[0]
        dl = _head_lane(dl_ref[...], head)
        causal = _iota((t, t), 0) >= _iota((t, t), 1)

        def step(j, carry, masked):
            dq, row_sum = carry
            rows = pl.ds(pl.multiple_of(j * t, t), t)
            vj = v_ref[rows, :]
            p = jnp.exp2(_dg(q, k_ref[rows, :], 1, 1) - lse_col)
            if masked:
                p = jnp.where(causal, p, 0.0)
            ds = p * (_dg(do_b, vj, 1, 1) - dl)
            ds_b = ds.astype(bf16)
            dk_ref[rows, :] += _dg(ds_b, q_main, 0, 0)
            dv_ref[rows, :] += _dg(p.astype(bf16), do_b, 0, 0)
            df_ref[0, j] += -jnp.sum(ds, axis=0, keepdims=True)
            dq = dq + jnp.dot(ds_b, k_ref[rows, 0:HEAD_DIM], preferred_element_type=f32)
            return dq, row_sum + jnp.sum(ds, axis=-1, keepdims=True)

        carry = lax.fori_loop(0, qi, lambda j, c: step(j, c, False),
                              (jnp.zeros((t, HEAD_DIM), f32), jnp.zeros((t, 1), f32)))
        dq, row_sum = step(qi, carry, True)
        dq_ref[...] = dq
        dfq_ref[0] = row_sum

    blk = pl.BlockSpec((t, HEAD_DIM), lambda h, i: (i, h))
    blk2 = pl.BlockSpec((t, 2 * HEAD_DIM), lambda h, i: (i, h))
    full = pl.BlockSpec((s_len, HEAD_DIM), lambda h, i: (0, h))
    full2 = pl.BlockSpec((s_len, 2 * HEAD_DIM), lambda h, i: (0, h))
    colv = pl.BlockSpec((1, t, 1), lambda h, i: (h, i, 0))
    rowv = pl.BlockSpec((1, nq, 1, t), lambda h, i: (h, 0, 0, 0))
    lanes = pl.BlockSpec((t, N_SMALL), lambda h, i: (i, 0))
    wide = jax.ShapeDtypeStruct((s_len, WIDTH), f32)
    return pl.pallas_call(
        body, name="fox_bwd", grid=(HEADS, nq),
        in_specs=[blk2, blk, colv, lanes, full2, full],
        out_specs=[blk, full, full, rowv, colv],
        out_shape=[wide, wide, wide, jax.ShapeDtypeStruct((HEADS, nq, 1, t), f32),
                   jax.ShapeDtypeStruct((HEADS, s_len, 1), f32)],
        compiler_params=_params("parallel", "arbitrary"),
    )(qs, do, lse, delta, kn, vb)


INTRA_CHUNKS = 8
SCAN_FWD_CHUNKS = 8
SCAN_BWD_CHUNKS = 4


def _gdn_intra_fwd(gq, gk, gv, small):
    s_len = gq.shape[0]
    cpb = INTRA_CHUNKS
    rows_blk = cpb * CHUNK
    n_chunks = s_len // CHUNK

    def body(q_ref, k_ref, v_ref, sm_ref, u_ref, w_ref, qg_ref, kd_ref, attn_ref, t_ref, eg_ref):
        head = pl.program_id(0)
        sm = sm_ref[...]
        gc_b, gl_b, beta_b = (_head_slab(sm, LANE_GC + head), _head_slab(sm, LANE_GLAST + head),
                              _head_slab(sm, LANE_BETA + head))
        ms, rhss = [], []
        for ci in range(cpb):
            rows = pl.ds(ci * CHUNK, CHUNK)
            sl = slice(ci * CHUNK, (ci + 1) * CHUNK)
            m, rhs, qg, kd, attn, eg_last = _gdn_intra_pre(q_ref[rows, :], k_ref[rows, :], v_ref[rows, :],
                                                           gc_b[sl], gl_b[sl], beta_b[sl], _BF_PLAIN[1])
            qg_ref[rows, :] = qg.astype(bf16)
            kd_ref[rows, :] = kd.astype(bf16)
            attn_ref[0, ci] = attn.astype(bf16)
            eg_ref[0, ci] = eg_last
            ms.append(m)
            rhss.append(rhs)
        for ci, (t, rhs) in enumerate(zip(_inv_unit_lower_many(ms), rhss)):
            rows = pl.ds(ci * CHUNK, CHUNK)
            t_ref[0, ci] = t
            uw = _X3_PLAIN[0](t, rhs)
            u_ref[rows, :] = uw[:, :HEAD_DIM]
            w_ref[rows, :] = uw[:, HEAD_DIM:].astype(bf16)

    blk = pl.BlockSpec((rows_blk, HEAD_DIM), lambda h, i: (i, h))
    sq = pl.BlockSpec((1, cpb, CHUNK, CHUNK), lambda h, i: (h, i, 0, 0))
    wide_bf = jax.ShapeDtypeStruct((s_len, WIDTH), bf16)
    return pl.pallas_call(
        body, name="gdn_intra_fwd", grid=(HEADS, s_len // rows_blk),
        in_specs=[blk] * 3 + [pl.BlockSpec((rows_blk, N_SMALL), lambda h, i: (i, 0))],
        out_specs=[blk] * 4 + [sq, sq, pl.BlockSpec((1, cpb, SUBLANES, HEAD_DIM), lambda h, i: (h, i, 0, 0))],
        out_shape=[jax.ShapeDtypeStruct((s_len, WIDTH), f32), wide_bf, wide_bf, wide_bf,
                   jax.ShapeDtypeStruct((HEADS, n_chunks, CHUNK, CHUNK), bf16),
                   jax.ShapeDtypeStruct((HEADS, n_chunks, CHUNK, CHUNK), f32),
                   jax.ShapeDtypeStruct((HEADS, n_chunks, SUBLANES, HEAD_DIM), f32)],
        compiler_params=_params("parallel", "parallel"),
    )(gq, gk, gv, small)


def _gdn_scan_fwd(u, w, qg, kd, attn, eg):
    s_len = u.shape[0]
    cpb = SCAN_FWD_CHUNKS
    rows_blk = cpb * CHUNK
    n_chunks = s_len // CHUNK

    def body(u_ref, w_ref, qg_ref, kd_ref, attn_ref, eg_ref, o_ref, st_ref, s_sc):
        @pl.when(pl.program_id(0) == 0)
        def _():
            s_sc[...] = jnp.zeros_like(s_sc)

        def chunk(ci, _):
            rows = pl.ds(pl.multiple_of(ci * CHUNK, CHUNK), CHUNK)
            cols = [slice(h * HEAD_DIM, (h + 1) * HEAD_DIM) for h in range(HEADS)]
            s0 = [s_sc[h] for h in range(HEADS)]
            s0_b = [s.astype(bf16) for s in s0]
            for h in range(HEADS):
                st_ref[h, ci] = s0[h]
            ws = [jnp.dot(w_ref[rows, cols[h]], s0_b[h], preferred_element_type=f32) for h in range(HEADS)]
            qs = [jnp.dot(qg_ref[rows, cols[h]], s0_b[h], preferred_element_type=f32) for h in range(HEADS)]
            vn_b = [(u_ref[rows, cols[h]] - ws[h]).astype(bf16) for h in range(HEADS)]
            av = [jnp.dot(attn_ref[h, ci], vn_b[h], preferred_element_type=f32) for h in range(HEADS)]
            kv = [_dg(kd_ref[rows, cols[h]], vn_b[h], 0, 0) for h in range(HEADS)]
            for h in range(HEADS):
                o_ref[rows, cols[h]] = qs[h] + av[h]
                s_sc[h] = _scale_rows(s0[h], eg_ref[h, ci]) + kv[h]
            return 0

        lax.fori_loop(0, cpb, chunk, 0)

    row = pl.BlockSpec((rows_blk, WIDTH), lambda i: (i, 0))
    return pl.pallas_call(
        body, name="gdn_scan_fwd", grid=(s_len // rows_blk,),
        in_specs=[row] * 4 + [pl.BlockSpec((HEADS, cpb, CHUNK, CHUNK), lambda i: (0, i, 0, 0)),
                              pl.BlockSpec((HEADS, cpb, SUBLANES, HEAD_DIM), lambda i: (0, i, 0, 0))],
        out_specs=[row, pl.BlockSpec((HEADS, cpb, HEAD_DIM, HEAD_DIM), lambda i: (0, i, 0, 0))],
        out_shape=[jax.ShapeDtypeStruct((s_len, WIDTH), f32),
                   jax.ShapeDtypeStruct((HEADS, n_chunks, HEAD_DIM, HEAD_DIM), f32)],
        scratch_shapes=[pltpu.VMEM((HEADS, HEAD_DIM, HEAD_DIM), f32)],
        compiler_params=_params("arbitrary"),
    )(u, w, qg, kd, attn, eg)


def _gdn_scan_bwd(u, w, qg, kd, attn, eg, states, d_o):
    s_len = u.shape[0]
    cpb = SCAN_BWD_CHUNKS
    rows_blk = cpb * CHUNK
    n_chunks = s_len // CHUNK
    nb = s_len // rows_blk

    def body(u_ref, w_ref, qg_ref, kd_ref, attn_ref, eg_ref, st_ref, do_ref,
             du_ref, dw_ref, dqg_ref, dkd_ref, dattn_ref, deg_ref, ds_sc):
        @pl.when(pl.program_id(0) == 0)
        def _():
            ds_sc[...] = jnp.zeros_like(ds_sc)

        def chunk(step, _):
            ci = cpb - 1 - step
            rows = pl.ds(pl.multiple_of(ci * CHUNK, CHUNK), CHUNK)
            hs = range(HEADS)
            cols = [slice(h * HEAD_DIM, (h + 1) * HEAD_DIM) for h in hs]
            s0 = [st_ref[h, ci] for h in hs]
            s0_b = [s.astype(bf16) for s in s0]
            ds1 = [ds_sc[h] for h in hs]
            ds1_b = [d.astype(bf16) for d in ds1]
            do_b = [do_ref[rows, cols[h]].astype(bf16) for h in hs]
            ws = [jnp.dot(w_ref[rows, cols[h]], s0_b[h], preferred_element_type=f32) for h in hs]
            ad = [_dg(attn_ref[h, ci], do_b[h], 0, 0) for h in hs]
            kd_ds = [jnp.dot(kd_ref[rows, cols[h]], ds1_b[h], preferred_element_type=f32) for h in hs]
            dqg = [_dg(do_b[h], s0_b[h], 1, 1) for h in hs]
            qd = [_dg(qg_ref[rows, cols[h]], do_b[h], 0, 0) for h in hs]
            vn_b = [(u_ref[rows, cols[h]] - ws[h]).astype(bf16) for h in hs]
            dvn = [ad[h] + kd_ds[h] for h in hs]
            dvn_b = [d.astype(bf16) for d in dvn]
            dattn = [_dg(do_b[h], vn_b[h], 1, 1) for h in hs]
            dkd = [_dg(vn_b[h], ds1_b[h], 1, 1) for h in hs]
            dw = [_dg(dvn_b[h], s0_b[h], 1, 1) for h in hs]
            wd = [_dg(w_ref[rows, cols[h]], dvn_b[h], 0, 0) for h in hs]
            for h in hs:
                dattn_ref[h, ci] = dattn[h]
                dqg_ref[rows, cols[h]] = dqg[h]
                dkd_ref[rows, cols[h]] = dkd[h]
                du_ref[rows, cols[h]] = dvn[h]
                dw_ref[rows, cols[h]] = -dw[h]
                ds_sc[h] = qd[h] - wd[h] + _scale_rows(ds1[h], eg_ref[h, ci])
                deg_ref[h, ci] = jnp.sum((ds1[h] * s0[h]).reshape(HEAD_DIM // SUBLANES, SUBLANES, HEAD_DIM), axis=0)
            return 0

        lax.fori_loop(0, cpb, chunk, 0)

    row = pl.BlockSpec((rows_blk, WIDTH), lambda i: (nb - 1 - i, 0))
    sq = pl.BlockSpec((HEADS, cpb, CHUNK, CHUNK), lambda i: (0, nb - 1 - i, 0, 0))
    egs = pl.BlockSpec((HEADS, cpb, SUBLANES, HEAD_DIM), lambda i: (0, nb - 1 - i, 0, 0))
    wide = jax.ShapeDtypeStruct((s_len, WIDTH), f32)
    return pl.pallas_call(
        body, name="gdn_scan_bwd", grid=(nb,),
        in_specs=[row] * 4 + [sq, egs, pl.BlockSpec((HEADS, cpb, HEAD_DIM, HEAD_DIM), lambda i: (0, nb - 1 - i, 0, 0)), row],
        out_specs=[row] * 4 + [sq, egs],
        out_shape=[wide] * 4 + [jax.ShapeDtypeStruct((HEADS, n_chunks, CHUNK, CHUNK), f32),
                                jax.ShapeDtypeStruct((HEADS, n_chunks, SUBLANES, HEAD_DIM), f32)],
        scratch_shapes=[pltpu.VMEM((HEADS, HEAD_DIM, HEAD_DIM), f32)],
        compiler_params=_params("arbitrary"),
    )(u, w, qg, kd, attn, eg, states, d_o)


def _gdn_intra_bwd(gq, gk, gv, small, t_inv, du, dw, dqg, dkd, dattn, deg):
    s_len = gq.shape[0]
    cpb = INTRA_CHUNKS
    rows_blk = cpb * CHUNK

    def body(q_ref, k_ref, v_ref, sm_ref, t_ref, du_ref, dw_ref, dqg_ref, dkd_ref, dattn_ref, deg_ref,
             dq_ref, dk_ref, dv_ref, dsm_ref):
        head = pl.program_id(1)

        def batch(value):
            return value.reshape(cpb, CHUNK, HEAD_DIM)

        sm = sm_ref[...]
        slabs = [batch(_head_slab(sm, first + head)) for first in (LANE_GC, LANE_GLAST, LANE_BETA)]
        t_known = t_ref[0]
        _, vjp = jax.vjp(lambda q, k, v, gc, gl, b: _gdn_intra(q, k, v, gc, gl, b, t_known),
                         batch(q_ref[...]), batch(k_ref[...]), batch(v_ref[...]), *slabs)
        duw = jnp.concatenate([batch(du_ref[...]), batch(dw_ref[...])], axis=-1)
        dq, dk, dv, dgc, dgl, db = vjp((duw, batch(dqg_ref[...]), batch(dkd_ref[...]), dattn_ref[0], deg_ref[0]))
        for ref, grad in zip((dq_ref, dk_ref, dv_ref), (dq, dk, dv)):
            ref[...] = grad.reshape(rows_blk, HEAD_DIM)

        @pl.when(head == 0)
        def _():
            dsm_ref[...] = jnp.zeros_like(dsm_ref)

        lane = _iota((rows_blk, N_SMALL), 1)
        acc = dsm_ref[...]
        for first, grad in ((LANE_GC, dgc), (LANE_GLAST, dgl), (LANE_BETA, db)):
            col = jnp.sum(grad.reshape(rows_blk, HEAD_DIM), axis=1, keepdims=True)
            acc = acc + jnp.where(lane == first + head, col, 0.0)
        dsm_ref[...] = acc

    blk = pl.BlockSpec((rows_blk, HEAD_DIM), lambda i, h: (i, h))
    sq = pl.BlockSpec((1, cpb, CHUNK, CHUNK), lambda i, h: (h, i, 0, 0))
    egs = pl.BlockSpec((1, cpb, SUBLANES, HEAD_DIM), lambda i, h: (h, i, 0, 0))
    lanes = pl.BlockSpec((rows_blk, N_SMALL), lambda i, h: (i, 0))
    wide = jax.ShapeDtypeStruct((s_len, WIDTH), f32)
    return pl.pallas_call(
        body, name="gdn_intra_bwd", grid=(s_len // rows_blk, HEADS),
        in_specs=[blk] * 3 + [lanes, sq] + [blk] * 4 + [sq, egs],
        out_specs=[blk] * 3 + [lanes],
        out_shape=[wide] * 3 + [jax.ShapeDtypeStruct((s_len, N_SMALL), f32)],
        compiler_params=_params("parallel", "arbitrary"),
    )(gq, gk, gv, small, t_inv, du, dw, dqg, dkd, dattn, deg)


MIX_TM = 256


def _mix_fwd(fox_o, gdn_o, p_main, gnorm_g):
    s_len = fox_o.shape[0]
    tm = MIX_TM

    def body(fo_ref, go_ref, fz_ref, gz_ref, g_ref, mixed_ref):
        fz = fz_ref[...]
        mixed_ref[:, 0:WIDTH] = (fo_ref[...] * (fz * _sigmoid(fz))).astype(bf16)
        gz = gz_ref[...]
        gate = gz * _sigmoid(gz)
        gg = g_ref[...]
        for h in range(HEADS):
            sl = slice(h * HEAD_DIM, (h + 1) * HEAD_DIM)
            o = go_ref[:, sl]
            r = lax.rsqrt(jnp.mean(o * o, axis=-1, keepdims=True) + EPS)
            mixed_ref[:, WIDTH + h * HEAD_DIM:WIDTH + (h + 1) * HEAD_DIM] = (o * r * gg * gate[:, sl]).astype(bf16)

    row = pl.BlockSpec((tm, WIDTH), lambda i: (i, 0))
    return pl.pallas_call(
        body, name="mix_fwd", grid=(s_len // tm,),
        in_specs=[row, row, pl.BlockSpec((tm, WIDTH), lambda i: (i, 3)), pl.BlockSpec((tm, WIDTH), lambda i: (i, 7)),
                  pl.BlockSpec((1, LANES), lambda i: (0, 0))],
        out_specs=pl.BlockSpec((tm, 2 * WIDTH), lambda i: (i, 0)),
        out_shape=jax.ShapeDtypeStruct((s_len, 2 * WIDTH), bf16),
        compiler_params=_params("parallel"),
    )(fox_o, gdn_o, p_main, p_main, gnorm_g)


def _silu_grad(z):
    sg = _sigmoid(z)
    return sg * (1.0 + z * (1.0 - sg))


def _mix_bwd(dmixed, fox_o, gdn_o, p_main, gnorm_g):
    s_len = fox_o.shape[0]
    tm = MIX_TM

    def body(dm_ref, fo_ref, go_ref, fz_ref, gz_ref, g_ref, dof_ref, delta_ref, dfz_ref, dgz_ref, dgo_ref, dg_ref):
        @pl.when(pl.program_id(0) == 0)
        def _():
            dg_ref[...] = jnp.zeros_like(dg_ref)

        lane = _iota((tm, LANES), 1)
        fz = fz_ref[...]
        dmf = dm_ref[:, 0:WIDTH]
        fo = fo_ref[...]
        dof = dmf * (fz * _sigmoid(fz))
        dof_ref[...] = dof.astype(bf16)
        dfz_ref[...] = (dmf * fo * _silu_grad(fz)).astype(bf16)
        prod = dof * fo
        delta = jnp.zeros((tm, LANES), f32)
        for h in range(HEADS):
            dh = jnp.sum(prod[:, h * HEAD_DIM:(h + 1) * HEAD_DIM], axis=-1, keepdims=True)
            delta = jnp.where(lane == h, dh, delta)
        delta_ref[...] = delta

        gz = gz_ref[...]
        dmg = dm_ref[:, WIDTH:2 * WIDTH]
        gate = gz * _sigmoid(gz)
        sgrad = _silu_grad(gz)
        gg = g_ref[...]
        dg_acc = jnp.zeros((1, HEAD_DIM), f32)
        for h in range(HEADS):
            sl = slice(h * HEAD_DIM, (h + 1) * HEAD_DIM)
            o = go_ref[:, sl]
            r = lax.rsqrt(jnp.mean(o * o, axis=-1, keepdims=True) + EPS)
            on = o * r
            dmh = dmg[:, sl]
            dgz_ref[:, sl] = (dmh * (on * gg) * sgrad[:, sl]).astype(bf16)
            dy = dmh * gate[:, sl]
            dg_acc = dg_acc + jnp.sum(dy * on, axis=0, keepdims=True)
            tt = dy * gg
            dgo_ref[:, sl] = r * (tt - on * jnp.mean(tt * on, axis=-1, keepdims=True))
        dg_ref[...] += dg_acc

    row = pl.BlockSpec((tm, WIDTH), lambda i: (i, 0))
    wide_bf = jax.ShapeDtypeStruct((s_len, WIDTH), bf16)
    return pl.pallas_call(
        body, name="mix_bwd", grid=(s_len // tm,),
        in_specs=[pl.BlockSpec((tm, 2 * WIDTH), lambda i: (i, 0)), row, row,
                  pl.BlockSpec((tm, WIDTH), lambda i: (i, 3)), pl.BlockSpec((tm, WIDTH), lambda i: (i, 7)),
                  pl.BlockSpec((1, LANES), lambda i: (0, 0))],
        out_specs=[row, pl.BlockSpec((tm, LANES), lambda i: (i, 0)), row, row, row,
                   pl.BlockSpec((1, LANES), lambda i: (0, 0))],
        out_shape=[wide_bf, jax.ShapeDtypeStruct((s_len, LANES), f32), wide_bf, wide_bf,
                   jax.ShapeDtypeStruct((s_len, WIDTH), f32), jax.ShapeDtypeStruct((1, LANES), f32)],
        compiler_params=_params("arbitrary"),
    )(dmixed, fox_o, gdn_o, p_main, p_main, gnorm_g)


def _out_head(mixed, w_out, x, target, gate, final_g):
    s_len = x.shape[0]
    tm = 256

    def body(mx_ref, w_ref, x_ref, t_ref, gate_ref, fg_ref, loss_ref, dy_ref, dz_ref, dm_ref, dfg_ref, dgate_ref):
        @pl.when(pl.program_id(0) == 0)
        def _():
            loss_ref[...] = jnp.zeros_like(loss_ref)
            dfg_ref[...] = jnp.zeros_like(dfg_ref)
            dgate_ref[...] = jnp.zeros_like(dgate_ref)

        w = w_ref[...]
        z = jnp.dot(mx_ref[...], w, preferred_element_type=f32)
        gate_v, fg = gate_ref[...], fg_ref[...]
        y1 = x_ref[...] + gate_v * z
        r = lax.rsqrt(jnp.mean(y1 * y1, axis=-1, keepdims=True) + EPS)
        yn = y1 * r
        err = yn * fg - t_ref[...]
        loss_ref[...] += 0.5 * jnp.sum(jnp.mean(err * err, axis=-1, keepdims=True))
        dout = err * (1.0 / D_MODEL)
        dfg_ref[...] += jnp.sum(dout * yn, axis=0, keepdims=True)
        tt = dout * fg
        dy1 = r * (tt - yn * jnp.mean(tt * yn, axis=-1, keepdims=True))
        dy_ref[...] = dy1
        dgate_ref[...] += jnp.sum(dy1 * z, axis=0, keepdims=True)
        dz = (dy1 * gate_v).astype(bf16)
        dz_ref[...] = dz
        dm_ref[...] = _dg(dz, w, 1, 1)

    row = pl.BlockSpec((tm, D_MODEL), lambda i: (i, 0))
    vec = pl.BlockSpec((1, D_MODEL), lambda i: (0, 0))
    big = jax.ShapeDtypeStruct((s_len, D_MODEL), f32)
    return pl.pallas_call(
        body, name="out_head", grid=(s_len // tm,),
        in_specs=[row, pl.BlockSpec((D_MODEL, D_MODEL), lambda i: (0, 0)), row, row, vec, vec],
        out_specs=[pl.BlockSpec((1, LANES), lambda i: (0, 0)), row, row, row, vec, vec],
        out_shape=[jax.ShapeDtypeStruct((1, LANES), f32), big, jax.ShapeDtypeStruct((s_len, D_MODEL), bf16), big,
                   jax.ShapeDtypeStruct((1, D_MODEL), f32), jax.ShapeDtypeStruct((1, D_MODEL), f32)],
        compiler_params=_params("arbitrary"),
    )(mixed, w_out, x, target, gate, final_g)


def _matmul_tn(name, a, b, out_dtype):
    k_len, m_len = a.shape
    n_len = b.shape[1]
    tk, tm, tn = min(2048, k_len), min(1024, m_len), min(1024, n_len)
    nk = k_len // tk

    def body(a_ref, b_ref, o_ref, acc_sc):
        k = pl.program_id(2)

        @pl.when(k == 0)
        def _():
            acc_sc[...] = jnp.zeros_like(acc_sc)

        acc_sc[...] += _dg(a_ref[...], b_ref[...], 0, 0)

        @pl.when(k == nk - 1)
        def _():
            o_ref[...] = acc_sc[...].astype(out_dtype)

    return pl.pallas_call(
        body, name=name, grid=(m_len // tm, n_len // tn, nk),
        in_specs=[pl.BlockSpec((tk, tm), lambda i, j, k: (k, i)), pl.BlockSpec((tk, tn), lambda i, j, k: (k, j))],
        out_specs=pl.BlockSpec((tm, tn), lambda i, j, k: (i, j)),
        out_shape=jax.ShapeDtypeStruct((m_len, n_len), out_dtype),
        scratch_shapes=[pltpu.VMEM((tm, tn), f32)],
        compiler_params=_params("parallel", "parallel", "arbitrary"),
    )(a, b)


def _post1(p_main, p_small, qn_g, kn_g, conv_w, bvec, alog, dqs, dkn, dgq, dgk, dgv, d_small, df, df_query):
    s_len = p_main.shape[0]
    tm = PREP_TM
    nb = s_len // tm

    def body(fq_ref, fk_ref, gq_ref, gk_ref, gv_ref, hq_ref, hk_ref, hv_ref, ps_ref, qg_ref, kg_ref, cw_ref, bv_ref,
             al_ref, dqs_ref, dkn_ref, dgq_ref, dgk_ref, dgv_ref, dsm_ref, df_ref, dfq_in_ref,
             dfq_ref, dfk_ref, dconv_ref, dps_ref, dqg_ref, dkg_ref, sums_ref, xe_sc, carry_sc):
        step = pl.program_id(0)
        blk = nb - 1 - step

        @pl.when(step == 0)
        def _():
            carry_sc[...] = jnp.zeros_like(carry_sc)
            dqg_ref[...] = jnp.zeros_like(dqg_ref)
            dkg_ref[...] = jnp.zeros_like(dkg_ref)
            sums_ref[...] = jnp.zeros_like(sums_ref)

        for x_ref, g_ref, dy_ref, o_ref, acc_ref, mul in ((fq_ref, qg_ref, dqs_ref, dfq_ref, dqg_ref, QK_SCALE),
                                                          (fk_ref, kg_ref, dkn_ref, dfk_ref, dkg_ref, LN2)):
            gain = g_ref[...]
            acc = jnp.zeros((1, HEAD_DIM), f32)
            for h in range(HEADS):
                sl = slice(h * HEAD_DIM, (h + 1) * HEAD_DIM)
                xv = x_ref[:, sl]
                r = lax.rsqrt(jnp.mean(xv * xv, axis=-1, keepdims=True) + EPS)
                xn = xv * r
                dy = dy_ref[:, sl] * mul
                acc = acc + jnp.sum(dy * xn, axis=0, keepdims=True)
                tt = dy * gain
                o_ref[:, sl] = (r * (tt - xn * jnp.mean(tt * xn, axis=-1, keepdims=True))).astype(bf16)
            acc_ref[...] += acc

        first = blk == 0
        for sec, (x_ref, halo_ref, dy_ref) in enumerate(((gq_ref, hq_ref, dgq_ref), (gk_ref, hk_ref, dgk_ref),
                                                         (gv_ref, hv_ref, dgv_ref))):
            xe_sc[0:HALO, :] = jnp.where(first, 0.0, halo_ref[...])
            xe_sc[HALO:, :] = x_ref[...]
            cv = _conv_section(xe_sc, cw_ref, slice(sec * WIDTH, (sec + 1) * WIDTH), tm)
            sgrad = _silu_grad(cv)
            if sec == 2:
                dconv_ref[:, sec * WIDTH:(sec + 1) * WIDTH] = dy_ref[...] * sgrad
            else:
                y = cv * _sigmoid(cv)
                mul = QK_SCALE if sec == 0 else 1.0
                for h in range(HEADS):
                    sl = slice(h * HEAD_DIM, (h + 1) * HEAD_DIM)
                    yh = y[:, sl]
                    r = lax.rsqrt(jnp.sum(yh * yh, axis=-1, keepdims=True) + EPS)
                    dqh = dy_ref[:, sl]
                    dyh = (mul * r) * (dqh - yh * (r * r) * jnp.sum(dqh * yh, axis=-1, keepdims=True))
                    dconv_ref[:, sec * WIDTH + h * HEAD_DIM:sec * WIDTH + (h + 1) * HEAD_DIM] = dyh * sgrad[:, sl]

        lane = _iota((tm, N_SMALL), 1)
        z, _, gval, beta = _small_fwd(ps_ref[...], bv_ref[...], al_ref[...])
        sig_z = _sigmoid(z)
        dsm = dsm_ref[...]
        in_g = (lane >= LANE_G) & (lane < LANE_G + HEADS)
        dgc = jnp.where(in_g, pltpu.roll(dsm, N_SMALL - (LANE_GC - LANE_G), 1), 0.0)
        dgl = jnp.where(in_g, pltpu.roll(dsm, N_SMALL - (LANE_GLAST - LANE_G), 1), 0.0)
        tri_c, ones_c = _chunk_masks(tm)
        dg = (_dg(tri_c, dgc, 0, 0, HI) + jnp.dot(ones_c, dgl, preferred_element_type=f32, precision=HI))
        dbeta = dsm
        dfb = jnp.where(lane < HEADS, df_ref[...], 0.0)
        for h in range(HEADS):
            dfb = dfb + jnp.where(lane == h, dfq_in_ref[h], 0.0)
        tri_u = (_iota((tm, tm), 1) >= _iota((tm, tm), 0)).astype(f32)
        dlogf = jnp.dot(tri_u, dfb, preferred_element_type=f32, precision=HI) + carry_sc[...]
        carry_sc[...] += jnp.sum(dfb, axis=0, keepdims=True)
        dff = dlogf * (1.0 - sig_z)
        dga = dg * (-jnp.exp(al_ref[...])) * sig_z
        dgb_small = dbeta * beta * (1.0 - beta)
        dps = jnp.where(lane < HEADS, dff, jnp.where(lane < 2 * HEADS, dga, jnp.where(lane < 3 * HEADS, dgb_small, 0.0)))
        dps_ref[...] = dps.astype(bf16)
        row = _iota((8, N_SMALL), 0)
        s0 = jnp.sum(dps, axis=0, keepdims=True)
        s1 = jnp.sum(jnp.where((lane >= HEADS) & (lane < 2 * HEADS), dg * gval, 0.0), axis=0, keepdims=True)
        sums_ref[...] += jnp.where(row == 0, s0, jnp.where(row == 1, s1, 0.0))

    def col(cb):
        return pl.BlockSpec((tm, WIDTH), lambda i: (nb - 1 - i, cb))

    def halo(cb):
        return pl.BlockSpec((HALO, WIDTH), lambda i: (jnp.maximum((nb - 1 - i) * (tm // HALO) - 1, 0), cb))

    vec = pl.BlockSpec((1, LANES), lambda i: (0, 0))
    row0 = pl.BlockSpec((tm, WIDTH), lambda i: (nb - 1 - i, 0))
    small = pl.BlockSpec((tm, N_SMALL), lambda i: (nb - 1 - i, 0))
    wide_bf = jax.ShapeDtypeStruct((s_len, WIDTH), bf16)
    return pl.pallas_call(
        body, name="post1", grid=(nb,),
        in_specs=[col(0), col(1), col(4), col(5), col(6), halo(4), halo(5), halo(6), small, vec, vec,
                  pl.BlockSpec((CONV_K, 3 * WIDTH), lambda i: (0, 0)), vec, vec,
                  row0, row0, row0, row0, row0, small, small,
                  pl.BlockSpec((HEADS, tm, 1), lambda i: (0, nb - 1 - i, 0))],
        out_specs=[row0, row0, pl.BlockSpec((tm, 3 * WIDTH), lambda i: (nb - 1 - i, 0)), small, vec, vec,
                   pl.BlockSpec((8, N_SMALL), lambda i: (0, 0))],
        out_shape=[wide_bf, wide_bf, jax.ShapeDtypeStruct((s_len, 3 * WIDTH), f32),
                   jax.ShapeDtypeStruct((s_len, N_SMALL), bf16), jax.ShapeDtypeStruct((1, LANES), f32),
                   jax.ShapeDtypeStruct((1, LANES), f32), jax.ShapeDtypeStruct((8, N_SMALL), f32)],
        scratch_shapes=[pltpu.VMEM((tm + HALO, WIDTH), f32), pltpu.VMEM((1, N_SMALL), f32)],
        compiler_params=_params("arbitrary"),
    )(p_main, p_main, p_main, p_main, p_main, p_main, p_main, p_main, p_small, qn_g, kn_g, conv_w, bvec, alog,
      dqs, dkn, dgq, dgk, dgv, d_small, df, df_query)


def _post2(p_main, dconv, conv_w):
    s_len = p_main.shape[0]
    tm = PREP_TM
    nb = s_len // tm

    def body(gq_ref, gk_ref, gv_ref, hq_ref, hk_ref, hv_ref, dc_ref, dnext_ref, cw_ref, dx_ref, dw_ref, xe_sc, de_sc):
        i = pl.program_id(0)

        @pl.when(i == 0)
        def _():
            dw_ref[...] = jnp.zeros_like(dw_ref)

        first, last = i == 0, i == nb - 1
        row = _iota((8, WIDTH), 0)
        for sec, (x_ref, halo_ref) in enumerate(((gq_ref, hq_ref), (gk_ref, hk_ref), (gv_ref, hv_ref))):
            cols = slice(sec * WIDTH, (sec + 1) * WIDTH)
            dc = dc_ref[:, cols]
            de_sc[0:tm, :] = dc
            de_sc[tm:, :] = jnp.where(last, 0.0, dnext_ref[:, cols])
            dx = cw_ref[pl.ds(CONV_K - 1, 1), cols] * dc
            for tap in range(CONV_K - 1):
                dx = dx + cw_ref[pl.ds(tap, 1), cols] * de_sc[pl.ds(CONV_K - 1 - tap, tm), :]
            dx_ref[:, cols] = dx.astype(bf16)
            xe_sc[0:HALO, :] = jnp.where(first, 0.0, halo_ref[...])
            xe_sc[HALO:, :] = x_ref[...]
            dw = jnp.zeros((8, WIDTH), f32)
            for tap in range(CONV_K):
                contrib = jnp.sum(dc * xe_sc[pl.ds(HALO - (CONV_K - 1) + tap, tm), :], axis=0, keepdims=True)
                dw = jnp.where(row == tap, contrib, dw)
            dw_ref[:, cols] += dw

    def col(cb):
        return pl.BlockSpec((tm, WIDTH), lambda i: (i, cb))

    def halo(cb):
        return pl.BlockSpec((HALO, WIDTH), lambda i: (jnp.maximum(i * (tm // HALO) - 1, 0), cb))

    return pl.pallas_call(
        body, name="post2", grid=(nb,),
        in_specs=[col(4), col(5), col(6), halo(4), halo(5), halo(6),
                  pl.BlockSpec((tm, 3 * WIDTH), lambda i: (i, 0)),
                  pl.BlockSpec((HALO, 3 * WIDTH), lambda i: (jnp.minimum((i + 1) * (tm // HALO), s_len // HALO - 1), 0)),
                  pl.BlockSpec((CONV_K, 3 * WIDTH), lambda i: (0, 0))],
        out_specs=[pl.BlockSpec((tm, 3 * WIDTH), lambda i: (i, 0)), pl.BlockSpec((8, 3 * WIDTH), lambda i: (0, 0))],
        out_shape=[jax.ShapeDtypeStruct((s_len, 3 * WIDTH), bf16), jax.ShapeDtypeStruct((8, 3 * WIDTH), f32)],
        scratch_shapes=[pltpu.VMEM((tm + HALO, WIDTH), f32), pltpu.VMEM((tm + HALO, WIDTH), f32)],
        compiler_params=_params("arbitrary"),
    )(p_main, p_main, p_main, p_main, p_main, p_main, dconv, dconv, conv_w)


def _in_proj_bwd(dp_main, dp_small, wt_main, wt_small, x, dy1, norm_g, scale1p):
    s_len = x.shape[0]
    tm, tk = 512, 1024
    nk = N_MAIN // tk

    def body(dp_ref, dps_ref, w_ref, ws_ref, x_ref, dy_ref, g_ref, sc_ref, dx_ref, dsh_ref, dsc_ref, dg_ref, acc_sc):
        i, k = pl.program_id(0), pl.program_id(1)

        @pl.when((i == 0) & (k == 0))
        def _():
            dsh_ref[...] = jnp.zeros_like(dsh_ref)
            dsc_ref[...] = jnp.zeros_like(dsc_ref)
            dg_ref[...] = jnp.zeros_like(dg_ref)

        @pl.when(k == 0)
        def _():
            acc_sc[...] = jnp.dot(dps_ref[...], ws_ref[...], preferred_element_type=f32)

        acc_sc[...] += jnp.dot(dp_ref[...], w_ref[...], preferred_element_type=f32)

        @pl.when(k == nk - 1)
        def _():
            dh = acc_sc[...]
            xb = x_ref[...]
            r = lax.rsqrt(jnp.mean(xb * xb, axis=-1, keepdims=True) + EPS)
            xr = xb * r
            gain = g_ref[...]
            dsh_ref[...] += jnp.sum(dh, axis=0, keepdims=True)
            dsc_ref[...] += jnp.sum(dh * (xr * gain), axis=0, keepdims=True)
            dxn = dh * sc_ref[...]
            dg_ref[...] += jnp.sum(dxn * xr, axis=0, keepdims=True)
            tt = dxn * gain
            dx_ref[...] = r * (tt - xr * jnp.mean(tt * xr, axis=-1, keepdims=True)) + dy_ref[...]

    row = pl.BlockSpec((tm, D_MODEL), lambda i, k: (i, 0))
    vec = pl.BlockSpec((1, D_MODEL), lambda i, k: (0, 0))
    vshape = jax.ShapeDtypeStruct((1, D_MODEL), f32)
    return pl.pallas_call(
        body, name="in_proj_bwd", grid=(s_len // tm, nk),
        in_specs=[pl.BlockSpec((tm, tk), lambda i, k: (i, k)), pl.BlockSpec((tm, N_SMALL), lambda i, k: (i, 0)),
                  pl.BlockSpec((tk, D_MODEL), lambda i, k: (k, 0)), pl.BlockSpec((N_SMALL, D_MODEL), lambda i, k: (0, 0)),
                  row, row, vec, vec],
        out_specs=[row, vec, vec, vec],
        out_shape=[jax.ShapeDtypeStruct((s_len, D_MODEL), f32), vshape, vshape, vshape],
        scratch_shapes=[pltpu.VMEM((tm, D_MODEL), f32)],
        compiler_params=_params("arbitrary", "arbitrary"),
    )(dp_main, dp_small, wt_main, wt_small, x, dy1, norm_g, scale1p)


def _adamw(name, w, g_stack, m, v, tr, tc=None):
    n_stack, rows, cols = g_stack.shape
    tc = cols if tc is None else tc

    def body(w_ref, g_ref, m_ref, v_ref, go_ref, d_ref, mo_ref, vo_ref):
        g = g_ref[0].astype(f32)
        for k in range(1, n_stack):
            g = g + g_ref[k].astype(f32)
        go_ref[0] = g
        m_new = ADAM_B1 * m_ref[0] + (1.0 - ADAM_B1) * g
        v_new = ADAM_B2 * v_ref[0] + (1.0 - ADAM_B2) * (g * g)
        mo_ref[0] = m_new
        vo_ref[0] = v_new
        m_hat = m_new / (1.0 - ADAM_B1 ** ADAM_STEP)
        v_hat = v_new / (1.0 - ADAM_B2 ** ADAM_STEP)
        d_ref[0] = -ADAM_LR * (m_hat / (jnp.sqrt(v_hat) + ADAM_EPS) + ADAM_WD * w_ref[0])

    blk = pl.BlockSpec((1, tr, tc), lambda i, j: (0, i, j))
    shape = jax.ShapeDtypeStruct((1, rows, cols), f32)
    return pl.pallas_call(
        body, name=name, grid=(rows // tr, cols // tc),
        in_specs=[blk, pl.BlockSpec((n_stack, tr, tc), lambda i, j: (0, i, j)), blk, blk],
        out_specs=[blk] * 4, out_shape=[shape] * 4,
        compiler_params=_params("parallel", "parallel"),
    )(w, g_stack, m, v)


def _w_ada_grad(c_all_t, dmod_pad):
    def body(c_ref, d_ref, o_ref):
        cv = c_ref[...]
        o_ref[...] = jnp.dot(cv * _sigmoid(cv), d_ref[...], preferred_element_type=f32, precision=HI)

    return pl.pallas_call(body, name="w_ada_grad",
                          out_shape=jax.ShapeDtypeStruct((c_all_t.shape[0], dmod_pad.shape[1]), f32),
                          compiler_params=_params())(c_all_t, dmod_pad)


SMALL_NAMES = ("norm_g", "b_ada", "b_fgate", "fox_qn_g", "fox_kn_g", "gdn_A_log", "gdn_dt_bias", "gdn_norm_g", "final_g")
SMALL_SIZES = (D_MODEL, 3 * D_MODEL, HEADS, HEAD_DIM, HEAD_DIM, HEADS, HEADS, HEAD_DIM, D_MODEL)
SMALL_PACK = 10752


def _pack(vectors, total):
    flat = jnp.concatenate([t.reshape(-1) for t in vectors])
    return jnp.pad(flat, (0, total - flat.shape[0])).reshape(1, total)


def _lanes(*pieces):
    row = jnp.zeros((LANES,), f32)
    for off, vec in pieces:
        row = lax.dynamic_update_slice(row, vec.reshape(-1).astype(f32), (off,))
    return row.reshape(1, LANES)


def kernel(x, c, norm_g, w_ada, b_ada, w_in, b_fgate, fox_qn_g, fox_kn_g, gdn_conv_w, gdn_A_log, gdn_dt_bias, gdn_norm_g, w_out, final_g, loss_target, m_norm_g, m_w_ada, m_b_ada, m_w_in, m_b_fgate, m_fox_qn_g, m_fox_kn_g, m_gdn_conv_w, m_gdn_A_log, m_gdn_dt_bias, m_gdn_norm_g, m_w_out, m_final_g, v_norm_g, v_w_ada, v_b_ada, v_w_in, v_b_fgate, v_fox_qn_g, v_fox_kn_g, v_gdn_conv_w, v_gdn_A_log, v_gdn_dt_bias, v_gdn_norm_g, v_w_out, v_final_g):
    me = _my_index()
    s_len = x.shape[1]
    nq = s_len // FOX_T
    x2 = x.reshape(s_len, D_MODEL)
    tgt = loss_target.reshape(s_len, D_MODEL)
    ada_cols = w_ada.shape[2]
    in_cols = w_in.shape[2]
    conv_cols = gdn_conv_w.shape[2]

    (c_all,) = _gather_direct("gather_c", [c])
    c_all = c_all.reshape(N_DEV, D_MODEL)
    b_shard = lax.dynamic_slice(b_ada, (0, me * ada_cols), (1, ada_cols))
    mod_mine = _mod_shard(c_all, w_ada[0], b_shard)
    wt_shard = jnp.transpose(w_in[0])
    mod_all, wt_all, w_out_all, conv_all = _gather_two_level(
        "gather_weights", [mod_mine, wt_shard.astype(bf16), w_out[0].astype(bf16), gdn_conv_w[0]])
    mod = lax.dynamic_slice(mod_all, (0, me, 0), (N_DEV, 1, ada_cols)).reshape(1, 3 * D_MODEL)
    shift, scale, gate = mod[:, :D_MODEL], mod[:, D_MODEL:2 * D_MODEL], mod[:, 2 * D_MODEL:]
    scale1p = 1.0 + scale
    wt_full = wt_all.reshape(N_DEV * in_cols, D_MODEL)
    g0 = 4 * WIDTH + HEADS
    w_main = jnp.concatenate([wt_full[:4 * WIDTH], wt_full[g0:g0 + 4 * WIDTH]], axis=0)
    w_small = jnp.concatenate([wt_full[4 * WIDTH:g0], wt_full[g0 + 4 * WIDTH:],
                               jnp.zeros((N_SMALL - 3 * HEADS, D_MODEL), bf16)], axis=0)
    w_out_full = w_out_all.reshape(2 * WIDTH, D_MODEL)
    conv_full = jnp.transpose(conv_all, (1, 0, 2)).reshape(CONV_K, 3 * WIDTH)

    qn_g, kn_g, gn_g = fox_qn_g.reshape(1, LANES), fox_kn_g.reshape(1, LANES), gdn_norm_g.reshape(1, LANES)
    bvec = _lanes((0, b_fgate), (HEADS, gdn_dt_bias))
    alog = _lanes((HEADS, gdn_A_log))
    fg = final_g.reshape(1, D_MODEL)

    p_main, p_small, h_bf = _in_proj(x2, norm_g, scale1p, shift, w_main, w_small)
    qs, kn, vb, gq, gk, gv, small = _prep(p_main, p_small, qn_g, kn_g, conv_full, bvec, alog)
    fox_o, lse = _fox_fwd(qs, kn, vb)
    gu, gw, gqg, gkd, gattn, t_inv, eg_last = _gdn_intra_fwd(gq, gk, gv, small)
    gdn_o, states = _gdn_scan_fwd(gu, gw, gqg, gkd, gattn, eg_last)
    mixed = _mix_fwd(fox_o, gdn_o, p_main, gn_g)

    loss_row, dy1, dz, dmixed, d_final_g, d_gate = _out_head(mixed, w_out_full, x2, tgt, gate, fg)
    loss = lax.psum(loss_row[0, 0], AXES)
    dw_out = _matmul_tn("dw_out", mixed, dz, bf16)
    do_fox, delta, dfz, dgz, dgdn_o, d_gn_g = _mix_bwd(dmixed, fox_o, gdn_o, p_main, gn_g)
    dqs, dkn, dvf, df_key, df_query = _fox_bwd(qs, kn, vb, do_fox, lse, delta)
    du, dw, dqg, dkd, dattn, deg = _gdn_scan_bwd(gu, gw, gqg, gkd, gattn, eg_last, states, dgdn_o)
    dgq, dgk, dgv, d_small = _gdn_intra_bwd(gq, gk, gv, small, t_inv, du, dw, dqg, dkd, dattn, deg)
    df_small = jnp.pad(jnp.transpose(df_key.reshape(HEADS, s_len)), ((0, 0), (0, N_SMALL - HEADS)))
    dfq, dfk, dconv, dp_small, d_qn_g, d_kn_g, sums = _post1(
        p_main, p_small, qn_g, kn_g, conv_full, bvec, alog, dqs, dkn, dgq, dgk, dgv, d_small, df_small, df_query)
    dgqkv, d_conv = _post2(p_main, dconv, conv_full)
    dp_main = jnp.concatenate([dfq, dfk, dvf.astype(bf16), dfz, dgqkv, dgz], axis=1)
    grad_x, d_shift, d_scale, d_norm_g = _in_proj_bwd(dp_main, dp_small, w_main, w_small, x2, dy1, norm_g, scale1p)
    dw_main = _matmul_tn("dw_main", dp_main, h_bf, bf16)
    dw_small = _matmul_tn("dw_small", dp_small, h_bf, bf16)
    dw_in_full = jnp.concatenate([dw_main[:4 * WIDTH], dw_small[:HEADS], dw_main[4 * WIDTH:],
                                  dw_small[HEADS:3 * HEADS]], axis=0)
    dw_in_parts = dw_in_full.reshape(N_DEV, in_cols, D_MODEL)
    dw_out_parts = dw_out.reshape(N_DEV, w_out.shape[1], D_MODEL)

    dmod = jnp.concatenate([d_shift, d_scale, d_gate], axis=1)
    small_grads = _pack([d_norm_g, dmod, sums[0, :HEADS], d_qn_g, d_kn_g, sums[1, HEADS:2 * HEADS],
                         sums[0, HEADS:2 * HEADS], d_gn_g, d_final_g], SMALL_PACK)
    conv_grad = d_conv[:CONV_K]
    pair_in, pair_out = _pair_exchange("pair_grads", [dw_in_parts, dw_out_parts])
    core = lax.axis_index("c").astype(jnp.int32).reshape(1)
    dw_in_recv, dw_out_recv = _chip_exchange(
        "chip_grads", [_pair_sum("pair_sum_w_in", dw_in_parts, pair_in, core),
                       _pair_sum("pair_sum_w_out", dw_out_parts, pair_out, core)])
    small_all, conv_all_g = _gather_direct("gather_small_grads", [small_grads, conv_grad])

    outs = {}
    to_t = lambda t: jnp.transpose(t, (0, 2, 1))
    outs["w_in"] = tuple(to_t(t) for t in _adamw("adamw_w_in", to_t(w_in), dw_in_recv, to_t(m_w_in), to_t(v_w_in),
                                                  in_cols, 256))
    outs["w_out"] = _adamw("adamw_w_out", w_out, dw_out_recv, m_w_out, v_w_out, 128)
    conv_mine = lax.dynamic_slice(jnp.transpose(conv_all_g.reshape(N_DEV, CONV_K, N_DEV, conv_cols), (0, 2, 1, 3)),
                                  (0, me, 0, 0), (N_DEV, 1, CONV_K, conv_cols)).reshape(N_DEV, CONV_K, conv_cols)
    outs["gdn_conv_w"] = _adamw("adamw_conv", gdn_conv_w, conv_mine, m_gdn_conv_w, v_gdn_conv_w, CONV_K)
    small_all = small_all.reshape(N_DEV, 1, SMALL_PACK)
    dmod_all = small_all[:, 0, D_MODEL:D_MODEL + 3 * D_MODEL]
    dmod_mine = lax.dynamic_slice(dmod_all, (0, me * ada_cols), (N_DEV, ada_cols))
    c_all_t = jnp.pad(jnp.transpose(c_all), ((0, 0), (0, LANES - N_DEV)))
    g_w_ada = _w_ada_grad(c_all_t, jnp.pad(dmod_mine, ((0, LANES - N_DEV), (0, 0))))
    outs["w_ada"] = _adamw("adamw_w_ada", w_ada, g_w_ada[None], m_w_ada, v_w_ada, 256)
    given = dict(norm_g=(norm_g, m_norm_g, v_norm_g), b_ada=(b_ada, m_b_ada, v_b_ada), b_fgate=(b_fgate, m_b_fgate, v_b_fgate),
                 fox_qn_g=(fox_qn_g, m_fox_qn_g, v_fox_qn_g), fox_kn_g=(fox_kn_g, m_fox_kn_g, v_fox_kn_g),
                 gdn_A_log=(gdn_A_log, m_gdn_A_log, v_gdn_A_log), gdn_dt_bias=(gdn_dt_bias, m_gdn_dt_bias, v_gdn_dt_bias),
                 gdn_norm_g=(gdn_norm_g, m_gdn_norm_g, v_gdn_norm_g), final_g=(final_g, m_final_g, v_final_g))
    w_pack = _pack([given[n][0] for n in SMALL_NAMES], SMALL_PACK)
    m_pack = _pack([given[n][1] for n in SMALL_NAMES], SMALL_PACK)
    v_pack = _pack([given[n][2] for n in SMALL_NAMES], SMALL_PACK)
    packed = _adamw("adamw_small", w_pack[None], small_all, m_pack[None], v_pack[None], 1)
    off = 0
    for n, size in zip(SMALL_NAMES, SMALL_SIZES):
        outs[n] = tuple(t[0, 0, off:off + size].reshape(given[n][0].shape) for t in packed)
        off += size

    order = ("norm_g", "w_ada", "b_ada", "w_in", "b_fgate", "fox_qn_g", "fox_kn_g", "gdn_conv_w", "gdn_A_log",
             "gdn_dt_bias", "gdn_norm_g", "w_out", "final_g")
    result = [loss, grad_x.reshape(x.shape)]
    for part in range(4):
        result += [outs[n][part] for n in order]
    return tuple(result)
```

```python
import math

import jax
import jax.numpy as jnp
from jax import lax
from jax.experimental import pallas as pl
from jax.experimental.pallas import tpu as pltpu

f32 = jnp.float32
bf16 = jnp.bfloat16
HI = lax.Precision.HIGHEST

N_DEV = 8
AXES = ("x", "y", "c")
D_MODEL = 2048
HEADS = 8
HEAD_DIM = 128
WIDTH = HEADS * HEAD_DIM
CHUNK = 64
CONV_K = 4
EPS = 1e-6
QK_SCALE = HEAD_DIM ** -0.5
LOG2E = 1.0 / math.log(2.0)
LN2 = math.log(2.0)
N_MAIN = 8 * WIDTH
N_SMALL = 128
LANE_F, LANE_G, LANE_BETA, LANE_GC, LANE_GLAST = 0, 8, 16, 24, 32
IN_WIDTH = 8 * WIDTH + 3 * HEADS
LANES = 128
VMEM_LIMIT = 56 * 1024 * 1024

ADAM_LR, ADAM_B1, ADAM_B2, ADAM_EPS, ADAM_WD, ADAM_STEP = 0.001, 0.9, 0.999, 1e-08, 0.01, 10


def _params(*sem):
    return pltpu.CompilerParams(dimension_semantics=sem, vmem_limit_bytes=VMEM_LIMIT)


def _iota(shape, dim):
    return lax.broadcasted_iota(jnp.int32, shape, dim)


def _sigmoid(z):
    return 1.0 / (1.0 + jnp.exp(-z))


def _softplus_parts(z):
    t = jnp.log(1.0 + jnp.exp(-jnp.abs(z)))
    return jnp.minimum(z, 0.0) - t, jnp.maximum(z, 0.0) + t


def _dg(a, b, ca, cb, prec=None):
    if a.ndim == 3:
        dims = (((ca + 1,), (cb + 1,)), ((0,), (0,)))
    else:
        dims = (((ca,), (cb,)), ((), ()))
    return lax.dot_general(a, b, dims, preferred_element_type=f32, precision=prec)


def _dot_bf16(a, b, ca, cb):
    return _dg(a.astype(bf16), b.astype(bf16), ca, cb)


def _split_bf16(a):
    hi = a.astype(bf16)
    return hi, (a - hi.astype(f32)).astype(bf16)


def _dot_3pass(a, b, ca, cb):
    a_hi, a_lo = _split_bf16(a)
    b_hi, b_lo = _split_bf16(b)
    return _dg(a_hi, b_hi, ca, cb) + (_dg(a_hi, b_lo, ca, cb) + _dg(a_lo, b_hi, ca, cb))


def _make_mm(dot):
    def nn_(a, b):
        return dot(a, b, 1, 0)

    def nt_(a, b):
        return dot(a, b, 1, 1)

    def tn_(a, b):
        return dot(a, b, 0, 0)

    @jax.custom_vjp
    def nn(a, b):
        return nn_(a, b)

    @jax.custom_vjp
    def nt(a, b):
        return nt_(a, b)

    @jax.custom_vjp
    def tn(a, b):
        return tn_(a, b)

    nn.defvjp(lambda a, b: (nn_(a, b), (a, b)), lambda r, g: (nt_(g, r[1]), tn_(r[0], g)))
    nt.defvjp(lambda a, b: (nt_(a, b), (a, b)), lambda r, g: (nn_(g, r[1]), tn_(g, r[0])))
    tn.defvjp(lambda a, b: (tn_(a, b), (a, b)), lambda r, g: (nt_(r[1], g), nn_(r[0], g)))
    return (nn_, nt_, tn_), (nn, nt, tn)


_BF_PLAIN, _BF_VJP = _make_mm(_dot_bf16)
_X3_PLAIN, _X3_VJP = _make_mm(_dot_3pass)


def _inv_unit_lower_many(ms):
    c = CHUNK
    nn = _X3_PLAIN[0]
    eye = (_iota((c, c), 0) == _iota((c, c), 1)).astype(f32)
    top = _iota((2 * c, c), 0) < c
    xs = [jnp.concatenate([eye - m, nn(m, m)], axis=0) for m in ms]
    for _ in range(int(math.log2(CHUNK)) - 2):
        xs = [jnp.where(top, x, 0.0) + nn(x, x[c:]) for x in xs]
    return [x[:c] + nn(x[:c], x[c:]) for x in xs]


@jax.custom_vjp
def _inv_given(m, t):
    return t


_inv_given.defvjp(lambda m, t: (t, t),
                  lambda t, g: (-_X3_PLAIN[1](_X3_PLAIN[2](t, g), t), jnp.zeros_like(t)))

SUBLANES = 8


def _gdn_intra_pre(q, k, v, gc_b, g_last_b, beta_b, bnt):
    c = CHUNK
    r_i, c_i = _iota((c, c), 0), _iota((c, c), 1)
    lower, strict = r_i >= c_i, r_i > c_i
    gc_i = gc_b[..., :c]
    gc_j = jnp.swapaxes(gc_i, -1, -2)
    decay = jnp.where(lower, jnp.exp(jnp.where(lower, gc_i - gc_j, 0.0)), 0.0)
    kb = k * beta_b
    both = bnt(jnp.concatenate([kb, q], axis=-2), k)
    m = jnp.where(strict, both[..., :c, :] * decay, 0.0)
    attn = jnp.where(lower, both[..., c:, :] * decay, 0.0)
    eg = jnp.exp(gc_b)
    rhs = jnp.concatenate([v * beta_b, kb * eg], axis=-1)
    k_dec = k * jnp.exp(g_last_b - gc_b)
    eg_last = jnp.exp(g_last_b[..., :SUBLANES, :])
    return m, rhs, q * eg, k_dec, attn, eg_last


def _gdn_intra(q, k, v, gc_b, g_last_b, beta_b, t_known):
    m, rhs, qg, k_dec, attn, eg_last = _gdn_intra_pre(q, k, v, gc_b, g_last_b, beta_b, _BF_VJP[1])
    return _X3_VJP[0](_inv_given(m, t_known), rhs), qg, k_dec, attn, eg_last


def _scale_rows(s, eg_last):
    return (s.reshape(HEAD_DIM // SUBLANES, SUBLANES, HEAD_DIM) * eg_last[None]).reshape(HEAD_DIM, HEAD_DIM)


def _my_index():
    return 4 * lax.axis_index("x") + 2 * lax.axis_index("y") + lax.axis_index("c")


def _peer(d):
    x, y, c = lax.axis_index("x"), lax.axis_index("y"), lax.axis_index("c")
    px, py, pc = (x + (d >> 2)) % 2, (y + ((d >> 1) & 1)) % 2, (c + (d & 1)) % 2
    return (px, py, pc), 4 * px + 2 * py + pc


def _gather_direct(name, arrays):
    n = len(arrays)

    def body(*refs):
        srcs, dsts = refs[:n], refs[n:2 * n]
        send_sems, recv_sems, local_sems = refs[2 * n:]
        me = _my_index()

        def copy(k, d, started):
            peer, pidx = _peer(d)
            return pltpu.make_async_remote_copy(
                src_ref=srcs[k], dst_ref=dsts[k].at[me if started else pidx], send_sem=send_sems.at[k * 7 + d - 1],
                recv_sem=recv_sems.at[k * 7 + d - 1], device_id=peer, device_id_type=pl.DeviceIdType.MESH)

        local = [pltpu.make_async_copy(srcs[k], dsts[k].at[me], local_sems.at[k]) for k in range(n)]
        sends = [copy(k, d, True) for k in range(n) for d in range(1, N_DEV)]
        for cp in local + sends:
            cp.start()
        for k in range(n):
            for d in range(1, N_DEV):
                copy(k, d, False).wait_recv()
        for cp in sends:
            cp.wait_send()
        for cp in local:
            cp.wait()

    out_shape = [jax.ShapeDtypeStruct((N_DEV,) + a.shape, a.dtype) for a in arrays]
    any_spec = pl.BlockSpec(memory_space=pl.ANY)
    return pl.pallas_call(
        body, name=name, out_shape=out_shape, in_specs=[any_spec] * n, out_specs=[any_spec] * n,
        scratch_shapes=[pltpu.SemaphoreType.DMA((7 * n,)), pltpu.SemaphoreType.DMA((7 * n,)),
                        pltpu.SemaphoreType.DMA((n,))],
        compiler_params=pltpu.CompilerParams(has_side_effects=True),
    )(*arrays)


N_CHIPS = 4


def _pair_exchange(name, arrays):
    n = len(arrays)

    def body(*refs):
        srcs, dsts = refs[:n], refs[n:2 * n]
        send_sems, recv_sems = refs[2 * n:]
        x, y, c = lax.axis_index("x"), lax.axis_index("y"), lax.axis_index("c")
        sibling = (x, y, 1 - c)

        def copy(k, j):
            return pltpu.make_async_remote_copy(
                src_ref=srcs[k].at[2 * j + (1 - c)], dst_ref=dsts[k].at[j], send_sem=send_sems.at[k * N_CHIPS + j],
                recv_sem=recv_sems.at[k * N_CHIPS + j], device_id=sibling, device_id_type=pl.DeviceIdType.MESH)

        copies = [copy(k, j) for k in range(n) for j in range(N_CHIPS)]
        for cp in copies:
            cp.start()
        for cp in copies:
            cp.wait_recv()
        for cp in copies:
            cp.wait_send()

    any_spec = pl.BlockSpec(memory_space=pl.ANY)
    return pl.pallas_call(
        body, name=name, out_shape=[jax.ShapeDtypeStruct((N_CHIPS,) + a.shape[1:], a.dtype) for a in arrays],
        in_specs=[any_spec] * n, out_specs=[any_spec] * n,
        scratch_shapes=[pltpu.SemaphoreType.DMA((N_CHIPS * n,)), pltpu.SemaphoreType.DMA((N_CHIPS * n,))],
        compiler_params=pltpu.CompilerParams(has_side_effects=True),
    )(*arrays)


def _chip_exchange(name, arrays):
    n = len(arrays)

    def body(*refs):
        srcs, dsts = refs[:n], refs[n:2 * n]
        send_sems, recv_sems, local_sems = refs[2 * n:]
        x, y, c = lax.axis_index("x"), lax.axis_index("y"), lax.axis_index("c")
        my_chip = 2 * x + y

        def peer(d):
            px, py = (x + (d >> 1)) % 2, (y + (d & 1)) % 2
            return (px, py, c), 2 * px + py

        def remote(k, d, started):
            to, chip = peer(d)
            return pltpu.make_async_remote_copy(
                src_ref=srcs[k].at[chip], dst_ref=dsts[k].at[my_chip if started else chip],
                send_sem=send_sems.at[k * 3 + d - 1], recv_sem=recv_sems.at[k * 3 + d - 1],
                device_id=to, device_id_type=pl.DeviceIdType.MESH)

        local = [pltpu.make_async_copy(srcs[k].at[my_chip], dsts[k].at[my_chip], local_sems.at[k]) for k in range(n)]
        sends = [remote(k, d, True) for k in range(n) for d in range(1, N_CHIPS)]
        for cp in local + sends:
            cp.start()
        for k in range(n):
            for d in range(1, N_CHIPS):
                remote(k, d, False).wait_recv()
        for cp in sends:
            cp.wait_send()
        for cp in local:
            cp.wait()

    any_spec = pl.BlockSpec(memory_space=pl.ANY)
    return pl.pallas_call(
        body, name=name, out_shape=[jax.ShapeDtypeStruct(a.shape, a.dtype) for a in arrays],
        in_specs=[any_spec] * n, out_specs=[any_spec] * n,
        scratch_shapes=[pltpu.SemaphoreType.DMA((3 * n,)), pltpu.SemaphoreType.DMA((3 * n,)),
                        pltpu.SemaphoreType.DMA((n,))],
        compiler_params=pltpu.CompilerParams(has_side_effects=True),
    )(*arrays)


def _pair_sum(name, parts, received, core):
    n_blocks, rows, cols = received.shape
    tr = rows if rows % 256 else 256

    def body(core_ref, mine_ref, recv_ref, o_ref):
        o_ref[...] = (mine_ref[...].astype(f32) + recv_ref[...].astype(f32)).astype(bf16)

    return pl.pallas_call(
        body, name=name,
        grid_spec=pltpu.PrefetchScalarGridSpec(
            num_scalar_prefetch=1, grid=(n_blocks, rows // tr),
            in_specs=[pl.BlockSpec((1, tr, cols), lambda j, i, core_ref: (2 * j + core_ref[0], i, 0)),
                      pl.BlockSpec((1, tr, cols), lambda j, i, core_ref: (j, i, 0))],
            out_specs=pl.BlockSpec((1, tr, cols), lambda j, i, core_ref: (j, i, 0))),
        out_shape=jax.ShapeDtypeStruct((n_blocks, rows, cols), bf16),
        compiler_params=_params("parallel", "parallel"),
    )(core, parts, received)


def _gather_two_level(name, arrays):
    n = len(arrays)

    def body(*refs):
        srcs, dsts = refs[:n], refs[n:2 * n]
        send_sems, recv_sems, local_sems = refs[2 * n:]
        x, y, c = lax.axis_index("x"), lax.axis_index("y"), lax.axis_index("c")
        sibling = (x, y, 1 - c)
        chips = [((x + 1) % 2, y), (x, (y + 1) % 2), ((x + 1) % 2, (y + 1) % 2)]

        def index(px, py, pc):
            return 4 * px + 2 * py + pc

        def copy(k, slot, block, to, src=None):
            return pltpu.make_async_remote_copy(
                src_ref=dsts[k].at[index(*block)] if src is None else src, dst_ref=dsts[k].at[index(*block)],
                send_sem=send_sems.at[k * 7 + slot], recv_sem=recv_sems.at[k * 7 + slot],
                device_id=to, device_id_type=pl.DeviceIdType.MESH)

        me = (x, y, c)
        local = [pltpu.make_async_copy(srcs[k], dsts[k].at[index(*me)], local_sems.at[k]) for k in range(n)]
        first = [copy(k, 0, me, sibling, src=srcs[k]) for k in range(n)]
        first += [copy(k, 1 + j, me, (*chip, c), src=srcs[k]) for j, chip in enumerate(chips) for k in range(n)]
        for cp in local + first:
            cp.start()
        passed = []
        for j, chip in enumerate(chips):
            for k in range(n):
                copy(k, 1 + j, (*chip, c), me).wait_recv()
                fwd = copy(k, 4 + j, (*chip, c), sibling)
                fwd.start()
                passed.append(fwd)
        for k in range(n):
            copy(k, 0, sibling, me).wait_recv()
            for j, chip in enumerate(chips):
                copy(k, 4 + j, (*chip, 1 - c), me).wait_recv()
        for cp in first + passed:
            cp.wait_send()
        for cp in local:
            cp.wait()

    any_spec = pl.BlockSpec(memory_space=pl.ANY)
    return pl.pallas_call(
        body, name=name, out_shape=[jax.ShapeDtypeStruct((N_DEV,) + a.shape, a.dtype) for a in arrays],
        in_specs=[any_spec] * n, out_specs=[any_spec] * n,
        scratch_shapes=[pltpu.SemaphoreType.DMA((7 * n,)), pltpu.SemaphoreType.DMA((7 * n,)),
                        pltpu.SemaphoreType.DMA((n,))],
        compiler_params=pltpu.CompilerParams(has_side_effects=True),
    )(*arrays)


def _mod_shard(c_all, w_ada, b_shard):
    def body(c_ref, w_ref, b_ref, o_ref):
        cv = c_ref[...]
        ca = cv * _sigmoid(cv)
        o_ref[...] = jnp.dot(ca.astype(bf16), w_ref[...].astype(bf16), preferred_element_type=f32) + b_ref[...]

    return pl.pallas_call(body, name="mod_shard", out_shape=jax.ShapeDtypeStruct((N_DEV, w_ada.shape[1]), f32),
                          compiler_params=_params())(c_all, w_ada, b_shard)


def _norm_mod(x, norm_g, scale1p, shift):
    s_len = x.shape[0]
    tm = 512

    def body(x_ref, g_ref, sc_ref, sh_ref, h_ref):
        xb = x_ref[...]
        r = lax.rsqrt(jnp.mean(xb * xb, axis=-1, keepdims=True) + EPS)
        h_ref[...] = ((xb * r * g_ref[...]) * sc_ref[...] + sh_ref[...]).astype(bf16)

    row = pl.BlockSpec((tm, D_MODEL), lambda i: (i, 0))
    vec = pl.BlockSpec((1, D_MODEL), lambda i: (0, 0))
    return pl.pallas_call(body, name="norm_mod", grid=(s_len // tm,), in_specs=[row, vec, vec, vec], out_specs=row,
                          out_shape=jax.ShapeDtypeStruct((s_len, D_MODEL), bf16),
                          compiler_params=_params("parallel"))(x, norm_g, scale1p, shift)


def _in_proj(h, wt_main, wt_small):
    s_len = h.shape[0]
    tm, tn = min(1024, s_len), 1024

    def body(h_ref, w_ref, ws_ref, p_ref, ps_ref):
        @pl.when(pl.program_id(1) == 0)
        def _():
            ps_ref[...] = _dg(h_ref[...], ws_ref[...], 1, 1)

        p_ref[...] = _dg(h_ref[...], w_ref[...], 1, 1)

    return pl.pallas_call(
        body, name="in_proj", grid=(s_len // tm, N_MAIN // tn),
        in_specs=[pl.BlockSpec((tm, D_MODEL), lambda i, j: (i, 0)),
                  pl.BlockSpec((tn, D_MODEL), lambda i, j: (j, 0)),
                  pl.BlockSpec((N_SMALL, D_MODEL), lambda i, j: (0, 0))],
        out_specs=[pl.BlockSpec((tm, tn), lambda i, j: (i, j)),
                   pl.BlockSpec((tm, N_SMALL), lambda i, j: (i, 0))],
        out_shape=[jax.ShapeDtypeStruct((s_len, N_MAIN), f32), jax.ShapeDtypeStruct((s_len, N_SMALL), f32)],
        compiler_params=_params("parallel", "arbitrary"),
    )(h, wt_main, wt_small)


PREP_TM = 256
HALO = 8


def _conv_section(xe_ref, cw_ref, cols, tm):
    acc = cw_ref[pl.ds(CONV_K - 1, 1), cols] * xe_ref[pl.ds(HALO, tm), :]
    for tap in range(CONV_K - 1):
        acc = acc + cw_ref[pl.ds(tap, 1), cols] * xe_ref[pl.ds(HALO - (CONV_K - 1) + tap, tm), :]
    return acc


def _small_fwd(ps, bvec, alog):
    z = ps + bvec
    logsig, softp = _softplus_parts(z)
    gval = -jnp.exp(alog) * softp
    beta = _sigmoid(ps)
    return z, logsig, gval, beta


def _head_lane(block, lane):
    return jnp.sum(jnp.where(_iota(block.shape, 1) == lane, block, 0.0), axis=1, keepdims=True)


def _head_slab(block, lane):
    return jnp.broadcast_to(_head_lane(block, lane), block.shape)


def _chunk_masks(tm):
    r, c = _iota((tm, tm), 0), _iota((tm, tm), 1)
    same = (r // CHUNK) == (c // CHUNK)
    return (same & (r >= c)).astype(f32), same.astype(f32)


def _prep(p_main, p_small, qn_g, kn_g, conv_w, bvec, alog):
    s_len = p_main.shape[0]
    tm = PREP_TM
    nb = s_len // tm

    def body(fq_ref, fk_ref, fv_ref, gq_ref, gk_ref, gv_ref, hq_ref, hk_ref, hv_ref, ps_ref, qg_ref, kg_ref,
             cw_ref, bv_ref, al_ref,
             qs_ref, kn_ref, vb_ref, gqo_ref, gko_ref, gvo_ref, small_ref, xe_sc, carry_sc):
        i = pl.program_id(0)

        @pl.when(i == 0)
        def _():
            carry_sc[...] = jnp.zeros_like(carry_sc)

        vb_ref[...] = fv_ref[...].astype(bf16)

        first = i == 0
        for sec, (x_ref, halo_ref, o_ref) in enumerate(((gq_ref, hq_ref, gqo_ref), (gk_ref, hk_ref, gko_ref),
                                                        (gv_ref, hv_ref, gvo_ref))):
            xe_sc[0:HALO, :] = jnp.where(first, 0.0, halo_ref[...])
            xe_sc[HALO:, :] = x_ref[...]
            cv = _conv_section(xe_sc, cw_ref, slice(sec * WIDTH, (sec + 1) * WIDTH), tm)
            y = cv * _sigmoid(cv)
            if sec == 2:
                o_ref[...] = y
            else:
                mul = QK_SCALE if sec == 0 else 1.0
                for h in range(HEADS):
                    sl = slice(h * HEAD_DIM, (h + 1) * HEAD_DIM)
                    yh = y[:, sl]
                    o_ref[:, sl] = yh * (lax.rsqrt(jnp.sum(yh * yh, axis=-1, keepdims=True) + EPS) * mul)

        lane = _iota((tm, N_SMALL), 1)
        _, logsig, gval, beta = _small_fwd(ps_ref[...], bv_ref[...], al_ref[...])
        lf = jnp.where(lane < HEADS, logsig, 0.0)
        tri = (_iota((tm, tm), 0) >= _iota((tm, tm), 1)).astype(f32)
        fcum = jnp.dot(tri, lf, preferred_element_type=f32, precision=HI) + carry_sc[...]
        carry_sc[...] += jnp.sum(lf, axis=0, keepdims=True)
        tri_c, ones_c = _chunk_masks(tm)
        g_lanes = jnp.where((lane >= LANE_G) & (lane < LANE_G + HEADS), gval, 0.0)
        gc = jnp.dot(tri_c, g_lanes, preferred_element_type=f32, precision=HI)
        g_last = jnp.dot(ones_c, g_lanes, preferred_element_type=f32, precision=HI)
        small = jnp.where(lane < LANE_G, fcum, jnp.where(lane < LANE_BETA, gval, jnp.where(lane < LANE_GC, beta, 0.0)))
        small_ref[...] = small + pltpu.roll(gc, LANE_GC - LANE_G, 1) + pltpu.roll(g_last, LANE_GLAST - LANE_G, 1)

        qg, kg = qg_ref[...], kg_ref[...]
        f2 = fcum * LOG2E
        for h in range(HEADS):
            sl = slice(h * HEAD_DIM, (h + 1) * HEAD_DIM)
            q = fq_ref[:, sl]
            rq = lax.rsqrt(jnp.mean(q * q, axis=-1, keepdims=True) + EPS)
            k = fk_ref[:, sl]
            rk = lax.rsqrt(jnp.mean(k * k, axis=-1, keepdims=True) + EPS)
            f_col = _head_lane(f2, LANE_F + h)
            hi = f_col.astype(bf16).astype(f32)
            mid = (f_col - hi).astype(bf16).astype(f32)
            lo = f_col - hi - mid
            q_bias = jnp.where(lane == 0, hi, jnp.where(lane == 1, mid, jnp.where(lane == 2, lo,
                                                                                  jnp.where(lane < 6, 1.0, 0.0))))
            k_bias = jnp.where(lane < 3, 1.0, jnp.where(lane == 3, -hi, jnp.where(lane == 4, -mid,
                                                                                 jnp.where(lane == 5, -lo, 0.0))))
            base = 2 * h * HEAD_DIM
            qs_ref[:, base:base + HEAD_DIM] = (q * rq * qg * (QK_SCALE * LOG2E)).astype(bf16)
            qs_ref[:, base + HEAD_DIM:base + 2 * HEAD_DIM] = q_bias.astype(bf16)
            kn_ref[:, base:base + HEAD_DIM] = (k * rk * kg).astype(bf16)
            kn_ref[:, base + HEAD_DIM:base + 2 * HEAD_DIM] = k_bias.astype(bf16)

    def col(cb):
        return pl.BlockSpec((tm, WIDTH), lambda i: (i, cb))

    def halo(cb):
        return pl.BlockSpec((HALO, WIDTH), lambda i: (jnp.maximum(i * (tm // HALO) - 1, 0), cb))

    vec = pl.BlockSpec((1, LANES), lambda i: (0, 0))
    wide_f32 = jax.ShapeDtypeStruct((s_len, WIDTH), f32)
    wide_bf = jax.ShapeDtypeStruct((s_len, WIDTH), bf16)
    out_col = pl.BlockSpec((tm, WIDTH), lambda i: (i, 0))
    return pl.pallas_call(
        body, name="prep", grid=(nb,),
        in_specs=[col(0), col(1), col(2), col(4), col(5), col(6), halo(4), halo(5), halo(6),
                  pl.BlockSpec((tm, N_SMALL), lambda i: (i, 0)), vec, vec,
                  pl.BlockSpec((CONV_K, 3 * WIDTH), lambda i: (0, 0)), vec, vec],
        out_specs=[pl.BlockSpec((tm, 2 * WIDTH), lambda i: (i, 0))] * 2 + [out_col] * 4
                  + [pl.BlockSpec((tm, N_SMALL), lambda i: (i, 0))],
        out_shape=[jax.ShapeDtypeStruct((s_len, 2 * WIDTH), bf16)] * 2 + [wide_bf, wide_f32, wide_f32, wide_f32,
                                                                          jax.ShapeDtypeStruct((s_len, N_SMALL), f32)],
        scratch_shapes=[pltpu.VMEM((tm + HALO, WIDTH), f32), pltpu.VMEM((1, N_SMALL), f32)],
        compiler_params=_params("arbitrary"),
    )(p_main, p_main, p_main, p_main, p_main, p_main, p_main, p_main, p_main, p_small, qn_g, kn_g, conv_w, bvec, alog)


FOX_T = 1024
NEG_BIG = -1e30


def _fox_fwd(qs, kn, vb):
    s_len = qs.shape[0]
    t = FOX_T
    nq = s_len // t

    def body(q_ref, k_ref, v_ref, o_ref, lse_ref):
        qi = pl.program_id(1)
        q = q_ref[...]
        causal = _iota((t, t), 0) >= _iota((t, t), 1)

        def step(j, carry, masked):
            m, l, acc = carry
            rows = pl.ds(pl.multiple_of(j * t, t), t)
            s = _dg(q, k_ref[rows, :], 1, 1)
            if masked:
                s = jnp.where(causal, s, NEG_BIG)
            m_new = jnp.maximum(m, jnp.max(s, axis=-1, keepdims=True))
            p = jnp.exp2(s - m_new)
            alpha = jnp.exp2(m - m_new)
            l = alpha * l + jnp.sum(p, axis=-1, keepdims=True)
            acc = alpha * acc + jnp.dot(p.astype(bf16), v_ref[rows, :], preferred_element_type=f32)
            return m_new, l, acc

        init = (jnp.full((t, 1), NEG_BIG, f32), jnp.zeros((t, 1), f32), jnp.zeros((t, HEAD_DIM), f32))
        carry = lax.fori_loop(0, qi, lambda j, c: step(j, c, False), init)
        m, l, acc = step(qi, carry, True)
        o_ref[...] = acc / l
        lse_ref[0] = m + jnp.log2(l)

    return pl.pallas_call(
        body, name="fox_fwd", grid=(HEADS, nq),
        in_specs=[pl.BlockSpec((t, 2 * HEAD_DIM), lambda h, i: (i, h)),
                  pl.BlockSpec((s_len, 2 * HEAD_DIM), lambda h, i: (0, h)),
                  pl.BlockSpec((s_len, HEAD_DIM), lambda h, i: (0, h))],
        out_specs=[pl.BlockSpec((t, HEAD_DIM), lambda h, i: (i, h)),
                   pl.BlockSpec((1, t, 1), lambda h, i: (h, i, 0))],
        out_shape=[jax.ShapeDtypeStruct((s_len, WIDTH), f32), jax.ShapeDtypeStruct((HEADS, s_len, 1), f32)],
        compiler_params=_params("parallel", "arbitrary"),
    )(qs, kn, vb)


def _fox_bwd(qs, kn, vb, do, lse, delta):
    s_len = qs.shape[0]
    t = FOX_T
    nq = s_len // t

    def body(q_ref, do_ref, lse_ref, dl_ref, k_ref, v_ref, dq_ref, dk_ref, dvb_ref, df_ref, dfq_ref, dv_ref):
        head, qi = pl.program_id(0), pl.program_id(1)

        @pl.when(qi == 0)
        def _():
            dk_ref[...] = jnp.zeros_like(dk_ref)
            dv_ref[...] = jnp.zeros_like(dv_ref)
            df_ref[...] = jnp.zeros_like(df_ref)

        q, do_b = q_ref[...], do_ref[...]
        q_main = q_ref[:, 0:HEAD_DIM]
        lse_col = lse_ref[0]
        dl = _head_lane(dl_ref[...], head)
        causal = _iota((t, t), 0) >= _iota((t, t), 1)

        def step(j, carry, masked):
            dq, row_sum = carry
            rows = pl.ds(pl.multiple_of(j * t, t), t)
            vj = v_ref[rows, :]
            p = jnp.exp2(_dg(q, k_ref[rows, :], 1, 1) - lse_col)
            if masked:
                p = jnp.where(causal, p, 0.0)
            ds = p * (_dg(do_b, vj, 1, 1) - dl)
            ds_b = ds.astype(bf16)
            dk_ref[rows, :] += _dg(ds_b, q_main, 0, 0)
            dv_ref[rows, :] += _dg(p.astype(bf16), do_b, 0, 0)
            df_ref[0, j] += -jnp.sum(ds, axis=0, keepdims=True)
            dq = dq + jnp.dot(ds_b, k_ref[rows, 0:HEAD_DIM], preferred_element_type=f32)
            return dq, row_sum + jnp.sum(ds, axis=-1, keepdims=True)

        carry = lax.fori_loop(0, qi, lambda j, c: step(j, c, False),
                              (jnp.zeros((t, HEAD_DIM), f32), jnp.zeros((t, 1), f32)))
        dq, row_sum = step(qi, carry, True)
        dq_ref[...] = dq
        dfq_ref[0] = row_sum

        @pl.when(qi == nq - 1)
        def _():
            dvb_ref[...] = dv_ref[...].astype(bf16)

    blk = pl.BlockSpec((t, HEAD_DIM), lambda h, i: (i, h))
    blk2 = pl.BlockSpec((t, 2 * HEAD_DIM), lambda h, i: (i, h))
    full = pl.BlockSpec((s_len, HEAD_DIM), lambda h, i: (0, h))
    full2 = pl.BlockSpec((s_len, 2 * HEAD_DIM), lambda h, i: (0, h))
    colv = pl.BlockSpec((1, t, 1), lambda h, i: (h, i, 0))
    rowv = pl.BlockSpec((1, nq, 1, t), lambda h, i: (h, 0, 0, 0))
    lanes = pl.BlockSpec((t, N_SMALL), lambda h, i: (i, 0))
    wide = jax.ShapeDtypeStruct((s_len, WIDTH), f32)
    return pl.pallas_call(
        body, name="fox_bwd", grid=(HEADS, nq),
        in_specs=[blk2, blk, colv, lanes, full2, full],
        out_specs=[blk, full, full, rowv, colv],
        out_shape=[wide, wide, jax.ShapeDtypeStruct((s_len, WIDTH), bf16), jax.ShapeDtypeStruct((HEADS, nq, 1, t), f32),
                   jax.ShapeDtypeStruct((HEADS, s_len, 1), f32)],
        scratch_shapes=[pltpu.VMEM((s_len, HEAD_DIM), f32)],
        compiler_params=_params("parallel", "arbitrary"),
    )(qs, do, lse, delta, kn, vb)


INTRA_CHUNKS = 8
SCAN_FWD_CHUNKS = 8
SCAN_BWD_CHUNKS = 4


def _gdn_intra_fwd(gq, gk, gv, small):
    s_len = gq.shape[0]
    cpb = INTRA_CHUNKS
    rows_blk = cpb * CHUNK
    n_chunks = s_len // CHUNK

    def body(q_ref, k_ref, v_ref, sm_ref, u_ref, w_ref, qg_ref, kd_ref, attn_ref, t_ref, eg_ref):
        head = pl.program_id(0)
        sm = sm_ref[...]
        gc_b, gl_b, beta_b = (_head_slab(sm, LANE_GC + head), _head_slab(sm, LANE_GLAST + head),
                              _head_slab(sm, LANE_BETA + head))
        ms, rhss = [], []
        for ci in range(cpb):
            rows = pl.ds(ci * CHUNK, CHUNK)
            sl = slice(ci * CHUNK, (ci + 1) * CHUNK)
            m, rhs, qg, kd, attn, eg_last = _gdn_intra_pre(q_ref[rows, :], k_ref[rows, :], v_ref[rows, :],
                                                           gc_b[sl], gl_b[sl], beta_b[sl], _BF_PLAIN[1])
            qg_ref[rows, :] = qg.astype(bf16)
            kd_ref[rows, :] = kd.astype(bf16)
            attn_ref[0, ci] = attn.astype(bf16)
            eg_ref[0, ci] = eg_last
            ms.append(m)
            rhss.append(rhs)
        for ci, (t, rhs) in enumerate(zip(_inv_unit_lower_many(ms), rhss)):
            rows = pl.ds(ci * CHUNK, CHUNK)
            t_ref[0, ci] = t
            uw = _X3_PLAIN[0](t, rhs)
            u_ref[rows, :] = uw[:, :HEAD_DIM]
            w_ref[rows, :] = uw[:, HEAD_DIM:].astype(bf16)

    blk = pl.BlockSpec((rows_blk, HEAD_DIM), lambda h, i: (i, h))
    sq = pl.BlockSpec((1, cpb, CHUNK, CHUNK), lambda h, i: (h, i, 0, 0))
    wide_bf = jax.ShapeDtypeStruct((s_len, WIDTH), bf16)
    return pl.pallas_call(
        body, name="gdn_intra_fwd", grid=(HEADS, s_len // rows_blk),
        in_specs=[blk] * 3 + [pl.BlockSpec((rows_blk, N_SMALL), lambda h, i: (i, 0))],
        out_specs=[blk] * 4 + [sq, sq, pl.BlockSpec((1, cpb, SUBLANES, HEAD_DIM), lambda h, i: (h, i, 0, 0))],
        out_shape=[jax.ShapeDtypeStruct((s_len, WIDTH), f32), wide_bf, wide_bf, wide_bf,
                   jax.ShapeDtypeStruct((HEADS, n_chunks, CHUNK, CHUNK), bf16),
                   jax.ShapeDtypeStruct((HEADS, n_chunks, CHUNK, CHUNK), f32),
                   jax.ShapeDtypeStruct((HEADS, n_chunks, SUBLANES, HEAD_DIM), f32)],
        compiler_params=_params("parallel", "parallel"),
    )(gq, gk, gv, small)


def _gdn_scan_fwd(u, w, qg, kd, attn, eg):
    s_len = u.shape[0]
    cpb = SCAN_FWD_CHUNKS
    rows_blk = cpb * CHUNK
    n_chunks = s_len // CHUNK

    def body(u_ref, w_ref, qg_ref, kd_ref, attn_ref, eg_ref, o_ref, st_ref, s_sc):
        @pl.when(pl.program_id(0) == 0)
        def _():
            s_sc[...] = jnp.zeros_like(s_sc)

        def chunk(ci, _):
            rows = pl.ds(pl.multiple_of(ci * CHUNK, CHUNK), CHUNK)
            cols = [slice(h * HEAD_DIM, (h + 1) * HEAD_DIM) for h in range(HEADS)]
            s0 = [s_sc[h] for h in range(HEADS)]
            s0_b = [s.astype(bf16) for s in s0]
            for h in range(HEADS):
                st_ref[h, ci] = s0[h]
            ws = [jnp.dot(w_ref[rows, cols[h]], s0_b[h], preferred_element_type=f32) for h in range(HEADS)]
            qs = [jnp.dot(qg_ref[rows, cols[h]], s0_b[h], preferred_element_type=f32) for h in range(HEADS)]
            vn_b = [(u_ref[rows, cols[h]] - ws[h]).astype(bf16) for h in range(HEADS)]
            av = [jnp.dot(attn_ref[h, ci], vn_b[h], preferred_element_type=f32) for h in range(HEADS)]
            kv = [_dg(kd_ref[rows, cols[h]], vn_b[h], 0, 0) for h in range(HEADS)]
            for h in range(HEADS):
                o_ref[rows, cols[h]] = qs[h] + av[h]
                s_sc[h] = _scale_rows(s0[h], eg_ref[h, ci]) + kv[h]
            return 0

        lax.fori_loop(0, cpb, chunk, 0)

    row = pl.BlockSpec((rows_blk, WIDTH), lambda i: (i, 0))
    return pl.pallas_call(
        body, name="gdn_scan_fwd", grid=(s_len // rows_blk,),
        in_specs=[row] * 4 + [pl.BlockSpec((HEADS, cpb, CHUNK, CHUNK), lambda i: (0, i, 0, 0)),
                              pl.BlockSpec((HEADS, cpb, SUBLANES, HEAD_DIM), lambda i: (0, i, 0, 0))],
        out_specs=[row, pl.BlockSpec((HEADS, cpb, HEAD_DIM, HEAD_DIM), lambda i: (0, i, 0, 0))],
        out_shape=[jax.ShapeDtypeStruct((s_len, WIDTH), f32),
                   jax.ShapeDtypeStruct((HEADS, n_chunks, HEAD_DIM, HEAD_DIM), f32)],
        scratch_shapes=[pltpu.VMEM((HEADS, HEAD_DIM, HEAD_DIM), f32)],
        compiler_params=_params("arbitrary"),
    )(u, w, qg, kd, attn, eg)


def _gdn_scan_bwd(u, w, qg, kd, attn, eg, states, d_o):
    s_len = u.shape[0]
    cpb = SCAN_BWD_CHUNKS
    rows_blk = cpb * CHUNK
    n_chunks = s_len // CHUNK
    nb = s_len // rows_blk

    def body(u_ref, w_ref, qg_ref, kd_ref, attn_ref, eg_ref, st_ref, do_ref,
             du_ref, dw_ref, dqg_ref, dkd_ref, dattn_ref, deg_ref, ds_sc):
        @pl.when(pl.program_id(0) == 0)
        def _():
            ds_sc[...] = jnp.zeros_like(ds_sc)

        def chunk(step, _):
            ci = cpb - 1 - step
            rows = pl.ds(pl.multiple_of(ci * CHUNK, CHUNK), CHUNK)
            hs = range(HEADS)
            cols = [slice(h * HEAD_DIM, (h + 1) * HEAD_DIM) for h in hs]
            s0 = [st_ref[h, ci] for h in hs]
            s0_b = [s.astype(bf16) for s in s0]
            ds1 = [ds_sc[h] for h in hs]
            ds1_b = [d.astype(bf16) for d in ds1]
            do_b = [do_ref[rows, cols[h]].astype(bf16) for h in hs]
            ws = [jnp.dot(w_ref[rows, cols[h]], s0_b[h], preferred_element_type=f32) for h in hs]
            ad = [_dg(attn_ref[h, ci], do_b[h], 0, 0) for h in hs]
            kd_ds = [jnp.dot(kd_ref[rows, cols[h]], ds1_b[h], preferred_element_type=f32) for h in hs]
            dqg = [_dg(do_b[h], s0_b[h], 1, 1) for h in hs]
            qd = [_dg(qg_ref[rows, cols[h]], do_b[h], 0, 0) for h in hs]
            vn_b = [(u_ref[rows, cols[h]] - ws[h]).astype(bf16) for h in hs]
            dvn = [ad[h] + kd_ds[h] for h in hs]
            dvn_b = [d.astype(bf16) for d in dvn]
            dattn = [_dg(do_b[h], vn_b[h], 1, 1) for h in hs]
            dkd = [_dg(vn_b[h], ds1_b[h], 1, 1) for h in hs]
            dw = [_dg(dvn_b[h], s0_b[h], 1, 1) for h in hs]
            wd = [_dg(w_ref[rows, cols[h]], dvn_b[h], 0, 0) for h in hs]
            for h in hs:
                dattn_ref[h, ci] = dattn[h]
                dqg_ref[rows, cols[h]] = dqg[h]
                dkd_ref[rows, cols[h]] = dkd[h]
                du_ref[rows, cols[h]] = dvn[h]
                dw_ref[rows, cols[h]] = -dw[h]
                ds_sc[h] = qd[h] - wd[h] + _scale_rows(ds1[h], eg_ref[h, ci])
                deg_ref[h, ci] = jnp.sum((ds1[h] * s0[h]).reshape(HEAD_DIM // SUBLANES, SUBLANES, HEAD_DIM), axis=0)
            return 0

        lax.fori_loop(0, cpb, chunk, 0)

    row = pl.BlockSpec((rows_blk, WIDTH), lambda i: (nb - 1 - i, 0))
    sq = pl.BlockSpec((HEADS, cpb, CHUNK, CHUNK), lambda i: (0, nb - 1 - i, 0, 0))
    egs = pl.BlockSpec((HEADS, cpb, SUBLANES, HEAD_DIM), lambda i: (0, nb - 1 - i, 0, 0))
    wide = jax.ShapeDtypeStruct((s_len, WIDTH), f32)
    return pl.pallas_call(
        body, name="gdn_scan_bwd", grid=(nb,),
        in_specs=[row] * 4 + [sq, egs, pl.BlockSpec((HEADS, cpb, HEAD_DIM, HEAD_DIM), lambda i: (0, nb - 1 - i, 0, 0)), row],
        out_specs=[row] * 4 + [sq, egs],
        out_shape=[wide] * 4 + [jax.ShapeDtypeStruct((HEADS, n_chunks, CHUNK, CHUNK), f32),
                                jax.ShapeDtypeStruct((HEADS, n_chunks, SUBLANES, HEAD_DIM), f32)],
        scratch_shapes=[pltpu.VMEM((HEADS, HEAD_DIM, HEAD_DIM), f32)],
        compiler_params=_params("arbitrary"),
    )(u, w, qg, kd, attn, eg, states, d_o)


def _gdn_intra_bwd(gq, gk, gv, small, t_inv, du, dw, dqg, dkd, dattn, deg):
    s_len = gq.shape[0]
    cpb = INTRA_CHUNKS
    rows_blk = cpb * CHUNK

    def body(q_ref, k_ref, v_ref, sm_ref, t_ref, du_ref, dw_ref, dqg_ref, dkd_ref, dattn_ref, deg_ref,
             dq_ref, dk_ref, dv_ref, dsm_ref):
        head = pl.program_id(1)

        def batch(value):
            return value.reshape(cpb, CHUNK, HEAD_DIM)

        sm = sm_ref[...]
        slabs = [batch(_head_slab(sm, first + head)) for first in (LANE_GC, LANE_GLAST, LANE_BETA)]
        t_known = t_ref[0]
        _, vjp = jax.vjp(lambda q, k, v, gc, gl, b: _gdn_intra(q, k, v, gc, gl, b, t_known),
                         batch(q_ref[...]), batch(k_ref[...]), batch(v_ref[...]), *slabs)
        duw = jnp.concatenate([batch(du_ref[...]), batch(dw_ref[...])], axis=-1)
        dq, dk, dv, dgc, dgl, db = vjp((duw, batch(dqg_ref[...]), batch(dkd_ref[...]), dattn_ref[0], deg_ref[0]))
        for ref, grad in zip((dq_ref, dk_ref, dv_ref), (dq, dk, dv)):
            ref[...] = grad.reshape(rows_blk, HEAD_DIM)

        @pl.when(head == 0)
        def _():
            dsm_ref[...] = jnp.zeros_like(dsm_ref)

        lane = _iota((rows_blk, N_SMALL), 1)
        acc = dsm_ref[...]
        for first, grad in ((LANE_GC, dgc), (LANE_GLAST, dgl), (LANE_BETA, db)):
            col = jnp.sum(grad.reshape(rows_blk, HEAD_DIM), axis=1, keepdims=True)
            acc = acc + jnp.where(lane == first + head, col, 0.0)
        dsm_ref[...] = acc

    blk = pl.BlockSpec((rows_blk, HEAD_DIM), lambda i, h: (i, h))
    sq = pl.BlockSpec((1, cpb, CHUNK, CHUNK), lambda i, h: (h, i, 0, 0))
    egs = pl.BlockSpec((1, cpb, SUBLANES, HEAD_DIM), lambda i, h: (h, i, 0, 0))
    lanes = pl.BlockSpec((rows_blk, N_SMALL), lambda i, h: (i, 0))
    wide = jax.ShapeDtypeStruct((s_len, WIDTH), f32)
    return pl.pallas_call(
        body, name="gdn_intra_bwd", grid=(s_len // rows_blk, HEADS),
        in_specs=[blk] * 3 + [lanes, sq] + [blk] * 4 + [sq, egs],
        out_specs=[blk] * 3 + [lanes],
        out_shape=[wide] * 3 + [jax.ShapeDtypeStruct((s_len, N_SMALL), f32)],
        compiler_params=_params("parallel", "arbitrary"),
    )(gq, gk, gv, small, t_inv, du, dw, dqg, dkd, dattn, deg)


MIX_TM = 256


def _mix_fwd(fox_o, gdn_o, p_main, gnorm_g):
    s_len = fox_o.shape[0]
    tm = MIX_TM

    def body(fo_ref, go_ref, fz_ref, gz_ref, g_ref, mixed_ref):
        fz = fz_ref[...]
        mixed_ref[:, 0:WIDTH] = (fo_ref[...] * (fz * _sigmoid(fz))).astype(bf16)
        gz = gz_ref[...]
        gate = gz * _sigmoid(gz)
        gg = g_ref[...]
        for h in range(HEADS):
            sl = slice(h * HEAD_DIM, (h + 1) * HEAD_DIM)
            o = go_ref[:, sl]
            r = lax.rsqrt(jnp.mean(o * o, axis=-1, keepdims=True) + EPS)
            mixed_ref[:, WIDTH + h * HEAD_DIM:WIDTH + (h + 1) * HEAD_DIM] = (o * r * gg * gate[:, sl]).astype(bf16)

    row = pl.BlockSpec((tm, WIDTH), lambda i: (i, 0))
    return pl.pallas_call(
        body, name="mix_fwd", grid=(s_len // tm,),
        in_specs=[row, row, pl.BlockSpec((tm, WIDTH), lambda i: (i, 3)), pl.BlockSpec((tm, WIDTH), lambda i: (i, 7)),
                  pl.BlockSpec((1, LANES), lambda i: (0, 0))],
        out_specs=pl.BlockSpec((tm, 2 * WIDTH), lambda i: (i, 0)),
        out_shape=jax.ShapeDtypeStruct((s_len, 2 * WIDTH), bf16),
        compiler_params=_params("parallel"),
    )(fox_o, gdn_o, p_main, p_main, gnorm_g)


def _silu_grad(z):
    sg = _sigmoid(z)
    return sg * (1.0 + z * (1.0 - sg))


def _mix_bwd(dmixed, fox_o, gdn_o, p_main, gnorm_g):
    s_len = fox_o.shape[0]
    tm = MIX_TM

    def body(dm_ref, fo_ref, go_ref, fz_ref, gz_ref, g_ref, dof_ref, delta_ref, dfz_ref, dgz_ref, dgo_ref, dg_ref):
        @pl.when(pl.program_id(0) == 0)
        def _():
            dg_ref[...] = jnp.zeros_like(dg_ref)

        lane = _iota((tm, LANES), 1)
        fz = fz_ref[...]
        dmf = dm_ref[:, 0:WIDTH]
        fo = fo_ref[...]
        dof = dmf * (fz * _sigmoid(fz))
        dof_ref[...] = dof.astype(bf16)
        dfz_ref[...] = (dmf * fo * _silu_grad(fz)).astype(bf16)
        prod = dof * fo
        delta = jnp.zeros((tm, LANES), f32)
        for h in range(HEADS):
            dh = jnp.sum(prod[:, h * HEAD_DIM:(h + 1) * HEAD_DIM], axis=-1, keepdims=True)
            delta = jnp.where(lane == h, dh, delta)
        delta_ref[...] = delta

        gz = gz_ref[...]
        dmg = dm_ref[:, WIDTH:2 * WIDTH]
        gate = gz * _sigmoid(gz)
        sgrad = _silu_grad(gz)
        gg = g_ref[...]
        dg_acc = jnp.zeros((1, HEAD_DIM), f32)
        for h in range(HEADS):
            sl = slice(h * HEAD_DIM, (h + 1) * HEAD_DIM)
            o = go_ref[:, sl]
            r = lax.rsqrt(jnp.mean(o * o, axis=-1, keepdims=True) + EPS)
            on = o * r
            dmh = dmg[:, sl]
            dgz_ref[:, sl] = (dmh * (on * gg) * sgrad[:, sl]).astype(bf16)
            dy = dmh * gate[:, sl]
            dg_acc = dg_acc + jnp.sum(dy * on, axis=0, keepdims=True)
            tt = dy * gg
            dgo_ref[:, sl] = r * (tt - on * jnp.mean(tt * on, axis=-1, keepdims=True))
        dg_ref[...] += dg_acc

    row = pl.BlockSpec((tm, WIDTH), lambda i: (i, 0))
    wide_bf = jax.ShapeDtypeStruct((s_len, WIDTH), bf16)
    return pl.pallas_call(
        body, name="mix_bwd", grid=(s_len // tm,),
        in_specs=[pl.BlockSpec((tm, 2 * WIDTH), lambda i: (i, 0)), row, row,
                  pl.BlockSpec((tm, WIDTH), lambda i: (i, 3)), pl.BlockSpec((tm, WIDTH), lambda i: (i, 7)),
                  pl.BlockSpec((1, LANES), lambda i: (0, 0))],
        out_specs=[row, pl.BlockSpec((tm, LANES), lambda i: (i, 0)), row, row, row,
                   pl.BlockSpec((1, LANES), lambda i: (0, 0))],
        out_shape=[wide_bf, jax.ShapeDtypeStruct((s_len, LANES), f32), wide_bf, wide_bf,
                   jax.ShapeDtypeStruct((s_len, WIDTH), f32), jax.ShapeDtypeStruct((1, LANES), f32)],
        compiler_params=_params("arbitrary"),
    )(dmixed, fox_o, gdn_o, p_main, p_main, gnorm_g)


def _out_head(mixed, w_out, x, target, gate, final_g):
    s_len = x.shape[0]
    tm = 256

    def body(mx_ref, w_ref, x_ref, t_ref, gate_ref, fg_ref, loss_ref, dy_ref, dz_ref, dm_ref, dfg_ref, dgate_ref):
        @pl.when(pl.program_id(0) == 0)
        def _():
            loss_ref[...] = jnp.zeros_like(loss_ref)
            dfg_ref[...] = jnp.zeros_like(dfg_ref)
            dgate_ref[...] = jnp.zeros_like(dgate_ref)

        w = w_ref[...]
        z = jnp.dot(mx_ref[...], w, preferred_element_type=f32)
        gate_v, fg = gate_ref[...], fg_ref[...]
        y1 = x_ref[...] + gate_v * z
        r = lax.rsqrt(jnp.mean(y1 * y1, axis=-1, keepdims=True) + EPS)
        yn = y1 * r
        err = yn * fg - t_ref[...]
        loss_ref[...] += 0.5 * jnp.sum(jnp.mean(err * err, axis=-1, keepdims=True))
        dout = err * (1.0 / D_MODEL)
        dfg_ref[...] += jnp.sum(dout * yn, axis=0, keepdims=True)
        tt = dout * fg
        dy1 = r * (tt - yn * jnp.mean(tt * yn, axis=-1, keepdims=True))
        dy_ref[...] = dy1
        dgate_ref[...] += jnp.sum(dy1 * z, axis=0, keepdims=True)
        dz = (dy1 * gate_v).astype(bf16)
        dz_ref[...] = dz
        dm_ref[...] = _dg(dz, w, 1, 1)

    row = pl.BlockSpec((tm, D_MODEL), lambda i: (i, 0))
    vec = pl.BlockSpec((1, D_MODEL), lambda i: (0, 0))
    big = jax.ShapeDtypeStruct((s_len, D_MODEL), f32)
    return pl.pallas_call(
        body, name="out_head", grid=(s_len // tm,),
        in_specs=[row, pl.BlockSpec((D_MODEL, D_MODEL), lambda i: (0, 0)), row, row, vec, vec],
        out_specs=[pl.BlockSpec((1, LANES), lambda i: (0, 0)), row, row, row, vec, vec],
        out_shape=[jax.ShapeDtypeStruct((1, LANES), f32), big, jax.ShapeDtypeStruct((s_len, D_MODEL), bf16), big,
                   jax.ShapeDtypeStruct((1, D_MODEL), f32), jax.ShapeDtypeStruct((1, D_MODEL), f32)],
        compiler_params=_params("arbitrary"),
    )(mixed, w_out, x, target, gate, final_g)


def _matmul_tn(name, a, b, out_dtype):
    k_len, m_len = a.shape
    n_len = b.shape[1]
    tk, tm, tn = min(2048, k_len), min(1024, m_len), min(1024, n_len)
    nk = k_len // tk

    def body(a_ref, b_ref, o_ref, acc_sc):
        k = pl.program_id(2)

        @pl.when(k == 0)
        def _():
            acc_sc[...] = jnp.zeros_like(acc_sc)

        acc_sc[...] += _dg(a_ref[...], b_ref[...], 0, 0)

        @pl.when(k == nk - 1)
        def _():
            o_ref[...] = acc_sc[...].astype(out_dtype)

    return pl.pallas_call(
        body, name=name, grid=(m_len // tm, n_len // tn, nk),
        in_specs=[pl.BlockSpec((tk, tm), lambda i, j, k: (k, i)), pl.BlockSpec((tk, tn), lambda i, j, k: (k, j))],
        out_specs=pl.BlockSpec((tm, tn), lambda i, j, k: (i, j)),
        out_shape=jax.ShapeDtypeStruct((m_len, n_len), out_dtype),
        scratch_shapes=[pltpu.VMEM((tm, tn), f32)],
        compiler_params=_params("parallel", "parallel", "arbitrary"),
    )(a, b)


def _post1(p_main, p_small, qn_g, kn_g, conv_w, bvec, alog, dqs, dkn, dgq, dgk, dgv, d_small, df, df_query):
    s_len = p_main.shape[0]
    tm = PREP_TM
    nb = s_len // tm

    def body(fq_ref, fk_ref, gq_ref, gk_ref, gv_ref, hq_ref, hk_ref, hv_ref, ps_ref, qg_ref, kg_ref, cw_ref, bv_ref,
             al_ref, dqs_ref, dkn_ref, dgq_ref, dgk_ref, dgv_ref, dsm_ref, df_ref, dfq_in_ref,
             dfq_ref, dfk_ref, dconv_ref, dps_ref, dqg_ref, dkg_ref, sums_ref, xe_sc, carry_sc):
        step = pl.program_id(0)
        blk = nb - 1 - step

        @pl.when(step == 0)
        def _():
            carry_sc[...] = jnp.zeros_like(carry_sc)
            dqg_ref[...] = jnp.zeros_like(dqg_ref)
            dkg_ref[...] = jnp.zeros_like(dkg_ref)
            sums_ref[...] = jnp.zeros_like(sums_ref)

        for x_ref, g_ref, dy_ref, o_ref, acc_ref, mul in ((fq_ref, qg_ref, dqs_ref, dfq_ref, dqg_ref, QK_SCALE),
                                                          (fk_ref, kg_ref, dkn_ref, dfk_ref, dkg_ref, LN2)):
            gain = g_ref[...]
            acc = jnp.zeros((1, HEAD_DIM), f32)
            for h in range(HEADS):
                sl = slice(h * HEAD_DIM, (h + 1) * HEAD_DIM)
                xv = x_ref[:, sl]
                r = lax.rsqrt(jnp.mean(xv * xv, axis=-1, keepdims=True) + EPS)
                xn = xv * r
                dy = dy_ref[:, sl] * mul
                acc = acc + jnp.sum(dy * xn, axis=0, keepdims=True)
                tt = dy * gain
                o_ref[:, sl] = (r * (tt - xn * jnp.mean(tt * xn, axis=-1, keepdims=True))).astype(bf16)
            acc_ref[...] += acc

        first = blk == 0
        for sec, (x_ref, halo_ref, dy_ref) in enumerate(((gq_ref, hq_ref, dgq_ref), (gk_ref, hk_ref, dgk_ref),
                                                         (gv_ref, hv_ref, dgv_ref))):
            xe_sc[0:HALO, :] = jnp.where(first, 0.0, halo_ref[...])
            xe_sc[HALO:, :] = x_ref[...]
            cv = _conv_section(xe_sc, cw_ref, slice(sec * WIDTH, (sec + 1) * WIDTH), tm)
            sgrad = _silu_grad(cv)
            if sec == 2:
                dconv_ref[:, sec * WIDTH:(sec + 1) * WIDTH] = dy_ref[...] * sgrad
            else:
                y = cv * _sigmoid(cv)
                mul = QK_SCALE if sec == 0 else 1.0
                for h in range(HEADS):
                    sl = slice(h * HEAD_DIM, (h + 1) * HEAD_DIM)
                    yh = y[:, sl]
                    r = lax.rsqrt(jnp.sum(yh * yh, axis=-1, keepdims=True) + EPS)
                    dqh = dy_ref[:, sl]
                    dyh = (mul * r) * (dqh - yh * (r * r) * jnp.sum(dqh * yh, axis=-1, keepdims=True))
                    dconv_ref[:, sec * WIDTH + h * HEAD_DIM:sec * WIDTH + (h + 1) * HEAD_DIM] = dyh * sgrad[:, sl]

        lane = _iota((tm, N_SMALL), 1)
        z, _, gval, beta = _small_fwd(ps_ref[...], bv_ref[...], al_ref[...])
        sig_z = _sigmoid(z)
        dsm = dsm_ref[...]
        in_g = (lane >= LANE_G) & (lane < LANE_G + HEADS)
        dgc = jnp.where(in_g, pltpu.roll(dsm, N_SMALL - (LANE_GC - LANE_G), 1), 0.0)
        dgl = jnp.where(in_g, pltpu.roll(dsm, N_SMALL - (LANE_GLAST - LANE_G), 1), 0.0)
        tri_c, ones_c = _chunk_masks(tm)
        dg = (_dg(tri_c, dgc, 0, 0, HI) + jnp.dot(ones_c, dgl, preferred_element_type=f32, precision=HI))
        dbeta = dsm
        dfb = jnp.where(lane < HEADS, df_ref[...], 0.0)
        for h in range(HEADS):
            dfb = dfb + jnp.where(lane == h, dfq_in_ref[h], 0.0)
        tri_u = (_iota((tm, tm), 1) >= _iota((tm, tm), 0)).astype(f32)
        dlogf = jnp.dot(tri_u, dfb, preferred_element_type=f32, precision=HI) + carry_sc[...]
        carry_sc[...] += jnp.sum(dfb, axis=0, keepdims=True)
        dff = dlogf * (1.0 - sig_z)
        dga = dg * (-jnp.exp(al_ref[...])) * sig_z
        dgb_small = dbeta * beta * (1.0 - beta)
        dps = jnp.where(lane < HEADS, dff, jnp.where(lane < 2 * HEADS, dga, jnp.where(lane < 3 * HEADS, dgb_small, 0.0)))
        dps_ref[...] = dps.astype(bf16)
        row = _iota((8, N_SMALL), 0)
        s0 = jnp.sum(dps, axis=0, keepdims=True)
        s1 = jnp.sum(jnp.where((lane >= HEADS) & (lane < 2 * HEADS), dg * gval, 0.0), axis=0, keepdims=True)
        sums_ref[...] += jnp.where(row == 0, s0, jnp.where(row == 1, s1, 0.0))

    def col(cb):
        return pl.BlockSpec((tm, WIDTH), lambda i: (nb - 1 - i, cb))

    def halo(cb):
        return pl.BlockSpec((HALO, WIDTH), lambda i: (jnp.maximum((nb - 1 - i) * (tm // HALO) - 1, 0), cb))

    vec = pl.BlockSpec((1, LANES), lambda i: (0, 0))
    row0 = pl.BlockSpec((tm, WIDTH), lambda i: (nb - 1 - i, 0))
    small = pl.BlockSpec((tm, N_SMALL), lambda i: (nb - 1 - i, 0))
    wide_bf = jax.ShapeDtypeStruct((s_len, WIDTH), bf16)
    return pl.pallas_call(
        body, name="post1", grid=(nb,),
        in_specs=[col(0), col(1), col(4), col(5), col(6), halo(4), halo(5), halo(6), small, vec, vec,
                  pl.BlockSpec((CONV_K, 3 * WIDTH), lambda i: (0, 0)), vec, vec,
                  row0, row0, row0, row0, row0, small, small,
                  pl.BlockSpec((HEADS, tm, 1), lambda i: (0, nb - 1 - i, 0))],
        out_specs=[row0, row0, pl.BlockSpec((tm, 3 * WIDTH), lambda i: (nb - 1 - i, 0)), small, vec, vec,
                   pl.BlockSpec((8, N_SMALL), lambda i: (0, 0))],
        out_shape=[wide_bf, wide_bf, jax.ShapeDtypeStruct((s_len, 3 * WIDTH), f32),
                   jax.ShapeDtypeStruct((s_len, N_SMALL), bf16), jax.ShapeDtypeStruct((1, LANES), f32),
                   jax.ShapeDtypeStruct((1, LANES), f32), jax.ShapeDtypeStruct((8, N_SMALL), f32)],
        scratch_shapes=[pltpu.VMEM((tm + HALO, WIDTH), f32), pltpu.VMEM((1, N_SMALL), f32)],
        compiler_params=_params("arbitrary"),
    )(p_main, p_main, p_main, p_main, p_main, p_main, p_main, p_main, p_small, qn_g, kn_g, conv_w, bvec, alog,
      dqs, dkn, dgq, dgk, dgv, d_small, df, df_query)


def _post2(p_main, dconv, conv_w):
    s_len = p_main.shape[0]
    tm = PREP_TM
    nb = s_len // tm

    def body(gq_ref, gk_ref, gv_ref, hq_ref, hk_ref, hv_ref, dc_ref, dnext_ref, cw_ref, dx_ref, dw_ref, xe_sc, de_sc):
        i = pl.program_id(0)

        @pl.when(i == 0)
        def _():
            dw_ref[...] = jnp.zeros_like(dw_ref)

        first, last = i == 0, i == nb - 1
        row = _iota((8, WIDTH), 0)
        for sec, (x_ref, halo_ref) in enumerate(((gq_ref, hq_ref), (gk_ref, hk_ref), (gv_ref, hv_ref))):
            cols = slice(sec * WIDTH, (sec + 1) * WIDTH)
            dc = dc_ref[:, cols]
            de_sc[0:tm, :] = dc
            de_sc[tm:, :] = jnp.where(last, 0.0, dnext_ref[:, cols])
            dx = cw_ref[pl.ds(CONV_K - 1, 1), cols] * dc
            for tap in range(CONV_K - 1):
                dx = dx + cw_ref[pl.ds(tap, 1), cols] * de_sc[pl.ds(CONV_K - 1 - tap, tm), :]
            dx_ref[:, cols] = dx.astype(bf16)
            xe_sc[0:HALO, :] = jnp.where(first, 0.0, halo_ref[...])
            xe_sc[HALO:, :] = x_ref[...]
            dw = jnp.zeros((8, WIDTH), f32)
            for tap in range(CONV_K):
                contrib = jnp.sum(dc * xe_sc[pl.ds(HALO - (CONV_K - 1) + tap, tm), :], axis=0, keepdims=True)
                dw = jnp.where(row == tap, contrib, dw)
            dw_ref[:, cols] += dw

    def col(cb):
        return pl.BlockSpec((tm, WIDTH), lambda i: (i, cb))

    def halo(cb):
        return pl.BlockSpec((HALO, WIDTH), lambda i: (jnp.maximum(i * (tm // HALO) - 1, 0), cb))

    return pl.pallas_call(
        body, name="post2", grid=(nb,),
        in_specs=[col(4), col(5), col(6), halo(4), halo(5), halo(6),
                  pl.BlockSpec((tm, 3 * WIDTH), lambda i: (i, 0)),
                  pl.BlockSpec((HALO, 3 * WIDTH), lambda i: (jnp.minimum((i + 1) * (tm // HALO), s_len // HALO - 1), 0)),
                  pl.BlockSpec((CONV_K, 3 * WIDTH), lambda i: (0, 0))],
        out_specs=[pl.BlockSpec((tm, 3 * WIDTH), lambda i: (i, 0)), pl.BlockSpec((8, 3 * WIDTH), lambda i: (0, 0))],
        out_shape=[jax.ShapeDtypeStruct((s_len, 3 * WIDTH), bf16), jax.ShapeDtypeStruct((8, 3 * WIDTH), f32)],
        scratch_shapes=[pltpu.VMEM((tm + HALO, WIDTH), f32), pltpu.VMEM((tm + HALO, WIDTH), f32)],
        compiler_params=_params("arbitrary"),
    )(p_main, p_main, p_main, p_main, p_main, p_main, dconv, dconv, conv_w)


def _in_proj_bwd(dp_pieces, dp_small, wt_main, wt_small):
    s_len = dp_small.shape[0]
    tm, tk = min(1024, s_len), WIDTH
    nk = N_MAIN // tk
    first_section = [sum(p.shape[1] // tk for p in dp_pieces[:n]) for n in range(len(dp_pieces))]
    n_pieces = len(dp_pieces)

    def body(*refs):
        piece_refs = refs[:n_pieces]
        dps_ref, w_ref, ws_ref, dh_ref = refs[n_pieces:]
        k = pl.program_id(1)

        @pl.when(k == 0)
        def _():
            dh_ref[...] = jnp.dot(dps_ref[...], ws_ref[...], preferred_element_type=f32)

        for piece, ref, first in zip(dp_pieces, piece_refs, first_section):
            @pl.when((k >= first) & (k < first + piece.shape[1] // tk))
            def _(ref=ref):
                dh_ref[...] += jnp.dot(ref[...], w_ref[...], preferred_element_type=f32)

    def piece_spec(piece, first):
        last = piece.shape[1] // tk - 1
        return pl.BlockSpec((tm, tk), lambda i, k: (i, jnp.clip(k - first, 0, last)))

    return pl.pallas_call(
        body, name="in_proj_bwd", grid=(s_len // tm, nk),
        in_specs=[piece_spec(p, f) for p, f in zip(dp_pieces, first_section)]
                 + [pl.BlockSpec((tm, N_SMALL), lambda i, k: (i, 0)),
                    pl.BlockSpec((tk, D_MODEL), lambda i, k: (k, 0)), pl.BlockSpec((N_SMALL, D_MODEL), lambda i, k: (0, 0))],
        out_specs=pl.BlockSpec((tm, D_MODEL), lambda i, k: (i, 0)),
        out_shape=jax.ShapeDtypeStruct((s_len, D_MODEL), f32),
        compiler_params=_params("parallel", "arbitrary"),
    )(*dp_pieces, dp_small, wt_main, wt_small)


def _adaln_bwd(dh, x, dy1, norm_g, scale1p):
    s_len = x.shape[0]
    tm = 256

    def body(dh_ref, x_ref, dy_ref, g_ref, sc_ref, dx_ref, dsh_ref, dsc_ref, dg_ref):
        @pl.when(pl.program_id(0) == 0)
        def _():
            dsh_ref[...] = jnp.zeros_like(dsh_ref)
            dsc_ref[...] = jnp.zeros_like(dsc_ref)
            dg_ref[...] = jnp.zeros_like(dg_ref)

        dh = dh_ref[...]
        xb = x_ref[...]
        r = lax.rsqrt(jnp.mean(xb * xb, axis=-1, keepdims=True) + EPS)
        xr = xb * r
        gain = g_ref[...]
        dsh_ref[...] += jnp.sum(dh, axis=0, keepdims=True)
        dsc_ref[...] += jnp.sum(dh * (xr * gain), axis=0, keepdims=True)
        dxn = dh * sc_ref[...]
        dg_ref[...] += jnp.sum(dxn * xr, axis=0, keepdims=True)
        tt = dxn * gain
        dx_ref[...] = r * (tt - xr * jnp.mean(tt * xr, axis=-1, keepdims=True)) + dy_ref[...]

    row = pl.BlockSpec((tm, D_MODEL), lambda i: (i, 0))
    vec = pl.BlockSpec((1, D_MODEL), lambda i: (0, 0))
    vshape = jax.ShapeDtypeStruct((1, D_MODEL), f32)
    return pl.pallas_call(
        body, name="adaln_bwd", grid=(s_len // tm,),
        in_specs=[row, row, row, vec, vec], out_specs=[row, vec, vec, vec],
        out_shape=[jax.ShapeDtypeStruct((s_len, D_MODEL), f32), vshape, vshape, vshape],
        compiler_params=_params("arbitrary"),
    )(dh, x, dy1, norm_g, scale1p)


def _adamw(name, w, g_stack, m, v, tr, tc=None):
    n_stack, rows, cols = g_stack.shape
    tc = cols if tc is None else tc

    def body(w_ref, g_ref, m_ref, v_ref, go_ref, d_ref, mo_ref, vo_ref):
        g = g_ref[0].astype(f32)
        for k in range(1, n_stack):
            g = g + g_ref[k].astype(f32)
        go_ref[0] = g
        m_new = ADAM_B1 * m_ref[0] + (1.0 - ADAM_B1) * g
        v_new = ADAM_B2 * v_ref[0] + (1.0 - ADAM_B2) * (g * g)
        mo_ref[0] = m_new
        vo_ref[0] = v_new
        m_hat = m_new / (1.0 - ADAM_B1 ** ADAM_STEP)
        v_hat = v_new / (1.0 - ADAM_B2 ** ADAM_STEP)
        d_ref[0] = -ADAM_LR * (m_hat / (jnp.sqrt(v_hat) + ADAM_EPS) + ADAM_WD * w_ref[0])

    blk = pl.BlockSpec((1, tr, tc), lambda i, j: (0, i, j))
    shape = jax.ShapeDtypeStruct((1, rows, cols), f32)
    return pl.pallas_call(
        body, name=name, grid=(rows // tr, cols // tc),
        in_specs=[blk, pl.BlockSpec((n_stack, tr, tc), lambda i, j: (0, i, j)), blk, blk],
        out_specs=[blk] * 4, out_shape=[shape] * 4,
        compiler_params=_params("parallel", "parallel"),
    )(w, g_stack, m, v)


def _w_ada_grad(c_all_t, dmod_pad):
    def body(c_ref, d_ref, o_ref):
        cv = c_ref[...]
        o_ref[...] = jnp.dot(cv * _sigmoid(cv), d_ref[...], preferred_element_type=f32, precision=HI)

    return pl.pallas_call(body, name="w_ada_grad",
                          out_shape=jax.ShapeDtypeStruct((c_all_t.shape[0], dmod_pad.shape[1]), f32),
                          compiler_params=_params())(c_all_t, dmod_pad)


SMALL_NAMES = ("norm_g", "b_ada", "b_fgate", "fox_qn_g", "fox_kn_g", "gdn_A_log", "gdn_dt_bias", "gdn_norm_g", "final_g")
SMALL_SIZES = (D_MODEL, 3 * D_MODEL, HEADS, HEAD_DIM, HEAD_DIM, HEADS, HEADS, HEAD_DIM, D_MODEL)
SMALL_PACK = 10752


def _pack(vectors, total):
    flat = jnp.concatenate([t.reshape(-1) for t in vectors])
    return jnp.pad(flat, (0, total - flat.shape[0])).reshape(1, total)


def _lanes(*pieces):
    parts, at = [], 0
    for off, vec in pieces:
        flat = vec.reshape(-1).astype(f32)
        parts += [jnp.zeros((off - at,), f32), flat]
        at = off + flat.shape[0]
    parts.append(jnp.zeros((LANES - at,), f32))
    return jnp.concatenate(parts).reshape(1, LANES)


def kernel(x, c, norm_g, w_ada, b_ada, w_in, b_fgate, fox_qn_g, fox_kn_g, gdn_conv_w, gdn_A_log, gdn_dt_bias, gdn_norm_g, w_out, final_g, loss_target, m_norm_g, m_w_ada, m_b_ada, m_w_in, m_b_fgate, m_fox_qn_g, m_fox_kn_g, m_gdn_conv_w, m_gdn_A_log, m_gdn_dt_bias, m_gdn_norm_g, m_w_out, m_final_g, v_norm_g, v_w_ada, v_b_ada, v_w_in, v_b_fgate, v_fox_qn_g, v_fox_kn_g, v_gdn_conv_w, v_gdn_A_log, v_gdn_dt_bias, v_gdn_norm_g, v_w_out, v_final_g):
    me = _my_index()
    s_len = x.shape[1]
    nq = s_len // FOX_T
    x2 = x.reshape(s_len, D_MODEL)
    tgt = loss_target.reshape(s_len, D_MODEL)
    ada_cols = w_ada.shape[2]
    in_cols = w_in.shape[2]
    conv_cols = gdn_conv_w.shape[2]

    (c_all,) = _gather_direct("gather_c", [c])
    c_all = c_all.reshape(N_DEV, D_MODEL)
    b_shard = lax.dynamic_slice(b_ada, (0, me * ada_cols), (1, ada_cols))
    mod_mine = _mod_shard(c_all, w_ada[0], b_shard)
    wt_shard = jnp.transpose(w_in[0])
    mod_all, wt_all, w_out_all, conv_all = _gather_two_level(
        "gather_weights", [mod_mine, wt_shard.astype(bf16), w_out[0].astype(bf16), gdn_conv_w[0]])
    mod = lax.dynamic_slice(mod_all, (0, me, 0), (N_DEV, 1, ada_cols)).reshape(1, 3 * D_MODEL)
    shift, scale, gate = mod[:, :D_MODEL], mod[:, D_MODEL:2 * D_MODEL], mod[:, 2 * D_MODEL:]
    scale1p = 1.0 + scale
    wt_full = wt_all.reshape(N_DEV * in_cols, D_MODEL)
    g0 = 4 * WIDTH + HEADS
    w_main = jnp.concatenate([wt_full[:4 * WIDTH], wt_full[g0:g0 + 4 * WIDTH]], axis=0)
    w_small = jnp.concatenate([wt_full[4 * WIDTH:g0], wt_full[g0 + 4 * WIDTH:],
                               jnp.zeros((N_SMALL - 3 * HEADS, D_MODEL), bf16)], axis=0)
    w_out_full = w_out_all.reshape(2 * WIDTH, D_MODEL)
    conv_full = jnp.transpose(conv_all, (1, 0, 2)).reshape(CONV_K, 3 * WIDTH)

    qn_g, kn_g, gn_g = fox_qn_g.reshape(1, LANES), fox_kn_g.reshape(1, LANES), gdn_norm_g.reshape(1, LANES)
    bvec = _lanes((0, b_fgate), (HEADS, gdn_dt_bias))
    alog = _lanes((HEADS, gdn_A_log))
    fg = final_g.reshape(1, D_MODEL)

    h_bf = _norm_mod(x2, norm_g, scale1p, shift)
    p_main, p_small = _in_proj(h_bf, w_main, w_small)
    qs, kn, vb, gq, gk, gv, small = _prep(p_main, p_small, qn_g, kn_g, conv_full, bvec, alog)
    fox_o, lse = _fox_fwd(qs, kn, vb)
    gu, gw, gqg, gkd, gattn, t_inv, eg_last = _gdn_intra_fwd(gq, gk, gv, small)
    gdn_o, states = _gdn_scan_fwd(gu, gw, gqg, gkd, gattn, eg_last)
    mixed = _mix_fwd(fox_o, gdn_o, p_main, gn_g)

    loss_row, dy1, dz, dmixed, d_final_g, d_gate = _out_head(mixed, w_out_full, x2, tgt, gate, fg)
    loss = lax.psum(loss_row[0, 0], AXES)
    dw_out = _matmul_tn("dw_out", mixed, dz, bf16)
    do_fox, delta, dfz, dgz, dgdn_o, d_gn_g = _mix_bwd(dmixed, fox_o, gdn_o, p_main, gn_g)
    dqs, dkn, dvf, df_key, df_query = _fox_bwd(qs, kn, vb, do_fox, lse, delta)
    du, dw, dqg, dkd, dattn, deg = _gdn_scan_bwd(gu, gw, gqg, gkd, gattn, eg_last, states, dgdn_o)
    dgq, dgk, dgv, d_small = _gdn_intra_bwd(gq, gk, gv, small, t_inv, du, dw, dqg, dkd, dattn, deg)
    df_small = jnp.pad(jnp.transpose(df_key.reshape(HEADS, s_len)), ((0, 0), (0, N_SMALL - HEADS)))
    dfq, dfk, dconv, dp_small, d_qn_g, d_kn_g, sums = _post1(
        p_main, p_small, qn_g, kn_g, conv_full, bvec, alog, dqs, dkn, dgq, dgk, dgv, d_small, df_small, df_query)
    dgqkv, d_conv = _post2(p_main, dconv, conv_full)
    dp_pieces = [dfq, dfk, dvf, dfz, dgqkv, dgz]
    dh = _in_proj_bwd(dp_pieces, dp_small, w_main, w_small)
    grad_x, d_shift, d_scale, d_norm_g = _adaln_bwd(dh, x2, dy1, norm_g, scale1p)
    dw_rows = [_matmul_tn("dw_main_%d" % n, piece, h_bf, bf16) for n, piece in enumerate(dp_pieces)]
    dw_small = _matmul_tn("dw_small", dp_small, h_bf, bf16)
    dw_in_full = jnp.concatenate(dw_rows[:4] + [dw_small[:HEADS]] + dw_rows[4:] + [dw_small[HEADS:3 * HEADS]],
                                 axis=0)
    dw_in_parts = dw_in_full.reshape(N_DEV, in_cols, D_MODEL)
    dw_out_parts = dw_out.reshape(N_DEV, w_out.shape[1], D_MODEL)

    dmod = jnp.concatenate([d_shift, d_scale, d_gate], axis=1)
    small_grads = _pack([d_norm_g, dmod, sums[0, :HEADS], d_qn_g, d_kn_g, sums[1, HEADS:2 * HEADS],
                         sums[0, HEADS:2 * HEADS], d_gn_g, d_final_g], SMALL_PACK)
    conv_grad = d_conv[:CONV_K]
    pair_in, pair_out = _pair_exchange("pair_grads", [dw_in_parts, dw_out_parts])
    core = lax.axis_index("c").astype(jnp.int32).reshape(1)
    dw_in_recv, dw_out_recv = _chip_exchange(
        "chip_grads", [_pair_sum("pair_sum_w_in", dw_in_parts, pair_in, core),
                       _pair_sum("pair_sum_w_out", dw_out_parts, pair_out, core)])
    small_all, conv_all_g = _gather_direct("gather_small_grads", [small_grads, conv_grad])

    outs = {}
    to_t = lambda t: jnp.transpose(t, (0, 2, 1))
    outs["w_in"] = tuple(to_t(t) for t in _adamw("adamw_w_in", to_t(w_in), dw_in_recv, to_t(m_w_in), to_t(v_w_in),
                                                  in_cols, 256))
    outs["w_out"] = _adamw("adamw_w_out", w_out, dw_out_recv, m_w_out, v_w_out, 128)
    conv_mine = lax.dynamic_slice(jnp.transpose(conv_all_g.reshape(N_DEV, CONV_K, N_DEV, conv_cols), (0, 2, 1, 3)),
                                  (0, me, 0, 0), (N_DEV, 1, CONV_K, conv_cols)).reshape(N_DEV, CONV_K, conv_cols)
    outs["gdn_conv_w"] = _adamw("adamw_conv", gdn_conv_w, conv_mine, m_gdn_conv_w, v_gdn_conv_w, CONV_K)
    small_all = small_all.reshape(N_DEV, 1, SMALL_PACK)
    dmod_all = small_all[:, 0, D_MODEL:D_MODEL + 3 * D_MODEL]
    dmod_mine = lax.dynamic_slice(dmod_all, (0, me * ada_cols), (N_DEV, ada_cols))
    c_all_t = jnp.pad(jnp.transpose(c_all), ((0, 0), (0, LANES - N_DEV)))
    g_w_ada = _w_ada_grad(c_all_t, jnp.pad(dmod_mine, ((0, LANES - N_DEV), (0, 0))))
    outs["w_ada"] = _adamw("adamw_w_ada", w_ada, g_w_ada[None], m_w_ada, v_w_ada, 256)
    given = dict(norm_g=(norm_g, m_norm_g, v_norm_g), b_ada=(b_ada, m_b_ada, v_b_ada), b_fgate=(b_fgate, m_b_fgate, v_b_fgate),
                 fox_qn_g=(fox_qn_g, m_fox_qn_g, v_fox_qn_g), fox_kn_g=(fox_kn_g, m_fox_kn_g, v_fox_kn_g),
                 gdn_A_log=(gdn_A_log, m_gdn_A_log, v_gdn_A_log), gdn_dt_bias=(gdn_dt_bias, m_gdn_dt_bias, v_gdn_dt_bias),
                 gdn_norm_g=(gdn_norm_g, m_gdn_norm_g, v_gdn_norm_g), final_g=(final_g, m_final_g, v_final_g))
    w_pack = _pack([given[n][0] for n in SMALL_NAMES], SMALL_PACK)
    m_pack = _pack([given[n][1] for n in SMALL_NAMES], SMALL_PACK)
    v_pack = _pack([given[n][2] for n in SMALL_NAMES], SMALL_PACK)
    packed = _adamw("adamw_small", w_pack[None], small_all, m_pack[None], v_pack[None], 1)
    off = 0
    for n, size in zip(SMALL_NAMES, SMALL_SIZES):
        outs[n] = tuple(t[0, 0, off:off + size].reshape(given[n][0].shape) for t in packed)
        off += size

    order = ("norm_g", "w_ada", "b_ada", "w_in", "b_fgate", "fox_qn_g", "fox_kn_g", "gdn_conv_w", "gdn_A_log",
             "gdn_dt_bias", "gdn_norm_g", "w_out", "final_g")
    result = [loss, grad_x.reshape(x.shape)]
    for part in range(4):
        result += [outs[n][part] for n in order]
    return tuple(result)
```

```python
import math

import jax
import jax.numpy as jnp
from jax import lax
from jax.experimental import pallas as pl
from jax.experimental.pallas import tpu as pltpu

f32 = jnp.float32
bf16 = jnp.bfloat16
HI = lax.Precision.HIGHEST

N_DEV = 8
AXES = ("x", "y", "c")
D_MODEL = 2048
HEADS = 8
HEAD_DIM = 128
WIDTH = HEADS * HEAD_DIM
CHUNK = 64
CONV_K = 4
EPS = 1e-6
QK_SCALE = HEAD_DIM ** -0.5
LOG2E = 1.0 / math.log(2.0)
LN2 = math.log(2.0)
N_MAIN = 8 * WIDTH
N_SMALL = 128
LANE_F, LANE_G, LANE_BETA, LANE_GC, LANE_GLAST = 0, 8, 16, 24, 32
IN_WIDTH = 8 * WIDTH + 3 * HEADS
LANES = 128
VMEM_LIMIT = 56 * 1024 * 1024

ADAM_LR, ADAM_B1, ADAM_B2, ADAM_EPS, ADAM_WD, ADAM_STEP = 0.001, 0.9, 0.999, 1e-08, 0.01, 10


def _params(*sem):
    return pltpu.CompilerParams(dimension_semantics=sem, vmem_limit_bytes=VMEM_LIMIT)


def _iota(shape, dim):
    return lax.broadcasted_iota(jnp.int32, shape, dim)


def _sigmoid(z):
    return 1.0 / (1.0 + jnp.exp(-z))


def _softplus_parts(z):
    t = jnp.log(1.0 + jnp.exp(-jnp.abs(z)))
    return jnp.minimum(z, 0.0) - t, jnp.maximum(z, 0.0) + t


def _dg(a, b, ca, cb, prec=None):
    if a.ndim == 3:
        dims = (((ca + 1,), (cb + 1,)), ((0,), (0,)))
    else:
        dims = (((ca,), (cb,)), ((), ()))
    return lax.dot_general(a, b, dims, preferred_element_type=f32, precision=prec)


def _dot_bf16(a, b, ca, cb):
    return _dg(a.astype(bf16), b.astype(bf16), ca, cb)


def _split_bf16(a):
    hi = a.astype(bf16)
    return hi, (a - hi.astype(f32)).astype(bf16)


def _dot_3pass(a, b, ca, cb):
    a_hi, a_lo = _split_bf16(a)
    b_hi, b_lo = _split_bf16(b)
    return _dg(a_hi, b_hi, ca, cb) + (_dg(a_hi, b_lo, ca, cb) + _dg(a_lo, b_hi, ca, cb))


def _make_mm(dot):
    def nn_(a, b):
        return dot(a, b, 1, 0)

    def nt_(a, b):
        return dot(a, b, 1, 1)

    def tn_(a, b):
        return dot(a, b, 0, 0)

    @jax.custom_vjp
    def nn(a, b):
        return nn_(a, b)

    @jax.custom_vjp
    def nt(a, b):
        return nt_(a, b)

    @jax.custom_vjp
    def tn(a, b):
        return tn_(a, b)

    nn.defvjp(lambda a, b: (nn_(a, b), (a, b)), lambda r, g: (nt_(g, r[1]), tn_(r[0], g)))
    nt.defvjp(lambda a, b: (nt_(a, b), (a, b)), lambda r, g: (nn_(g, r[1]), tn_(g, r[0])))
    tn.defvjp(lambda a, b: (tn_(a, b), (a, b)), lambda r, g: (nt_(r[1], g), nn_(r[0], g)))
    return (nn_, nt_, tn_), (nn, nt, tn)


_BF_PLAIN, _BF_VJP = _make_mm(_dot_bf16)
_X3_PLAIN, _X3_VJP = _make_mm(_dot_3pass)


def _inv_unit_lower_many(ms):
    c = CHUNK
    nn = _X3_PLAIN[0]
    eye = (_iota((c, c), 0) == _iota((c, c), 1)).astype(f32)
    top = _iota((2 * c, c), 0) < c
    xs = [jnp.concatenate([eye - m, nn(m, m)], axis=0) for m in ms]
    for _ in range(int(math.log2(CHUNK)) - 2):
        xs = [jnp.where(top, x, 0.0) + nn(x, x[c:]) for x in xs]
    return [x[:c] + nn(x[:c], x[c:]) for x in xs]


@jax.custom_vjp
def _inv_given(m, t):
    return t


_inv_given.defvjp(lambda m, t: (t, t),
                  lambda t, g: (-_X3_PLAIN[1](_X3_PLAIN[2](t, g), t), jnp.zeros_like(t)))

SUBLANES = 8


def _gdn_intra_pre(q, k, v, gc_b, g_last_b, beta_b, bnt):
    c = CHUNK
    r_i, c_i = _iota((c, c), 0), _iota((c, c), 1)
    lower, strict = r_i >= c_i, r_i > c_i
    gc_i = gc_b[..., :c]
    gc_j = jnp.swapaxes(gc_i, -1, -2)
    decay = jnp.where(lower, jnp.exp(jnp.where(lower, gc_i - gc_j, 0.0)), 0.0)
    kb = k * beta_b
    both = bnt(jnp.concatenate([kb, q], axis=-2), k)
    m = jnp.where(strict, both[..., :c, :] * decay, 0.0)
    attn = jnp.where(lower, both[..., c:, :] * decay, 0.0)
    eg = jnp.exp(gc_b)
    rhs = jnp.concatenate([v * beta_b, kb * eg], axis=-1)
    k_dec = k * jnp.exp(g_last_b - gc_b)
    eg_last = jnp.exp(g_last_b[..., :SUBLANES, :])
    return m, rhs, q * eg, k_dec, attn, eg_last


def _gdn_intra(q, k, v, gc_b, g_last_b, beta_b, t_known):
    m, rhs, qg, k_dec, attn, eg_last = _gdn_intra_pre(q, k, v, gc_b, g_last_b, beta_b, _BF_VJP[1])
    return _X3_VJP[0](_inv_given(m, t_known), rhs), qg, k_dec, attn, eg_last


def _scale_rows(s, eg_last):
    return (s.reshape(HEAD_DIM // SUBLANES, SUBLANES, HEAD_DIM) * eg_last[None]).reshape(HEAD_DIM, HEAD_DIM)


def _my_index():
    return 4 * lax.axis_index("x") + 2 * lax.axis_index("y") + lax.axis_index("c")


def _peer(d):
    x, y, c = lax.axis_index("x"), lax.axis_index("y"), lax.axis_index("c")
    px, py, pc = (x + (d >> 2)) % 2, (y + ((d >> 1) & 1)) % 2, (c + (d & 1)) % 2
    return (px, py, pc), 4 * px + 2 * py + pc


def _gather_direct(name, arrays):
    n = len(arrays)

    def body(*refs):
        srcs, dsts = refs[:n], refs[n:2 * n]
        send_sems, recv_sems, local_sems = refs[2 * n:]
        me = _my_index()

        def copy(k, d, started):
            peer, pidx = _peer(d)
            return pltpu.make_async_remote_copy(
                src_ref=srcs[k], dst_ref=dsts[k].at[me if started else pidx], send_sem=send_sems.at[k * 7 + d - 1],
                recv_sem=recv_sems.at[k * 7 + d - 1], device_id=peer, device_id_type=pl.DeviceIdType.MESH)

        local = [pltpu.make_async_copy(srcs[k], dsts[k].at[me], local_sems.at[k]) for k in range(n)]
        sends = [copy(k, d, True) for k in range(n) for d in range(1, N_DEV)]
        for cp in local + sends:
            cp.start()
        for k in range(n):
            for d in range(1, N_DEV):
                copy(k, d, False).wait_recv()
        for cp in sends:
            cp.wait_send()
        for cp in local:
            cp.wait()

    out_shape = [jax.ShapeDtypeStruct((N_DEV,) + a.shape, a.dtype) for a in arrays]
    any_spec = pl.BlockSpec(memory_space=pl.ANY)
    return pl.pallas_call(
        body, name=name, out_shape=out_shape, in_specs=[any_spec] * n, out_specs=[any_spec] * n,
        scratch_shapes=[pltpu.SemaphoreType.DMA((7 * n,)), pltpu.SemaphoreType.DMA((7 * n,)),
                        pltpu.SemaphoreType.DMA((n,))],
        compiler_params=pltpu.CompilerParams(has_side_effects=True),
    )(*arrays)


N_CHIPS = 4


def _pair_exchange(name, arrays):
    n = len(arrays)

    def body(*refs):
        srcs, dsts = refs[:n], refs[n:2 * n]
        send_sems, recv_sems = refs[2 * n:]
        x, y, c = lax.axis_index("x"), lax.axis_index("y"), lax.axis_index("c")
        sibling = (x, y, 1 - c)

        def copy(k, j):
            return pltpu.make_async_remote_copy(
                src_ref=srcs[k].at[2 * j + (1 - c)], dst_ref=dsts[k].at[j], send_sem=send_sems.at[k * N_CHIPS + j],
                recv_sem=recv_sems.at[k * N_CHIPS + j], device_id=sibling, device_id_type=pl.DeviceIdType.MESH)

        copies = [copy(k, j) for k in range(n) for j in range(N_CHIPS)]
        for cp in copies:
            cp.start()
        for cp in copies:
            cp.wait_recv()
        for cp in copies:
            cp.wait_send()

    any_spec = pl.BlockSpec(memory_space=pl.ANY)
    return pl.pallas_call(
        body, name=name, out_shape=[jax.ShapeDtypeStruct((N_CHIPS,) + a.shape[1:], a.dtype) for a in arrays],
        in_specs=[any_spec] * n, out_specs=[any_spec] * n,
        scratch_shapes=[pltpu.SemaphoreType.DMA((N_CHIPS * n,)), pltpu.SemaphoreType.DMA((N_CHIPS * n,))],
        compiler_params=pltpu.CompilerParams(has_side_effects=True),
    )(*arrays)


def _chip_exchange(name, arrays):
    n = len(arrays)

    def body(*refs):
        srcs, dsts = refs[:n], refs[n:2 * n]
        send_sems, recv_sems, local_sems = refs[2 * n:]
        x, y, c = lax.axis_index("x"), lax.axis_index("y"), lax.axis_index("c")
        my_chip = 2 * x + y

        def peer(d):
            px, py = (x + (d >> 1)) % 2, (y + (d & 1)) % 2
            return (px, py, c), 2 * px + py

        def remote(k, d, started):
            to, chip = peer(d)
            return pltpu.make_async_remote_copy(
                src_ref=srcs[k].at[chip], dst_ref=dsts[k].at[my_chip if started else chip],
                send_sem=send_sems.at[k * 3 + d - 1], recv_sem=recv_sems.at[k * 3 + d - 1],
                device_id=to, device_id_type=pl.DeviceIdType.MESH)

        local = [pltpu.make_async_copy(srcs[k].at[my_chip], dsts[k].at[my_chip], local_sems.at[k]) for k in range(n)]
        sends = [remote(k, d, True) for k in range(n) for d in range(1, N_CHIPS)]
        for cp in local + sends:
            cp.start()
        for k in range(n):
            for d in range(1, N_CHIPS):
                remote(k, d, False).wait_recv()
        for cp in sends:
            cp.wait_send()
        for cp in local:
            cp.wait()

    any_spec = pl.BlockSpec(memory_space=pl.ANY)
    return pl.pallas_call(
        body, name=name, out_shape=[jax.ShapeDtypeStruct(a.shape, a.dtype) for a in arrays],
        in_specs=[any_spec] * n, out_specs=[any_spec] * n,
        scratch_shapes=[pltpu.SemaphoreType.DMA((3 * n,)), pltpu.SemaphoreType.DMA((3 * n,)),
                        pltpu.SemaphoreType.DMA((n,))],
        compiler_params=pltpu.CompilerParams(has_side_effects=True),
    )(*arrays)


def _pair_sum(name, parts, received, core):
    n_blocks, rows, cols = received.shape
    tr = rows if rows % 256 else 256

    def body(core_ref, mine_ref, recv_ref, o_ref):
        o_ref[...] = (mine_ref[...].astype(f32) + recv_ref[...].astype(f32)).astype(bf16)

    return pl.pallas_call(
        body, name=name,
        grid_spec=pltpu.PrefetchScalarGridSpec(
            num_scalar_prefetch=1, grid=(n_blocks, rows // tr),
            in_specs=[pl.BlockSpec((1, tr, cols), lambda j, i, core_ref: (2 * j + core_ref[0], i, 0)),
                      pl.BlockSpec((1, tr, cols), lambda j, i, core_ref: (j, i, 0))],
            out_specs=pl.BlockSpec((1, tr, cols), lambda j, i, core_ref: (j, i, 0))),
        out_shape=jax.ShapeDtypeStruct((n_blocks, rows, cols), bf16),
        compiler_params=_params("parallel", "parallel"),
    )(core, parts, received)


def _gather_two_level(name, arrays):
    n = len(arrays)

    def body(*refs):
        srcs, dsts = refs[:n], refs[n:2 * n]
        send_sems, recv_sems, local_sems = refs[2 * n:]
        x, y, c = lax.axis_index("x"), lax.axis_index("y"), lax.axis_index("c")
        sibling = (x, y, 1 - c)
        chips = [((x + 1) % 2, y), (x, (y + 1) % 2), ((x + 1) % 2, (y + 1) % 2)]

        def index(px, py, pc):
            return 4 * px + 2 * py + pc

        def copy(k, slot, block, to, src=None):
            return pltpu.make_async_remote_copy(
                src_ref=dsts[k].at[index(*block)] if src is None else src, dst_ref=dsts[k].at[index(*block)],
                send_sem=send_sems.at[k * 7 + slot], recv_sem=recv_sems.at[k * 7 + slot],
                device_id=to, device_id_type=pl.DeviceIdType.MESH)

        me = (x, y, c)
        local = [pltpu.make_async_copy(srcs[k], dsts[k].at[index(*me)], local_sems.at[k]) for k in range(n)]
        first = [copy(k, 0, me, sibling, src=srcs[k]) for k in range(n)]
        first += [copy(k, 1 + j, me, (*chip, c), src=srcs[k]) for j, chip in enumerate(chips) for k in range(n)]
        for cp in local + first:
            cp.start()
        passed = []
        for j, chip in enumerate(chips):
            for k in range(n):
                copy(k, 1 + j, (*chip, c), me).wait_recv()
                fwd = copy(k, 4 + j, (*chip, c), sibling)
                fwd.start()
                passed.append(fwd)
        for k in range(n):
            copy(k, 0, sibling, me).wait_recv()
            for j, chip in enumerate(chips):
                copy(k, 4 + j, (*chip, 1 - c), me).wait_recv()
        for cp in first + passed:
            cp.wait_send()
        for cp in local:
            cp.wait()

    any_spec = pl.BlockSpec(memory_space=pl.ANY)
    return pl.pallas_call(
        body, name=name, out_shape=[jax.ShapeDtypeStruct((N_DEV,) + a.shape, a.dtype) for a in arrays],
        in_specs=[any_spec] * n, out_specs=[any_spec] * n,
        scratch_shapes=[pltpu.SemaphoreType.DMA((7 * n,)), pltpu.SemaphoreType.DMA((7 * n,)),
                        pltpu.SemaphoreType.DMA((n,))],
        compiler_params=pltpu.CompilerParams(has_side_effects=True),
    )(*arrays)


def _mod_shard(c_all, w_ada, b_shard):
    def body(c_ref, w_ref, b_ref, o_ref):
        cv = c_ref[...]
        ca = cv * _sigmoid(cv)
        o_ref[...] = jnp.dot(ca.astype(bf16), w_ref[...].astype(bf16), preferred_element_type=f32) + b_ref[...]

    return pl.pallas_call(body, name="mod_shard", out_shape=jax.ShapeDtypeStruct((N_DEV, w_ada.shape[1]), f32),
                          compiler_params=_params())(c_all, w_ada, b_shard)


def _norm_mod(x, norm_g, scale1p, shift):
    s_len = x.shape[0]
    tm = 512

    def body(x_ref, g_ref, sc_ref, sh_ref, h_ref):
        xb = x_ref[...]
        r = lax.rsqrt(jnp.mean(xb * xb, axis=-1, keepdims=True) + EPS)
        h_ref[...] = ((xb * r * g_ref[...]) * sc_ref[...] + sh_ref[...]).astype(bf16)

    row = pl.BlockSpec((tm, D_MODEL), lambda i: (i, 0))
    vec = pl.BlockSpec((1, D_MODEL), lambda i: (0, 0))
    return pl.pallas_call(body, name="norm_mod", grid=(s_len // tm,), in_specs=[row, vec, vec, vec], out_specs=row,
                          out_shape=jax.ShapeDtypeStruct((s_len, D_MODEL), bf16),
                          compiler_params=_params("parallel"))(x, norm_g, scale1p, shift)


def _in_proj(h, wt_main, wt_small):
    s_len = h.shape[0]
    tm, tn = min(1024, s_len), 1024

    def body(h_ref, w_ref, ws_ref, p_ref, ps_ref):
        @pl.when(pl.program_id(1) == 0)
        def _():
            ps_ref[...] = _dg(h_ref[...], ws_ref[...], 1, 1)

        p_ref[...] = _dg(h_ref[...], w_ref[...], 1, 1)

    return pl.pallas_call(
        body, name="in_proj", grid=(s_len // tm, N_MAIN // tn),
        in_specs=[pl.BlockSpec((tm, D_MODEL), lambda i, j: (i, 0)),
                  pl.BlockSpec((tn, D_MODEL), lambda i, j: (j, 0)),
                  pl.BlockSpec((N_SMALL, D_MODEL), lambda i, j: (0, 0))],
        out_specs=[pl.BlockSpec((tm, tn), lambda i, j: (i, j)),
                   pl.BlockSpec((tm, N_SMALL), lambda i, j: (i, 0))],
        out_shape=[jax.ShapeDtypeStruct((s_len, N_MAIN), f32), jax.ShapeDtypeStruct((s_len, N_SMALL), f32)],
        compiler_params=_params("parallel", "arbitrary"),
    )(h, wt_main, wt_small)


PREP_TM = 256
HALO = 8


def _conv_section(xe_ref, cw_ref, cols, tm):
    acc = cw_ref[pl.ds(CONV_K - 1, 1), cols] * xe_ref[pl.ds(HALO, tm), :]
    for tap in range(CONV_K - 1):
        acc = acc + cw_ref[pl.ds(tap, 1), cols] * xe_ref[pl.ds(HALO - (CONV_K - 1) + tap, tm), :]
    return acc


def _small_fwd(ps, bvec, alog):
    z = ps + bvec
    logsig, softp = _softplus_parts(z)
    gval = -jnp.exp(alog) * softp
    beta = _sigmoid(ps)
    return z, logsig, gval, beta


def _head_lane(block, lane):
    return jnp.sum(jnp.where(_iota(block.shape, 1) == lane, block, 0.0), axis=1, keepdims=True)


def _head_slab(block, lane):
    return jnp.broadcast_to(_head_lane(block, lane), block.shape)


def _chunk_masks(tm):
    r, c = _iota((tm, tm), 0), _iota((tm, tm), 1)
    same = (r // CHUNK) == (c // CHUNK)
    return (same & (r >= c)).astype(f32), same.astype(f32)


def _prep(p_main, p_small, qn_g, kn_g, conv_w, bvec, alog):
    s_len = p_main.shape[0]
    tm = PREP_TM
    nb = s_len // tm

    def body(fq_ref, fk_ref, fv_ref, gq_ref, gk_ref, gv_ref, hq_ref, hk_ref, hv_ref, ps_ref, qg_ref, kg_ref,
             cw_ref, bv_ref, al_ref,
             qs_ref, kn_ref, vb_ref, gqo_ref, gko_ref, gvo_ref, small_ref, xe_sc, carry_sc):
        i = pl.program_id(0)

        @pl.when(i == 0)
        def _():
            carry_sc[...] = jnp.zeros_like(carry_sc)

        vb_ref[...] = fv_ref[...].astype(bf16)

        first = i == 0
        for sec, (x_ref, halo_ref, o_ref) in enumerate(((gq_ref, hq_ref, gqo_ref), (gk_ref, hk_ref, gko_ref),
                                                        (gv_ref, hv_ref, gvo_ref))):
            xe_sc[0:HALO, :] = jnp.where(first, 0.0, halo_ref[...])
            xe_sc[HALO:, :] = x_ref[...]
            cv = _conv_section(xe_sc, cw_ref, slice(sec * WIDTH, (sec + 1) * WIDTH), tm)
            y = cv * _sigmoid(cv)
            if sec == 2:
                o_ref[...] = y
            else:
                mul = QK_SCALE if sec == 0 else 1.0
                for h in range(HEADS):
                    sl = slice(h * HEAD_DIM, (h + 1) * HEAD_DIM)
                    yh = y[:, sl]
                    o_ref[:, sl] = yh * (lax.rsqrt(jnp.sum(yh * yh, axis=-1, keepdims=True) + EPS) * mul)

        lane = _iota((tm, N_SMALL), 1)
        _, logsig, gval, beta = _small_fwd(ps_ref[...], bv_ref[...], al_ref[...])
        lf = jnp.where(lane < HEADS, logsig, 0.0)
        tri = (_iota((tm, tm), 0) >= _iota((tm, tm), 1)).astype(f32)
        fcum = jnp.dot(tri, lf, preferred_element_type=f32, precision=HI) + carry_sc[...]
        carry_sc[...] += jnp.sum(lf, axis=0, keepdims=True)
        tri_c, ones_c = _chunk_masks(tm)
        g_lanes = jnp.where((lane >= LANE_G) & (lane < LANE_G + HEADS), gval, 0.0)
        gc = jnp.dot(tri_c, g_lanes, preferred_element_type=f32, precision=HI)
        g_last = jnp.dot(ones_c, g_lanes, preferred_element_type=f32, precision=HI)
        small = jnp.where(lane < LANE_G, fcum, jnp.where(lane < LANE_BETA, gval, jnp.where(lane < LANE_GC, beta, 0.0)))
        small_ref[...] = small + pltpu.roll(gc, LANE_GC - LANE_G, 1) + pltpu.roll(g_last, LANE_GLAST - LANE_G, 1)

        qg, kg = qg_ref[...], kg_ref[...]
        f2 = fcum * LOG2E
        for h in range(HEADS):
            sl = slice(h * HEAD_DIM, (h + 1) * HEAD_DIM)
            q = fq_ref[:, sl]
            rq = lax.rsqrt(jnp.mean(q * q, axis=-1, keepdims=True) + EPS)
            k = fk_ref[:, sl]
            rk = lax.rsqrt(jnp.mean(k * k, axis=-1, keepdims=True) + EPS)
            f_col = _head_lane(f2, LANE_F + h)
            hi = f_col.astype(bf16).astype(f32)
            mid = (f_col - hi).astype(bf16).astype(f32)
            lo = f_col - hi - mid
            q_bias = jnp.where(lane == 0, hi, jnp.where(lane == 1, mid, jnp.where(lane == 2, lo,
                                                                                  jnp.where(lane < 6, 1.0, 0.0))))
            k_bias = jnp.where(lane < 3, 1.0, jnp.where(lane == 3, -hi, jnp.where(lane == 4, -mid,
                                                                                 jnp.where(lane == 5, -lo, 0.0))))
            base = 2 * h * HEAD_DIM
            qs_ref[:, base:base + HEAD_DIM] = (q * rq * qg * (QK_SCALE * LOG2E)).astype(bf16)
            qs_ref[:, base + HEAD_DIM:base + 2 * HEAD_DIM] = q_bias.astype(bf16)
            kn_ref[:, base:base + HEAD_DIM] = (k * rk * kg).astype(bf16)
            kn_ref[:, base + HEAD_DIM:base + 2 * HEAD_DIM] = k_bias.astype(bf16)

    def col(cb):
        return pl.BlockSpec((tm, WIDTH), lambda i: (i, cb))

    def halo(cb):
        return pl.BlockSpec((HALO, WIDTH), lambda i: (jnp.maximum(i * (tm // HALO) - 1, 0), cb))

    vec = pl.BlockSpec((1, LANES), lambda i: (0, 0))
    wide_f32 = jax.ShapeDtypeStruct((s_len, WIDTH), f32)
    wide_bf = jax.ShapeDtypeStruct((s_len, WIDTH), bf16)
    out_col = pl.BlockSpec((tm, WIDTH), lambda i: (i, 0))
    return pl.pallas_call(
        body, name="prep", grid=(nb,),
        in_specs=[col(0), col(1), col(2), col(4), col(5), col(6), halo(4), halo(5), halo(6),
                  pl.BlockSpec((tm, N_SMALL), lambda i: (i, 0)), vec, vec,
                  pl.BlockSpec((CONV_K, 3 * WIDTH), lambda i: (0, 0)), vec, vec],
        out_specs=[pl.BlockSpec((tm, 2 * WIDTH), lambda i: (i, 0))] * 2 + [out_col] * 4
                  + [pl.BlockSpec((tm, N_SMALL), lambda i: (i, 0))],
        out_shape=[jax.ShapeDtypeStruct((s_len, 2 * WIDTH), bf16)] * 2 + [wide_bf, wide_f32, wide_f32, wide_f32,
                                                                          jax.ShapeDtypeStruct((s_len, N_SMALL), f32)],
        scratch_shapes=[pltpu.VMEM((tm + HALO, WIDTH), f32), pltpu.VMEM((1, N_SMALL), f32)],
        compiler_params=_params("arbitrary"),
    )(p_main, p_main, p_main, p_main, p_main, p_main, p_main, p_main, p_main, p_small, qn_g, kn_g, conv_w, bvec, alog)


FOX_T = 1024
NEG_BIG = -1e30


def _fox_fwd(qs, kn, vb):
    s_len = qs.shape[0]
    t = FOX_T
    nq = s_len // t

    half = t // 2

    def body(q_ref, k_ref, v_ref, o_ref, lse_ref):
        qi = pl.program_id(1)
        q = q_ref[...]

        def update(q_rows, k_rows, carry, mask):
            m, l, acc = carry
            s = _dg(q_rows, k_ref[k_rows, :], 1, 1)
            if mask is not None:
                s = jnp.where(mask, s, NEG_BIG)
            m_new = jnp.maximum(m, jnp.max(s, axis=-1, keepdims=True))
            p = jnp.exp2(s - m_new)
            alpha = jnp.exp2(m - m_new)
            l = alpha * l + jnp.sum(p, axis=-1, keepdims=True)
            acc = alpha * acc + jnp.dot(p.astype(bf16), v_ref[k_rows, :], preferred_element_type=f32)
            return m_new, l, acc

        init = (jnp.full((t, 1), NEG_BIG, f32), jnp.zeros((t, 1), f32), jnp.zeros((t, HEAD_DIM), f32))
        carry = lax.fori_loop(
            0, qi, lambda j, c: update(q, pl.ds(pl.multiple_of(j * t, t), t), c, None), init)
        start = pl.multiple_of(qi * t, t)
        carry = update(q, pl.ds(start, half), carry, _iota((t, half), 0) >= _iota((t, half), 1))
        lower = update(q[half:], pl.ds(pl.multiple_of(qi * t + half, half), half), tuple(c[half:] for c in carry),
                       _iota((half, half), 0) >= _iota((half, half), 1))
        m, l, acc = (jnp.concatenate([c[:half], lo], axis=0) for c, lo in zip(carry, lower))
        o_ref[...] = acc / l
        lse_ref[0] = m + jnp.log2(l)

    return pl.pallas_call(
        body, name="fox_fwd", grid=(HEADS, nq),
        in_specs=[pl.BlockSpec((t, 2 * HEAD_DIM), lambda h, i: (i, h)),
                  pl.BlockSpec((s_len, 2 * HEAD_DIM), lambda h, i: (0, h)),
                  pl.BlockSpec((s_len, HEAD_DIM), lambda h, i: (0, h))],
        out_specs=[pl.BlockSpec((t, HEAD_DIM), lambda h, i: (i, h)),
                   pl.BlockSpec((1, t, 1), lambda h, i: (h, i, 0))],
        out_shape=[jax.ShapeDtypeStruct((s_len, WIDTH), f32), jax.ShapeDtypeStruct((HEADS, s_len, 1), f32)],
        compiler_params=_params("parallel", "arbitrary"),
    )(qs, kn, vb)


def _fox_bwd(qs, kn, vb, do, lse, delta):
    s_len = qs.shape[0]
    t = FOX_T
    nq = s_len // t
    half = t // 2

    def body(q_ref, do_ref, lse_ref, dl_ref, k_ref, v_ref, dq_ref, dk_ref, dvb_ref, df_ref, dfq_ref, dv_ref):
        head, qi = pl.program_id(0), pl.program_id(1)

        @pl.when(qi == 0)
        def _():
            dk_ref[...] = jnp.zeros_like(dk_ref)
            dv_ref[...] = jnp.zeros_like(dv_ref)
            df_ref[...] = jnp.zeros_like(df_ref)

        lse_col = lse_ref[0]
        dl = _head_lane(dl_ref[...], head)

        def update(q_rows, k_rows, df_lanes, j, carry, mask):
            dq, row_sum = carry
            q, do_b = q_ref[q_rows, :], do_ref[q_rows, :]
            p = jnp.exp2(_dg(q, k_ref[k_rows, :], 1, 1) - lse_col[q_rows])
            if mask is not None:
                p = jnp.where(mask, p, 0.0)
            ds = p * (_dg(do_b, v_ref[k_rows, :], 1, 1) - dl[q_rows])
            ds_b = ds.astype(bf16)
            dk_ref[k_rows, :] += _dg(ds_b, q_ref[q_rows, 0:HEAD_DIM], 0, 0)
            dv_ref[k_rows, :] += _dg(p.astype(bf16), do_b, 0, 0)
            df_ref[0, j, :, df_lanes] += -jnp.sum(ds, axis=0, keepdims=True)
            dq = dq + jnp.dot(ds_b, k_ref[k_rows, 0:HEAD_DIM], preferred_element_type=f32)
            return dq, row_sum + jnp.sum(ds, axis=-1, keepdims=True)

        everything, upper, lower = slice(0, t), slice(0, half), slice(half, t)
        carry = lax.fori_loop(
            0, qi, lambda j, c: update(everything, pl.ds(pl.multiple_of(j * t, t), t), everything, j, c, None),
            (jnp.zeros((t, HEAD_DIM), f32), jnp.zeros((t, 1), f32)))
        carry = update(everything, pl.ds(pl.multiple_of(qi * t, t), half), upper, qi, carry,
                       _iota((t, half), 0) >= _iota((t, half), 1))
        low = update(lower, pl.ds(pl.multiple_of(qi * t + half, half), half), lower, qi,
                     tuple(c[half:] for c in carry), _iota((half, half), 0) >= _iota((half, half), 1))
        dq, row_sum = (jnp.concatenate([c[:half], lo], axis=0) for c, lo in zip(carry, low))
        dq_ref[...] = dq
        dfq_ref[0] = row_sum

        @pl.when(qi == nq - 1)
        def _():
            dvb_ref[...] = dv_ref[...].astype(bf16)

    blk = pl.BlockSpec((t, HEAD_DIM), lambda h, i: (i, h))
    blk2 = pl.BlockSpec((t, 2 * HEAD_DIM), lambda h, i: (i, h))
    full = pl.BlockSpec((s_len, HEAD_DIM), lambda h, i: (0, h))
    full2 = pl.BlockSpec((s_len, 2 * HEAD_DIM), lambda h, i: (0, h))
    colv = pl.BlockSpec((1, t, 1), lambda h, i: (h, i, 0))
    rowv = pl.BlockSpec((1, nq, 1, t), lambda h, i: (h, 0, 0, 0))
    lanes = pl.BlockSpec((t, N_SMALL), lambda h, i: (i, 0))
    wide = jax.ShapeDtypeStruct((s_len, WIDTH), f32)
    return pl.pallas_call(
        body, name="fox_bwd", grid=(HEADS, nq),
        in_specs=[blk2, blk, colv, lanes, full2, full],
        out_specs=[blk, full, full, rowv, colv],
        out_shape=[wide, wide, jax.ShapeDtypeStruct((s_len, WIDTH), bf16), jax.ShapeDtypeStruct((HEADS, nq, 1, t), f32),
                   jax.ShapeDtypeStruct((HEADS, s_len, 1), f32)],
        scratch_shapes=[pltpu.VMEM((s_len, HEAD_DIM), f32)],
        compiler_params=_params("parallel", "arbitrary"),
    )(qs, do, lse, delta, kn, vb)


INTRA_CHUNKS = 8
SCAN_FWD_CHUNKS = 8
SCAN_BWD_CHUNKS = 4


def _gdn_intra_fwd(gq, gk, gv, small):
    s_len = gq.shape[0]
    cpb = INTRA_CHUNKS
    rows_blk = cpb * CHUNK
    n_chunks = s_len // CHUNK

    def body(q_ref, k_ref, v_ref, sm_ref, u_ref, w_ref, qg_ref, kd_ref, attn_ref, t_ref, eg_ref):
        head = pl.program_id(0)
        sm = sm_ref[...]
        gc_b, gl_b, beta_b = (_head_slab(sm, LANE_GC + head), _head_slab(sm, LANE_GLAST + head),
                              _head_slab(sm, LANE_BETA + head))
        ms, rhss = [], []
        for ci in range(cpb):
            rows = pl.ds(ci * CHUNK, CHUNK)
            sl = slice(ci * CHUNK, (ci + 1) * CHUNK)
            m, rhs, qg, kd, attn, eg_last = _gdn_intra_pre(q_ref[rows, :], k_ref[rows, :], v_ref[rows, :],
                                                           gc_b[sl], gl_b[sl], beta_b[sl], _BF_PLAIN[1])
            qg_ref[rows, :] = qg.astype(bf16)
            kd_ref[rows, :] = kd.astype(bf16)
            attn_ref[0, ci] = attn.astype(bf16)
            eg_ref[0, ci] = eg_last
            ms.append(m)
            rhss.append(rhs)
        for ci, (t, rhs) in enumerate(zip(_inv_unit_lower_many(ms), rhss)):
            rows = pl.ds(ci * CHUNK, CHUNK)
            t_ref[0, ci] = t
            uw = _X3_PLAIN[0](t, rhs)
            u_ref[rows, :] = uw[:, :HEAD_DIM]
            w_ref[rows, :] = uw[:, HEAD_DIM:].astype(bf16)

    blk = pl.BlockSpec((rows_blk, HEAD_DIM), lambda h, i: (i, h))
    sq = pl.BlockSpec((1, cpb, CHUNK, CHUNK), lambda h, i: (h, i, 0, 0))
    wide_bf = jax.ShapeDtypeStruct((s_len, WIDTH), bf16)
    return pl.pallas_call(
        body, name="gdn_intra_fwd", grid=(HEADS, s_len // rows_blk),
        in_specs=[blk] * 3 + [pl.BlockSpec((rows_blk, N_SMALL), lambda h, i: (i, 0))],
        out_specs=[blk] * 4 + [sq, sq, pl.BlockSpec((1, cpb, SUBLANES, HEAD_DIM), lambda h, i: (h, i, 0, 0))],
        out_shape=[jax.ShapeDtypeStruct((s_len, WIDTH), f32), wide_bf, wide_bf, wide_bf,
                   jax.ShapeDtypeStruct((HEADS, n_chunks, CHUNK, CHUNK), bf16),
                   jax.ShapeDtypeStruct((HEADS, n_chunks, CHUNK, CHUNK), f32),
                   jax.ShapeDtypeStruct((HEADS, n_chunks, SUBLANES, HEAD_DIM), f32)],
        compiler_params=_params("parallel", "parallel"),
    )(gq, gk, gv, small)


def _gdn_scan_fwd(u, w, qg, kd, attn, eg):
    s_len = u.shape[0]
    cpb = SCAN_FWD_CHUNKS
    rows_blk = cpb * CHUNK
    n_chunks = s_len // CHUNK

    def body(u_ref, w_ref, qg_ref, kd_ref, attn_ref, eg_ref, o_ref, st_ref, s_sc):
        @pl.when(pl.program_id(0) == 0)
        def _():
            s_sc[...] = jnp.zeros_like(s_sc)

        def chunk(ci, _):
            rows = pl.ds(pl.multiple_of(ci * CHUNK, CHUNK), CHUNK)
            cols = [slice(h * HEAD_DIM, (h + 1) * HEAD_DIM) for h in range(HEADS)]
            s0 = [s_sc[h] for h in range(HEADS)]
            s0_b = [s.astype(bf16) for s in s0]
            for h in range(HEADS):
                st_ref[h, ci] = s0[h]
            ws = [jnp.dot(w_ref[rows, cols[h]], s0_b[h], preferred_element_type=f32) for h in range(HEADS)]
            qs = [jnp.dot(qg_ref[rows, cols[h]], s0_b[h], preferred_element_type=f32) for h in range(HEADS)]
            vn_b = [(u_ref[rows, cols[h]] - ws[h]).astype(bf16) for h in range(HEADS)]
            av = [jnp.dot(attn_ref[h, ci], vn_b[h], preferred_element_type=f32) for h in range(HEADS)]
            kv = [_dg(kd_ref[rows, cols[h]], vn_b[h], 0, 0) for h in range(HEADS)]
            for h in range(HEADS):
                o_ref[rows, cols[h]] = qs[h] + av[h]
                s_sc[h] = _scale_rows(s0[h], eg_ref[h, ci]) + kv[h]
            return 0

        lax.fori_loop(0, cpb, chunk, 0)

    row = pl.BlockSpec((rows_blk, WIDTH), lambda i: (i, 0))
    return pl.pallas_call(
        body, name="gdn_scan_fwd", grid=(s_len // rows_blk,),
        in_specs=[row] * 4 + [pl.BlockSpec((HEADS, cpb, CHUNK, CHUNK), lambda i: (0, i, 0, 0)),
                              pl.BlockSpec((HEADS, cpb, SUBLANES, HEAD_DIM), lambda i: (0, i, 0, 0))],
        out_specs=[row, pl.BlockSpec((HEADS, cpb, HEAD_DIM, HEAD_DIM), lambda i: (0, i, 0, 0))],
        out_shape=[jax.ShapeDtypeStruct((s_len, WIDTH), f32),
                   jax.ShapeDtypeStruct((HEADS, n_chunks, HEAD_DIM, HEAD_DIM), f32)],
        scratch_shapes=[pltpu.VMEM((HEADS, HEAD_DIM, HEAD_DIM), f32)],
        compiler_params=_params("arbitrary"),
    )(u, w, qg, kd, attn, eg)


def _gdn_scan_bwd(u, w, qg, kd, attn, eg, states, d_o):
    s_len = u.shape[0]
    cpb = SCAN_BWD_CHUNKS
    rows_blk = cpb * CHUNK
    n_chunks = s_len // CHUNK
    nb = s_len // rows_blk

    def body(u_ref, w_ref, qg_ref, kd_ref, attn_ref, eg_ref, st_ref, do_ref,
             du_ref, dw_ref, dqg_ref, dkd_ref, dattn_ref, deg_ref, ds_sc):
        @pl.when(pl.program_id(0) == 0)
        def _():
            ds_sc[...] = jnp.zeros_like(ds_sc)

        def chunk(step, _):
            ci = cpb - 1 - step
            rows = pl.ds(pl.multiple_of(ci * CHUNK, CHUNK), CHUNK)
            hs = range(HEADS)
            cols = [slice(h * HEAD_DIM, (h + 1) * HEAD_DIM) for h in hs]
            s0 = [st_ref[h, ci] for h in hs]
            s0_b = [s.astype(bf16) for s in s0]
            ds1 = [ds_sc[h] for h in hs]
            ds1_b = [d.astype(bf16) for d in ds1]
            do_b = [do_ref[rows, cols[h]].astype(bf16) for h in hs]
            ws = [jnp.dot(w_ref[rows, cols[h]], s0_b[h], preferred_element_type=f32) for h in hs]
            ad = [_dg(attn_ref[h, ci], do_b[h], 0, 0) for h in hs]
            kd_ds = [jnp.dot(kd_ref[rows, cols[h]], ds1_b[h], preferred_element_type=f32) for h in hs]
            dqg = [_dg(do_b[h], s0_b[h], 1, 1) for h in hs]
            qd = [_dg(qg_ref[rows, cols[h]], do_b[h], 0, 0) for h in hs]
            vn_b = [(u_ref[rows, cols[h]] - ws[h]).astype(bf16) for h in hs]
            dvn = [ad[h] + kd_ds[h] for h in hs]
            dvn_b = [d.astype(bf16) for d in dvn]
            dattn = [_dg(do_b[h], vn_b[h], 1, 1) for h in hs]
            dkd = [_dg(vn_b[h], ds1_b[h], 1, 1) for h in hs]
            dw = [_dg(dvn_b[h], s0_b[h], 1, 1) for h in hs]
            wd = [_dg(w_ref[rows, cols[h]], dvn_b[h], 0, 0) for h in hs]
            for h in hs:
                dattn_ref[h, ci] = dattn[h]
                dqg_ref[rows, cols[h]] = dqg[h]
                dkd_ref[rows, cols[h]] = dkd[h]
                du_ref[rows, cols[h]] = dvn[h]
                dw_ref[rows, cols[h]] = -dw[h]
                ds_sc[h] = qd[h] - wd[h] + _scale_rows(ds1[h], eg_ref[h, ci])
                deg_ref[h, ci] = jnp.sum((ds1[h] * s0[h]).reshape(HEAD_DIM // SUBLANES, SUBLANES, HEAD_DIM), axis=0)
            return 0

        lax.fori_loop(0, cpb, chunk, 0)

    row = pl.BlockSpec((rows_blk, WIDTH), lambda i: (nb - 1 - i, 0))
    sq = pl.BlockSpec((HEADS, cpb, CHUNK, CHUNK), lambda i: (0, nb - 1 - i, 0, 0))
    egs = pl.BlockSpec((HEADS, cpb, SUBLANES, HEAD_DIM), lambda i: (0, nb - 1 - i, 0, 0))
    wide = jax.ShapeDtypeStruct((s_len, WIDTH), f32)
    return pl.pallas_call(
        body, name="gdn_scan_bwd", grid=(nb,),
        in_specs=[row] * 4 + [sq, egs, pl.BlockSpec((HEADS, cpb, HEAD_DIM, HEAD_DIM), lambda i: (0, nb - 1 - i, 0, 0)), row],
        out_specs=[row] * 4 + [sq, egs],
        out_shape=[wide] * 4 + [jax.ShapeDtypeStruct((HEADS, n_chunks, CHUNK, CHUNK), f32),
                                jax.ShapeDtypeStruct((HEADS, n_chunks, SUBLANES, HEAD_DIM), f32)],
        scratch_shapes=[pltpu.VMEM((HEADS, HEAD_DIM, HEAD_DIM), f32)],
        compiler_params=_params("arbitrary"),
    )(u, w, qg, kd, attn, eg, states, d_o)


def _gdn_intra_bwd(gq, gk, gv, small, t_inv, du, dw, dqg, dkd, dattn, deg):
    s_len = gq.shape[0]
    cpb = INTRA_CHUNKS
    rows_blk = cpb * CHUNK

    def body(q_ref, k_ref, v_ref, sm_ref, t_ref, du_ref, dw_ref, dqg_ref, dkd_ref, dattn_ref, deg_ref,
             dq_ref, dk_ref, dv_ref, dsm_ref):
        head = pl.program_id(1)

        def batch(value):
            return value.reshape(cpb, CHUNK, HEAD_DIM)

        sm = sm_ref[...]
        slabs = [batch(_head_slab(sm, first + head)) for first in (LANE_GC, LANE_GLAST, LANE_BETA)]
        t_known = t_ref[0]
        _, vjp = jax.vjp(lambda q, k, v, gc, gl, b: _gdn_intra(q, k, v, gc, gl, b, t_known),
                         batch(q_ref[...]), batch(k_ref[...]), batch(v_ref[...]), *slabs)
        duw = jnp.concatenate([batch(du_ref[...]), batch(dw_ref[...])], axis=-1)
        dq, dk, dv, dgc, dgl, db = vjp((duw, batch(dqg_ref[...]), batch(dkd_ref[...]), dattn_ref[0], deg_ref[0]))
        for ref, grad in zip((dq_ref, dk_ref, dv_ref), (dq, dk, dv)):
            ref[...] = grad.reshape(rows_blk, HEAD_DIM)

        @pl.when(head == 0)
        def _():
            dsm_ref[...] = jnp.zeros_like(dsm_ref)

        lane = _iota((rows_blk, N_SMALL), 1)
        acc = dsm_ref[...]
        for first, grad in ((LANE_GC, dgc), (LANE_GLAST, dgl), (LANE_BETA, db)):
            col = jnp.sum(grad.reshape(rows_blk, HEAD_DIM), axis=1, keepdims=True)
            acc = acc + jnp.where(lane == first + head, col, 0.0)
        dsm_ref[...] = acc

    blk = pl.BlockSpec((rows_blk, HEAD_DIM), lambda i, h: (i, h))
    sq = pl.BlockSpec((1, cpb, CHUNK, CHUNK), lambda i, h: (h, i, 0, 0))
    egs = pl.BlockSpec((1, cpb, SUBLANES, HEAD_DIM), lambda i, h: (h, i, 0, 0))
    lanes = pl.BlockSpec((rows_blk, N_SMALL), lambda i, h: (i, 0))
    wide = jax.ShapeDtypeStruct((s_len, WIDTH), f32)
    return pl.pallas_call(
        body, name="gdn_intra_bwd", grid=(s_len // rows_blk, HEADS),
        in_specs=[blk] * 3 + [lanes, sq] + [blk] * 4 + [sq, egs],
        out_specs=[blk] * 3 + [lanes],
        out_shape=[wide] * 3 + [jax.ShapeDtypeStruct((s_len, N_SMALL), f32)],
        compiler_params=_params("parallel", "arbitrary"),
    )(gq, gk, gv, small, t_inv, du, dw, dqg, dkd, dattn, deg)


MIX_TM = 256


def _mix_fwd(fox_o, gdn_o, p_main, gnorm_g):
    s_len = fox_o.shape[0]
    tm = MIX_TM

    def body(fo_ref, go_ref, fz_ref, gz_ref, g_ref, mixed_ref):
        fz = fz_ref[...]
        mixed_ref[:, 0:WIDTH] = (fo_ref[...] * (fz * _sigmoid(fz))).astype(bf16)
        gz = gz_ref[...]
        gate = gz * _sigmoid(gz)
        gg = g_ref[...]
        for h in range(HEADS):
            sl = slice(h * HEAD_DIM, (h + 1) * HEAD_DIM)
            o = go_ref[:, sl]
            r = lax.rsqrt(jnp.mean(o * o, axis=-1, keepdims=True) + EPS)
            mixed_ref[:, WIDTH + h * HEAD_DIM:WIDTH + (h + 1) * HEAD_DIM] = (o * r * gg * gate[:, sl]).astype(bf16)

    row = pl.BlockSpec((tm, WIDTH), lambda i: (i, 0))
    return pl.pallas_call(
        body, name="mix_fwd", grid=(s_len // tm,),
        in_specs=[row, row, pl.BlockSpec((tm, WIDTH), lambda i: (i, 3)), pl.BlockSpec((tm, WIDTH), lambda i: (i, 7)),
                  pl.BlockSpec((1, LANES), lambda i: (0, 0))],
        out_specs=pl.BlockSpec((tm, 2 * WIDTH), lambda i: (i, 0)),
        out_shape=jax.ShapeDtypeStruct((s_len, 2 * WIDTH), bf16),
        compiler_params=_params("parallel"),
    )(fox_o, gdn_o, p_main, p_main, gnorm_g)


def _silu_grad(z):
    sg = _sigmoid(z)
    return sg * (1.0 + z * (1.0 - sg))


def _mix_bwd(dmixed, fox_o, gdn_o, p_main, gnorm_g):
    s_len = fox_o.shape[0]
    tm = MIX_TM

    def body(dm_ref, fo_ref, go_ref, fz_ref, gz_ref, g_ref, dof_ref, delta_ref, dfz_ref, dgz_ref, dgo_ref, dg_ref):
        @pl.when(pl.program_id(0) == 0)
        def _():
            dg_ref[...] = jnp.zeros_like(dg_ref)

        lane = _iota((tm, LANES), 1)
        fz = fz_ref[...]
        dmf = dm_ref[:, 0:WIDTH]
        fo = fo_ref[...]
        dof = dmf * (fz * _sigmoid(fz))
        dof_ref[...] = dof.astype(bf16)
        dfz_ref[...] = (dmf * fo * _silu_grad(fz)).astype(bf16)
        prod = dof * fo
        delta = jnp.zeros((tm, LANES), f32)
        for h in range(HEADS):
            dh = jnp.sum(prod[:, h * HEAD_DIM:(h + 1) * HEAD_DIM], axis=-1, keepdims=True)
            delta = jnp.where(lane == h, dh, delta)
        delta_ref[...] = delta

        gz = gz_ref[...]
        dmg = dm_ref[:, WIDTH:2 * WIDTH]
        gate = gz * _sigmoid(gz)
        sgrad = _silu_grad(gz)
        gg = g_ref[...]
        dg_acc = jnp.zeros((1, HEAD_DIM), f32)
        for h in range(HEADS):
            sl = slice(h * HEAD_DIM, (h + 1) * HEAD_DIM)
            o = go_ref[:, sl]
            r = lax.rsqrt(jnp.mean(o * o, axis=-1, keepdims=True) + EPS)
            on = o * r
            dmh = dmg[:, sl]
            dgz_ref[:, sl] = (dmh * (on * gg) * sgrad[:, sl]).astype(bf16)
            dy = dmh * gate[:, sl]
            dg_acc = dg_acc + jnp.sum(dy * on, axis=0, keepdims=True)
            tt = dy * gg
            dgo_ref[:, sl] = r * (tt - on * jnp.mean(tt * on, axis=-1, keepdims=True))
        dg_ref[...] += dg_acc

    row = pl.BlockSpec((tm, WIDTH), lambda i: (i, 0))
    wide_bf = jax.ShapeDtypeStruct((s_len, WIDTH), bf16)
    return pl.pallas_call(
        body, name="mix_bwd", grid=(s_len // tm,),
        in_specs=[pl.BlockSpec((tm, 2 * WIDTH), lambda i: (i, 0)), row, row,
                  pl.BlockSpec((tm, WIDTH), lambda i: (i, 3)), pl.BlockSpec((tm, WIDTH), lambda i: (i, 7)),
                  pl.BlockSpec((1, LANES), lambda i: (0, 0))],
        out_specs=[row, pl.BlockSpec((tm, LANES), lambda i: (i, 0)), row, row, row,
                   pl.BlockSpec((1, LANES), lambda i: (0, 0))],
        out_shape=[wide_bf, jax.ShapeDtypeStruct((s_len, LANES), f32), wide_bf, wide_bf,
                   jax.ShapeDtypeStruct((s_len, WIDTH), f32), jax.ShapeDtypeStruct((1, LANES), f32)],
        compiler_params=_params("arbitrary"),
    )(dmixed, fox_o, gdn_o, p_main, p_main, gnorm_g)


def _out_head(mixed, w_out, x, target, gate, final_g):
    s_len = x.shape[0]
    tm = 256

    def body(mx_ref, w_ref, x_ref, t_ref, gate_ref, fg_ref, loss_ref, dy_ref, dz_ref, dm_ref, dfg_ref, dgate_ref):
        @pl.when(pl.program_id(0) == 0)
        def _():
            loss_ref[...] = jnp.zeros_like(loss_ref)
            dfg_ref[...] = jnp.zeros_like(dfg_ref)
            dgate_ref[...] = jnp.zeros_like(dgate_ref)

        w = w_ref[...]
        z = jnp.dot(mx_ref[...], w, preferred_element_type=f32)
        gate_v, fg = gate_ref[...], fg_ref[...]
        y1 = x_ref[...] + gate_v * z
        r = lax.rsqrt(jnp.mean(y1 * y1, axis=-1, keepdims=True) + EPS)
        yn = y1 * r
        err = yn * fg - t_ref[...]
        loss_ref[...] += 0.5 * jnp.sum(jnp.mean(err * err, axis=-1, keepdims=True))
        dout = err * (1.0 / D_MODEL)
        dfg_ref[...] += jnp.sum(dout * yn, axis=0, keepdims=True)
        tt = dout * fg
        dy1 = r * (tt - yn * jnp.mean(tt * yn, axis=-1, keepdims=True))
        dy_ref[...] = dy1
        dgate_ref[...] += jnp.sum(dy1 * z, axis=0, keepdims=True)
        dz = (dy1 * gate_v).astype(bf16)
        dz_ref[...] = dz
        dm_ref[...] = _dg(dz, w, 1, 1)

    row = pl.BlockSpec((tm, D_MODEL), lambda i: (i, 0))
    vec = pl.BlockSpec((1, D_MODEL), lambda i: (0, 0))
    big = jax.ShapeDtypeStruct((s_len, D_MODEL), f32)
    return pl.pallas_call(
        body, name="out_head", grid=(s_len // tm,),
        in_specs=[row, pl.BlockSpec((D_MODEL, D_MODEL), lambda i: (0, 0)), row, row, vec, vec],
        out_specs=[pl.BlockSpec((1, LANES), lambda i: (0, 0)), row, row, row, vec, vec],
        out_shape=[jax.ShapeDtypeStruct((1, LANES), f32), big, jax.ShapeDtypeStruct((s_len, D_MODEL), bf16), big,
                   jax.ShapeDtypeStruct((1, D_MODEL), f32), jax.ShapeDtypeStruct((1, D_MODEL), f32)],
        compiler_params=_params("arbitrary"),
    )(mixed, w_out, x, target, gate, final_g)


def _matmul_tn(name, a, b, out_dtype):
    k_len, m_len = a.shape
    n_len = b.shape[1]
    tk, tm, tn = min(2048, k_len), min(1024, m_len), min(1024, n_len)
    nk = k_len // tk

    def body(a_ref, b_ref, o_ref, acc_sc):
        k = pl.program_id(2)

        @pl.when(k == 0)
        def _():
            acc_sc[...] = jnp.zeros_like(acc_sc)

        acc_sc[...] += _dg(a_ref[...], b_ref[...], 0, 0)

        @pl.when(k == nk - 1)
        def _():
            o_ref[...] = acc_sc[...].astype(out_dtype)

    return pl.pallas_call(
        body, name=name, grid=(m_len // tm, n_len // tn, nk),
        in_specs=[pl.BlockSpec((tk, tm), lambda i, j, k: (k, i)), pl.BlockSpec((tk, tn), lambda i, j, k: (k, j))],
        out_specs=pl.BlockSpec((tm, tn), lambda i, j, k: (i, j)),
        out_shape=jax.ShapeDtypeStruct((m_len, n_len), out_dtype),
        scratch_shapes=[pltpu.VMEM((tm, tn), f32)],
        compiler_params=_params("parallel", "parallel", "arbitrary"),
    )(a, b)


def _post1(p_main, p_small, qn_g, kn_g, conv_w, bvec, alog, dqs, dkn, dgq, dgk, dgv, d_small, df, df_query):
    s_len = p_main.shape[0]
    tm = PREP_TM
    nb = s_len // tm

    def body(fq_ref, fk_ref, gq_ref, gk_ref, gv_ref, hq_ref, hk_ref, hv_ref, ps_ref, qg_ref, kg_ref, cw_ref, bv_ref,
             al_ref, dqs_ref, dkn_ref, dgq_ref, dgk_ref, dgv_ref, dsm_ref, df_ref, dfq_in_ref,
             dfq_ref, dfk_ref, dx_ref, dps_ref, dqg_ref, dkg_ref, sums_ref, dw_ref, xe_sc, carry_sc, dc_sc, next_sc):
        step = pl.program_id(0)
        blk = nb - 1 - step

        @pl.when(step == 0)
        def _():
            carry_sc[...] = jnp.zeros_like(carry_sc)
            next_sc[...] = jnp.zeros_like(next_sc)
            dqg_ref[...] = jnp.zeros_like(dqg_ref)
            dkg_ref[...] = jnp.zeros_like(dkg_ref)
            sums_ref[...] = jnp.zeros_like(sums_ref)
            dw_ref[...] = jnp.zeros_like(dw_ref)

        for x_ref, g_ref, dy_ref, o_ref, acc_ref, mul in ((fq_ref, qg_ref, dqs_ref, dfq_ref, dqg_ref, QK_SCALE),
                                                          (fk_ref, kg_ref, dkn_ref, dfk_ref, dkg_ref, LN2)):
            gain = g_ref[...]
            acc = jnp.zeros((1, HEAD_DIM), f32)
            for h in range(HEADS):
                sl = slice(h * HEAD_DIM, (h + 1) * HEAD_DIM)
                xv = x_ref[:, sl]
                r = lax.rsqrt(jnp.mean(xv * xv, axis=-1, keepdims=True) + EPS)
                xn = xv * r
                dy = dy_ref[:, sl] * mul
                acc = acc + jnp.sum(dy * xn, axis=0, keepdims=True)
                tt = dy * gain
                o_ref[:, sl] = (r * (tt - xn * jnp.mean(tt * xn, axis=-1, keepdims=True))).astype(bf16)
            acc_ref[...] += acc

        first = blk == 0
        for sec, (x_ref, halo_ref, dy_ref) in enumerate(((gq_ref, hq_ref, dgq_ref), (gk_ref, hk_ref, dgk_ref),
                                                         (gv_ref, hv_ref, dgv_ref))):
            cols = slice(sec * WIDTH, (sec + 1) * WIDTH)
            xe_sc[0:HALO, :] = jnp.where(first, 0.0, halo_ref[...])
            xe_sc[HALO:, :] = x_ref[...]
            cv = _conv_section(xe_sc, cw_ref, cols, tm)
            sgrad = _silu_grad(cv)
            if sec == 2:
                dc_sc[0:tm, :] = dy_ref[...] * sgrad
            else:
                y = cv * _sigmoid(cv)
                mul = QK_SCALE if sec == 0 else 1.0
                for h in range(HEADS):
                    sl = slice(h * HEAD_DIM, (h + 1) * HEAD_DIM)
                    yh = y[:, sl]
                    r = lax.rsqrt(jnp.sum(yh * yh, axis=-1, keepdims=True) + EPS)
                    dqh = dy_ref[:, sl]
                    dyh = (mul * r) * (dqh - yh * (r * r) * jnp.sum(dqh * yh, axis=-1, keepdims=True))
                    dc_sc[0:tm, sl] = dyh * sgrad[:, sl]
            dc_sc[tm:, :] = next_sc[sec]
            dc = dc_sc[0:tm, :]
            dx = cw_ref[pl.ds(CONV_K - 1, 1), cols] * dc
            for tap in range(CONV_K - 1):
                dx = dx + cw_ref[pl.ds(tap, 1), cols] * dc_sc[pl.ds(CONV_K - 1 - tap, tm), :]
            dx_ref[:, cols] = dx.astype(bf16)
            dw = jnp.zeros((8, WIDTH), f32)
            tap_row = _iota((8, WIDTH), 0)
            for tap in range(CONV_K):
                contrib = jnp.sum(dc * xe_sc[pl.ds(HALO - (CONV_K - 1) + tap, tm), :], axis=0, keepdims=True)
                dw = jnp.where(tap_row == tap, contrib, dw)
            dw_ref[:, cols] += dw
            next_sc[sec] = dc_sc[0:HALO, :]

        lane = _iota((tm, N_SMALL), 1)
        z, _, gval, beta = _small_fwd(ps_ref[...], bv_ref[...], al_ref[...])
        sig_z = _sigmoid(z)
        dsm = dsm_ref[...]
        in_g = (lane >= LANE_G) & (lane < LANE_G + HEADS)
        dgc = jnp.where(in_g, pltpu.roll(dsm, N_SMALL - (LANE_GC - LANE_G), 1), 0.0)
        dgl = jnp.where(in_g, pltpu.roll(dsm, N_SMALL - (LANE_GLAST - LANE_G), 1), 0.0)
        tri_c, ones_c = _chunk_masks(tm)
        dg = (_dg(tri_c, dgc, 0, 0, HI) + jnp.dot(ones_c, dgl, preferred_element_type=f32, precision=HI))
        dbeta = dsm
        dfb = jnp.where(lane < HEADS, df_ref[...], 0.0)
        for h in range(HEADS):
            dfb = dfb + jnp.where(lane == h, dfq_in_ref[h], 0.0)
        tri_u = (_iota((tm, tm), 1) >= _iota((tm, tm), 0)).astype(f32)
        dlogf = jnp.dot(tri_u, dfb, preferred_element_type=f32, precision=HI) + carry_sc[...]
        carry_sc[...] += jnp.sum(dfb, axis=0, keepdims=True)
        dff = dlogf * (1.0 - sig_z)
        dga = dg * (-jnp.exp(al_ref[...])) * sig_z
        dgb_small = dbeta * beta * (1.0 - beta)
        dps = jnp.where(lane < HEADS, dff, jnp.where(lane < 2 * HEADS, dga, jnp.where(lane < 3 * HEADS, dgb_small, 0.0)))
        dps_ref[...] = dps.astype(bf16)
        row = _iota((8, N_SMALL), 0)
        s0 = jnp.sum(dps, axis=0, keepdims=True)
        s1 = jnp.sum(jnp.where((lane >= HEADS) & (lane < 2 * HEADS), dg * gval, 0.0), axis=0, keepdims=True)
        sums_ref[...] += jnp.where(row == 0, s0, jnp.where(row == 1, s1, 0.0))

    def col(cb):
        return pl.BlockSpec((tm, WIDTH), lambda i: (nb - 1 - i, cb))

    def halo(cb):
        return pl.BlockSpec((HALO, WIDTH), lambda i: (jnp.maximum((nb - 1 - i) * (tm // HALO) - 1, 0), cb))

    vec = pl.BlockSpec((1, LANES), lambda i: (0, 0))
    row0 = pl.BlockSpec((tm, WIDTH), lambda i: (nb - 1 - i, 0))
    small = pl.BlockSpec((tm, N_SMALL), lambda i: (nb - 1 - i, 0))
    wide_bf = jax.ShapeDtypeStruct((s_len, WIDTH), bf16)
    return pl.pallas_call(
        body, name="post1", grid=(nb,),
        in_specs=[col(0), col(1), col(4), col(5), col(6), halo(4), halo(5), halo(6), small, vec, vec,
                  pl.BlockSpec((CONV_K, 3 * WIDTH), lambda i: (0, 0)), vec, vec,
                  row0, row0, row0, row0, row0, small, small,
                  pl.BlockSpec((HEADS, tm, 1), lambda i: (0, nb - 1 - i, 0))],
        out_specs=[row0, row0, pl.BlockSpec((tm, 3 * WIDTH), lambda i: (nb - 1 - i, 0)), small, vec, vec,
                   pl.BlockSpec((8, N_SMALL), lambda i: (0, 0)), pl.BlockSpec((8, 3 * WIDTH), lambda i: (0, 0))],
        out_shape=[wide_bf, wide_bf, jax.ShapeDtypeStruct((s_len, 3 * WIDTH), bf16),
                   jax.ShapeDtypeStruct((s_len, N_SMALL), bf16), jax.ShapeDtypeStruct((1, LANES), f32),
                   jax.ShapeDtypeStruct((1, LANES), f32), jax.ShapeDtypeStruct((8, N_SMALL), f32),
                   jax.ShapeDtypeStruct((8, 3 * WIDTH), f32)],
        scratch_shapes=[pltpu.VMEM((tm + HALO, WIDTH), f32), pltpu.VMEM((1, N_SMALL), f32),
                        pltpu.VMEM((tm + HALO, WIDTH), f32), pltpu.VMEM((3, HALO, WIDTH), f32)],
        compiler_params=_params("arbitrary"),
    )(p_main, p_main, p_main, p_main, p_main, p_main, p_main, p_main, p_small, qn_g, kn_g, conv_w, bvec, alog,
      dqs, dkn, dgq, dgk, dgv, d_small, df, df_query)


def _in_proj_bwd(dp_pieces, dp_small, wt_main, wt_small):
    s_len = dp_small.shape[0]
    tm, tk = min(1024, s_len), WIDTH
    nk = N_MAIN // tk
    first_section = [sum(p.shape[1] // tk for p in dp_pieces[:n]) for n in range(len(dp_pieces))]
    n_pieces = len(dp_pieces)

    def body(*refs):
        piece_refs = refs[:n_pieces]
        dps_ref, w_ref, ws_ref, dh_ref = refs[n_pieces:]
        k = pl.program_id(1)

        @pl.when(k == 0)
        def _():
            dh_ref[...] = jnp.dot(dps_ref[...], ws_ref[...], preferred_element_type=f32)

        for piece, ref, first in zip(dp_pieces, piece_refs, first_section):
            @pl.when((k >= first) & (k < first + piece.shape[1] // tk))
            def _(ref=ref):
                dh_ref[...] += jnp.dot(ref[...], w_ref[...], preferred_element_type=f32)

    def piece_spec(piece, first):
        last = piece.shape[1] // tk - 1
        return pl.BlockSpec((tm, tk), lambda i, k: (i, jnp.clip(k - first, 0, last)))

    return pl.pallas_call(
        body, name="in_proj_bwd", grid=(s_len // tm, nk),
        in_specs=[piece_spec(p, f) for p, f in zip(dp_pieces, first_section)]
                 + [pl.BlockSpec((tm, N_SMALL), lambda i, k: (i, 0)),
                    pl.BlockSpec((tk, D_MODEL), lambda i, k: (k, 0)), pl.BlockSpec((N_SMALL, D_MODEL), lambda i, k: (0, 0))],
        out_specs=pl.BlockSpec((tm, D_MODEL), lambda i, k: (i, 0)),
        out_shape=jax.ShapeDtypeStruct((s_len, D_MODEL), f32),
        compiler_params=_params("parallel", "arbitrary"),
    )(*dp_pieces, dp_small, wt_main, wt_small)


def _adaln_bwd(dh, x, dy1, norm_g, scale1p):
    s_len = x.shape[0]
    tm = 256

    def body(dh_ref, x_ref, dy_ref, g_ref, sc_ref, dx_ref, dsh_ref, dsc_ref, dg_ref):
        @pl.when(pl.program_id(0) == 0)
        def _():
            dsh_ref[...] = jnp.zeros_like(dsh_ref)
            dsc_ref[...] = jnp.zeros_like(dsc_ref)
            dg_ref[...] = jnp.zeros_like(dg_ref)

        dh = dh_ref[...]
        xb = x_ref[...]
        r = lax.rsqrt(jnp.mean(xb * xb, axis=-1, keepdims=True) + EPS)
        xr = xb * r
        gain = g_ref[...]
        dsh_ref[...] += jnp.sum(dh, axis=0, keepdims=True)
        dsc_ref[...] += jnp.sum(dh * (xr * gain), axis=0, keepdims=True)
        dxn = dh * sc_ref[...]
        dg_ref[...] += jnp.sum(dxn * xr, axis=0, keepdims=True)
        tt = dxn * gain
        dx_ref[...] = r * (tt - xr * jnp.mean(tt * xr, axis=-1, keepdims=True)) + dy_ref[...]

    row = pl.BlockSpec((tm, D_MODEL), lambda i: (i, 0))
    vec = pl.BlockSpec((1, D_MODEL), lambda i: (0, 0))
    vshape = jax.ShapeDtypeStruct((1, D_MODEL), f32)
    return pl.pallas_call(
        body, name="adaln_bwd", grid=(s_len // tm,),
        in_specs=[row, row, row, vec, vec], out_specs=[row, vec, vec, vec],
        out_shape=[jax.ShapeDtypeStruct((s_len, D_MODEL), f32), vshape, vshape, vshape],
        compiler_params=_params("arbitrary"),
    )(dh, x, dy1, norm_g, scale1p)


def _adamw(name, w, g_stack, m, v, tr, tc=None):
    n_stack, rows, cols = g_stack.shape
    tc = cols if tc is None else tc

    def body(w_ref, g_ref, m_ref, v_ref, go_ref, d_ref, mo_ref, vo_ref):
        g = g_ref[0].astype(f32)
        for k in range(1, n_stack):
            g = g + g_ref[k].astype(f32)
        go_ref[0] = g
        m_new = ADAM_B1 * m_ref[0] + (1.0 - ADAM_B1) * g
        v_new = ADAM_B2 * v_ref[0] + (1.0 - ADAM_B2) * (g * g)
        mo_ref[0] = m_new
        vo_ref[0] = v_new
        m_hat = m_new / (1.0 - ADAM_B1 ** ADAM_STEP)
        v_hat = v_new / (1.0 - ADAM_B2 ** ADAM_STEP)
        d_ref[0] = -ADAM_LR * (m_hat / (jnp.sqrt(v_hat) + ADAM_EPS) + ADAM_WD * w_ref[0])

    blk = pl.BlockSpec((1, tr, tc), lambda i, j: (0, i, j))
    shape = jax.ShapeDtypeStruct((1, rows, cols), f32)
    return pl.pallas_call(
        body, name=name, grid=(rows // tr, cols // tc),
        in_specs=[blk, pl.BlockSpec((n_stack, tr, tc), lambda i, j: (0, i, j)), blk, blk],
        out_specs=[blk] * 4, out_shape=[shape] * 4,
        compiler_params=_params("parallel", "parallel"),
    )(w, g_stack, m, v)


def _w_ada_grad(c_all_t, dmod_pad):
    def body(c_ref, d_ref, o_ref):
        cv = c_ref[...]
        o_ref[...] = jnp.dot(cv * _sigmoid(cv), d_ref[...], preferred_element_type=f32, precision=HI)

    return pl.pallas_call(body, name="w_ada_grad",
                          out_shape=jax.ShapeDtypeStruct((c_all_t.shape[0], dmod_pad.shape[1]), f32),
                          compiler_params=_params())(c_all_t, dmod_pad)


SMALL_NAMES = ("norm_g", "b_ada", "b_fgate", "fox_qn_g", "fox_kn_g", "gdn_A_log", "gdn_dt_bias", "gdn_norm_g", "final_g")
SMALL_SIZES = (D_MODEL, 3 * D_MODEL, HEADS, HEAD_DIM, HEAD_DIM, HEADS, HEADS, HEAD_DIM, D_MODEL)
SMALL_PACK = 10752


def _pack(vectors, total):
    flat = jnp.concatenate([t.reshape(-1) for t in vectors])
    return jnp.pad(flat, (0, total - flat.shape[0])).reshape(1, total)


def _lanes(*pieces):
    parts, at = [], 0
    for off, vec in pieces:
        flat = vec.reshape(-1).astype(f32)
        parts += [jnp.zeros((off - at,), f32), flat]
        at = off + flat.shape[0]
    parts.append(jnp.zeros((LANES - at,), f32))
    return jnp.concatenate(parts).reshape(1, LANES)


def kernel(x, c, norm_g, w_ada, b_ada, w_in, b_fgate, fox_qn_g, fox_kn_g, gdn_conv_w, gdn_A_log, gdn_dt_bias, gdn_norm_g, w_out, final_g, loss_target, m_norm_g, m_w_ada, m_b_ada, m_w_in, m_b_fgate, m_fox_qn_g, m_fox_kn_g, m_gdn_conv_w, m_gdn_A_log, m_gdn_dt_bias, m_gdn_norm_g, m_w_out, m_final_g, v_norm_g, v_w_ada, v_b_ada, v_w_in, v_b_fgate, v_fox_qn_g, v_fox_kn_g, v_gdn_conv_w, v_gdn_A_log, v_gdn_dt_bias, v_gdn_norm_g, v_w_out, v_final_g):
    me = _my_index()
    s_len = x.shape[1]
    nq = s_len // FOX_T
    x2 = x.reshape(s_len, D_MODEL)
    tgt = loss_target.reshape(s_len, D_MODEL)
    ada_cols = w_ada.shape[2]
    in_cols = w_in.shape[2]
    conv_cols = gdn_conv_w.shape[2]

    (c_all,) = _gather_direct("gather_c", [c])
    c_all = c_all.reshape(N_DEV, D_MODEL)
    b_shard = lax.dynamic_slice(b_ada, (0, me * ada_cols), (1, ada_cols))
    mod_mine = _mod_shard(c_all, w_ada[0], b_shard)
    wt_shard = jnp.transpose(w_in[0])
    mod_all, wt_all, w_out_all, conv_all = _gather_two_level(
        "gather_weights", [mod_mine, wt_shard.astype(bf16), w_out[0].astype(bf16), gdn_conv_w[0]])
    mod = lax.dynamic_slice(mod_all, (0, me, 0), (N_DEV, 1, ada_cols)).reshape(1, 3 * D_MODEL)
    shift, scale, gate = mod[:, :D_MODEL], mod[:, D_MODEL:2 * D_MODEL], mod[:, 2 * D_MODEL:]
    scale1p = 1.0 + scale
    wt_full = wt_all.reshape(N_DEV * in_cols, D_MODEL)
    g0 = 4 * WIDTH + HEADS
    w_main = jnp.concatenate([wt_full[:4 * WIDTH], wt_full[g0:g0 + 4 * WIDTH]], axis=0)
    w_small = jnp.concatenate([wt_full[4 * WIDTH:g0], wt_full[g0 + 4 * WIDTH:],
                               jnp.zeros((N_SMALL - 3 * HEADS, D_MODEL), bf16)], axis=0)
    w_out_full = w_out_all.reshape(2 * WIDTH, D_MODEL)
    conv_full = jnp.transpose(conv_all, (1, 0, 2)).reshape(CONV_K, 3 * WIDTH)

    qn_g, kn_g, gn_g = fox_qn_g.reshape(1, LANES), fox_kn_g.reshape(1, LANES), gdn_norm_g.reshape(1, LANES)
    bvec = _lanes((0, b_fgate), (HEADS, gdn_dt_bias))
    alog = _lanes((HEADS, gdn_A_log))
    fg = final_g.reshape(1, D_MODEL)

    h_bf = _norm_mod(x2, norm_g, scale1p, shift)
    p_main, p_small = _in_proj(h_bf, w_main, w_small)
    qs, kn, vb, gq, gk, gv, small = _prep(p_main, p_small, qn_g, kn_g, conv_full, bvec, alog)
    fox_o, lse = _fox_fwd(qs, kn, vb)
    gu, gw, gqg, gkd, gattn, t_inv, eg_last = _gdn_intra_fwd(gq, gk, gv, small)
    gdn_o, states = _gdn_scan_fwd(gu, gw, gqg, gkd, gattn, eg_last)
    mixed = _mix_fwd(fox_o, gdn_o, p_main, gn_g)

    loss_row, dy1, dz, dmixed, d_final_g, d_gate = _out_head(mixed, w_out_full, x2, tgt, gate, fg)
    loss = lax.psum(loss_row[0, 0], AXES)
    dw_out = _matmul_tn("dw_out", mixed, dz, bf16)
    do_fox, delta, dfz, dgz, dgdn_o, d_gn_g = _mix_bwd(dmixed, fox_o, gdn_o, p_main, gn_g)
    dqs, dkn, dvf, df_key, df_query = _fox_bwd(qs, kn, vb, do_fox, lse, delta)
    du, dw, dqg, dkd, dattn, deg = _gdn_scan_bwd(gu, gw, gqg, gkd, gattn, eg_last, states, dgdn_o)
    dgq, dgk, dgv, d_small = _gdn_intra_bwd(gq, gk, gv, small, t_inv, du, dw, dqg, dkd, dattn, deg)
    df_small = jnp.pad(jnp.transpose(df_key.reshape(HEADS, s_len)), ((0, 0), (0, N_SMALL - HEADS)))
    dfq, dfk, dgqkv, dp_small, d_qn_g, d_kn_g, sums, d_conv = _post1(
        p_main, p_small, qn_g, kn_g, conv_full, bvec, alog, dqs, dkn, dgq, dgk, dgv, d_small, df_small, df_query)
    dp_pieces = [dfq, dfk, dvf, dfz, dgqkv, dgz]
    dh = _in_proj_bwd(dp_pieces, dp_small, w_main, w_small)
    grad_x, d_shift, d_scale, d_norm_g = _adaln_bwd(dh, x2, dy1, norm_g, scale1p)
    dw_rows = [_matmul_tn("dw_main_%d" % n, piece, h_bf, bf16) for n, piece in enumerate(dp_pieces)]
    dw_small = _matmul_tn("dw_small", dp_small, h_bf, bf16)
    dw_in_full = jnp.concatenate(dw_rows[:4] + [dw_small[:HEADS]] + dw_rows[4:] + [dw_small[HEADS:3 * HEADS]],
                                 axis=0)
    dw_in_parts = dw_in_full.reshape(N_DEV, in_cols, D_MODEL)
    dw_out_parts = dw_out.reshape(N_DEV, w_out.shape[1], D_MODEL)

    dmod = jnp.concatenate([d_shift, d_scale, d_gate], axis=1)
    small_grads = _pack([d_norm_g, dmod, sums[0, :HEADS], d_qn_g, d_kn_g, sums[1, HEADS:2 * HEADS],
                         sums[0, HEADS:2 * HEADS], d_gn_g, d_final_g], SMALL_PACK)
    conv_grad = d_conv[:CONV_K]
    pair_in, pair_out = _pair_exchange("pair_grads", [dw_in_parts, dw_out_parts])
    core = lax.axis_index("c").astype(jnp.int32).reshape(1)
    dw_in_recv, dw_out_recv = _chip_exchange(
        "chip_grads", [_pair_sum("pair_sum_w_in", dw_in_parts, pair_in, core),
                       _pair_sum("pair_sum_w_out", dw_out_parts, pair_out, core)])
    small_all, conv_all_g = _gather_direct("gather_small_grads", [small_grads, conv_grad])

    outs = {}
    to_t = lambda t: jnp.transpose(t, (0, 2, 1))
    outs["w_in"] = tuple(to_t(t) for t in _adamw("adamw_w_in", to_t(w_in), dw_in_recv, to_t(m_w_in), to_t(v_w_in),
                                                  in_cols, 256))
    outs["w_out"] = _adamw("adamw_w_out", w_out, dw_out_recv, m_w_out, v_w_out, 128)
    conv_mine = lax.dynamic_slice(jnp.transpose(conv_all_g.reshape(N_DEV, CONV_K, N_DEV, conv_cols), (0, 2, 1, 3)),
                                  (0, me, 0, 0), (N_DEV, 1, CONV_K, conv_cols)).reshape(N_DEV, CONV_K, conv_cols)
    outs["gdn_conv_w"] = _adamw("adamw_conv", gdn_conv_w, conv_mine, m_gdn_conv_w, v_gdn_conv_w, CONV_K)
    small_all = small_all.reshape(N_DEV, 1, SMALL_PACK)
    dmod_all = small_all[:, 0, D_MODEL:D_MODEL + 3 * D_MODEL]
    dmod_mine = lax.dynamic_slice(dmod_all, (0, me * ada_cols), (N_DEV, ada_cols))
    c_all_t = jnp.pad(jnp.transpose(c_all), ((0, 0), (0, LANES - N_DEV)))
    g_w_ada = _w_ada_grad(c_all_t, jnp.pad(dmod_mine, ((0, LANES - N_DEV), (0, 0))))
    outs["w_ada"] = _adamw("adamw_w_ada", w_ada, g_w_ada[None], m_w_ada, v_w_ada, 256)
    given = dict(norm_g=(norm_g, m_norm_g, v_norm_g), b_ada=(b_ada, m_b_ada, v_b_ada), b_fgate=(b_fgate, m_b_fgate, v_b_fgate),
                 fox_qn_g=(fox_qn_g, m_fox_qn_g, v_fox_qn_g), fox_kn_g=(fox_kn_g, m_fox_kn_g, v_fox_kn_g),
                 gdn_A_log=(gdn_A_log, m_gdn_A_log, v_gdn_A_log), gdn_dt_bias=(gdn_dt_bias, m_gdn_dt_bias, v_gdn_dt_bias),
                 gdn_norm_g=(gdn_norm_g, m_gdn_norm_g, v_gdn_norm_g), final_g=(final_g, m_final_g, v_final_g))
    w_pack = _pack([given[n][0] for n in SMALL_NAMES], SMALL_PACK)
    m_pack = _pack([given[n][1] for n in SMALL_NAMES], SMALL_PACK)
    v_pack = _pack([given[n][2] for n in SMALL_NAMES], SMALL_PACK)
    packed = _adamw("adamw_small", w_pack[None], small_all, m_pack[None], v_pack[None], 1)
    off = 0
    for n, size in zip(SMALL_NAMES, SMALL_SIZES):
        outs[n] = tuple(t[0, 0, off:off + size].reshape(given[n][0].shape) for t in packed)
        off += size

    order = ("norm_g", "w_ada", "b_ada", "w_in", "b_fgate", "fox_qn_g", "fox_kn_g", "gdn_conv_w", "gdn_A_log",
             "gdn_dt_bias", "gdn_norm_g", "w_out", "final_g")
    result = [loss, grad_x.reshape(x.shape)]
    for part in range(4):
        result += [outs[n][part] for n in order]
    return tuple(result)
```

```python
import math

import jax
import jax.numpy as jnp
from jax import lax
from jax.experimental import pallas as pl
from jax.experimental.pallas import tpu as pltpu

f32 = jnp.float32
bf16 = jnp.bfloat16
HI = lax.Precision.HIGHEST

N_DEV = 8
AXES = ("x", "y", "c")
D_MODEL = 2048
HEADS = 8
HEAD_DIM = 128
WIDTH = HEADS * HEAD_DIM
CHUNK = 64
CONV_K = 4
EPS = 1e-6
QK_SCALE = HEAD_DIM ** -0.5
LOG2E = 1.0 / math.log(2.0)
LN2 = math.log(2.0)
N_MAIN = 8 * WIDTH
N_SMALL = 128
LANE_F, LANE_G, LANE_BETA, LANE_GC, LANE_GLAST = 0, 8, 16, 24, 32
IN_WIDTH = 8 * WIDTH + 3 * HEADS
LANES = 128
VMEM_LIMIT = 56 * 1024 * 1024

ADAM_LR, ADAM_B1, ADAM_B2, ADAM_EPS, ADAM_WD, ADAM_STEP = 0.001, 0.9, 0.999, 1e-08, 0.01, 10


def _params(*sem):
    return pltpu.CompilerParams(dimension_semantics=sem, vmem_limit_bytes=VMEM_LIMIT)


def _iota(shape, dim):
    return lax.broadcasted_iota(jnp.int32, shape, dim)


def _sigmoid(z):
    return 1.0 / (1.0 + jnp.exp(-z))


def _softplus_parts(z):
    t = jnp.log(1.0 + jnp.exp(-jnp.abs(z)))
    return jnp.minimum(z, 0.0) - t, jnp.maximum(z, 0.0) + t


def _dg(a, b, ca, cb, prec=None):
    if a.ndim == 3:
        dims = (((ca + 1,), (cb + 1,)), ((0,), (0,)))
    else:
        dims = (((ca,), (cb,)), ((), ()))
    return lax.dot_general(a, b, dims, preferred_element_type=f32, precision=prec)


def _dot_bf16(a, b, ca, cb):
    return _dg(a.astype(bf16), b.astype(bf16), ca, cb)


def _split_bf16(a):
    hi = a.astype(bf16)
    return hi, (a - hi.astype(f32)).astype(bf16)


def _dot_3pass(a, b, ca, cb):
    a_hi, a_lo = _split_bf16(a)
    b_hi, b_lo = _split_bf16(b)
    return _dg(a_hi, b_hi, ca, cb) + (_dg(a_hi, b_lo, ca, cb) + _dg(a_lo, b_hi, ca, cb))


def _make_mm(dot):
    def nn_(a, b):
        return dot(a, b, 1, 0)

    def nt_(a, b):
        return dot(a, b, 1, 1)

    def tn_(a, b):
        return dot(a, b, 0, 0)

    @jax.custom_vjp
    def nn(a, b):
        return nn_(a, b)

    @jax.custom_vjp
    def nt(a, b):
        return nt_(a, b)

    @jax.custom_vjp
    def tn(a, b):
        return tn_(a, b)

    nn.defvjp(lambda a, b: (nn_(a, b), (a, b)), lambda r, g: (nt_(g, r[1]), tn_(r[0], g)))
    nt.defvjp(lambda a, b: (nt_(a, b), (a, b)), lambda r, g: (nn_(g, r[1]), tn_(g, r[0])))
    tn.defvjp(lambda a, b: (tn_(a, b), (a, b)), lambda r, g: (nt_(r[1], g), nn_(r[0], g)))
    return (nn_, nt_, tn_), (nn, nt, tn)


_BF_PLAIN, _BF_VJP = _make_mm(_dot_bf16)
_X3_PLAIN, _X3_VJP = _make_mm(_dot_3pass)


def _inv_unit_lower_many(ms):
    c = CHUNK
    nn = _X3_PLAIN[0]
    eye = (_iota((c, c), 0) == _iota((c, c), 1)).astype(f32)
    top = _iota((2 * c, c), 0) < c
    xs = [jnp.concatenate([eye - m, nn(m, m)], axis=0) for m in ms]
    for _ in range(int(math.log2(CHUNK)) - 2):
        xs = [jnp.where(top, x, 0.0) + nn(x, x[c:]) for x in xs]
    return [x[:c] + nn(x[:c], x[c:]) for x in xs]


@jax.custom_vjp
def _inv_given(m, t):
    return t


_inv_given.defvjp(lambda m, t: (t, t),
                  lambda t, g: (-_X3_PLAIN[1](_X3_PLAIN[2](t, g), t), jnp.zeros_like(t)))

SUBLANES = 8


def _gdn_intra_pre(q, k, v, gc_b, g_last_b, beta_b, bnt):
    c = CHUNK
    r_i, c_i = _iota((c, c), 0), _iota((c, c), 1)
    lower, strict = r_i >= c_i, r_i > c_i
    gc_i = gc_b[..., :c]
    gc_j = jnp.swapaxes(gc_i, -1, -2)
    decay = jnp.where(lower, jnp.exp(jnp.where(lower, gc_i - gc_j, 0.0)), 0.0)
    kb = k * beta_b
    both = bnt(jnp.concatenate([kb, q], axis=-2), k)
    m = jnp.where(strict, both[..., :c, :] * decay, 0.0)
    attn = jnp.where(lower, both[..., c:, :] * decay, 0.0)
    eg = jnp.exp(gc_b)
    rhs = jnp.concatenate([v * beta_b, kb * eg], axis=-1)
    k_dec = k * jnp.exp(g_last_b - gc_b)
    eg_last = jnp.exp(g_last_b[..., :SUBLANES, :])
    return m, rhs, q * eg, k_dec, attn, eg_last


def _gdn_intra(q, k, v, gc_b, g_last_b, beta_b, t_known):
    m, rhs, qg, k_dec, attn, eg_last = _gdn_intra_pre(q, k, v, gc_b, g_last_b, beta_b, _BF_VJP[1])
    return _X3_VJP[0](_inv_given(m, t_known), rhs), qg, k_dec, attn, eg_last


def _scale_rows(s, eg_last):
    return (s.reshape(HEAD_DIM // SUBLANES, SUBLANES, HEAD_DIM) * eg_last[None]).reshape(HEAD_DIM, HEAD_DIM)


def _my_index():
    return 4 * lax.axis_index("x") + 2 * lax.axis_index("y") + lax.axis_index("c")


def _peer(d):
    x, y, c = lax.axis_index("x"), lax.axis_index("y"), lax.axis_index("c")
    px, py, pc = (x + (d >> 2)) % 2, (y + ((d >> 1) & 1)) % 2, (c + (d & 1)) % 2
    return (px, py, pc), 4 * px + 2 * py + pc


def _gather_direct(name, arrays):
    n = len(arrays)

    def body(*refs):
        srcs, dsts = refs[:n], refs[n:2 * n]
        send_sems, recv_sems, local_sems = refs[2 * n:]
        me = _my_index()

        def copy(k, d, started):
            peer, pidx = _peer(d)
            return pltpu.make_async_remote_copy(
                src_ref=srcs[k], dst_ref=dsts[k].at[me if started else pidx], send_sem=send_sems.at[k * 7 + d - 1],
                recv_sem=recv_sems.at[k * 7 + d - 1], device_id=peer, device_id_type=pl.DeviceIdType.MESH)

        local = [pltpu.make_async_copy(srcs[k], dsts[k].at[me], local_sems.at[k]) for k in range(n)]
        sends = [copy(k, d, True) for k in range(n) for d in range(1, N_DEV)]
        for cp in local + sends:
            cp.start()
        for k in range(n):
            for d in range(1, N_DEV):
                copy(k, d, False).wait_recv()
        for cp in sends:
            cp.wait_send()
        for cp in local:
            cp.wait()

    out_shape = [jax.ShapeDtypeStruct((N_DEV,) + a.shape, a.dtype) for a in arrays]
    any_spec = pl.BlockSpec(memory_space=pl.ANY)
    return pl.pallas_call(
        body, name=name, out_shape=out_shape, in_specs=[any_spec] * n, out_specs=[any_spec] * n,
        scratch_shapes=[pltpu.SemaphoreType.DMA((7 * n,)), pltpu.SemaphoreType.DMA((7 * n,)),
                        pltpu.SemaphoreType.DMA((n,))],
        compiler_params=pltpu.CompilerParams(has_side_effects=True),
    )(*arrays)


N_CHIPS = 4


def _pair_exchange(name, arrays):
    n = len(arrays)

    def body(*refs):
        srcs, dsts = refs[:n], refs[n:2 * n]
        send_sems, recv_sems = refs[2 * n:]
        x, y, c = lax.axis_index("x"), lax.axis_index("y"), lax.axis_index("c")
        sibling = (x, y, 1 - c)

        def copy(k, j):
            return pltpu.make_async_remote_copy(
                src_ref=srcs[k].at[2 * j + (1 - c)], dst_ref=dsts[k].at[j], send_sem=send_sems.at[k * N_CHIPS + j],
                recv_sem=recv_sems.at[k * N_CHIPS + j], device_id=sibling, device_id_type=pl.DeviceIdType.MESH)

        copies = [copy(k, j) for k in range(n) for j in range(N_CHIPS)]
        for cp in copies:
            cp.start()
        for cp in copies:
            cp.wait_recv()
        for cp in copies:
            cp.wait_send()

    any_spec = pl.BlockSpec(memory_space=pl.ANY)
    return pl.pallas_call(
        body, name=name, out_shape=[jax.ShapeDtypeStruct((N_CHIPS,) + a.shape[1:], a.dtype) for a in arrays],
        in_specs=[any_spec] * n, out_specs=[any_spec] * n,
        scratch_shapes=[pltpu.SemaphoreType.DMA((N_CHIPS * n,)), pltpu.SemaphoreType.DMA((N_CHIPS * n,))],
        compiler_params=pltpu.CompilerParams(has_side_effects=True),
    )(*arrays)


def _chip_exchange(name, arrays):
    n = len(arrays)

    def body(*refs):
        srcs, dsts = refs[:n], refs[n:2 * n]
        send_sems, recv_sems, local_sems = refs[2 * n:]
        x, y, c = lax.axis_index("x"), lax.axis_index("y"), lax.axis_index("c")
        my_chip = 2 * x + y

        def peer(d):
            px, py = (x + (d >> 1)) % 2, (y + (d & 1)) % 2
            return (px, py, c), 2 * px + py

        def remote(k, d, started):
            to, chip = peer(d)
            return pltpu.make_async_remote_copy(
                src_ref=srcs[k].at[chip], dst_ref=dsts[k].at[my_chip if started else chip],
                send_sem=send_sems.at[k * 3 + d - 1], recv_sem=recv_sems.at[k * 3 + d - 1],
                device_id=to, device_id_type=pl.DeviceIdType.MESH)

        local = [pltpu.make_async_copy(srcs[k].at[my_chip], dsts[k].at[my_chip], local_sems.at[k]) for k in range(n)]
        sends = [remote(k, d, True) for k in range(n) for d in range(1, N_CHIPS)]
        for cp in local + sends:
            cp.start()
        for k in range(n):
            for d in range(1, N_CHIPS):
                remote(k, d, False).wait_recv()
        for cp in sends:
            cp.wait_send()
        for cp in local:
            cp.wait()

    any_spec = pl.BlockSpec(memory_space=pl.ANY)
    return pl.pallas_call(
        body, name=name, out_shape=[jax.ShapeDtypeStruct(a.shape, a.dtype) for a in arrays],
        in_specs=[any_spec] * n, out_specs=[any_spec] * n,
        scratch_shapes=[pltpu.SemaphoreType.DMA((3 * n,)), pltpu.SemaphoreType.DMA((3 * n,)),
                        pltpu.SemaphoreType.DMA((n,))],
        compiler_params=pltpu.CompilerParams(has_side_effects=True),
    )(*arrays)


def _pair_sum(name, parts, received, core):
    n_blocks, rows, cols = received.shape
    tr = rows if rows % 256 else 256

    def body(core_ref, mine_ref, recv_ref, o_ref):
        o_ref[...] = (mine_ref[...].astype(f32) + recv_ref[...].astype(f32)).astype(bf16)

    return pl.pallas_call(
        body, name=name,
        grid_spec=pltpu.PrefetchScalarGridSpec(
            num_scalar_prefetch=1, grid=(n_blocks, rows // tr),
            in_specs=[pl.BlockSpec((1, tr, cols), lambda j, i, core_ref: (2 * j + core_ref[0], i, 0)),
                      pl.BlockSpec((1, tr, cols), lambda j, i, core_ref: (j, i, 0))],
            out_specs=pl.BlockSpec((1, tr, cols), lambda j, i, core_ref: (j, i, 0))),
        out_shape=jax.ShapeDtypeStruct((n_blocks, rows, cols), bf16),
        compiler_params=_params("parallel", "parallel"),
    )(core, parts, received)


def _gather_two_level(name, arrays):
    n = len(arrays)

    def body(*refs):
        srcs, dsts = refs[:n], refs[n:2 * n]
        send_sems, recv_sems, local_sems = refs[2 * n:]
        x, y, c = lax.axis_index("x"), lax.axis_index("y"), lax.axis_index("c")
        sibling = (x, y, 1 - c)
        chips = [((x + 1) % 2, y), (x, (y + 1) % 2), ((x + 1) % 2, (y + 1) % 2)]

        def index(px, py, pc):
            return 4 * px + 2 * py + pc

        def copy(k, slot, block, to, src=None):
            return pltpu.make_async_remote_copy(
                src_ref=dsts[k].at[index(*block)] if src is None else src, dst_ref=dsts[k].at[index(*block)],
                send_sem=send_sems.at[k * 7 + slot], recv_sem=recv_sems.at[k * 7 + slot],
                device_id=to, device_id_type=pl.DeviceIdType.MESH)

        me = (x, y, c)
        local = [pltpu.make_async_copy(srcs[k], dsts[k].at[index(*me)], local_sems.at[k]) for k in range(n)]
        first = [copy(k, 0, me, sibling, src=srcs[k]) for k in range(n)]
        first += [copy(k, 1 + j, me, (*chip, c), src=srcs[k]) for j, chip in enumerate(chips) for k in range(n)]
        for cp in local + first:
            cp.start()
        passed = []
        for j, chip in enumerate(chips):
            for k in range(n):
                copy(k, 1 + j, (*chip, c), me).wait_recv()
                fwd = copy(k, 4 + j, (*chip, c), sibling)
                fwd.start()
                passed.append(fwd)
        for k in range(n):
            copy(k, 0, sibling, me).wait_recv()
            for j, chip in enumerate(chips):
                copy(k, 4 + j, (*chip, 1 - c), me).wait_recv()
        for cp in first + passed:
            cp.wait_send()
        for cp in local:
            cp.wait()

    any_spec = pl.BlockSpec(memory_space=pl.ANY)
    return pl.pallas_call(
        body, name=name, out_shape=[jax.ShapeDtypeStruct((N_DEV,) + a.shape, a.dtype) for a in arrays],
        in_specs=[any_spec] * n, out_specs=[any_spec] * n,
        scratch_shapes=[pltpu.SemaphoreType.DMA((7 * n,)), pltpu.SemaphoreType.DMA((7 * n,)),
                        pltpu.SemaphoreType.DMA((n,))],
        compiler_params=pltpu.CompilerParams(has_side_effects=True),
    )(*arrays)


def _mod_shard(c_all, w_ada, b_shard):
    def body(c_ref, w_ref, b_ref, o_ref):
        cv = c_ref[...]
        ca = cv * _sigmoid(cv)
        o_ref[...] = jnp.dot(ca.astype(bf16), w_ref[...].astype(bf16), preferred_element_type=f32) + b_ref[...]

    return pl.pallas_call(body, name="mod_shard", out_shape=jax.ShapeDtypeStruct((N_DEV, w_ada.shape[1]), f32),
                          compiler_params=_params())(c_all, w_ada, b_shard)


def _norm_mod(x, norm_g, scale1p, shift):
    s_len = x.shape[0]
    tm = 512

    def body(x_ref, g_ref, sc_ref, sh_ref, h_ref):
        xb = x_ref[...]
        r = lax.rsqrt(jnp.mean(xb * xb, axis=-1, keepdims=True) + EPS)
        h_ref[...] = ((xb * r * g_ref[...]) * sc_ref[...] + sh_ref[...]).astype(bf16)

    row = pl.BlockSpec((tm, D_MODEL), lambda i: (i, 0))
    vec = pl.BlockSpec((1, D_MODEL), lambda i: (0, 0))
    return pl.pallas_call(body, name="norm_mod", grid=(s_len // tm,), in_specs=[row, vec, vec, vec], out_specs=row,
                          out_shape=jax.ShapeDtypeStruct((s_len, D_MODEL), bf16),
                          compiler_params=_params("parallel"))(x, norm_g, scale1p, shift)


def _in_proj(h, wt_main, wt_small):
    s_len = h.shape[0]
    tm, tn = min(1024, s_len), 1024

    def body(h_ref, w_ref, ws_ref, p_ref, ps_ref):
        @pl.when(pl.program_id(1) == 0)
        def _():
            ps_ref[...] = _dg(h_ref[...], ws_ref[...], 1, 1)

        p_ref[...] = _dg(h_ref[...], w_ref[...], 1, 1)

    return pl.pallas_call(
        body, name="in_proj", grid=(s_len // tm, N_MAIN // tn),
        in_specs=[pl.BlockSpec((tm, D_MODEL), lambda i, j: (i, 0)),
                  pl.BlockSpec((tn, D_MODEL), lambda i, j: (j, 0)),
                  pl.BlockSpec((N_SMALL, D_MODEL), lambda i, j: (0, 0))],
        out_specs=[pl.BlockSpec((tm, tn), lambda i, j: (i, j)),
                   pl.BlockSpec((tm, N_SMALL), lambda i, j: (i, 0))],
        out_shape=[jax.ShapeDtypeStruct((s_len, N_MAIN), f32), jax.ShapeDtypeStruct((s_len, N_SMALL), f32)],
        compiler_params=_params("parallel", "arbitrary"),
    )(h, wt_main, wt_small)


PREP_TM = 256
HALO = 8


def _conv_section(xe_ref, cw_ref, cols, tm):
    acc = cw_ref[pl.ds(CONV_K - 1, 1), cols] * xe_ref[pl.ds(HALO, tm), :]
    for tap in range(CONV_K - 1):
        acc = acc + cw_ref[pl.ds(tap, 1), cols] * xe_ref[pl.ds(HALO - (CONV_K - 1) + tap, tm), :]
    return acc


def _small_fwd(ps, bvec, alog):
    z = ps + bvec
    logsig, softp = _softplus_parts(z)
    gval = -jnp.exp(alog) * softp
    beta = _sigmoid(ps)
    return z, logsig, gval, beta


def _head_lane(block, lane):
    return jnp.sum(jnp.where(_iota(block.shape, 1) == lane, block, 0.0), axis=1, keepdims=True)


def _head_slab(block, lane):
    return jnp.broadcast_to(_head_lane(block, lane), block.shape)


def _chunk_masks(tm):
    r, c = _iota((tm, tm), 0), _iota((tm, tm), 1)
    same = (r // CHUNK) == (c // CHUNK)
    return (same & (r >= c)).astype(f32), same.astype(f32)


def _prep(p_main, p_small, qn_g, kn_g, conv_w, bvec, alog):
    s_len = p_main.shape[0]
    tm = PREP_TM
    nb = s_len // tm

    def body(fq_ref, fk_ref, fv_ref, gq_ref, gk_ref, gv_ref, hq_ref, hk_ref, hv_ref, ps_ref, qg_ref, kg_ref,
             cw_ref, bv_ref, al_ref,
             qs_ref, kn_ref, vb_ref, gqo_ref, gko_ref, gvo_ref, small_ref, xe_sc, carry_sc):
        i = pl.program_id(0)

        @pl.when(i == 0)
        def _():
            carry_sc[...] = jnp.zeros_like(carry_sc)

        vb_ref[...] = fv_ref[...].astype(bf16)

        first = i == 0
        for sec, (x_ref, halo_ref, o_ref) in enumerate(((gq_ref, hq_ref, gqo_ref), (gk_ref, hk_ref, gko_ref),
                                                        (gv_ref, hv_ref, gvo_ref))):
            xe_sc[0:HALO, :] = jnp.where(first, 0.0, halo_ref[...])
            xe_sc[HALO:, :] = x_ref[...]
            cv = _conv_section(xe_sc, cw_ref, slice(sec * WIDTH, (sec + 1) * WIDTH), tm)
            y = cv * _sigmoid(cv)
            if sec == 2:
                o_ref[...] = y
            else:
                mul = QK_SCALE if sec == 0 else 1.0
                for h in range(HEADS):
                    sl = slice(h * HEAD_DIM, (h + 1) * HEAD_DIM)
                    yh = y[:, sl]
                    o_ref[:, sl] = yh * (lax.rsqrt(jnp.sum(yh * yh, axis=-1, keepdims=True) + EPS) * mul)

        lane = _iota((tm, N_SMALL), 1)
        _, logsig, gval, beta = _small_fwd(ps_ref[...], bv_ref[...], al_ref[...])
        lf = jnp.where(lane < HEADS, logsig, 0.0)
        tri = (_iota((tm, tm), 0) >= _iota((tm, tm), 1)).astype(f32)
        fcum = jnp.dot(tri, lf, preferred_element_type=f32, precision=HI) + carry_sc[...]
        carry_sc[...] += jnp.sum(lf, axis=0, keepdims=True)
        tri_c, ones_c = _chunk_masks(tm)
        g_lanes = jnp.where((lane >= LANE_G) & (lane < LANE_G + HEADS), gval, 0.0)
        gc = jnp.dot(tri_c, g_lanes, preferred_element_type=f32, precision=HI)
        g_last = jnp.dot(ones_c, g_lanes, preferred_element_type=f32, precision=HI)
        small = jnp.where(lane < LANE_G, fcum, jnp.where(lane < LANE_BETA, gval, jnp.where(lane < LANE_GC, beta, 0.0)))
        small_ref[...] = small + pltpu.roll(gc, LANE_GC - LANE_G, 1) + pltpu.roll(g_last, LANE_GLAST - LANE_G, 1)

        qg, kg = qg_ref[...], kg_ref[...]
        f2 = fcum * LOG2E
        for h in range(HEADS):
            sl = slice(h * HEAD_DIM, (h + 1) * HEAD_DIM)
            q = fq_ref[:, sl]
            rq = lax.rsqrt(jnp.mean(q * q, axis=-1, keepdims=True) + EPS)
            k = fk_ref[:, sl]
            rk = lax.rsqrt(jnp.mean(k * k, axis=-1, keepdims=True) + EPS)
            f_col = _head_lane(f2, LANE_F + h)
            hi = f_col.astype(bf16).astype(f32)
            mid = (f_col - hi).astype(bf16).astype(f32)
            lo = f_col - hi - mid
            q_bias = jnp.where(lane == 0, hi, jnp.where(lane == 1, mid, jnp.where(lane == 2, lo,
                                                                                  jnp.where(lane < 6, 1.0, 0.0))))
            k_bias = jnp.where(lane < 3, 1.0, jnp.where(lane == 3, -hi, jnp.where(lane == 4, -mid,
                                                                                 jnp.where(lane == 5, -lo, 0.0))))
            base = 2 * h * HEAD_DIM
            qs_ref[:, base:base + HEAD_DIM] = (q * rq * qg * (QK_SCALE * LOG2E)).astype(bf16)
            qs_ref[:, base + HEAD_DIM:base + 2 * HEAD_DIM] = q_bias.astype(bf16)
            kn_ref[:, base:base + HEAD_DIM] = (k * rk * kg).astype(bf16)
            kn_ref[:, base + HEAD_DIM:base + 2 * HEAD_DIM] = k_bias.astype(bf16)

    def col(cb):
        return pl.BlockSpec((tm, WIDTH), lambda i: (i, cb))

    def halo(cb):
        return pl.BlockSpec((HALO, WIDTH), lambda i: (jnp.maximum(i * (tm // HALO) - 1, 0), cb))

    vec = pl.BlockSpec((1, LANES), lambda i: (0, 0))
    wide_f32 = jax.ShapeDtypeStruct((s_len, WIDTH), f32)
    wide_bf = jax.ShapeDtypeStruct((s_len, WIDTH), bf16)
    out_col = pl.BlockSpec((tm, WIDTH), lambda i: (i, 0))
    return pl.pallas_call(
        body, name="prep", grid=(nb,),
        in_specs=[col(0), col(1), col(2), col(4), col(5), col(6), halo(4), halo(5), halo(6),
                  pl.BlockSpec((tm, N_SMALL), lambda i: (i, 0)), vec, vec,
                  pl.BlockSpec((CONV_K, 3 * WIDTH), lambda i: (0, 0)), vec, vec],
        out_specs=[pl.BlockSpec((tm, 2 * WIDTH), lambda i: (i, 0))] * 2 + [out_col] * 4
                  + [pl.BlockSpec((tm, N_SMALL), lambda i: (i, 0))],
        out_shape=[jax.ShapeDtypeStruct((s_len, 2 * WIDTH), bf16)] * 2 + [wide_bf, wide_f32, wide_f32, wide_f32,
                                                                          jax.ShapeDtypeStruct((s_len, N_SMALL), f32)],
        scratch_shapes=[pltpu.VMEM((tm + HALO, WIDTH), f32), pltpu.VMEM((1, N_SMALL), f32)],
        compiler_params=_params("arbitrary"),
    )(p_main, p_main, p_main, p_main, p_main, p_main, p_main, p_main, p_main, p_small, qn_g, kn_g, conv_w, bvec, alog)


FOX_T = 1024
NEG_BIG = -1e30


def _fox_fwd(qs, kn, vb):
    s_len = qs.shape[0]
    t = FOX_T
    nq = s_len // t

    def body(q_ref, k_ref, v_ref, o_ref, lse_ref):
        qi = pl.program_id(1)
        q = q_ref[...]

        causal = _iota((t, t), 0) >= _iota((t, t), 1)

        def step(j, carry, masked):
            m, l, acc = carry
            rows = pl.ds(pl.multiple_of(j * t, t), t)
            s = _dg(q, k_ref[rows, :], 1, 1)
            if masked:
                s = jnp.where(causal, s, NEG_BIG)
            m_new = jnp.maximum(m, jnp.max(s, axis=-1, keepdims=True))
            p = jnp.exp2(s - m_new)
            alpha = jnp.exp2(m - m_new)
            l = alpha * l + jnp.sum(p, axis=-1, keepdims=True)
            acc = alpha * acc + jnp.dot(p.astype(bf16), v_ref[rows, :], preferred_element_type=f32)
            return m_new, l, acc

        init = (jnp.full((t, 1), NEG_BIG, f32), jnp.zeros((t, 1), f32), jnp.zeros((t, HEAD_DIM), f32))
        carry = lax.fori_loop(0, qi, lambda j, c: step(j, c, False), init)
        m, l, acc = step(qi, carry, True)
        o_ref[...] = acc / l
        lse_ref[0] = m + jnp.log2(l)

    return pl.pallas_call(
        body, name="fox_fwd", grid=(HEADS, nq),
        in_specs=[pl.BlockSpec((t, 2 * HEAD_DIM), lambda h, i: (i, h)),
                  pl.BlockSpec((s_len, 2 * HEAD_DIM), lambda h, i: (0, h)),
                  pl.BlockSpec((s_len, HEAD_DIM), lambda h, i: (0, h))],
        out_specs=[pl.BlockSpec((t, HEAD_DIM), lambda h, i: (i, h)),
                   pl.BlockSpec((1, t, 1), lambda h, i: (h, i, 0))],
        out_shape=[jax.ShapeDtypeStruct((s_len, WIDTH), f32), jax.ShapeDtypeStruct((HEADS, s_len, 1), f32)],
        compiler_params=_params("parallel", "arbitrary"),
    )(qs, kn, vb)


def _fox_bwd(qs, kn, vb, do, lse, delta):
    s_len = qs.shape[0]
    t = FOX_T
    nq = s_len // t
    half = t // 2

    def body(q_ref, do_ref, lse_ref, dl_ref, k_ref, v_ref, dq_ref, dk_ref, dvb_ref, df_ref, dfq_ref, dv_ref):
        head, qi = pl.program_id(0), pl.program_id(1)

        @pl.when(qi == 0)
        def _():
            dk_ref[...] = jnp.zeros_like(dk_ref)
            dv_ref[...] = jnp.zeros_like(dv_ref)
            df_ref[...] = jnp.zeros_like(df_ref)

        lse_col = lse_ref[0]
        dl = _head_lane(dl_ref[...], head)

        def update(q_rows, k_rows, df_lanes, j, carry, mask):
            dq, row_sum = carry
            q, do_b = q_ref[q_rows, :], do_ref[q_rows, :]
            p = jnp.exp2(_dg(q, k_ref[k_rows, :], 1, 1) - lse_col[q_rows])
            if mask is not None:
                p = jnp.where(mask, p, 0.0)
            ds = p * (_dg(do_b, v_ref[k_rows, :], 1, 1) - dl[q_rows])
            ds_b = ds.astype(bf16)
            dk_ref[k_rows, :] += _dg(ds_b, q_ref[q_rows, 0:HEAD_DIM], 0, 0)
            dv_ref[k_rows, :] += _dg(p.astype(bf16), do_b, 0, 0)
            df_ref[0, j, :, df_lanes] += -jnp.sum(ds, axis=0, keepdims=True)
            dq = dq + jnp.dot(ds_b, k_ref[k_rows, 0:HEAD_DIM], preferred_element_type=f32)
            return dq, row_sum + jnp.sum(ds, axis=-1, keepdims=True)

        everything, upper, lower = slice(0, t), slice(0, half), slice(half, t)
        carry = lax.fori_loop(
            0, qi, lambda j, c: update(everything, pl.ds(pl.multiple_of(j * t, t), t), everything, j, c, None),
            (jnp.zeros((t, HEAD_DIM), f32), jnp.zeros((t, 1), f32)))
        carry = update(everything, pl.ds(pl.multiple_of(qi * t, t), half), upper, qi, carry,
                       _iota((t, half), 0) >= _iota((t, half), 1))
        low = update(lower, pl.ds(pl.multiple_of(qi * t + half, half), half), lower, qi,
                     tuple(c[half:] for c in carry), _iota((half, half), 0) >= _iota((half, half), 1))
        dq, row_sum = (jnp.concatenate([c[:half], lo], axis=0) for c, lo in zip(carry, low))
        dq_ref[...] = dq
        dfq_ref[0] = row_sum

        @pl.when(qi == nq - 1)
        def _():
            dvb_ref[...] = dv_ref[...].astype(bf16)

    blk = pl.BlockSpec((t, HEAD_DIM), lambda h, i: (i, h))
    blk2 = pl.BlockSpec((t, 2 * HEAD_DIM), lambda h, i: (i, h))
    full = pl.BlockSpec((s_len, HEAD_DIM), lambda h, i: (0, h))
    full2 = pl.BlockSpec((s_len, 2 * HEAD_DIM), lambda h, i: (0, h))
    colv = pl.BlockSpec((1, t, 1), lambda h, i: (h, i, 0))
    rowv = pl.BlockSpec((1, nq, 1, t), lambda h, i: (h, 0, 0, 0))
    lanes = pl.BlockSpec((t, N_SMALL), lambda h, i: (i, 0))
    wide = jax.ShapeDtypeStruct((s_len, WIDTH), f32)
    return pl.pallas_call(
        body, name="fox_bwd", grid=(HEADS, nq),
        in_specs=[blk2, blk, colv, lanes, full2, full],
        out_specs=[blk, full, full, rowv, colv],
        out_shape=[wide, wide, jax.ShapeDtypeStruct((s_len, WIDTH), bf16), jax.ShapeDtypeStruct((HEADS, nq, 1, t), f32),
                   jax.ShapeDtypeStruct((HEADS, s_len, 1), f32)],
        scratch_shapes=[pltpu.VMEM((s_len, HEAD_DIM), f32)],
        compiler_params=_params("parallel", "arbitrary"),
    )(qs, do, lse, delta, kn, vb)


INTRA_CHUNKS = 8
SCAN_FWD_CHUNKS = 8
SCAN_BWD_CHUNKS = 4


def _gdn_intra_fwd(gq, gk, gv, small):
    s_len = gq.shape[0]
    cpb = INTRA_CHUNKS
    rows_blk = cpb * CHUNK
    n_chunks = s_len // CHUNK

    def body(q_ref, k_ref, v_ref, sm_ref, u_ref, w_ref, qg_ref, kd_ref, attn_ref, t_ref, eg_ref):
        head = pl.program_id(0)
        sm = sm_ref[...]
        gc_b, gl_b, beta_b = (_head_slab(sm, LANE_GC + head), _head_slab(sm, LANE_GLAST + head),
                              _head_slab(sm, LANE_BETA + head))
        ms, rhss = [], []
        for ci in range(cpb):
            rows = pl.ds(ci * CHUNK, CHUNK)
            sl = slice(ci * CHUNK, (ci + 1) * CHUNK)
            m, rhs, qg, kd, attn, eg_last = _gdn_intra_pre(q_ref[rows, :], k_ref[rows, :], v_ref[rows, :],
                                                           gc_b[sl], gl_b[sl], beta_b[sl], _BF_PLAIN[1])
            qg_ref[rows, :] = qg.astype(bf16)
            kd_ref[rows, :] = kd.astype(bf16)
            attn_ref[0, ci] = attn.astype(bf16)
            eg_ref[0, ci] = eg_last
            ms.append(m)
            rhss.append(rhs)
        for ci, (t, rhs) in enumerate(zip(_inv_unit_lower_many(ms), rhss)):
            rows = pl.ds(ci * CHUNK, CHUNK)
            t_ref[0, ci] = t
            uw = _X3_PLAIN[0](t, rhs)
            u_ref[rows, :] = uw[:, :HEAD_DIM]
            w_ref[rows, :] = uw[:, HEAD_DIM:].astype(bf16)

    blk = pl.BlockSpec((rows_blk, HEAD_DIM), lambda h, i: (i, h))
    sq = pl.BlockSpec((1, cpb, CHUNK, CHUNK), lambda h, i: (h, i, 0, 0))
    wide_bf = jax.ShapeDtypeStruct((s_len, WIDTH), bf16)
    return pl.pallas_call(
        body, name="gdn_intra_fwd", grid=(HEADS, s_len // rows_blk),
        in_specs=[blk] * 3 + [pl.BlockSpec((rows_blk, N_SMALL), lambda h, i: (i, 0))],
        out_specs=[blk] * 4 + [sq, sq, pl.BlockSpec((1, cpb, SUBLANES, HEAD_DIM), lambda h, i: (h, i, 0, 0))],
        out_shape=[jax.ShapeDtypeStruct((s_len, WIDTH), f32), wide_bf, wide_bf, wide_bf,
                   jax.ShapeDtypeStruct((HEADS, n_chunks, CHUNK, CHUNK), bf16),
                   jax.ShapeDtypeStruct((HEADS, n_chunks, CHUNK, CHUNK), f32),
                   jax.ShapeDtypeStruct((HEADS, n_chunks, SUBLANES, HEAD_DIM), f32)],
        compiler_params=_params("parallel", "parallel"),
    )(gq, gk, gv, small)


def _gdn_scan_fwd(u, w, qg, kd, attn, eg):
    s_len = u.shape[0]
    cpb = SCAN_FWD_CHUNKS
    rows_blk = cpb * CHUNK
    n_chunks = s_len // CHUNK

    def body(u_ref, w_ref, qg_ref, kd_ref, attn_ref, eg_ref, o_ref, st_ref, s_sc):
        @pl.when(pl.program_id(0) == 0)
        def _():
            s_sc[...] = jnp.zeros_like(s_sc)

        def chunk(ci, _):
            rows = pl.ds(pl.multiple_of(ci * CHUNK, CHUNK), CHUNK)
            cols = [slice(h * HEAD_DIM, (h + 1) * HEAD_DIM) for h in range(HEADS)]
            s0 = [s_sc[h] for h in range(HEADS)]
            s0_b = [s.astype(bf16) for s in s0]
            for h in range(HEADS):
                st_ref[h, ci] = s0[h]
            ws = [jnp.dot(w_ref[rows, cols[h]], s0_b[h], preferred_element_type=f32) for h in range(HEADS)]
            qs = [jnp.dot(qg_ref[rows, cols[h]], s0_b[h], preferred_element_type=f32) for h in range(HEADS)]
            vn_b = [(u_ref[rows, cols[h]] - ws[h]).astype(bf16) for h in range(HEADS)]
            av = [jnp.dot(attn_ref[h, ci], vn_b[h], preferred_element_type=f32) for h in range(HEADS)]
            kv = [_dg(kd_ref[rows, cols[h]], vn_b[h], 0, 0) for h in range(HEADS)]
            for h in range(HEADS):
                o_ref[rows, cols[h]] = qs[h] + av[h]
                s_sc[h] = _scale_rows(s0[h], eg_ref[h, ci]) + kv[h]
            return 0

        lax.fori_loop(0, cpb, chunk, 0)

    row = pl.BlockSpec((rows_blk, WIDTH), lambda i: (i, 0))
    return pl.pallas_call(
        body, name="gdn_scan_fwd", grid=(s_len // rows_blk,),
        in_specs=[row] * 4 + [pl.BlockSpec((HEADS, cpb, CHUNK, CHUNK), lambda i: (0, i, 0, 0)),
                              pl.BlockSpec((HEADS, cpb, SUBLANES, HEAD_DIM), lambda i: (0, i, 0, 0))],
        out_specs=[row, pl.BlockSpec((HEADS, cpb, HEAD_DIM, HEAD_DIM), lambda i: (0, i, 0, 0))],
        out_shape=[jax.ShapeDtypeStruct((s_len, WIDTH), f32),
                   jax.ShapeDtypeStruct((HEADS, n_chunks, HEAD_DIM, HEAD_DIM), f32)],
        scratch_shapes=[pltpu.VMEM((HEADS, HEAD_DIM, HEAD_DIM), f32)],
        compiler_params=_params("arbitrary"),
    )(u, w, qg, kd, attn, eg)


def _gdn_scan_bwd(u, w, qg, kd, attn, eg, states, d_o):
    s_len = u.shape[0]
    cpb = SCAN_BWD_CHUNKS
    rows_blk = cpb * CHUNK
    n_chunks = s_len // CHUNK
    nb = s_len // rows_blk

    def body(u_ref, w_ref, qg_ref, kd_ref, attn_ref, eg_ref, st_ref, do_ref,
             du_ref, dw_ref, dqg_ref, dkd_ref, dattn_ref, deg_ref, ds_sc):
        @pl.when(pl.program_id(0) == 0)
        def _():
            ds_sc[...] = jnp.zeros_like(ds_sc)

        def chunk(step, _):
            ci = cpb - 1 - step
            rows = pl.ds(pl.multiple_of(ci * CHUNK, CHUNK), CHUNK)
            hs = range(HEADS)
            cols = [slice(h * HEAD_DIM, (h + 1) * HEAD_DIM) for h in hs]
            s0 = [st_ref[h, ci] for h in hs]
            s0_b = [s.astype(bf16) for s in s0]
            ds1 = [ds_sc[h] for h in hs]
            ds1_b = [d.astype(bf16) for d in ds1]
            do_b = [do_ref[rows, cols[h]].astype(bf16) for h in hs]
            ws = [jnp.dot(w_ref[rows, cols[h]], s0_b[h], preferred_element_type=f32) for h in hs]
            ad = [_dg(attn_ref[h, ci], do_b[h], 0, 0) for h in hs]
            kd_ds = [jnp.dot(kd_ref[rows, cols[h]], ds1_b[h], preferred_element_type=f32) for h in hs]
            dqg = [_dg(do_b[h], s0_b[h], 1, 1) for h in hs]
            qd = [_dg(qg_ref[rows, cols[h]], do_b[h], 0, 0) for h in hs]
            vn_b = [(u_ref[rows, cols[h]] - ws[h]).astype(bf16) for h in hs]
            dvn = [ad[h] + kd_ds[h] for h in hs]
            dvn_b = [d.astype(bf16) for d in dvn]
            dattn = [_dg(do_b[h], vn_b[h], 1, 1) for h in hs]
            dkd = [_dg(vn_b[h], ds1_b[h], 1, 1) for h in hs]
            dw = [_dg(dvn_b[h], s0_b[h], 1, 1) for h in hs]
            wd = [_dg(w_ref[rows, cols[h]], dvn_b[h], 0, 0) for h in hs]
            for h in hs:
                dattn_ref[h, ci] = dattn[h]
                dqg_ref[rows, cols[h]] = dqg[h]
                dkd_ref[rows, cols[h]] = dkd[h]
                du_ref[rows, cols[h]] = dvn[h]
                dw_ref[rows, cols[h]] = -dw[h]
                ds_sc[h] = qd[h] - wd[h] + _scale_rows(ds1[h], eg_ref[h, ci])
                deg_ref[h, ci] = jnp.sum((ds1[h] * s0[h]).reshape(HEAD_DIM // SUBLANES, SUBLANES, HEAD_DIM), axis=0)
            return 0

        lax.fori_loop(0, cpb, chunk, 0)

    row = pl.BlockSpec((rows_blk, WIDTH), lambda i: (nb - 1 - i, 0))
    sq = pl.BlockSpec((HEADS, cpb, CHUNK, CHUNK), lambda i: (0, nb - 1 - i, 0, 0))
    egs = pl.BlockSpec((HEADS, cpb, SUBLANES, HEAD_DIM), lambda i: (0, nb - 1 - i, 0, 0))
    wide = jax.ShapeDtypeStruct((s_len, WIDTH), f32)
    return pl.pallas_call(
        body, name="gdn_scan_bwd", grid=(nb,),
        in_specs=[row] * 4 + [sq, egs, pl.BlockSpec((HEADS, cpb, HEAD_DIM, HEAD_DIM), lambda i: (0, nb - 1 - i, 0, 0)), row],
        out_specs=[row] * 4 + [sq, egs],
        out_shape=[wide] * 4 + [jax.ShapeDtypeStruct((HEADS, n_chunks, CHUNK, CHUNK), f32),
                                jax.ShapeDtypeStruct((HEADS, n_chunks, SUBLANES, HEAD_DIM), f32)],
        scratch_shapes=[pltpu.VMEM((HEADS, HEAD_DIM, HEAD_DIM), f32)],
        compiler_params=_params("arbitrary"),
    )(u, w, qg, kd, attn, eg, states, d_o)


def _gdn_intra_bwd(gq, gk, gv, small, t_inv, du, dw, dqg, dkd, dattn, deg):
    s_len = gq.shape[0]
    cpb = INTRA_CHUNKS
    rows_blk = cpb * CHUNK

    def body(q_ref, k_ref, v_ref, sm_ref, t_ref, du_ref, dw_ref, dqg_ref, dkd_ref, dattn_ref, deg_ref,
             dq_ref, dk_ref, dv_ref, dsm_ref):
        head = pl.program_id(1)

        def batch(value):
            return value.reshape(cpb, CHUNK, HEAD_DIM)

        sm = sm_ref[...]
        slabs = [batch(_head_slab(sm, first + head)) for first in (LANE_GC, LANE_GLAST, LANE_BETA)]
        t_known = t_ref[0]
        _, vjp = jax.vjp(lambda q, k, v, gc, gl, b: _gdn_intra(q, k, v, gc, gl, b, t_known),
                         batch(q_ref[...]), batch(k_ref[...]), batch(v_ref[...]), *slabs)
        duw = jnp.concatenate([batch(du_ref[...]), batch(dw_ref[...])], axis=-1)
        dq, dk, dv, dgc, dgl, db = vjp((duw, batch(dqg_ref[...]), batch(dkd_ref[...]), dattn_ref[0], deg_ref[0]))
        for ref, grad in zip((dq_ref, dk_ref, dv_ref), (dq, dk, dv)):
            ref[...] = grad.reshape(rows_blk, HEAD_DIM)

        @pl.when(head == 0)
        def _():
            dsm_ref[...] = jnp.zeros_like(dsm_ref)

        lane = _iota((rows_blk, N_SMALL), 1)
        acc = dsm_ref[...]
        for first, grad in ((LANE_GC, dgc), (LANE_GLAST, dgl), (LANE_BETA, db)):
            col = jnp.sum(grad.reshape(rows_blk, HEAD_DIM), axis=1, keepdims=True)
            acc = acc + jnp.where(lane == first + head, col, 0.0)
        dsm_ref[...] = acc

    blk = pl.BlockSpec((rows_blk, HEAD_DIM), lambda i, h: (i, h))
    sq = pl.BlockSpec((1, cpb, CHUNK, CHUNK), lambda i, h: (h, i, 0, 0))
    egs = pl.BlockSpec((1, cpb, SUBLANES, HEAD_DIM), lambda i, h: (h, i, 0, 0))
    lanes = pl.BlockSpec((rows_blk, N_SMALL), lambda i, h: (i, 0))
    wide = jax.ShapeDtypeStruct((s_len, WIDTH), f32)
    return pl.pallas_call(
        body, name="gdn_intra_bwd", grid=(s_len // rows_blk, HEADS),
        in_specs=[blk] * 3 + [lanes, sq] + [blk] * 4 + [sq, egs],
        out_specs=[blk] * 3 + [lanes],
        out_shape=[wide] * 3 + [jax.ShapeDtypeStruct((s_len, N_SMALL), f32)],
        compiler_params=_params("parallel", "arbitrary"),
    )(gq, gk, gv, small, t_inv, du, dw, dqg, dkd, dattn, deg)


MIX_TM = 256


def _mix_fwd(fox_o, gdn_o, p_main, gnorm_g):
    s_len = fox_o.shape[0]
    tm = MIX_TM

    def body(fo_ref, go_ref, fz_ref, gz_ref, g_ref, mixed_ref):
        fz = fz_ref[...]
        mixed_ref[:, 0:WIDTH] = (fo_ref[...] * (fz * _sigmoid(fz))).astype(bf16)
        gz = gz_ref[...]
        gate = gz * _sigmoid(gz)
        gg = g_ref[...]
        for h in range(HEADS):
            sl = slice(h * HEAD_DIM, (h + 1) * HEAD_DIM)
            o = go_ref[:, sl]
            r = lax.rsqrt(jnp.mean(o * o, axis=-1, keepdims=True) + EPS)
            mixed_ref[:, WIDTH + h * HEAD_DIM:WIDTH + (h + 1) * HEAD_DIM] = (o * r * gg * gate[:, sl]).astype(bf16)

    row = pl.BlockSpec((tm, WIDTH), lambda i: (i, 0))
    return pl.pallas_call(
        body, name="mix_fwd", grid=(s_len // tm,),
        in_specs=[row, row, pl.BlockSpec((tm, WIDTH), lambda i: (i, 3)), pl.BlockSpec((tm, WIDTH), lambda i: (i, 7)),
                  pl.BlockSpec((1, LANES), lambda i: (0, 0))],
        out_specs=pl.BlockSpec((tm, 2 * WIDTH), lambda i: (i, 0)),
        out_shape=jax.ShapeDtypeStruct((s_len, 2 * WIDTH), bf16),
        compiler_params=_params("parallel"),
    )(fox_o, gdn_o, p_main, p_main, gnorm_g)


def _silu_grad(z):
    sg = _sigmoid(z)
    return sg * (1.0 + z * (1.0 - sg))


def _mix_bwd(dmixed, fox_o, gdn_o, p_main, gnorm_g):
    s_len = fox_o.shape[0]
    tm = MIX_TM

    def body(dm_ref, fo_ref, go_ref, fz_ref, gz_ref, g_ref, dof_ref, delta_ref, dfz_ref, dgz_ref, dgo_ref, dg_ref):
        @pl.when(pl.program_id(0) == 0)
        def _():
            dg_ref[...] = jnp.zeros_like(dg_ref)

        lane = _iota((tm, LANES), 1)
        fz = fz_ref[...]
        dmf = dm_ref[:, 0:WIDTH]
        fo = fo_ref[...]
        dof = dmf * (fz * _sigmoid(fz))
        dof_ref[...] = dof.astype(bf16)
        dfz_ref[...] = (dmf * fo * _silu_grad(fz)).astype(bf16)
        prod = dof * fo
        delta = jnp.zeros((tm, LANES), f32)
        for h in range(HEADS):
            dh = jnp.sum(prod[:, h * HEAD_DIM:(h + 1) * HEAD_DIM], axis=-1, keepdims=True)
            delta = jnp.where(lane == h, dh, delta)
        delta_ref[...] = delta

        gz = gz_ref[...]
        dmg = dm_ref[:, WIDTH:2 * WIDTH]
        gate = gz * _sigmoid(gz)
        sgrad = _silu_grad(gz)
        gg = g_ref[...]
        dg_acc = jnp.zeros((1, HEAD_DIM), f32)
        for h in range(HEADS):
            sl = slice(h * HEAD_DIM, (h + 1) * HEAD_DIM)
            o = go_ref[:, sl]
            r = lax.rsqrt(jnp.mean(o * o, axis=-1, keepdims=True) + EPS)
            on = o * r
            dmh = dmg[:, sl]
            dgz_ref[:, sl] = (dmh * (on * gg) * sgrad[:, sl]).astype(bf16)
            dy = dmh * gate[:, sl]
            dg_acc = dg_acc + jnp.sum(dy * on, axis=0, keepdims=True)
            tt = dy * gg
            dgo_ref[:, sl] = r * (tt - on * jnp.mean(tt * on, axis=-1, keepdims=True))
        dg_ref[...] += dg_acc

    row = pl.BlockSpec((tm, WIDTH), lambda i: (i, 0))
    wide_bf = jax.ShapeDtypeStruct((s_len, WIDTH), bf16)
    return pl.pallas_call(
        body, name="mix_bwd", grid=(s_len // tm,),
        in_specs=[pl.BlockSpec((tm, 2 * WIDTH), lambda i: (i, 0)), row, row,
                  pl.BlockSpec((tm, WIDTH), lambda i: (i, 3)), pl.BlockSpec((tm, WIDTH), lambda i: (i, 7)),
                  pl.BlockSpec((1, LANES), lambda i: (0, 0))],
        out_specs=[row, pl.BlockSpec((tm, LANES), lambda i: (i, 0)), row, row, row,
                   pl.BlockSpec((1, LANES), lambda i: (0, 0))],
        out_shape=[wide_bf, jax.ShapeDtypeStruct((s_len, LANES), f32), wide_bf, wide_bf,
                   jax.ShapeDtypeStruct((s_len, WIDTH), f32), jax.ShapeDtypeStruct((1, LANES), f32)],
        compiler_params=_params("arbitrary"),
    )(dmixed, fox_o, gdn_o, p_main, p_main, gnorm_g)


def _out_head(mixed, w_out, x, target, gate, final_g):
    s_len = x.shape[0]
    tm = 256

    def body(mx_ref, w_ref, x_ref, t_ref, gate_ref, fg_ref, loss_ref, dy_ref, dz_ref, dm_ref, dfg_ref, dgate_ref):
        @pl.when(pl.program_id(0) == 0)
        def _():
            loss_ref[...] = jnp.zeros_like(loss_ref)
            dfg_ref[...] = jnp.zeros_like(dfg_ref)
            dgate_ref[...] = jnp.zeros_like(dgate_ref)

        w = w_ref[...]
        z = jnp.dot(mx_ref[...], w, preferred_element_type=f32)
        gate_v, fg = gate_ref[...], fg_ref[...]
        y1 = x_ref[...] + gate_v * z
        r = lax.rsqrt(jnp.mean(y1 * y1, axis=-1, keepdims=True) + EPS)
        yn = y1 * r
        err = yn * fg - t_ref[...]
        loss_ref[...] += 0.5 * jnp.sum(jnp.mean(err * err, axis=-1, keepdims=True))
        dout = err * (1.0 / D_MODEL)
        dfg_ref[...] += jnp.sum(dout * yn, axis=0, keepdims=True)
        tt = dout * fg
        dy1 = r * (tt - yn * jnp.mean(tt * yn, axis=-1, keepdims=True))
        dy_ref[...] = dy1
        dgate_ref[...] += jnp.sum(dy1 * z, axis=0, keepdims=True)
        dz = (dy1 * gate_v).astype(bf16)
        dz_ref[...] = dz
        dm_ref[...] = _dg(dz, w, 1, 1)

    row = pl.BlockSpec((tm, D_MODEL), lambda i: (i, 0))
    vec = pl.BlockSpec((1, D_MODEL), lambda i: (0, 0))
    big = jax.ShapeDtypeStruct((s_len, D_MODEL), f32)
    return pl.pallas_call(
        body, name="out_head", grid=(s_len // tm,),
        in_specs=[row, pl.BlockSpec((D_MODEL, D_MODEL), lambda i: (0, 0)), row, row, vec, vec],
        out_specs=[pl.BlockSpec((1, LANES), lambda i: (0, 0)), row, row, row, vec, vec],
        out_shape=[jax.ShapeDtypeStruct((1, LANES), f32), big, jax.ShapeDtypeStruct((s_len, D_MODEL), bf16), big,
                   jax.ShapeDtypeStruct((1, D_MODEL), f32), jax.ShapeDtypeStruct((1, D_MODEL), f32)],
        compiler_params=_params("arbitrary"),
    )(mixed, w_out, x, target, gate, final_g)


def _matmul_tn(name, a, b, out_dtype):
    k_len, m_len = a.shape
    n_len = b.shape[1]
    tk, tm, tn = min(2048, k_len), min(1024, m_len), min(1024, n_len)
    nk = k_len // tk

    def body(a_ref, b_ref, o_ref, acc_sc):
        k = pl.program_id(2)

        @pl.when(k == 0)
        def _():
            acc_sc[...] = jnp.zeros_like(acc_sc)

        acc_sc[...] += _dg(a_ref[...], b_ref[...], 0, 0)

        @pl.when(k == nk - 1)
        def _():
            o_ref[...] = acc_sc[...].astype(out_dtype)

    return pl.pallas_call(
        body, name=name, grid=(m_len // tm, n_len // tn, nk),
        in_specs=[pl.BlockSpec((tk, tm), lambda i, j, k: (k, i)), pl.BlockSpec((tk, tn), lambda i, j, k: (k, j))],
        out_specs=pl.BlockSpec((tm, tn), lambda i, j, k: (i, j)),
        out_shape=jax.ShapeDtypeStruct((m_len, n_len), out_dtype),
        scratch_shapes=[pltpu.VMEM((tm, tn), f32)],
        compiler_params=_params("parallel", "parallel", "arbitrary"),
    )(a, b)


def _post1(p_main, p_small, qn_g, kn_g, conv_w, bvec, alog, dqs, dkn, dgq, dgk, dgv, d_small, df, df_query):
    s_len = p_main.shape[0]
    tm = PREP_TM
    nb = s_len // tm

    def body(fq_ref, fk_ref, gq_ref, gk_ref, gv_ref, hq_ref, hk_ref, hv_ref, ps_ref, qg_ref, kg_ref, cw_ref, bv_ref,
             al_ref, dqs_ref, dkn_ref, dgq_ref, dgk_ref, dgv_ref, dsm_ref, df_ref, dfq_in_ref,
             dfq_ref, dfk_ref, dx_ref, dps_ref, dqg_ref, dkg_ref, sums_ref, dw_ref, xe_sc, carry_sc, dc_sc, next_sc):
        step = pl.program_id(0)
        blk = nb - 1 - step

        @pl.when(step == 0)
        def _():
            carry_sc[...] = jnp.zeros_like(carry_sc)
            next_sc[...] = jnp.zeros_like(next_sc)
            dqg_ref[...] = jnp.zeros_like(dqg_ref)
            dkg_ref[...] = jnp.zeros_like(dkg_ref)
            sums_ref[...] = jnp.zeros_like(sums_ref)
            dw_ref[...] = jnp.zeros_like(dw_ref)

        for x_ref, g_ref, dy_ref, o_ref, acc_ref, mul in ((fq_ref, qg_ref, dqs_ref, dfq_ref, dqg_ref, QK_SCALE),
                                                          (fk_ref, kg_ref, dkn_ref, dfk_ref, dkg_ref, LN2)):
            gain = g_ref[...]
            acc = jnp.zeros((1, HEAD_DIM), f32)
            for h in range(HEADS):
                sl = slice(h * HEAD_DIM, (h + 1) * HEAD_DIM)
                xv = x_ref[:, sl]
                r = lax.rsqrt(jnp.mean(xv * xv, axis=-1, keepdims=True) + EPS)
                xn = xv * r
                dy = dy_ref[:, sl] * mul
                acc = acc + jnp.sum(dy * xn, axis=0, keepdims=True)
                tt = dy * gain
                o_ref[:, sl] = (r * (tt - xn * jnp.mean(tt * xn, axis=-1, keepdims=True))).astype(bf16)
            acc_ref[...] += acc

        first = blk == 0
        for sec, (x_ref, halo_ref, dy_ref) in enumerate(((gq_ref, hq_ref, dgq_ref), (gk_ref, hk_ref, dgk_ref),
                                                         (gv_ref, hv_ref, dgv_ref))):
            cols = slice(sec * WIDTH, (sec + 1) * WIDTH)
            xe_sc[0:HALO, :] = jnp.where(first, 0.0, halo_ref[...])
            xe_sc[HALO:, :] = x_ref[...]
            cv = _conv_section(xe_sc, cw_ref, cols, tm)
            sgrad = _silu_grad(cv)
            if sec == 2:
                dc_sc[0:tm, :] = dy_ref[...] * sgrad
            else:
                y = cv * _sigmoid(cv)
                mul = QK_SCALE if sec == 0 else 1.0
                for h in range(HEADS):
                    sl = slice(h * HEAD_DIM, (h + 1) * HEAD_DIM)
                    yh = y[:, sl]
                    r = lax.rsqrt(jnp.sum(yh * yh, axis=-1, keepdims=True) + EPS)
                    dqh = dy_ref[:, sl]
                    dyh = (mul * r) * (dqh - yh * (r * r) * jnp.sum(dqh * yh, axis=-1, keepdims=True))
                    dc_sc[0:tm, sl] = dyh * sgrad[:, sl]
            dc_sc[tm:, :] = next_sc[sec]
            x_rows = xe_sc[pl.ds(HALO, tm), :]
            dx = jnp.zeros((tm, WIDTH), f32)
            dw = jnp.zeros((8, WIDTH), f32)
            tap_row = _iota((8, WIDTH), 0)
            for tap in range(CONV_K):
                ahead = dc_sc[pl.ds(CONV_K - 1 - tap, tm), :]
                dx = dx + cw_ref[pl.ds(tap, 1), cols] * ahead
                dw = jnp.where(tap_row == tap, jnp.sum(x_rows * ahead, axis=0, keepdims=True), dw)
            dx_ref[:, cols] = dx.astype(bf16)
            dw_ref[:, cols] += dw
            next_sc[sec] = dc_sc[0:HALO, :]

        lane = _iota((tm, N_SMALL), 1)
        z, _, gval, beta = _small_fwd(ps_ref[...], bv_ref[...], al_ref[...])
        sig_z = _sigmoid(z)
        dsm = dsm_ref[...]
        in_g = (lane >= LANE_G) & (lane < LANE_G + HEADS)
        dgc = jnp.where(in_g, pltpu.roll(dsm, N_SMALL - (LANE_GC - LANE_G), 1), 0.0)
        dgl = jnp.where(in_g, pltpu.roll(dsm, N_SMALL - (LANE_GLAST - LANE_G), 1), 0.0)
        tri_c, ones_c = _chunk_masks(tm)
        dg = (_dg(tri_c, dgc, 0, 0, HI) + jnp.dot(ones_c, dgl, preferred_element_type=f32, precision=HI))
        dbeta = dsm
        dfb = jnp.where(lane < HEADS, df_ref[...], 0.0)
        for h in range(HEADS):
            dfb = dfb + jnp.where(lane == h, dfq_in_ref[h], 0.0)
        tri_u = (_iota((tm, tm), 1) >= _iota((tm, tm), 0)).astype(f32)
        dlogf = jnp.dot(tri_u, dfb, preferred_element_type=f32, precision=HI) + carry_sc[...]
        carry_sc[...] += jnp.sum(dfb, axis=0, keepdims=True)
        dff = dlogf * (1.0 - sig_z)
        dga = dg * (-jnp.exp(al_ref[...])) * sig_z
        dgb_small = dbeta * beta * (1.0 - beta)
        dps = jnp.where(lane < HEADS, dff, jnp.where(lane < 2 * HEADS, dga, jnp.where(lane < 3 * HEADS, dgb_small, 0.0)))
        dps_ref[...] = dps.astype(bf16)
        row = _iota((8, N_SMALL), 0)
        s0 = jnp.sum(dps, axis=0, keepdims=True)
        s1 = jnp.sum(jnp.where((lane >= HEADS) & (lane < 2 * HEADS), dg * gval, 0.0), axis=0, keepdims=True)
        sums_ref[...] += jnp.where(row == 0, s0, jnp.where(row == 1, s1, 0.0))

    def col(cb):
        return pl.BlockSpec((tm, WIDTH), lambda i: (nb - 1 - i, cb))

    def halo(cb):
        return pl.BlockSpec((HALO, WIDTH), lambda i: (jnp.maximum((nb - 1 - i) * (tm // HALO) - 1, 0), cb))

    vec = pl.BlockSpec((1, LANES), lambda i: (0, 0))
    row0 = pl.BlockSpec((tm, WIDTH), lambda i: (nb - 1 - i, 0))
    small = pl.BlockSpec((tm, N_SMALL), lambda i: (nb - 1 - i, 0))
    wide_bf = jax.ShapeDtypeStruct((s_len, WIDTH), bf16)
    return pl.pallas_call(
        body, name="post1", grid=(nb,),
        in_specs=[col(0), col(1), col(4), col(5), col(6), halo(4), halo(5), halo(6), small, vec, vec,
                  pl.BlockSpec((CONV_K, 3 * WIDTH), lambda i: (0, 0)), vec, vec,
                  row0, row0, row0, row0, row0, small, small,
                  pl.BlockSpec((HEADS, tm, 1), lambda i: (0, nb - 1 - i, 0))],
        out_specs=[row0, row0, pl.BlockSpec((tm, 3 * WIDTH), lambda i: (nb - 1 - i, 0)), small, vec, vec,
                   pl.BlockSpec((8, N_SMALL), lambda i: (0, 0)), pl.BlockSpec((8, 3 * WIDTH), lambda i: (0, 0))],
        out_shape=[wide_bf, wide_bf, jax.ShapeDtypeStruct((s_len, 3 * WIDTH), bf16),
                   jax.ShapeDtypeStruct((s_len, N_SMALL), bf16), jax.ShapeDtypeStruct((1, LANES), f32),
                   jax.ShapeDtypeStruct((1, LANES), f32), jax.ShapeDtypeStruct((8, N_SMALL), f32),
                   jax.ShapeDtypeStruct((8, 3 * WIDTH), f32)],
        scratch_shapes=[pltpu.VMEM((tm + HALO, WIDTH), f32), pltpu.VMEM((1, N_SMALL), f32),
                        pltpu.VMEM((tm + HALO, WIDTH), f32), pltpu.VMEM((3, HALO, WIDTH), f32)],
        compiler_params=_params("arbitrary"),
    )(p_main, p_main, p_main, p_main, p_main, p_main, p_main, p_main, p_small, qn_g, kn_g, conv_w, bvec, alog,
      dqs, dkn, dgq, dgk, dgv, d_small, df, df_query)


def _in_proj_bwd(dp_pieces, dp_small, wt_main, wt_small):
    s_len = dp_small.shape[0]
    tm, tk = min(1024, s_len), WIDTH
    nk = N_MAIN // tk
    first_section = [sum(p.shape[1] // tk for p in dp_pieces[:n]) for n in range(len(dp_pieces))]
    n_pieces = len(dp_pieces)

    def body(*refs):
        piece_refs = refs[:n_pieces]
        dps_ref, w_ref, ws_ref, dh_ref = refs[n_pieces:]
        k = pl.program_id(1)

        @pl.when(k == 0)
        def _():
            dh_ref[...] = jnp.dot(dps_ref[...], ws_ref[...], preferred_element_type=f32)

        for piece, ref, first in zip(dp_pieces, piece_refs, first_section):
            @pl.when((k >= first) & (k < first + piece.shape[1] // tk))
            def _(ref=ref):
                dh_ref[...] += jnp.dot(ref[...], w_ref[...], preferred_element_type=f32)

    def piece_spec(piece, first):
        last = piece.shape[1] // tk - 1
        return pl.BlockSpec((tm, tk), lambda i, k: (i, jnp.clip(k - first, 0, last)))

    return pl.pallas_call(
        body, name="in_proj_bwd", grid=(s_len // tm, nk),
        in_specs=[piece_spec(p, f) for p, f in zip(dp_pieces, first_section)]
                 + [pl.BlockSpec((tm, N_SMALL), lambda i, k: (i, 0)),
                    pl.BlockSpec((tk, D_MODEL), lambda i, k: (k, 0)), pl.BlockSpec((N_SMALL, D_MODEL), lambda i, k: (0, 0))],
        out_specs=pl.BlockSpec((tm, D_MODEL), lambda i, k: (i, 0)),
        out_shape=jax.ShapeDtypeStruct((s_len, D_MODEL), f32),
        compiler_params=_params("parallel", "arbitrary"),
    )(*dp_pieces, dp_small, wt_main, wt_small)


def _adaln_bwd(dh, x, dy1, norm_g, scale1p):
    s_len = x.shape[0]
    tm = 256

    def body(dh_ref, x_ref, dy_ref, g_ref, sc_ref, dx_ref, dsh_ref, dsc_ref, dg_ref):
        @pl.when(pl.program_id(0) == 0)
        def _():
            dsh_ref[...] = jnp.zeros_like(dsh_ref)
            dsc_ref[...] = jnp.zeros_like(dsc_ref)
            dg_ref[...] = jnp.zeros_like(dg_ref)

        dh = dh_ref[...]
        xb = x_ref[...]
        r = lax.rsqrt(jnp.mean(xb * xb, axis=-1, keepdims=True) + EPS)
        xr = xb * r
        gain = g_ref[...]
        dsh_ref[...] += jnp.sum(dh, axis=0, keepdims=True)
        dsc_ref[...] += jnp.sum(dh * (xr * gain), axis=0, keepdims=True)
        dxn = dh * sc_ref[...]
        dg_ref[...] += jnp.sum(dxn * xr, axis=0, keepdims=True)
        tt = dxn * gain
        dx_ref[...] = r * (tt - xr * jnp.mean(tt * xr, axis=-1, keepdims=True)) + dy_ref[...]

    row = pl.BlockSpec((tm, D_MODEL), lambda i: (i, 0))
    vec = pl.BlockSpec((1, D_MODEL), lambda i: (0, 0))
    vshape = jax.ShapeDtypeStruct((1, D_MODEL), f32)
    return pl.pallas_call(
        body, name="adaln_bwd", grid=(s_len // tm,),
        in_specs=[row, row, row, vec, vec], out_specs=[row, vec, vec, vec],
        out_shape=[jax.ShapeDtypeStruct((s_len, D_MODEL), f32), vshape, vshape, vshape],
        compiler_params=_params("arbitrary"),
    )(dh, x, dy1, norm_g, scale1p)


def _adamw(name, w, g_stack, m, v, tr, tc=None):
    n_stack, rows, cols = g_stack.shape
    tc = cols if tc is None else tc

    def body(w_ref, g_ref, m_ref, v_ref, go_ref, d_ref, mo_ref, vo_ref):
        g = g_ref[0].astype(f32)
        for k in range(1, n_stack):
            g = g + g_ref[k].astype(f32)
        go_ref[0] = g
        m_new = ADAM_B1 * m_ref[0] + (1.0 - ADAM_B1) * g
        v_new = ADAM_B2 * v_ref[0] + (1.0 - ADAM_B2) * (g * g)
        mo_ref[0] = m_new
        vo_ref[0] = v_new
        m_hat = m_new / (1.0 - ADAM_B1 ** ADAM_STEP)
        v_hat = v_new / (1.0 - ADAM_B2 ** ADAM_STEP)
        d_ref[0] = -ADAM_LR * (m_hat / (jnp.sqrt(v_hat) + ADAM_EPS) + ADAM_WD * w_ref[0])

    blk = pl.BlockSpec((1, tr, tc), lambda i, j: (0, i, j))
    shape = jax.ShapeDtypeStruct((1, rows, cols), f32)
    return pl.pallas_call(
        body, name=name, grid=(rows // tr, cols // tc),
        in_specs=[blk, pl.BlockSpec((n_stack, tr, tc), lambda i, j: (0, i, j)), blk, blk],
        out_specs=[blk] * 4, out_shape=[shape] * 4,
        compiler_params=_params("parallel", "parallel"),
    )(w, g_stack, m, v)


def _w_ada_grad(c_all_t, dmod_pad):
    def body(c_ref, d_ref, o_ref):
        cv = c_ref[...]
        o_ref[...] = jnp.dot(cv * _sigmoid(cv), d_ref[...], preferred_element_type=f32, precision=HI)

    return pl.pallas_call(body, name="w_ada_grad",
                          out_shape=jax.ShapeDtypeStruct((c_all_t.shape[0], dmod_pad.shape[1]), f32),
                          compiler_params=_params())(c_all_t, dmod_pad)


SMALL_NAMES = ("norm_g", "b_ada", "b_fgate", "fox_qn_g", "fox_kn_g", "gdn_A_log", "gdn_dt_bias", "gdn_norm_g", "final_g")
SMALL_SIZES = (D_MODEL, 3 * D_MODEL, HEADS, HEAD_DIM, HEAD_DIM, HEADS, HEADS, HEAD_DIM, D_MODEL)
SMALL_PACK = 10752


def _pack(vectors, total):
    flat = jnp.concatenate([t.reshape(-1) for t in vectors])
    return jnp.pad(flat, (0, total - flat.shape[0])).reshape(1, total)


def _lanes(*pieces):
    parts, at = [], 0
    for off, vec in pieces:
        flat = vec.reshape(-1).astype(f32)
        parts += [jnp.zeros((off - at,), f32), flat]
        at = off + flat.shape[0]
    parts.append(jnp.zeros((LANES - at,), f32))
    return jnp.concatenate(parts).reshape(1, LANES)


def kernel(x, c, norm_g, w_ada, b_ada, w_in, b_fgate, fox_qn_g, fox_kn_g, gdn_conv_w, gdn_A_log, gdn_dt_bias, gdn_norm_g, w_out, final_g, loss_target, m_norm_g, m_w_ada, m_b_ada, m_w_in, m_b_fgate, m_fox_qn_g, m_fox_kn_g, m_gdn_conv_w, m_gdn_A_log, m_gdn_dt_bias, m_gdn_norm_g, m_w_out, m_final_g, v_norm_g, v_w_ada, v_b_ada, v_w_in, v_b_fgate, v_fox_qn_g, v_fox_kn_g, v_gdn_conv_w, v_gdn_A_log, v_gdn_dt_bias, v_gdn_norm_g, v_w_out, v_final_g):
    me = _my_index()
    s_len = x.shape[1]
    nq = s_len // FOX_T
    x2 = x.reshape(s_len, D_MODEL)
    tgt = loss_target.reshape(s_len, D_MODEL)
    ada_cols = w_ada.shape[2]
    in_cols = w_in.shape[2]
    conv_cols = gdn_conv_w.shape[2]

    (c_all,) = _gather_direct("gather_c", [c])
    c_all = c_all.reshape(N_DEV, D_MODEL)
    b_shard = lax.dynamic_slice(b_ada, (0, me * ada_cols), (1, ada_cols))
    mod_mine = _mod_shard(c_all, w_ada[0], b_shard)
    wt_shard = jnp.transpose(w_in[0])
    mod_all, wt_all, w_out_all, conv_all = _gather_two_level(
        "gather_weights", [mod_mine, wt_shard.astype(bf16), w_out[0].astype(bf16), gdn_conv_w[0]])
    mod = lax.dynamic_slice(mod_all, (0, me, 0), (N_DEV, 1, ada_cols)).reshape(1, 3 * D_MODEL)
    shift, scale, gate = mod[:, :D_MODEL], mod[:, D_MODEL:2 * D_MODEL], mod[:, 2 * D_MODEL:]
    scale1p = 1.0 + scale
    wt_full = wt_all.reshape(N_DEV * in_cols, D_MODEL)
    g0 = 4 * WIDTH + HEADS
    w_main = jnp.concatenate([wt_full[:4 * WIDTH], wt_full[g0:g0 + 4 * WIDTH]], axis=0)
    w_small = jnp.concatenate([wt_full[4 * WIDTH:g0], wt_full[g0 + 4 * WIDTH:],
                               jnp.zeros((N_SMALL - 3 * HEADS, D_MODEL), bf16)], axis=0)
    w_out_full = w_out_all.reshape(2 * WIDTH, D_MODEL)
    conv_full = jnp.transpose(conv_all, (1, 0, 2)).reshape(CONV_K, 3 * WIDTH)

    qn_g, kn_g, gn_g = fox_qn_g.reshape(1, LANES), fox_kn_g.reshape(1, LANES), gdn_norm_g.reshape(1, LANES)
    bvec = _lanes((0, b_fgate), (HEADS, gdn_dt_bias))
    alog = _lanes((HEADS, gdn_A_log))
    fg = final_g.reshape(1, D_MODEL)

    h_bf = _norm_mod(x2, norm_g, scale1p, shift)
    p_main, p_small = _in_proj(h_bf, w_main, w_small)
    qs, kn, vb, gq, gk, gv, small = _prep(p_main, p_small, qn_g, kn_g, conv_full, bvec, alog)
    fox_o, lse = _fox_fwd(qs, kn, vb)
    gu, gw, gqg, gkd, gattn, t_inv, eg_last = _gdn_intra_fwd(gq, gk, gv, small)
    gdn_o, states = _gdn_scan_fwd(gu, gw, gqg, gkd, gattn, eg_last)
    mixed = _mix_fwd(fox_o, gdn_o, p_main, gn_g)

    loss_row, dy1, dz, dmixed, d_final_g, d_gate = _out_head(mixed, w_out_full, x2, tgt, gate, fg)
    loss = lax.psum(loss_row[0, 0], AXES)
    dw_out = _matmul_tn("dw_out", mixed, dz, bf16)
    do_fox, delta, dfz, dgz, dgdn_o, d_gn_g = _mix_bwd(dmixed, fox_o, gdn_o, p_main, gn_g)
    dqs, dkn, dvf, df_key, df_query = _fox_bwd(qs, kn, vb, do_fox, lse, delta)
    du, dw, dqg, dkd, dattn, deg = _gdn_scan_bwd(gu, gw, gqg, gkd, gattn, eg_last, states, dgdn_o)
    dgq, dgk, dgv, d_small = _gdn_intra_bwd(gq, gk, gv, small, t_inv, du, dw, dqg, dkd, dattn, deg)
    df_small = jnp.pad(jnp.transpose(df_key.reshape(HEADS, s_len)), ((0, 0), (0, N_SMALL - HEADS)))
    dfq, dfk, dgqkv, dp_small, d_qn_g, d_kn_g, sums, d_conv = _post1(
        p_main, p_small, qn_g, kn_g, conv_full, bvec, alog, dqs, dkn, dgq, dgk, dgv, d_small, df_small, df_query)
    dp_pieces = [dfq, dfk, dvf, dfz, dgqkv, dgz]
    dh = _in_proj_bwd(dp_pieces, dp_small, w_main, w_small)
    grad_x, d_shift, d_scale, d_norm_g = _adaln_bwd(dh, x2, dy1, norm_g, scale1p)
    dw_rows = [_matmul_tn("dw_main_%d" % n, piece, h_bf, bf16) for n, piece in enumerate(dp_pieces)]
    dw_small = _matmul_tn("dw_small", dp_small, h_bf, bf16)
    dw_in_full = jnp.concatenate(dw_rows[:4] + [dw_small[:HEADS]] + dw_rows[4:] + [dw_small[HEADS:3 * HEADS]],
                                 axis=0)
    dw_in_parts = dw_in_full.reshape(N_DEV, in_cols, D_MODEL)
    dw_out_parts = dw_out.reshape(N_DEV, w_out.shape[1], D_MODEL)

    dmod = jnp.concatenate([d_shift, d_scale, d_gate], axis=1)
    small_grads = _pack([d_norm_g, dmod, sums[0, :HEADS], d_qn_g, d_kn_g, sums[1, HEADS:2 * HEADS],
                         sums[0, HEADS:2 * HEADS], d_gn_g, d_final_g], SMALL_PACK)
    conv_grad = d_conv[:CONV_K]
    pair_in, pair_out = _pair_exchange("pair_grads", [dw_in_parts, dw_out_parts])
    core = lax.axis_index("c").astype(jnp.int32).reshape(1)
    dw_in_recv, dw_out_recv = _chip_exchange(
        "chip_grads", [_pair_sum("pair_sum_w_in", dw_in_parts, pair_in, core),
                       _pair_sum("pair_sum_w_out", dw_out_parts, pair_out, core)])
    small_all, conv_all_g = _gather_direct("gather_small_grads", [small_grads, conv_grad])

    outs = {}
    to_t = lambda t: jnp.transpose(t, (0, 2, 1))
    outs["w_in"] = tuple(to_t(t) for t in _adamw("adamw_w_in", to_t(w_in), dw_in_recv, to_t(m_w_in), to_t(v_w_in),
                                                  in_cols, 256))
    outs["w_out"] = _adamw("adamw_w_out", w_out, dw_out_recv, m_w_out, v_w_out, 128)
    conv_mine = lax.dynamic_slice(jnp.transpose(conv_all_g.reshape(N_DEV, CONV_K, N_DEV, conv_cols), (0, 2, 1, 3)),
                                  (0, me, 0, 0), (N_DEV, 1, CONV_K, conv_cols)).reshape(N_DEV, CONV_K, conv_cols)
    outs["gdn_conv_w"] = _adamw("adamw_conv", gdn_conv_w, conv_mine, m_gdn_conv_w, v_gdn_conv_w, CONV_K)
    small_all = small_all.reshape(N_DEV, 1, SMALL_PACK)
    dmod_all = small_all[:, 0, D_MODEL:D_MODEL + 3 * D_MODEL]
    dmod_mine = lax.dynamic_slice(dmod_all, (0, me * ada_cols), (N_DEV, ada_cols))
    c_all_t = jnp.pad(jnp.transpose(c_all), ((0, 0), (0, LANES - N_DEV)))
    g_w_ada = _w_ada_grad(c_all_t, jnp.pad(dmod_mine, ((0, LANES - N_DEV), (0, 0))))
    outs["w_ada"] = _adamw("adamw_w_ada", w_ada, g_w_ada[None], m_w_ada, v_w_ada, 256)
    given = dict(norm_g=(norm_g, m_norm_g, v_norm_g), b_ada=(b_ada, m_b_ada, v_b_ada), b_fgate=(b_fgate, m_b_fgate, v_b_fgate),
                 fox_qn_g=(fox_qn_g, m_fox_qn_g, v_fox_qn_g), fox_kn_g=(fox_kn_g, m_fox_kn_g, v_fox_kn_g),
                 gdn_A_log=(gdn_A_log, m_gdn_A_log, v_gdn_A_log), gdn_dt_bias=(gdn_dt_bias, m_gdn_dt_bias, v_gdn_dt_bias),
                 gdn_norm_g=(gdn_norm_g, m_gdn_norm_g, v_gdn_norm_g), final_g=(final_g, m_final_g, v_final_g))
    w_pack = _pack([given[n][0] for n in SMALL_NAMES], SMALL_PACK)
    m_pack = _pack([given[n][1] for n in SMALL_NAMES], SMALL_PACK)
    v_pack = _pack([given[n][2] for n in SMALL_NAMES], SMALL_PACK)
    packed = _adamw("adamw_small", w_pack[None], small_all, m_pack[None], v_pack[None], 1)
    off = 0
    for n, size in zip(SMALL_NAMES, SMALL_SIZES):
        outs[n] = tuple(t[0, 0, off:off + size].reshape(given[n][0].shape) for t in packed)
        off += size

    order = ("norm_g", "w_ada", "b_ada", "w_in", "b_fgate", "fox_qn_g", "fox_kn_g", "gdn_conv_w", "gdn_A_log",
             "gdn_dt_bias", "gdn_norm_g", "w_out", "final_g")
    result = [loss, grad_x.reshape(x.shape)]
    for part in range(4):
        result += [outs[n][part] for n in order]
    return tuple(result)
```

```python
import math

import jax
import jax.numpy as jnp
from jax import lax
from jax.experimental import pallas as pl
from jax.experimental.pallas import tpu as pltpu

f32 = jnp.float32
bf16 = jnp.bfloat16
HI = lax.Precision.HIGHEST

N_DEV = 8
AXES = ("x", "y", "c")
D_MODEL = 2048
HEADS = 8
HEAD_DIM = 128
WIDTH = HEADS * HEAD_DIM
CHUNK = 64
CONV_K = 4
EPS = 1e-6
QK_SCALE = HEAD_DIM ** -0.5
LOG2E = 1.0 / math.log(2.0)
LN2 = math.log(2.0)
N_MAIN = 8 * WIDTH
N_SMALL = 128
LANE_F, LANE_G, LANE_BETA, LANE_GC, LANE_GLAST = 0, 8, 16, 24, 32
IN_WIDTH = 8 * WIDTH + 3 * HEADS
LANES = 128
VMEM_LIMIT = 56 * 1024 * 1024

ADAM_LR, ADAM_B1, ADAM_B2, ADAM_EPS, ADAM_WD, ADAM_STEP = 0.001, 0.9, 0.999, 1e-08, 0.01, 10


def _params(*sem):
    return pltpu.CompilerParams(dimension_semantics=sem, vmem_limit_bytes=VMEM_LIMIT)


def _iota(shape, dim):
    return lax.broadcasted_iota(jnp.int32, shape, dim)


def _sigmoid(z):
    return 1.0 / (1.0 + jnp.exp(-z))


def _softplus_parts(z):
    t = jnp.log(1.0 + jnp.exp(-jnp.abs(z)))
    return jnp.minimum(z, 0.0) - t, jnp.maximum(z, 0.0) + t


def _dg(a, b, ca, cb, prec=None):
    if a.ndim == 3:
        dims = (((ca + 1,), (cb + 1,)), ((0,), (0,)))
    else:
        dims = (((ca,), (cb,)), ((), ()))
    return lax.dot_general(a, b, dims, preferred_element_type=f32, precision=prec)


def _dot_bf16(a, b, ca, cb):
    return _dg(a.astype(bf16), b.astype(bf16), ca, cb)


def _split_bf16(a):
    hi = a.astype(bf16)
    return hi, (a - hi.astype(f32)).astype(bf16)


def _dot_3pass(a, b, ca, cb):
    a_hi, a_lo = _split_bf16(a)
    b_hi, b_lo = _split_bf16(b)
    return _dg(a_hi, b_hi, ca, cb) + (_dg(a_hi, b_lo, ca, cb) + _dg(a_lo, b_hi, ca, cb))


def _make_mm(dot):
    def nn_(a, b):
        return dot(a, b, 1, 0)

    def nt_(a, b):
        return dot(a, b, 1, 1)

    def tn_(a, b):
        return dot(a, b, 0, 0)

    @jax.custom_vjp
    def nn(a, b):
        return nn_(a, b)

    @jax.custom_vjp
    def nt(a, b):
        return nt_(a, b)

    @jax.custom_vjp
    def tn(a, b):
        return tn_(a, b)

    nn.defvjp(lambda a, b: (nn_(a, b), (a, b)), lambda r, g: (nt_(g, r[1]), tn_(r[0], g)))
    nt.defvjp(lambda a, b: (nt_(a, b), (a, b)), lambda r, g: (nn_(g, r[1]), tn_(g, r[0])))
    tn.defvjp(lambda a, b: (tn_(a, b), (a, b)), lambda r, g: (nt_(r[1], g), nn_(r[0], g)))
    return (nn_, nt_, tn_), (nn, nt, tn)


_BF_PLAIN, _BF_VJP = _make_mm(_dot_bf16)
_X3_PLAIN, _X3_VJP = _make_mm(_dot_3pass)


def _inv_unit_lower_many(ms):
    c = CHUNK
    nn = _X3_PLAIN[0]
    eye = (_iota((c, c), 0) == _iota((c, c), 1)).astype(f32)
    top = _iota((2 * c, c), 0) < c
    xs = [jnp.concatenate([eye - m, nn(m, m)], axis=0) for m in ms]
    for _ in range(int(math.log2(CHUNK)) - 2):
        xs = [jnp.where(top, x, 0.0) + nn(x, x[c:]) for x in xs]
    return [x[:c] + nn(x[:c], x[c:]) for x in xs]


@jax.custom_vjp
def _inv_given(m, t):
    return t


_inv_given.defvjp(lambda m, t: (t, t),
                  lambda t, g: (-_X3_PLAIN[1](_X3_PLAIN[2](t, g), t), jnp.zeros_like(t)))

SUBLANES = 8


def _gdn_intra_pre(q, k, v, gc_b, g_last_b, beta_b, bnt):
    c = CHUNK
    r_i, c_i = _iota((c, c), 0), _iota((c, c), 1)
    lower, strict = r_i >= c_i, r_i > c_i
    gc_i = gc_b[..., :c]
    gc_j = jnp.swapaxes(gc_i, -1, -2)
    decay = jnp.where(lower, jnp.exp(jnp.where(lower, gc_i - gc_j, 0.0)), 0.0)
    kb = k * beta_b
    both = bnt(jnp.concatenate([kb, q], axis=-2), k)
    m = jnp.where(strict, both[..., :c, :] * decay, 0.0)
    attn = jnp.where(lower, both[..., c:, :] * decay, 0.0)
    eg = jnp.exp(gc_b)
    rhs = jnp.concatenate([v * beta_b, kb * eg], axis=-1)
    k_dec = k * jnp.exp(g_last_b - gc_b)
    eg_last = jnp.exp(g_last_b[..., :SUBLANES, :])
    return m, rhs, q * eg, k_dec, attn, eg_last


def _gdn_intra(q, k, v, gc_b, g_last_b, beta_b, t_known):
    m, rhs, qg, k_dec, attn, eg_last = _gdn_intra_pre(q, k, v, gc_b, g_last_b, beta_b, _BF_VJP[1])
    return _X3_VJP[0](_inv_given(m, t_known), rhs), qg, k_dec, attn, eg_last


def _scale_rows(s, eg_last):
    return (s.reshape(HEAD_DIM // SUBLANES, SUBLANES, HEAD_DIM) * eg_last[None]).reshape(HEAD_DIM, HEAD_DIM)


def _my_index():
    return 4 * lax.axis_index("x") + 2 * lax.axis_index("y") + lax.axis_index("c")


def _peer(d):
    x, y, c = lax.axis_index("x"), lax.axis_index("y"), lax.axis_index("c")
    px, py, pc = (x + (d >> 2)) % 2, (y + ((d >> 1) & 1)) % 2, (c + (d & 1)) % 2
    return (px, py, pc), 4 * px + 2 * py + pc


def _gather_direct(name, arrays):
    n = len(arrays)

    def body(*refs):
        srcs, dsts = refs[:n], refs[n:2 * n]
        send_sems, recv_sems, local_sems = refs[2 * n:]
        me = _my_index()

        def copy(k, d, started):
            peer, pidx = _peer(d)
            return pltpu.make_async_remote_copy(
                src_ref=srcs[k], dst_ref=dsts[k].at[me if started else pidx], send_sem=send_sems.at[k * 7 + d - 1],
                recv_sem=recv_sems.at[k * 7 + d - 1], device_id=peer, device_id_type=pl.DeviceIdType.MESH)

        local = [pltpu.make_async_copy(srcs[k], dsts[k].at[me], local_sems.at[k]) for k in range(n)]
        sends = [copy(k, d, True) for k in range(n) for d in range(1, N_DEV)]
        for cp in local + sends:
            cp.start()
        for k in range(n):
            for d in range(1, N_DEV):
                copy(k, d, False).wait_recv()
        for cp in sends:
            cp.wait_send()
        for cp in local:
            cp.wait()

    out_shape = [jax.ShapeDtypeStruct((N_DEV,) + a.shape, a.dtype) for a in arrays]
    any_spec = pl.BlockSpec(memory_space=pl.ANY)
    return pl.pallas_call(
        body, name=name, out_shape=out_shape, in_specs=[any_spec] * n, out_specs=[any_spec] * n,
        scratch_shapes=[pltpu.SemaphoreType.DMA((7 * n,)), pltpu.SemaphoreType.DMA((7 * n,)),
                        pltpu.SemaphoreType.DMA((n,))],
        compiler_params=pltpu.CompilerParams(has_side_effects=True),
    )(*arrays)


N_CHIPS = 4


def _pair_exchange(name, arrays):
    n = len(arrays)

    def body(*refs):
        srcs, dsts = refs[:n], refs[n:2 * n]
        send_sems, recv_sems = refs[2 * n:]
        x, y, c = lax.axis_index("x"), lax.axis_index("y"), lax.axis_index("c")
        sibling = (x, y, 1 - c)

        def copy(k, j):
            return pltpu.make_async_remote_copy(
                src_ref=srcs[k].at[2 * j + (1 - c)], dst_ref=dsts[k].at[j], send_sem=send_sems.at[k * N_CHIPS + j],
                recv_sem=recv_sems.at[k * N_CHIPS + j], device_id=sibling, device_id_type=pl.DeviceIdType.MESH)

        copies = [copy(k, j) for k in range(n) for j in range(N_CHIPS)]
        for cp in copies:
            cp.start()
        for cp in copies:
            cp.wait_recv()
        for cp in copies:
            cp.wait_send()

    any_spec = pl.BlockSpec(memory_space=pl.ANY)
    return pl.pallas_call(
        body, name=name, out_shape=[jax.ShapeDtypeStruct((N_CHIPS,) + a.shape[1:], a.dtype) for a in arrays],
        in_specs=[any_spec] * n, out_specs=[any_spec] * n,
        scratch_shapes=[pltpu.SemaphoreType.DMA((N_CHIPS * n,)), pltpu.SemaphoreType.DMA((N_CHIPS * n,))],
        compiler_params=pltpu.CompilerParams(has_side_effects=True),
    )(*arrays)


def _chip_exchange(name, arrays):
    n = len(arrays)

    def body(*refs):
        srcs, dsts = refs[:n], refs[n:2 * n]
        send_sems, recv_sems, local_sems = refs[2 * n:]
        x, y, c = lax.axis_index("x"), lax.axis_index("y"), lax.axis_index("c")
        my_chip = 2 * x + y

        def peer(d):
            px, py = (x + (d >> 1)) % 2, (y + (d & 1)) % 2
            return (px, py, c), 2 * px + py

        def remote(k, d, started):
            to, chip = peer(d)
            return pltpu.make_async_remote_copy(
                src_ref=srcs[k].at[chip], dst_ref=dsts[k].at[my_chip if started else chip],
                send_sem=send_sems.at[k * 3 + d - 1], recv_sem=recv_sems.at[k * 3 + d - 1],
                device_id=to, device_id_type=pl.DeviceIdType.MESH)

        local = [pltpu.make_async_copy(srcs[k].at[my_chip], dsts[k].at[my_chip], local_sems.at[k]) for k in range(n)]
        sends = [remote(k, d, True) for k in range(n) for d in range(1, N_CHIPS)]
        for cp in local + sends:
            cp.start()
        for k in range(n):
            for d in range(1, N_CHIPS):
                remote(k, d, False).wait_recv()
        for cp in sends:
            cp.wait_send()
        for cp in local:
            cp.wait()

    any_spec = pl.BlockSpec(memory_space=pl.ANY)
    return pl.pallas_call(
        body, name=name, out_shape=[jax.ShapeDtypeStruct(a.shape, a.dtype) for a in arrays],
        in_specs=[any_spec] * n, out_specs=[any_spec] * n,
        scratch_shapes=[pltpu.SemaphoreType.DMA((3 * n,)), pltpu.SemaphoreType.DMA((3 * n,)),
                        pltpu.SemaphoreType.DMA((n,))],
        compiler_params=pltpu.CompilerParams(has_side_effects=True),
    )(*arrays)


def _pair_sum(name, parts, received, core):
    n_blocks, rows, cols = received.shape
    tr = rows if rows % 256 else 256

    def body(core_ref, mine_ref, recv_ref, o_ref):
        o_ref[...] = (mine_ref[...].astype(f32) + recv_ref[...].astype(f32)).astype(bf16)

    return pl.pallas_call(
        body, name=name,
        grid_spec=pltpu.PrefetchScalarGridSpec(
            num_scalar_prefetch=1, grid=(n_blocks, rows // tr),
            in_specs=[pl.BlockSpec((1, tr, cols), lambda j, i, core_ref: (2 * j + core_ref[0], i, 0)),
                      pl.BlockSpec((1, tr, cols), lambda j, i, core_ref: (j, i, 0))],
            out_specs=pl.BlockSpec((1, tr, cols), lambda j, i, core_ref: (j, i, 0))),
        out_shape=jax.ShapeDtypeStruct((n_blocks, rows, cols), bf16),
        compiler_params=_params("parallel", "parallel"),
    )(core, parts, received)


def _gather_two_level(name, arrays):
    n = len(arrays)

    def body(*refs):
        srcs, dsts = refs[:n], refs[n:2 * n]
        send_sems, recv_sems, local_sems = refs[2 * n:]
        x, y, c = lax.axis_index("x"), lax.axis_index("y"), lax.axis_index("c")
        sibling = (x, y, 1 - c)
        near = ((x + 1 - c) % 2, (y + c) % 2)
        far = ((x + c) % 2, (y + 1 - c) % 2)
        diag = ((x + 1) % 2, (y + 1) % 2)
        near_slot, far_slot = 1 + c, 2 - c

        def index(px, py, pc):
            return 4 * px + 2 * py + pc

        def copy(k, slot, block, to, src=None):
            return pltpu.make_async_remote_copy(
                src_ref=dsts[k].at[index(*block)] if src is None else src, dst_ref=dsts[k].at[index(*block)],
                send_sem=send_sems.at[k * 7 + slot], recv_sem=recv_sems.at[k * 7 + slot],
                device_id=to, device_id_type=pl.DeviceIdType.MESH)

        me = (x, y, c)
        local = [pltpu.make_async_copy(srcs[k], dsts[k].at[index(*me)], local_sems.at[k]) for k in range(n)]
        started = [copy(k, 0, me, sibling, src=srcs[k]) for k in range(n)]
        started += [copy(k, near_slot, me, (*near, c), src=srcs[k]) for k in range(n)]
        started += [copy(k, far_slot, me, (*far, c), src=srcs[k]) for k in range(n)]
        for cp in local + started:
            cp.start()
        for k in range(n):
            copy(k, near_slot, (*near, c), me).wait_recv()
            passed = [copy(k, 3, (*near, c), (*far, c)), copy(k, 3 + near_slot, (*near, c), sibling)]
            for cp in passed:
                cp.start()
            started += passed
        for slot, chip in ((far_slot, far), (3, diag)):
            for k in range(n):
                copy(k, slot, (*chip, c), me).wait_recv()
                passed = copy(k, 3 + slot, (*chip, c), sibling)
                passed.start()
                started.append(passed)
        for k in range(n):
            copy(k, 0, sibling, me).wait_recv()
            for slot, chip in ((near_slot, near), (far_slot, far), (3, diag)):
                copy(k, 3 + slot, (*chip, 1 - c), me).wait_recv()
        for cp in started:
            cp.wait_send()
        for cp in local:
            cp.wait()

    any_spec = pl.BlockSpec(memory_space=pl.ANY)
    return pl.pallas_call(
        body, name=name, out_shape=[jax.ShapeDtypeStruct((N_DEV,) + a.shape, a.dtype) for a in arrays],
        in_specs=[any_spec] * n, out_specs=[any_spec] * n,
        scratch_shapes=[pltpu.SemaphoreType.DMA((7 * n,)), pltpu.SemaphoreType.DMA((7 * n,)),
                        pltpu.SemaphoreType.DMA((n,))],
        compiler_params=pltpu.CompilerParams(has_side_effects=True),
    )(*arrays)


def _mod_shard(c_all, w_ada, b_shard):
    def body(c_ref, w_ref, b_ref, o_ref):
        cv = c_ref[...]
        ca = cv * _sigmoid(cv)
        o_ref[...] = jnp.dot(ca.astype(bf16), w_ref[...].astype(bf16), preferred_element_type=f32) + b_ref[...]

    return pl.pallas_call(body, name="mod_shard", out_shape=jax.ShapeDtypeStruct((N_DEV, w_ada.shape[1]), f32),
                          compiler_params=_params())(c_all, w_ada, b_shard)


def _norm_mod(x, norm_g, scale1p, shift):
    s_len = x.shape[0]
    tm = 512

    def body(x_ref, g_ref, sc_ref, sh_ref, h_ref):
        xb = x_ref[...]
        r = lax.rsqrt(jnp.mean(xb * xb, axis=-1, keepdims=True) + EPS)
        h_ref[...] = ((xb * r * g_ref[...]) * sc_ref[...] + sh_ref[...]).astype(bf16)

    row = pl.BlockSpec((tm, D_MODEL), lambda i: (i, 0))
    vec = pl.BlockSpec((1, D_MODEL), lambda i: (0, 0))
    return pl.pallas_call(body, name="norm_mod", grid=(s_len // tm,), in_specs=[row, vec, vec, vec], out_specs=row,
                          out_shape=jax.ShapeDtypeStruct((s_len, D_MODEL), bf16),
                          compiler_params=_params("parallel"))(x, norm_g, scale1p, shift)


def _in_proj(h, wt_main, wt_small):
    s_len = h.shape[0]
    tm, tn = min(1024, s_len), 1024

    def body(h_ref, w_ref, ws_ref, p_ref, ps_ref):
        @pl.when(pl.program_id(1) == 0)
        def _():
            ps_ref[...] = _dg(h_ref[...], ws_ref[...], 1, 1)

        p_ref[...] = _dg(h_ref[...], w_ref[...], 1, 1)

    return pl.pallas_call(
        body, name="in_proj", grid=(s_len // tm, N_MAIN // tn),
        in_specs=[pl.BlockSpec((tm, D_MODEL), lambda i, j: (i, 0)),
                  pl.BlockSpec((tn, D_MODEL), lambda i, j: (j, 0)),
                  pl.BlockSpec((N_SMALL, D_MODEL), lambda i, j: (0, 0))],
        out_specs=[pl.BlockSpec((tm, tn), lambda i, j: (i, j)),
                   pl.BlockSpec((tm, N_SMALL), lambda i, j: (i, 0))],
        out_shape=[jax.ShapeDtypeStruct((s_len, N_MAIN), f32), jax.ShapeDtypeStruct((s_len, N_SMALL), f32)],
        compiler_params=_params("parallel", "arbitrary"),
    )(h, wt_main, wt_small)


PREP_TM = 256
HALO = 8


def _conv_section(xe_ref, cw_ref, cols, tm):
    acc = cw_ref[pl.ds(CONV_K - 1, 1), cols] * xe_ref[pl.ds(HALO, tm), :]
    for tap in range(CONV_K - 1):
        acc = acc + cw_ref[pl.ds(tap, 1), cols] * xe_ref[pl.ds(HALO - (CONV_K - 1) + tap, tm), :]
    return acc


def _small_fwd(ps, bvec, alog):
    z = ps + bvec
    logsig, softp = _softplus_parts(z)
    gval = -jnp.exp(alog) * softp
    beta = _sigmoid(ps)
    return z, logsig, gval, beta


def _head_lane(block, lane):
    return jnp.sum(jnp.where(_iota(block.shape, 1) == lane, block, 0.0), axis=1, keepdims=True)


def _head_slab(block, lane):
    return jnp.broadcast_to(_head_lane(block, lane), block.shape)


def _chunk_masks(tm):
    r, c = _iota((tm, tm), 0), _iota((tm, tm), 1)
    same = (r // CHUNK) == (c // CHUNK)
    return (same & (r >= c)).astype(f32), same.astype(f32)


def _prep(p_main, p_small, qn_g, kn_g, conv_w, bvec, alog):
    s_len = p_main.shape[0]
    tm = PREP_TM
    nb = s_len // tm

    def body(fq_ref, fk_ref, fv_ref, gq_ref, gk_ref, gv_ref, hq_ref, hk_ref, hv_ref, ps_ref, qg_ref, kg_ref,
             cw_ref, bv_ref, al_ref,
             qs_ref, kn_ref, vb_ref, gqo_ref, gko_ref, gvo_ref, small_ref, xe_sc, carry_sc):
        i = pl.program_id(0)

        @pl.when(i == 0)
        def _():
            carry_sc[...] = jnp.zeros_like(carry_sc)

        vb_ref[...] = fv_ref[...].astype(bf16)

        first = i == 0
        for sec, (x_ref, halo_ref, o_ref) in enumerate(((gq_ref, hq_ref, gqo_ref), (gk_ref, hk_ref, gko_ref),
                                                        (gv_ref, hv_ref, gvo_ref))):
            xe_sc[0:HALO, :] = jnp.where(first, 0.0, halo_ref[...])
            xe_sc[HALO:, :] = x_ref[...]
            cv = _conv_section(xe_sc, cw_ref, slice(sec * WIDTH, (sec + 1) * WIDTH), tm)
            y = cv * _sigmoid(cv)
            if sec == 2:
                o_ref[...] = y
            else:
                mul = QK_SCALE if sec == 0 else 1.0
                for h in range(HEADS):
                    sl = slice(h * HEAD_DIM, (h + 1) * HEAD_DIM)
                    yh = y[:, sl]
                    o_ref[:, sl] = yh * (lax.rsqrt(jnp.sum(yh * yh, axis=-1, keepdims=True) + EPS) * mul)

        lane = _iota((tm, N_SMALL), 1)
        _, logsig, gval, beta = _small_fwd(ps_ref[...], bv_ref[...], al_ref[...])
        lf = jnp.where(lane < HEADS, logsig, 0.0)
        tri = (_iota((tm, tm), 0) >= _iota((tm, tm), 1)).astype(f32)
        fcum = jnp.dot(tri, lf, preferred_element_type=f32, precision=HI) + carry_sc[...]
        carry_sc[...] += jnp.sum(lf, axis=0, keepdims=True)
        tri_c, ones_c = _chunk_masks(tm)
        g_lanes = jnp.where((lane >= LANE_G) & (lane < LANE_G + HEADS), gval, 0.0)
        gc = jnp.dot(tri_c, g_lanes, preferred_element_type=f32, precision=HI)
        g_last = jnp.dot(ones_c, g_lanes, preferred_element_type=f32, precision=HI)
        small = jnp.where(lane < LANE_G, fcum, jnp.where(lane < LANE_BETA, gval, jnp.where(lane < LANE_GC, beta, 0.0)))
        small_ref[...] = small + pltpu.roll(gc, LANE_GC - LANE_G, 1) + pltpu.roll(g_last, LANE_GLAST - LANE_G, 1)

        qg, kg = qg_ref[...], kg_ref[...]
        f2 = fcum * LOG2E
        for h in range(HEADS):
            sl = slice(h * HEAD_DIM, (h + 1) * HEAD_DIM)
            q = fq_ref[:, sl]
            rq = lax.rsqrt(jnp.mean(q * q, axis=-1, keepdims=True) + EPS)
            k = fk_ref[:, sl]
            rk = lax.rsqrt(jnp.mean(k * k, axis=-1, keepdims=True) + EPS)
            f_col = _head_lane(f2, LANE_F + h)
            hi = f_col.astype(bf16).astype(f32)
            mid = (f_col - hi).astype(bf16).astype(f32)
            lo = f_col - hi - mid
            q_bias = jnp.where(lane == 0, hi, jnp.where(lane == 1, mid, jnp.where(lane == 2, lo,
                                                                                  jnp.where(lane < 6, 1.0, 0.0))))
            k_bias = jnp.where(lane < 3, 1.0, jnp.where(lane == 3, -hi, jnp.where(lane == 4, -mid,
                                                                                 jnp.where(lane == 5, -lo, 0.0))))
            base = 2 * h * HEAD_DIM
            qs_ref[:, base:base + HEAD_DIM] = (q * rq * qg * (QK_SCALE * LOG2E)).astype(bf16)
            qs_ref[:, base + HEAD_DIM:base + 2 * HEAD_DIM] = q_bias.astype(bf16)
            kn_ref[:, base:base + HEAD_DIM] = (k * rk * kg).astype(bf16)
            kn_ref[:, base + HEAD_DIM:base + 2 * HEAD_DIM] = k_bias.astype(bf16)

    def col(cb):
        return pl.BlockSpec((tm, WIDTH), lambda i: (i, cb))

    def halo(cb):
        return pl.BlockSpec((HALO, WIDTH), lambda i: (jnp.maximum(i * (tm // HALO) - 1, 0), cb))

    vec = pl.BlockSpec((1, LANES), lambda i: (0, 0))
    wide_f32 = jax.ShapeDtypeStruct((s_len, WIDTH), f32)
    wide_bf = jax.ShapeDtypeStruct((s_len, WIDTH), bf16)
    out_col = pl.BlockSpec((tm, WIDTH), lambda i: (i, 0))
    return pl.pallas_call(
        body, name="prep", grid=(nb,),
        in_specs=[col(0), col(1), col(2), col(4), col(5), col(6), halo(4), halo(5), halo(6),
                  pl.BlockSpec((tm, N_SMALL), lambda i: (i, 0)), vec, vec,
                  pl.BlockSpec((CONV_K, 3 * WIDTH), lambda i: (0, 0)), vec, vec],
        out_specs=[pl.BlockSpec((tm, 2 * WIDTH), lambda i: (i, 0))] * 2 + [out_col] * 4
                  + [pl.BlockSpec((tm, N_SMALL), lambda i: (i, 0))],
        out_shape=[jax.ShapeDtypeStruct((s_len, 2 * WIDTH), bf16)] * 2 + [wide_bf, wide_f32, wide_f32, wide_f32,
                                                                          jax.ShapeDtypeStruct((s_len, N_SMALL), f32)],
        scratch_shapes=[pltpu.VMEM((tm + HALO, WIDTH), f32), pltpu.VMEM((1, N_SMALL), f32)],
        compiler_params=_params("arbitrary"),
    )(p_main, p_main, p_main, p_main, p_main, p_main, p_main, p_main, p_main, p_small, qn_g, kn_g, conv_w, bvec, alog)


FOX_T = 1024
NEG_BIG = -1e30


def _fox_fwd(qs, kn, vb):
    s_len = qs.shape[0]
    t = FOX_T
    nq = s_len // t

    def body(q_ref, k_ref, v_ref, o_ref, lse_ref):
        qi = pl.program_id(1)
        q = q_ref[...]

        causal = _iota((t, t), 0) >= _iota((t, t), 1)

        def step(j, carry, masked):
            m, l, acc = carry
            rows = pl.ds(pl.multiple_of(j * t, t), t)
            s = _dg(q, k_ref[rows, :], 1, 1)
            if masked:
                s = jnp.where(causal, s, NEG_BIG)
            m_new = jnp.maximum(m, jnp.max(s, axis=-1, keepdims=True))
            p = jnp.exp2(s - m_new)
            alpha = jnp.exp2(m - m_new)
            l = alpha * l + jnp.sum(p, axis=-1, keepdims=True)
            acc = alpha * acc + jnp.dot(p.astype(bf16), v_ref[rows, :], preferred_element_type=f32)
            return m_new, l, acc

        init = (jnp.full((t, 1), NEG_BIG, f32), jnp.zeros((t, 1), f32), jnp.zeros((t, HEAD_DIM), f32))
        carry = lax.fori_loop(0, qi, lambda j, c: step(j, c, False), init)
        m, l, acc = step(qi, carry, True)
        o_ref[...] = acc / l
        lse_ref[0] = m + jnp.log2(l)

    return pl.pallas_call(
        body, name="fox_fwd", grid=(HEADS, nq),
        in_specs=[pl.BlockSpec((t, 2 * HEAD_DIM), lambda h, i: (i, h)),
                  pl.BlockSpec((s_len, 2 * HEAD_DIM), lambda h, i: (0, h)),
                  pl.BlockSpec((s_len, HEAD_DIM), lambda h, i: (0, h))],
        out_specs=[pl.BlockSpec((t, HEAD_DIM), lambda h, i: (i, h)),
                   pl.BlockSpec((1, t, 1), lambda h, i: (h, i, 0))],
        out_shape=[jax.ShapeDtypeStruct((s_len, WIDTH), f32), jax.ShapeDtypeStruct((HEADS, s_len, 1), f32)],
        compiler_params=_params("parallel", "arbitrary"),
    )(qs, kn, vb)


def _fox_bwd(qs, kn, vb, do, lse, delta):
    s_len = qs.shape[0]
    t = FOX_T
    nq = s_len // t
    half = t // 2

    def body(q_ref, do_ref, lse_ref, dl_ref, k_ref, v_ref, dq_ref, dk_ref, dvb_ref, df_ref, dfq_ref, dv_ref):
        head, qi = pl.program_id(0), pl.program_id(1)

        @pl.when(qi == 0)
        def _():
            dk_ref[...] = jnp.zeros_like(dk_ref)
            dv_ref[...] = jnp.zeros_like(dv_ref)
            df_ref[...] = jnp.zeros_like(df_ref)

        lse_col = lse_ref[0]
        dl = _head_lane(dl_ref[...], head)

        def update(q_rows, k_rows, df_lanes, j, carry, mask):
            dq, row_sum = carry
            q, do_b = q_ref[q_rows, :], do_ref[q_rows, :]
            p = jnp.exp2(_dg(q, k_ref[k_rows, :], 1, 1) - lse_col[q_rows])
            if mask is not None:
                p = jnp.where(mask, p, 0.0)
            ds = p * (_dg(do_b, v_ref[k_rows, :], 1, 1) - dl[q_rows])
            ds_b = ds.astype(bf16)
            dk_ref[k_rows, :] += _dg(ds_b, q_ref[q_rows, 0:HEAD_DIM], 0, 0)
            dv_ref[k_rows, :] += _dg(p.astype(bf16), do_b, 0, 0)
            df_ref[0, j, :, df_lanes] += -jnp.sum(ds, axis=0, keepdims=True)
            dq = dq + jnp.dot(ds_b, k_ref[k_rows, 0:HEAD_DIM], preferred_element_type=f32)
            return dq, row_sum + jnp.sum(ds, axis=-1, keepdims=True)

        everything, upper, lower = slice(0, t), slice(0, half), slice(half, t)
        carry = lax.fori_loop(
            0, qi, lambda j, c: update(everything, pl.ds(pl.multiple_of(j * t, t), t), everything, j, c, None),
            (jnp.zeros((t, HEAD_DIM), f32), jnp.zeros((t, 1), f32)))
        carry = update(everything, pl.ds(pl.multiple_of(qi * t, t), half), upper, qi, carry,
                       _iota((t, half), 0) >= _iota((t, half), 1))
        low = update(lower, pl.ds(pl.multiple_of(qi * t + half, half), half), lower, qi,
                     tuple(c[half:] for c in carry), _iota((half, half), 0) >= _iota((half, half), 1))
        dq, row_sum = (jnp.concatenate([c[:half], lo], axis=0) for c, lo in zip(carry, low))
        dq_ref[...] = dq
        dfq_ref[0] = row_sum

        @pl.when(qi == nq - 1)
        def _():
            dvb_ref[...] = dv_ref[...].astype(bf16)

    blk = pl.BlockSpec((t, HEAD_DIM), lambda h, i: (i, h))
    blk2 = pl.BlockSpec((t, 2 * HEAD_DIM), lambda h, i: (i, h))
    full = pl.BlockSpec((s_len, HEAD_DIM), lambda h, i: (0, h))
    full2 = pl.BlockSpec((s_len, 2 * HEAD_DIM), lambda h, i: (0, h))
    colv = pl.BlockSpec((1, t, 1), lambda h, i: (h, i, 0))
    rowv = pl.BlockSpec((1, nq, 1, t), lambda h, i: (h, 0, 0, 0))
    lanes = pl.BlockSpec((t, N_SMALL), lambda h, i: (i, 0))
    wide = jax.ShapeDtypeStruct((s_len, WIDTH), f32)
    return pl.pallas_call(
        body, name="fox_bwd", grid=(HEADS, nq),
        in_specs=[blk2, blk, colv, lanes, full2, full],
        out_specs=[blk, full, full, rowv, colv],
        out_shape=[wide, wide, jax.ShapeDtypeStruct((s_len, WIDTH), bf16), jax.ShapeDtypeStruct((HEADS, nq, 1, t), f32),
                   jax.ShapeDtypeStruct((HEADS, s_len, 1), f32)],
        scratch_shapes=[pltpu.VMEM((s_len, HEAD_DIM), f32)],
        compiler_params=_params("parallel", "arbitrary"),
    )(qs, do, lse, delta, kn, vb)


INTRA_CHUNKS = 8
SCAN_FWD_CHUNKS = 8
SCAN_BWD_CHUNKS = 4


def _gdn_intra_fwd(gq, gk, gv, small):
    s_len = gq.shape[0]
    cpb = INTRA_CHUNKS
    rows_blk = cpb * CHUNK
    n_chunks = s_len // CHUNK

    def body(q_ref, k_ref, v_ref, sm_ref, u_ref, w_ref, qg_ref, kd_ref, attn_ref, t_ref, eg_ref):
        head = pl.program_id(0)
        sm = sm_ref[...]
        gc_b, gl_b, beta_b = (_head_slab(sm, LANE_GC + head), _head_slab(sm, LANE_GLAST + head),
                              _head_slab(sm, LANE_BETA + head))
        ms, rhss = [], []
        for ci in range(cpb):
            rows = pl.ds(ci * CHUNK, CHUNK)
            sl = slice(ci * CHUNK, (ci + 1) * CHUNK)
            m, rhs, qg, kd, attn, eg_last = _gdn_intra_pre(q_ref[rows, :], k_ref[rows, :], v_ref[rows, :],
                                                           gc_b[sl], gl_b[sl], beta_b[sl], _BF_PLAIN[1])
            qg_ref[rows, :] = qg.astype(bf16)
            kd_ref[rows, :] = kd.astype(bf16)
            attn_ref[0, ci] = attn.astype(bf16)
            eg_ref[0, ci] = eg_last
            ms.append(m)
            rhss.append(rhs)
        for ci, (t, rhs) in enumerate(zip(_inv_unit_lower_many(ms), rhss)):
            rows = pl.ds(ci * CHUNK, CHUNK)
            t_ref[0, ci] = t
            uw = _X3_PLAIN[0](t, rhs)
            u_ref[rows, :] = uw[:, :HEAD_DIM]
            w_ref[rows, :] = uw[:, HEAD_DIM:].astype(bf16)

    blk = pl.BlockSpec((rows_blk, HEAD_DIM), lambda h, i: (i, h))
    sq = pl.BlockSpec((1, cpb, CHUNK, CHUNK), lambda h, i: (h, i, 0, 0))
    wide_bf = jax.ShapeDtypeStruct((s_len, WIDTH), bf16)
    return pl.pallas_call(
        body, name="gdn_intra_fwd", grid=(HEADS, s_len // rows_blk),
        in_specs=[blk] * 3 + [pl.BlockSpec((rows_blk, N_SMALL), lambda h, i: (i, 0))],
        out_specs=[blk] * 4 + [sq, sq, pl.BlockSpec((1, cpb, SUBLANES, HEAD_DIM), lambda h, i: (h, i, 0, 0))],
        out_shape=[jax.ShapeDtypeStruct((s_len, WIDTH), f32), wide_bf, wide_bf, wide_bf,
                   jax.ShapeDtypeStruct((HEADS, n_chunks, CHUNK, CHUNK), bf16),
                   jax.ShapeDtypeStruct((HEADS, n_chunks, CHUNK, CHUNK), f32),
                   jax.ShapeDtypeStruct((HEADS, n_chunks, SUBLANES, HEAD_DIM), f32)],
        compiler_params=_params("parallel", "parallel"),
    )(gq, gk, gv, small)


def _gdn_scan_fwd(u, w, qg, kd, attn, eg):
    s_len = u.shape[0]
    cpb = SCAN_FWD_CHUNKS
    rows_blk = cpb * CHUNK
    n_chunks = s_len // CHUNK

    def body(u_ref, w_ref, qg_ref, kd_ref, attn_ref, eg_ref, o_ref, st_ref, s_sc):
        @pl.when(pl.program_id(0) == 0)
        def _():
            s_sc[...] = jnp.zeros_like(s_sc)

        def chunk(ci, _):
            rows = pl.ds(pl.multiple_of(ci * CHUNK, CHUNK), CHUNK)
            cols = [slice(h * HEAD_DIM, (h + 1) * HEAD_DIM) for h in range(HEADS)]
            s0 = [s_sc[h] for h in range(HEADS)]
            s0_b = [s.astype(bf16) for s in s0]
            for h in range(HEADS):
                st_ref[h, ci] = s0[h]
            ws = [jnp.dot(w_ref[rows, cols[h]], s0_b[h], preferred_element_type=f32) for h in range(HEADS)]
            qs = [jnp.dot(qg_ref[rows, cols[h]], s0_b[h], preferred_element_type=f32) for h in range(HEADS)]
            vn_b = [(u_ref[rows, cols[h]] - ws[h]).astype(bf16) for h in range(HEADS)]
            av = [jnp.dot(attn_ref[h, ci], vn_b[h], preferred_element_type=f32) for h in range(HEADS)]
            kv = [_dg(kd_ref[rows, cols[h]], vn_b[h], 0, 0) for h in range(HEADS)]
            for h in range(HEADS):
                o_ref[rows, cols[h]] = qs[h] + av[h]
                s_sc[h] = _scale_rows(s0[h], eg_ref[h, ci]) + kv[h]
            return 0

        lax.fori_loop(0, cpb, chunk, 0)

    row = pl.BlockSpec((rows_blk, WIDTH), lambda i: (i, 0))
    return pl.pallas_call(
        body, name="gdn_scan_fwd", grid=(s_len // rows_blk,),
        in_specs=[row] * 4 + [pl.BlockSpec((HEADS, cpb, CHUNK, CHUNK), lambda i: (0, i, 0, 0)),
                              pl.BlockSpec((HEADS, cpb, SUBLANES, HEAD_DIM), lambda i: (0, i, 0, 0))],
        out_specs=[row, pl.BlockSpec((HEADS, cpb, HEAD_DIM, HEAD_DIM), lambda i: (0, i, 0, 0))],
        out_shape=[jax.ShapeDtypeStruct((s_len, WIDTH), f32),
                   jax.ShapeDtypeStruct((HEADS, n_chunks, HEAD_DIM, HEAD_DIM), f32)],
        scratch_shapes=[pltpu.VMEM((HEADS, HEAD_DIM, HEAD_DIM), f32)],
        compiler_params=_params("arbitrary"),
    )(u, w, qg, kd, attn, eg)


def _gdn_scan_bwd(u, w, qg, kd, attn, eg, states, d_o):
    s_len = u.shape[0]
    cpb = SCAN_BWD_CHUNKS
    rows_blk = cpb * CHUNK
    n_chunks = s_len // CHUNK
    nb = s_len // rows_blk

    def body(u_ref, w_ref, qg_ref, kd_ref, attn_ref, eg_ref, st_ref, do_ref,
             du_ref, dw_ref, dqg_ref, dkd_ref, dattn_ref, deg_ref, ds_sc):
        @pl.when(pl.program_id(0) == 0)
        def _():
            ds_sc[...] = jnp.zeros_like(ds_sc)

        def chunk(step, _):
            ci = cpb - 1 - step
            rows = pl.ds(pl.multiple_of(ci * CHUNK, CHUNK), CHUNK)
            hs = range(HEADS)
            cols = [slice(h * HEAD_DIM, (h + 1) * HEAD_DIM) for h in hs]
            s0 = [st_ref[h, ci] for h in hs]
            s0_b = [s.astype(bf16) for s in s0]
            ds1 = [ds_sc[h] for h in hs]
            ds1_b = [d.astype(bf16) for d in ds1]
            do_b = [do_ref[rows, cols[h]].astype(bf16) for h in hs]
            ws = [jnp.dot(w_ref[rows, cols[h]], s0_b[h], preferred_element_type=f32) for h in hs]
            ad = [_dg(attn_ref[h, ci], do_b[h], 0, 0) for h in hs]
            kd_ds = [jnp.dot(kd_ref[rows, cols[h]], ds1_b[h], preferred_element_type=f32) for h in hs]
            dqg = [_dg(do_b[h], s0_b[h], 1, 1) for h in hs]
            qd = [_dg(qg_ref[rows, cols[h]], do_b[h], 0, 0) for h in hs]
            vn_b = [(u_ref[rows, cols[h]] - ws[h]).astype(bf16) for h in hs]
            dvn = [ad[h] + kd_ds[h] for h in hs]
            dvn_b = [d.astype(bf16) for d in dvn]
            dattn = [_dg(do_b[h], vn_b[h], 1, 1) for h in hs]
            dkd = [_dg(vn_b[h], ds1_b[h], 1, 1) for h in hs]
            dw = [_dg(dvn_b[h], s0_b[h], 1, 1) for h in hs]
            wd = [_dg(w_ref[rows, cols[h]], dvn_b[h], 0, 0) for h in hs]
            for h in hs:
                dattn_ref[h, ci] = dattn[h]
                dqg_ref[rows, cols[h]] = dqg[h]
                dkd_ref[rows, cols[h]] = dkd[h]
                du_ref[rows, cols[h]] = dvn[h]
                dw_ref[rows, cols[h]] = -dw[h]
                ds_sc[h] = qd[h] - wd[h] + _scale_rows(ds1[h], eg_ref[h, ci])
                deg_ref[h, ci] = jnp.sum((ds1[h] * s0[h]).reshape(HEAD_DIM // SUBLANES, SUBLANES, HEAD_DIM), axis=0)
            return 0

        lax.fori_loop(0, cpb, chunk, 0)

    row = pl.BlockSpec((rows_blk, WIDTH), lambda i: (nb - 1 - i, 0))
    sq = pl.BlockSpec((HEADS, cpb, CHUNK, CHUNK), lambda i: (0, nb - 1 - i, 0, 0))
    egs = pl.BlockSpec((HEADS, cpb, SUBLANES, HEAD_DIM), lambda i: (0, nb - 1 - i, 0, 0))
    wide = jax.ShapeDtypeStruct((s_len, WIDTH), f32)
    return pl.pallas_call(
        body, name="gdn_scan_bwd", grid=(nb,),
        in_specs=[row] * 4 + [sq, egs, pl.BlockSpec((HEADS, cpb, HEAD_DIM, HEAD_DIM), lambda i: (0, nb - 1 - i, 0, 0)), row],
        out_specs=[row] * 4 + [sq, egs],
        out_shape=[wide] * 4 + [jax.ShapeDtypeStruct((HEADS, n_chunks, CHUNK, CHUNK), f32),
                                jax.ShapeDtypeStruct((HEADS, n_chunks, SUBLANES, HEAD_DIM), f32)],
        scratch_shapes=[pltpu.VMEM((HEADS, HEAD_DIM, HEAD_DIM), f32)],
        compiler_params=_params("arbitrary"),
    )(u, w, qg, kd, attn, eg, states, d_o)


def _gdn_intra_bwd(gq, gk, gv, small, t_inv, du, dw, dqg, dkd, dattn, deg):
    s_len = gq.shape[0]
    cpb = INTRA_CHUNKS
    rows_blk = cpb * CHUNK

    def body(q_ref, k_ref, v_ref, sm_ref, t_ref, du_ref, dw_ref, dqg_ref, dkd_ref, dattn_ref, deg_ref,
             dq_ref, dk_ref, dv_ref, dsm_ref):
        head = pl.program_id(1)

        def batch(value):
            return value.reshape(cpb, CHUNK, HEAD_DIM)

        sm = sm_ref[...]
        slabs = [batch(_head_slab(sm, first + head)) for first in (LANE_GC, LANE_GLAST, LANE_BETA)]
        t_known = t_ref[0]
        _, vjp = jax.vjp(lambda q, k, v, gc, gl, b: _gdn_intra(q, k, v, gc, gl, b, t_known),
                         batch(q_ref[...]), batch(k_ref[...]), batch(v_ref[...]), *slabs)
        duw = jnp.concatenate([batch(du_ref[...]), batch(dw_ref[...])], axis=-1)
        dq, dk, dv, dgc, dgl, db = vjp((duw, batch(dqg_ref[...]), batch(dkd_ref[...]), dattn_ref[0], deg_ref[0]))
        for ref, grad in zip((dq_ref, dk_ref, dv_ref), (dq, dk, dv)):
            ref[...] = grad.reshape(rows_blk, HEAD_DIM)

        @pl.when(head == 0)
        def _():
            dsm_ref[...] = jnp.zeros_like(dsm_ref)

        lane = _iota((rows_blk, N_SMALL), 1)
        acc = dsm_ref[...]
        for first, grad in ((LANE_GC, dgc), (LANE_GLAST, dgl), (LANE_BETA, db)):
            col = jnp.sum(grad.reshape(rows_blk, HEAD_DIM), axis=1, keepdims=True)
            acc = acc + jnp.where(lane == first + head, col, 0.0)
        dsm_ref[...] = acc

    blk = pl.BlockSpec((rows_blk, HEAD_DIM), lambda i, h: (i, h))
    sq = pl.BlockSpec((1, cpb, CHUNK, CHUNK), lambda i, h: (h, i, 0, 0))
    egs = pl.BlockSpec((1, cpb, SUBLANES, HEAD_DIM), lambda i, h: (h, i, 0, 0))
    lanes = pl.BlockSpec((rows_blk, N_SMALL), lambda i, h: (i, 0))
    wide = jax.ShapeDtypeStruct((s_len, WIDTH), f32)
    return pl.pallas_call(
        body, name="gdn_intra_bwd", grid=(s_len // rows_blk, HEADS),
        in_specs=[blk] * 3 + [lanes, sq] + [blk] * 4 + [sq, egs],
        out_specs=[blk] * 3 + [lanes],
        out_shape=[wide] * 3 + [jax.ShapeDtypeStruct((s_len, N_SMALL), f32)],
        compiler_params=_params("parallel", "arbitrary"),
    )(gq, gk, gv, small, t_inv, du, dw, dqg, dkd, dattn, deg)


MIX_TM = 256


def _mix_fwd(fox_o, gdn_o, p_main, gnorm_g):
    s_len = fox_o.shape[0]
    tm = MIX_TM

    def body(fo_ref, go_ref, fz_ref, gz_ref, g_ref, mixed_ref):
        fz = fz_ref[...]
        mixed_ref[:, 0:WIDTH] = (fo_ref[...] * (fz * _sigmoid(fz))).astype(bf16)
        gz = gz_ref[...]
        gate = gz * _sigmoid(gz)
        gg = g_ref[...]
        for h in range(HEADS):
            sl = slice(h * HEAD_DIM, (h + 1) * HEAD_DIM)
            o = go_ref[:, sl]
            r = lax.rsqrt(jnp.mean(o * o, axis=-1, keepdims=True) + EPS)
            mixed_ref[:, WIDTH + h * HEAD_DIM:WIDTH + (h + 1) * HEAD_DIM] = (o * r * gg * gate[:, sl]).astype(bf16)

    row = pl.BlockSpec((tm, WIDTH), lambda i: (i, 0))
    return pl.pallas_call(
        body, name="mix_fwd", grid=(s_len // tm,),
        in_specs=[row, row, pl.BlockSpec((tm, WIDTH), lambda i: (i, 3)), pl.BlockSpec((tm, WIDTH), lambda i: (i, 7)),
                  pl.BlockSpec((1, LANES), lambda i: (0, 0))],
        out_specs=pl.BlockSpec((tm, 2 * WIDTH), lambda i: (i, 0)),
        out_shape=jax.ShapeDtypeStruct((s_len, 2 * WIDTH), bf16),
        compiler_params=_params("parallel"),
    )(fox_o, gdn_o, p_main, p_main, gnorm_g)


def _silu_grad(z):
    sg = _sigmoid(z)
    return sg * (1.0 + z * (1.0 - sg))


def _mix_bwd(dmixed, fox_o, gdn_o, p_main, gnorm_g):
    s_len = fox_o.shape[0]
    tm = MIX_TM

    def body(dm_ref, fo_ref, go_ref, fz_ref, gz_ref, g_ref, dof_ref, delta_ref, dfz_ref, dgz_ref, dgo_ref, dg_ref):
        @pl.when(pl.program_id(0) == 0)
        def _():
            dg_ref[...] = jnp.zeros_like(dg_ref)

        lane = _iota((tm, LANES), 1)
        fz = fz_ref[...]
        dmf = dm_ref[:, 0:WIDTH]
        fo = fo_ref[...]
        dof = dmf * (fz * _sigmoid(fz))
        dof_ref[...] = dof.astype(bf16)
        dfz_ref[...] = (dmf * fo * _silu_grad(fz)).astype(bf16)
        prod = dof * fo
        delta = jnp.zeros((tm, LANES), f32)
        for h in range(HEADS):
            dh = jnp.sum(prod[:, h * HEAD_DIM:(h + 1) * HEAD_DIM], axis=-1, keepdims=True)
            delta = jnp.where(lane == h, dh, delta)
        delta_ref[...] = delta

        gz = gz_ref[...]
        dmg = dm_ref[:, WIDTH:2 * WIDTH]
        gate = gz * _sigmoid(gz)
        sgrad = _silu_grad(gz)
        gg = g_ref[...]
        dg_acc = jnp.zeros((1, HEAD_DIM), f32)
        for h in range(HEADS):
            sl = slice(h * HEAD_DIM, (h + 1) * HEAD_DIM)
            o = go_ref[:, sl]
            r = lax.rsqrt(jnp.mean(o * o, axis=-1, keepdims=True) + EPS)
            on = o * r
            dmh = dmg[:, sl]
            dgz_ref[:, sl] = (dmh * (on * gg) * sgrad[:, sl]).astype(bf16)
            dy = dmh * gate[:, sl]
            dg_acc = dg_acc + jnp.sum(dy * on, axis=0, keepdims=True)
            tt = dy * gg
            dgo_ref[:, sl] = r * (tt - on * jnp.mean(tt * on, axis=-1, keepdims=True))
        dg_ref[...] += dg_acc

    row = pl.BlockSpec((tm, WIDTH), lambda i: (i, 0))
    wide_bf = jax.ShapeDtypeStruct((s_len, WIDTH), bf16)
    return pl.pallas_call(
        body, name="mix_bwd", grid=(s_len // tm,),
        in_specs=[pl.BlockSpec((tm, 2 * WIDTH), lambda i: (i, 0)), row, row,
                  pl.BlockSpec((tm, WIDTH), lambda i: (i, 3)), pl.BlockSpec((tm, WIDTH), lambda i: (i, 7)),
                  pl.BlockSpec((1, LANES), lambda i: (0, 0))],
        out_specs=[row, pl.BlockSpec((tm, LANES), lambda i: (i, 0)), row, row, row,
                   pl.BlockSpec((1, LANES), lambda i: (0, 0))],
        out_shape=[wide_bf, jax.ShapeDtypeStruct((s_len, LANES), f32), wide_bf, wide_bf,
                   jax.ShapeDtypeStruct((s_len, WIDTH), f32), jax.ShapeDtypeStruct((1, LANES), f32)],
        compiler_params=_params("arbitrary"),
    )(dmixed, fox_o, gdn_o, p_main, p_main, gnorm_g)


def _out_head(mixed, w_out, x, target, gate, final_g):
    s_len = x.shape[0]
    tm = 256

    def body(mx_ref, w_ref, x_ref, t_ref, gate_ref, fg_ref, loss_ref, dy_ref, dz_ref, dm_ref, dfg_ref, dgate_ref):
        @pl.when(pl.program_id(0) == 0)
        def _():
            loss_ref[...] = jnp.zeros_like(loss_ref)
            dfg_ref[...] = jnp.zeros_like(dfg_ref)
            dgate_ref[...] = jnp.zeros_like(dgate_ref)

        w = w_ref[...]
        z = jnp.dot(mx_ref[...], w, preferred_element_type=f32)
        gate_v, fg = gate_ref[...], fg_ref[...]
        y1 = x_ref[...] + gate_v * z
        r = lax.rsqrt(jnp.mean(y1 * y1, axis=-1, keepdims=True) + EPS)
        yn = y1 * r
        err = yn * fg - t_ref[...]
        loss_ref[...] += 0.5 * jnp.sum(jnp.mean(err * err, axis=-1, keepdims=True))
        dout = err * (1.0 / D_MODEL)
        dfg_ref[...] += jnp.sum(dout * yn, axis=0, keepdims=True)
        tt = dout * fg
        dy1 = r * (tt - yn * jnp.mean(tt * yn, axis=-1, keepdims=True))
        dy_ref[...] = dy1
        dgate_ref[...] += jnp.sum(dy1 * z, axis=0, keepdims=True)
        dz = (dy1 * gate_v).astype(bf16)
        dz_ref[...] = dz
        dm_ref[...] = _dg(dz, w, 1, 1)

    row = pl.BlockSpec((tm, D_MODEL), lambda i: (i, 0))
    vec = pl.BlockSpec((1, D_MODEL), lambda i: (0, 0))
    big = jax.ShapeDtypeStruct((s_len, D_MODEL), f32)
    return pl.pallas_call(
        body, name="out_head", grid=(s_len // tm,),
        in_specs=[row, pl.BlockSpec((D_MODEL, D_MODEL), lambda i: (0, 0)), row, row, vec, vec],
        out_specs=[pl.BlockSpec((1, LANES), lambda i: (0, 0)), row, row, row, vec, vec],
        out_shape=[jax.ShapeDtypeStruct((1, LANES), f32), big, jax.ShapeDtypeStruct((s_len, D_MODEL), bf16), big,
                   jax.ShapeDtypeStruct((1, D_MODEL), f32), jax.ShapeDtypeStruct((1, D_MODEL), f32)],
        compiler_params=_params("arbitrary"),
    )(mixed, w_out, x, target, gate, final_g)


def _matmul_tn(name, a, b, out_dtype):
    k_len, m_len = a.shape
    n_len = b.shape[1]
    tk, tm, tn = min(2048, k_len), min(1024, m_len), min(1024, n_len)
    nk = k_len // tk

    def body(a_ref, b_ref, o_ref, acc_sc):
        k = pl.program_id(2)

        @pl.when(k == 0)
        def _():
            acc_sc[...] = jnp.zeros_like(acc_sc)

        acc_sc[...] += _dg(a_ref[...], b_ref[...], 0, 0)

        @pl.when(k == nk - 1)
        def _():
            o_ref[...] = acc_sc[...].astype(out_dtype)

    return pl.pallas_call(
        body, name=name, grid=(m_len // tm, n_len // tn, nk),
        in_specs=[pl.BlockSpec((tk, tm), lambda i, j, k: (k, i)), pl.BlockSpec((tk, tn), lambda i, j, k: (k, j))],
        out_specs=pl.BlockSpec((tm, tn), lambda i, j, k: (i, j)),
        out_shape=jax.ShapeDtypeStruct((m_len, n_len), out_dtype),
        scratch_shapes=[pltpu.VMEM((tm, tn), f32)],
        compiler_params=_params("parallel", "parallel", "arbitrary"),
    )(a, b)


def _post1(p_main, p_small, qn_g, kn_g, conv_w, bvec, alog, dqs, dkn, dgq, dgk, dgv, d_small, df, df_query):
    s_len = p_main.shape[0]
    tm = PREP_TM
    nb = s_len // tm

    def body(fq_ref, fk_ref, gq_ref, gk_ref, gv_ref, hq_ref, hk_ref, hv_ref, ps_ref, qg_ref, kg_ref, cw_ref, bv_ref,
             al_ref, dqs_ref, dkn_ref, dgq_ref, dgk_ref, dgv_ref, dsm_ref, df_ref, dfq_in_ref,
             dfq_ref, dfk_ref, dx_ref, dps_ref, dqg_ref, dkg_ref, sums_ref, dw_ref, xe_sc, carry_sc, dc_sc, next_sc):
        step = pl.program_id(0)
        blk = nb - 1 - step

        @pl.when(step == 0)
        def _():
            carry_sc[...] = jnp.zeros_like(carry_sc)
            next_sc[...] = jnp.zeros_like(next_sc)
            dqg_ref[...] = jnp.zeros_like(dqg_ref)
            dkg_ref[...] = jnp.zeros_like(dkg_ref)
            sums_ref[...] = jnp.zeros_like(sums_ref)
            dw_ref[...] = jnp.zeros_like(dw_ref)

        for x_ref, g_ref, dy_ref, o_ref, acc_ref, mul in ((fq_ref, qg_ref, dqs_ref, dfq_ref, dqg_ref, QK_SCALE),
                                                          (fk_ref, kg_ref, dkn_ref, dfk_ref, dkg_ref, LN2)):
            gain = g_ref[...]
            acc = jnp.zeros((1, HEAD_DIM), f32)
            for h in range(HEADS):
                sl = slice(h * HEAD_DIM, (h + 1) * HEAD_DIM)
                xv = x_ref[:, sl]
                r = lax.rsqrt(jnp.mean(xv * xv, axis=-1, keepdims=True) + EPS)
                xn = xv * r
                dy = dy_ref[:, sl] * mul
                acc = acc + jnp.sum(dy * xn, axis=0, keepdims=True)
                tt = dy * gain
                o_ref[:, sl] = (r * (tt - xn * jnp.mean(tt * xn, axis=-1, keepdims=True))).astype(bf16)
            acc_ref[...] += acc

        first = blk == 0
        for sec, (x_ref, halo_ref, dy_ref) in enumerate(((gq_ref, hq_ref, dgq_ref), (gk_ref, hk_ref, dgk_ref),
                                                         (gv_ref, hv_ref, dgv_ref))):
            cols = slice(sec * WIDTH, (sec + 1) * WIDTH)
            xe_sc[0:HALO, :] = jnp.where(first, 0.0, halo_ref[...])
            xe_sc[HALO:, :] = x_ref[...]
            cv = _conv_section(xe_sc, cw_ref, cols, tm)
            sgrad = _silu_grad(cv)
            if sec == 2:
                dc_sc[0:tm, :] = dy_ref[...] * sgrad
            else:
                y = cv * _sigmoid(cv)
                mul = QK_SCALE if sec == 0 else 1.0
                for h in range(HEADS):
                    sl = slice(h * HEAD_DIM, (h + 1) * HEAD_DIM)
                    yh = y[:, sl]
                    r = lax.rsqrt(jnp.sum(yh * yh, axis=-1, keepdims=True) + EPS)
                    dqh = dy_ref[:, sl]
                    dyh = (mul * r) * (dqh - yh * (r * r) * jnp.sum(dqh * yh, axis=-1, keepdims=True))
                    dc_sc[0:tm, sl] = dyh * sgrad[:, sl]
            dc_sc[tm:, :] = next_sc[sec]
            x_rows = xe_sc[pl.ds(HALO, tm), :]
            dx = jnp.zeros((tm, WIDTH), f32)
            dw = jnp.zeros((8, WIDTH), f32)
            tap_row = _iota((8, WIDTH), 0)
            for tap in range(CONV_K):
                ahead = dc_sc[pl.ds(CONV_K - 1 - tap, tm), :]
                dx = dx + cw_ref[pl.ds(tap, 1), cols] * ahead
                dw = jnp.where(tap_row == tap, jnp.sum(x_rows * ahead, axis=0, keepdims=True), dw)
            dx_ref[:, cols] = dx.astype(bf16)
            dw_ref[:, cols] += dw
            next_sc[sec] = dc_sc[0:HALO, :]

        lane = _iota((tm, N_SMALL), 1)
        z, _, gval, beta = _small_fwd(ps_ref[...], bv_ref[...], al_ref[...])
        sig_z = _sigmoid(z)
        dsm = dsm_ref[...]
        in_g = (lane >= LANE_G) & (lane < LANE_G + HEADS)
        dgc = jnp.where(in_g, pltpu.roll(dsm, N_SMALL - (LANE_GC - LANE_G), 1), 0.0)
        dgl = jnp.where(in_g, pltpu.roll(dsm, N_SMALL - (LANE_GLAST - LANE_G), 1), 0.0)
        tri_c, ones_c = _chunk_masks(tm)
        dg = (_dg(tri_c, dgc, 0, 0, HI) + jnp.dot(ones_c, dgl, preferred_element_type=f32, precision=HI))
        dbeta = dsm
        dfb = jnp.where(lane < HEADS, df_ref[...], 0.0)
        for h in range(HEADS):
            dfb = dfb + jnp.where(lane == h, dfq_in_ref[h], 0.0)
        tri_u = (_iota((tm, tm), 1) >= _iota((tm, tm), 0)).astype(f32)
        dlogf = jnp.dot(tri_u, dfb, preferred_element_type=f32, precision=HI) + carry_sc[...]
        carry_sc[...] += jnp.sum(dfb, axis=0, keepdims=True)
        dff = dlogf * (1.0 - sig_z)
        dga = dg * (-jnp.exp(al_ref[...])) * sig_z
        dgb_small = dbeta * beta * (1.0 - beta)
        dps = jnp.where(lane < HEADS, dff, jnp.where(lane < 2 * HEADS, dga, jnp.where(lane < 3 * HEADS, dgb_small, 0.0)))
        dps_ref[...] = dps.astype(bf16)
        row = _iota((8, N_SMALL), 0)
        s0 = jnp.sum(dps, axis=0, keepdims=True)
        s1 = jnp.sum(jnp.where((lane >= HEADS) & (lane < 2 * HEADS), dg * gval, 0.0), axis=0, keepdims=True)
        sums_ref[...] += jnp.where(row == 0, s0, jnp.where(row == 1, s1, 0.0))

    def col(cb):
        return pl.BlockSpec((tm, WIDTH), lambda i: (nb - 1 - i, cb))

    def halo(cb):
        return pl.BlockSpec((HALO, WIDTH), lambda i: (jnp.maximum((nb - 1 - i) * (tm // HALO) - 1, 0), cb))

    vec = pl.BlockSpec((1, LANES), lambda i: (0, 0))
    row0 = pl.BlockSpec((tm, WIDTH), lambda i: (nb - 1 - i, 0))
    small = pl.BlockSpec((tm, N_SMALL), lambda i: (nb - 1 - i, 0))
    wide_bf = jax.ShapeDtypeStruct((s_len, WIDTH), bf16)
    return pl.pallas_call(
        body, name="post1", grid=(nb,),
        in_specs=[col(0), col(1), col(4), col(5), col(6), halo(4), halo(5), halo(6), small, vec, vec,
                  pl.BlockSpec((CONV_K, 3 * WIDTH), lambda i: (0, 0)), vec, vec,
                  row0, row0, row0, row0, row0, small, small,
                  pl.BlockSpec((HEADS, tm, 1), lambda i: (0, nb - 1 - i, 0))],
        out_specs=[row0, row0, pl.BlockSpec((tm, 3 * WIDTH), lambda i: (nb - 1 - i, 0)), small, vec, vec,
                   pl.BlockSpec((8, N_SMALL), lambda i: (0, 0)), pl.BlockSpec((8, 3 * WIDTH), lambda i: (0, 0))],
        out_shape=[wide_bf, wide_bf, jax.ShapeDtypeStruct((s_len, 3 * WIDTH), bf16),
                   jax.ShapeDtypeStruct((s_len, N_SMALL), bf16), jax.ShapeDtypeStruct((1, LANES), f32),
                   jax.ShapeDtypeStruct((1, LANES), f32), jax.ShapeDtypeStruct((8, N_SMALL), f32),
                   jax.ShapeDtypeStruct((8, 3 * WIDTH), f32)],
        scratch_shapes=[pltpu.VMEM((tm + HALO, WIDTH), f32), pltpu.VMEM((1, N_SMALL), f32),
                        pltpu.VMEM((tm + HALO, WIDTH), f32), pltpu.VMEM((3, HALO, WIDTH), f32)],
        compiler_params=_params("arbitrary"),
    )(p_main, p_main, p_main, p_main, p_main, p_main, p_main, p_main, p_small, qn_g, kn_g, conv_w, bvec, alog,
      dqs, dkn, dgq, dgk, dgv, d_small, df, df_query)


def _in_proj_bwd(dp_pieces, dp_small, wt_main, wt_small):
    s_len = dp_small.shape[0]
    tm, tk = min(1024, s_len), WIDTH
    nk = N_MAIN // tk
    first_section = [sum(p.shape[1] // tk for p in dp_pieces[:n]) for n in range(len(dp_pieces))]
    n_pieces = len(dp_pieces)

    def body(*refs):
        piece_refs = refs[:n_pieces]
        dps_ref, w_ref, ws_ref, dh_ref = refs[n_pieces:]
        k = pl.program_id(1)

        @pl.when(k == 0)
        def _():
            dh_ref[...] = jnp.dot(dps_ref[...], ws_ref[...], preferred_element_type=f32)

        for piece, ref, first in zip(dp_pieces, piece_refs, first_section):
            @pl.when((k >= first) & (k < first + piece.shape[1] // tk))
            def _(ref=ref):
                dh_ref[...] += jnp.dot(ref[...], w_ref[...], preferred_element_type=f32)

    def piece_spec(piece, first):
        last = piece.shape[1] // tk - 1
        return pl.BlockSpec((tm, tk), lambda i, k: (i, jnp.clip(k - first, 0, last)))

    return pl.pallas_call(
        body, name="in_proj_bwd", grid=(s_len // tm, nk),
        in_specs=[piece_spec(p, f) for p, f in zip(dp_pieces, first_section)]
                 + [pl.BlockSpec((tm, N_SMALL), lambda i, k: (i, 0)),
                    pl.BlockSpec((tk, D_MODEL), lambda i, k: (k, 0)), pl.BlockSpec((N_SMALL, D_MODEL), lambda i, k: (0, 0))],
        out_specs=pl.BlockSpec((tm, D_MODEL), lambda i, k: (i, 0)),
        out_shape=jax.ShapeDtypeStruct((s_len, D_MODEL), f32),
        compiler_params=_params("parallel", "arbitrary"),
    )(*dp_pieces, dp_small, wt_main, wt_small)


def _adaln_bwd(dh, x, dy1, norm_g, scale1p):
    s_len = x.shape[0]
    tm = 256

    def body(dh_ref, x_ref, dy_ref, g_ref, sc_ref, dx_ref, dsh_ref, dsc_ref, dg_ref):
        @pl.when(pl.program_id(0) == 0)
        def _():
            dsh_ref[...] = jnp.zeros_like(dsh_ref)
            dsc_ref[...] = jnp.zeros_like(dsc_ref)
            dg_ref[...] = jnp.zeros_like(dg_ref)

        dh = dh_ref[...]
        xb = x_ref[...]
        r = lax.rsqrt(jnp.mean(xb * xb, axis=-1, keepdims=True) + EPS)
        xr = xb * r
        gain = g_ref[...]
        dsh_ref[...] += jnp.sum(dh, axis=0, keepdims=True)
        dsc_ref[...] += jnp.sum(dh * (xr * gain), axis=0, keepdims=True)
        dxn = dh * sc_ref[...]
        dg_ref[...] += jnp.sum(dxn * xr, axis=0, keepdims=True)
        tt = dxn * gain
        dx_ref[...] = r * (tt - xr * jnp.mean(tt * xr, axis=-1, keepdims=True)) + dy_ref[...]

    row = pl.BlockSpec((tm, D_MODEL), lambda i: (i, 0))
    vec = pl.BlockSpec((1, D_MODEL), lambda i: (0, 0))
    vshape = jax.ShapeDtypeStruct((1, D_MODEL), f32)
    return pl.pallas_call(
        body, name="adaln_bwd", grid=(s_len // tm,),
        in_specs=[row, row, row, vec, vec], out_specs=[row, vec, vec, vec],
        out_shape=[jax.ShapeDtypeStruct((s_len, D_MODEL), f32), vshape, vshape, vshape],
        compiler_params=_params("arbitrary"),
    )(dh, x, dy1, norm_g, scale1p)


def _adamw(name, w, g_stack, m, v, tr, tc=None):
    n_stack, rows, cols = g_stack.shape
    tc = cols if tc is None else tc

    def body(w_ref, g_ref, m_ref, v_ref, go_ref, d_ref, mo_ref, vo_ref):
        g = g_ref[0].astype(f32)
        for k in range(1, n_stack):
            g = g + g_ref[k].astype(f32)
        go_ref[0] = g
        m_new = ADAM_B1 * m_ref[0] + (1.0 - ADAM_B1) * g
        v_new = ADAM_B2 * v_ref[0] + (1.0 - ADAM_B2) * (g * g)
        mo_ref[0] = m_new
        vo_ref[0] = v_new
        m_hat = m_new / (1.0 - ADAM_B1 ** ADAM_STEP)
        v_hat = v_new / (1.0 - ADAM_B2 ** ADAM_STEP)
        d_ref[0] = -ADAM_LR * (m_hat / (jnp.sqrt(v_hat) + ADAM_EPS) + ADAM_WD * w_ref[0])

    blk = pl.BlockSpec((1, tr, tc), lambda i, j: (0, i, j))
    shape = jax.ShapeDtypeStruct((1, rows, cols), f32)
    return pl.pallas_call(
        body, name=name, grid=(rows // tr, cols // tc),
        in_specs=[blk, pl.BlockSpec((n_stack, tr, tc), lambda i, j: (0, i, j)), blk, blk],
        out_specs=[blk] * 4, out_shape=[shape] * 4,
        compiler_params=_params("parallel", "parallel"),
    )(w, g_stack, m, v)


def _w_ada_grad(c_all_t, dmod_pad):
    def body(c_ref, d_ref, o_ref):
        cv = c_ref[...]
        o_ref[...] = jnp.dot(cv * _sigmoid(cv), d_ref[...], preferred_element_type=f32, precision=HI)

    return pl.pallas_call(body, name="w_ada_grad",
                          out_shape=jax.ShapeDtypeStruct((c_all_t.shape[0], dmod_pad.shape[1]), f32),
                          compiler_params=_params())(c_all_t, dmod_pad)


SMALL_NAMES = ("norm_g", "b_ada", "b_fgate", "fox_qn_g", "fox_kn_g", "gdn_A_log", "gdn_dt_bias", "gdn_norm_g", "final_g")
SMALL_SIZES = (D_MODEL, 3 * D_MODEL, HEADS, HEAD_DIM, HEAD_DIM, HEADS, HEADS, HEAD_DIM, D_MODEL)
SMALL_PACK = 10752


def _pack(vectors, total):
    flat = jnp.concatenate([t.reshape(-1) for t in vectors])
    return jnp.pad(flat, (0, total - flat.shape[0])).reshape(1, total)


def _lanes(*pieces):
    parts, at = [], 0
    for off, vec in pieces:
        flat = vec.reshape(-1).astype(f32)
        parts += [jnp.zeros((off - at,), f32), flat]
        at = off + flat.shape[0]
    parts.append(jnp.zeros((LANES - at,), f32))
    return jnp.concatenate(parts).reshape(1, LANES)


def kernel(x, c, norm_g, w_ada, b_ada, w_in, b_fgate, fox_qn_g, fox_kn_g, gdn_conv_w, gdn_A_log, gdn_dt_bias, gdn_norm_g, w_out, final_g, loss_target, m_norm_g, m_w_ada, m_b_ada, m_w_in, m_b_fgate, m_fox_qn_g, m_fox_kn_g, m_gdn_conv_w, m_gdn_A_log, m_gdn_dt_bias, m_gdn_norm_g, m_w_out, m_final_g, v_norm_g, v_w_ada, v_b_ada, v_w_in, v_b_fgate, v_fox_qn_g, v_fox_kn_g, v_gdn_conv_w, v_gdn_A_log, v_gdn_dt_bias, v_gdn_norm_g, v_w_out, v_final_g):
    me = _my_index()
    s_len = x.shape[1]
    nq = s_len // FOX_T
    x2 = x.reshape(s_len, D_MODEL)
    tgt = loss_target.reshape(s_len, D_MODEL)
    ada_cols = w_ada.shape[2]
    in_cols = w_in.shape[2]
    conv_cols = gdn_conv_w.shape[2]

    (c_all,) = _gather_direct("gather_c", [c])
    c_all = c_all.reshape(N_DEV, D_MODEL)
    b_shard = lax.dynamic_slice(b_ada, (0, me * ada_cols), (1, ada_cols))
    mod_mine = _mod_shard(c_all, w_ada[0], b_shard)
    wt_shard = jnp.transpose(w_in[0])
    mod_all, wt_all, w_out_all, conv_all = _gather_two_level(
        "gather_weights", [mod_mine, wt_shard.astype(bf16), w_out[0].astype(bf16), gdn_conv_w[0]])
    mod = lax.dynamic_slice(mod_all, (0, me, 0), (N_DEV, 1, ada_cols)).reshape(1, 3 * D_MODEL)
    shift, scale, gate = mod[:, :D_MODEL], mod[:, D_MODEL:2 * D_MODEL], mod[:, 2 * D_MODEL:]
    scale1p = 1.0 + scale
    wt_full = wt_all.reshape(N_DEV * in_cols, D_MODEL)
    g0 = 4 * WIDTH + HEADS
    w_main = jnp.concatenate([wt_full[:4 * WIDTH], wt_full[g0:g0 + 4 * WIDTH]], axis=0)
    w_small = jnp.concatenate([wt_full[4 * WIDTH:g0], wt_full[g0 + 4 * WIDTH:],
                               jnp.zeros((N_SMALL - 3 * HEADS, D_MODEL), bf16)], axis=0)
    w_out_full = w_out_all.reshape(2 * WIDTH, D_MODEL)
    conv_full = jnp.transpose(conv_all, (1, 0, 2)).reshape(CONV_K, 3 * WIDTH)

    qn_g, kn_g, gn_g = fox_qn_g.reshape(1, LANES), fox_kn_g.reshape(1, LANES), gdn_norm_g.reshape(1, LANES)
    bvec = _lanes((0, b_fgate), (HEADS, gdn_dt_bias))
    alog = _lanes((HEADS, gdn_A_log))
    fg = final_g.reshape(1, D_MODEL)

    h_bf = _norm_mod(x2, norm_g, scale1p, shift)
    p_main, p_small = _in_proj(h_bf, w_main, w_small)
    qs, kn, vb, gq, gk, gv, small = _prep(p_main, p_small, qn_g, kn_g, conv_full, bvec, alog)
    fox_o, lse = _fox_fwd(qs, kn, vb)
    gu, gw, gqg, gkd, gattn, t_inv, eg_last = _gdn_intra_fwd(gq, gk, gv, small)
    gdn_o, states = _gdn_scan_fwd(gu, gw, gqg, gkd, gattn, eg_last)
    mixed = _mix_fwd(fox_o, gdn_o, p_main, gn_g)

    loss_row, dy1, dz, dmixed, d_final_g, d_gate = _out_head(mixed, w_out_full, x2, tgt, gate, fg)
    loss = lax.psum(loss_row[0, 0], AXES)
    dw_out = _matmul_tn("dw_out", mixed, dz, bf16)
    do_fox, delta, dfz, dgz, dgdn_o, d_gn_g = _mix_bwd(dmixed, fox_o, gdn_o, p_main, gn_g)
    dqs, dkn, dvf, df_key, df_query = _fox_bwd(qs, kn, vb, do_fox, lse, delta)
    du, dw, dqg, dkd, dattn, deg = _gdn_scan_bwd(gu, gw, gqg, gkd, gattn, eg_last, states, dgdn_o)
    dgq, dgk, dgv, d_small = _gdn_intra_bwd(gq, gk, gv, small, t_inv, du, dw, dqg, dkd, dattn, deg)
    df_small = jnp.pad(jnp.transpose(df_key.reshape(HEADS, s_len)), ((0, 0), (0, N_SMALL - HEADS)))
    dfq, dfk, dgqkv, dp_small, d_qn_g, d_kn_g, sums, d_conv = _post1(
        p_main, p_small, qn_g, kn_g, conv_full, bvec, alog, dqs, dkn, dgq, dgk, dgv, d_small, df_small, df_query)
    dp_pieces = [dfq, dfk, dvf, dfz, dgqkv, dgz]
    dh = _in_proj_bwd(dp_pieces, dp_small, w_main, w_small)
    grad_x, d_shift, d_scale, d_norm_g = _adaln_bwd(dh, x2, dy1, norm_g, scale1p)
    dw_rows = [_matmul_tn("dw_main_%d" % n, piece, h_bf, bf16) for n, piece in enumerate(dp_pieces)]
    dw_small = _matmul_tn("dw_small", dp_small, h_bf, bf16)
    dw_in_full = jnp.concatenate(dw_rows[:4] + [dw_small[:HEADS]] + dw_rows[4:] + [dw_small[HEADS:3 * HEADS]],
                                 axis=0)
    dw_in_parts = dw_in_full.reshape(N_DEV, in_cols, D_MODEL)
    dw_out_parts = dw_out.reshape(N_DEV, w_out.shape[1], D_MODEL)

    dmod = jnp.concatenate([d_shift, d_scale, d_gate], axis=1)
    small_grads = _pack([d_norm_g, dmod, sums[0, :HEADS], d_qn_g, d_kn_g, sums[1, HEADS:2 * HEADS],
                         sums[0, HEADS:2 * HEADS], d_gn_g, d_final_g], SMALL_PACK)
    conv_grad = d_conv[:CONV_K]
    pair_in, pair_out = _pair_exchange("pair_grads", [dw_in_parts, dw_out_parts])
    core = lax.axis_index("c").astype(jnp.int32).reshape(1)
    dw_in_recv, dw_out_recv = _chip_exchange(
        "chip_grads", [_pair_sum("pair_sum_w_in", dw_in_parts, pair_in, core),
                       _pair_sum("pair_sum_w_out", dw_out_parts, pair_out, core)])
    small_all, conv_all_g = _gather_direct("gather_small_grads", [small_grads, conv_grad])

    outs = {}
    to_t = lambda t: jnp.transpose(t, (0, 2, 1))
    outs["w_in"] = tuple(to_t(t) for t in _adamw("adamw_w_in", to_t(w_in), dw_in_recv, to_t(m_w_in), to_t(v_w_in),
                                                  in_cols, 256))
    outs["w_out"] = _adamw("adamw_w_out", w_out, dw_out_recv, m_w_out, v_w_out, 128)
    conv_mine = lax.dynamic_slice(jnp.transpose(conv_all_g.reshape(N_DEV, CONV_K, N_DEV, conv_cols), (0, 2, 1, 3)),
                                  (0, me, 0, 0), (N_DEV, 1, CONV_K, conv_cols)).reshape(N_DEV, CONV_K, conv_cols)
    outs["gdn_conv_w"] = _adamw("adamw_conv", gdn_conv_w, conv_mine, m_gdn_conv_w, v_gdn_conv_w, CONV_K)
    small_all = small_all.reshape(N_DEV, 1, SMALL_PACK)
    dmod_all = small_all[:, 0, D_MODEL:D_MODEL + 3 * D_MODEL]
    dmod_mine = lax.dynamic_slice(dmod_all, (0, me * ada_cols), (N_DEV, ada_cols))
    c_all_t = jnp.pad(jnp.transpose(c_all), ((0, 0), (0, LANES - N_DEV)))
    g_w_ada = _w_ada_grad(c_all_t, jnp.pad(dmod_mine, ((0, LANES - N_DEV), (0, 0))))
    outs["w_ada"] = _adamw("adamw_w_ada", w_ada, g_w_ada[None], m_w_ada, v_w_ada, 256)
    given = dict(norm_g=(norm_g, m_norm_g, v_norm_g), b_ada=(b_ada, m_b_ada, v_b_ada), b_fgate=(b_fgate, m_b_fgate, v_b_fgate),
                 fox_qn_g=(fox_qn_g, m_fox_qn_g, v_fox_qn_g), fox_kn_g=(fox_kn_g, m_fox_kn_g, v_fox_kn_g),
                 gdn_A_log=(gdn_A_log, m_gdn_A_log, v_gdn_A_log), gdn_dt_bias=(gdn_dt_bias, m_gdn_dt_bias, v_gdn_dt_bias),
                 gdn_norm_g=(gdn_norm_g, m_gdn_norm_g, v_gdn_norm_g), final_g=(final_g, m_final_g, v_final_g))
    w_pack = _pack([given[n][0] for n in SMALL_NAMES], SMALL_PACK)
    m_pack = _pack([given[n][1] for n in SMALL_NAMES], SMALL_PACK)
    v_pack = _pack([given[n][2] for n in SMALL_NAMES], SMALL_PACK)
    packed = _adamw("adamw_small", w_pack[None], small_all, m_pack[None], v_pack[None], 1)
    off = 0
    for n, size in zip(SMALL_NAMES, SMALL_SIZES):
        outs[n] = tuple(t[0, 0, off:off + size].reshape(given[n][0].shape) for t in packed)
        off += size

    order = ("norm_g", "w_ada", "b_ada", "w_in", "b_fgate", "fox_qn_g", "fox_kn_g", "gdn_conv_w", "gdn_A_log",
             "gdn_dt_bias", "gdn_norm_g", "w_out", "final_g")
    result = [loss, grad_x.reshape(x.shape)]
    for part in range(4):
        result += [outs[n][part] for n in order]
    return tuple(result)
```

```python
import math

import jax
import jax.numpy as jnp
from jax import lax
from jax.experimental import pallas as pl
from jax.experimental.pallas import tpu as pltpu

f32 = jnp.float32
bf16 = jnp.bfloat16
HI = lax.Precision.HIGHEST

N_DEV = 8
AXES = ("x", "y", "c")
D_MODEL = 2048
HEADS = 8
HEAD_DIM = 128
WIDTH = HEADS * HEAD_DIM
CHUNK = 64
CONV_K = 4
EPS = 1e-6
QK_SCALE = HEAD_DIM ** -0.5
LOG2E = 1.0 / math.log(2.0)
LN2 = math.log(2.0)
N_MAIN = 8 * WIDTH
N_SMALL = 128
LANE_F, LANE_G, LANE_BETA, LANE_GC, LANE_GLAST = 0, 8, 16, 24, 32
IN_WIDTH = 8 * WIDTH + 3 * HEADS
LANES = 128
VMEM_LIMIT = 56 * 1024 * 1024

ADAM_LR, ADAM_B1, ADAM_B2, ADAM_EPS, ADAM_WD, ADAM_STEP = 0.001, 0.9, 0.999, 1e-08, 0.01, 10


def _params(*sem):
    return pltpu.CompilerParams(dimension_semantics=sem, vmem_limit_bytes=VMEM_LIMIT)


def _iota(shape, dim):
    return lax.broadcasted_iota(jnp.int32, shape, dim)


def _sigmoid(z):
    return 1.0 / (1.0 + jnp.exp(-z))


def _softplus_parts(z):
    t = jnp.log(1.0 + jnp.exp(-jnp.abs(z)))
    return jnp.minimum(z, 0.0) - t, jnp.maximum(z, 0.0) + t


def _dg(a, b, ca, cb, prec=None):
    if a.ndim == 3:
        dims = (((ca + 1,), (cb + 1,)), ((0,), (0,)))
    else:
        dims = (((ca,), (cb,)), ((), ()))
    return lax.dot_general(a, b, dims, preferred_element_type=f32, precision=prec)


def _dot_bf16(a, b, ca, cb):
    return _dg(a.astype(bf16), b.astype(bf16), ca, cb)


def _split_bf16(a):
    hi = a.astype(bf16)
    return hi, (a - hi.astype(f32)).astype(bf16)


def _dot_3pass(a, b, ca, cb):
    a_hi, a_lo = _split_bf16(a)
    b_hi, b_lo = _split_bf16(b)
    return _dg(a_hi, b_hi, ca, cb) + (_dg(a_hi, b_lo, ca, cb) + _dg(a_lo, b_hi, ca, cb))


def _make_mm(dot):
    def nn_(a, b):
        return dot(a, b, 1, 0)

    def nt_(a, b):
        return dot(a, b, 1, 1)

    def tn_(a, b):
        return dot(a, b, 0, 0)

    @jax.custom_vjp
    def nn(a, b):
        return nn_(a, b)

    @jax.custom_vjp
    def nt(a, b):
        return nt_(a, b)

    @jax.custom_vjp
    def tn(a, b):
        return tn_(a, b)

    nn.defvjp(lambda a, b: (nn_(a, b), (a, b)), lambda r, g: (nt_(g, r[1]), tn_(r[0], g)))
    nt.defvjp(lambda a, b: (nt_(a, b), (a, b)), lambda r, g: (nn_(g, r[1]), tn_(g, r[0])))
    tn.defvjp(lambda a, b: (tn_(a, b), (a, b)), lambda r, g: (nt_(r[1], g), nn_(r[0], g)))
    return (nn_, nt_, tn_), (nn, nt, tn)


_BF_PLAIN, _BF_VJP = _make_mm(_dot_bf16)
_X3_PLAIN, _X3_VJP = _make_mm(_dot_3pass)


def _inv_unit_lower_many(ms):
    c = CHUNK
    nn = _X3_PLAIN[0]
    eye = (_iota((c, c), 0) == _iota((c, c), 1)).astype(f32)
    top = _iota((2 * c, c), 0) < c
    xs = [jnp.concatenate([eye - m, nn(m, m)], axis=0) for m in ms]
    for _ in range(int(math.log2(CHUNK)) - 2):
        xs = [jnp.where(top, x, 0.0) + nn(x, x[c:]) for x in xs]
    return [x[:c] + nn(x[:c], x[c:]) for x in xs]


@jax.custom_vjp
def _inv_given(m, t):
    return t


_inv_given.defvjp(lambda m, t: (t, t),
                  lambda t, g: (-_X3_PLAIN[1](_X3_PLAIN[2](t, g), t), jnp.zeros_like(t)))

SUBLANES = 8


def _gdn_intra_pre(q, k, v, gc_b, g_last_b, beta_b, bnt):
    c = CHUNK
    r_i, c_i = _iota((c, c), 0), _iota((c, c), 1)
    lower, strict = r_i >= c_i, r_i > c_i
    gc_i = gc_b[..., :c]
    gc_j = jnp.swapaxes(gc_i, -1, -2)
    decay = jnp.where(lower, jnp.exp(jnp.where(lower, gc_i - gc_j, 0.0)), 0.0)
    kb = k * beta_b
    both = bnt(jnp.concatenate([kb, q], axis=-2), k)
    m = jnp.where(strict, both[..., :c, :] * decay, 0.0)
    attn = jnp.where(lower, both[..., c:, :] * decay, 0.0)
    eg = jnp.exp(gc_b)
    rhs = jnp.concatenate([v * beta_b, kb * eg], axis=-1)
    k_dec = k * jnp.exp(g_last_b - gc_b)
    eg_last = jnp.exp(g_last_b[..., :SUBLANES, :])
    return m, rhs, q * eg, k_dec, attn, eg_last


def _gdn_intra(q, k, v, gc_b, g_last_b, beta_b, t_known):
    m, rhs, qg, k_dec, attn, eg_last = _gdn_intra_pre(q, k, v, gc_b, g_last_b, beta_b, _BF_VJP[1])
    return _X3_VJP[0](_inv_given(m, t_known), rhs), qg, k_dec, attn, eg_last


def _scale_rows(s, eg_last):
    return (s.reshape(HEAD_DIM // SUBLANES, SUBLANES, HEAD_DIM) * eg_last[None]).reshape(HEAD_DIM, HEAD_DIM)


def _my_index():
    return 4 * lax.axis_index("x") + 2 * lax.axis_index("y") + lax.axis_index("c")


def _peer(d):
    x, y, c = lax.axis_index("x"), lax.axis_index("y"), lax.axis_index("c")
    px, py, pc = (x + (d >> 2)) % 2, (y + ((d >> 1) & 1)) % 2, (c + (d & 1)) % 2
    return (px, py, pc), 4 * px + 2 * py + pc


def _gather_direct(name, arrays):
    n = len(arrays)

    def body(*refs):
        srcs, dsts = refs[:n], refs[n:2 * n]
        send_sems, recv_sems, local_sems = refs[2 * n:]
        me = _my_index()

        def copy(k, d, started):
            peer, pidx = _peer(d)
            return pltpu.make_async_remote_copy(
                src_ref=srcs[k], dst_ref=dsts[k].at[me if started else pidx], send_sem=send_sems.at[k * 7 + d - 1],
                recv_sem=recv_sems.at[k * 7 + d - 1], device_id=peer, device_id_type=pl.DeviceIdType.MESH)

        local = [pltpu.make_async_copy(srcs[k], dsts[k].at[me], local_sems.at[k]) for k in range(n)]
        sends = [copy(k, d, True) for k in range(n) for d in range(1, N_DEV)]
        for cp in local + sends:
            cp.start()
        for k in range(n):
            for d in range(1, N_DEV):
                copy(k, d, False).wait_recv()
        for cp in sends:
            cp.wait_send()
        for cp in local:
            cp.wait()

    out_shape = [jax.ShapeDtypeStruct((N_DEV,) + a.shape, a.dtype) for a in arrays]
    any_spec = pl.BlockSpec(memory_space=pl.ANY)
    return pl.pallas_call(
        body, name=name, out_shape=out_shape, in_specs=[any_spec] * n, out_specs=[any_spec] * n,
        scratch_shapes=[pltpu.SemaphoreType.DMA((7 * n,)), pltpu.SemaphoreType.DMA((7 * n,)),
                        pltpu.SemaphoreType.DMA((n,))],
        compiler_params=pltpu.CompilerParams(has_side_effects=True),
    )(*arrays)


N_CHIPS = 4


def _pair_exchange(name, arrays):
    n = len(arrays)

    def body(*refs):
        srcs, dsts = refs[:n], refs[n:2 * n]
        send_sems, recv_sems = refs[2 * n:]
        x, y, c = lax.axis_index("x"), lax.axis_index("y"), lax.axis_index("c")
        sibling = (x, y, 1 - c)

        def copy(k, j):
            return pltpu.make_async_remote_copy(
                src_ref=srcs[k].at[2 * j + (1 - c)], dst_ref=dsts[k].at[j], send_sem=send_sems.at[k * N_CHIPS + j],
                recv_sem=recv_sems.at[k * N_CHIPS + j], device_id=sibling, device_id_type=pl.DeviceIdType.MESH)

        copies = [copy(k, j) for k in range(n) for j in range(N_CHIPS)]
        for cp in copies:
            cp.start()
        for cp in copies:
            cp.wait_recv()
        for cp in copies:
            cp.wait_send()

    any_spec = pl.BlockSpec(memory_space=pl.ANY)
    return pl.pallas_call(
        body, name=name, out_shape=[jax.ShapeDtypeStruct((N_CHIPS,) + a.shape[1:], a.dtype) for a in arrays],
        in_specs=[any_spec] * n, out_specs=[any_spec] * n,
        scratch_shapes=[pltpu.SemaphoreType.DMA((N_CHIPS * n,)), pltpu.SemaphoreType.DMA((N_CHIPS * n,))],
        compiler_params=pltpu.CompilerParams(has_side_effects=True),
    )(*arrays)


def _chip_exchange(name, arrays):
    n = len(arrays)

    def body(*refs):
        srcs, dsts = refs[:n], refs[n:2 * n]
        send_sems, recv_sems, local_sems = refs[2 * n:]
        x, y, c = lax.axis_index("x"), lax.axis_index("y"), lax.axis_index("c")
        my_chip = 2 * x + y

        def peer(d):
            px, py = (x + (d >> 1)) % 2, (y + (d & 1)) % 2
            return (px, py, c), 2 * px + py

        def remote(k, d, started):
            to, chip = peer(d)
            return pltpu.make_async_remote_copy(
                src_ref=srcs[k].at[chip], dst_ref=dsts[k].at[my_chip if started else chip],
                send_sem=send_sems.at[k * 3 + d - 1], recv_sem=recv_sems.at[k * 3 + d - 1],
                device_id=to, device_id_type=pl.DeviceIdType.MESH)

        local = [pltpu.make_async_copy(srcs[k].at[my_chip], dsts[k].at[my_chip], local_sems.at[k]) for k in range(n)]
        sends = [remote(k, d, True) for k in range(n) for d in range(1, N_CHIPS)]
        for cp in local + sends:
            cp.start()
        for k in range(n):
            for d in range(1, N_CHIPS):
                remote(k, d, False).wait_recv()
        for cp in sends:
            cp.wait_send()
        for cp in local:
            cp.wait()

    any_spec = pl.BlockSpec(memory_space=pl.ANY)
    return pl.pallas_call(
        body, name=name, out_shape=[jax.ShapeDtypeStruct(a.shape, a.dtype) for a in arrays],
        in_specs=[any_spec] * n, out_specs=[any_spec] * n,
        scratch_shapes=[pltpu.SemaphoreType.DMA((3 * n,)), pltpu.SemaphoreType.DMA((3 * n,)),
                        pltpu.SemaphoreType.DMA((n,))],
        compiler_params=pltpu.CompilerParams(has_side_effects=True),
    )(*arrays)


def _pair_sum(name, parts, received, core):
    n_blocks, rows, cols = received.shape
    tr = rows if rows % 256 else 256

    def body(core_ref, mine_ref, recv_ref, o_ref):
        o_ref[...] = (mine_ref[...].astype(f32) + recv_ref[...].astype(f32)).astype(bf16)

    return pl.pallas_call(
        body, name=name,
        grid_spec=pltpu.PrefetchScalarGridSpec(
            num_scalar_prefetch=1, grid=(n_blocks, rows // tr),
            in_specs=[pl.BlockSpec((1, tr, cols), lambda j, i, core_ref: (2 * j + core_ref[0], i, 0)),
                      pl.BlockSpec((1, tr, cols), lambda j, i, core_ref: (j, i, 0))],
            out_specs=pl.BlockSpec((1, tr, cols), lambda j, i, core_ref: (j, i, 0))),
        out_shape=jax.ShapeDtypeStruct((n_blocks, rows, cols), bf16),
        compiler_params=_params("parallel", "parallel"),
    )(core, parts, received)


def _gather_two_level(name, arrays):
    n = len(arrays)

    def body(*refs):
        srcs, dsts = refs[:n], refs[n:2 * n]
        send_sems, recv_sems, local_sems = refs[2 * n:]
        x, y, c = lax.axis_index("x"), lax.axis_index("y"), lax.axis_index("c")
        sibling = (x, y, 1 - c)
        near = ((x + 1 - c) % 2, (y + c) % 2)
        far = ((x + c) % 2, (y + 1 - c) % 2)
        diag = ((x + 1) % 2, (y + 1) % 2)
        near_slot, far_slot = 1 + c, 2 - c

        def index(px, py, pc):
            return 4 * px + 2 * py + pc

        def copy(k, slot, block, to, src=None):
            return pltpu.make_async_remote_copy(
                src_ref=dsts[k].at[index(*block)] if src is None else src, dst_ref=dsts[k].at[index(*block)],
                send_sem=send_sems.at[k * 7 + slot], recv_sem=recv_sems.at[k * 7 + slot],
                device_id=to, device_id_type=pl.DeviceIdType.MESH)

        me = (x, y, c)
        local = [pltpu.make_async_copy(srcs[k], dsts[k].at[index(*me)], local_sems.at[k]) for k in range(n)]
        started = [copy(k, 0, me, sibling, src=srcs[k]) for k in range(n)]
        started += [copy(k, near_slot, me, (*near, c), src=srcs[k]) for k in range(n)]
        started += [copy(k, far_slot, me, (*far, c), src=srcs[k]) for k in range(n)]
        for cp in local + started:
            cp.start()
        for k in range(n):
            copy(k, near_slot, (*near, c), me).wait_recv()
            passed = [copy(k, 3, (*near, c), (*far, c)), copy(k, 3 + near_slot, (*near, c), sibling)]
            for cp in passed:
                cp.start()
            started += passed
        for slot, chip in ((far_slot, far), (3, diag)):
            for k in range(n):
                copy(k, slot, (*chip, c), me).wait_recv()
                passed = copy(k, 3 + slot, (*chip, c), sibling)
                passed.start()
                started.append(passed)
        for k in range(n):
            copy(k, 0, sibling, me).wait_recv()
            for slot, chip in ((near_slot, near), (far_slot, far), (3, diag)):
                copy(k, 3 + slot, (*chip, 1 - c), me).wait_recv()
        for cp in started:
            cp.wait_send()
        for cp in local:
            cp.wait()

    any_spec = pl.BlockSpec(memory_space=pl.ANY)
    return pl.pallas_call(
        body, name=name, out_shape=[jax.ShapeDtypeStruct((N_DEV,) + a.shape, a.dtype) for a in arrays],
        in_specs=[any_spec] * n, out_specs=[any_spec] * n,
        scratch_shapes=[pltpu.SemaphoreType.DMA((7 * n,)), pltpu.SemaphoreType.DMA((7 * n,)),
                        pltpu.SemaphoreType.DMA((n,))],
        compiler_params=pltpu.CompilerParams(has_side_effects=True),
    )(*arrays)


def _mod_shard(c_all, w_ada, b_shard):
    def body(c_ref, w_ref, b_ref, o_ref):
        cv = c_ref[...]
        ca = cv * _sigmoid(cv)
        o_ref[...] = jnp.dot(ca.astype(bf16), w_ref[...].astype(bf16), preferred_element_type=f32) + b_ref[...]

    return pl.pallas_call(body, name="mod_shard", out_shape=jax.ShapeDtypeStruct((N_DEV, w_ada.shape[1]), f32),
                          compiler_params=_params())(c_all, w_ada, b_shard)


def _norm_mod(x, norm_g, scale1p, shift):
    s_len = x.shape[0]
    tm = 512

    def body(x_ref, g_ref, sc_ref, sh_ref, h_ref):
        xb = x_ref[...]
        r = lax.rsqrt(jnp.mean(xb * xb, axis=-1, keepdims=True) + EPS)
        h_ref[...] = ((xb * r * g_ref[...]) * sc_ref[...] + sh_ref[...]).astype(bf16)

    row = pl.BlockSpec((tm, D_MODEL), lambda i: (i, 0))
    vec = pl.BlockSpec((1, D_MODEL), lambda i: (0, 0))
    return pl.pallas_call(body, name="norm_mod", grid=(s_len // tm,), in_specs=[row, vec, vec, vec], out_specs=row,
                          out_shape=jax.ShapeDtypeStruct((s_len, D_MODEL), bf16),
                          compiler_params=_params("parallel"))(x, norm_g, scale1p, shift)


def _in_proj(h, wt_main, wt_small):
    s_len = h.shape[0]
    tm, tn = min(1024, s_len), 1024

    def body(h_ref, w_ref, ws_ref, p_ref, ps_ref):
        @pl.when(pl.program_id(1) == 0)
        def _():
            ps_ref[...] = _dg(h_ref[...], ws_ref[...], 1, 1)

        p_ref[...] = _dg(h_ref[...], w_ref[...], 1, 1)

    return pl.pallas_call(
        body, name="in_proj", grid=(s_len // tm, N_MAIN // tn),
        in_specs=[pl.BlockSpec((tm, D_MODEL), lambda i, j: (i, 0)),
                  pl.BlockSpec((tn, D_MODEL), lambda i, j: (j, 0)),
                  pl.BlockSpec((N_SMALL, D_MODEL), lambda i, j: (0, 0))],
        out_specs=[pl.BlockSpec((tm, tn), lambda i, j: (i, j)),
                   pl.BlockSpec((tm, N_SMALL), lambda i, j: (i, 0))],
        out_shape=[jax.ShapeDtypeStruct((s_len, N_MAIN), f32), jax.ShapeDtypeStruct((s_len, N_SMALL), f32)],
        compiler_params=_params("parallel", "arbitrary"),
    )(h, wt_main, wt_small)


PREP_TM = 256
HALO = 8


def _conv_section(xe_ref, cw_ref, cols, tm):
    acc = cw_ref[pl.ds(CONV_K - 1, 1), cols] * xe_ref[pl.ds(HALO, tm), :]
    for tap in range(CONV_K - 1):
        acc = acc + cw_ref[pl.ds(tap, 1), cols] * xe_ref[pl.ds(HALO - (CONV_K - 1) + tap, tm), :]
    return acc


def _small_fwd(ps, bvec, alog):
    z = ps + bvec
    logsig, softp = _softplus_parts(z)
    gval = -jnp.exp(alog) * softp
    beta = _sigmoid(ps)
    return z, logsig, gval, beta


def _head_lane(block, lane):
    return jnp.sum(jnp.where(_iota(block.shape, 1) == lane, block, 0.0), axis=1, keepdims=True)


def _head_slab(block, lane):
    return jnp.broadcast_to(_head_lane(block, lane), block.shape)


def _chunk_masks(tm):
    r, c = _iota((tm, tm), 0), _iota((tm, tm), 1)
    same = (r // CHUNK) == (c // CHUNK)
    return (same & (r >= c)).astype(f32), same.astype(f32)


def _prep(p_main, p_small, qn_g, kn_g, conv_w, bvec, alog):
    s_len = p_main.shape[0]
    tm = PREP_TM
    nb = s_len // tm

    def body(fq_ref, fk_ref, fv_ref, gq_ref, gk_ref, gv_ref, hq_ref, hk_ref, hv_ref, ps_ref, qg_ref, kg_ref,
             cw_ref, bv_ref, al_ref,
             qs_ref, kn_ref, vb_ref, gqo_ref, gko_ref, gvo_ref, small_ref, xe_sc, carry_sc):
        i = pl.program_id(0)

        @pl.when(i == 0)
        def _():
            carry_sc[...] = jnp.zeros_like(carry_sc)

        vb_ref[...] = fv_ref[...].astype(bf16)

        first = i == 0
        for sec, (x_ref, halo_ref, o_ref) in enumerate(((gq_ref, hq_ref, gqo_ref), (gk_ref, hk_ref, gko_ref),
                                                        (gv_ref, hv_ref, gvo_ref))):
            xe_sc[0:HALO, :] = jnp.where(first, 0.0, halo_ref[...])
            xe_sc[HALO:, :] = x_ref[...]
            cv = _conv_section(xe_sc, cw_ref, slice(sec * WIDTH, (sec + 1) * WIDTH), tm)
            y = cv * _sigmoid(cv)
            if sec == 2:
                o_ref[...] = y
            else:
                mul = QK_SCALE if sec == 0 else 1.0
                for h in range(HEADS):
                    sl = slice(h * HEAD_DIM, (h + 1) * HEAD_DIM)
                    yh = y[:, sl]
                    o_ref[:, sl] = yh * (lax.rsqrt(jnp.sum(yh * yh, axis=-1, keepdims=True) + EPS) * mul)

        lane = _iota((tm, N_SMALL), 1)
        _, logsig, gval, beta = _small_fwd(ps_ref[...], bv_ref[...], al_ref[...])
        lf = jnp.where(lane < HEADS, logsig, 0.0)
        tri = (_iota((tm, tm), 0) >= _iota((tm, tm), 1)).astype(f32)
        fcum = jnp.dot(tri, lf, preferred_element_type=f32, precision=HI) + carry_sc[...]
        carry_sc[...] += jnp.sum(lf, axis=0, keepdims=True)
        tri_c, ones_c = _chunk_masks(tm)
        g_lanes = jnp.where((lane >= LANE_G) & (lane < LANE_G + HEADS), gval, 0.0)
        gc = jnp.dot(tri_c, g_lanes, preferred_element_type=f32, precision=HI)
        g_last = jnp.dot(ones_c, g_lanes, preferred_element_type=f32, precision=HI)
        small = jnp.where(lane < LANE_G, fcum, jnp.where(lane < LANE_BETA, gval, jnp.where(lane < LANE_GC, beta, 0.0)))
        small_ref[...] = small + pltpu.roll(gc, LANE_GC - LANE_G, 1) + pltpu.roll(g_last, LANE_GLAST - LANE_G, 1)

        qg, kg = qg_ref[...], kg_ref[...]
        f2 = fcum * LOG2E
        for h in range(HEADS):
            sl = slice(h * HEAD_DIM, (h + 1) * HEAD_DIM)
            q = fq_ref[:, sl]
            rq = lax.rsqrt(jnp.mean(q * q, axis=-1, keepdims=True) + EPS)
            k = fk_ref[:, sl]
            rk = lax.rsqrt(jnp.mean(k * k, axis=-1, keepdims=True) + EPS)
            f_col = _head_lane(f2, LANE_F + h)
            hi = f_col.astype(bf16).astype(f32)
            mid = (f_col - hi).astype(bf16).astype(f32)
            lo = f_col - hi - mid
            q_bias = jnp.where(lane == 0, hi, jnp.where(lane == 1, mid, jnp.where(lane == 2, lo,
                                                                                  jnp.where(lane < 6, 1.0, 0.0))))
            k_bias = jnp.where(lane < 3, 1.0, jnp.where(lane == 3, -hi, jnp.where(lane == 4, -mid,
                                                                                 jnp.where(lane == 5, -lo, 0.0))))
            base = 2 * h * HEAD_DIM
            qs_ref[:, base:base + HEAD_DIM] = (q * rq * qg * (QK_SCALE * LOG2E)).astype(bf16)
            qs_ref[:, base + HEAD_DIM:base + 2 * HEAD_DIM] = q_bias.astype(bf16)
            kn_ref[:, base:base + HEAD_DIM] = (k * rk * kg).astype(bf16)
            kn_ref[:, base + HEAD_DIM:base + 2 * HEAD_DIM] = k_bias.astype(bf16)

    def col(cb):
        return pl.BlockSpec((tm, WIDTH), lambda i: (i, cb))

    def halo(cb):
        return pl.BlockSpec((HALO, WIDTH), lambda i: (jnp.maximum(i * (tm // HALO) - 1, 0), cb))

    vec = pl.BlockSpec((1, LANES), lambda i: (0, 0))
    wide_f32 = jax.ShapeDtypeStruct((s_len, WIDTH), f32)
    wide_bf = jax.ShapeDtypeStruct((s_len, WIDTH), bf16)
    out_col = pl.BlockSpec((tm, WIDTH), lambda i: (i, 0))
    return pl.pallas_call(
        body, name="prep", grid=(nb,),
        in_specs=[col(0), col(1), col(2), col(4), col(5), col(6), halo(4), halo(5), halo(6),
                  pl.BlockSpec((tm, N_SMALL), lambda i: (i, 0)), vec, vec,
                  pl.BlockSpec((CONV_K, 3 * WIDTH), lambda i: (0, 0)), vec, vec],
        out_specs=[pl.BlockSpec((tm, 2 * WIDTH), lambda i: (i, 0))] * 2 + [out_col] * 4
                  + [pl.BlockSpec((tm, N_SMALL), lambda i: (i, 0))],
        out_shape=[jax.ShapeDtypeStruct((s_len, 2 * WIDTH), bf16)] * 2 + [wide_bf, wide_f32, wide_f32, wide_f32,
                                                                          jax.ShapeDtypeStruct((s_len, N_SMALL), f32)],
        scratch_shapes=[pltpu.VMEM((tm + HALO, WIDTH), f32), pltpu.VMEM((1, N_SMALL), f32)],
        compiler_params=_params("arbitrary"),
    )(p_main, p_main, p_main, p_main, p_main, p_main, p_main, p_main, p_main, p_small, qn_g, kn_g, conv_w, bvec, alog)


FOX_T = 1024
NEG_BIG = -1e30


def _fox_fwd(qs, kn, vb):
    s_len = qs.shape[0]
    t = FOX_T
    nq = s_len // t

    def body(q_ref, k_ref, v_ref, o_ref, lse_ref):
        qi = pl.program_id(1)
        q = q_ref[...]

        causal = _iota((t, t), 0) >= _iota((t, t), 1)

        def step(j, carry, masked):
            m, l, acc = carry
            rows = pl.ds(pl.multiple_of(j * t, t), t)
            s = _dg(q, k_ref[rows, :], 1, 1)
            if masked:
                s = jnp.where(causal, s, NEG_BIG)
            m_new = jnp.maximum(m, jnp.max(s, axis=-1, keepdims=True))
            p = jnp.exp2(s - m_new)
            alpha = jnp.exp2(m - m_new)
            l = alpha * l + jnp.sum(p, axis=-1, keepdims=True)
            acc = alpha * acc + jnp.dot(p.astype(bf16), v_ref[rows, :], preferred_element_type=f32)
            return m_new, l, acc

        init = (jnp.full((t, 1), NEG_BIG, f32), jnp.zeros((t, 1), f32), jnp.zeros((t, HEAD_DIM), f32))
        carry = lax.fori_loop(0, qi, lambda j, c: step(j, c, False), init)
        m, l, acc = step(qi, carry, True)
        o_ref[...] = acc / l
        lse_ref[0] = m + jnp.log2(l)

    return pl.pallas_call(
        body, name="fox_fwd", grid=(HEADS, nq),
        in_specs=[pl.BlockSpec((t, 2 * HEAD_DIM), lambda h, i: (i, h)),
                  pl.BlockSpec((s_len, 2 * HEAD_DIM), lambda h, i: (0, h)),
                  pl.BlockSpec((s_len, HEAD_DIM), lambda h, i: (0, h))],
        out_specs=[pl.BlockSpec((t, HEAD_DIM), lambda h, i: (i, h)),
                   pl.BlockSpec((1, t, 1), lambda h, i: (h, i, 0))],
        out_shape=[jax.ShapeDtypeStruct((s_len, WIDTH), f32), jax.ShapeDtypeStruct((HEADS, s_len, 1), f32)],
        compiler_params=_params("parallel", "arbitrary"),
    )(qs, kn, vb)


def _fox_bwd(qs, kn, vb, do, lse, delta):
    s_len = qs.shape[0]
    t = FOX_T
    nq = s_len // t
    half = t // 2

    def body(q_ref, do_ref, lse_ref, dl_ref, k_ref, v_ref, dq_ref, dk_ref, dvb_ref, df_ref, dfq_ref, dv_ref):
        head, qi = pl.program_id(0), pl.program_id(1)

        @pl.when(qi == 0)
        def _():
            dk_ref[...] = jnp.zeros_like(dk_ref)
            dv_ref[...] = jnp.zeros_like(dv_ref)
            df_ref[...] = jnp.zeros_like(df_ref)

        lse_col = lse_ref[0]
        dl = _head_lane(dl_ref[...], head)

        def update(q_rows, k_rows, df_lanes, j, carry, mask):
            dq, row_sum = carry
            q, do_b = q_ref[q_rows, :], do_ref[q_rows, :]
            p = jnp.exp2(_dg(q, k_ref[k_rows, :], 1, 1) - lse_col[q_rows])
            if mask is not None:
                p = jnp.where(mask, p, 0.0)
            ds = p * (_dg(do_b, v_ref[k_rows, :], 1, 1) - dl[q_rows])
            ds_b = ds.astype(bf16)
            dk_ref[k_rows, :] += _dg(ds_b, q_ref[q_rows, 0:HEAD_DIM], 0, 0)
            dv_ref[k_rows, :] += _dg(p.astype(bf16), do_b, 0, 0)
            df_ref[0, j, :, df_lanes] += -jnp.sum(ds, axis=0, keepdims=True)
            dq = dq + jnp.dot(ds_b, k_ref[k_rows, 0:HEAD_DIM], preferred_element_type=f32)
            return dq, row_sum + jnp.sum(ds, axis=-1, keepdims=True)

        everything, upper, lower = slice(0, t), slice(0, half), slice(half, t)
        carry = lax.fori_loop(
            0, qi, lambda j, c: update(everything, pl.ds(pl.multiple_of(j * t, t), t), everything, j, c, None),
            (jnp.zeros((t, HEAD_DIM), f32), jnp.zeros((t, 1), f32)))
        carry = update(everything, pl.ds(pl.multiple_of(qi * t, t), half), upper, qi, carry,
                       _iota((t, half), 0) >= _iota((t, half), 1))
        low = update(lower, pl.ds(pl.multiple_of(qi * t + half, half), half), lower, qi,
                     tuple(c[half:] for c in carry), _iota((half, half), 0) >= _iota((half, half), 1))
        dq, row_sum = (jnp.concatenate([c[:half], lo], axis=0) for c, lo in zip(carry, low))
        dq_ref[...] = dq
        dfq_ref[0] = row_sum

        @pl.when(qi == nq - 1)
        def _():
            dvb_ref[...] = dv_ref[...].astype(bf16)

    blk = pl.BlockSpec((t, HEAD_DIM), lambda h, i: (i, h))
    blk2 = pl.BlockSpec((t, 2 * HEAD_DIM), lambda h, i: (i, h))
    full = pl.BlockSpec((s_len, HEAD_DIM), lambda h, i: (0, h))
    full2 = pl.BlockSpec((s_len, 2 * HEAD_DIM), lambda h, i: (0, h))
    colv = pl.BlockSpec((1, t, 1), lambda h, i: (h, i, 0))
    rowv = pl.BlockSpec((1, nq, 1, t), lambda h, i: (h, 0, 0, 0))
    lanes = pl.BlockSpec((t, N_SMALL), lambda h, i: (i, 0))
    wide = jax.ShapeDtypeStruct((s_len, WIDTH), f32)
    return pl.pallas_call(
        body, name="fox_bwd", grid=(HEADS, nq),
        in_specs=[blk2, blk, colv, lanes, full2, full],
        out_specs=[blk, full, full, rowv, colv],
        out_shape=[wide, wide, jax.ShapeDtypeStruct((s_len, WIDTH), bf16), jax.ShapeDtypeStruct((HEADS, nq, 1, t), f32),
                   jax.ShapeDtypeStruct((HEADS, s_len, 1), f32)],
        scratch_shapes=[pltpu.VMEM((s_len, HEAD_DIM), f32)],
        compiler_params=_params("parallel", "arbitrary"),
    )(qs, do, lse, delta, kn, vb)


INTRA_CHUNKS = 8
SCAN_FWD_CHUNKS = 8
SCAN_BWD_CHUNKS = 4


def _gdn_intra_fwd(gq, gk, gv, small):
    s_len = gq.shape[0]
    cpb = INTRA_CHUNKS
    rows_blk = cpb * CHUNK
    n_chunks = s_len // CHUNK

    def body(q_ref, k_ref, v_ref, sm_ref, u_ref, w_ref, qg_ref, kd_ref, attn_ref, t_ref, eg_ref):
        head = pl.program_id(0)
        sm = sm_ref[...]
        gc_b, gl_b, beta_b = (_head_slab(sm, LANE_GC + head), _head_slab(sm, LANE_GLAST + head),
                              _head_slab(sm, LANE_BETA + head))
        ms, rhss = [], []
        for ci in range(cpb):
            rows = pl.ds(ci * CHUNK, CHUNK)
            sl = slice(ci * CHUNK, (ci + 1) * CHUNK)
            m, rhs, qg, kd, attn, eg_last = _gdn_intra_pre(q_ref[rows, :], k_ref[rows, :], v_ref[rows, :],
                                                           gc_b[sl], gl_b[sl], beta_b[sl], _BF_PLAIN[1])
            qg_ref[rows, :] = qg.astype(bf16)
            kd_ref[rows, :] = kd.astype(bf16)
            attn_ref[0, ci] = attn.astype(bf16)
            eg_ref[0, ci] = eg_last
            ms.append(m)
            rhss.append(rhs)
        for ci, (t, rhs) in enumerate(zip(_inv_unit_lower_many(ms), rhss)):
            rows = pl.ds(ci * CHUNK, CHUNK)
            t_ref[0, ci] = t
            uw = _X3_PLAIN[0](t, rhs)
            u_ref[rows, :] = uw[:, :HEAD_DIM]
            w_ref[rows, :] = uw[:, HEAD_DIM:].astype(bf16)

    blk = pl.BlockSpec((rows_blk, HEAD_DIM), lambda h, i: (i, h))
    sq = pl.BlockSpec((1, cpb, CHUNK, CHUNK), lambda h, i: (h, i, 0, 0))
    wide_bf = jax.ShapeDtypeStruct((s_len, WIDTH), bf16)
    return pl.pallas_call(
        body, name="gdn_intra_fwd", grid=(HEADS, s_len // rows_blk),
        in_specs=[blk] * 3 + [pl.BlockSpec((rows_blk, N_SMALL), lambda h, i: (i, 0))],
        out_specs=[blk] * 4 + [sq, sq, pl.BlockSpec((1, cpb, SUBLANES, HEAD_DIM), lambda h, i: (h, i, 0, 0))],
        out_shape=[jax.ShapeDtypeStruct((s_len, WIDTH), f32), wide_bf, wide_bf, wide_bf,
                   jax.ShapeDtypeStruct((HEADS, n_chunks, CHUNK, CHUNK), bf16),
                   jax.ShapeDtypeStruct((HEADS, n_chunks, CHUNK, CHUNK), f32),
                   jax.ShapeDtypeStruct((HEADS, n_chunks, SUBLANES, HEAD_DIM), f32)],
        compiler_params=_params("parallel", "parallel"),
    )(gq, gk, gv, small)


def _gdn_scan_fwd(u, w, qg, kd, attn, eg):
    s_len = u.shape[0]
    cpb = SCAN_FWD_CHUNKS
    rows_blk = cpb * CHUNK
    n_chunks = s_len // CHUNK

    def body(u_ref, w_ref, qg_ref, kd_ref, attn_ref, eg_ref, o_ref, st_ref, s_sc):
        @pl.when(pl.program_id(0) == 0)
        def _():
            s_sc[...] = jnp.zeros_like(s_sc)

        def chunk(ci, _):
            rows = pl.ds(pl.multiple_of(ci * CHUNK, CHUNK), CHUNK)
            cols = [slice(h * HEAD_DIM, (h + 1) * HEAD_DIM) for h in range(HEADS)]
            s0 = [s_sc[h] for h in range(HEADS)]
            s0_b = [s.astype(bf16) for s in s0]
            for h in range(HEADS):
                st_ref[h, ci] = s0[h]
            ws = [jnp.dot(w_ref[rows, cols[h]], s0_b[h], preferred_element_type=f32) for h in range(HEADS)]
            qs = [jnp.dot(qg_ref[rows, cols[h]], s0_b[h], preferred_element_type=f32) for h in range(HEADS)]
            vn_b = [(u_ref[rows, cols[h]] - ws[h]).astype(bf16) for h in range(HEADS)]
            av = [jnp.dot(attn_ref[h, ci], vn_b[h], preferred_element_type=f32) for h in range(HEADS)]
            kv = [_dg(kd_ref[rows, cols[h]], vn_b[h], 0, 0) for h in range(HEADS)]
            for h in range(HEADS):
                o_ref[rows, cols[h]] = qs[h] + av[h]
                s_sc[h] = _scale_rows(s0[h], eg_ref[h, ci]) + kv[h]
            return 0

        lax.fori_loop(0, cpb, chunk, 0)

    row = pl.BlockSpec((rows_blk, WIDTH), lambda i: (i, 0))
    return pl.pallas_call(
        body, name="gdn_scan_fwd", grid=(s_len // rows_blk,),
        in_specs=[row] * 4 + [pl.BlockSpec((HEADS, cpb, CHUNK, CHUNK), lambda i: (0, i, 0, 0)),
                              pl.BlockSpec((HEADS, cpb, SUBLANES, HEAD_DIM), lambda i: (0, i, 0, 0))],
        out_specs=[row, pl.BlockSpec((HEADS, cpb, HEAD_DIM, HEAD_DIM), lambda i: (0, i, 0, 0))],
        out_shape=[jax.ShapeDtypeStruct((s_len, WIDTH), f32),
                   jax.ShapeDtypeStruct((HEADS, n_chunks, HEAD_DIM, HEAD_DIM), f32)],
        scratch_shapes=[pltpu.VMEM((HEADS, HEAD_DIM, HEAD_DIM), f32)],
        compiler_params=_params("arbitrary"),
    )(u, w, qg, kd, attn, eg)


def _gdn_scan_bwd(u, w, qg, kd, attn, eg, states, d_o):
    s_len = u.shape[0]
    cpb = SCAN_BWD_CHUNKS
    rows_blk = cpb * CHUNK
    n_chunks = s_len // CHUNK
    nb = s_len // rows_blk

    def body(u_ref, w_ref, qg_ref, kd_ref, attn_ref, eg_ref, st_ref, do_ref,
             du_ref, dw_ref, dqg_ref, dkd_ref, dattn_ref, deg_ref, ds_sc):
        @pl.when(pl.program_id(0) == 0)
        def _():
            ds_sc[...] = jnp.zeros_like(ds_sc)

        def chunk(step, _):
            ci = cpb - 1 - step
            rows = pl.ds(pl.multiple_of(ci * CHUNK, CHUNK), CHUNK)
            hs = range(HEADS)
            cols = [slice(h * HEAD_DIM, (h + 1) * HEAD_DIM) for h in hs]
            s0 = [st_ref[h, ci] for h in hs]
            s0_b = [s.astype(bf16) for s in s0]
            ds1 = [ds_sc[h] for h in hs]
            ds1_b = [d.astype(bf16) for d in ds1]
            do_b = [do_ref[rows, cols[h]].astype(bf16) for h in hs]
            ws = [jnp.dot(w_ref[rows, cols[h]], s0_b[h], preferred_element_type=f32) for h in hs]
            ad = [_dg(attn_ref[h, ci], do_b[h], 0, 0) for h in hs]
            kd_ds = [jnp.dot(kd_ref[rows, cols[h]], ds1_b[h], preferred_element_type=f32) for h in hs]
            dqg = [_dg(do_b[h], s0_b[h], 1, 1) for h in hs]
            qd = [_dg(qg_ref[rows, cols[h]], do_b[h], 0, 0) for h in hs]
            vn_b = [(u_ref[rows, cols[h]] - ws[h]).astype(bf16) for h in hs]
            dvn = [ad[h] + kd_ds[h] for h in hs]
            dvn_b = [d.astype(bf16) for d in dvn]
            dattn = [_dg(do_b[h], vn_b[h], 1, 1) for h in hs]
            dkd = [_dg(vn_b[h], ds1_b[h], 1, 1) for h in hs]
            dw = [_dg(dvn_b[h], s0_b[h], 1, 1) for h in hs]
            wd = [_dg(w_ref[rows, cols[h]], dvn_b[h], 0, 0) for h in hs]
            for h in hs:
                dattn_ref[h, ci] = dattn[h]
                dqg_ref[rows, cols[h]] = dqg[h]
                dkd_ref[rows, cols[h]] = dkd[h]
                du_ref[rows, cols[h]] = dvn[h]
                dw_ref[rows, cols[h]] = -dw[h]
                ds_sc[h] = qd[h] - wd[h] + _scale_rows(ds1[h], eg_ref[h, ci])
                deg_ref[h, ci] = jnp.sum((ds1[h] * s0[h]).reshape(HEAD_DIM // SUBLANES, SUBLANES, HEAD_DIM), axis=0)
            return 0

        lax.fori_loop(0, cpb, chunk, 0)

    row = pl.BlockSpec((rows_blk, WIDTH), lambda i: (nb - 1 - i, 0))
    sq = pl.BlockSpec((HEADS, cpb, CHUNK, CHUNK), lambda i: (0, nb - 1 - i, 0, 0))
    egs = pl.BlockSpec((HEADS, cpb, SUBLANES, HEAD_DIM), lambda i: (0, nb - 1 - i, 0, 0))
    wide = jax.ShapeDtypeStruct((s_len, WIDTH), f32)
    return pl.pallas_call(
        body, name="gdn_scan_bwd", grid=(nb,),
        in_specs=[row] * 4 + [sq, egs, pl.BlockSpec((HEADS, cpb, HEAD_DIM, HEAD_DIM), lambda i: (0, nb - 1 - i, 0, 0)), row],
        out_specs=[row] * 4 + [sq, egs],
        out_shape=[wide] * 4 + [jax.ShapeDtypeStruct((HEADS, n_chunks, CHUNK, CHUNK), f32),
                                jax.ShapeDtypeStruct((HEADS, n_chunks, SUBLANES, HEAD_DIM), f32)],
        scratch_shapes=[pltpu.VMEM((HEADS, HEAD_DIM, HEAD_DIM), f32)],
        compiler_params=_params("arbitrary"),
    )(u, w, qg, kd, attn, eg, states, d_o)


def _gdn_intra_bwd(gq, gk, gv, small, t_inv, du, dw, dqg, dkd, dattn, deg):
    s_len = gq.shape[0]
    cpb = INTRA_CHUNKS
    rows_blk = cpb * CHUNK

    def body(q_ref, k_ref, v_ref, sm_ref, t_ref, du_ref, dw_ref, dqg_ref, dkd_ref, dattn_ref, deg_ref,
             dq_ref, dk_ref, dv_ref, dsm_ref):
        head = pl.program_id(1)

        def batch(value):
            return value.reshape(cpb, CHUNK, HEAD_DIM)

        sm = sm_ref[...]
        slabs = [batch(_head_slab(sm, first + head)) for first in (LANE_GC, LANE_GLAST, LANE_BETA)]
        t_known = t_ref[0]
        _, vjp = jax.vjp(lambda q, k, v, gc, gl, b: _gdn_intra(q, k, v, gc, gl, b, t_known),
                         batch(q_ref[...]), batch(k_ref[...]), batch(v_ref[...]), *slabs)
        duw = jnp.concatenate([batch(du_ref[...]), batch(dw_ref[...])], axis=-1)
        dq, dk, dv, dgc, dgl, db = vjp((duw, batch(dqg_ref[...]), batch(dkd_ref[...]), dattn_ref[0], deg_ref[0]))
        for ref, grad in zip((dq_ref, dk_ref, dv_ref), (dq, dk, dv)):
            ref[...] = grad.reshape(rows_blk, HEAD_DIM)

        @pl.when(head == 0)
        def _():
            dsm_ref[...] = jnp.zeros_like(dsm_ref)

        lane = _iota((rows_blk, N_SMALL), 1)
        acc = dsm_ref[...]
        for first, grad in ((LANE_GC, dgc), (LANE_GLAST, dgl), (LANE_BETA, db)):
            col = jnp.sum(grad.reshape(rows_blk, HEAD_DIM), axis=1, keepdims=True)
            acc = acc + jnp.where(lane == first + head, col, 0.0)
        dsm_ref[...] = acc

    blk = pl.BlockSpec((rows_blk, HEAD_DIM), lambda i, h: (i, h))
    sq = pl.BlockSpec((1, cpb, CHUNK, CHUNK), lambda i, h: (h, i, 0, 0))
    egs = pl.BlockSpec((1, cpb, SUBLANES, HEAD_DIM), lambda i, h: (h, i, 0, 0))
    lanes = pl.BlockSpec((rows_blk, N_SMALL), lambda i, h: (i, 0))
    wide = jax.ShapeDtypeStruct((s_len, WIDTH), f32)
    return pl.pallas_call(
        body, name="gdn_intra_bwd", grid=(s_len // rows_blk, HEADS),
        in_specs=[blk] * 3 + [lanes, sq] + [blk] * 4 + [sq, egs],
        out_specs=[blk] * 3 + [lanes],
        out_shape=[wide] * 3 + [jax.ShapeDtypeStruct((s_len, N_SMALL), f32)],
        compiler_params=_params("parallel", "arbitrary"),
    )(gq, gk, gv, small, t_inv, du, dw, dqg, dkd, dattn, deg)


MIX_TM = 512


def _mix_fwd(fox_o, gdn_o, p_main, gnorm_g):
    s_len = fox_o.shape[0]
    tm = MIX_TM

    def body(fo_ref, go_ref, fz_ref, gz_ref, g_ref, mixed_ref):
        fz = fz_ref[...]
        mixed_ref[:, 0:WIDTH] = (fo_ref[...] * (fz * _sigmoid(fz))).astype(bf16)
        gz = gz_ref[...]
        gate = gz * _sigmoid(gz)
        gg = g_ref[...]
        for h in range(HEADS):
            sl = slice(h * HEAD_DIM, (h + 1) * HEAD_DIM)
            o = go_ref[:, sl]
            r = lax.rsqrt(jnp.mean(o * o, axis=-1, keepdims=True) + EPS)
            mixed_ref[:, WIDTH + h * HEAD_DIM:WIDTH + (h + 1) * HEAD_DIM] = (o * r * gg * gate[:, sl]).astype(bf16)

    row = pl.BlockSpec((tm, WIDTH), lambda i: (i, 0))
    return pl.pallas_call(
        body, name="mix_fwd", grid=(s_len // tm,),
        in_specs=[row, row, pl.BlockSpec((tm, WIDTH), lambda i: (i, 3)), pl.BlockSpec((tm, WIDTH), lambda i: (i, 7)),
                  pl.BlockSpec((1, LANES), lambda i: (0, 0))],
        out_specs=pl.BlockSpec((tm, 2 * WIDTH), lambda i: (i, 0)),
        out_shape=jax.ShapeDtypeStruct((s_len, 2 * WIDTH), bf16),
        compiler_params=_params("parallel"),
    )(fox_o, gdn_o, p_main, p_main, gnorm_g)


def _silu_and_grad(z):
    sg = _sigmoid(z)
    return z * sg, sg * (1.0 + z * (1.0 - sg))


def _mix_bwd(dmixed, fox_o, gdn_o, p_main, gnorm_g):
    s_len = fox_o.shape[0]
    tm = MIX_TM

    def body(dm_ref, fo_ref, go_ref, fz_ref, gz_ref, g_ref, dof_ref, delta_ref, dfz_ref, dgz_ref, dgo_ref, dg_ref):
        @pl.when(pl.program_id(0) == 0)
        def _():
            dg_ref[...] = jnp.zeros_like(dg_ref)

        lane = _iota((tm, LANES), 1)
        fz = fz_ref[...]
        dmf = dm_ref[:, 0:WIDTH]
        fo = fo_ref[...]
        f_gate, f_grad = _silu_and_grad(fz)
        dof = dmf * f_gate
        dof_ref[...] = dof.astype(bf16)
        dfz_ref[...] = (dmf * fo * f_grad).astype(bf16)
        prod = dof * fo
        delta = jnp.zeros((tm, LANES), f32)
        for h in range(HEADS):
            dh = jnp.sum(prod[:, h * HEAD_DIM:(h + 1) * HEAD_DIM], axis=-1, keepdims=True)
            delta = jnp.where(lane == h, dh, delta)
        delta_ref[...] = delta

        gz = gz_ref[...]
        dmg = dm_ref[:, WIDTH:2 * WIDTH]
        gate, sgrad = _silu_and_grad(gz)
        gg = g_ref[...]
        dg_acc = jnp.zeros((1, HEAD_DIM), f32)
        for h in range(HEADS):
            sl = slice(h * HEAD_DIM, (h + 1) * HEAD_DIM)
            o = go_ref[:, sl]
            r = lax.rsqrt(jnp.mean(o * o, axis=-1, keepdims=True) + EPS)
            on = o * r
            dmh = dmg[:, sl]
            dgz_ref[:, sl] = (dmh * (on * gg) * sgrad[:, sl]).astype(bf16)
            dy = dmh * gate[:, sl]
            dg_acc = dg_acc + jnp.sum(dy * on, axis=0, keepdims=True)
            tt = dy * gg
            dgo_ref[:, sl] = r * (tt - on * jnp.mean(tt * on, axis=-1, keepdims=True))
        dg_ref[...] += dg_acc

    row = pl.BlockSpec((tm, WIDTH), lambda i: (i, 0))
    wide_bf = jax.ShapeDtypeStruct((s_len, WIDTH), bf16)
    return pl.pallas_call(
        body, name="mix_bwd", grid=(s_len // tm,),
        in_specs=[pl.BlockSpec((tm, 2 * WIDTH), lambda i: (i, 0)), row, row,
                  pl.BlockSpec((tm, WIDTH), lambda i: (i, 3)), pl.BlockSpec((tm, WIDTH), lambda i: (i, 7)),
                  pl.BlockSpec((1, LANES), lambda i: (0, 0))],
        out_specs=[row, pl.BlockSpec((tm, LANES), lambda i: (i, 0)), row, row, row,
                   pl.BlockSpec((1, LANES), lambda i: (0, 0))],
        out_shape=[wide_bf, jax.ShapeDtypeStruct((s_len, LANES), f32), wide_bf, wide_bf,
                   jax.ShapeDtypeStruct((s_len, WIDTH), f32), jax.ShapeDtypeStruct((1, LANES), f32)],
        compiler_params=_params("arbitrary"),
    )(dmixed, fox_o, gdn_o, p_main, p_main, gnorm_g)


def _out_head(mixed, w_out, x, target, gate, final_g):
    s_len = x.shape[0]
    tm = 256

    def body(mx_ref, w_ref, x_ref, t_ref, gate_ref, fg_ref, loss_ref, dy_ref, dz_ref, dm_ref, dfg_ref, dgate_ref):
        @pl.when(pl.program_id(0) == 0)
        def _():
            loss_ref[...] = jnp.zeros_like(loss_ref)
            dfg_ref[...] = jnp.zeros_like(dfg_ref)
            dgate_ref[...] = jnp.zeros_like(dgate_ref)

        w = w_ref[...]
        z = jnp.dot(mx_ref[...], w, preferred_element_type=f32)
        gate_v, fg = gate_ref[...], fg_ref[...]
        y1 = x_ref[...] + gate_v * z
        r = lax.rsqrt(jnp.mean(y1 * y1, axis=-1, keepdims=True) + EPS)
        yn = y1 * r
        err = yn * fg - t_ref[...]
        loss_ref[...] += 0.5 * jnp.sum(jnp.mean(err * err, axis=-1, keepdims=True))
        dout = err * (1.0 / D_MODEL)
        dfg_ref[...] += jnp.sum(dout * yn, axis=0, keepdims=True)
        tt = dout * fg
        dy1 = r * (tt - yn * jnp.mean(tt * yn, axis=-1, keepdims=True))
        dy_ref[...] = dy1
        dgate_ref[...] += jnp.sum(dy1 * z, axis=0, keepdims=True)
        dz = (dy1 * gate_v).astype(bf16)
        dz_ref[...] = dz
        dm_ref[...] = _dg(dz, w, 1, 1)

    row = pl.BlockSpec((tm, D_MODEL), lambda i: (i, 0))
    vec = pl.BlockSpec((1, D_MODEL), lambda i: (0, 0))
    big = jax.ShapeDtypeStruct((s_len, D_MODEL), f32)
    return pl.pallas_call(
        body, name="out_head", grid=(s_len // tm,),
        in_specs=[row, pl.BlockSpec((D_MODEL, D_MODEL), lambda i: (0, 0)), row, row, vec, vec],
        out_specs=[pl.BlockSpec((1, LANES), lambda i: (0, 0)), row, row, row, vec, vec],
        out_shape=[jax.ShapeDtypeStruct((1, LANES), f32), big, jax.ShapeDtypeStruct((s_len, D_MODEL), bf16), big,
                   jax.ShapeDtypeStruct((1, D_MODEL), f32), jax.ShapeDtypeStruct((1, D_MODEL), f32)],
        compiler_params=_params("arbitrary"),
    )(mixed, w_out, x, target, gate, final_g)


def _matmul_tn(name, a, b, out_dtype):
    k_len, m_len = a.shape
    n_len = b.shape[1]
    tk, tm, tn = min(2048, k_len), min(1024, m_len), min(2048, n_len)
    nk = k_len // tk

    def body(a_ref, b_ref, o_ref, acc_sc):
        k = pl.program_id(2)

        @pl.when(k == 0)
        def _():
            acc_sc[...] = jnp.zeros_like(acc_sc)

        acc_sc[...] += _dg(a_ref[...], b_ref[...], 0, 0)

        @pl.when(k == nk - 1)
        def _():
            o_ref[...] = acc_sc[...].astype(out_dtype)

    return pl.pallas_call(
        body, name=name, grid=(m_len // tm, n_len // tn, nk),
        in_specs=[pl.BlockSpec((tk, tm), lambda i, j, k: (k, i)), pl.BlockSpec((tk, tn), lambda i, j, k: (k, j))],
        out_specs=pl.BlockSpec((tm, tn), lambda i, j, k: (i, j)),
        out_shape=jax.ShapeDtypeStruct((m_len, n_len), out_dtype),
        scratch_shapes=[pltpu.VMEM((tm, tn), f32)],
        compiler_params=_params("parallel", "parallel", "arbitrary"),
    )(a, b)


def _post1(p_main, p_small, qn_g, kn_g, conv_w, bvec, alog, dqs, dkn, dgq, dgk, dgv, d_small, df, df_query):
    s_len = p_main.shape[0]
    tm = PREP_TM
    nb = s_len // tm

    def body(fq_ref, fk_ref, gq_ref, gk_ref, gv_ref, hq_ref, hk_ref, hv_ref, ps_ref, qg_ref, kg_ref, cw_ref, bv_ref,
             al_ref, dqs_ref, dkn_ref, dgq_ref, dgk_ref, dgv_ref, dsm_ref, df_ref, dfq_in_ref,
             dfq_ref, dfk_ref, dx_ref, dps_ref, dqg_ref, dkg_ref, sums_ref, dw_ref, xe_sc, carry_sc, dc_sc, next_sc):
        step = pl.program_id(0)
        blk = nb - 1 - step

        @pl.when(step == 0)
        def _():
            carry_sc[...] = jnp.zeros_like(carry_sc)
            next_sc[...] = jnp.zeros_like(next_sc)
            dqg_ref[...] = jnp.zeros_like(dqg_ref)
            dkg_ref[...] = jnp.zeros_like(dkg_ref)
            sums_ref[...] = jnp.zeros_like(sums_ref)
            dw_ref[...] = jnp.zeros_like(dw_ref)

        for x_ref, g_ref, dy_ref, o_ref, acc_ref, mul in ((fq_ref, qg_ref, dqs_ref, dfq_ref, dqg_ref, QK_SCALE),
                                                          (fk_ref, kg_ref, dkn_ref, dfk_ref, dkg_ref, LN2)):
            gain = g_ref[...]
            acc = jnp.zeros((1, HEAD_DIM), f32)
            for h in range(HEADS):
                sl = slice(h * HEAD_DIM, (h + 1) * HEAD_DIM)
                xv = x_ref[:, sl]
                r = lax.rsqrt(jnp.mean(xv * xv, axis=-1, keepdims=True) + EPS)
                xn = xv * r
                dy = dy_ref[:, sl] * mul
                acc = acc + jnp.sum(dy * xn, axis=0, keepdims=True)
                tt = dy * gain
                o_ref[:, sl] = (r * (tt - xn * jnp.mean(tt * xn, axis=-1, keepdims=True))).astype(bf16)
            acc_ref[...] += acc

        first = blk == 0
        for sec, (x_ref, halo_ref, dy_ref) in enumerate(((gq_ref, hq_ref, dgq_ref), (gk_ref, hk_ref, dgk_ref),
                                                         (gv_ref, hv_ref, dgv_ref))):
            cols = slice(sec * WIDTH, (sec + 1) * WIDTH)
            xe_sc[0:HALO, :] = jnp.where(first, 0.0, halo_ref[...])
            xe_sc[HALO:, :] = x_ref[...]
            cv = _conv_section(xe_sc, cw_ref, cols, tm)
            y, sgrad = _silu_and_grad(cv)
            if sec == 2:
                dc_sc[0:tm, :] = dy_ref[...] * sgrad
            else:
                mul = QK_SCALE if sec == 0 else 1.0
                for h in range(HEADS):
                    sl = slice(h * HEAD_DIM, (h + 1) * HEAD_DIM)
                    yh = y[:, sl]
                    r = lax.rsqrt(jnp.sum(yh * yh, axis=-1, keepdims=True) + EPS)
                    dqh = dy_ref[:, sl]
                    dyh = (mul * r) * (dqh - yh * (r * r) * jnp.sum(dqh * yh, axis=-1, keepdims=True))
                    dc_sc[0:tm, sl] = dyh * sgrad[:, sl]
            dc_sc[tm:, :] = next_sc[sec]
            x_rows = xe_sc[pl.ds(HALO, tm), :]
            dx = jnp.zeros((tm, WIDTH), f32)
            dw = jnp.zeros((8, WIDTH), f32)
            tap_row = _iota((8, WIDTH), 0)
            for tap in range(CONV_K):
                ahead = dc_sc[pl.ds(CONV_K - 1 - tap, tm), :]
                dx = dx + cw_ref[pl.ds(tap, 1), cols] * ahead
                dw = jnp.where(tap_row == tap, jnp.sum(x_rows * ahead, axis=0, keepdims=True), dw)
            dx_ref[:, cols] = dx.astype(bf16)
            dw_ref[:, cols] += dw
            next_sc[sec] = dc_sc[0:HALO, :]

        lane = _iota((tm, N_SMALL), 1)
        z, _, gval, beta = _small_fwd(ps_ref[...], bv_ref[...], al_ref[...])
        sig_z = _sigmoid(z)
        dsm = dsm_ref[...]
        in_g = (lane >= LANE_G) & (lane < LANE_G + HEADS)
        dgc = jnp.where(in_g, pltpu.roll(dsm, N_SMALL - (LANE_GC - LANE_G), 1), 0.0)
        dgl = jnp.where(in_g, pltpu.roll(dsm, N_SMALL - (LANE_GLAST - LANE_G), 1), 0.0)
        tri_c, ones_c = _chunk_masks(tm)
        dg = (_dg(tri_c, dgc, 0, 0, HI) + jnp.dot(ones_c, dgl, preferred_element_type=f32, precision=HI))
        dbeta = dsm
        dfb = jnp.where(lane < HEADS, df_ref[...], 0.0)
        for h in range(HEADS):
            dfb = dfb + jnp.where(lane == h, dfq_in_ref[h], 0.0)
        tri_u = (_iota((tm, tm), 1) >= _iota((tm, tm), 0)).astype(f32)
        dlogf = jnp.dot(tri_u, dfb, preferred_element_type=f32, precision=HI) + carry_sc[...]
        carry_sc[...] += jnp.sum(dfb, axis=0, keepdims=True)
        dff = dlogf * (1.0 - sig_z)
        dga = dg * (-jnp.exp(al_ref[...])) * sig_z
        dgb_small = dbeta * beta * (1.0 - beta)
        dps = jnp.where(lane < HEADS, dff, jnp.where(lane < 2 * HEADS, dga, jnp.where(lane < 3 * HEADS, dgb_small, 0.0)))
        dps_ref[...] = dps.astype(bf16)
        row = _iota((8, N_SMALL), 0)
        s0 = jnp.sum(dps, axis=0, keepdims=True)
        s1 = jnp.sum(jnp.where((lane >= HEADS) & (lane < 2 * HEADS), dg * gval, 0.0), axis=0, keepdims=True)
        sums_ref[...] += jnp.where(row == 0, s0, jnp.where(row == 1, s1, 0.0))

    def col(cb):
        return pl.BlockSpec((tm, WIDTH), lambda i: (nb - 1 - i, cb))

    def halo(cb):
        return pl.BlockSpec((HALO, WIDTH), lambda i: (jnp.maximum((nb - 1 - i) * (tm // HALO) - 1, 0), cb))

    vec = pl.BlockSpec((1, LANES), lambda i: (0, 0))
    row0 = pl.BlockSpec((tm, WIDTH), lambda i: (nb - 1 - i, 0))
    small = pl.BlockSpec((tm, N_SMALL), lambda i: (nb - 1 - i, 0))
    wide_bf = jax.ShapeDtypeStruct((s_len, WIDTH), bf16)
    return pl.pallas_call(
        body, name="post1", grid=(nb,),
        in_specs=[col(0), col(1), col(4), col(5), col(6), halo(4), halo(5), halo(6), small, vec, vec,
                  pl.BlockSpec((CONV_K, 3 * WIDTH), lambda i: (0, 0)), vec, vec,
                  row0, row0, row0, row0, row0, small, small,
                  pl.BlockSpec((HEADS, tm, 1), lambda i: (0, nb - 1 - i, 0))],
        out_specs=[row0, row0, pl.BlockSpec((tm, 3 * WIDTH), lambda i: (nb - 1 - i, 0)), small, vec, vec,
                   pl.BlockSpec((8, N_SMALL), lambda i: (0, 0)), pl.BlockSpec((8, 3 * WIDTH), lambda i: (0, 0))],
        out_shape=[wide_bf, wide_bf, jax.ShapeDtypeStruct((s_len, 3 * WIDTH), bf16),
                   jax.ShapeDtypeStruct((s_len, N_SMALL), bf16), jax.ShapeDtypeStruct((1, LANES), f32),
                   jax.ShapeDtypeStruct((1, LANES), f32), jax.ShapeDtypeStruct((8, N_SMALL), f32),
                   jax.ShapeDtypeStruct((8, 3 * WIDTH), f32)],
        scratch_shapes=[pltpu.VMEM((tm + HALO, WIDTH), f32), pltpu.VMEM((1, N_SMALL), f32),
                        pltpu.VMEM((tm + HALO, WIDTH), f32), pltpu.VMEM((3, HALO, WIDTH), f32)],
        compiler_params=_params("arbitrary"),
    )(p_main, p_main, p_main, p_main, p_main, p_main, p_main, p_main, p_small, qn_g, kn_g, conv_w, bvec, alog,
      dqs, dkn, dgq, dgk, dgv, d_small, df, df_query)


def _in_proj_bwd(dp_pieces, dp_small, wt_main, wt_small):
    s_len = dp_small.shape[0]
    tm, tk = min(1024, s_len), WIDTH
    nk = N_MAIN // tk
    first_section = [sum(p.shape[1] // tk for p in dp_pieces[:n]) for n in range(len(dp_pieces))]
    n_pieces = len(dp_pieces)

    def body(*refs):
        piece_refs = refs[:n_pieces]
        dps_ref, w_ref, ws_ref, dh_ref = refs[n_pieces:]
        k = pl.program_id(1)

        @pl.when(k == 0)
        def _():
            dh_ref[...] = jnp.dot(dps_ref[...], ws_ref[...], preferred_element_type=f32)

        for piece, ref, first in zip(dp_pieces, piece_refs, first_section):
            @pl.when((k >= first) & (k < first + piece.shape[1] // tk))
            def _(ref=ref):
                dh_ref[...] += jnp.dot(ref[...], w_ref[...], preferred_element_type=f32)

    def piece_spec(piece, first):
        last = piece.shape[1] // tk - 1
        return pl.BlockSpec((tm, tk), lambda i, k: (i, jnp.clip(k - first, 0, last)))

    return pl.pallas_call(
        body, name="in_proj_bwd", grid=(s_len // tm, nk),
        in_specs=[piece_spec(p, f) for p, f in zip(dp_pieces, first_section)]
                 + [pl.BlockSpec((tm, N_SMALL), lambda i, k: (i, 0)),
                    pl.BlockSpec((tk, D_MODEL), lambda i, k: (k, 0)), pl.BlockSpec((N_SMALL, D_MODEL), lambda i, k: (0, 0))],
        out_specs=pl.BlockSpec((tm, D_MODEL), lambda i, k: (i, 0)),
        out_shape=jax.ShapeDtypeStruct((s_len, D_MODEL), f32),
        compiler_params=_params("parallel", "arbitrary"),
    )(*dp_pieces, dp_small, wt_main, wt_small)


def _adaln_bwd(dh, x, dy1, norm_g, scale1p):
    s_len = x.shape[0]
    tm = 512

    def body(dh_ref, x_ref, dy_ref, g_ref, sc_ref, dx_ref, dsh_ref, dsc_ref, dg_ref):
        @pl.when(pl.program_id(0) == 0)
        def _():
            dsh_ref[...] = jnp.zeros_like(dsh_ref)
            dsc_ref[...] = jnp.zeros_like(dsc_ref)
            dg_ref[...] = jnp.zeros_like(dg_ref)

        dh = dh_ref[...]
        xb = x_ref[...]
        r = lax.rsqrt(jnp.mean(xb * xb, axis=-1, keepdims=True) + EPS)
        xr = xb * r
        gain = g_ref[...]
        dsh_ref[...] += jnp.sum(dh, axis=0, keepdims=True)
        dsc_ref[...] += jnp.sum(dh * (xr * gain), axis=0, keepdims=True)
        dxn = dh * sc_ref[...]
        dg_ref[...] += jnp.sum(dxn * xr, axis=0, keepdims=True)
        tt = dxn * gain
        dx_ref[...] = r * (tt - xr * jnp.mean(tt * xr, axis=-1, keepdims=True)) + dy_ref[...]

    row = pl.BlockSpec((tm, D_MODEL), lambda i: (i, 0))
    vec = pl.BlockSpec((1, D_MODEL), lambda i: (0, 0))
    vshape = jax.ShapeDtypeStruct((1, D_MODEL), f32)
    return pl.pallas_call(
        body, name="adaln_bwd", grid=(s_len // tm,),
        in_specs=[row, row, row, vec, vec], out_specs=[row, vec, vec, vec],
        out_shape=[jax.ShapeDtypeStruct((s_len, D_MODEL), f32), vshape, vshape, vshape],
        compiler_params=_params("arbitrary"),
    )(dh, x, dy1, norm_g, scale1p)


def _adamw(name, w, g_stack, m, v, tr, tc=None):
    n_stack, rows, cols = g_stack.shape
    tc = cols if tc is None else tc

    def body(w_ref, g_ref, m_ref, v_ref, go_ref, d_ref, mo_ref, vo_ref):
        g = g_ref[0].astype(f32)
        for k in range(1, n_stack):
            g = g + g_ref[k].astype(f32)
        go_ref[0] = g
        m_new = ADAM_B1 * m_ref[0] + (1.0 - ADAM_B1) * g
        v_new = ADAM_B2 * v_ref[0] + (1.0 - ADAM_B2) * (g * g)
        mo_ref[0] = m_new
        vo_ref[0] = v_new
        m_hat = m_new / (1.0 - ADAM_B1 ** ADAM_STEP)
        v_hat = v_new / (1.0 - ADAM_B2 ** ADAM_STEP)
        d_ref[0] = -ADAM_LR * (m_hat / (jnp.sqrt(v_hat) + ADAM_EPS) + ADAM_WD * w_ref[0])

    blk = pl.BlockSpec((1, tr, tc), lambda i, j: (0, i, j))
    shape = jax.ShapeDtypeStruct((1, rows, cols), f32)
    return pl.pallas_call(
        body, name=name, grid=(rows // tr, cols // tc),
        in_specs=[blk, pl.BlockSpec((n_stack, tr, tc), lambda i, j: (0, i, j)), blk, blk],
        out_specs=[blk] * 4, out_shape=[shape] * 4,
        compiler_params=_params("parallel", "parallel"),
    )(w, g_stack, m, v)


def _w_ada_grad(c_all_t, dmod_pad):
    def body(c_ref, d_ref, o_ref):
        cv = c_ref[...]
        o_ref[...] = jnp.dot(cv * _sigmoid(cv), d_ref[...], preferred_element_type=f32, precision=HI)

    return pl.pallas_call(body, name="w_ada_grad",
                          out_shape=jax.ShapeDtypeStruct((c_all_t.shape[0], dmod_pad.shape[1]), f32),
                          compiler_params=_params())(c_all_t, dmod_pad)


SMALL_NAMES = ("norm_g", "b_ada", "b_fgate", "fox_qn_g", "fox_kn_g", "gdn_A_log", "gdn_dt_bias", "gdn_norm_g", "final_g")
SMALL_SIZES = (D_MODEL, 3 * D_MODEL, HEADS, HEAD_DIM, HEAD_DIM, HEADS, HEADS, HEAD_DIM, D_MODEL)
SMALL_PACK = 10752


def _pack(vectors, total):
    flat = jnp.concatenate([t.reshape(-1) for t in vectors])
    return jnp.pad(flat, (0, total - flat.shape[0])).reshape(1, total)


def _lanes(*pieces):
    parts, at = [], 0
    for off, vec in pieces:
        flat = vec.reshape(-1).astype(f32)
        parts += [jnp.zeros((off - at,), f32), flat]
        at = off + flat.shape[0]
    parts.append(jnp.zeros((LANES - at,), f32))
    return jnp.concatenate(parts).reshape(1, LANES)


def kernel(x, c, norm_g, w_ada, b_ada, w_in, b_fgate, fox_qn_g, fox_kn_g, gdn_conv_w, gdn_A_log, gdn_dt_bias, gdn_norm_g, w_out, final_g, loss_target, m_norm_g, m_w_ada, m_b_ada, m_w_in, m_b_fgate, m_fox_qn_g, m_fox_kn_g, m_gdn_conv_w, m_gdn_A_log, m_gdn_dt_bias, m_gdn_norm_g, m_w_out, m_final_g, v_norm_g, v_w_ada, v_b_ada, v_w_in, v_b_fgate, v_fox_qn_g, v_fox_kn_g, v_gdn_conv_w, v_gdn_A_log, v_gdn_dt_bias, v_gdn_norm_g, v_w_out, v_final_g):
    me = _my_index()
    s_len = x.shape[1]
    nq = s_len // FOX_T
    x2 = x.reshape(s_len, D_MODEL)
    tgt = loss_target.reshape(s_len, D_MODEL)
    ada_cols = w_ada.shape[2]
    in_cols = w_in.shape[2]
    conv_cols = gdn_conv_w.shape[2]

    (c_all,) = _gather_direct("gather_c", [c])
    c_all = c_all.reshape(N_DEV, D_MODEL)
    b_shard = lax.dynamic_slice(b_ada, (0, me * ada_cols), (1, ada_cols))
    mod_mine = _mod_shard(c_all, w_ada[0], b_shard)
    wt_shard = jnp.transpose(w_in[0])
    mod_all, wt_all, w_out_all, conv_all = _gather_two_level(
        "gather_weights", [mod_mine, wt_shard.astype(bf16), w_out[0].astype(bf16), gdn_conv_w[0]])
    mod = lax.dynamic_slice(mod_all, (0, me, 0), (N_DEV, 1, ada_cols)).reshape(1, 3 * D_MODEL)
    shift, scale, gate = mod[:, :D_MODEL], mod[:, D_MODEL:2 * D_MODEL], mod[:, 2 * D_MODEL:]
    scale1p = 1.0 + scale
    wt_full = wt_all.reshape(N_DEV * in_cols, D_MODEL)
    g0 = 4 * WIDTH + HEADS
    w_main = jnp.concatenate([wt_full[:4 * WIDTH], wt_full[g0:g0 + 4 * WIDTH]], axis=0)
    w_small = jnp.concatenate([wt_full[4 * WIDTH:g0], wt_full[g0 + 4 * WIDTH:],
                               jnp.zeros((N_SMALL - 3 * HEADS, D_MODEL), bf16)], axis=0)
    w_out_full = w_out_all.reshape(2 * WIDTH, D_MODEL)
    conv_full = jnp.transpose(conv_all, (1, 0, 2)).reshape(CONV_K, 3 * WIDTH)

    qn_g, kn_g, gn_g = fox_qn_g.reshape(1, LANES), fox_kn_g.reshape(1, LANES), gdn_norm_g.reshape(1, LANES)
    bvec = _lanes((0, b_fgate), (HEADS, gdn_dt_bias))
    alog = _lanes((HEADS, gdn_A_log))
    fg = final_g.reshape(1, D_MODEL)

    h_bf = _norm_mod(x2, norm_g, scale1p, shift)
    p_main, p_small = _in_proj(h_bf, w_main, w_small)
    qs, kn, vb, gq, gk, gv, small = _prep(p_main, p_small, qn_g, kn_g, conv_full, bvec, alog)
    fox_o, lse = _fox_fwd(qs, kn, vb)
    gu, gw, gqg, gkd, gattn, t_inv, eg_last = _gdn_intra_fwd(gq, gk, gv, small)
    gdn_o, states = _gdn_scan_fwd(gu, gw, gqg, gkd, gattn, eg_last)
    mixed = _mix_fwd(fox_o, gdn_o, p_main, gn_g)

    loss_row, dy1, dz, dmixed, d_final_g, d_gate = _out_head(mixed, w_out_full, x2, tgt, gate, fg)
    loss = lax.psum(loss_row[0, 0], AXES)
    dw_out = _matmul_tn("dw_out", mixed, dz, bf16)
    do_fox, delta, dfz, dgz, dgdn_o, d_gn_g = _mix_bwd(dmixed, fox_o, gdn_o, p_main, gn_g)
    dqs, dkn, dvf, df_key, df_query = _fox_bwd(qs, kn, vb, do_fox, lse, delta)
    du, dw, dqg, dkd, dattn, deg = _gdn_scan_bwd(gu, gw, gqg, gkd, gattn, eg_last, states, dgdn_o)
    dgq, dgk, dgv, d_small = _gdn_intra_bwd(gq, gk, gv, small, t_inv, du, dw, dqg, dkd, dattn, deg)
    df_small = jnp.pad(jnp.transpose(df_key.reshape(HEADS, s_len)), ((0, 0), (0, N_SMALL - HEADS)))
    dfq, dfk, dgqkv, dp_small, d_qn_g, d_kn_g, sums, d_conv = _post1(
        p_main, p_small, qn_g, kn_g, conv_full, bvec, alog, dqs, dkn, dgq, dgk, dgv, d_small, df_small, df_query)
    dp_pieces = [dfq, dfk, dvf, dfz, dgqkv, dgz]
    dh = _in_proj_bwd(dp_pieces, dp_small, w_main, w_small)
    grad_x, d_shift, d_scale, d_norm_g = _adaln_bwd(dh, x2, dy1, norm_g, scale1p)
    dw_rows = [_matmul_tn("dw_main_%d" % n, piece, h_bf, bf16) for n, piece in enumerate(dp_pieces)]
    dw_small = _matmul_tn("dw_small", dp_small, h_bf, bf16)
    dw_in_full = jnp.concatenate(dw_rows[:4] + [dw_small[:HEADS]] + dw_rows[4:] + [dw_small[HEADS:3 * HEADS]],
                                 axis=0)
    dw_in_parts = dw_in_full.reshape(N_DEV, in_cols, D_MODEL)
    dw_out_parts = dw_out.reshape(N_DEV, w_out.shape[1], D_MODEL)

    dmod = jnp.concatenate([d_shift, d_scale, d_gate], axis=1)
    small_grads = _pack([d_norm_g, dmod, sums[0, :HEADS], d_qn_g, d_kn_g, sums[1, HEADS:2 * HEADS],
                         sums[0, HEADS:2 * HEADS], d_gn_g, d_final_g], SMALL_PACK)
    conv_grad = d_conv[:CONV_K]
    pair_in, pair_out = _pair_exchange("pair_grads", [dw_in_parts, dw_out_parts])
    core = lax.axis_index("c").astype(jnp.int32).reshape(1)
    dw_in_recv, dw_out_recv = _chip_exchange(
        "chip_grads", [_pair_sum("pair_sum_w_in", dw_in_parts, pair_in, core),
                       _pair_sum("pair_sum_w_out", dw_out_parts, pair_out, core)])
    small_all, conv_all_g = _gather_direct("gather_small_grads", [small_grads, conv_grad])

    outs = {}
    to_t = lambda t: jnp.transpose(t, (0, 2, 1))
    outs["w_in"] = tuple(to_t(t) for t in _adamw("adamw_w_in", to_t(w_in), dw_in_recv, to_t(m_w_in), to_t(v_w_in),
                                                  in_cols, 256))
    outs["w_out"] = _adamw("adamw_w_out", w_out, dw_out_recv, m_w_out, v_w_out, 128)
    conv_mine = lax.dynamic_slice(jnp.transpose(conv_all_g.reshape(N_DEV, CONV_K, N_DEV, conv_cols), (0, 2, 1, 3)),
                                  (0, me, 0, 0), (N_DEV, 1, CONV_K, conv_cols)).reshape(N_DEV, CONV_K, conv_cols)
    outs["gdn_conv_w"] = _adamw("adamw_conv", gdn_conv_w, conv_mine, m_gdn_conv_w, v_gdn_conv_w, CONV_K)
    small_all = small_all.reshape(N_DEV, 1, SMALL_PACK)
    dmod_all = small_all[:, 0, D_MODEL:D_MODEL + 3 * D_MODEL]
    dmod_mine = lax.dynamic_slice(dmod_all, (0, me * ada_cols), (N_DEV, ada_cols))
    c_all_t = jnp.pad(jnp.transpose(c_all), ((0, 0), (0, LANES - N_DEV)))
    g_w_ada = _w_ada_grad(c_all_t, jnp.pad(dmod_mine, ((0, LANES - N_DEV), (0, 0))))
    outs["w_ada"] = _adamw("adamw_w_ada", w_ada, g_w_ada[None], m_w_ada, v_w_ada, 256)
    given = dict(norm_g=(norm_g, m_norm_g, v_norm_g), b_ada=(b_ada, m_b_ada, v_b_ada), b_fgate=(b_fgate, m_b_fgate, v_b_fgate),
                 fox_qn_g=(fox_qn_g, m_fox_qn_g, v_fox_qn_g), fox_kn_g=(fox_kn_g, m_fox_kn_g, v_fox_kn_g),
                 gdn_A_log=(gdn_A_log, m_gdn_A_log, v_gdn_A_log), gdn_dt_bias=(gdn_dt_bias, m_gdn_dt_bias, v_gdn_dt_bias),
                 gdn_norm_g=(gdn_norm_g, m_gdn_norm_g, v_gdn_norm_g), final_g=(final_g, m_final_g, v_final_g))
    w_pack = _pack([given[n][0] for n in SMALL_NAMES], SMALL_PACK)
    m_pack = _pack([given[n][1] for n in SMALL_NAMES], SMALL_PACK)
    v_pack = _pack([given[n][2] for n in SMALL_NAMES], SMALL_PACK)
    packed = _adamw("adamw_small", w_pack[None], small_all, m_pack[None], v_pack[None], 1)
    off = 0
    for n, size in zip(SMALL_NAMES, SMALL_SIZES):
        outs[n] = tuple(t[0, 0, off:off + size].reshape(given[n][0].shape) for t in packed)
        off += size

    order = ("norm_g", "w_ada", "b_ada", "w_in", "b_fgate", "fox_qn_g", "fox_kn_g", "gdn_conv_w", "gdn_A_log",
             "gdn_dt_bias", "gdn_norm_g", "w_out", "final_g")
    result = [loss, grad_x.reshape(x.shape)]
    for part in range(4):
        result += [outs[n][part] for n in order]
    return tuple(result)
```

```python
import math

import jax
import jax.numpy as jnp
from jax import lax
from jax.experimental import pallas as pl
from jax.experimental.pallas import tpu as pltpu

f32 = jnp.float32
bf16 = jnp.bfloat16
HI = lax.Precision.HIGHEST

N_DEV = 8
AXES = ("x", "y", "c")
D_MODEL = 2048
HEADS = 8
HEAD_DIM = 128
WIDTH = HEADS * HEAD_DIM
CHUNK = 64
CONV_K = 4
EPS = 1e-6
QK_SCALE = HEAD_DIM ** -0.5
LOG2E = 1.0 / math.log(2.0)
LN2 = math.log(2.0)
N_MAIN = 8 * WIDTH
N_SMALL = 128
LANE_F, LANE_G, LANE_BETA, LANE_GC, LANE_GLAST = 0, 8, 16, 24, 32
IN_WIDTH = 8 * WIDTH + 3 * HEADS
LANES = 128
VMEM_LIMIT = 56 * 1024 * 1024

ADAM_LR, ADAM_B1, ADAM_B2, ADAM_EPS, ADAM_WD, ADAM_STEP = 0.001, 0.9, 0.999, 1e-08, 0.01, 10


def _params(*sem):
    return pltpu.CompilerParams(dimension_semantics=sem, vmem_limit_bytes=VMEM_LIMIT)


def _iota(shape, dim):
    return lax.broadcasted_iota(jnp.int32, shape, dim)


def _sigmoid(z):
    return 1.0 / (1.0 + jnp.exp(-z))


def _softplus_parts(z):
    t = jnp.log(1.0 + jnp.exp(-jnp.abs(z)))
    return jnp.minimum(z, 0.0) - t, jnp.maximum(z, 0.0) + t


def _dg(a, b, ca, cb, prec=None):
    if a.ndim == 3:
        dims = (((ca + 1,), (cb + 1,)), ((0,), (0,)))
    else:
        dims = (((ca,), (cb,)), ((), ()))
    return lax.dot_general(a, b, dims, preferred_element_type=f32, precision=prec)


def _dot_bf16(a, b, ca, cb):
    return _dg(a.astype(bf16), b.astype(bf16), ca, cb)


def _split_bf16(a):
    hi = a.astype(bf16)
    return hi, (a - hi.astype(f32)).astype(bf16)


def _dot_3pass(a, b, ca, cb):
    a_hi, a_lo = _split_bf16(a)
    b_hi, b_lo = _split_bf16(b)
    return _dg(a_hi, b_hi, ca, cb) + (_dg(a_hi, b_lo, ca, cb) + _dg(a_lo, b_hi, ca, cb))


def _make_mm(dot):
    def nn_(a, b):
        return dot(a, b, 1, 0)

    def nt_(a, b):
        return dot(a, b, 1, 1)

    def tn_(a, b):
        return dot(a, b, 0, 0)

    @jax.custom_vjp
    def nn(a, b):
        return nn_(a, b)

    @jax.custom_vjp
    def nt(a, b):
        return nt_(a, b)

    @jax.custom_vjp
    def tn(a, b):
        return tn_(a, b)

    nn.defvjp(lambda a, b: (nn_(a, b), (a, b)), lambda r, g: (nt_(g, r[1]), tn_(r[0], g)))
    nt.defvjp(lambda a, b: (nt_(a, b), (a, b)), lambda r, g: (nn_(g, r[1]), tn_(g, r[0])))
    tn.defvjp(lambda a, b: (tn_(a, b), (a, b)), lambda r, g: (nt_(r[1], g), nn_(r[0], g)))
    return (nn_, nt_, tn_), (nn, nt, tn)


_BF_PLAIN, _BF_VJP = _make_mm(_dot_bf16)
_X3_PLAIN, _X3_VJP = _make_mm(_dot_3pass)


def _inv_unit_lower_many(ms):
    c = CHUNK
    nn = _X3_PLAIN[0]
    eye = (_iota((c, c), 0) == _iota((c, c), 1)).astype(f32)
    top = _iota((2 * c, c), 0) < c
    xs = [jnp.concatenate([eye - m, nn(m, m)], axis=0) for m in ms]
    for _ in range(int(math.log2(CHUNK)) - 2):
        xs = [jnp.where(top, x, 0.0) + nn(x, x[c:]) for x in xs]
    return [x[:c] + nn(x[:c], x[c:]) for x in xs]


@jax.custom_vjp
def _inv_given(m, t):
    return t


_inv_given.defvjp(lambda m, t: (t, t),
                  lambda t, g: (-_X3_PLAIN[1](_X3_PLAIN[2](t, g), t), jnp.zeros_like(t)))

SUBLANES = 8


def _gdn_intra_pre(q, k, v, gc_b, g_last_b, beta_b, bnt):
    c = CHUNK
    r_i, c_i = _iota((c, c), 0), _iota((c, c), 1)
    lower, strict = r_i >= c_i, r_i > c_i
    gc_i = gc_b[..., :c]
    gc_j = jnp.swapaxes(gc_i, -1, -2)
    decay = jnp.where(lower, jnp.exp(jnp.where(lower, gc_i - gc_j, 0.0)), 0.0)
    kb = k * beta_b
    both = bnt(jnp.concatenate([kb, q], axis=-2), k)
    m = jnp.where(strict, both[..., :c, :] * decay, 0.0)
    attn = jnp.where(lower, both[..., c:, :] * decay, 0.0)
    eg = jnp.exp(gc_b)
    rhs = jnp.concatenate([v * beta_b, kb * eg], axis=-1)
    k_dec = k * jnp.exp(g_last_b - gc_b)
    eg_last = jnp.exp(g_last_b[..., :SUBLANES, :])
    return m, rhs, q * eg, k_dec, attn, eg_last


def _gdn_intra(q, k, v, gc_b, g_last_b, beta_b, t_known):
    m, rhs, qg, k_dec, attn, eg_last = _gdn_intra_pre(q, k, v, gc_b, g_last_b, beta_b, _BF_VJP[1])
    return _X3_VJP[0](_inv_given(m, t_known), rhs), qg, k_dec, attn, eg_last


def _scale_rows(s, eg_last):
    return (s.reshape(HEAD_DIM // SUBLANES, SUBLANES, HEAD_DIM) * eg_last[None]).reshape(HEAD_DIM, HEAD_DIM)


def _my_index():
    return 4 * lax.axis_index("x") + 2 * lax.axis_index("y") + lax.axis_index("c")


def _peer(d):
    x, y, c = lax.axis_index("x"), lax.axis_index("y"), lax.axis_index("c")
    px, py, pc = (x + (d >> 2)) % 2, (y + ((d >> 1) & 1)) % 2, (c + (d & 1)) % 2
    return (px, py, pc), 4 * px + 2 * py + pc


def _gather_direct(name, arrays):
    n = len(arrays)

    def body(*refs):
        srcs, dsts = refs[:n], refs[n:2 * n]
        send_sems, recv_sems, local_sems = refs[2 * n:]
        me = _my_index()

        def copy(k, d, started):
            peer, pidx = _peer(d)
            return pltpu.make_async_remote_copy(
                src_ref=srcs[k], dst_ref=dsts[k].at[me if started else pidx], send_sem=send_sems.at[k * 7 + d - 1],
                recv_sem=recv_sems.at[k * 7 + d - 1], device_id=peer, device_id_type=pl.DeviceIdType.MESH)

        local = [pltpu.make_async_copy(srcs[k], dsts[k].at[me], local_sems.at[k]) for k in range(n)]
        sends = [copy(k, d, True) for k in range(n) for d in range(1, N_DEV)]
        for cp in local + sends:
            cp.start()
        for k in range(n):
            for d in range(1, N_DEV):
                copy(k, d, False).wait_recv()
        for cp in sends:
            cp.wait_send()
        for cp in local:
            cp.wait()

    out_shape = [jax.ShapeDtypeStruct((N_DEV,) + a.shape, a.dtype) for a in arrays]
    any_spec = pl.BlockSpec(memory_space=pl.ANY)
    return pl.pallas_call(
        body, name=name, out_shape=out_shape, in_specs=[any_spec] * n, out_specs=[any_spec] * n,
        scratch_shapes=[pltpu.SemaphoreType.DMA((7 * n,)), pltpu.SemaphoreType.DMA((7 * n,)),
                        pltpu.SemaphoreType.DMA((n,))],
        compiler_params=pltpu.CompilerParams(has_side_effects=True),
    )(*arrays)


N_CHIPS = 4


def _pair_exchange(name, arrays):
    n = len(arrays)

    def body(*refs):
        srcs, dsts = refs[:n], refs[n:2 * n]
        send_sems, recv_sems = refs[2 * n:]
        x, y, c = lax.axis_index("x"), lax.axis_index("y"), lax.axis_index("c")
        sibling = (x, y, 1 - c)

        def copy(k, j):
            return pltpu.make_async_remote_copy(
                src_ref=srcs[k].at[2 * j + (1 - c)], dst_ref=dsts[k].at[j], send_sem=send_sems.at[k * N_CHIPS + j],
                recv_sem=recv_sems.at[k * N_CHIPS + j], device_id=sibling, device_id_type=pl.DeviceIdType.MESH)

        copies = [copy(k, j) for k in range(n) for j in range(N_CHIPS)]
        for cp in copies:
            cp.start()
        for cp in copies:
            cp.wait_recv()
        for cp in copies:
            cp.wait_send()

    any_spec = pl.BlockSpec(memory_space=pl.ANY)
    return pl.pallas_call(
        body, name=name, out_shape=[jax.ShapeDtypeStruct((N_CHIPS,) + a.shape[1:], a.dtype) for a in arrays],
        in_specs=[any_spec] * n, out_specs=[any_spec] * n,
        scratch_shapes=[pltpu.SemaphoreType.DMA((N_CHIPS * n,)), pltpu.SemaphoreType.DMA((N_CHIPS * n,))],
        compiler_params=pltpu.CompilerParams(has_side_effects=True),
    )(*arrays)


def _chip_exchange(name, arrays):
    n = len(arrays)

    def body(*refs):
        srcs, dsts = refs[:n], refs[n:2 * n]
        send_sems, recv_sems, local_sems = refs[2 * n:]
        x, y, c = lax.axis_index("x"), lax.axis_index("y"), lax.axis_index("c")
        my_chip = 2 * x + y

        def peer(d):
            px, py = (x + (d >> 1)) % 2, (y + (d & 1)) % 2
            return (px, py, c), 2 * px + py

        def remote(k, d, started):
            to, chip = peer(d)
            return pltpu.make_async_remote_copy(
                src_ref=srcs[k].at[chip], dst_ref=dsts[k].at[my_chip if started else chip],
                send_sem=send_sems.at[k * 3 + d - 1], recv_sem=recv_sems.at[k * 3 + d - 1],
                device_id=to, device_id_type=pl.DeviceIdType.MESH)

        local = [pltpu.make_async_copy(srcs[k].at[my_chip], dsts[k].at[my_chip], local_sems.at[k]) for k in range(n)]
        sends = [remote(k, d, True) for k in range(n) for d in range(1, N_CHIPS)]
        for cp in local + sends:
            cp.start()
        for k in range(n):
            for d in range(1, N_CHIPS):
                remote(k, d, False).wait_recv()
        for cp in sends:
            cp.wait_send()
        for cp in local:
            cp.wait()

    any_spec = pl.BlockSpec(memory_space=pl.ANY)
    return pl.pallas_call(
        body, name=name, out_shape=[jax.ShapeDtypeStruct(a.shape, a.dtype) for a in arrays],
        in_specs=[any_spec] * n, out_specs=[any_spec] * n,
        scratch_shapes=[pltpu.SemaphoreType.DMA((3 * n,)), pltpu.SemaphoreType.DMA((3 * n,)),
                        pltpu.SemaphoreType.DMA((n,))],
        compiler_params=pltpu.CompilerParams(has_side_effects=True),
    )(*arrays)


def _pair_sum(name, parts, received, core):
    n_blocks, rows, cols = received.shape
    tr = rows if rows % 256 else 256

    def body(core_ref, mine_ref, recv_ref, o_ref):
        o_ref[...] = (mine_ref[...].astype(f32) + recv_ref[...].astype(f32)).astype(bf16)

    return pl.pallas_call(
        body, name=name,
        grid_spec=pltpu.PrefetchScalarGridSpec(
            num_scalar_prefetch=1, grid=(n_blocks, rows // tr),
            in_specs=[pl.BlockSpec((1, tr, cols), lambda j, i, core_ref: (2 * j + core_ref[0], i, 0)),
                      pl.BlockSpec((1, tr, cols), lambda j, i, core_ref: (j, i, 0))],
            out_specs=pl.BlockSpec((1, tr, cols), lambda j, i, core_ref: (j, i, 0))),
        out_shape=jax.ShapeDtypeStruct((n_blocks, rows, cols), bf16),
        compiler_params=_params("parallel", "parallel"),
    )(core, parts, received)


_HBM = pl.BlockSpec(memory_space=pltpu.HBM)
_SEM = pl.BlockSpec(memory_space=pltpu.SEMAPHORE)
_DATAFLOW = pltpu.SideEffectType.DATAFLOW_SIDE_EFFECTING


def _gather_behind_start(name, block):
    def body(src_ref, land_ref, *outs):
        send_sems, recv_sems = outs[:N_DEV - 1], outs[N_DEV - 1:2 * (N_DEV - 1)]
        token = outs[-1]
        me = _my_index()
        for d in range(1, N_DEV):
            peer, _ = _peer(d)
            pltpu.make_async_remote_copy(src_ref=src_ref, dst_ref=land_ref.at[me], send_sem=send_sems[d - 1],
                                         recv_sem=recv_sems[d - 1], device_id=peer,
                                         device_id_type=pl.DeviceIdType.MESH).start()
        token[...] = jnp.zeros_like(token)

    n_sem = 2 * (N_DEV - 1)
    land_shape = (N_DEV,) + block.shape
    return pl.pallas_call(
        body, name=name,
        out_shape=tuple([pltpu.SemaphoreType.DMA(())] * n_sem
                        + [pltpu.HBM(block.shape, block.dtype), pltpu.HBM(land_shape, block.dtype),
                           jax.ShapeDtypeStruct((SUBLANES, LANES), f32)]),
        in_specs=(_HBM, _HBM), out_specs=tuple([_SEM] * n_sem + [_HBM, _HBM, pl.BlockSpec(memory_space=pltpu.VMEM)]),
        input_output_aliases={0: n_sem, 1: n_sem + 1},
        compiler_params=pltpu.CompilerParams(has_side_effects=_DATAFLOW),
    )(pltpu.with_memory_space_constraint(block, pltpu.HBM),
      pltpu.with_memory_space_constraint(lax.empty(land_shape, block.dtype), pltpu.HBM))


def _gather_behind_wait(name, started, after):
    n_sem = 2 * (N_DEV - 1)
    sems, block_thru, land_thru = started[:n_sem], started[n_sem], started[n_sem + 1]

    def body(src_ref, land_ref, *rest):
        send_sems, recv_sems = rest[:N_DEV - 1], rest[N_DEV - 1:n_sem]
        for d in range(1, N_DEV):
            peer, pidx = _peer(d)
            copy = pltpu.make_async_remote_copy(src_ref=src_ref, dst_ref=land_ref.at[pidx], send_sem=send_sems[d - 1],
                                                recv_sem=recv_sems[d - 1], device_id=peer,
                                                device_id_type=pl.DeviceIdType.MESH)
            copy.wait_send()
            copy.wait_recv()

    return pl.pallas_call(
        body, name=name,
        out_shape=(pltpu.HBM(block_thru.shape, block_thru.dtype), pltpu.HBM(land_thru.shape, land_thru.dtype)),
        in_specs=tuple([_HBM, _HBM] + [_SEM] * n_sem + [pl.BlockSpec(memory_space=pl.ANY)]), out_specs=(_HBM, _HBM),
        input_output_aliases={0: 0, 1: 1},
        compiler_params=pltpu.CompilerParams(has_side_effects=_DATAFLOW),
    )(block_thru, land_thru, *sems, after)[1]


def _gather_two_level(name, arrays):
    n = len(arrays)

    def body(*refs):
        srcs, dsts = refs[:n], refs[n:2 * n]
        send_sems, recv_sems, local_sems = refs[2 * n:]
        x, y, c = lax.axis_index("x"), lax.axis_index("y"), lax.axis_index("c")
        sibling = (x, y, 1 - c)
        near = ((x + 1 - c) % 2, (y + c) % 2)
        far = ((x + c) % 2, (y + 1 - c) % 2)
        diag = ((x + 1) % 2, (y + 1) % 2)
        near_slot, far_slot = 1 + c, 2 - c

        def index(px, py, pc):
            return 4 * px + 2 * py + pc

        def copy(k, slot, block, to, src=None):
            return pltpu.make_async_remote_copy(
                src_ref=dsts[k].at[index(*block)] if src is None else src, dst_ref=dsts[k].at[index(*block)],
                send_sem=send_sems.at[k * 7 + slot], recv_sem=recv_sems.at[k * 7 + slot],
                device_id=to, device_id_type=pl.DeviceIdType.MESH)

        me = (x, y, c)
        local = [pltpu.make_async_copy(srcs[k], dsts[k].at[index(*me)], local_sems.at[k]) for k in range(n)]
        started = [copy(k, 0, me, sibling, src=srcs[k]) for k in range(n)]
        started += [copy(k, near_slot, me, (*near, c), src=srcs[k]) for k in range(n)]
        started += [copy(k, far_slot, me, (*far, c), src=srcs[k]) for k in range(n)]
        for cp in local + started:
            cp.start()
        for k in range(n):
            copy(k, near_slot, (*near, c), me).wait_recv()
            passed = [copy(k, 3, (*near, c), (*far, c)), copy(k, 3 + near_slot, (*near, c), sibling)]
            for cp in passed:
                cp.start()
            started += passed
        for slot, chip in ((far_slot, far), (3, diag)):
            for k in range(n):
                copy(k, slot, (*chip, c), me).wait_recv()
                passed = copy(k, 3 + slot, (*chip, c), sibling)
                passed.start()
                started.append(passed)
        for k in range(n):
            copy(k, 0, sibling, me).wait_recv()
            for slot, chip in ((near_slot, near), (far_slot, far), (3, diag)):
                copy(k, 3 + slot, (*chip, 1 - c), me).wait_recv()
        for cp in started:
            cp.wait_send()
        for cp in local:
            cp.wait()

    any_spec = pl.BlockSpec(memory_space=pl.ANY)
    return pl.pallas_call(
        body, name=name, out_shape=[jax.ShapeDtypeStruct((N_DEV,) + a.shape, a.dtype) for a in arrays],
        in_specs=[any_spec] * n, out_specs=[any_spec] * n,
        scratch_shapes=[pltpu.SemaphoreType.DMA((7 * n,)), pltpu.SemaphoreType.DMA((7 * n,)),
                        pltpu.SemaphoreType.DMA((n,))],
        compiler_params=pltpu.CompilerParams(has_side_effects=True),
    )(*arrays)


def _mod_shard(c_all, w_ada, b_shard):
    def body(c_ref, w_ref, b_ref, o_ref):
        cv = c_ref[...]
        ca = cv * _sigmoid(cv)
        o_ref[...] = jnp.dot(ca.astype(bf16), w_ref[...].astype(bf16), preferred_element_type=f32) + b_ref[...]

    return pl.pallas_call(body, name="mod_shard", out_shape=jax.ShapeDtypeStruct((N_DEV, w_ada.shape[1]), f32),
                          compiler_params=_params())(c_all, w_ada, b_shard)


def _norm_mod(x, norm_g, scale1p, shift):
    s_len = x.shape[0]
    tm = 512

    def body(x_ref, g_ref, sc_ref, sh_ref, h_ref):
        xb = x_ref[...]
        r = lax.rsqrt(jnp.mean(xb * xb, axis=-1, keepdims=True) + EPS)
        h_ref[...] = ((xb * r * g_ref[...]) * sc_ref[...] + sh_ref[...]).astype(bf16)

    row = pl.BlockSpec((tm, D_MODEL), lambda i: (i, 0))
    vec = pl.BlockSpec((1, D_MODEL), lambda i: (0, 0))
    return pl.pallas_call(body, name="norm_mod", grid=(s_len // tm,), in_specs=[row, vec, vec, vec], out_specs=row,
                          out_shape=jax.ShapeDtypeStruct((s_len, D_MODEL), bf16),
                          compiler_params=_params("parallel"))(x, norm_g, scale1p, shift)


def _in_proj(h, wt_main, wt_small):
    s_len = h.shape[0]
    tm, tn = min(1024, s_len), 1024

    def body(h_ref, w_ref, ws_ref, p_ref, ps_ref):
        @pl.when(pl.program_id(1) == 0)
        def _():
            ps_ref[...] = _dg(h_ref[...], ws_ref[...], 1, 1)

        p_ref[...] = _dg(h_ref[...], w_ref[...], 1, 1)

    return pl.pallas_call(
        body, name="in_proj", grid=(s_len // tm, N_MAIN // tn),
        in_specs=[pl.BlockSpec((tm, D_MODEL), lambda i, j: (i, 0)),
                  pl.BlockSpec((tn, D_MODEL), lambda i, j: (j, 0)),
                  pl.BlockSpec((N_SMALL, D_MODEL), lambda i, j: (0, 0))],
        out_specs=[pl.BlockSpec((tm, tn), lambda i, j: (i, j)),
                   pl.BlockSpec((tm, N_SMALL), lambda i, j: (i, 0))],
        out_shape=[jax.ShapeDtypeStruct((s_len, N_MAIN), f32), jax.ShapeDtypeStruct((s_len, N_SMALL), f32)],
        compiler_params=_params("parallel", "arbitrary"),
    )(h, wt_main, wt_small)


PREP_TM = 256
HALO = 8


def _conv_section(xe_ref, cw_ref, cols, tm):
    acc = cw_ref[pl.ds(CONV_K - 1, 1), cols] * xe_ref[pl.ds(HALO, tm), :]
    for tap in range(CONV_K - 1):
        acc = acc + cw_ref[pl.ds(tap, 1), cols] * xe_ref[pl.ds(HALO - (CONV_K - 1) + tap, tm), :]
    return acc


def _small_fwd(ps, bvec, alog):
    z = ps + bvec
    logsig, softp = _softplus_parts(z)
    gval = -jnp.exp(alog) * softp
    beta = _sigmoid(ps)
    return z, logsig, gval, beta


def _head_lane(block, lane):
    return jnp.sum(jnp.where(_iota(block.shape, 1) == lane, block, 0.0), axis=1, keepdims=True)


def _head_slab(block, lane):
    return jnp.broadcast_to(_head_lane(block, lane), block.shape)


def _chunk_masks(tm):
    r, c = _iota((tm, tm), 0), _iota((tm, tm), 1)
    same = (r // CHUNK) == (c // CHUNK)
    return (same & (r >= c)).astype(f32), same.astype(f32)


def _prep(p_main, p_small, qn_g, kn_g, conv_w, bvec, alog):
    s_len = p_main.shape[0]
    tm = PREP_TM
    nb = s_len // tm

    def body(fq_ref, fk_ref, fv_ref, gq_ref, gk_ref, gv_ref, hq_ref, hk_ref, hv_ref, ps_ref, qg_ref, kg_ref,
             cw_ref, bv_ref, al_ref,
             qs_ref, kn_ref, vb_ref, gqo_ref, gko_ref, gvo_ref, small_ref, xe_sc, carry_sc):
        i = pl.program_id(0)

        @pl.when(i == 0)
        def _():
            carry_sc[...] = jnp.zeros_like(carry_sc)

        vb_ref[...] = fv_ref[...].astype(bf16)

        first = i == 0
        for sec, (x_ref, halo_ref, o_ref) in enumerate(((gq_ref, hq_ref, gqo_ref), (gk_ref, hk_ref, gko_ref),
                                                        (gv_ref, hv_ref, gvo_ref))):
            xe_sc[0:HALO, :] = jnp.where(first, 0.0, halo_ref[...])
            xe_sc[HALO:, :] = x_ref[...]
            cv = _conv_section(xe_sc, cw_ref, slice(sec * WIDTH, (sec + 1) * WIDTH), tm)
            y = cv * _sigmoid(cv)
            if sec == 2:
                o_ref[...] = y
            else:
                mul = QK_SCALE if sec == 0 else 1.0
                for h in range(HEADS):
                    sl = slice(h * HEAD_DIM, (h + 1) * HEAD_DIM)
                    yh = y[:, sl]
                    o_ref[:, sl] = yh * (lax.rsqrt(jnp.sum(yh * yh, axis=-1, keepdims=True) + EPS) * mul)

        lane = _iota((tm, N_SMALL), 1)
        _, logsig, gval, beta = _small_fwd(ps_ref[...], bv_ref[...], al_ref[...])
        lf = jnp.where(lane < HEADS, logsig, 0.0)
        tri = (_iota((tm, tm), 0) >= _iota((tm, tm), 1)).astype(f32)
        fcum = jnp.dot(tri, lf, preferred_element_type=f32, precision=HI) + carry_sc[...]
        carry_sc[...] += jnp.sum(lf, axis=0, keepdims=True)
        tri_c, ones_c = _chunk_masks(tm)
        g_lanes = jnp.where((lane >= LANE_G) & (lane < LANE_G + HEADS), gval, 0.0)
        gc = jnp.dot(tri_c, g_lanes, preferred_element_type=f32, precision=HI)
        g_last = jnp.dot(ones_c, g_lanes, preferred_element_type=f32, precision=HI)
        small = jnp.where(lane < LANE_G, fcum, jnp.where(lane < LANE_BETA, gval, jnp.where(lane < LANE_GC, beta, 0.0)))
        small_ref[...] = small + pltpu.roll(gc, LANE_GC - LANE_G, 1) + pltpu.roll(g_last, LANE_GLAST - LANE_G, 1)

        qg, kg = qg_ref[...], kg_ref[...]
        f2 = fcum * LOG2E
        for h in range(HEADS):
            sl = slice(h * HEAD_DIM, (h + 1) * HEAD_DIM)
            q = fq_ref[:, sl]
            rq = lax.rsqrt(jnp.mean(q * q, axis=-1, keepdims=True) + EPS)
            k = fk_ref[:, sl]
            rk = lax.rsqrt(jnp.mean(k * k, axis=-1, keepdims=True) + EPS)
            f_col = _head_lane(f2, LANE_F + h)
            hi = f_col.astype(bf16).astype(f32)
            mid = (f_col - hi).astype(bf16).astype(f32)
            lo = f_col - hi - mid
            q_bias = jnp.where(lane == 0, hi, jnp.where(lane == 1, mid, jnp.where(lane == 2, lo,
                                                                                  jnp.where(lane < 6, 1.0, 0.0))))
            k_bias = jnp.where(lane < 3, 1.0, jnp.where(lane == 3, -hi, jnp.where(lane == 4, -mid,
                                                                                 jnp.where(lane == 5, -lo, 0.0))))
            base = 2 * h * HEAD_DIM
            qs_ref[:, base:base + HEAD_DIM] = (q * rq * qg * (QK_SCALE * LOG2E)).astype(bf16)
            qs_ref[:, base + HEAD_DIM:base + 2 * HEAD_DIM] = q_bias.astype(bf16)
            kn_ref[:, base:base + HEAD_DIM] = (k * rk * kg).astype(bf16)
            kn_ref[:, base + HEAD_DIM:base + 2 * HEAD_DIM] = k_bias.astype(bf16)

    def col(cb):
        return pl.BlockSpec((tm, WIDTH), lambda i: (i, cb))

    def halo(cb):
        return pl.BlockSpec((HALO, WIDTH), lambda i: (jnp.maximum(i * (tm // HALO) - 1, 0), cb))

    vec = pl.BlockSpec((1, LANES), lambda i: (0, 0))
    wide_f32 = jax.ShapeDtypeStruct((s_len, WIDTH), f32)
    wide_bf = jax.ShapeDtypeStruct((s_len, WIDTH), bf16)
    out_col = pl.BlockSpec((tm, WIDTH), lambda i: (i, 0))
    return pl.pallas_call(
        body, name="prep", grid=(nb,),
        in_specs=[col(0), col(1), col(2), col(4), col(5), col(6), halo(4), halo(5), halo(6),
                  pl.BlockSpec((tm, N_SMALL), lambda i: (i, 0)), vec, vec,
                  pl.BlockSpec((CONV_K, 3 * WIDTH), lambda i: (0, 0)), vec, vec],
        out_specs=[pl.BlockSpec((tm, 2 * WIDTH), lambda i: (i, 0))] * 2 + [out_col] * 4
                  + [pl.BlockSpec((tm, N_SMALL), lambda i: (i, 0))],
        out_shape=[jax.ShapeDtypeStruct((s_len, 2 * WIDTH), bf16)] * 2 + [wide_bf, wide_f32, wide_f32, wide_f32,
                                                                          jax.ShapeDtypeStruct((s_len, N_SMALL), f32)],
        scratch_shapes=[pltpu.VMEM((tm + HALO, WIDTH), f32), pltpu.VMEM((1, N_SMALL), f32)],
        compiler_params=_params("arbitrary"),
    )(p_main, p_main, p_main, p_main, p_main, p_main, p_main, p_main, p_main, p_small, qn_g, kn_g, conv_w, bvec, alog)


FOX_T = 1024
NEG_BIG = -1e30


def _fox_fwd(qs, kn, vb):
    s_len = qs.shape[0]
    t = FOX_T
    nq = s_len // t

    def body(q_ref, k_ref, v_ref, o_ref, lse_ref):
        qi = pl.program_id(1)
        q = q_ref[...]

        causal = _iota((t, t), 0) >= _iota((t, t), 1)

        def step(j, carry, masked):
            m, l, acc = carry
            rows = pl.ds(pl.multiple_of(j * t, t), t)
            s = _dg(q, k_ref[rows, :], 1, 1)
            if masked:
                s = jnp.where(causal, s, NEG_BIG)
            m_new = jnp.maximum(m, jnp.max(s, axis=-1, keepdims=True))
            p = jnp.exp2(s - m_new)
            alpha = jnp.exp2(m - m_new)
            l = alpha * l + jnp.sum(p, axis=-1, keepdims=True)
            acc = alpha * acc + jnp.dot(p.astype(bf16), v_ref[rows, :], preferred_element_type=f32)
            return m_new, l, acc

        init = (jnp.full((t, 1), NEG_BIG, f32), jnp.zeros((t, 1), f32), jnp.zeros((t, HEAD_DIM), f32))
        carry = lax.fori_loop(0, qi, lambda j, c: step(j, c, False), init)
        m, l, acc = step(qi, carry, True)
        o_ref[...] = acc / l
        lse_ref[0] = m + jnp.log2(l)

    return pl.pallas_call(
        body, name="fox_fwd", grid=(HEADS, nq),
        in_specs=[pl.BlockSpec((t, 2 * HEAD_DIM), lambda h, i: (i, h)),
                  pl.BlockSpec((s_len, 2 * HEAD_DIM), lambda h, i: (0, h)),
                  pl.BlockSpec((s_len, HEAD_DIM), lambda h, i: (0, h))],
        out_specs=[pl.BlockSpec((t, HEAD_DIM), lambda h, i: (i, h)),
                   pl.BlockSpec((1, t, 1), lambda h, i: (h, i, 0))],
        out_shape=[jax.ShapeDtypeStruct((s_len, WIDTH), f32), jax.ShapeDtypeStruct((HEADS, s_len, 1), f32)],
        compiler_params=_params("parallel", "arbitrary"),
    )(qs, kn, vb)


def _fox_bwd(qs, kn, vb, do, lse, delta):
    s_len = qs.shape[0]
    t = FOX_T
    nq = s_len // t
    half = t // 2

    def body(q_ref, do_ref, lse_ref, dl_ref, k_ref, v_ref, dq_ref, dk_ref, dvb_ref, df_ref, dfq_ref, dv_ref):
        head, qi = pl.program_id(0), pl.program_id(1)

        @pl.when(qi == 0)
        def _():
            dk_ref[...] = jnp.zeros_like(dk_ref)
            dv_ref[...] = jnp.zeros_like(dv_ref)
            df_ref[...] = jnp.zeros_like(df_ref)

        lse_col = lse_ref[0]
        dl = _head_lane(dl_ref[...], head)

        def update(q_rows, k_rows, df_lanes, j, carry, mask):
            dq, row_sum = carry
            q, do_b = q_ref[q_rows, :], do_ref[q_rows, :]
            p = jnp.exp2(_dg(q, k_ref[k_rows, :], 1, 1) - lse_col[q_rows])
            if mask is not None:
                p = jnp.where(mask, p, 0.0)
            ds = p * (_dg(do_b, v_ref[k_rows, :], 1, 1) - dl[q_rows])
            ds_b = ds.astype(bf16)
            dk_ref[k_rows, :] += _dg(ds_b, q_ref[q_rows, 0:HEAD_DIM], 0, 0)
            dv_ref[k_rows, :] += _dg(p.astype(bf16), do_b, 0, 0)
            df_ref[0, j, :, df_lanes] += -jnp.sum(ds, axis=0, keepdims=True)
            dq = dq + jnp.dot(ds_b, k_ref[k_rows, 0:HEAD_DIM], preferred_element_type=f32)
            return dq, row_sum + jnp.sum(ds, axis=-1, keepdims=True)

        everything, upper, lower = slice(0, t), slice(0, half), slice(half, t)
        carry = lax.fori_loop(
            0, qi, lambda j, c: update(everything, pl.ds(pl.multiple_of(j * t, t), t), everything, j, c, None),
            (jnp.zeros((t, HEAD_DIM), f32), jnp.zeros((t, 1), f32)))
        carry = update(everything, pl.ds(pl.multiple_of(qi * t, t), half), upper, qi, carry,
                       _iota((t, half), 0) >= _iota((t, half), 1))
        low = update(lower, pl.ds(pl.multiple_of(qi * t + half, half), half), lower, qi,
                     tuple(c[half:] for c in carry), _iota((half, half), 0) >= _iota((half, half), 1))
        dq, row_sum = (jnp.concatenate([c[:half], lo], axis=0) for c, lo in zip(carry, low))
        dq_ref[...] = dq
        dfq_ref[0] = row_sum

        @pl.when(qi == nq - 1)
        def _():
            dvb_ref[...] = dv_ref[...].astype(bf16)

    blk = pl.BlockSpec((t, HEAD_DIM), lambda h, i: (i, h))
    blk2 = pl.BlockSpec((t, 2 * HEAD_DIM), lambda h, i: (i, h))
    full = pl.BlockSpec((s_len, HEAD_DIM), lambda h, i: (0, h))
    full2 = pl.BlockSpec((s_len, 2 * HEAD_DIM), lambda h, i: (0, h))
    colv = pl.BlockSpec((1, t, 1), lambda h, i: (h, i, 0))
    rowv = pl.BlockSpec((1, nq, 1, t), lambda h, i: (h, 0, 0, 0))
    lanes = pl.BlockSpec((t, N_SMALL), lambda h, i: (i, 0))
    wide = jax.ShapeDtypeStruct((s_len, WIDTH), f32)
    return pl.pallas_call(
        body, name="fox_bwd", grid=(HEADS, nq),
        in_specs=[blk2, blk, colv, lanes, full2, full],
        out_specs=[blk, full, full, rowv, colv],
        out_shape=[wide, wide, jax.ShapeDtypeStruct((s_len, WIDTH), bf16), jax.ShapeDtypeStruct((HEADS, nq, 1, t), f32),
                   jax.ShapeDtypeStruct((HEADS, s_len, 1), f32)],
        scratch_shapes=[pltpu.VMEM((s_len, HEAD_DIM), f32)],
        compiler_params=_params("parallel", "arbitrary"),
    )(qs, do, lse, delta, kn, vb)


INTRA_CHUNKS = 8
SCAN_FWD_CHUNKS = 8
SCAN_BWD_CHUNKS = 4


def _gdn_intra_fwd(gq, gk, gv, small):
    s_len = gq.shape[0]
    cpb = INTRA_CHUNKS
    rows_blk = cpb * CHUNK
    n_chunks = s_len // CHUNK

    def body(q_ref, k_ref, v_ref, sm_ref, u_ref, w_ref, qg_ref, kd_ref, attn_ref, t_ref, eg_ref):
        head = pl.program_id(0)
        sm = sm_ref[...]
        gc_b, gl_b, beta_b = (_head_slab(sm, LANE_GC + head), _head_slab(sm, LANE_GLAST + head),
                              _head_slab(sm, LANE_BETA + head))
        ms, rhss = [], []
        for ci in range(cpb):
            rows = pl.ds(ci * CHUNK, CHUNK)
            sl = slice(ci * CHUNK, (ci + 1) * CHUNK)
            m, rhs, qg, kd, attn, eg_last = _gdn_intra_pre(q_ref[rows, :], k_ref[rows, :], v_ref[rows, :],
                                                           gc_b[sl], gl_b[sl], beta_b[sl], _BF_PLAIN[1])
            qg_ref[rows, :] = qg.astype(bf16)
            kd_ref[rows, :] = kd.astype(bf16)
            attn_ref[0, ci] = attn.astype(bf16)
            eg_ref[0, ci] = eg_last
            ms.append(m)
            rhss.append(rhs)
        for ci, (t, rhs) in enumerate(zip(_inv_unit_lower_many(ms), rhss)):
            rows = pl.ds(ci * CHUNK, CHUNK)
            t_ref[0, ci] = t
            uw = _X3_PLAIN[0](t, rhs)
            u_ref[rows, :] = uw[:, :HEAD_DIM]
            w_ref[rows, :] = uw[:, HEAD_DIM:].astype(bf16)

    blk = pl.BlockSpec((rows_blk, HEAD_DIM), lambda h, i: (i, h))
    sq = pl.BlockSpec((1, cpb, CHUNK, CHUNK), lambda h, i: (h, i, 0, 0))
    wide_bf = jax.ShapeDtypeStruct((s_len, WIDTH), bf16)
    return pl.pallas_call(
        body, name="gdn_intra_fwd", grid=(HEADS, s_len // rows_blk),
        in_specs=[blk] * 3 + [pl.BlockSpec((rows_blk, N_SMALL), lambda h, i: (i, 0))],
        out_specs=[blk] * 4 + [sq, sq, pl.BlockSpec((1, cpb, SUBLANES, HEAD_DIM), lambda h, i: (h, i, 0, 0))],
        out_shape=[jax.ShapeDtypeStruct((s_len, WIDTH), f32), wide_bf, wide_bf, wide_bf,
                   jax.ShapeDtypeStruct((HEADS, n_chunks, CHUNK, CHUNK), bf16),
                   jax.ShapeDtypeStruct((HEADS, n_chunks, CHUNK, CHUNK), f32),
                   jax.ShapeDtypeStruct((HEADS, n_chunks, SUBLANES, HEAD_DIM), f32)],
        compiler_params=_params("parallel", "parallel"),
    )(gq, gk, gv, small)


def _gdn_scan_fwd(u, w, qg, kd, attn, eg):
    s_len = u.shape[0]
    cpb = SCAN_FWD_CHUNKS
    rows_blk = cpb * CHUNK
    n_chunks = s_len // CHUNK

    def body(u_ref, w_ref, qg_ref, kd_ref, attn_ref, eg_ref, o_ref, st_ref, s_sc):
        @pl.when(pl.program_id(0) == 0)
        def _():
            s_sc[...] = jnp.zeros_like(s_sc)

        def chunk(ci, _):
            rows = pl.ds(pl.multiple_of(ci * CHUNK, CHUNK), CHUNK)
            cols = [slice(h * HEAD_DIM, (h + 1) * HEAD_DIM) for h in range(HEADS)]
            s0 = [s_sc[h] for h in range(HEADS)]
            s0_b = [s.astype(bf16) for s in s0]
            for h in range(HEADS):
                st_ref[h, ci] = s0[h]
            ws = [jnp.dot(w_ref[rows, cols[h]], s0_b[h], preferred_element_type=f32) for h in range(HEADS)]
            qs = [jnp.dot(qg_ref[rows, cols[h]], s0_b[h], preferred_element_type=f32) for h in range(HEADS)]
            vn_b = [(u_ref[rows, cols[h]] - ws[h]).astype(bf16) for h in range(HEADS)]
            av = [jnp.dot(attn_ref[h, ci], vn_b[h], preferred_element_type=f32) for h in range(HEADS)]
            kv = [_dg(kd_ref[rows, cols[h]], vn_b[h], 0, 0) for h in range(HEADS)]
            for h in range(HEADS):
                o_ref[rows, cols[h]] = qs[h] + av[h]
                s_sc[h] = _scale_rows(s0[h], eg_ref[h, ci]) + kv[h]
            return 0

        lax.fori_loop(0, cpb, chunk, 0)

    row = pl.BlockSpec((rows_blk, WIDTH), lambda i: (i, 0))
    return pl.pallas_call(
        body, name="gdn_scan_fwd", grid=(s_len // rows_blk,),
        in_specs=[row] * 4 + [pl.BlockSpec((HEADS, cpb, CHUNK, CHUNK), lambda i: (0, i, 0, 0)),
                              pl.BlockSpec((HEADS, cpb, SUBLANES, HEAD_DIM), lambda i: (0, i, 0, 0))],
        out_specs=[row, pl.BlockSpec((HEADS, cpb, HEAD_DIM, HEAD_DIM), lambda i: (0, i, 0, 0))],
        out_shape=[jax.ShapeDtypeStruct((s_len, WIDTH), f32),
                   jax.ShapeDtypeStruct((HEADS, n_chunks, HEAD_DIM, HEAD_DIM), f32)],
        scratch_shapes=[pltpu.VMEM((HEADS, HEAD_DIM, HEAD_DIM), f32)],
        compiler_params=_params("arbitrary"),
    )(u, w, qg, kd, attn, eg)


def _gdn_scan_bwd(u, w, qg, kd, attn, eg, states, d_o):
    s_len = u.shape[0]
    cpb = SCAN_BWD_CHUNKS
    rows_blk = cpb * CHUNK
    n_chunks = s_len // CHUNK
    nb = s_len // rows_blk

    def body(u_ref, w_ref, qg_ref, kd_ref, attn_ref, eg_ref, st_ref, do_ref,
             du_ref, dw_ref, dqg_ref, dkd_ref, dattn_ref, deg_ref, ds_sc):
        @pl.when(pl.program_id(0) == 0)
        def _():
            ds_sc[...] = jnp.zeros_like(ds_sc)

        def chunk(step, _):
            ci = cpb - 1 - step
            rows = pl.ds(pl.multiple_of(ci * CHUNK, CHUNK), CHUNK)
            hs = range(HEADS)
            cols = [slice(h * HEAD_DIM, (h + 1) * HEAD_DIM) for h in hs]
            s0 = [st_ref[h, ci] for h in hs]
            s0_b = [s.astype(bf16) for s in s0]
            ds1 = [ds_sc[h] for h in hs]
            ds1_b = [d.astype(bf16) for d in ds1]
            do_b = [do_ref[rows, cols[h]].astype(bf16) for h in hs]
            ws = [jnp.dot(w_ref[rows, cols[h]], s0_b[h], preferred_element_type=f32) for h in hs]
            ad = [_dg(attn_ref[h, ci], do_b[h], 0, 0) for h in hs]
            kd_ds = [jnp.dot(kd_ref[rows, cols[h]], ds1_b[h], preferred_element_type=f32) for h in hs]
            dqg = [_dg(do_b[h], s0_b[h], 1, 1) for h in hs]
            qd = [_dg(qg_ref[rows, cols[h]], do_b[h], 0, 0) for h in hs]
            vn_b = [(u_ref[rows, cols[h]] - ws[h]).astype(bf16) for h in hs]
            dvn = [ad[h] + kd_ds[h] for h in hs]
            dvn_b = [d.astype(bf16) for d in dvn]
            dattn = [_dg(do_b[h], vn_b[h], 1, 1) for h in hs]
            dkd = [_dg(vn_b[h], ds1_b[h], 1, 1) for h in hs]
            dw = [_dg(dvn_b[h], s0_b[h], 1, 1) for h in hs]
            wd = [_dg(w_ref[rows, cols[h]], dvn_b[h], 0, 0) for h in hs]
            for h in hs:
                dattn_ref[h, ci] = dattn[h]
                dqg_ref[rows, cols[h]] = dqg[h]
                dkd_ref[rows, cols[h]] = dkd[h]
                du_ref[rows, cols[h]] = dvn[h]
                dw_ref[rows, cols[h]] = -dw[h]
                ds_sc[h] = qd[h] - wd[h] + _scale_rows(ds1[h], eg_ref[h, ci])
                deg_ref[h, ci] = jnp.sum((ds1[h] * s0[h]).reshape(HEAD_DIM // SUBLANES, SUBLANES, HEAD_DIM), axis=0)
            return 0

        lax.fori_loop(0, cpb, chunk, 0)

    row = pl.BlockSpec((rows_blk, WIDTH), lambda i: (nb - 1 - i, 0))
    sq = pl.BlockSpec((HEADS, cpb, CHUNK, CHUNK), lambda i: (0, nb - 1 - i, 0, 0))
    egs = pl.BlockSpec((HEADS, cpb, SUBLANES, HEAD_DIM), lambda i: (0, nb - 1 - i, 0, 0))
    wide = jax.ShapeDtypeStruct((s_len, WIDTH), f32)
    return pl.pallas_call(
        body, name="gdn_scan_bwd", grid=(nb,),
        in_specs=[row] * 4 + [sq, egs, pl.BlockSpec((HEADS, cpb, HEAD_DIM, HEAD_DIM), lambda i: (0, nb - 1 - i, 0, 0)), row],
        out_specs=[row] * 4 + [sq, egs],
        out_shape=[wide] * 4 + [jax.ShapeDtypeStruct((HEADS, n_chunks, CHUNK, CHUNK), f32),
                                jax.ShapeDtypeStruct((HEADS, n_chunks, SUBLANES, HEAD_DIM), f32)],
        scratch_shapes=[pltpu.VMEM((HEADS, HEAD_DIM, HEAD_DIM), f32)],
        compiler_params=_params("arbitrary"),
    )(u, w, qg, kd, attn, eg, states, d_o)


def _gdn_intra_bwd(gq, gk, gv, small, t_inv, du, dw, dqg, dkd, dattn, deg):
    s_len = gq.shape[0]
    cpb = INTRA_CHUNKS
    rows_blk = cpb * CHUNK

    def body(q_ref, k_ref, v_ref, sm_ref, t_ref, du_ref, dw_ref, dqg_ref, dkd_ref, dattn_ref, deg_ref,
             dq_ref, dk_ref, dv_ref, dsm_ref):
        head = pl.program_id(1)

        def batch(value):
            return value.reshape(cpb, CHUNK, HEAD_DIM)

        sm = sm_ref[...]
        slabs = [batch(_head_slab(sm, first + head)) for first in (LANE_GC, LANE_GLAST, LANE_BETA)]
        t_known = t_ref[0]
        _, vjp = jax.vjp(lambda q, k, v, gc, gl, b: _gdn_intra(q, k, v, gc, gl, b, t_known),
                         batch(q_ref[...]), batch(k_ref[...]), batch(v_ref[...]), *slabs)
        duw = jnp.concatenate([batch(du_ref[...]), batch(dw_ref[...])], axis=-1)
        dq, dk, dv, dgc, dgl, db = vjp((duw, batch(dqg_ref[...]), batch(dkd_ref[...]), dattn_ref[0], deg_ref[0]))
        for ref, grad in zip((dq_ref, dk_ref, dv_ref), (dq, dk, dv)):
            ref[...] = grad.reshape(rows_blk, HEAD_DIM)

        @pl.when(head == 0)
        def _():
            dsm_ref[...] = jnp.zeros_like(dsm_ref)

        lane = _iota((rows_blk, N_SMALL), 1)
        acc = dsm_ref[...]
        for first, grad in ((LANE_GC, dgc), (LANE_GLAST, dgl), (LANE_BETA, db)):
            col = jnp.sum(grad.reshape(rows_blk, HEAD_DIM), axis=1, keepdims=True)
            acc = acc + jnp.where(lane == first + head, col, 0.0)
        dsm_ref[...] = acc

    blk = pl.BlockSpec((rows_blk, HEAD_DIM), lambda i, h: (i, h))
    sq = pl.BlockSpec((1, cpb, CHUNK, CHUNK), lambda i, h: (h, i, 0, 0))
    egs = pl.BlockSpec((1, cpb, SUBLANES, HEAD_DIM), lambda i, h: (h, i, 0, 0))
    lanes = pl.BlockSpec((rows_blk, N_SMALL), lambda i, h: (i, 0))
    wide = jax.ShapeDtypeStruct((s_len, WIDTH), f32)
    return pl.pallas_call(
        body, name="gdn_intra_bwd", grid=(s_len // rows_blk, HEADS),
        in_specs=[blk] * 3 + [lanes, sq] + [blk] * 4 + [sq, egs],
        out_specs=[blk] * 3 + [lanes],
        out_shape=[wide] * 3 + [jax.ShapeDtypeStruct((s_len, N_SMALL), f32)],
        compiler_params=_params("parallel", "arbitrary"),
    )(gq, gk, gv, small, t_inv, du, dw, dqg, dkd, dattn, deg)


MIX_TM = 512


def _mix_fwd(fox_o, gdn_o, p_main, gnorm_g):
    s_len = fox_o.shape[0]
    tm = MIX_TM

    def body(fo_ref, go_ref, fz_ref, gz_ref, g_ref, mixed_ref):
        fz = fz_ref[...]
        mixed_ref[:, 0:WIDTH] = (fo_ref[...] * (fz * _sigmoid(fz))).astype(bf16)
        gz = gz_ref[...]
        gate = gz * _sigmoid(gz)
        gg = g_ref[...]
        for h in range(HEADS):
            sl = slice(h * HEAD_DIM, (h + 1) * HEAD_DIM)
            o = go_ref[:, sl]
            r = lax.rsqrt(jnp.mean(o * o, axis=-1, keepdims=True) + EPS)
            mixed_ref[:, WIDTH + h * HEAD_DIM:WIDTH + (h + 1) * HEAD_DIM] = (o * r * gg * gate[:, sl]).astype(bf16)

    row = pl.BlockSpec((tm, WIDTH), lambda i: (i, 0))
    return pl.pallas_call(
        body, name="mix_fwd", grid=(s_len // tm,),
        in_specs=[row, row, pl.BlockSpec((tm, WIDTH), lambda i: (i, 3)), pl.BlockSpec((tm, WIDTH), lambda i: (i, 7)),
                  pl.BlockSpec((1, LANES), lambda i: (0, 0))],
        out_specs=pl.BlockSpec((tm, 2 * WIDTH), lambda i: (i, 0)),
        out_shape=jax.ShapeDtypeStruct((s_len, 2 * WIDTH), bf16),
        compiler_params=_params("parallel"),
    )(fox_o, gdn_o, p_main, p_main, gnorm_g)


def _silu_and_grad(z):
    sg = _sigmoid(z)
    return z * sg, sg * (1.0 + z * (1.0 - sg))


def _mix_bwd(dmixed, fox_o, gdn_o, p_main, gnorm_g):
    s_len = fox_o.shape[0]
    tm = MIX_TM

    def body(dm_ref, fo_ref, go_ref, fz_ref, gz_ref, g_ref, dof_ref, delta_ref, dfz_ref, dgz_ref, dgo_ref, dg_ref):
        @pl.when(pl.program_id(0) == 0)
        def _():
            dg_ref[...] = jnp.zeros_like(dg_ref)

        lane = _iota((tm, LANES), 1)
        fz = fz_ref[...]
        dmf = dm_ref[:, 0:WIDTH]
        fo = fo_ref[...]
        f_gate, f_grad = _silu_and_grad(fz)
        dof = dmf * f_gate
        dof_ref[...] = dof.astype(bf16)
        dfz_ref[...] = (dmf * fo * f_grad).astype(bf16)
        prod = dof * fo
        delta = jnp.zeros((tm, LANES), f32)
        for h in range(HEADS):
            dh = jnp.sum(prod[:, h * HEAD_DIM:(h + 1) * HEAD_DIM], axis=-1, keepdims=True)
            delta = jnp.where(lane == h, dh, delta)
        delta_ref[...] = delta

        gz = gz_ref[...]
        dmg = dm_ref[:, WIDTH:2 * WIDTH]
        gate, sgrad = _silu_and_grad(gz)
        gg = g_ref[...]
        dg_acc = jnp.zeros((1, HEAD_DIM), f32)
        for h in range(HEADS):
            sl = slice(h * HEAD_DIM, (h + 1) * HEAD_DIM)
            o = go_ref[:, sl]
            r = lax.rsqrt(jnp.mean(o * o, axis=-1, keepdims=True) + EPS)
            on = o * r
            dmh = dmg[:, sl]
            dgz_ref[:, sl] = (dmh * (on * gg) * sgrad[:, sl]).astype(bf16)
            dy = dmh * gate[:, sl]
            dg_acc = dg_acc + jnp.sum(dy * on, axis=0, keepdims=True)
            tt = dy * gg
            dgo_ref[:, sl] = r * (tt - on * jnp.mean(tt * on, axis=-1, keepdims=True))
        dg_ref[...] += dg_acc

    row = pl.BlockSpec((tm, WIDTH), lambda i: (i, 0))
    wide_bf = jax.ShapeDtypeStruct((s_len, WIDTH), bf16)
    return pl.pallas_call(
        body, name="mix_bwd", grid=(s_len // tm,),
        in_specs=[pl.BlockSpec((tm, 2 * WIDTH), lambda i: (i, 0)), row, row,
                  pl.BlockSpec((tm, WIDTH), lambda i: (i, 3)), pl.BlockSpec((tm, WIDTH), lambda i: (i, 7)),
                  pl.BlockSpec((1, LANES), lambda i: (0, 0))],
        out_specs=[row, pl.BlockSpec((tm, LANES), lambda i: (i, 0)), row, row, row,
                   pl.BlockSpec((1, LANES), lambda i: (0, 0))],
        out_shape=[wide_bf, jax.ShapeDtypeStruct((s_len, LANES), f32), wide_bf, wide_bf,
                   jax.ShapeDtypeStruct((s_len, WIDTH), f32), jax.ShapeDtypeStruct((1, LANES), f32)],
        compiler_params=_params("arbitrary"),
    )(dmixed, fox_o, gdn_o, p_main, p_main, gnorm_g)


def _out_head(mixed, w_out, x, target, gate, final_g):
    s_len = x.shape[0]
    tm = 256

    def body(mx_ref, w_ref, x_ref, t_ref, gate_ref, fg_ref, loss_ref, dy_ref, dz_ref, dm_ref, dfg_ref, dgate_ref):
        @pl.when(pl.program_id(0) == 0)
        def _():
            loss_ref[...] = jnp.zeros_like(loss_ref)
            dfg_ref[...] = jnp.zeros_like(dfg_ref)
            dgate_ref[...] = jnp.zeros_like(dgate_ref)

        w = w_ref[...]
        z = jnp.dot(mx_ref[...], w, preferred_element_type=f32)
        gate_v, fg = gate_ref[...], fg_ref[...]
        y1 = x_ref[...] + gate_v * z
        r = lax.rsqrt(jnp.mean(y1 * y1, axis=-1, keepdims=True) + EPS)
        yn = y1 * r
        err = yn * fg - t_ref[...]
        loss_ref[...] += 0.5 * jnp.sum(jnp.mean(err * err, axis=-1, keepdims=True))
        dout = err * (1.0 / D_MODEL)
        dfg_ref[...] += jnp.sum(dout * yn, axis=0, keepdims=True)
        tt = dout * fg
        dy1 = r * (tt - yn * jnp.mean(tt * yn, axis=-1, keepdims=True))
        dy_ref[...] = dy1
        dgate_ref[...] += jnp.sum(dy1 * z, axis=0, keepdims=True)
        dz = (dy1 * gate_v).astype(bf16)
        dz_ref[...] = dz
        dm_ref[...] = _dg(dz, w, 1, 1)

    row = pl.BlockSpec((tm, D_MODEL), lambda i: (i, 0))
    vec = pl.BlockSpec((1, D_MODEL), lambda i: (0, 0))
    big = jax.ShapeDtypeStruct((s_len, D_MODEL), f32)
    return pl.pallas_call(
        body, name="out_head", grid=(s_len // tm,),
        in_specs=[row, pl.BlockSpec((D_MODEL, D_MODEL), lambda i: (0, 0)), row, row, vec, vec],
        out_specs=[pl.BlockSpec((1, LANES), lambda i: (0, 0)), row, row, row, vec, vec],
        out_shape=[jax.ShapeDtypeStruct((1, LANES), f32), big, jax.ShapeDtypeStruct((s_len, D_MODEL), bf16), big,
                   jax.ShapeDtypeStruct((1, D_MODEL), f32), jax.ShapeDtypeStruct((1, D_MODEL), f32)],
        compiler_params=_params("arbitrary"),
    )(mixed, w_out, x, target, gate, final_g)


def _matmul_tn(name, a, b, out_dtype):
    k_len, m_len = a.shape
    n_len = b.shape[1]
    tk, tm, tn = min(2048, k_len), min(1024, m_len), min(2048, n_len)
    nk = k_len // tk

    def body(a_ref, b_ref, o_ref, acc_sc):
        k = pl.program_id(2)

        @pl.when(k == 0)
        def _():
            acc_sc[...] = jnp.zeros_like(acc_sc)

        acc_sc[...] += _dg(a_ref[...], b_ref[...], 0, 0)

        @pl.when(k == nk - 1)
        def _():
            o_ref[...] = acc_sc[...].astype(out_dtype)

    return pl.pallas_call(
        body, name=name, grid=(m_len // tm, n_len // tn, nk),
        in_specs=[pl.BlockSpec((tk, tm), lambda i, j, k: (k, i)), pl.BlockSpec((tk, tn), lambda i, j, k: (k, j))],
        out_specs=pl.BlockSpec((tm, tn), lambda i, j, k: (i, j)),
        out_shape=jax.ShapeDtypeStruct((m_len, n_len), out_dtype),
        scratch_shapes=[pltpu.VMEM((tm, tn), f32)],
        compiler_params=_params("parallel", "parallel", "arbitrary"),
    )(a, b)


def _post1(p_main, p_small, qn_g, kn_g, conv_w, bvec, alog, dqs, dkn, dgq, dgk, dgv, d_small, df, df_query):
    s_len = p_main.shape[0]
    tm = PREP_TM
    nb = s_len // tm

    def body(fq_ref, fk_ref, gq_ref, gk_ref, gv_ref, hq_ref, hk_ref, hv_ref, ps_ref, qg_ref, kg_ref, cw_ref, bv_ref,
             al_ref, dqs_ref, dkn_ref, dgq_ref, dgk_ref, dgv_ref, dsm_ref, df_ref, dfq_in_ref,
             dfq_ref, dfk_ref, dx_ref, dps_ref, dqg_ref, dkg_ref, sums_ref, dw_ref, xe_sc, carry_sc, dc_sc, next_sc):
        step = pl.program_id(0)
        blk = nb - 1 - step

        @pl.when(step == 0)
        def _():
            carry_sc[...] = jnp.zeros_like(carry_sc)
            next_sc[...] = jnp.zeros_like(next_sc)
            dqg_ref[...] = jnp.zeros_like(dqg_ref)
            dkg_ref[...] = jnp.zeros_like(dkg_ref)
            sums_ref[...] = jnp.zeros_like(sums_ref)
            dw_ref[...] = jnp.zeros_like(dw_ref)

        for x_ref, g_ref, dy_ref, o_ref, acc_ref, mul in ((fq_ref, qg_ref, dqs_ref, dfq_ref, dqg_ref, QK_SCALE),
                                                          (fk_ref, kg_ref, dkn_ref, dfk_ref, dkg_ref, LN2)):
            gain = g_ref[...]
            acc = jnp.zeros((1, HEAD_DIM), f32)
            for h in range(HEADS):
                sl = slice(h * HEAD_DIM, (h + 1) * HEAD_DIM)
                xv = x_ref[:, sl]
                r = lax.rsqrt(jnp.mean(xv * xv, axis=-1, keepdims=True) + EPS)
                xn = xv * r
                dy = dy_ref[:, sl] * mul
                acc = acc + jnp.sum(dy * xn, axis=0, keepdims=True)
                tt = dy * gain
                o_ref[:, sl] = (r * (tt - xn * jnp.mean(tt * xn, axis=-1, keepdims=True))).astype(bf16)
            acc_ref[...] += acc

        first = blk == 0
        for sec, (x_ref, halo_ref, dy_ref) in enumerate(((gq_ref, hq_ref, dgq_ref), (gk_ref, hk_ref, dgk_ref),
                                                         (gv_ref, hv_ref, dgv_ref))):
            cols = slice(sec * WIDTH, (sec + 1) * WIDTH)
            xe_sc[0:HALO, :] = jnp.where(first, 0.0, halo_ref[...])
            xe_sc[HALO:, :] = x_ref[...]
            cv = _conv_section(xe_sc, cw_ref, cols, tm)
            y, sgrad = _silu_and_grad(cv)
            if sec == 2:
                dc_sc[0:tm, :] = dy_ref[...] * sgrad
            else:
                mul = QK_SCALE if sec == 0 else 1.0
                for h in range(HEADS):
                    sl = slice(h * HEAD_DIM, (h + 1) * HEAD_DIM)
                    yh = y[:, sl]
                    r = lax.rsqrt(jnp.sum(yh * yh, axis=-1, keepdims=True) + EPS)
                    dqh = dy_ref[:, sl]
                    dyh = (mul * r) * (dqh - yh * (r * r) * jnp.sum(dqh * yh, axis=-1, keepdims=True))
                    dc_sc[0:tm, sl] = dyh * sgrad[:, sl]
            dc_sc[tm:, :] = next_sc[sec]
            x_rows = xe_sc[pl.ds(HALO, tm), :]
            dx = jnp.zeros((tm, WIDTH), f32)
            dw = jnp.zeros((8, WIDTH), f32)
            tap_row = _iota((8, WIDTH), 0)
            for tap in range(CONV_K):
                ahead = dc_sc[pl.ds(CONV_K - 1 - tap, tm), :]
                dx = dx + cw_ref[pl.ds(tap, 1), cols] * ahead
                dw = jnp.where(tap_row == tap, jnp.sum(x_rows * ahead, axis=0, keepdims=True), dw)
            dx_ref[:, cols] = dx.astype(bf16)
            dw_ref[:, cols] += dw
            next_sc[sec] = dc_sc[0:HALO, :]

        lane = _iota((tm, N_SMALL), 1)
        z, _, gval, beta = _small_fwd(ps_ref[...], bv_ref[...], al_ref[...])
        sig_z = _sigmoid(z)
        dsm = dsm_ref[...]
        in_g = (lane >= LANE_G) & (lane < LANE_G + HEADS)
        dgc = jnp.where(in_g, pltpu.roll(dsm, N_SMALL - (LANE_GC - LANE_G), 1), 0.0)
        dgl = jnp.where(in_g, pltpu.roll(dsm, N_SMALL - (LANE_GLAST - LANE_G), 1), 0.0)
        tri_c, ones_c = _chunk_masks(tm)
        dg = (_dg(tri_c, dgc, 0, 0, HI) + jnp.dot(ones_c, dgl, preferred_element_type=f32, precision=HI))
        dbeta = dsm
        dfb = jnp.where(lane < HEADS, df_ref[...], 0.0)
        for h in range(HEADS):
            dfb = dfb + jnp.where(lane == h, dfq_in_ref[h], 0.0)
        tri_u = (_iota((tm, tm), 1) >= _iota((tm, tm), 0)).astype(f32)
        dlogf = jnp.dot(tri_u, dfb, preferred_element_type=f32, precision=HI) + carry_sc[...]
        carry_sc[...] += jnp.sum(dfb, axis=0, keepdims=True)
        dff = dlogf * (1.0 - sig_z)
        dga = dg * (-jnp.exp(al_ref[...])) * sig_z
        dgb_small = dbeta * beta * (1.0 - beta)
        dps = jnp.where(lane < HEADS, dff, jnp.where(lane < 2 * HEADS, dga, jnp.where(lane < 3 * HEADS, dgb_small, 0.0)))
        dps_ref[...] = dps.astype(bf16)
        row = _iota((8, N_SMALL), 0)
        s0 = jnp.sum(dps, axis=0, keepdims=True)
        s1 = jnp.sum(jnp.where((lane >= HEADS) & (lane < 2 * HEADS), dg * gval, 0.0), axis=0, keepdims=True)
        sums_ref[...] += jnp.where(row == 0, s0, jnp.where(row == 1, s1, 0.0))

    def col(cb):
        return pl.BlockSpec((tm, WIDTH), lambda i: (nb - 1 - i, cb))

    def halo(cb):
        return pl.BlockSpec((HALO, WIDTH), lambda i: (jnp.maximum((nb - 1 - i) * (tm // HALO) - 1, 0), cb))

    vec = pl.BlockSpec((1, LANES), lambda i: (0, 0))
    row0 = pl.BlockSpec((tm, WIDTH), lambda i: (nb - 1 - i, 0))
    small = pl.BlockSpec((tm, N_SMALL), lambda i: (nb - 1 - i, 0))
    wide_bf = jax.ShapeDtypeStruct((s_len, WIDTH), bf16)
    return pl.pallas_call(
        body, name="post1", grid=(nb,),
        in_specs=[col(0), col(1), col(4), col(5), col(6), halo(4), halo(5), halo(6), small, vec, vec,
                  pl.BlockSpec((CONV_K, 3 * WIDTH), lambda i: (0, 0)), vec, vec,
                  row0, row0, row0, row0, row0, small, small,
                  pl.BlockSpec((HEADS, tm, 1), lambda i: (0, nb - 1 - i, 0))],
        out_specs=[row0, row0, pl.BlockSpec((tm, 3 * WIDTH), lambda i: (nb - 1 - i, 0)), small, vec, vec,
                   pl.BlockSpec((8, N_SMALL), lambda i: (0, 0)), pl.BlockSpec((8, 3 * WIDTH), lambda i: (0, 0))],
        out_shape=[wide_bf, wide_bf, jax.ShapeDtypeStruct((s_len, 3 * WIDTH), bf16),
                   jax.ShapeDtypeStruct((s_len, N_SMALL), bf16), jax.ShapeDtypeStruct((1, LANES), f32),
                   jax.ShapeDtypeStruct((1, LANES), f32), jax.ShapeDtypeStruct((8, N_SMALL), f32),
                   jax.ShapeDtypeStruct((8, 3 * WIDTH), f32)],
        scratch_shapes=[pltpu.VMEM((tm + HALO, WIDTH), f32), pltpu.VMEM((1, N_SMALL), f32),
                        pltpu.VMEM((tm + HALO, WIDTH), f32), pltpu.VMEM((3, HALO, WIDTH), f32)],
        compiler_params=_params("arbitrary"),
    )(p_main, p_main, p_main, p_main, p_main, p_main, p_main, p_main, p_small, qn_g, kn_g, conv_w, bvec, alog,
      dqs, dkn, dgq, dgk, dgv, d_small, df, df_query)


def _in_proj_bwd(dp_pieces, dp_small, wt_main, wt_small):
    s_len = dp_small.shape[0]
    tm, tk = min(1024, s_len), WIDTH
    nk = N_MAIN // tk
    first_section = [sum(p.shape[1] // tk for p in dp_pieces[:n]) for n in range(len(dp_pieces))]
    n_pieces = len(dp_pieces)

    def body(*refs):
        piece_refs = refs[:n_pieces]
        dps_ref, w_ref, ws_ref, dh_ref = refs[n_pieces:]
        k = pl.program_id(1)

        @pl.when(k == 0)
        def _():
            dh_ref[...] = jnp.dot(dps_ref[...], ws_ref[...], preferred_element_type=f32)

        for piece, ref, first in zip(dp_pieces, piece_refs, first_section):
            @pl.when((k >= first) & (k < first + piece.shape[1] // tk))
            def _(ref=ref):
                dh_ref[...] += jnp.dot(ref[...], w_ref[...], preferred_element_type=f32)

    def piece_spec(piece, first):
        last = piece.shape[1] // tk - 1
        return pl.BlockSpec((tm, tk), lambda i, k: (i, jnp.clip(k - first, 0, last)))

    return pl.pallas_call(
        body, name="in_proj_bwd", grid=(s_len // tm, nk),
        in_specs=[piece_spec(p, f) for p, f in zip(dp_pieces, first_section)]
                 + [pl.BlockSpec((tm, N_SMALL), lambda i, k: (i, 0)),
                    pl.BlockSpec((tk, D_MODEL), lambda i, k: (k, 0)), pl.BlockSpec((N_SMALL, D_MODEL), lambda i, k: (0, 0))],
        out_specs=pl.BlockSpec((tm, D_MODEL), lambda i, k: (i, 0)),
        out_shape=jax.ShapeDtypeStruct((s_len, D_MODEL), f32),
        compiler_params=_params("parallel", "arbitrary"),
    )(*dp_pieces, dp_small, wt_main, wt_small)


def _adaln_bwd(dh, x, dy1, norm_g, scale1p):
    s_len = x.shape[0]
    tm = 512

    def body(dh_ref, x_ref, dy_ref, g_ref, sc_ref, dx_ref, dsh_ref, dsc_ref, dg_ref):
        @pl.when(pl.program_id(0) == 0)
        def _():
            dsh_ref[...] = jnp.zeros_like(dsh_ref)
            dsc_ref[...] = jnp.zeros_like(dsc_ref)
            dg_ref[...] = jnp.zeros_like(dg_ref)

        dh = dh_ref[...]
        xb = x_ref[...]
        r = lax.rsqrt(jnp.mean(xb * xb, axis=-1, keepdims=True) + EPS)
        xr = xb * r
        gain = g_ref[...]
        dsh_ref[...] += jnp.sum(dh, axis=0, keepdims=True)
        dsc_ref[...] += jnp.sum(dh * (xr * gain), axis=0, keepdims=True)
        dxn = dh * sc_ref[...]
        dg_ref[...] += jnp.sum(dxn * xr, axis=0, keepdims=True)
        tt = dxn * gain
        dx_ref[...] = r * (tt - xr * jnp.mean(tt * xr, axis=-1, keepdims=True)) + dy_ref[...]

    row = pl.BlockSpec((tm, D_MODEL), lambda i: (i, 0))
    vec = pl.BlockSpec((1, D_MODEL), lambda i: (0, 0))
    vshape = jax.ShapeDtypeStruct((1, D_MODEL), f32)
    return pl.pallas_call(
        body, name="adaln_bwd", grid=(s_len // tm,),
        in_specs=[row, row, row, vec, vec], out_specs=[row, vec, vec, vec],
        out_shape=[jax.ShapeDtypeStruct((s_len, D_MODEL), f32), vshape, vshape, vshape],
        compiler_params=_params("arbitrary"),
    )(dh, x, dy1, norm_g, scale1p)


def _adamw(name, w, g_stack, m, v, tr, tc=None):
    n_stack, rows, cols = g_stack.shape
    tc = cols if tc is None else tc

    def body(w_ref, g_ref, m_ref, v_ref, go_ref, d_ref, mo_ref, vo_ref):
        g = g_ref[0].astype(f32)
        for k in range(1, n_stack):
            g = g + g_ref[k].astype(f32)
        go_ref[0] = g
        m_new = ADAM_B1 * m_ref[0] + (1.0 - ADAM_B1) * g
        v_new = ADAM_B2 * v_ref[0] + (1.0 - ADAM_B2) * (g * g)
        mo_ref[0] = m_new
        vo_ref[0] = v_new
        m_hat = m_new / (1.0 - ADAM_B1 ** ADAM_STEP)
        v_hat = v_new / (1.0 - ADAM_B2 ** ADAM_STEP)
        d_ref[0] = -ADAM_LR * (m_hat / (jnp.sqrt(v_hat) + ADAM_EPS) + ADAM_WD * w_ref[0])

    blk = pl.BlockSpec((1, tr, tc), lambda i, j: (0, i, j))
    shape = jax.ShapeDtypeStruct((1, rows, cols), f32)
    return pl.pallas_call(
        body, name=name, grid=(rows // tr, cols // tc),
        in_specs=[blk, pl.BlockSpec((n_stack, tr, tc), lambda i, j: (0, i, j)), blk, blk],
        out_specs=[blk] * 4, out_shape=[shape] * 4,
        compiler_params=_params("parallel", "parallel"),
    )(w, g_stack, m, v)


def _w_ada_grad(c_all_t, dmod_pad):
    def body(c_ref, d_ref, o_ref):
        cv = c_ref[...]
        o_ref[...] = jnp.dot(cv * _sigmoid(cv), d_ref[...], preferred_element_type=f32, precision=HI)

    return pl.pallas_call(body, name="w_ada_grad",
                          out_shape=jax.ShapeDtypeStruct((c_all_t.shape[0], dmod_pad.shape[1]), f32),
                          compiler_params=_params())(c_all_t, dmod_pad)


SMALL_NAMES = ("norm_g", "b_ada", "b_fgate", "fox_qn_g", "fox_kn_g", "gdn_A_log", "gdn_dt_bias", "gdn_norm_g", "final_g")
SMALL_SIZES = (D_MODEL, 3 * D_MODEL, HEADS, HEAD_DIM, HEAD_DIM, HEADS, HEADS, HEAD_DIM, D_MODEL)
SMALL_PACK = 10752


def _pack(vectors, total):
    flat = jnp.concatenate([t.reshape(-1) for t in vectors])
    return jnp.pad(flat, (0, total - flat.shape[0])).reshape(1, total)


def _lanes(*pieces):
    parts, at = [], 0
    for off, vec in pieces:
        flat = vec.reshape(-1).astype(f32)
        parts += [jnp.zeros((off - at,), f32), flat]
        at = off + flat.shape[0]
    parts.append(jnp.zeros((LANES - at,), f32))
    return jnp.concatenate(parts).reshape(1, LANES)


def kernel(x, c, norm_g, w_ada, b_ada, w_in, b_fgate, fox_qn_g, fox_kn_g, gdn_conv_w, gdn_A_log, gdn_dt_bias, gdn_norm_g, w_out, final_g, loss_target, m_norm_g, m_w_ada, m_b_ada, m_w_in, m_b_fgate, m_fox_qn_g, m_fox_kn_g, m_gdn_conv_w, m_gdn_A_log, m_gdn_dt_bias, m_gdn_norm_g, m_w_out, m_final_g, v_norm_g, v_w_ada, v_b_ada, v_w_in, v_b_fgate, v_fox_qn_g, v_fox_kn_g, v_gdn_conv_w, v_gdn_A_log, v_gdn_dt_bias, v_gdn_norm_g, v_w_out, v_final_g):
    me = _my_index()
    s_len = x.shape[1]
    nq = s_len // FOX_T
    x2 = x.reshape(s_len, D_MODEL)
    tgt = loss_target.reshape(s_len, D_MODEL)
    ada_cols = w_ada.shape[2]
    in_cols = w_in.shape[2]
    conv_cols = gdn_conv_w.shape[2]

    (c_all,) = _gather_direct("gather_c", [c])
    c_all = c_all.reshape(N_DEV, D_MODEL)
    b_shard = lax.dynamic_slice(b_ada, (0, me * ada_cols), (1, ada_cols))
    mod_mine = _mod_shard(c_all, w_ada[0], b_shard)
    wt_shard = jnp.transpose(w_in[0])
    mod_all, wt_all, conv_all = _gather_two_level(
        "gather_weights", [mod_mine, wt_shard.astype(bf16), gdn_conv_w[0]])
    w_out_mine = w_out[0].astype(bf16)
    w_out_started = _gather_behind_start("gather_w_out_start", w_out_mine)
    mod = lax.dynamic_slice(mod_all, (0, me, 0), (N_DEV, 1, ada_cols)).reshape(1, 3 * D_MODEL)
    shift, scale, gate = mod[:, :D_MODEL], mod[:, D_MODEL:2 * D_MODEL], mod[:, 2 * D_MODEL:]
    scale1p = 1.0 + scale + w_out_started[-1][0, 0]
    wt_full = wt_all.reshape(N_DEV * in_cols, D_MODEL)
    g0 = 4 * WIDTH + HEADS
    w_main = jnp.concatenate([wt_full[:4 * WIDTH], wt_full[g0:g0 + 4 * WIDTH]], axis=0)
    w_small = jnp.concatenate([wt_full[4 * WIDTH:g0], wt_full[g0 + 4 * WIDTH:],
                               jnp.zeros((N_SMALL - 3 * HEADS, D_MODEL), bf16)], axis=0)
    conv_full = jnp.transpose(conv_all, (1, 0, 2)).reshape(CONV_K, 3 * WIDTH)

    qn_g, kn_g, gn_g = fox_qn_g.reshape(1, LANES), fox_kn_g.reshape(1, LANES), gdn_norm_g.reshape(1, LANES)
    bvec = _lanes((0, b_fgate), (HEADS, gdn_dt_bias))
    alog = _lanes((HEADS, gdn_A_log))
    fg = final_g.reshape(1, D_MODEL)

    h_bf = _norm_mod(x2, norm_g, scale1p, shift)
    p_main, p_small = _in_proj(h_bf, w_main, w_small)
    qs, kn, vb, gq, gk, gv, small = _prep(p_main, p_small, qn_g, kn_g, conv_full, bvec, alog)
    fox_o, lse = _fox_fwd(qs, kn, vb)
    gu, gw, gqg, gkd, gattn, t_inv, eg_last = _gdn_intra_fwd(gq, gk, gv, small)
    gdn_o, states = _gdn_scan_fwd(gu, gw, gqg, gkd, gattn, eg_last)
    mixed = _mix_fwd(fox_o, gdn_o, p_main, gn_g)
    w_out_all = _gather_behind_wait("gather_w_out_wait", w_out_started, mixed)
    w_out_full = lax.dynamic_update_slice(w_out_all, w_out_mine[None], (me, 0, 0)).reshape(2 * WIDTH, D_MODEL)

    loss_row, dy1, dz, dmixed, d_final_g, d_gate = _out_head(mixed, w_out_full, x2, tgt, gate, fg)
    loss = lax.psum(loss_row[0, 0], AXES)
    dw_out = _matmul_tn("dw_out", mixed, dz, bf16)
    do_fox, delta, dfz, dgz, dgdn_o, d_gn_g = _mix_bwd(dmixed, fox_o, gdn_o, p_main, gn_g)
    dqs, dkn, dvf, df_key, df_query = _fox_bwd(qs, kn, vb, do_fox, lse, delta)
    du, dw, dqg, dkd, dattn, deg = _gdn_scan_bwd(gu, gw, gqg, gkd, gattn, eg_last, states, dgdn_o)
    dgq, dgk, dgv, d_small = _gdn_intra_bwd(gq, gk, gv, small, t_inv, du, dw, dqg, dkd, dattn, deg)
    df_small = jnp.pad(jnp.transpose(df_key.reshape(HEADS, s_len)), ((0, 0), (0, N_SMALL - HEADS)))
    dfq, dfk, dgqkv, dp_small, d_qn_g, d_kn_g, sums, d_conv = _post1(
        p_main, p_small, qn_g, kn_g, conv_full, bvec, alog, dqs, dkn, dgq, dgk, dgv, d_small, df_small, df_query)
    dp_pieces = [dfq, dfk, dvf, dfz, dgqkv, dgz]
    dh = _in_proj_bwd(dp_pieces, dp_small, w_main, w_small)
    grad_x, d_shift, d_scale, d_norm_g = _adaln_bwd(dh, x2, dy1, norm_g, scale1p)
    dw_rows = [_matmul_tn("dw_main_%d" % n, piece, h_bf, bf16) for n, piece in enumerate(dp_pieces)]
    dw_small = _matmul_tn("dw_small", dp_small, h_bf, bf16)
    dw_in_full = jnp.concatenate(dw_rows[:4] + [dw_small[:HEADS]] + dw_rows[4:] + [dw_small[HEADS:3 * HEADS]],
                                 axis=0)
    dw_in_parts = dw_in_full.reshape(N_DEV, in_cols, D_MODEL)
    dw_out_parts = dw_out.reshape(N_DEV, w_out.shape[1], D_MODEL)

    dmod = jnp.concatenate([d_shift, d_scale, d_gate], axis=1)
    small_grads = _pack([d_norm_g, dmod, sums[0, :HEADS], d_qn_g, d_kn_g, sums[1, HEADS:2 * HEADS],
                         sums[0, HEADS:2 * HEADS], d_gn_g, d_final_g], SMALL_PACK)
    conv_grad = d_conv[:CONV_K]
    pair_in, pair_out = _pair_exchange("pair_grads", [dw_in_parts, dw_out_parts])
    core = lax.axis_index("c").astype(jnp.int32).reshape(1)
    dw_in_recv, dw_out_recv = _chip_exchange(
        "chip_grads", [_pair_sum("pair_sum_w_in", dw_in_parts, pair_in, core),
                       _pair_sum("pair_sum_w_out", dw_out_parts, pair_out, core)])
    small_all, conv_all_g = _gather_direct("gather_small_grads", [small_grads, conv_grad])

    outs = {}
    to_t = lambda t: jnp.transpose(t, (0, 2, 1))
    outs["w_in"] = tuple(to_t(t) for t in _adamw("adamw_w_in", to_t(w_in), dw_in_recv, to_t(m_w_in), to_t(v_w_in),
                                                  in_cols, 256))
    outs["w_out"] = _adamw("adamw_w_out", w_out, dw_out_recv, m_w_out, v_w_out, 128)
    conv_mine = lax.dynamic_slice(jnp.transpose(conv_all_g.reshape(N_DEV, CONV_K, N_DEV, conv_cols), (0, 2, 1, 3)),
                                  (0, me, 0, 0), (N_DEV, 1, CONV_K, conv_cols)).reshape(N_DEV, CONV_K, conv_cols)
    outs["gdn_conv_w"] = _adamw("adamw_conv", gdn_conv_w, conv_mine, m_gdn_conv_w, v_gdn_conv_w, CONV_K)
    small_all = small_all.reshape(N_DEV, 1, SMALL_PACK)
    dmod_all = small_all[:, 0, D_MODEL:D_MODEL + 3 * D_MODEL]
    dmod_mine = lax.dynamic_slice(dmod_all, (0, me * ada_cols), (N_DEV, ada_cols))
    c_all_t = jnp.pad(jnp.transpose(c_all), ((0, 0), (0, LANES - N_DEV)))
    g_w_ada = _w_ada_grad(c_all_t, jnp.pad(dmod_mine, ((0, LANES - N_DEV), (0, 0))))
    outs["w_ada"] = _adamw("adamw_w_ada", w_ada, g_w_ada[None], m_w_ada, v_w_ada, 256)
    given = dict(norm_g=(norm_g, m_norm_g, v_norm_g), b_ada=(b_ada, m_b_ada, v_b_ada), b_fgate=(b_fgate, m_b_fgate, v_b_fgate),
                 fox_qn_g=(fox_qn_g, m_fox_qn_g, v_fox_qn_g), fox_kn_g=(fox_kn_g, m_fox_kn_g, v_fox_kn_g),
                 gdn_A_log=(gdn_A_log, m_gdn_A_log, v_gdn_A_log), gdn_dt_bias=(gdn_dt_bias, m_gdn_dt_bias, v_gdn_dt_bias),
                 gdn_norm_g=(gdn_norm_g, m_gdn_norm_g, v_gdn_norm_g), final_g=(final_g, m_final_g, v_final_g))
    w_pack = _pack([given[n][0] for n in SMALL_NAMES], SMALL_PACK)
    m_pack = _pack([given[n][1] for n in SMALL_NAMES], SMALL_PACK)
    v_pack = _pack([given[n][2] for n in SMALL_NAMES], SMALL_PACK)
    packed = _adamw("adamw_small", w_pack[None], small_all, m_pack[None], v_pack[None], 1)
    off = 0
    for n, size in zip(SMALL_NAMES, SMALL_SIZES):
        outs[n] = tuple(t[0, 0, off:off + size].reshape(given[n][0].shape) for t in packed)
        off += size

    order = ("norm_g", "w_ada", "b_ada", "w_in", "b_fgate", "fox_qn_g", "fox_kn_g", "gdn_conv_w", "gdn_A_log",
             "gdn_dt_bias", "gdn_norm_g", "w_out", "final_g")
    result = [loss, grad_x.reshape(x.shape)]
    for part in range(4):
        result += [outs[n][part] for n in order]
    return tuple(result)
```

```python
import math

import jax
import jax.numpy as jnp
from jax import lax
from jax.experimental import pallas as pl
from jax.experimental.pallas import tpu as pltpu

f32 = jnp.float32
bf16 = jnp.bfloat16
HI = lax.Precision.HIGHEST

N_DEV = 8
AXES = ("x", "y", "c")
D_MODEL = 2048
HEADS = 8
HEAD_DIM = 128
WIDTH = HEADS * HEAD_DIM
CHUNK = 64
CONV_K = 4
EPS = 1e-6
QK_SCALE = HEAD_DIM ** -0.5
LOG2E = 1.0 / math.log(2.0)
LN2 = math.log(2.0)
N_MAIN = 8 * WIDTH
N_SMALL = 128
LANE_F, LANE_G, LANE_BETA, LANE_GC, LANE_GLAST = 0, 8, 16, 24, 32
IN_WIDTH = 8 * WIDTH + 3 * HEADS
LANES = 128
VMEM_LIMIT = 56 * 1024 * 1024

ADAM_LR, ADAM_B1, ADAM_B2, ADAM_EPS, ADAM_WD, ADAM_STEP = 0.001, 0.9, 0.999, 1e-08, 0.01, 10


def _params(*sem):
    return pltpu.CompilerParams(dimension_semantics=sem, vmem_limit_bytes=VMEM_LIMIT)


def _iota(shape, dim):
    return lax.broadcasted_iota(jnp.int32, shape, dim)


def _sigmoid(z):
    return 1.0 / (1.0 + jnp.exp(-z))


def _softplus_parts(z):
    t = jnp.log(1.0 + jnp.exp(-jnp.abs(z)))
    return jnp.minimum(z, 0.0) - t, jnp.maximum(z, 0.0) + t


def _dg(a, b, ca, cb, prec=None):
    if a.ndim == 3:
        dims = (((ca + 1,), (cb + 1,)), ((0,), (0,)))
    else:
        dims = (((ca,), (cb,)), ((), ()))
    return lax.dot_general(a, b, dims, preferred_element_type=f32, precision=prec)


def _dot_bf16(a, b, ca, cb):
    return _dg(a.astype(bf16), b.astype(bf16), ca, cb)


def _split_bf16(a):
    hi = a.astype(bf16)
    return hi, (a - hi.astype(f32)).astype(bf16)


def _dot_3pass(a, b, ca, cb):
    a_hi, a_lo = _split_bf16(a)
    b_hi, b_lo = _split_bf16(b)
    return _dg(a_hi, b_hi, ca, cb) + (_dg(a_hi, b_lo, ca, cb) + _dg(a_lo, b_hi, ca, cb))


def _make_mm(dot):
    def nn_(a, b):
        return dot(a, b, 1, 0)

    def nt_(a, b):
        return dot(a, b, 1, 1)

    def tn_(a, b):
        return dot(a, b, 0, 0)

    @jax.custom_vjp
    def nn(a, b):
        return nn_(a, b)

    @jax.custom_vjp
    def nt(a, b):
        return nt_(a, b)

    @jax.custom_vjp
    def tn(a, b):
        return tn_(a, b)

    nn.defvjp(lambda a, b: (nn_(a, b), (a, b)), lambda r, g: (nt_(g, r[1]), tn_(r[0], g)))
    nt.defvjp(lambda a, b: (nt_(a, b), (a, b)), lambda r, g: (nn_(g, r[1]), tn_(g, r[0])))
    tn.defvjp(lambda a, b: (tn_(a, b), (a, b)), lambda r, g: (nt_(r[1], g), nn_(r[0], g)))
    return (nn_, nt_, tn_), (nn, nt, tn)


_BF_PLAIN, _BF_VJP = _make_mm(_dot_bf16)
_X3_PLAIN, _X3_VJP = _make_mm(_dot_3pass)


def _inv_unit_lower_many(ms):
    c = CHUNK
    nn = _X3_PLAIN[0]
    eye = (_iota((c, c), 0) == _iota((c, c), 1)).astype(f32)
    top = _iota((2 * c, c), 0) < c
    xs = [jnp.concatenate([eye - m, nn(m, m)], axis=0) for m in ms]
    for _ in range(int(math.log2(CHUNK)) - 2):
        xs = [jnp.where(top, x, 0.0) + nn(x, x[c:]) for x in xs]
    return [x[:c] + nn(x[:c], x[c:]) for x in xs]


@jax.custom_vjp
def _inv_given(m, t):
    return t


_inv_given.defvjp(lambda m, t: (t, t),
                  lambda t, g: (-_X3_PLAIN[1](_X3_PLAIN[2](t, g), t), jnp.zeros_like(t)))

SUBLANES = 8


def _gdn_intra_pre(q, k, v, gc_b, g_last_b, beta_b, bnt):
    c = CHUNK
    r_i, c_i = _iota((c, c), 0), _iota((c, c), 1)
    lower, strict = r_i >= c_i, r_i > c_i
    gc_i = gc_b[..., :c]
    gc_j = jnp.swapaxes(gc_i, -1, -2)
    decay = jnp.where(lower, jnp.exp(jnp.where(lower, gc_i - gc_j, 0.0)), 0.0)
    kb = k * beta_b
    both = bnt(jnp.concatenate([kb, q], axis=-2), k)
    m = jnp.where(strict, both[..., :c, :] * decay, 0.0)
    attn = jnp.where(lower, both[..., c:, :] * decay, 0.0)
    eg = jnp.exp(gc_b)
    rhs = jnp.concatenate([v * beta_b, kb * eg], axis=-1)
    k_dec = k * jnp.exp(g_last_b - gc_b)
    eg_last = jnp.exp(g_last_b[..., :SUBLANES, :])
    return m, rhs, q * eg, k_dec, attn, eg_last


def _gdn_intra(q, k, v, gc_b, g_last_b, beta_b, t_known):
    m, rhs, qg, k_dec, attn, eg_last = _gdn_intra_pre(q, k, v, gc_b, g_last_b, beta_b, _BF_VJP[1])
    return _X3_VJP[0](_inv_given(m, t_known), rhs), qg, k_dec, attn, eg_last


def _scale_rows(s, eg_last):
    return (s.reshape(HEAD_DIM // SUBLANES, SUBLANES, HEAD_DIM) * eg_last[None]).reshape(HEAD_DIM, HEAD_DIM)


def _my_index():
    return 4 * lax.axis_index("x") + 2 * lax.axis_index("y") + lax.axis_index("c")


def _peer(d):
    x, y, c = lax.axis_index("x"), lax.axis_index("y"), lax.axis_index("c")
    px, py, pc = (x + (d >> 2)) % 2, (y + ((d >> 1) & 1)) % 2, (c + (d & 1)) % 2
    return (px, py, pc), 4 * px + 2 * py + pc


def _gather_direct(name, arrays):
    n = len(arrays)

    def body(*refs):
        srcs, dsts = refs[:n], refs[n:2 * n]
        send_sems, recv_sems, local_sems = refs[2 * n:]
        me = _my_index()

        def copy(k, d, started):
            peer, pidx = _peer(d)
            return pltpu.make_async_remote_copy(
                src_ref=srcs[k], dst_ref=dsts[k].at[me if started else pidx], send_sem=send_sems.at[k * 7 + d - 1],
                recv_sem=recv_sems.at[k * 7 + d - 1], device_id=peer, device_id_type=pl.DeviceIdType.MESH)

        local = [pltpu.make_async_copy(srcs[k], dsts[k].at[me], local_sems.at[k]) for k in range(n)]
        sends = [copy(k, d, True) for k in range(n) for d in range(1, N_DEV)]
        for cp in local + sends:
            cp.start()
        for k in range(n):
            for d in range(1, N_DEV):
                copy(k, d, False).wait_recv()
        for cp in sends:
            cp.wait_send()
        for cp in local:
            cp.wait()

    out_shape = [jax.ShapeDtypeStruct((N_DEV,) + a.shape, a.dtype) for a in arrays]
    any_spec = pl.BlockSpec(memory_space=pl.ANY)
    return pl.pallas_call(
        body, name=name, out_shape=out_shape, in_specs=[any_spec] * n, out_specs=[any_spec] * n,
        scratch_shapes=[pltpu.SemaphoreType.DMA((7 * n,)), pltpu.SemaphoreType.DMA((7 * n,)),
                        pltpu.SemaphoreType.DMA((n,))],
        compiler_params=pltpu.CompilerParams(has_side_effects=True),
    )(*arrays)


N_CHIPS = 4


def _pair_exchange(name, arrays):
    n = len(arrays)

    def body(*refs):
        srcs, dsts = refs[:n], refs[n:2 * n]
        send_sems, recv_sems = refs[2 * n:]
        x, y, c = lax.axis_index("x"), lax.axis_index("y"), lax.axis_index("c")
        sibling = (x, y, 1 - c)

        def copy(k, j):
            return pltpu.make_async_remote_copy(
                src_ref=srcs[k].at[2 * j + (1 - c)], dst_ref=dsts[k].at[j], send_sem=send_sems.at[k * N_CHIPS + j],
                recv_sem=recv_sems.at[k * N_CHIPS + j], device_id=sibling, device_id_type=pl.DeviceIdType.MESH)

        copies = [copy(k, j) for k in range(n) for j in range(N_CHIPS)]
        for cp in copies:
            cp.start()
        for cp in copies:
            cp.wait_recv()
        for cp in copies:
            cp.wait_send()

    any_spec = pl.BlockSpec(memory_space=pl.ANY)
    return pl.pallas_call(
        body, name=name, out_shape=[jax.ShapeDtypeStruct((N_CHIPS,) + a.shape[1:], a.dtype) for a in arrays],
        in_specs=[any_spec] * n, out_specs=[any_spec] * n,
        scratch_shapes=[pltpu.SemaphoreType.DMA((N_CHIPS * n,)), pltpu.SemaphoreType.DMA((N_CHIPS * n,))],
        compiler_params=pltpu.CompilerParams(has_side_effects=True),
    )(*arrays)


def _chip_exchange(name, arrays):
    n = len(arrays)

    def body(*refs):
        srcs, dsts = refs[:n], refs[n:2 * n]
        send_sems, recv_sems, local_sems = refs[2 * n:]
        x, y, c = lax.axis_index("x"), lax.axis_index("y"), lax.axis_index("c")
        my_chip = 2 * x + y

        def peer(d):
            px, py = (x + (d >> 1)) % 2, (y + (d & 1)) % 2
            return (px, py, c), 2 * px + py

        def remote(k, d, started):
            to, chip = peer(d)
            return pltpu.make_async_remote_copy(
                src_ref=srcs[k].at[chip], dst_ref=dsts[k].at[my_chip if started else chip],
                send_sem=send_sems.at[k * 3 + d - 1], recv_sem=recv_sems.at[k * 3 + d - 1],
                device_id=to, device_id_type=pl.DeviceIdType.MESH)

        local = [pltpu.make_async_copy(srcs[k].at[my_chip], dsts[k].at[my_chip], local_sems.at[k]) for k in range(n)]
        sends = [remote(k, d, True) for k in range(n) for d in range(1, N_CHIPS)]
        for cp in local + sends:
            cp.start()
        for k in range(n):
            for d in range(1, N_CHIPS):
                remote(k, d, False).wait_recv()
        for cp in sends:
            cp.wait_send()
        for cp in local:
            cp.wait()

    any_spec = pl.BlockSpec(memory_space=pl.ANY)
    return pl.pallas_call(
        body, name=name, out_shape=[jax.ShapeDtypeStruct(a.shape, a.dtype) for a in arrays],
        in_specs=[any_spec] * n, out_specs=[any_spec] * n,
        scratch_shapes=[pltpu.SemaphoreType.DMA((3 * n,)), pltpu.SemaphoreType.DMA((3 * n,)),
                        pltpu.SemaphoreType.DMA((n,))],
        compiler_params=pltpu.CompilerParams(has_side_effects=True),
    )(*arrays)


def _pair_sum(name, parts, received, core):
    n_blocks, rows, cols = received.shape
    tr = rows if rows % 256 else 256

    def body(core_ref, mine_ref, recv_ref, o_ref):
        o_ref[...] = (mine_ref[...].astype(f32) + recv_ref[...].astype(f32)).astype(bf16)

    return pl.pallas_call(
        body, name=name,
        grid_spec=pltpu.PrefetchScalarGridSpec(
            num_scalar_prefetch=1, grid=(n_blocks, rows // tr),
            in_specs=[pl.BlockSpec((1, tr, cols), lambda j, i, core_ref: (2 * j + core_ref[0], i, 0)),
                      pl.BlockSpec((1, tr, cols), lambda j, i, core_ref: (j, i, 0))],
            out_specs=pl.BlockSpec((1, tr, cols), lambda j, i, core_ref: (j, i, 0))),
        out_shape=jax.ShapeDtypeStruct((n_blocks, rows, cols), bf16),
        compiler_params=_params("parallel", "parallel"),
    )(core, parts, received)


_HBM = pl.BlockSpec(memory_space=pltpu.HBM)
_SEM = pl.BlockSpec(memory_space=pltpu.SEMAPHORE)
_DATAFLOW = pltpu.SideEffectType.DATAFLOW_SIDE_EFFECTING


def _gather_behind_start(name, block, scatter=False):
    def body(src_ref, land_ref, *outs):
        send_sems, recv_sems = outs[:N_DEV - 1], outs[N_DEV - 1:2 * (N_DEV - 1)]
        token = outs[-1]
        me = _my_index()
        for d in range(1, N_DEV):
            peer, pidx = _peer(d)
            pltpu.make_async_remote_copy(src_ref=src_ref.at[pidx] if scatter else src_ref, dst_ref=land_ref.at[me],
                                         send_sem=send_sems[d - 1], recv_sem=recv_sems[d - 1], device_id=peer,
                                         device_id_type=pl.DeviceIdType.MESH).start()
        token[...] = jnp.zeros_like(token)

    n_sem = 2 * (N_DEV - 1)
    land_shape = block.shape if scatter else (N_DEV,) + block.shape
    return pl.pallas_call(
        body, name=name,
        out_shape=tuple([pltpu.SemaphoreType.DMA(())] * n_sem
                        + [pltpu.HBM(block.shape, block.dtype), pltpu.HBM(land_shape, block.dtype),
                           jax.ShapeDtypeStruct((SUBLANES, LANES), f32)]),
        in_specs=(_HBM, _HBM), out_specs=tuple([_SEM] * n_sem + [_HBM, _HBM, pl.BlockSpec(memory_space=pltpu.VMEM)]),
        input_output_aliases={0: n_sem, 1: n_sem + 1},
        compiler_params=pltpu.CompilerParams(has_side_effects=_DATAFLOW),
    )(pltpu.with_memory_space_constraint(block, pltpu.HBM),
      pltpu.with_memory_space_constraint(lax.empty(land_shape, block.dtype), pltpu.HBM))


def _gather_behind_wait(name, started, after, scatter=False):
    n_sem = 2 * (N_DEV - 1)
    sems, block_thru, land_thru = started[:n_sem], started[n_sem], started[n_sem + 1]

    def body(src_ref, land_ref, *rest):
        send_sems, recv_sems = rest[:N_DEV - 1], rest[N_DEV - 1:n_sem]
        for d in range(1, N_DEV):
            peer, pidx = _peer(d)
            copy = pltpu.make_async_remote_copy(src_ref=src_ref.at[pidx] if scatter else src_ref,
                                                dst_ref=land_ref.at[pidx], send_sem=send_sems[d - 1],
                                                recv_sem=recv_sems[d - 1], device_id=peer,
                                                device_id_type=pl.DeviceIdType.MESH)
            copy.wait_send()
            copy.wait_recv()

    return pl.pallas_call(
        body, name=name,
        out_shape=(pltpu.HBM(block_thru.shape, block_thru.dtype), pltpu.HBM(land_thru.shape, land_thru.dtype)),
        in_specs=tuple([_HBM, _HBM] + [_SEM] * n_sem + [pl.BlockSpec(memory_space=pl.ANY)]), out_specs=(_HBM, _HBM),
        input_output_aliases={0: 0, 1: 1},
        compiler_params=pltpu.CompilerParams(has_side_effects=_DATAFLOW),
    )(block_thru, land_thru, *sems, after)[1]


def _gather_two_level(name, arrays):
    n = len(arrays)

    def body(*refs):
        srcs, dsts = refs[:n], refs[n:2 * n]
        send_sems, recv_sems, local_sems = refs[2 * n:]
        x, y, c = lax.axis_index("x"), lax.axis_index("y"), lax.axis_index("c")
        sibling = (x, y, 1 - c)
        near = ((x + 1 - c) % 2, (y + c) % 2)
        far = ((x + c) % 2, (y + 1 - c) % 2)
        diag = ((x + 1) % 2, (y + 1) % 2)
        near_slot, far_slot = 1 + c, 2 - c

        def index(px, py, pc):
            return 4 * px + 2 * py + pc

        def copy(k, slot, block, to, src=None):
            return pltpu.make_async_remote_copy(
                src_ref=dsts[k].at[index(*block)] if src is None else src, dst_ref=dsts[k].at[index(*block)],
                send_sem=send_sems.at[k * 7 + slot], recv_sem=recv_sems.at[k * 7 + slot],
                device_id=to, device_id_type=pl.DeviceIdType.MESH)

        me = (x, y, c)
        local = [pltpu.make_async_copy(srcs[k], dsts[k].at[index(*me)], local_sems.at[k]) for k in range(n)]
        started = [copy(k, 0, me, sibling, src=srcs[k]) for k in range(n)]
        started += [copy(k, near_slot, me, (*near, c), src=srcs[k]) for k in range(n)]
        started += [copy(k, far_slot, me, (*far, c), src=srcs[k]) for k in range(n)]
        for cp in local + started:
            cp.start()
        for k in range(n):
            copy(k, near_slot, (*near, c), me).wait_recv()
            passed = [copy(k, 3, (*near, c), (*far, c)), copy(k, 3 + near_slot, (*near, c), sibling)]
            for cp in passed:
                cp.start()
            started += passed
        for slot, chip in ((far_slot, far), (3, diag)):
            for k in range(n):
                copy(k, slot, (*chip, c), me).wait_recv()
                passed = copy(k, 3 + slot, (*chip, c), sibling)
                passed.start()
                started.append(passed)
        for k in range(n):
            copy(k, 0, sibling, me).wait_recv()
            for slot, chip in ((near_slot, near), (far_slot, far), (3, diag)):
                copy(k, 3 + slot, (*chip, 1 - c), me).wait_recv()
        for cp in started:
            cp.wait_send()
        for cp in local:
            cp.wait()

    any_spec = pl.BlockSpec(memory_space=pl.ANY)
    return pl.pallas_call(
        body, name=name, out_shape=[jax.ShapeDtypeStruct((N_DEV,) + a.shape, a.dtype) for a in arrays],
        in_specs=[any_spec] * n, out_specs=[any_spec] * n,
        scratch_shapes=[pltpu.SemaphoreType.DMA((7 * n,)), pltpu.SemaphoreType.DMA((7 * n,)),
                        pltpu.SemaphoreType.DMA((n,))],
        compiler_params=pltpu.CompilerParams(has_side_effects=True),
    )(*arrays)


def _mod_shard(c_all, w_ada, b_shard):
    def body(c_ref, w_ref, b_ref, o_ref):
        cv = c_ref[...]
        ca = cv * _sigmoid(cv)
        o_ref[...] = jnp.dot(ca.astype(bf16), w_ref[...].astype(bf16), preferred_element_type=f32) + b_ref[...]

    return pl.pallas_call(body, name="mod_shard", out_shape=jax.ShapeDtypeStruct((N_DEV, w_ada.shape[1]), f32),
                          compiler_params=_params())(c_all, w_ada, b_shard)


def _norm_mod(x, norm_g, scale1p, shift):
    s_len = x.shape[0]
    tm = 512

    def body(x_ref, g_ref, sc_ref, sh_ref, h_ref):
        xb = x_ref[...]
        r = lax.rsqrt(jnp.mean(xb * xb, axis=-1, keepdims=True) + EPS)
        h_ref[...] = ((xb * r * g_ref[...]) * sc_ref[...] + sh_ref[...]).astype(bf16)

    row = pl.BlockSpec((tm, D_MODEL), lambda i: (i, 0))
    vec = pl.BlockSpec((1, D_MODEL), lambda i: (0, 0))
    return pl.pallas_call(body, name="norm_mod", grid=(s_len // tm,), in_specs=[row, vec, vec, vec], out_specs=row,
                          out_shape=jax.ShapeDtypeStruct((s_len, D_MODEL), bf16),
                          compiler_params=_params("parallel"))(x, norm_g, scale1p, shift)


def _in_proj(h, wt_main, wt_small):
    s_len = h.shape[0]
    tm, tn = min(1024, s_len), 1024

    def body(h_ref, w_ref, ws_ref, p_ref, ps_ref):
        @pl.when(pl.program_id(1) == 0)
        def _():
            ps_ref[...] = _dg(h_ref[...], ws_ref[...], 1, 1)

        p_ref[...] = _dg(h_ref[...], w_ref[...], 1, 1)

    return pl.pallas_call(
        body, name="in_proj", grid=(s_len // tm, N_MAIN // tn),
        in_specs=[pl.BlockSpec((tm, D_MODEL), lambda i, j: (i, 0)),
                  pl.BlockSpec((tn, D_MODEL), lambda i, j: (j, 0)),
                  pl.BlockSpec((N_SMALL, D_MODEL), lambda i, j: (0, 0))],
        out_specs=[pl.BlockSpec((tm, tn), lambda i, j: (i, j)),
                   pl.BlockSpec((tm, N_SMALL), lambda i, j: (i, 0))],
        out_shape=[jax.ShapeDtypeStruct((s_len, N_MAIN), f32), jax.ShapeDtypeStruct((s_len, N_SMALL), f32)],
        compiler_params=_params("parallel", "arbitrary"),
    )(h, wt_main, wt_small)


PREP_TM = 256
HALO = 8


def _conv_section(xe_ref, cw_ref, cols, tm):
    acc = cw_ref[pl.ds(CONV_K - 1, 1), cols] * xe_ref[pl.ds(HALO, tm), :]
    for tap in range(CONV_K - 1):
        acc = acc + cw_ref[pl.ds(tap, 1), cols] * xe_ref[pl.ds(HALO - (CONV_K - 1) + tap, tm), :]
    return acc


def _small_fwd(ps, bvec, alog):
    z = ps + bvec
    logsig, softp = _softplus_parts(z)
    gval = -jnp.exp(alog) * softp
    beta = _sigmoid(ps)
    return z, logsig, gval, beta


def _head_lane(block, lane):
    return jnp.sum(jnp.where(_iota(block.shape, 1) == lane, block, 0.0), axis=1, keepdims=True)


def _head_slab(block, lane):
    return jnp.broadcast_to(_head_lane(block, lane), block.shape)


def _chunk_masks(tm):
    r, c = _iota((tm, tm), 0), _iota((tm, tm), 1)
    same = (r // CHUNK) == (c // CHUNK)
    return (same & (r >= c)).astype(f32), same.astype(f32)


def _prep(p_main, p_small, qn_g, kn_g, conv_w, bvec, alog):
    s_len = p_main.shape[0]
    tm = PREP_TM
    nb = s_len // tm

    def body(fq_ref, fk_ref, fv_ref, gq_ref, gk_ref, gv_ref, hq_ref, hk_ref, hv_ref, ps_ref, qg_ref, kg_ref,
             cw_ref, bv_ref, al_ref,
             qs_ref, kn_ref, vb_ref, gqo_ref, gko_ref, gvo_ref, small_ref, xe_sc, carry_sc):
        i = pl.program_id(0)

        @pl.when(i == 0)
        def _():
            carry_sc[...] = jnp.zeros_like(carry_sc)

        vb_ref[...] = fv_ref[...].astype(bf16)

        first = i == 0
        for sec, (x_ref, halo_ref, o_ref) in enumerate(((gq_ref, hq_ref, gqo_ref), (gk_ref, hk_ref, gko_ref),
                                                        (gv_ref, hv_ref, gvo_ref))):
            xe_sc[0:HALO, :] = jnp.where(first, 0.0, halo_ref[...])
            xe_sc[HALO:, :] = x_ref[...]
            cv = _conv_section(xe_sc, cw_ref, slice(sec * WIDTH, (sec + 1) * WIDTH), tm)
            y = cv * _sigmoid(cv)
            if sec == 2:
                o_ref[...] = y
            else:
                mul = QK_SCALE if sec == 0 else 1.0
                for h in range(HEADS):
                    sl = slice(h * HEAD_DIM, (h + 1) * HEAD_DIM)
                    yh = y[:, sl]
                    o_ref[:, sl] = yh * (lax.rsqrt(jnp.sum(yh * yh, axis=-1, keepdims=True) + EPS) * mul)

        lane = _iota((tm, N_SMALL), 1)
        _, logsig, gval, beta = _small_fwd(ps_ref[...], bv_ref[...], al_ref[...])
        lf = jnp.where(lane < HEADS, logsig, 0.0)
        tri = (_iota((tm, tm), 0) >= _iota((tm, tm), 1)).astype(f32)
        fcum = jnp.dot(tri, lf, preferred_element_type=f32, precision=HI) + carry_sc[...]
        carry_sc[...] += jnp.sum(lf, axis=0, keepdims=True)
        tri_c, ones_c = _chunk_masks(tm)
        g_lanes = jnp.where((lane >= LANE_G) & (lane < LANE_G + HEADS), gval, 0.0)
        gc = jnp.dot(tri_c, g_lanes, preferred_element_type=f32, precision=HI)
        g_last = jnp.dot(ones_c, g_lanes, preferred_element_type=f32, precision=HI)
        small = jnp.where(lane < LANE_G, fcum, jnp.where(lane < LANE_BETA, gval, jnp.where(lane < LANE_GC, beta, 0.0)))
        small_ref[...] = small + pltpu.roll(gc, LANE_GC - LANE_G, 1) + pltpu.roll(g_last, LANE_GLAST - LANE_G, 1)

        qg, kg = qg_ref[...], kg_ref[...]
        f2 = fcum * LOG2E
        for h in range(HEADS):
            sl = slice(h * HEAD_DIM, (h + 1) * HEAD_DIM)
            q = fq_ref[:, sl]
            rq = lax.rsqrt(jnp.mean(q * q, axis=-1, keepdims=True) + EPS)
            k = fk_ref[:, sl]
            rk = lax.rsqrt(jnp.mean(k * k, axis=-1, keepdims=True) + EPS)
            f_col = _head_lane(f2, LANE_F + h)
            hi = f_col.astype(bf16).astype(f32)
            mid = (f_col - hi).astype(bf16).astype(f32)
            lo = f_col - hi - mid
            q_bias = jnp.where(lane == 0, hi, jnp.where(lane == 1, mid, jnp.where(lane == 2, lo,
                                                                                  jnp.where(lane < 6, 1.0, 0.0))))
            k_bias = jnp.where(lane < 3, 1.0, jnp.where(lane == 3, -hi, jnp.where(lane == 4, -mid,
                                                                                 jnp.where(lane == 5, -lo, 0.0))))
            base = 2 * h * HEAD_DIM
            qs_ref[:, base:base + HEAD_DIM] = (q * rq * qg * (QK_SCALE * LOG2E)).astype(bf16)
            qs_ref[:, base + HEAD_DIM:base + 2 * HEAD_DIM] = q_bias.astype(bf16)
            kn_ref[:, base:base + HEAD_DIM] = (k * rk * kg).astype(bf16)
            kn_ref[:, base + HEAD_DIM:base + 2 * HEAD_DIM] = k_bias.astype(bf16)

    def col(cb):
        return pl.BlockSpec((tm, WIDTH), lambda i: (i, cb))

    def halo(cb):
        return pl.BlockSpec((HALO, WIDTH), lambda i: (jnp.maximum(i * (tm // HALO) - 1, 0), cb))

    vec = pl.BlockSpec((1, LANES), lambda i: (0, 0))
    wide_f32 = jax.ShapeDtypeStruct((s_len, WIDTH), f32)
    wide_bf = jax.ShapeDtypeStruct((s_len, WIDTH), bf16)
    out_col = pl.BlockSpec((tm, WIDTH), lambda i: (i, 0))
    return pl.pallas_call(
        body, name="prep", grid=(nb,),
        in_specs=[col(0), col(1), col(2), col(4), col(5), col(6), halo(4), halo(5), halo(6),
                  pl.BlockSpec((tm, N_SMALL), lambda i: (i, 0)), vec, vec,
                  pl.BlockSpec((CONV_K, 3 * WIDTH), lambda i: (0, 0)), vec, vec],
        out_specs=[pl.BlockSpec((tm, 2 * WIDTH), lambda i: (i, 0))] * 2 + [out_col] * 4
                  + [pl.BlockSpec((tm, N_SMALL), lambda i: (i, 0))],
        out_shape=[jax.ShapeDtypeStruct((s_len, 2 * WIDTH), bf16)] * 2 + [wide_bf, wide_f32, wide_f32, wide_f32,
                                                                          jax.ShapeDtypeStruct((s_len, N_SMALL), f32)],
        scratch_shapes=[pltpu.VMEM((tm + HALO, WIDTH), f32), pltpu.VMEM((1, N_SMALL), f32)],
        compiler_params=_params("arbitrary"),
    )(p_main, p_main, p_main, p_main, p_main, p_main, p_main, p_main, p_main, p_small, qn_g, kn_g, conv_w, bvec, alog)


FOX_T = 1024
NEG_BIG = -1e30


def _fox_fwd(qs, kn, vb):
    s_len = qs.shape[0]
    t = FOX_T
    nq = s_len // t

    def body(q_ref, k_ref, v_ref, o_ref, lse_ref):
        qi = pl.program_id(1)
        q = q_ref[...]

        causal = _iota((t, t), 0) >= _iota((t, t), 1)

        def step(j, carry, masked):
            m, l, acc = carry
            rows = pl.ds(pl.multiple_of(j * t, t), t)
            s = _dg(q, k_ref[rows, :], 1, 1)
            if masked:
                s = jnp.where(causal, s, NEG_BIG)
            m_new = jnp.maximum(m, jnp.max(s, axis=-1, keepdims=True))
            p = jnp.exp2(s - m_new)
            alpha = jnp.exp2(m - m_new)
            l = alpha * l + jnp.sum(p, axis=-1, keepdims=True)
            acc = alpha * acc + jnp.dot(p.astype(bf16), v_ref[rows, :], preferred_element_type=f32)
            return m_new, l, acc

        init = (jnp.full((t, 1), NEG_BIG, f32), jnp.zeros((t, 1), f32), jnp.zeros((t, HEAD_DIM), f32))
        carry = lax.fori_loop(0, qi, lambda j, c: step(j, c, False), init)
        m, l, acc = step(qi, carry, True)
        o_ref[...] = acc / l
        lse_ref[0] = m + jnp.log2(l)

    return pl.pallas_call(
        body, name="fox_fwd", grid=(HEADS, nq),
        in_specs=[pl.BlockSpec((t, 2 * HEAD_DIM), lambda h, i: (i, h)),
                  pl.BlockSpec((s_len, 2 * HEAD_DIM), lambda h, i: (0, h)),
                  pl.BlockSpec((s_len, HEAD_DIM), lambda h, i: (0, h))],
        out_specs=[pl.BlockSpec((t, HEAD_DIM), lambda h, i: (i, h)),
                   pl.BlockSpec((1, t, 1), lambda h, i: (h, i, 0))],
        out_shape=[jax.ShapeDtypeStruct((s_len, WIDTH), f32), jax.ShapeDtypeStruct((HEADS, s_len, 1), f32)],
        compiler_params=_params("parallel", "arbitrary"),
    )(qs, kn, vb)


def _fox_bwd(qs, kn, vb, do, lse, delta):
    s_len = qs.shape[0]
    t = FOX_T
    nq = s_len // t
    half = t // 2

    def body(q_ref, do_ref, lse_ref, dl_ref, k_ref, v_ref, dq_ref, dk_ref, dvb_ref, df_ref, dfq_ref, dv_ref):
        head, qi = pl.program_id(0), pl.program_id(1)

        @pl.when(qi == 0)
        def _():
            dk_ref[...] = jnp.zeros_like(dk_ref)
            dv_ref[...] = jnp.zeros_like(dv_ref)
            df_ref[...] = jnp.zeros_like(df_ref)

        lse_col = lse_ref[0]
        dl = _head_lane(dl_ref[...], head)

        def update(q_rows, k_rows, df_lanes, j, carry, mask):
            dq, row_sum = carry
            q, do_b = q_ref[q_rows, :], do_ref[q_rows, :]
            p = jnp.exp2(_dg(q, k_ref[k_rows, :], 1, 1) - lse_col[q_rows])
            if mask is not None:
                p = jnp.where(mask, p, 0.0)
            ds = p * (_dg(do_b, v_ref[k_rows, :], 1, 1) - dl[q_rows])
            ds_b = ds.astype(bf16)
            dk_ref[k_rows, :] += _dg(ds_b, q_ref[q_rows, 0:HEAD_DIM], 0, 0)
            dv_ref[k_rows, :] += _dg(p.astype(bf16), do_b, 0, 0)
            df_ref[0, j, :, df_lanes] += -jnp.sum(ds, axis=0, keepdims=True)
            dq = dq + jnp.dot(ds_b, k_ref[k_rows, 0:HEAD_DIM], preferred_element_type=f32)
            return dq, row_sum + jnp.sum(ds, axis=-1, keepdims=True)

        everything, upper, lower = slice(0, t), slice(0, half), slice(half, t)
        carry = lax.fori_loop(
            0, qi, lambda j, c: update(everything, pl.ds(pl.multiple_of(j * t, t), t), everything, j, c, None),
            (jnp.zeros((t, HEAD_DIM), f32), jnp.zeros((t, 1), f32)))
        carry = update(everything, pl.ds(pl.multiple_of(qi * t, t), half), upper, qi, carry,
                       _iota((t, half), 0) >= _iota((t, half), 1))
        low = update(lower, pl.ds(pl.multiple_of(qi * t + half, half), half), lower, qi,
                     tuple(c[half:] for c in carry), _iota((half, half), 0) >= _iota((half, half), 1))
        dq, row_sum = (jnp.concatenate([c[:half], lo], axis=0) for c, lo in zip(carry, low))
        dq_ref[...] = dq
        dfq_ref[0] = row_sum

        @pl.when(qi == nq - 1)
        def _():
            dvb_ref[...] = dv_ref[...].astype(bf16)

    blk = pl.BlockSpec((t, HEAD_DIM), lambda h, i: (i, h))
    blk2 = pl.BlockSpec((t, 2 * HEAD_DIM), lambda h, i: (i, h))
    full = pl.BlockSpec((s_len, HEAD_DIM), lambda h, i: (0, h))
    full2 = pl.BlockSpec((s_len, 2 * HEAD_DIM), lambda h, i: (0, h))
    colv = pl.BlockSpec((1, t, 1), lambda h, i: (h, i, 0))
    rowv = pl.BlockSpec((1, nq, 1, t), lambda h, i: (h, 0, 0, 0))
    lanes = pl.BlockSpec((t, N_SMALL), lambda h, i: (i, 0))
    wide = jax.ShapeDtypeStruct((s_len, WIDTH), f32)
    return pl.pallas_call(
        body, name="fox_bwd", grid=(HEADS, nq),
        in_specs=[blk2, blk, colv, lanes, full2, full],
        out_specs=[blk, full, full, rowv, colv],
        out_shape=[wide, wide, jax.ShapeDtypeStruct((s_len, WIDTH), bf16), jax.ShapeDtypeStruct((HEADS, nq, 1, t), f32),
                   jax.ShapeDtypeStruct((HEADS, s_len, 1), f32)],
        scratch_shapes=[pltpu.VMEM((s_len, HEAD_DIM), f32)],
        compiler_params=_params("parallel", "arbitrary"),
    )(qs, do, lse, delta, kn, vb)


INTRA_CHUNKS = 8
SCAN_FWD_CHUNKS = 8
SCAN_BWD_CHUNKS = 4


def _gdn_intra_fwd(gq, gk, gv, small):
    s_len = gq.shape[0]
    cpb = INTRA_CHUNKS
    rows_blk = cpb * CHUNK
    n_chunks = s_len // CHUNK

    def body(q_ref, k_ref, v_ref, sm_ref, u_ref, w_ref, qg_ref, kd_ref, attn_ref, t_ref, eg_ref):
        head = pl.program_id(0)
        sm = sm_ref[...]
        gc_b, gl_b, beta_b = (_head_slab(sm, LANE_GC + head), _head_slab(sm, LANE_GLAST + head),
                              _head_slab(sm, LANE_BETA + head))
        ms, rhss = [], []
        for ci in range(cpb):
            rows = pl.ds(ci * CHUNK, CHUNK)
            sl = slice(ci * CHUNK, (ci + 1) * CHUNK)
            m, rhs, qg, kd, attn, eg_last = _gdn_intra_pre(q_ref[rows, :], k_ref[rows, :], v_ref[rows, :],
                                                           gc_b[sl], gl_b[sl], beta_b[sl], _BF_PLAIN[1])
            qg_ref[rows, :] = qg.astype(bf16)
            kd_ref[rows, :] = kd.astype(bf16)
            attn_ref[0, ci] = attn.astype(bf16)
            eg_ref[0, ci] = eg_last
            ms.append(m)
            rhss.append(rhs)
        for ci, (t, rhs) in enumerate(zip(_inv_unit_lower_many(ms), rhss)):
            rows = pl.ds(ci * CHUNK, CHUNK)
            t_ref[0, ci] = t
            uw = _X3_PLAIN[0](t, rhs)
            u_ref[rows, :] = uw[:, :HEAD_DIM]
            w_ref[rows, :] = uw[:, HEAD_DIM:].astype(bf16)

    blk = pl.BlockSpec((rows_blk, HEAD_DIM), lambda h, i: (i, h))
    sq = pl.BlockSpec((1, cpb, CHUNK, CHUNK), lambda h, i: (h, i, 0, 0))
    wide_bf = jax.ShapeDtypeStruct((s_len, WIDTH), bf16)
    return pl.pallas_call(
        body, name="gdn_intra_fwd", grid=(HEADS, s_len // rows_blk),
        in_specs=[blk] * 3 + [pl.BlockSpec((rows_blk, N_SMALL), lambda h, i: (i, 0))],
        out_specs=[blk] * 4 + [sq, sq, pl.BlockSpec((1, cpb, SUBLANES, HEAD_DIM), lambda h, i: (h, i, 0, 0))],
        out_shape=[jax.ShapeDtypeStruct((s_len, WIDTH), f32), wide_bf, wide_bf, wide_bf,
                   jax.ShapeDtypeStruct((HEADS, n_chunks, CHUNK, CHUNK), bf16),
                   jax.ShapeDtypeStruct((HEADS, n_chunks, CHUNK, CHUNK), f32),
                   jax.ShapeDtypeStruct((HEADS, n_chunks, SUBLANES, HEAD_DIM), f32)],
        compiler_params=_params("parallel", "parallel"),
    )(gq, gk, gv, small)


def _gdn_scan_fwd(u, w, qg, kd, attn, eg):
    s_len = u.shape[0]
    cpb = SCAN_FWD_CHUNKS
    rows_blk = cpb * CHUNK
    n_chunks = s_len // CHUNK

    def body(u_ref, w_ref, qg_ref, kd_ref, attn_ref, eg_ref, o_ref, st_ref, s_sc):
        @pl.when(pl.program_id(0) == 0)
        def _():
            s_sc[...] = jnp.zeros_like(s_sc)

        def chunk(ci, _):
            rows = pl.ds(pl.multiple_of(ci * CHUNK, CHUNK), CHUNK)
            cols = [slice(h * HEAD_DIM, (h + 1) * HEAD_DIM) for h in range(HEADS)]
            s0 = [s_sc[h] for h in range(HEADS)]
            s0_b = [s.astype(bf16) for s in s0]
            for h in range(HEADS):
                st_ref[h, ci] = s0[h]
            ws = [jnp.dot(w_ref[rows, cols[h]], s0_b[h], preferred_element_type=f32) for h in range(HEADS)]
            qs = [jnp.dot(qg_ref[rows, cols[h]], s0_b[h], preferred_element_type=f32) for h in range(HEADS)]
            vn_b = [(u_ref[rows, cols[h]] - ws[h]).astype(bf16) for h in range(HEADS)]
            av = [jnp.dot(attn_ref[h, ci], vn_b[h], preferred_element_type=f32) for h in range(HEADS)]
            kv = [_dg(kd_ref[rows, cols[h]], vn_b[h], 0, 0) for h in range(HEADS)]
            for h in range(HEADS):
                o_ref[rows, cols[h]] = qs[h] + av[h]
                s_sc[h] = _scale_rows(s0[h], eg_ref[h, ci]) + kv[h]
            return 0

        lax.fori_loop(0, cpb, chunk, 0)

    row = pl.BlockSpec((rows_blk, WIDTH), lambda i: (i, 0))
    return pl.pallas_call(
        body, name="gdn_scan_fwd", grid=(s_len // rows_blk,),
        in_specs=[row] * 4 + [pl.BlockSpec((HEADS, cpb, CHUNK, CHUNK), lambda i: (0, i, 0, 0)),
                              pl.BlockSpec((HEADS, cpb, SUBLANES, HEAD_DIM), lambda i: (0, i, 0, 0))],
        out_specs=[row, pl.BlockSpec((HEADS, cpb, HEAD_DIM, HEAD_DIM), lambda i: (0, i, 0, 0))],
        out_shape=[jax.ShapeDtypeStruct((s_len, WIDTH), f32),
                   jax.ShapeDtypeStruct((HEADS, n_chunks, HEAD_DIM, HEAD_DIM), f32)],
        scratch_shapes=[pltpu.VMEM((HEADS, HEAD_DIM, HEAD_DIM), f32)],
        compiler_params=_params("arbitrary"),
    )(u, w, qg, kd, attn, eg)


def _gdn_scan_bwd(u, w, qg, kd, attn, eg, states, d_o):
    s_len = u.shape[0]
    cpb = SCAN_BWD_CHUNKS
    rows_blk = cpb * CHUNK
    n_chunks = s_len // CHUNK
    nb = s_len // rows_blk

    def body(u_ref, w_ref, qg_ref, kd_ref, attn_ref, eg_ref, st_ref, do_ref,
             du_ref, dw_ref, dqg_ref, dkd_ref, dattn_ref, deg_ref, ds_sc):
        @pl.when(pl.program_id(0) == 0)
        def _():
            ds_sc[...] = jnp.zeros_like(ds_sc)

        def chunk(step, _):
            ci = cpb - 1 - step
            rows = pl.ds(pl.multiple_of(ci * CHUNK, CHUNK), CHUNK)
            hs = range(HEADS)
            cols = [slice(h * HEAD_DIM, (h + 1) * HEAD_DIM) for h in hs]
            s0 = [st_ref[h, ci] for h in hs]
            s0_b = [s.astype(bf16) for s in s0]
            ds1 = [ds_sc[h] for h in hs]
            ds1_b = [d.astype(bf16) for d in ds1]
            do_b = [do_ref[rows, cols[h]].astype(bf16) for h in hs]
            ws = [jnp.dot(w_ref[rows, cols[h]], s0_b[h], preferred_element_type=f32) for h in hs]
            ad = [_dg(attn_ref[h, ci], do_b[h], 0, 0) for h in hs]
            kd_ds = [jnp.dot(kd_ref[rows, cols[h]], ds1_b[h], preferred_element_type=f32) for h in hs]
            dqg = [_dg(do_b[h], s0_b[h], 1, 1) for h in hs]
            qd = [_dg(qg_ref[rows, cols[h]], do_b[h], 0, 0) for h in hs]
            vn_b = [(u_ref[rows, cols[h]] - ws[h]).astype(bf16) for h in hs]
            dvn = [ad[h] + kd_ds[h] for h in hs]
            dvn_b = [d.astype(bf16) for d in dvn]
            dattn = [_dg(do_b[h], vn_b[h], 1, 1) for h in hs]
            dkd = [_dg(vn_b[h], ds1_b[h], 1, 1) for h in hs]
            dw = [_dg(dvn_b[h], s0_b[h], 1, 1) for h in hs]
            wd = [_dg(w_ref[rows, cols[h]], dvn_b[h], 0, 0) for h in hs]
            for h in hs:
                dattn_ref[h, ci] = dattn[h]
                dqg_ref[rows, cols[h]] = dqg[h]
                dkd_ref[rows, cols[h]] = dkd[h]
                du_ref[rows, cols[h]] = dvn[h]
                dw_ref[rows, cols[h]] = -dw[h]
                ds_sc[h] = qd[h] - wd[h] + _scale_rows(ds1[h], eg_ref[h, ci])
                deg_ref[h, ci] = jnp.sum((ds1[h] * s0[h]).reshape(HEAD_DIM // SUBLANES, SUBLANES, HEAD_DIM), axis=0)
            return 0

        lax.fori_loop(0, cpb, chunk, 0)

    row = pl.BlockSpec((rows_blk, WIDTH), lambda i: (nb - 1 - i, 0))
    sq = pl.BlockSpec((HEADS, cpb, CHUNK, CHUNK), lambda i: (0, nb - 1 - i, 0, 0))
    egs = pl.BlockSpec((HEADS, cpb, SUBLANES, HEAD_DIM), lambda i: (0, nb - 1 - i, 0, 0))
    wide = jax.ShapeDtypeStruct((s_len, WIDTH), f32)
    return pl.pallas_call(
        body, name="gdn_scan_bwd", grid=(nb,),
        in_specs=[row] * 4 + [sq, egs, pl.BlockSpec((HEADS, cpb, HEAD_DIM, HEAD_DIM), lambda i: (0, nb - 1 - i, 0, 0)), row],
        out_specs=[row] * 4 + [sq, egs],
        out_shape=[wide] * 4 + [jax.ShapeDtypeStruct((HEADS, n_chunks, CHUNK, CHUNK), f32),
                                jax.ShapeDtypeStruct((HEADS, n_chunks, SUBLANES, HEAD_DIM), f32)],
        scratch_shapes=[pltpu.VMEM((HEADS, HEAD_DIM, HEAD_DIM), f32)],
        compiler_params=_params("arbitrary"),
    )(u, w, qg, kd, attn, eg, states, d_o)


def _gdn_intra_bwd(gq, gk, gv, small, t_inv, du, dw, dqg, dkd, dattn, deg):
    s_len = gq.shape[0]
    cpb = INTRA_CHUNKS
    rows_blk = cpb * CHUNK

    def body(q_ref, k_ref, v_ref, sm_ref, t_ref, du_ref, dw_ref, dqg_ref, dkd_ref, dattn_ref, deg_ref,
             dq_ref, dk_ref, dv_ref, dsm_ref):
        head = pl.program_id(1)

        def batch(value):
            return value.reshape(cpb, CHUNK, HEAD_DIM)

        sm = sm_ref[...]
        slabs = [batch(_head_slab(sm, first + head)) for first in (LANE_GC, LANE_GLAST, LANE_BETA)]
        t_known = t_ref[0]
        _, vjp = jax.vjp(lambda q, k, v, gc, gl, b: _gdn_intra(q, k, v, gc, gl, b, t_known),
                         batch(q_ref[...]), batch(k_ref[...]), batch(v_ref[...]), *slabs)
        duw = jnp.concatenate([batch(du_ref[...]), batch(dw_ref[...])], axis=-1)
        dq, dk, dv, dgc, dgl, db = vjp((duw, batch(dqg_ref[...]), batch(dkd_ref[...]), dattn_ref[0], deg_ref[0]))
        for ref, grad in zip((dq_ref, dk_ref, dv_ref), (dq, dk, dv)):
            ref[...] = grad.reshape(rows_blk, HEAD_DIM)

        @pl.when(head == 0)
        def _():
            dsm_ref[...] = jnp.zeros_like(dsm_ref)

        lane = _iota((rows_blk, N_SMALL), 1)
        acc = dsm_ref[...]
        for first, grad in ((LANE_GC, dgc), (LANE_GLAST, dgl), (LANE_BETA, db)):
            col = jnp.sum(grad.reshape(rows_blk, HEAD_DIM), axis=1, keepdims=True)
            acc = acc + jnp.where(lane == first + head, col, 0.0)
        dsm_ref[...] = acc

    blk = pl.BlockSpec((rows_blk, HEAD_DIM), lambda i, h: (i, h))
    sq = pl.BlockSpec((1, cpb, CHUNK, CHUNK), lambda i, h: (h, i, 0, 0))
    egs = pl.BlockSpec((1, cpb, SUBLANES, HEAD_DIM), lambda i, h: (h, i, 0, 0))
    lanes = pl.BlockSpec((rows_blk, N_SMALL), lambda i, h: (i, 0))
    wide = jax.ShapeDtypeStruct((s_len, WIDTH), f32)
    return pl.pallas_call(
        body, name="gdn_intra_bwd", grid=(s_len // rows_blk, HEADS),
        in_specs=[blk] * 3 + [lanes, sq] + [blk] * 4 + [sq, egs],
        out_specs=[blk] * 3 + [lanes],
        out_shape=[wide] * 3 + [jax.ShapeDtypeStruct((s_len, N_SMALL), f32)],
        compiler_params=_params("parallel", "arbitrary"),
    )(gq, gk, gv, small, t_inv, du, dw, dqg, dkd, dattn, deg)


MIX_TM = 512


def _mix_fwd(fox_o, gdn_o, p_main, gnorm_g):
    s_len = fox_o.shape[0]
    tm = MIX_TM

    def body(fo_ref, go_ref, fz_ref, gz_ref, g_ref, mixed_ref):
        fz = fz_ref[...]
        mixed_ref[:, 0:WIDTH] = (fo_ref[...] * (fz * _sigmoid(fz))).astype(bf16)
        gz = gz_ref[...]
        gate = gz * _sigmoid(gz)
        gg = g_ref[...]
        for h in range(HEADS):
            sl = slice(h * HEAD_DIM, (h + 1) * HEAD_DIM)
            o = go_ref[:, sl]
            r = lax.rsqrt(jnp.mean(o * o, axis=-1, keepdims=True) + EPS)
            mixed_ref[:, WIDTH + h * HEAD_DIM:WIDTH + (h + 1) * HEAD_DIM] = (o * r * gg * gate[:, sl]).astype(bf16)

    row = pl.BlockSpec((tm, WIDTH), lambda i: (i, 0))
    return pl.pallas_call(
        body, name="mix_fwd", grid=(s_len // tm,),
        in_specs=[row, row, pl.BlockSpec((tm, WIDTH), lambda i: (i, 3)), pl.BlockSpec((tm, WIDTH), lambda i: (i, 7)),
                  pl.BlockSpec((1, LANES), lambda i: (0, 0))],
        out_specs=pl.BlockSpec((tm, 2 * WIDTH), lambda i: (i, 0)),
        out_shape=jax.ShapeDtypeStruct((s_len, 2 * WIDTH), bf16),
        compiler_params=_params("parallel"),
    )(fox_o, gdn_o, p_main, p_main, gnorm_g)


def _silu_and_grad(z):
    sg = _sigmoid(z)
    return z * sg, sg * (1.0 + z * (1.0 - sg))


def _mix_bwd(dmixed, fox_o, gdn_o, p_main, gnorm_g):
    s_len = fox_o.shape[0]
    tm = MIX_TM

    def body(dm_ref, fo_ref, go_ref, fz_ref, gz_ref, g_ref, dof_ref, delta_ref, dfz_ref, dgz_ref, dgo_ref, dg_ref):
        @pl.when(pl.program_id(0) == 0)
        def _():
            dg_ref[...] = jnp.zeros_like(dg_ref)

        lane = _iota((tm, LANES), 1)
        fz = fz_ref[...]
        dmf = dm_ref[:, 0:WIDTH]
        fo = fo_ref[...]
        f_gate, f_grad = _silu_and_grad(fz)
        dof = dmf * f_gate
        dof_ref[...] = dof.astype(bf16)
        dfz_ref[...] = (dmf * fo * f_grad).astype(bf16)
        prod = dof * fo
        delta = jnp.zeros((tm, LANES), f32)
        for h in range(HEADS):
            dh = jnp.sum(prod[:, h * HEAD_DIM:(h + 1) * HEAD_DIM], axis=-1, keepdims=True)
            delta = jnp.where(lane == h, dh, delta)
        delta_ref[...] = delta

        gz = gz_ref[...]
        dmg = dm_ref[:, WIDTH:2 * WIDTH]
        gate, sgrad = _silu_and_grad(gz)
        gg = g_ref[...]
        dg_acc = jnp.zeros((1, HEAD_DIM), f32)
        for h in range(HEADS):
            sl = slice(h * HEAD_DIM, (h + 1) * HEAD_DIM)
            o = go_ref[:, sl]
            r = lax.rsqrt(jnp.mean(o * o, axis=-1, keepdims=True) + EPS)
            on = o * r
            dmh = dmg[:, sl]
            dgz_ref[:, sl] = (dmh * (on * gg) * sgrad[:, sl]).astype(bf16)
            dy = dmh * gate[:, sl]
            dg_acc = dg_acc + jnp.sum(dy * on, axis=0, keepdims=True)
            tt = dy * gg
            dgo_ref[:, sl] = r * (tt - on * jnp.mean(tt * on, axis=-1, keepdims=True))
        dg_ref[...] += dg_acc

    row = pl.BlockSpec((tm, WIDTH), lambda i: (i, 0))
    wide_bf = jax.ShapeDtypeStruct((s_len, WIDTH), bf16)
    return pl.pallas_call(
        body, name="mix_bwd", grid=(s_len // tm,),
        in_specs=[pl.BlockSpec((tm, 2 * WIDTH), lambda i: (i, 0)), row, row,
                  pl.BlockSpec((tm, WIDTH), lambda i: (i, 3)), pl.BlockSpec((tm, WIDTH), lambda i: (i, 7)),
                  pl.BlockSpec((1, LANES), lambda i: (0, 0))],
        out_specs=[row, pl.BlockSpec((tm, LANES), lambda i: (i, 0)), row, row, row,
                   pl.BlockSpec((1, LANES), lambda i: (0, 0))],
        out_shape=[wide_bf, jax.ShapeDtypeStruct((s_len, LANES), f32), wide_bf, wide_bf,
                   jax.ShapeDtypeStruct((s_len, WIDTH), f32), jax.ShapeDtypeStruct((1, LANES), f32)],
        compiler_params=_params("arbitrary"),
    )(dmixed, fox_o, gdn_o, p_main, p_main, gnorm_g)


def _out_head(mixed, w_out, x, target, gate, final_g):
    s_len = x.shape[0]
    tm = 256

    def body(mx_ref, w_ref, x_ref, t_ref, gate_ref, fg_ref, loss_ref, dy_ref, dz_ref, dm_ref, dfg_ref, dgate_ref):
        @pl.when(pl.program_id(0) == 0)
        def _():
            loss_ref[...] = jnp.zeros_like(loss_ref)
            dfg_ref[...] = jnp.zeros_like(dfg_ref)
            dgate_ref[...] = jnp.zeros_like(dgate_ref)

        w = w_ref[...]
        z = jnp.dot(mx_ref[...], w, preferred_element_type=f32)
        gate_v, fg = gate_ref[...], fg_ref[...]
        y1 = x_ref[...] + gate_v * z
        r = lax.rsqrt(jnp.mean(y1 * y1, axis=-1, keepdims=True) + EPS)
        yn = y1 * r
        err = yn * fg - t_ref[...]
        loss_ref[...] += 0.5 * jnp.sum(jnp.mean(err * err, axis=-1, keepdims=True))
        dout = err * (1.0 / D_MODEL)
        dfg_ref[...] += jnp.sum(dout * yn, axis=0, keepdims=True)
        tt = dout * fg
        dy1 = r * (tt - yn * jnp.mean(tt * yn, axis=-1, keepdims=True))
        dy_ref[...] = dy1
        dgate_ref[...] += jnp.sum(dy1 * z, axis=0, keepdims=True)
        dz = (dy1 * gate_v).astype(bf16)
        dz_ref[...] = dz
        dm_ref[...] = _dg(dz, w, 1, 1)

    row = pl.BlockSpec((tm, D_MODEL), lambda i: (i, 0))
    vec = pl.BlockSpec((1, D_MODEL), lambda i: (0, 0))
    big = jax.ShapeDtypeStruct((s_len, D_MODEL), f32)
    return pl.pallas_call(
        body, name="out_head", grid=(s_len // tm,),
        in_specs=[row, pl.BlockSpec((D_MODEL, D_MODEL), lambda i: (0, 0)), row, row, vec, vec],
        out_specs=[pl.BlockSpec((1, LANES), lambda i: (0, 0)), row, row, row, vec, vec],
        out_shape=[jax.ShapeDtypeStruct((1, LANES), f32), big, jax.ShapeDtypeStruct((s_len, D_MODEL), bf16), big,
                   jax.ShapeDtypeStruct((1, D_MODEL), f32), jax.ShapeDtypeStruct((1, D_MODEL), f32)],
        compiler_params=_params("arbitrary"),
    )(mixed, w_out, x, target, gate, final_g)


def _matmul_tn(name, a, b, out_dtype):
    k_len, m_len = a.shape
    n_len = b.shape[1]
    tk, tm, tn = min(2048, k_len), min(1024, m_len), min(2048, n_len)
    nk = k_len // tk

    def body(a_ref, b_ref, o_ref, acc_sc):
        k = pl.program_id(2)

        @pl.when(k == 0)
        def _():
            acc_sc[...] = jnp.zeros_like(acc_sc)

        acc_sc[...] += _dg(a_ref[...], b_ref[...], 0, 0)

        @pl.when(k == nk - 1)
        def _():
            o_ref[...] = acc_sc[...].astype(out_dtype)

    return pl.pallas_call(
        body, name=name, grid=(m_len // tm, n_len // tn, nk),
        in_specs=[pl.BlockSpec((tk, tm), lambda i, j, k: (k, i)), pl.BlockSpec((tk, tn), lambda i, j, k: (k, j))],
        out_specs=pl.BlockSpec((tm, tn), lambda i, j, k: (i, j)),
        out_shape=jax.ShapeDtypeStruct((m_len, n_len), out_dtype),
        scratch_shapes=[pltpu.VMEM((tm, tn), f32)],
        compiler_params=_params("parallel", "parallel", "arbitrary"),
    )(a, b)


def _post1(p_main, p_small, qn_g, kn_g, conv_w, bvec, alog, dqs, dkn, dgq, dgk, dgv, d_small, df, df_query):
    s_len = p_main.shape[0]
    tm = PREP_TM
    nb = s_len // tm

    def body(fq_ref, fk_ref, gq_ref, gk_ref, gv_ref, hq_ref, hk_ref, hv_ref, ps_ref, qg_ref, kg_ref, cw_ref, bv_ref,
             al_ref, dqs_ref, dkn_ref, dgq_ref, dgk_ref, dgv_ref, dsm_ref, df_ref, dfq_in_ref,
             dfq_ref, dfk_ref, dx_ref, dps_ref, dqg_ref, dkg_ref, sums_ref, dw_ref, xe_sc, carry_sc, dc_sc, next_sc):
        step = pl.program_id(0)
        blk = nb - 1 - step

        @pl.when(step == 0)
        def _():
            carry_sc[...] = jnp.zeros_like(carry_sc)
            next_sc[...] = jnp.zeros_like(next_sc)
            dqg_ref[...] = jnp.zeros_like(dqg_ref)
            dkg_ref[...] = jnp.zeros_like(dkg_ref)
            sums_ref[...] = jnp.zeros_like(sums_ref)
            dw_ref[...] = jnp.zeros_like(dw_ref)

        for x_ref, g_ref, dy_ref, o_ref, acc_ref, mul in ((fq_ref, qg_ref, dqs_ref, dfq_ref, dqg_ref, QK_SCALE),
                                                          (fk_ref, kg_ref, dkn_ref, dfk_ref, dkg_ref, LN2)):
            gain = g_ref[...]
            acc = jnp.zeros((1, HEAD_DIM), f32)
            for h in range(HEADS):
                sl = slice(h * HEAD_DIM, (h + 1) * HEAD_DIM)
                xv = x_ref[:, sl]
                r = lax.rsqrt(jnp.mean(xv * xv, axis=-1, keepdims=True) + EPS)
                xn = xv * r
                dy = dy_ref[:, sl] * mul
                acc = acc + jnp.sum(dy * xn, axis=0, keepdims=True)
                tt = dy * gain
                o_ref[:, sl] = (r * (tt - xn * jnp.mean(tt * xn, axis=-1, keepdims=True))).astype(bf16)
            acc_ref[...] += acc

        first = blk == 0
        for sec, (x_ref, halo_ref, dy_ref) in enumerate(((gq_ref, hq_ref, dgq_ref), (gk_ref, hk_ref, dgk_ref),
                                                         (gv_ref, hv_ref, dgv_ref))):
            cols = slice(sec * WIDTH, (sec + 1) * WIDTH)
            xe_sc[0:HALO, :] = jnp.where(first, 0.0, halo_ref[...])
            xe_sc[HALO:, :] = x_ref[...]
            cv = _conv_section(xe_sc, cw_ref, cols, tm)
            y, sgrad = _silu_and_grad(cv)
            if sec == 2:
                dc_sc[0:tm, :] = dy_ref[...] * sgrad
            else:
                mul = QK_SCALE if sec == 0 else 1.0
                for h in range(HEADS):
                    sl = slice(h * HEAD_DIM, (h + 1) * HEAD_DIM)
                    yh = y[:, sl]
                    r = lax.rsqrt(jnp.sum(yh * yh, axis=-1, keepdims=True) + EPS)
                    dqh = dy_ref[:, sl]
                    dyh = (mul * r) * (dqh - yh * (r * r) * jnp.sum(dqh * yh, axis=-1, keepdims=True))
                    dc_sc[0:tm, sl] = dyh * sgrad[:, sl]
            dc_sc[tm:, :] = next_sc[sec]
            x_rows = xe_sc[pl.ds(HALO, tm), :]
            dx = jnp.zeros((tm, WIDTH), f32)
            dw = jnp.zeros((8, WIDTH), f32)
            tap_row = _iota((8, WIDTH), 0)
            for tap in range(CONV_K):
                ahead = dc_sc[pl.ds(CONV_K - 1 - tap, tm), :]
                dx = dx + cw_ref[pl.ds(tap, 1), cols] * ahead
                dw = jnp.where(tap_row == tap, jnp.sum(x_rows * ahead, axis=0, keepdims=True), dw)
            dx_ref[:, cols] = dx.astype(bf16)
            dw_ref[:, cols] += dw
            next_sc[sec] = dc_sc[0:HALO, :]

        lane = _iota((tm, N_SMALL), 1)
        z, _, gval, beta = _small_fwd(ps_ref[...], bv_ref[...], al_ref[...])
        sig_z = _sigmoid(z)
        dsm = dsm_ref[...]
        in_g = (lane >= LANE_G) & (lane < LANE_G + HEADS)
        dgc = jnp.where(in_g, pltpu.roll(dsm, N_SMALL - (LANE_GC - LANE_G), 1), 0.0)
        dgl = jnp.where(in_g, pltpu.roll(dsm, N_SMALL - (LANE_GLAST - LANE_G), 1), 0.0)
        tri_c, ones_c = _chunk_masks(tm)
        dg = (_dg(tri_c, dgc, 0, 0, HI) + jnp.dot(ones_c, dgl, preferred_element_type=f32, precision=HI))
        dbeta = dsm
        dfb = jnp.where(lane < HEADS, df_ref[...], 0.0)
        for h in range(HEADS):
            dfb = dfb + jnp.where(lane == h, dfq_in_ref[h], 0.0)
        tri_u = (_iota((tm, tm), 1) >= _iota((tm, tm), 0)).astype(f32)
        dlogf = jnp.dot(tri_u, dfb, preferred_element_type=f32, precision=HI) + carry_sc[...]
        carry_sc[...] += jnp.sum(dfb, axis=0, keepdims=True)
        dff = dlogf * (1.0 - sig_z)
        dga = dg * (-jnp.exp(al_ref[...])) * sig_z
        dgb_small = dbeta * beta * (1.0 - beta)
        dps = jnp.where(lane < HEADS, dff, jnp.where(lane < 2 * HEADS, dga, jnp.where(lane < 3 * HEADS, dgb_small, 0.0)))
        dps_ref[...] = dps.astype(bf16)
        row = _iota((8, N_SMALL), 0)
        s0 = jnp.sum(dps, axis=0, keepdims=True)
        s1 = jnp.sum(jnp.where((lane >= HEADS) & (lane < 2 * HEADS), dg * gval, 0.0), axis=0, keepdims=True)
        sums_ref[...] += jnp.where(row == 0, s0, jnp.where(row == 1, s1, 0.0))

    def col(cb):
        return pl.BlockSpec((tm, WIDTH), lambda i: (nb - 1 - i, cb))

    def halo(cb):
        return pl.BlockSpec((HALO, WIDTH), lambda i: (jnp.maximum((nb - 1 - i) * (tm // HALO) - 1, 0), cb))

    vec = pl.BlockSpec((1, LANES), lambda i: (0, 0))
    row0 = pl.BlockSpec((tm, WIDTH), lambda i: (nb - 1 - i, 0))
    small = pl.BlockSpec((tm, N_SMALL), lambda i: (nb - 1 - i, 0))
    wide_bf = jax.ShapeDtypeStruct((s_len, WIDTH), bf16)
    return pl.pallas_call(
        body, name="post1", grid=(nb,),
        in_specs=[col(0), col(1), col(4), col(5), col(6), halo(4), halo(5), halo(6), small, vec, vec,
                  pl.BlockSpec((CONV_K, 3 * WIDTH), lambda i: (0, 0)), vec, vec,
                  row0, row0, row0, row0, row0, small, small,
                  pl.BlockSpec((HEADS, tm, 1), lambda i: (0, nb - 1 - i, 0))],
        out_specs=[row0, row0, pl.BlockSpec((tm, 3 * WIDTH), lambda i: (nb - 1 - i, 0)), small, vec, vec,
                   pl.BlockSpec((8, N_SMALL), lambda i: (0, 0)), pl.BlockSpec((8, 3 * WIDTH), lambda i: (0, 0))],
        out_shape=[wide_bf, wide_bf, jax.ShapeDtypeStruct((s_len, 3 * WIDTH), bf16),
                   jax.ShapeDtypeStruct((s_len, N_SMALL), bf16), jax.ShapeDtypeStruct((1, LANES), f32),
                   jax.ShapeDtypeStruct((1, LANES), f32), jax.ShapeDtypeStruct((8, N_SMALL), f32),
                   jax.ShapeDtypeStruct((8, 3 * WIDTH), f32)],
        scratch_shapes=[pltpu.VMEM((tm + HALO, WIDTH), f32), pltpu.VMEM((1, N_SMALL), f32),
                        pltpu.VMEM((tm + HALO, WIDTH), f32), pltpu.VMEM((3, HALO, WIDTH), f32)],
        compiler_params=_params("arbitrary"),
    )(p_main, p_main, p_main, p_main, p_main, p_main, p_main, p_main, p_small, qn_g, kn_g, conv_w, bvec, alog,
      dqs, dkn, dgq, dgk, dgv, d_small, df, df_query)


def _in_proj_bwd(dp_pieces, dp_small, wt_main, wt_small):
    s_len = dp_small.shape[0]
    tm, tk = min(1024, s_len), WIDTH
    nk = N_MAIN // tk
    first_section = [sum(p.shape[1] // tk for p in dp_pieces[:n]) for n in range(len(dp_pieces))]
    n_pieces = len(dp_pieces)

    def body(*refs):
        piece_refs = refs[:n_pieces]
        dps_ref, w_ref, ws_ref, dh_ref = refs[n_pieces:]
        k = pl.program_id(1)

        @pl.when(k == 0)
        def _():
            dh_ref[...] = jnp.dot(dps_ref[...], ws_ref[...], preferred_element_type=f32)

        for piece, ref, first in zip(dp_pieces, piece_refs, first_section):
            @pl.when((k >= first) & (k < first + piece.shape[1] // tk))
            def _(ref=ref):
                dh_ref[...] += jnp.dot(ref[...], w_ref[...], preferred_element_type=f32)

    def piece_spec(piece, first):
        last = piece.shape[1] // tk - 1
        return pl.BlockSpec((tm, tk), lambda i, k: (i, jnp.clip(k - first, 0, last)))

    return pl.pallas_call(
        body, name="in_proj_bwd", grid=(s_len // tm, nk),
        in_specs=[piece_spec(p, f) for p, f in zip(dp_pieces, first_section)]
                 + [pl.BlockSpec((tm, N_SMALL), lambda i, k: (i, 0)),
                    pl.BlockSpec((tk, D_MODEL), lambda i, k: (k, 0)), pl.BlockSpec((N_SMALL, D_MODEL), lambda i, k: (0, 0))],
        out_specs=pl.BlockSpec((tm, D_MODEL), lambda i, k: (i, 0)),
        out_shape=jax.ShapeDtypeStruct((s_len, D_MODEL), f32),
        compiler_params=_params("parallel", "arbitrary"),
    )(*dp_pieces, dp_small, wt_main, wt_small)


def _adaln_bwd(dh, x, dy1, norm_g, scale1p):
    s_len = x.shape[0]
    tm = 512

    def body(dh_ref, x_ref, dy_ref, g_ref, sc_ref, dx_ref, dsh_ref, dsc_ref, dg_ref):
        @pl.when(pl.program_id(0) == 0)
        def _():
            dsh_ref[...] = jnp.zeros_like(dsh_ref)
            dsc_ref[...] = jnp.zeros_like(dsc_ref)
            dg_ref[...] = jnp.zeros_like(dg_ref)

        dh = dh_ref[...]
        xb = x_ref[...]
        r = lax.rsqrt(jnp.mean(xb * xb, axis=-1, keepdims=True) + EPS)
        xr = xb * r
        gain = g_ref[...]
        dsh_ref[...] += jnp.sum(dh, axis=0, keepdims=True)
        dsc_ref[...] += jnp.sum(dh * (xr * gain), axis=0, keepdims=True)
        dxn = dh * sc_ref[...]
        dg_ref[...] += jnp.sum(dxn * xr, axis=0, keepdims=True)
        tt = dxn * gain
        dx_ref[...] = r * (tt - xr * jnp.mean(tt * xr, axis=-1, keepdims=True)) + dy_ref[...]

    row = pl.BlockSpec((tm, D_MODEL), lambda i: (i, 0))
    vec = pl.BlockSpec((1, D_MODEL), lambda i: (0, 0))
    vshape = jax.ShapeDtypeStruct((1, D_MODEL), f32)
    return pl.pallas_call(
        body, name="adaln_bwd", grid=(s_len // tm,),
        in_specs=[row, row, row, vec, vec], out_specs=[row, vec, vec, vec],
        out_shape=[jax.ShapeDtypeStruct((s_len, D_MODEL), f32), vshape, vshape, vshape],
        compiler_params=_params("arbitrary"),
    )(dh, x, dy1, norm_g, scale1p)


def _adamw(name, w, g_stack, m, v, tr, tc=None):
    n_stack, rows, cols = g_stack.shape
    tc = cols if tc is None else tc

    def body(w_ref, g_ref, m_ref, v_ref, go_ref, d_ref, mo_ref, vo_ref):
        g = g_ref[0].astype(f32)
        for k in range(1, n_stack):
            g = g + g_ref[k].astype(f32)
        go_ref[0] = g
        m_new = ADAM_B1 * m_ref[0] + (1.0 - ADAM_B1) * g
        v_new = ADAM_B2 * v_ref[0] + (1.0 - ADAM_B2) * (g * g)
        mo_ref[0] = m_new
        vo_ref[0] = v_new
        m_hat = m_new / (1.0 - ADAM_B1 ** ADAM_STEP)
        v_hat = v_new / (1.0 - ADAM_B2 ** ADAM_STEP)
        d_ref[0] = -ADAM_LR * (m_hat / (jnp.sqrt(v_hat) + ADAM_EPS) + ADAM_WD * w_ref[0])

    blk = pl.BlockSpec((1, tr, tc), lambda i, j: (0, i, j))
    shape = jax.ShapeDtypeStruct((1, rows, cols), f32)
    return pl.pallas_call(
        body, name=name, grid=(rows // tr, cols // tc),
        in_specs=[blk, pl.BlockSpec((n_stack, tr, tc), lambda i, j: (0, i, j)), blk, blk],
        out_specs=[blk] * 4, out_shape=[shape] * 4,
        compiler_params=_params("parallel", "parallel"),
    )(w, g_stack, m, v)


def _w_ada_grad(c_all_t, dmod_pad):
    def body(c_ref, d_ref, o_ref):
        cv = c_ref[...]
        o_ref[...] = jnp.dot(cv * _sigmoid(cv), d_ref[...], preferred_element_type=f32, precision=HI)

    return pl.pallas_call(body, name="w_ada_grad",
                          out_shape=jax.ShapeDtypeStruct((c_all_t.shape[0], dmod_pad.shape[1]), f32),
                          compiler_params=_params())(c_all_t, dmod_pad)


SMALL_NAMES = ("norm_g", "b_ada", "b_fgate", "fox_qn_g", "fox_kn_g", "gdn_A_log", "gdn_dt_bias", "gdn_norm_g", "final_g")
SMALL_SIZES = (D_MODEL, 3 * D_MODEL, HEADS, HEAD_DIM, HEAD_DIM, HEADS, HEADS, HEAD_DIM, D_MODEL)
SMALL_PACK = 10752


def _pack(vectors, total):
    flat = jnp.concatenate([t.reshape(-1) for t in vectors])
    return jnp.pad(flat, (0, total - flat.shape[0])).reshape(1, total)


def _lanes(*pieces):
    parts, at = [], 0
    for off, vec in pieces:
        flat = vec.reshape(-1).astype(f32)
        parts += [jnp.zeros((off - at,), f32), flat]
        at = off + flat.shape[0]
    parts.append(jnp.zeros((LANES - at,), f32))
    return jnp.concatenate(parts).reshape(1, LANES)


def kernel(x, c, norm_g, w_ada, b_ada, w_in, b_fgate, fox_qn_g, fox_kn_g, gdn_conv_w, gdn_A_log, gdn_dt_bias, gdn_norm_g, w_out, final_g, loss_target, m_norm_g, m_w_ada, m_b_ada, m_w_in, m_b_fgate, m_fox_qn_g, m_fox_kn_g, m_gdn_conv_w, m_gdn_A_log, m_gdn_dt_bias, m_gdn_norm_g, m_w_out, m_final_g, v_norm_g, v_w_ada, v_b_ada, v_w_in, v_b_fgate, v_fox_qn_g, v_fox_kn_g, v_gdn_conv_w, v_gdn_A_log, v_gdn_dt_bias, v_gdn_norm_g, v_w_out, v_final_g):
    me = _my_index()
    s_len = x.shape[1]
    nq = s_len // FOX_T
    x2 = x.reshape(s_len, D_MODEL)
    tgt = loss_target.reshape(s_len, D_MODEL)
    ada_cols = w_ada.shape[2]
    in_cols = w_in.shape[2]
    conv_cols = gdn_conv_w.shape[2]

    (c_all,) = _gather_direct("gather_c", [c])
    c_all = c_all.reshape(N_DEV, D_MODEL)
    b_shard = lax.dynamic_slice(b_ada, (0, me * ada_cols), (1, ada_cols))
    mod_mine = _mod_shard(c_all, w_ada[0], b_shard)
    wt_shard = jnp.transpose(w_in[0])
    mod_all, wt_all, conv_all = _gather_two_level(
        "gather_weights", [mod_mine, wt_shard.astype(bf16), gdn_conv_w[0]])
    w_out_mine = w_out[0].astype(bf16)
    w_out_started = _gather_behind_start("gather_w_out_start", w_out_mine)
    mod = lax.dynamic_slice(mod_all, (0, me, 0), (N_DEV, 1, ada_cols)).reshape(1, 3 * D_MODEL)
    shift, scale, gate = mod[:, :D_MODEL], mod[:, D_MODEL:2 * D_MODEL], mod[:, 2 * D_MODEL:]
    scale1p = 1.0 + scale + w_out_started[-1][0, 0]
    wt_full = wt_all.reshape(N_DEV * in_cols, D_MODEL)
    g0 = 4 * WIDTH + HEADS
    w_main = jnp.concatenate([wt_full[:4 * WIDTH], wt_full[g0:g0 + 4 * WIDTH]], axis=0)
    w_small = jnp.concatenate([wt_full[4 * WIDTH:g0], wt_full[g0 + 4 * WIDTH:],
                               jnp.zeros((N_SMALL - 3 * HEADS, D_MODEL), bf16)], axis=0)
    conv_full = jnp.transpose(conv_all, (1, 0, 2)).reshape(CONV_K, 3 * WIDTH)

    qn_g, kn_g, gn_g = fox_qn_g.reshape(1, LANES), fox_kn_g.reshape(1, LANES), gdn_norm_g.reshape(1, LANES)
    bvec = _lanes((0, b_fgate), (HEADS, gdn_dt_bias))
    alog = _lanes((HEADS, gdn_A_log))
    fg = final_g.reshape(1, D_MODEL)

    h_bf = _norm_mod(x2, norm_g, scale1p, shift)
    p_main, p_small = _in_proj(h_bf, w_main, w_small)
    qs, kn, vb, gq, gk, gv, small = _prep(p_main, p_small, qn_g, kn_g, conv_full, bvec, alog)
    fox_o, lse = _fox_fwd(qs, kn, vb)
    gu, gw, gqg, gkd, gattn, t_inv, eg_last = _gdn_intra_fwd(gq, gk, gv, small)
    gdn_o, states = _gdn_scan_fwd(gu, gw, gqg, gkd, gattn, eg_last)
    mixed = _mix_fwd(fox_o, gdn_o, p_main, gn_g)
    w_out_all = _gather_behind_wait("gather_w_out_wait", w_out_started, mixed)
    w_out_full = lax.dynamic_update_slice(w_out_all, w_out_mine[None], (me, 0, 0)).reshape(2 * WIDTH, D_MODEL)

    loss_row, dy1, dz, dmixed, d_final_g, d_gate = _out_head(mixed, w_out_full, x2, tgt, gate, fg)
    loss = lax.psum(loss_row[0, 0], AXES)
    dw_out = _matmul_tn("dw_out", mixed, dz, bf16)
    dw_out_parts = dw_out.reshape(N_DEV, w_out.shape[1], D_MODEL)
    dw_out_started = _gather_behind_start("scatter_dw_out_start", dw_out_parts, scatter=True)
    gn_g_after = gn_g + dw_out_started[-1][0:1, :]
    do_fox, delta, dfz, dgz, dgdn_o, d_gn_g = _mix_bwd(dmixed, fox_o, gdn_o, p_main, gn_g_after)
    dqs, dkn, dvf, df_key, df_query = _fox_bwd(qs, kn, vb, do_fox, lse, delta)
    du, dw, dqg, dkd, dattn, deg = _gdn_scan_bwd(gu, gw, gqg, gkd, gattn, eg_last, states, dgdn_o)
    dgq, dgk, dgv, d_small = _gdn_intra_bwd(gq, gk, gv, small, t_inv, du, dw, dqg, dkd, dattn, deg)
    df_small = jnp.pad(jnp.transpose(df_key.reshape(HEADS, s_len)), ((0, 0), (0, N_SMALL - HEADS)))
    dfq, dfk, dgqkv, dp_small, d_qn_g, d_kn_g, sums, d_conv = _post1(
        p_main, p_small, qn_g, kn_g, conv_full, bvec, alog, dqs, dkn, dgq, dgk, dgv, d_small, df_small, df_query)
    dp_pieces = [dfq, dfk, dvf, dfz, dgqkv, dgz]
    dh = _in_proj_bwd(dp_pieces, dp_small, w_main, w_small)
    grad_x, d_shift, d_scale, d_norm_g = _adaln_bwd(dh, x2, dy1, norm_g, scale1p)
    dw_rows = [_matmul_tn("dw_main_%d" % n, piece, h_bf, bf16) for n, piece in enumerate(dp_pieces)]
    dw_small = _matmul_tn("dw_small", dp_small, h_bf, bf16)
    dw_in_full = jnp.concatenate(dw_rows[:4] + [dw_small[:HEADS]] + dw_rows[4:] + [dw_small[HEADS:3 * HEADS]],
                                 axis=0)
    dw_in_parts = dw_in_full.reshape(N_DEV, in_cols, D_MODEL)

    dmod = jnp.concatenate([d_shift, d_scale, d_gate], axis=1)
    small_grads = _pack([d_norm_g, dmod, sums[0, :HEADS], d_qn_g, d_kn_g, sums[1, HEADS:2 * HEADS],
                         sums[0, HEADS:2 * HEADS], d_gn_g, d_final_g], SMALL_PACK)
    conv_grad = d_conv[:CONV_K]
    (pair_in,) = _pair_exchange("pair_grads", [dw_in_parts])
    core = lax.axis_index("c").astype(jnp.int32).reshape(1)
    (dw_in_recv,) = _chip_exchange("chip_grads", [_pair_sum("pair_sum_w_in", dw_in_parts, pair_in, core)])
    dw_out_landed = _gather_behind_wait("scatter_dw_out_wait", dw_out_started, dw_in_recv, scatter=True)
    dw_out_recv = lax.dynamic_update_slice(
        dw_out_landed, lax.dynamic_slice(dw_out_parts, (me, 0, 0), (1,) + dw_out_parts.shape[1:]), (me, 0, 0))
    small_all, conv_all_g = _gather_direct("gather_small_grads", [small_grads, conv_grad])

    outs = {}
    to_t = lambda t: jnp.transpose(t, (0, 2, 1))
    outs["w_in"] = tuple(to_t(t) for t in _adamw("adamw_w_in", to_t(w_in), dw_in_recv, to_t(m_w_in), to_t(v_w_in),
                                                  in_cols, 256))
    outs["w_out"] = _adamw("adamw_w_out", w_out, dw_out_recv, m_w_out, v_w_out, 128)
    conv_mine = lax.dynamic_slice(jnp.transpose(conv_all_g.reshape(N_DEV, CONV_K, N_DEV, conv_cols), (0, 2, 1, 3)),
                                  (0, me, 0, 0), (N_DEV, 1, CONV_K, conv_cols)).reshape(N_DEV, CONV_K, conv_cols)
    outs["gdn_conv_w"] = _adamw("adamw_conv", gdn_conv_w, conv_mine, m_gdn_conv_w, v_gdn_conv_w, CONV_K)
    small_all = small_all.reshape(N_DEV, 1, SMALL_PACK)
    dmod_all = small_all[:, 0, D_MODEL:D_MODEL + 3 * D_MODEL]
    dmod_mine = lax.dynamic_slice(dmod_all, (0, me * ada_cols), (N_DEV, ada_cols))
    c_all_t = jnp.pad(jnp.transpose(c_all), ((0, 0), (0, LANES - N_DEV)))
    g_w_ada = _w_ada_grad(c_all_t, jnp.pad(dmod_mine, ((0, LANES - N_DEV), (0, 0))))
    outs["w_ada"] = _adamw("adamw_w_ada", w_ada, g_w_ada[None], m_w_ada, v_w_ada, 256)
    given = dict(norm_g=(norm_g, m_norm_g, v_norm_g), b_ada=(b_ada, m_b_ada, v_b_ada), b_fgate=(b_fgate, m_b_fgate, v_b_fgate),
                 fox_qn_g=(fox_qn_g, m_fox_qn_g, v_fox_qn_g), fox_kn_g=(fox_kn_g, m_fox_kn_g, v_fox_kn_g),
                 gdn_A_log=(gdn_A_log, m_gdn_A_log, v_gdn_A_log), gdn_dt_bias=(gdn_dt_bias, m_gdn_dt_bias, v_gdn_dt_bias),
                 gdn_norm_g=(gdn_norm_g, m_gdn_norm_g, v_gdn_norm_g), final_g=(final_g, m_final_g, v_final_g))
    w_pack = _pack([given[n][0] for n in SMALL_NAMES], SMALL_PACK)
    m_pack = _pack([given[n][1] for n in SMALL_NAMES], SMALL_PACK)
    v_pack = _pack([given[n][2] for n in SMALL_NAMES], SMALL_PACK)
    packed = _adamw("adamw_small", w_pack[None], small_all, m_pack[None], v_pack[None], 1)
    off = 0
    for n, size in zip(SMALL_NAMES, SMALL_SIZES):
        outs[n] = tuple(t[0, 0, off:off + size].reshape(given[n][0].shape) for t in packed)
        off += size

    order = ("norm_g", "w_ada", "b_ada", "w_in", "b_fgate", "fox_qn_g", "fox_kn_g", "gdn_conv_w", "gdn_A_log",
             "gdn_dt_bias", "gdn_norm_g", "w_out", "final_g")
    result = [loss, grad_x.reshape(x.shape)]
    for part in range(4):
        result += [outs[n][part] for n in order]
    return tuple(result)
```
